```python
import jax, jax.numpy as jnp
from jax import lax
import numpy as np

D_MODEL = 1024
BATCH = 8
SEQ = 2048
DEPTH = 2

N_SB_HEADS = 8
SB_HEAD_DIM = 64
SB_WIDTH = N_SB_HEADS * SB_HEAD_DIM
Q_BLOCK = 128
POOL_WINDOWS = (2, 4, 8, 16)
POOL_GROUPS = len(POOL_WINDOWS)
POOL_GROUP_DIM = 128
POOL_WIDTH = POOL_GROUPS * POOL_GROUP_DIM
CONV_WIDTH = 512
CONV_TAPS = 31
SG_GROUPS = 4
SG_GROUP_DIM = 128
SG_WIDTH = SG_GROUPS * SG_GROUP_DIM
SG_CHUNK = 128
EVEN_IN = 3 * SB_WIDTH + POOL_WIDTH
ODD_IN = 2 * CONV_WIDTH + 2 * SG_WIDTH
MIX_WIDTH = 1024
D_FF = -(-(8 * D_MODEL) // (3 * 256)) * 256
PLE_DIM = 256
N_EVEN = (DEPTH + 1) // 2
N_ODD = DEPTH // 2
ALPHA = (2 * DEPTH) ** 0.25
BETA_INIT = (8 * DEPTH) ** -0.25
LN_EPS = 1e-5

kernel_name = "hybrid_stickbreak_pool_conformer_gmlp"


def layer_norm(x, g, b):
    xf = x.astype(jnp.float32)
    mu = jnp.mean(xf, axis=-1, keepdims=True)
    var = jnp.mean(jnp.square(xf - mu), axis=-1, keepdims=True)
    y = (xf - mu) * lax.rsqrt(var + LN_EPS)
    return (y * g.astype(jnp.float32) + b.astype(jnp.float32)).astype(x.dtype)


def stick_breaking_attention(q, k, v):
    S = q.shape[1]
    scale = SB_HEAD_DIM ** -0.5
    outs = []
    for i in range(S // Q_BLOCK):
        q0, q1 = i * Q_BLOCK, (i + 1) * Q_BLOCK
        qb = q[:, q0:q1]
        kb = k[:, :q1]
        vb = v[:, :q1]
        z = jnp.einsum('bqhd,bkhd->bhqk', qb, kb).astype(jnp.float32) * scale
        t_idx = q0 + jnp.arange(Q_BLOCK)[:, None]
        s_idx = jnp.arange(q1)[None, :]
        mask = s_idx < t_idx
        log_keep = jnp.where(mask, jax.nn.log_sigmoid(-z), 0.0)
        between = lax.cumsum(log_keep, axis=3, reverse=True) - log_keep
        w = jnp.where(mask, jnp.exp(jax.nn.log_sigmoid(z) + between), 0.0)
        outs.append(jnp.einsum('bhqk,bkhd->bqhd', w.astype(vb.dtype), vb))
    return jnp.concatenate(outs, axis=1)


def causal_window_mean(u, w):
    B, S, C = u.shape
    c = jnp.cumsum(u.astype(jnp.float32), axis=1)
    c_pad = jnp.concatenate([jnp.zeros((B, 1, C), jnp.float32), c], axis=1)
    hi = c_pad[:, 1:]
    lo = jnp.pad(c_pad[:, :S + 1 - w], ((0, 0), (w - 1, 0), (0, 0)))
    count = jnp.minimum(jnp.arange(1, S + 1), w).astype(jnp.float32)[None, :, None]
    return ((hi - lo) / count).astype(u.dtype)


def multiscale_pool(u, pool_w, pool_scale):
    B, S, _ = u.shape
    ug = u.reshape(B, S, POOL_GROUPS, POOL_GROUP_DIM)
    pooled = jnp.stack([causal_window_mean(ug[:, :, g], w) - ug[:, :, g]
                        for g, w in enumerate(POOL_WINDOWS)], axis=2)
    mixed = jnp.einsum('bsgc,gcd->bsgd', pooled, pool_w)
    return mixed.reshape(B, S, POOL_WIDTH) * pool_scale


def even_mixer(x, w_in, w_out, pool_w, pool_scale):
    B, S, _ = x.shape
    h = x @ w_in
    q = h[..., :SB_WIDTH].reshape(B, S, N_SB_HEADS, SB_HEAD_DIM)
    k = h[..., SB_WIDTH:2 * SB_WIDTH].reshape(B, S, N_SB_HEADS, SB_HEAD_DIM)
    v = h[..., 2 * SB_WIDTH:3 * SB_WIDTH].reshape(B, S, N_SB_HEADS, SB_HEAD_DIM)
    u = h[..., 3 * SB_WIDTH:]
    a = stick_breaking_attention(q, k, v).reshape(B, S, SB_WIDTH)
    b = multiscale_pool(u, pool_w, pool_scale)
    return jnp.concatenate([a, b], axis=-1) @ w_out


def conformer_conv(a, g, dw, ln_g, ln_b):
    h = a * jax.nn.sigmoid(g)
    h = lax.conv_general_dilated(h, dw[:, None, :], window_strides=(1,),
                                 padding=[(CONV_TAPS - 1, 0)],
                                 dimension_numbers=('NWC', 'WIO', 'NWC'),
                                 feature_group_count=CONV_WIDTH)
    return jax.nn.silu(layer_norm(h, ln_g, ln_b))


def chunked_spatial_gating(zc, ln_g, ln_b, sg_w, sg_b):
    B, S, _ = zc.shape
    z = jax.nn.gelu(zc)
    u, v = z[..., :SG_WIDTH], z[..., SG_WIDTH:]
    v = layer_norm(v, ln_g, ln_b)
    vc = v.reshape(B, S // SG_CHUNK, SG_CHUNK, SG_GROUPS, SG_GROUP_DIM)
    mask = jnp.tril(jnp.ones((SG_CHUNK, SG_CHUNK), sg_w.dtype))
    sv = jnp.einsum('gts,bcsgd->bctgd', sg_w * mask[None], vc)
    sv = sv + sg_b.T[None, None, :, :, None]
    return u * sv.reshape(B, S, SG_WIDTH)


def odd_mixer(x, w_in, w_out, conv_dw, conv_ln_g, conv_ln_b, sg_ln_g, sg_ln_b, sg_w, sg_b):
    h = x @ w_in
    a = h[..., :CONV_WIDTH]
    g = h[..., CONV_WIDTH:2 * CONV_WIDTH]
    zc = h[..., 2 * CONV_WIDTH:]
    c_out = conformer_conv(a, g, conv_dw, conv_ln_g, conv_ln_b)
    d_out = chunked_spatial_gating(zc, sg_ln_g, sg_ln_b, sg_w, sg_b)
    return jnp.concatenate([c_out, d_out], axis=-1) @ w_out


def swiglu(x, w_gate, w_up, w_down):
    return (jax.nn.silu(x @ w_gate) * (x @ w_up)) @ w_down


def _fwd_setup_inputs(seed: int = 0) -> dict:
    key = jax.random.key(seed)
    ks = iter(jax.random.split(key, 40))
    nrm = lambda shape, s: jax.random.normal(next(ks), shape, jnp.float32) * s
    d = D_MODEL
    return {
        "x": nrm((BATCH, SEQ, d), 1.0),
        "p": nrm((DEPTH, BATCH, SEQ, PLE_DIM), 1.0),
        "even_w_in": nrm((N_EVEN, d, EVEN_IN), d ** -0.5),
        "even_w_out": nrm((N_EVEN, MIX_WIDTH, d), MIX_WIDTH ** -0.5 * BETA_INIT),
        "pool_w": nrm((N_EVEN, POOL_GROUPS, POOL_GROUP_DIM, POOL_GROUP_DIM), POOL_GROUP_DIM ** -0.5),
        "pool_scale": 1.0 + nrm((N_EVEN, POOL_WIDTH), 0.02),
        "odd_w_in": nrm((N_ODD, d, ODD_IN), d ** -0.5),
        "odd_w_out": nrm((N_ODD, MIX_WIDTH, d), MIX_WIDTH ** -0.5 * BETA_INIT),
        "conv_dw": nrm((N_ODD, CONV_TAPS, CONV_WIDTH), CONV_TAPS ** -0.5),
        "conv_ln_g": 1.0 + nrm((N_ODD, CONV_WIDTH), 0.02),
        "conv_ln_b": nrm((N_ODD, CONV_WIDTH), 0.02),
        "sg_ln_g": 1.0 + nrm((N_ODD, SG_WIDTH), 0.02),
        "sg_ln_b": nrm((N_ODD, SG_WIDTH), 0.02),
        "sg_w": nrm((N_ODD, SG_GROUPS, SG_CHUNK, SG_CHUNK), SG_CHUNK ** -0.5),
        "sg_b": 1.0 + nrm((N_ODD, SG_GROUPS, SG_CHUNK), 0.1),
        "ln_mix_g": 1.0 + nrm((DEPTH, d), 0.02),
        "ln_mix_b": nrm((DEPTH, d), 0.02),
        "ffn_w_gate": nrm((DEPTH, d, D_FF), d ** -0.5),
        "ffn_w_up": nrm((DEPTH, d, D_FF), d ** -0.5),
        "ffn_w_down": nrm((DEPTH, D_FF, d), D_FF ** -0.5 * BETA_INIT),
        "ln_ffn_g": 1.0 + nrm((DEPTH, d), 0.02),
        "ln_ffn_b": nrm((DEPTH, d), 0.02),
        "ple_w_proj": nrm((DEPTH, PLE_DIM, d), PLE_DIM ** -0.5),
        "ple_w_gate": nrm((DEPTH, d, d), d ** -0.5),
        "ple_b_gate": nrm((DEPTH, d), 0.02),
    }


def _fwd_reference(x, p, even_w_in, even_w_out, pool_w, pool_scale,
              odd_w_in, odd_w_out, conv_dw, conv_ln_g, conv_ln_b,
              sg_ln_g, sg_ln_b, sg_w, sg_b,
              ln_mix_g, ln_mix_b, ffn_w_gate, ffn_w_up, ffn_w_down,
              ln_ffn_g, ln_ffn_b, ple_w_proj, ple_w_gate, ple_b_gate):
    for i in range(DEPTH):
        j = i // 2
        if i % 2 == 0:
            mix = even_mixer(x, even_w_in[j], even_w_out[j], pool_w[j], pool_scale[j])
        else:
            mix = odd_mixer(x, odd_w_in[j], odd_w_out[j], conv_dw[j], conv_ln_g[j], conv_ln_b[j],
                            sg_ln_g[j], sg_ln_b[j], sg_w[j], sg_b[j])
        x = layer_norm(ALPHA * x + mix, ln_mix_g[i], ln_mix_b[i])
        x = layer_norm(ALPHA * x + swiglu(x, ffn_w_gate[i], ffn_w_up[i], ffn_w_down[i]),
                       ln_ffn_g[i], ln_ffn_b[i])
        gate = jax.nn.sigmoid(x @ ple_w_gate[i] + ple_b_gate[i])
        x = x + gate * (p[i] @ ple_w_proj[i])
    return x


import jax as _jax
import jax.numpy as _jnp

TWIN_FORMAT = 'train_step'
FWD_PARAMS = ['x', 'p', 'even_w_in', 'even_w_out', 'pool_w', 'pool_scale', 'odd_w_in', 'odd_w_out', 'conv_dw', 'conv_ln_g', 'conv_ln_b', 'sg_ln_g', 'sg_ln_b', 'sg_w', 'sg_b', 'ln_mix_g', 'ln_mix_b', 'ffn_w_gate', 'ffn_w_up', 'ffn_w_down', 'ln_ffn_g', 'ln_ffn_b', 'ple_w_proj', 'ple_w_gate', 'ple_b_gate']
TWIN_WEIGHTS = ['even_w_in', 'even_w_out', 'pool_w', 'pool_scale', 'odd_w_in', 'odd_w_out', 'conv_dw', 'conv_ln_g', 'conv_ln_b', 'sg_ln_g', 'sg_ln_b', 'sg_w', 'sg_b', 'ln_mix_g', 'ln_mix_b', 'ffn_w_gate', 'ffn_w_up', 'ffn_w_down', 'ln_ffn_g', 'ln_ffn_b', 'ple_w_proj', 'ple_w_gate', 'ple_b_gate']
TWIN_DIFF_INPUT = 'x'
TWIN_INPUTS = ['x', 'p', 'even_w_in', 'even_w_out', 'pool_w', 'pool_scale', 'odd_w_in', 'odd_w_out', 'conv_dw', 'conv_ln_g', 'conv_ln_b', 'sg_ln_g', 'sg_ln_b', 'sg_w', 'sg_b', 'ln_mix_g', 'ln_mix_b', 'ffn_w_gate', 'ffn_w_up', 'ffn_w_down', 'ln_ffn_g', 'ln_ffn_b', 'ple_w_proj', 'ple_w_gate', 'ple_b_gate', 'loss_target', 'm_even_w_in', 'm_even_w_out', 'm_pool_w', 'm_pool_scale', 'm_odd_w_in', 'm_odd_w_out', 'm_conv_dw', 'm_conv_ln_g', 'm_conv_ln_b', 'm_sg_ln_g', 'm_sg_ln_b', 'm_sg_w', 'm_sg_b', 'm_ln_mix_g', 'm_ln_mix_b', 'm_ffn_w_gate', 'm_ffn_w_up', 'm_ffn_w_down', 'm_ln_ffn_g', 'm_ln_ffn_b', 'm_ple_w_proj', 'm_ple_w_gate', 'm_ple_b_gate', 'v_even_w_in', 'v_even_w_out', 'v_pool_w', 'v_pool_scale', 'v_odd_w_in', 'v_odd_w_out', 'v_conv_dw', 'v_conv_ln_g', 'v_conv_ln_b', 'v_sg_ln_g', 'v_sg_ln_b', 'v_sg_w', 'v_sg_b', 'v_ln_mix_g', 'v_ln_mix_b', 'v_ffn_w_gate', 'v_ffn_w_up', 'v_ffn_w_down', 'v_ln_ffn_g', 'v_ln_ffn_b', 'v_ple_w_proj', 'v_ple_w_gate', 'v_ple_b_gate']
TWIN_OUTPUTS = ['loss', 'grad_x', 'grad_even_w_in', 'grad_even_w_out', 'grad_pool_w', 'grad_pool_scale', 'grad_odd_w_in', 'grad_odd_w_out', 'grad_conv_dw', 'grad_conv_ln_g', 'grad_conv_ln_b', 'grad_sg_ln_g', 'grad_sg_ln_b', 'grad_sg_w', 'grad_sg_b', 'grad_ln_mix_g', 'grad_ln_mix_b', 'grad_ffn_w_gate', 'grad_ffn_w_up', 'grad_ffn_w_down', 'grad_ln_ffn_g', 'grad_ln_ffn_b', 'grad_ple_w_proj', 'grad_ple_w_gate', 'grad_ple_b_gate', 'delta_even_w_in', 'delta_even_w_out', 'delta_pool_w', 'delta_pool_scale', 'delta_odd_w_in', 'delta_odd_w_out', 'delta_conv_dw', 'delta_conv_ln_g', 'delta_conv_ln_b', 'delta_sg_ln_g', 'delta_sg_ln_b', 'delta_sg_w', 'delta_sg_b', 'delta_ln_mix_g', 'delta_ln_mix_b', 'delta_ffn_w_gate', 'delta_ffn_w_up', 'delta_ffn_w_down', 'delta_ln_ffn_g', 'delta_ln_ffn_b', 'delta_ple_w_proj', 'delta_ple_w_gate', 'delta_ple_b_gate', 'new_m_even_w_in', 'new_m_even_w_out', 'new_m_pool_w', 'new_m_pool_scale', 'new_m_odd_w_in', 'new_m_odd_w_out', 'new_m_conv_dw', 'new_m_conv_ln_g', 'new_m_conv_ln_b', 'new_m_sg_ln_g', 'new_m_sg_ln_b', 'new_m_sg_w', 'new_m_sg_b', 'new_m_ln_mix_g', 'new_m_ln_mix_b', 'new_m_ffn_w_gate', 'new_m_ffn_w_up', 'new_m_ffn_w_down', 'new_m_ln_ffn_g', 'new_m_ln_ffn_b', 'new_m_ple_w_proj', 'new_m_ple_w_gate', 'new_m_ple_b_gate', 'new_v_even_w_in', 'new_v_even_w_out', 'new_v_pool_w', 'new_v_pool_scale', 'new_v_odd_w_in', 'new_v_odd_w_out', 'new_v_conv_dw', 'new_v_conv_ln_g', 'new_v_conv_ln_b', 'new_v_sg_ln_g', 'new_v_sg_ln_b', 'new_v_sg_w', 'new_v_sg_b', 'new_v_ln_mix_g', 'new_v_ln_mix_b', 'new_v_ffn_w_gate', 'new_v_ffn_w_up', 'new_v_ffn_w_down', 'new_v_ln_ffn_g', 'new_v_ln_ffn_b', 'new_v_ple_w_proj', 'new_v_ple_w_gate', 'new_v_ple_b_gate']
TWIN_LEAF_KINDS = {'loss': 'loss', 'grad_x': 'grad_x', 'grad_even_w_in': 'grad_w', 'grad_even_w_out': 'grad_w', 'grad_pool_w': 'grad_w', 'grad_pool_scale': 'grad_w', 'grad_odd_w_in': 'grad_w', 'grad_odd_w_out': 'grad_w', 'grad_conv_dw': 'grad_w', 'grad_conv_ln_g': 'grad_w', 'grad_conv_ln_b': 'grad_w', 'grad_sg_ln_g': 'grad_w', 'grad_sg_ln_b': 'grad_w', 'grad_sg_w': 'grad_w', 'grad_sg_b': 'grad_w', 'grad_ln_mix_g': 'grad_w', 'grad_ln_mix_b': 'grad_w', 'grad_ffn_w_gate': 'grad_w', 'grad_ffn_w_up': 'grad_w', 'grad_ffn_w_down': 'grad_w', 'grad_ln_ffn_g': 'grad_w', 'grad_ln_ffn_b': 'grad_w', 'grad_ple_w_proj': 'grad_w', 'grad_ple_w_gate': 'grad_w', 'grad_ple_b_gate': 'grad_w', 'delta_even_w_in': 'delta_w', 'delta_even_w_out': 'delta_w', 'delta_pool_w': 'delta_w', 'delta_pool_scale': 'delta_w', 'delta_odd_w_in': 'delta_w', 'delta_odd_w_out': 'delta_w', 'delta_conv_dw': 'delta_w', 'delta_conv_ln_g': 'delta_w', 'delta_conv_ln_b': 'delta_w', 'delta_sg_ln_g': 'delta_w', 'delta_sg_ln_b': 'delta_w', 'delta_sg_w': 'delta_w', 'delta_sg_b': 'delta_w', 'delta_ln_mix_g': 'delta_w', 'delta_ln_mix_b': 'delta_w', 'delta_ffn_w_gate': 'delta_w', 'delta_ffn_w_up': 'delta_w', 'delta_ffn_w_down': 'delta_w', 'delta_ln_ffn_g': 'delta_w', 'delta_ln_ffn_b': 'delta_w', 'delta_ple_w_proj': 'delta_w', 'delta_ple_w_gate': 'delta_w', 'delta_ple_b_gate': 'delta_w', 'new_m_even_w_in': 'new_m', 'new_m_even_w_out': 'new_m', 'new_m_pool_w': 'new_m', 'new_m_pool_scale': 'new_m', 'new_m_odd_w_in': 'new_m', 'new_m_odd_w_out': 'new_m', 'new_m_conv_dw': 'new_m', 'new_m_conv_ln_g': 'new_m', 'new_m_conv_ln_b': 'new_m', 'new_m_sg_ln_g': 'new_m', 'new_m_sg_ln_b': 'new_m', 'new_m_sg_w': 'new_m', 'new_m_sg_b': 'new_m', 'new_m_ln_mix_g': 'new_m', 'new_m_ln_mix_b': 'new_m', 'new_m_ffn_w_gate': 'new_m', 'new_m_ffn_w_up': 'new_m', 'new_m_ffn_w_down': 'new_m', 'new_m_ln_ffn_g': 'new_m', 'new_m_ln_ffn_b': 'new_m', 'new_m_ple_w_proj': 'new_m', 'new_m_ple_w_gate': 'new_m', 'new_m_ple_b_gate': 'new_m', 'new_v_even_w_in': 'new_v', 'new_v_even_w_out': 'new_v', 'new_v_pool_w': 'new_v', 'new_v_pool_scale': 'new_v', 'new_v_odd_w_in': 'new_v', 'new_v_odd_w_out': 'new_v', 'new_v_conv_dw': 'new_v', 'new_v_conv_ln_g': 'new_v', 'new_v_conv_ln_b': 'new_v', 'new_v_sg_ln_g': 'new_v', 'new_v_sg_ln_b': 'new_v', 'new_v_sg_w': 'new_v', 'new_v_sg_b': 'new_v', 'new_v_ln_mix_g': 'new_v', 'new_v_ln_mix_b': 'new_v', 'new_v_ffn_w_gate': 'new_v', 'new_v_ffn_w_up': 'new_v', 'new_v_ffn_w_down': 'new_v', 'new_v_ln_ffn_g': 'new_v', 'new_v_ln_ffn_b': 'new_v', 'new_v_ple_w_proj': 'new_v', 'new_v_ple_w_gate': 'new_v', 'new_v_ple_b_gate': 'new_v'}


def _forward(args):
    return _fwd_reference(*[args[k] for k in FWD_PARAMS])


def _output_shape():
    out = _jax.eval_shape(lambda: _forward(_fwd_setup_inputs(0)))
    return out.shape, out.dtype

N_MICROBATCH = 1
ADAM_LR = 0.001
ADAM_B1 = 0.9
ADAM_B2 = 0.999
ADAM_EPS = 1e-08
ADAM_WD = 0.01
ADAM_STEP = 10
PER_EXAMPLE_BATCH_AXIS = {'x': 0, 'p': 1, 'loss_target': 0}
SHARED_INPUTS = []
_WEIGHT_DTYPES = {'even_w_in': _jnp.float32, 'even_w_out': _jnp.float32, 'pool_w': _jnp.float32, 'pool_scale': _jnp.float32, 'odd_w_in': _jnp.float32, 'odd_w_out': _jnp.float32, 'conv_dw': _jnp.float32, 'conv_ln_g': _jnp.float32, 'conv_ln_b': _jnp.float32, 'sg_ln_g': _jnp.float32, 'sg_ln_b': _jnp.float32, 'sg_w': _jnp.float32, 'sg_b': _jnp.float32, 'ln_mix_g': _jnp.float32, 'ln_mix_b': _jnp.float32, 'ffn_w_gate': _jnp.float32, 'ffn_w_up': _jnp.float32, 'ffn_w_down': _jnp.float32, 'ln_ffn_g': _jnp.float32, 'ln_ffn_b': _jnp.float32, 'ple_w_proj': _jnp.float32, 'ple_w_gate': _jnp.float32, 'ple_b_gate': _jnp.float32}
MOMENT_SCALE = {'even_w_in': 2.838173e-02, 'even_w_out': 7.490901e-02, 'pool_w': 4.248148e-02, 'pool_scale': 4.427806e-02, 'odd_w_in': 3.108248e-02, 'odd_w_out': 2.647841e-01, 'conv_dw': 3.727987e-02, 'conv_ln_g': 1.661854e-01, 'conv_ln_b': 2.420318e-01, 'sg_ln_g': 2.707821e-02, 'sg_ln_b': 2.469044e-02, 'sg_w': 2.456622e-02, 'sg_b': 3.349199e-02, 'ln_mix_g': 4.856685e-01, 'ln_mix_b': 1.557559e+00, 'ffn_w_gate': 1.891695e-02, 'ffn_w_up': 1.871295e-02, 'ffn_w_down': 6.203076e-02, 'ln_ffn_g': 1.168851e+01, 'ln_ffn_b': 1.940296e+00, 'ple_w_proj': 1.825296e-01, 'ple_w_gate': 1.007501e-01, 'ple_b_gate': 1.175856e+00}


def _to_microbatches(a, axis):
    t = _jnp.moveaxis(a, axis, 0)
    t = t.reshape((N_MICROBATCH, t.shape[0] // N_MICROBATCH) + t.shape[1:])
    return _jnp.moveaxis(t, 1, axis + 1)


def setup_inputs(seed: int = 0) -> dict:
    inp = _fwd_setup_inputs(seed)
    key = _jax.random.fold_in(_jax.random.key(seed), 7919)
    shape, _ = _output_shape()
    out = dict(inp)
    out["loss_target"] = _jax.random.normal(_jax.random.fold_in(key, 0), shape, _jnp.float32)
    for i, name in enumerate(TWIN_WEIGHTS):
        w = inp[name].astype(_jnp.float32)
        if MOMENT_SCALE is None:
            s = _jnp.sqrt(_jnp.mean(_jnp.square(w)) + 1e-30)
        else:
            s = MOMENT_SCALE[name]
        km, kv = _jax.random.split(_jax.random.fold_in(key, i + 1))
        out[name] = w
        out["m_" + name] = s * _jax.random.normal(km, w.shape, _jnp.float32)
        out["v_" + name] = (s * s) * _jax.random.uniform(kv, w.shape, _jnp.float32, 0.5, 1.5)
    if N_MICROBATCH > 1:
        for name, axis in PER_EXAMPLE_BATCH_AXIS.items():
            out[name] = _to_microbatches(out[name], axis)
    return {'x': out['x'], 'p': out['p'], 'even_w_in': out['even_w_in'], 'even_w_out': out['even_w_out'], 'pool_w': out['pool_w'], 'pool_scale': out['pool_scale'], 'odd_w_in': out['odd_w_in'], 'odd_w_out': out['odd_w_out'], 'conv_dw': out['conv_dw'], 'conv_ln_g': out['conv_ln_g'], 'conv_ln_b': out['conv_ln_b'], 'sg_ln_g': out['sg_ln_g'], 'sg_ln_b': out['sg_ln_b'], 'sg_w': out['sg_w'], 'sg_b': out['sg_b'], 'ln_mix_g': out['ln_mix_g'], 'ln_mix_b': out['ln_mix_b'], 'ffn_w_gate': out['ffn_w_gate'], 'ffn_w_up': out['ffn_w_up'], 'ffn_w_down': out['ffn_w_down'], 'ln_ffn_g': out['ln_ffn_g'], 'ln_ffn_b': out['ln_ffn_b'], 'ple_w_proj': out['ple_w_proj'], 'ple_w_gate': out['ple_w_gate'], 'ple_b_gate': out['ple_b_gate'], 'loss_target': out['loss_target'], 'm_even_w_in': out['m_even_w_in'], 'm_even_w_out': out['m_even_w_out'], 'm_pool_w': out['m_pool_w'], 'm_pool_scale': out['m_pool_scale'], 'm_odd_w_in': out['m_odd_w_in'], 'm_odd_w_out': out['m_odd_w_out'], 'm_conv_dw': out['m_conv_dw'], 'm_conv_ln_g': out['m_conv_ln_g'], 'm_conv_ln_b': out['m_conv_ln_b'], 'm_sg_ln_g': out['m_sg_ln_g'], 'm_sg_ln_b': out['m_sg_ln_b'], 'm_sg_w': out['m_sg_w'], 'm_sg_b': out['m_sg_b'], 'm_ln_mix_g': out['m_ln_mix_g'], 'm_ln_mix_b': out['m_ln_mix_b'], 'm_ffn_w_gate': out['m_ffn_w_gate'], 'm_ffn_w_up': out['m_ffn_w_up'], 'm_ffn_w_down': out['m_ffn_w_down'], 'm_ln_ffn_g': out['m_ln_ffn_g'], 'm_ln_ffn_b': out['m_ln_ffn_b'], 'm_ple_w_proj': out['m_ple_w_proj'], 'm_ple_w_gate': out['m_ple_w_gate'], 'm_ple_b_gate': out['m_ple_b_gate'], 'v_even_w_in': out['v_even_w_in'], 'v_even_w_out': out['v_even_w_out'], 'v_pool_w': out['v_pool_w'], 'v_pool_scale': out['v_pool_scale'], 'v_odd_w_in': out['v_odd_w_in'], 'v_odd_w_out': out['v_odd_w_out'], 'v_conv_dw': out['v_conv_dw'], 'v_conv_ln_g': out['v_conv_ln_g'], 'v_conv_ln_b': out['v_conv_ln_b'], 'v_sg_ln_g': out['v_sg_ln_g'], 'v_sg_ln_b': out['v_sg_ln_b'], 'v_sg_w': out['v_sg_w'], 'v_sg_b': out['v_sg_b'], 'v_ln_mix_g': out['v_ln_mix_g'], 'v_ln_mix_b': out['v_ln_mix_b'], 'v_ffn_w_gate': out['v_ffn_w_gate'], 'v_ffn_w_up': out['v_ffn_w_up'], 'v_ffn_w_down': out['v_ffn_w_down'], 'v_ln_ffn_g': out['v_ln_ffn_g'], 'v_ln_ffn_b': out['v_ln_ffn_b'], 'v_ple_w_proj': out['v_ple_w_proj'], 'v_ple_w_gate': out['v_ple_w_gate'], 'v_ple_b_gate': out['v_ple_b_gate']}


def _loss(weights, diff, rest, loss_target):
    with _jax.named_scope("forward"):
        args = {**rest, TWIN_DIFF_INPUT: diff, **{k: w.astype(_WEIGHT_DTYPES[k]) for k, w in weights.items()}}
        y = _forward(args)
    with _jax.named_scope("loss_head"):
        err = _jnp.square(y.astype(_jnp.float32) - loss_target)
        return 0.5 * _jnp.sum(_jnp.mean(err, axis=-1)) if err.ndim else 0.5 * err


def _adamw(w, g, m, v):
    m = ADAM_B1 * m + (1.0 - ADAM_B1) * g
    v = ADAM_B2 * v + (1.0 - ADAM_B2) * _jnp.square(g)
    m_hat = m / (1.0 - ADAM_B1 ** ADAM_STEP)
    v_hat = v / (1.0 - ADAM_B2 ** ADAM_STEP)
    delta = -ADAM_LR * (m_hat / (_jnp.sqrt(v_hat) + ADAM_EPS) + ADAM_WD * w)
    return delta, m, v


def reference(x, p, even_w_in, even_w_out, pool_w, pool_scale, odd_w_in, odd_w_out, conv_dw, conv_ln_g, conv_ln_b, sg_ln_g, sg_ln_b, sg_w, sg_b, ln_mix_g, ln_mix_b, ffn_w_gate, ffn_w_up, ffn_w_down, ln_ffn_g, ln_ffn_b, ple_w_proj, ple_w_gate, ple_b_gate, loss_target, m_even_w_in, m_even_w_out, m_pool_w, m_pool_scale, m_odd_w_in, m_odd_w_out, m_conv_dw, m_conv_ln_g, m_conv_ln_b, m_sg_ln_g, m_sg_ln_b, m_sg_w, m_sg_b, m_ln_mix_g, m_ln_mix_b, m_ffn_w_gate, m_ffn_w_up, m_ffn_w_down, m_ln_ffn_g, m_ln_ffn_b, m_ple_w_proj, m_ple_w_gate, m_ple_b_gate, v_even_w_in, v_even_w_out, v_pool_w, v_pool_scale, v_odd_w_in, v_odd_w_out, v_conv_dw, v_conv_ln_g, v_conv_ln_b, v_sg_ln_g, v_sg_ln_b, v_sg_w, v_sg_b, v_ln_mix_g, v_ln_mix_b, v_ffn_w_gate, v_ffn_w_up, v_ffn_w_down, v_ln_ffn_g, v_ln_ffn_b, v_ple_w_proj, v_ple_w_gate, v_ple_b_gate):
    given = dict(x=x, p=p, even_w_in=even_w_in, even_w_out=even_w_out, pool_w=pool_w, pool_scale=pool_scale, odd_w_in=odd_w_in, odd_w_out=odd_w_out, conv_dw=conv_dw, conv_ln_g=conv_ln_g, conv_ln_b=conv_ln_b, sg_ln_g=sg_ln_g, sg_ln_b=sg_ln_b, sg_w=sg_w, sg_b=sg_b, ln_mix_g=ln_mix_g, ln_mix_b=ln_mix_b, ffn_w_gate=ffn_w_gate, ffn_w_up=ffn_w_up, ffn_w_down=ffn_w_down, ln_ffn_g=ln_ffn_g, ln_ffn_b=ln_ffn_b, ple_w_proj=ple_w_proj, ple_w_gate=ple_w_gate, ple_b_gate=ple_b_gate, loss_target=loss_target, m_even_w_in=m_even_w_in, m_even_w_out=m_even_w_out, m_pool_w=m_pool_w, m_pool_scale=m_pool_scale, m_odd_w_in=m_odd_w_in, m_odd_w_out=m_odd_w_out, m_conv_dw=m_conv_dw, m_conv_ln_g=m_conv_ln_g, m_conv_ln_b=m_conv_ln_b, m_sg_ln_g=m_sg_ln_g, m_sg_ln_b=m_sg_ln_b, m_sg_w=m_sg_w, m_sg_b=m_sg_b, m_ln_mix_g=m_ln_mix_g, m_ln_mix_b=m_ln_mix_b, m_ffn_w_gate=m_ffn_w_gate, m_ffn_w_up=m_ffn_w_up, m_ffn_w_down=m_ffn_w_down, m_ln_ffn_g=m_ln_ffn_g, m_ln_ffn_b=m_ln_ffn_b, m_ple_w_proj=m_ple_w_proj, m_ple_w_gate=m_ple_w_gate, m_ple_b_gate=m_ple_b_gate, v_even_w_in=v_even_w_in, v_even_w_out=v_even_w_out, v_pool_w=v_pool_w, v_pool_scale=v_pool_scale, v_odd_w_in=v_odd_w_in, v_odd_w_out=v_odd_w_out, v_conv_dw=v_conv_dw, v_conv_ln_g=v_conv_ln_g, v_conv_ln_b=v_conv_ln_b, v_sg_ln_g=v_sg_ln_g, v_sg_ln_b=v_sg_ln_b, v_sg_w=v_sg_w, v_sg_b=v_sg_b, v_ln_mix_g=v_ln_mix_g, v_ln_mix_b=v_ln_mix_b, v_ffn_w_gate=v_ffn_w_gate, v_ffn_w_up=v_ffn_w_up, v_ffn_w_down=v_ffn_w_down, v_ln_ffn_g=v_ln_ffn_g, v_ln_ffn_b=v_ln_ffn_b, v_ple_w_proj=v_ple_w_proj, v_ple_w_gate=v_ple_w_gate, v_ple_b_gate=v_ple_b_gate)
    weights = {n: given[n] for n in TWIN_WEIGHTS}
    shared = {n: given[n] for n in SHARED_INPUTS}
    per_example = {n: given[n] for n in ['x', 'p']}
    grad_fn = _jax.value_and_grad(_loss, argnums=(0, 1))

    def one_microbatch(ex, loss_target):
        ex = dict(ex)
        diff = ex.pop(TWIN_DIFF_INPUT)
        return grad_fn(weights, diff, {**shared, **ex}, loss_target)

    if N_MICROBATCH == 1:
        loss, (grad_w, grad_x) = one_microbatch(per_example, given["loss_target"])
    else:
        def body(carry, xs):
            loss_sum, grad_sum = carry
            l_k, (gw_k, gx_k) = one_microbatch(xs[0], xs[1])
            with _jax.named_scope("update"):
                return (loss_sum + l_k, _jax.tree.map(_jnp.add, grad_sum, gw_k)), gx_k

        init = (_jnp.zeros((), _jnp.float32), _jax.tree.map(_jnp.zeros_like, weights))
        (loss, grad_w), grad_x = _jax.lax.scan(body, init, (per_example, given["loss_target"]))
    with _jax.named_scope("update"):
        delta_w, new_m, new_v = {}, {}, {}
        for n in TWIN_WEIGHTS:
            delta_w[n], new_m[n], new_v[n] = _adamw(weights[n], grad_w[n], given["m_" + n], given["v_" + n])
    return (loss, grad_x, *[grad_w[n] for n in TWIN_WEIGHTS], *[delta_w[n] for n in TWIN_WEIGHTS],
            *[new_m[n] for n in TWIN_WEIGHTS], *[new_v[n] for n in TWIN_WEIGHTS])
```

```python
import functools
import math

import jax
import jax.numpy as jnp
from jax import lax
from jax.experimental import pallas as pl
from jax.experimental.pallas import tpu as pltpu

F32, BF16 = jnp.float32, jnp.bfloat16
ALPHA = 4.0 ** 0.25
LN_EPS = 1e-5
QK_SCALE = 0.125
POOL_WINDOWS = (2, 4, 8, 16)
CONV_TAPS = 31
N_DEV = 8
FF_SHARD, FF_PAD = 352, 384
ADAM_LR, ADAM_B1, ADAM_B2, ADAM_EPS, ADAM_WD, ADAM_STEP = 0.001, 0.9, 0.999, 1e-08, 0.01, 10
VMEM_LIMIT = 56 * 1024 * 1024
MESH_T = pl.DeviceIdType.MESH


def _cp(sem=None):
    return pltpu.CompilerParams(dimension_semantics=sem, vmem_limit_bytes=VMEM_LIMIT)


def _dot(a, b):
    return jnp.dot(a, b, preferred_element_type=F32)


def _dot_nt(a, b):
    return lax.dot_general(a, b, (((1,), (1,)), ((), ())), preferred_element_type=F32)


def _dot_tn(a, b):
    return lax.dot_general(a, b, (((0,), (0,)), ((), ())), preferred_element_type=F32)


def _sigmoid(x):
    return 1.0 / (1.0 + jnp.exp(-x))


def _softplus(z):
    return jnp.maximum(z, 0.0) + jnp.log(1.0 + jnp.exp(-jnp.abs(z)))


_GELU_C = math.sqrt(2.0 / math.pi)


def _gelu(x):
    return 0.5 * x * (1.0 + jnp.tanh(_GELU_C * (x + 0.044715 * x * x * x)))


def _gelu_grad(x):
    t = jnp.tanh(_GELU_C * (x + 0.044715 * x * x * x))
    return 0.5 * (1.0 + t) + 0.5 * x * (1.0 - t * t) * _GELU_C * (1.0 + 3.0 * 0.044715 * x * x)


def _ln_fwd(r, g, b):
    mu = jnp.mean(r, axis=-1, keepdims=True)
    xc = r - mu
    var = jnp.mean(xc * xc, axis=-1, keepdims=True)
    rstd = lax.rsqrt(var + LN_EPS)
    xh = xc * rstd
    return xh * g + b, xh, rstd


def _ln_bwd(dy, xh, rstd, g):
    dxh = dy * g
    m1 = jnp.mean(dxh, axis=-1, keepdims=True)
    m2 = jnp.mean(dxh * xh, axis=-1, keepdims=True)
    return rstd * (dxh - m1 - xh * m2)


def _split2(x):
    hi = x.astype(BF16)
    lo = (x - hi.astype(F32)).astype(BF16)
    return hi, lo


def _colsum(x):
    return jnp.sum(x, axis=0, keepdims=True)


def _tok_call(name, body, tiled, full, out_tiled, out_acc=(), tm=256, scratch=()):
    def arr(t):
        return t[0] if isinstance(t, tuple) else t
    S = arr(tiled[0]).shape[0]
    tm = min(tm, S)

    def tspec(t):
        if isinstance(t, tuple):
            _, w, cb = t
            return pl.BlockSpec((tm, w), lambda i, cb=cb: (i, cb))
        return pl.BlockSpec((tm, t.shape[1]), lambda i: (i, 0))

    def fspec(t):
        if isinstance(t, tuple):
            a, l = t
            nd = a.ndim - 1
            return pl.BlockSpec((None,) + a.shape[1:], lambda i, l=l, nd=nd: (l,) + (0,) * nd)
        nd = t.ndim
        return pl.BlockSpec(t.shape, lambda i, nd=nd: (0,) * nd)

    def ospec(o):
        return pl.BlockSpec((tm, o.shape[1]), lambda i: (i, 0))

    def aspec(o):
        nd = len(o.shape)
        return pl.BlockSpec(o.shape, lambda i, nd=nd: (0,) * nd)

    outs = pl.pallas_call(
        body, name=name, grid=(S // tm,),
        in_specs=[tspec(t) for t in tiled] + [fspec(t) for t in full],
        out_specs=[ospec(o) for o in out_tiled] + [aspec(o) for o in out_acc],
        out_shape=list(out_tiled) + list(out_acc),
        scratch_shapes=list(scratch),
        compiler_params=_cp(("arbitrary",)),
    )(*[arr(t) for t in tiled], *[arr(t) for t in full])
    return outs


def _sds(shape, dtype=F32):
    return jax.ShapeDtypeStruct(tuple(shape), dtype)


def _acc(ref, val):
    @pl.when(pl.program_id(0) == 0)
    def _():
        ref[...] = val

    @pl.when(pl.program_id(0) != 0)
    def _():
        ref[...] += val


def mm_in(x, w):
    S, N = x.shape[0], w.shape[1]

    def body(x_ref, w_ref, h_ref, xb_ref):
        xb = x_ref[...].astype(BF16)
        xb_ref[...] = xb
        h_ref[...] = _dot(xb, w_ref[...])

    return _tok_call("mm_in", body, [x], [w], [_sds((S, N)), _sds((S, x.shape[1]), BF16)], tm=512)


def attn_fwd(h, T=128):
    S = h.shape[0]
    nq = S // T

    def body(q_ref, k_ref, v_ref, o_ref, t_ref):
        i = pl.program_id(1)
        lane = lax.broadcasted_iota(jnp.int32, (1, 128), 1)
        hm = (lane < 64, lane >= 64)
        row = lax.broadcasted_iota(jnp.int32, (T, T), 0)
        col = lax.broadcasted_iota(jnp.int32, (T, T), 1)
        u_incl = (row >= col).astype(BF16)
        causal = col < row
        qs = q_ref[...] * QK_SCALE
        qh = [jnp.where(m, qs, 0.0).astype(BF16) for m in hm]

        def block(kb, carry, diag):
            ks = pl.multiple_of(kb * T, T)
            k = k_ref[pl.ds(ks, T), :].astype(BF16)
            v = v_ref[pl.ds(ks, T), :]
            c, acc = list(carry[:2]), carry[2]
            for hd in range(2):
                z = _dot_nt(qh[hd], k)
                sp = _softplus(z)
                lk = jnp.where(causal, -sp, 0.0) if diag else -sp
                hi, lo = _split2(lk)
                incl = _dot(hi, u_incl) + _dot(lo, u_incl)
                e = (z - sp) + c[hd] + (incl - lk)
                w = jnp.exp(e)
                if diag:
                    w = jnp.where(causal, w, 0.0)
                vh = jnp.where(hm[hd], v, 0.0).astype(BF16)
                acc = acc + _dot(w.astype(BF16), vh)
                c[hd] = c[hd] + incl[:, 0:1]
            return c[0], c[1], acc

        zero = jnp.zeros((T, 1), F32)
        carry = block(i, (zero, zero, jnp.zeros((T, 128), F32)), True)
        carry = lax.fori_loop(0, i, lambda jj, cr: block(i - 1 - jj, cr, False), carry)
        o_ref[...] = carry[2].astype(BF16)
        t_ref[...] = jnp.where(hm[0], carry[0], carry[1])

    return pl.pallas_call(
        body, name="attn_fwd", grid=(4, nq),
        in_specs=[pl.BlockSpec((T, 128), lambda p, i: (i, p)),
                  pl.BlockSpec((S, 128), lambda p, i: (0, 4 + p)),
                  pl.BlockSpec((S, 128), lambda p, i: (0, 8 + p))],
        out_specs=[pl.BlockSpec((T, 128), lambda p, i: (i, p)),
                   pl.BlockSpec((T, 128), lambda p, i: (i, p))],
        out_shape=[_sds((S, 512), BF16), _sds((S, 512))],
        compiler_params=_cp(("arbitrary", "arbitrary")),
    )(h, h, h)


def pool_fwd(h, pool_w, pool_scale, CH=256):
    S = h.shape[0]
    CH = min(CH, S)

    def body(u_ref, w_ref, sc_ref, b_ref, pooled_ref, pad_ref):
        pad_ref[0:16, :] = jnp.zeros((16, 512), F32)
        pad_ref[16:16 + S, :] = u_ref[...]
        for g, win in enumerate(POOL_WINDOWS):
            cs = slice(g * 128, (g + 1) * 128)
            wq = w_ref[g].astype(BF16)
            for ch in range(S // CH):
                base = ch * CH
                acc = pad_ref[16 + base:16 + base + CH, cs]
                for sft in range(1, win):
                    acc = acc + pad_ref[16 + base - sft:16 + base - sft + CH, cs]
                t = base + lax.broadcasted_iota(jnp.int32, (CH, 1), 0)
                cnt = jnp.minimum(t + 1, win).astype(F32)
                pooled = (acc / cnt - pad_ref[16 + base:16 + base + CH, cs]).astype(BF16)
                pooled_ref[base:base + CH, cs] = pooled
                b_ref[base:base + CH, cs] = (_dot(pooled, wq) * sc_ref[:, cs]).astype(BF16)

    return pl.pallas_call(
        body, name="pool_fwd", grid=(1,),
        in_specs=[pl.BlockSpec((S, 512), lambda i: (0, 3)),
                  pl.BlockSpec((4, 128, 128), lambda i: (0, 0, 0)),
                  pl.BlockSpec((1, 512), lambda i: (0, 0))],
        out_specs=[pl.BlockSpec((S, 512), lambda i: (0, 0)), pl.BlockSpec((S, 512), lambda i: (0, 0))],
        out_shape=[_sds((S, 512), BF16), _sds((S, 512), BF16)],
        scratch_shapes=[pltpu.VMEM((S + 16, 512), F32)],
        compiler_params=_cp(("arbitrary",)),
    )(h, pool_w, pool_scale)


def conv_fwd(h, dw, CH=128):
    S = h.shape[0]

    def body(a_ref, g_ref, dw_ref, y_ref, hc_ref, pad_ref):
        hc = a_ref[...] * _sigmoid(g_ref[...])
        hc_ref[...] = hc
        pad_ref[0:32, :] = jnp.zeros((32, 128), F32)
        pad_ref[32:32 + S, :] = hc
        for ch in range(S // CH):
            base = ch * CH + 2
            acc = dw_ref[0:1, :] * pad_ref[base:base + CH, :]
            for k in range(1, CONV_TAPS):
                acc = acc + dw_ref[k:k + 1, :] * pad_ref[base + k:base + k + CH, :]
            y_ref[ch * CH:(ch + 1) * CH, :] = acc

    return pl.pallas_call(
        body, name="conv_fwd", grid=(4,),
        in_specs=[pl.BlockSpec((S, 128), lambda c: (0, c)),
                  pl.BlockSpec((S, 128), lambda c: (0, 4 + c)),
                  pl.BlockSpec((CONV_TAPS, 128), lambda c: (0, c))],
        out_specs=[pl.BlockSpec((S, 128), lambda c: (0, c)), pl.BlockSpec((S, 128), lambda c: (0, c))],
        out_shape=[_sds((S, 512)), _sds((S, 512))],
        scratch_shapes=[pltpu.VMEM((S + 32, 128), F32)],
        compiler_params=_cp(("arbitrary",)),
    )(h, h, dw)


def _masked_sg_w(w_ref, g):
    row = lax.broadcasted_iota(jnp.int32, (128, 128), 0)
    col = lax.broadcasted_iota(jnp.int32, (128, 128), 1)
    return jnp.where(row >= col, w_ref[g], 0.0).astype(BF16)


def odd_post(y, h, cl_g, cl_b, sl_g, sl_b, sg_w, sgb_bc, tm=256):
    S = y.shape[0]
    tm = min(tm, S)

    def body(y_ref, zc_ref, clg, clb, slg, slb, w_ref, sb_ref,
             c_ref, d_ref, xhc_ref, rsc_ref, xhv_ref, rsv_ref, sv_ref):
        lnc, xhc, rsc = _ln_fwd(y_ref[...], clg[...], clb[...])
        c_ref[...] = (lnc * _sigmoid(lnc)).astype(BF16)
        xhc_ref[...] = xhc
        rsc_ref[...] = rsc
        z = _gelu(zc_ref[...])
        vn, xhv, rsv = _ln_fwd(z[:, 512:], slg[...], slb[...])
        xhv_ref[...] = xhv
        rsv_ref[...] = rsv
        vnb = vn.astype(BF16)
        for g in range(4):
            wm = _masked_sg_w(w_ref, g)
            for ch in range(tm // 128):
                rs, cs = slice(ch * 128, (ch + 1) * 128), slice(g * 128, (g + 1) * 128)
                sv_ref[rs, cs] = _dot(wm, vnb[rs, cs]) + sb_ref[g]
        d_ref[...] = (z[:, :512] * sv_ref[...]).astype(BF16)

    return _tok_call(
        "odd_post", body, [y, (h, 1024, 1)], [cl_g, cl_b, sl_g, sl_b, sg_w, sgb_bc],
        [_sds((S, 512), BF16), _sds((S, 512), BF16), _sds((S, 512)), _sds((S, 1)),
         _sds((S, 512)), _sds((S, 1)), _sds((S, 512))], tm=tm)


def mm_out_ln(l1, l2, x, w, g, b):
    S, D = x.shape

    def body(l1_ref, l2_ref, x_ref, w_ref, g_ref, b_ref, y_ref, xh_ref, rs_ref):
        mix = _dot(l1_ref[...], w_ref[0:512, :]) + _dot(l2_ref[...], w_ref[512:1024, :])
        y, xh, rs = _ln_fwd(ALPHA * x_ref[...] + mix, g_ref[...], b_ref[...])
        y_ref[...] = y
        xh_ref[...] = xh
        rs_ref[...] = rs

    return _tok_call("mm_out_ln", body, [l1, l2, x], [w, g, b],
                     [_sds((S, D)), _sds((S, D)), _sds((S, 1))])


def ffn_up(x1, wg, wu, layer):
    S, D = x1.shape
    F = wg.shape[2]

    def body(x_ref, wg_ref, wu_ref, gate_ref, up_ref, hb_ref, xb_ref):
        xb = x_ref[...].astype(BF16)
        xb_ref[...] = xb
        gate = _dot(xb, wg_ref[...])
        up = _dot(xb, wu_ref[...])
        gate_ref[...] = gate
        up_ref[...] = up
        hb_ref[...] = (gate * _sigmoid(gate) * up).astype(BF16)

    return _tok_call("ffn_up", body, [x1], [(wg, layer), (wu, layer)],
                     [_sds((S, F)), _sds((S, F)), _sds((S, F), BF16), _sds((S, D), BF16)])


def ffn_down_ln(hb, x1, wd, layer, g, b):
    S, D = x1.shape

    def body(h_ref, x_ref, w_ref, g_ref, b_ref, y_ref, xh_ref, rs_ref):
        f = _dot(h_ref[...], w_ref[...])
        y, xh, rs = _ln_fwd(ALPHA * x_ref[...] + f, g_ref[...], b_ref[...])
        y_ref[...] = y
        xh_ref[...] = xh
        rs_ref[...] = rs

    return _tok_call("ffn_down_ln", body, [hb, x1], [(wd, layer), g, b],
                     [_sds((S, D)), _sds((S, D)), _sds((S, 1))])


def ple_fwd(x2, p, wpg, wpp, layer, bg, target=None):
    S, D = x2.shape
    last = target is not None

    def body(*refs):
        if last:
            x_ref, p_ref, t_ref, wg_ref, wp_ref, b_ref, x3_ref, sg_ref, pp_ref, xb_ref, pb_ref, dy_ref, ls_ref = refs
        else:
            x_ref, p_ref, wg_ref, wp_ref, b_ref, x3_ref, sg_ref, pp_ref, xb_ref, pb_ref = refs
        x = x_ref[...]
        xb = x.astype(BF16)
        pb = p_ref[...].astype(BF16)
        xb_ref[...] = xb
        pb_ref[...] = pb
        sg = _sigmoid(_dot(xb, wg_ref[...]) + b_ref[...])
        pp = _dot(pb, wp_ref[...])
        sg_ref[...] = sg
        pp_ref[...] = pp
        x3 = x + sg * pp
        x3_ref[...] = x3
        if last:
            err = x3 - t_ref[...]
            dy_ref[...] = err * (1.0 / D)
            _acc(ls_ref, _colsum(err * err))

    outs = [_sds((S, D)), _sds((S, D)), _sds((S, D)), _sds((S, D), BF16), _sds((S, p.shape[1]), BF16)]
    tiled = [x2, p] + ([target] if last else [])
    if last:
        outs.append(_sds((S, D)))
    return _tok_call("ple_fwd", body, tiled, [(wpg, layer), (wpp, layer), bg], outs,
                     [_sds((1, D))] if last else [])


def ple_bwd(dx3, sg, pp, wpg, layer):
    S, D = dx3.shape

    def body(d_ref, sg_ref, pp_ref, w_ref, dx_ref, dgp_ref, dpp_ref, dbg_ref):
        d, sg = d_ref[...], sg_ref[...]
        dgp = d * pp_ref[...] * sg * (1.0 - sg)
        dgpb = dgp.astype(BF16)
        dgp_ref[...] = dgpb
        dpp_ref[...] = (d * sg).astype(BF16)
        dx_ref[...] = d + _dot_nt(dgpb, w_ref[...])
        _acc(dbg_ref, _colsum(dgp))

    return _tok_call("ple_bwd", body, [dx3, sg, pp], [(wpg, layer)],
                     [_sds((S, D)), _sds((S, D), BF16), _sds((S, D), BF16)], [_sds((1, D))])


def ffn_bwd_a(dx2, xh, rs, g, gate, up, wd, layer):
    S, D = dx2.shape
    F = gate.shape[1]

    def body(d_ref, xh_ref, rs_ref, gate_ref, up_ref, g_ref, w_ref,
             dr_ref, drb_ref, dg_ref, du_ref, dlg_ref, dlb_ref):
        d, xh = d_ref[...], xh_ref[...]
        dr = _ln_bwd(d, xh, rs_ref[...], g_ref[...])
        drb = dr.astype(BF16)
        dr_ref[...] = dr
        drb_ref[...] = drb
        _acc(dlg_ref, _colsum(d * xh))
        _acc(dlb_ref, _colsum(d))
        dh = _dot_nt(drb, w_ref[...])
        gate, up = gate_ref[...], up_ref[...]
        s = _sigmoid(gate)
        dg_ref[...] = (dh * up * s * (1.0 + gate * (1.0 - s))).astype(BF16)
        du_ref[...] = (dh * gate * s).astype(BF16)

    return _tok_call("ffn_bwd_a", body, [dx2, xh, rs, gate, up], [g, (wd, layer)],
                     [_sds((S, D)), _sds((S, D), BF16), _sds((S, F), BF16), _sds((S, F), BF16)],
                     [_sds((1, D)), _sds((1, D))])


def ffn_bwd_b(dr, dgate_b, dup_b, wg, wu, layer):
    S, D = dr.shape

    def body(dr_ref, dg_ref, du_ref, wg_ref, wu_ref, dx_ref):
        dx_ref[...] = (ALPHA * dr_ref[...] + _dot_nt(dg_ref[...], wg_ref[...])
                       + _dot_nt(du_ref[...], wu_ref[...]))

    return _tok_call("ffn_bwd_b", body, [dr, dgate_b, dup_b], [(wg, layer), (wu, layer)], [_sds((S, D))])[0]


def mix_bwd(dx1, xh, rs, g, w):
    S, D = dx1.shape

    def body(d_ref, xh_ref, rs_ref, g_ref, w_ref, dr_ref, dmb_ref, dl_ref, dlg_ref, dlb_ref):
        d, xh = d_ref[...], xh_ref[...]
        dr = _ln_bwd(d, xh, rs_ref[...], g_ref[...])
        drb = dr.astype(BF16)
        dr_ref[...] = dr
        dmb_ref[...] = drb
        dl_ref[...] = _dot_nt(drb, w_ref[...])
        _acc(dlg_ref, _colsum(d * xh))
        _acc(dlb_ref, _colsum(d))

    return _tok_call("mix_bwd", body, [dx1, xh, rs], [g, w],
                     [_sds((S, D)), _sds((S, D), BF16), _sds((S, D))], [_sds((1, D)), _sds((1, D))])


def dx_in(dr, pieces, w):
    S, D = dr.shape
    offs = [o for _, o in pieces]
    widths = [a.shape[1] for a, _ in pieces]

    def body(*refs):
        dr_ref, prefs, w_ref, dx_ref = refs[0], refs[1:1 + len(pieces)], refs[-2], refs[-1]
        acc = ALPHA * dr_ref[...]
        for pr, o, n in zip(prefs, offs, widths):
            acc = acc + _dot_nt(pr[...], w_ref[:, o:o + n])
        dx_ref[...] = acc

    return _tok_call("dx_in", body, [dr] + [a for a, _ in pieces], [w], [_sds((S, D))])[0]


def odd_post_bwd(dl, h, xhc, rsc, xhv, rsv, sv, cl_g, cl_b, sl_g, sl_b, sg_w, tm=256):
    S = dl.shape[0]
    tm = min(tm, S)

    def body(dl_ref, zc_ref, xhc_ref, rsc_ref, xhv_ref, rsv_ref, sv_ref, clg, clb, slg, slb, w_ref,
             dy_ref, dzc_ref, dclg_ref, dclb_ref, dslg_ref, dslb_ref, dwm_ref, dsb_ref, dvn_ref):
        first = pl.program_id(0) == 0
        last = pl.program_id(0) == pl.num_programs(0) - 1
        dc, dd = dl_ref[:, 0:512], dl_ref[:, 512:1024]
        xhc = xhc_ref[...]
        lnc = xhc * clg[...] + clb[...]
        s = _sigmoid(lnc)
        dlnc = dc * s * (1.0 + lnc * (1.0 - s))
        dy_ref[...] = _ln_bwd(dlnc, xhc, rsc_ref[...], clg[...])
        _acc(dclg_ref, _colsum(dlnc * xhc))
        _acc(dclb_ref, _colsum(dlnc))
        zc = zc_ref[...]
        z = _gelu(zc)
        dsv = dd * z[:, :512]
        dsvb = dsv.astype(BF16)
        xhv = xhv_ref[...]
        vnb = (xhv * slg[...] + slb[...]).astype(BF16)

        @pl.when(first)
        def _():
            dwm_ref[...] = jnp.zeros_like(dwm_ref)
            dsb_ref[...] = jnp.zeros_like(dsb_ref)

        for g in range(4):
            wm = _masked_sg_w(w_ref, g)
            for ch in range(tm // 128):
                rs_, cs = slice(ch * 128, (ch + 1) * 128), slice(g * 128, (g + 1) * 128)
                dwm_ref[g] += _dot_nt(dsvb[rs_, cs], vnb[rs_, cs])
                dvn_ref[rs_, cs] = _dot_tn(wm, dsvb[rs_, cs])
                dsb_ref[g] += dsv[rs_, cs]
        dvn = dvn_ref[...]
        dvv = _ln_bwd(dvn, xhv, rsv_ref[...], slg[...])
        _acc(dslg_ref, _colsum(dvn * xhv))
        _acc(dslb_ref, _colsum(dvn))
        gg = _gelu_grad(zc)
        dzc_ref[:, 0:512] = (dd * sv_ref[...] * gg[:, :512]).astype(BF16)
        dzc_ref[:, 512:1024] = (dvv * gg[:, 512:]).astype(BF16)

        @pl.when(last)
        def _():
            row = lax.broadcasted_iota(jnp.int32, (128, 128), 0)
            col = lax.broadcasted_iota(jnp.int32, (128, 128), 1)
            for g in range(4):
                dwm_ref[g] = jnp.where(row >= col, dwm_ref[g], 0.0)
                dsb_ref[g] = jnp.broadcast_to(jnp.sum(dsb_ref[g], axis=1, keepdims=True), (128, 128))

    return _tok_call(
        "odd_post_bwd", body, [dl, (h, 1024, 1), xhc, rsc, xhv, rsv, sv], [cl_g, cl_b, sl_g, sl_b, sg_w],
        [_sds((S, 512)), _sds((S, 1024), BF16)],
        [_sds((1, 512)), _sds((1, 512)), _sds((1, 512)), _sds((1, 512)), _sds((4, 128, 128)), _sds((4, 128, 128))],
        tm=tm, scratch=[pltpu.VMEM((tm, 512), F32)])


def conv_bwd(dy, hc, h, dw, CH=128):
    S = dy.shape[0]

    def body(dy_ref, hc_ref, a_ref, g_ref, dw_ref, da_ref, dg_ref, ddw_ref, padh_ref, padd_ref, dhc_ref):
        padh_ref[0:32, :] = jnp.zeros((32, 128), F32)
        padh_ref[32:32 + S, :] = hc_ref[...]
        padd_ref[0:S, :] = dy_ref[...]
        padd_ref[S:S + 32, :] = jnp.zeros((32, 128), F32)
        taps = [jnp.zeros((1, 128), F32) for _ in range(CONV_TAPS)]
        for ch in range(S // CH):
            b0 = ch * CH
            dyc = padd_ref[b0:b0 + CH, :]
            acc = dw_ref[0:1, :] * padd_ref[b0 + 30:b0 + 30 + CH, :]
            taps[0] = taps[0] + _colsum(dyc * padh_ref[b0 + 2:b0 + 2 + CH, :])
            for k in range(1, CONV_TAPS):
                acc = acc + dw_ref[k:k + 1, :] * padd_ref[b0 + 30 - k:b0 + 30 - k + CH, :]
                taps[k] = taps[k] + _colsum(dyc * padh_ref[b0 + 2 + k:b0 + 2 + k + CH, :])
            dhc_ref[b0:b0 + CH, :] = acc
        for k in range(CONV_TAPS):
            ddw_ref[k:k + 1, :] = taps[k]
        dhc = dhc_ref[...]
        s = _sigmoid(g_ref[...])
        da_ref[...] = (dhc * s).astype(BF16)
        dg_ref[...] = (dhc * a_ref[...] * s * (1.0 - s)).astype(BF16)

    return pl.pallas_call(
        body, name="conv_bwd", grid=(4,),
        in_specs=[pl.BlockSpec((S, 128), lambda c: (0, c)),
                  pl.BlockSpec((S, 128), lambda c: (0, c)),
                  pl.BlockSpec((S, 128), lambda c: (0, c)),
                  pl.BlockSpec((S, 128), lambda c: (0, 4 + c)),
                  pl.BlockSpec((CONV_TAPS, 128), lambda c: (0, c))],
        out_specs=[pl.BlockSpec((S, 128), lambda c: (0, c)), pl.BlockSpec((S, 128), lambda c: (0, c)),
                   pl.BlockSpec((CONV_TAPS, 128), lambda c: (0, c))],
        out_shape=[_sds((S, 512), BF16), _sds((S, 512), BF16), _sds((CONV_TAPS, 512))],
        scratch_shapes=[pltpu.VMEM((S + 32, 128), F32), pltpu.VMEM((S + 32, 128), F32), pltpu.VMEM((S, 128), F32)],
        compiler_params=_cp(("arbitrary",)),
    )(dy, hc, h, h, dw)


def attn_bwd(h, dl, tb, T=128):
    S = h.shape[0]
    nq = S // T

    def body(q_ref, k_ref, v_ref, do_ref, t_ref, dq_ref, dk_ref, dv_ref, dka_ref, dva_ref):
        i = pl.program_id(1)
        lane = lax.broadcasted_iota(jnp.int32, (1, 128), 1)
        hm = (lane < 64, lane >= 64)
        row = lax.broadcasted_iota(jnp.int32, (T, T), 0)
        col = lax.broadcasted_iota(jnp.int32, (T, T), 1)
        u_le = (row <= col).astype(BF16)
        u_lt = (row < col).astype(BF16)
        causal = col < row

        @pl.when(i == 0)
        def _():
            dka_ref[...] = jnp.zeros_like(dka_ref)
            dva_ref[...] = jnp.zeros_like(dva_ref)

        qs = q_ref[...] * QK_SCALE
        do = do_ref[...]
        tb_ = t_ref[...]
        qh = [jnp.where(m, qs, 0.0).astype(BF16) for m in hm]
        doh = [jnp.where(m, do, 0.0).astype(BF16) for m in hm]
        tot = [tb_[:, 0:1], tb_[:, 64:65]]

        def block(kb, carry, diag):
            ks = pl.multiple_of(kb * T, T)
            kf = k_ref[pl.ds(ks, T), :]
            vf = v_ref[pl.ds(ks, T), :]
            kb16 = kf.astype(BF16)
            pc, gc, dq = list(carry[0:2]), list(carry[2:4]), carry[4]
            dk_blk = jnp.zeros((T, 128), F32)
            dv_blk = jnp.zeros((T, 128), F32)
            for hd in range(2):
                z = _dot_nt(qh[hd], kb16)
                sp = _softplus(z)
                lk = jnp.where(causal, -sp, 0.0) if diag else -sp
                hi, lo = _split2(lk)
                pre = _dot(hi, u_le) + _dot(lo, u_le)
                sig = jnp.exp(z - sp)
                w = sig * jnp.exp(tot[hd] - pc[hd] - pre)
                if diag:
                    w = jnp.where(causal, w, 0.0)
                vh = jnp.where(hm[hd], vf, 0.0).astype(BF16)
                gmat = _dot_nt(doh[hd], vh) * w
                ghi, glo = _split2(gmat)
                gex = gc[hd] + _dot(ghi, u_lt) + _dot(glo, u_lt)
                dz = gmat * (1.0 - sig) - sig * gex
                if diag:
                    dz = jnp.where(causal, dz, 0.0)
                dzb = dz.astype(BF16)
                kh = jnp.where(hm[hd], kf, 0.0).astype(BF16)
                dq = dq + _dot(dzb, kh)
                dk_blk = dk_blk + _dot_tn(dzb, qh[hd])
                dv_blk = dv_blk + _dot_tn(w.astype(BF16), doh[hd])
                pc[hd] = pc[hd] + pre[:, T - 1:T]
                gc[hd] = gex[:, T - 1:T] + gmat[:, T - 1:T]
            dka_ref[pl.ds(ks, T), :] += dk_blk
            dva_ref[pl.ds(ks, T), :] += dv_blk
            return pc[0], pc[1], gc[0], gc[1], dq

        zero = jnp.zeros((T, 1), F32)
        carry = (zero, zero, zero, zero, jnp.zeros((T, 128), F32))
        carry = lax.fori_loop(0, i, lambda kb, cr: block(kb, cr, False), carry)
        carry = block(i, carry, True)
        dq_ref[...] = (carry[4] * QK_SCALE).astype(BF16)

        @pl.when(i == nq - 1)
        def _():
            dk_ref[...] = dka_ref[...].astype(BF16)
            dv_ref[...] = dva_ref[...].astype(BF16)

    return pl.pallas_call(
        body, name="attn_bwd", grid=(4, nq),
        in_specs=[pl.BlockSpec((T, 128), lambda p, i: (i, p)),
                  pl.BlockSpec((S, 128), lambda p, i: (0, 4 + p)),
                  pl.BlockSpec((S, 128), lambda p, i: (0, 8 + p)),
                  pl.BlockSpec((T, 128), lambda p, i: (i, p)),
                  pl.BlockSpec((T, 128), lambda p, i: (i, p))],
        out_specs=[pl.BlockSpec((T, 128), lambda p, i: (i, p)),
                   pl.BlockSpec((S, 128), lambda p, i: (0, p)),
                   pl.BlockSpec((S, 128), lambda p, i: (0, p))],
        out_shape=[_sds((S, 512), BF16), _sds((S, 512), BF16), _sds((S, 512), BF16)],
        scratch_shapes=[pltpu.VMEM((S, 128), F32), pltpu.VMEM((S, 128), F32)],
        compiler_params=_cp(("arbitrary", "arbitrary")),
    )(h, h, h, dl, tb)


def pool_bwd(dl, pooled_b, pool_w, pool_scale, CH=256):
    S = dl.shape[0]
    CH = min(CH, S)

    def body(db_ref, pooled_ref, w_ref, sc_ref, du_ref, dw_ref, dsc_ref, pad_ref, dp_ref):
        pad_ref[S:S + 16, :] = jnp.zeros((16, 128), F32)
        for g, win in enumerate(POOL_WINDOWS):
            cs = slice(g * 128, (g + 1) * 128)
            wq = w_ref[g].astype(BF16)
            dwg = jnp.zeros((128, 128), F32)
            dsc = jnp.zeros((1, 128), F32)
            for ch in range(S // CH):
                rs_ = slice(ch * CH, (ch + 1) * CH)
                db = db_ref[rs_, cs]
                pb = pooled_ref[rs_, cs]
                dsc = dsc + _colsum(db * _dot(pb, wq))
                dmsb = (db * sc_ref[:, cs]).astype(BF16)
                dwg = dwg + _dot_tn(pb, dmsb)
                dpool = _dot_nt(dmsb, wq)
                t = ch * CH + lax.broadcasted_iota(jnp.int32, (CH, 1), 0)
                cnt = jnp.minimum(t + 1, win).astype(F32)
                dp_ref[rs_, :] = dpool
                pad_ref[rs_, :] = dpool / cnt
            dw_ref[g] = dwg
            dsc_ref[:, cs] = dsc
            for ch in range(S // CH):
                base = ch * CH
                acc = pad_ref[base:base + CH, :]
                for sft in range(1, win):
                    acc = acc + pad_ref[base + sft:base + sft + CH, :]
                du_ref[base:base + CH, cs] = (acc - dp_ref[base:base + CH, :]).astype(BF16)

    return pl.pallas_call(
        body, name="pool_bwd", grid=(1,),
        in_specs=[pl.BlockSpec((S, 512), lambda i: (0, 1)),
                  pl.BlockSpec((S, 512), lambda i: (0, 0)),
                  pl.BlockSpec((4, 128, 128), lambda i: (0, 0, 0)),
                  pl.BlockSpec((1, 512), lambda i: (0, 0))],
        out_specs=[pl.BlockSpec((S, 512), lambda i: (0, 0)),
                   pl.BlockSpec((4, 128, 128), lambda i: (0, 0, 0)),
                   pl.BlockSpec((1, 512), lambda i: (0, 0))],
        out_shape=[_sds((S, 512), BF16), _sds((4, 128, 128)), _sds((1, 512))],
        scratch_shapes=[pltpu.VMEM((S + 16, 128), F32), pltpu.VMEM((S, 128), F32)],
        compiler_params=_cp(("arbitrary",)),
    )(dl, pooled_b, pool_w, pool_scale)


def tn_into(a, b, out, out_b, lead, r0, c0, tk=512, tn=512):
    S, K = a.shape
    N = b.shape[1]
    tk, tn = min(tk, K), min(tn, N)
    assert K % tk == 0 and N % tn == 0 and r0 % tk == 0 and c0 % tn == 0
    rb, cb = r0 // tk, c0 // tn
    fresh = isinstance(out, jax.ShapeDtypeStruct)

    def body(*refs):
        a_ref, b_ref, o_ref, ob_ref = refs[0], refs[1], refs[-2], refs[-1]
        r = _dot_tn(a_ref[...], b_ref[...])
        o_ref[...] = r
        ob_ref[...] = r.astype(BF16)

    ospec = pl.BlockSpec((None, tk, tn), lambda i, j: (lead, rb + i, cb + j))
    in_specs = [pl.BlockSpec((S, tk), lambda i, j: (0, i)), pl.BlockSpec((S, tn), lambda i, j: (0, j))]
    args = [a, b]
    aliases = {}
    if not fresh:
        in_specs += [pl.BlockSpec(memory_space=pl.ANY), pl.BlockSpec(memory_space=pl.ANY)]
        args += [out, out_b]
        aliases = {2: 0, 3: 1}
    shp = out.shape
    return pl.pallas_call(
        body, name="tn_grad", grid=(K // tk, N // tn),
        in_specs=in_specs, out_specs=[ospec, ospec],
        out_shape=[_sds(shp, F32), _sds(shp, BF16)],
        input_output_aliases=aliases,
        compiler_params=_cp(("arbitrary", "arbitrary")),
    )(*args)


def _row(a, i):
    return a[i:i + 1]


def _grad_slabs(shapes):
    return {k: (_sds(s, F32), _sds(s, BF16)) for k, s in shapes.items()}


def local_step(x, p, target, W, P):
    S, D = x.shape
    FP = W["ffn_w_gate"].shape[2]
    big = _grad_slabs({
        "even_w_in": (1, D, 2048), "odd_w_in": (1, D, 2048), "even_w_out": (1, 1024, D), "odd_w_out": (1, 1024, D),
        "ffn_w_gate": (2, D, FP), "ffn_w_up": (2, D, FP), "ffn_w_down": (2, FP, D),
        "ple_w_proj": (2, p.shape[2], D), "ple_w_gate": (2, D, D)})
    small = {}

    def tn(name, a, b, lead=0, r0=0, c0=0):
        big[name] = tuple(tn_into(a, b, big[name][0], big[name][1], lead, r0, c0))

    sgb_bc = jnp.broadcast_to(P["sg_b"][:, :, None], (4, 128, 128))

    saved = []
    xin = x
    for i in range(2):
        s = {}
        if i == 0:
            s["h"], s["xb"] = mm_in(xin, W["even_w_in"])
            s["l1"], s["tb"] = attn_fwd(s["h"])
            s["l2"], s["pooled"] = pool_fwd(s["h"], P["pool_w"], P["pool_scale"])
            wout = W["even_w_out"]
        else:
            s["h"], s["xb"] = mm_in(xin, W["odd_w_in"])
            s["y"], s["hc"] = conv_fwd(s["h"], P["conv_dw"])
            (s["l1"], s["l2"], s["xhc"], s["rsc"], s["xhv"], s["rsv"], s["sv"]) = odd_post(
                s["y"], s["h"], P["conv_ln_g"], P["conv_ln_b"], P["sg_ln_g"], P["sg_ln_b"], P["sg_w"], sgb_bc)
            wout = W["odd_w_out"]
        x1, s["xh1"], s["rs1"] = mm_out_ln(s["l1"], s["l2"], xin, wout, _row(P["ln_mix_g"], i), _row(P["ln_mix_b"], i))
        s["gate"], s["up"], s["hb"], s["x1b"] = ffn_up(x1, W["ffn_w_gate"], W["ffn_w_up"], i)
        x2, s["xh2"], s["rs2"] = ffn_down_ln(s["hb"], x1, W["ffn_w_down"], i, _row(P["ln_ffn_g"], i), _row(P["ln_ffn_b"], i))
        outs = ple_fwd(x2, p[i], W["ple_w_gate"], W["ple_w_proj"], i, _row(P["ple_b_gate"], i),
                       target if i == 1 else None)
        xin, s["sg"], s["pp"], s["x2b"], s["pb"] = outs[:5]
        if i == 1:
            dx, sq = outs[5], outs[6]
        saved.append(s)

    lng = {k: [None, None] for k in ("ln_mix_g", "ln_mix_b", "ln_ffn_g", "ln_ffn_b", "ple_b_gate")}
    for i in (1, 0):
        s = saved[i]
        dx2, dgp_b, dpp_b, lng["ple_b_gate"][i] = ple_bwd(dx, s["sg"], s["pp"], W["ple_w_gate"], i)
        tn("ple_w_gate", s["x2b"], dgp_b, lead=i)
        tn("ple_w_proj", s["pb"], dpp_b, lead=i)
        dr2, dr2_b, dgate_b, dup_b, lng["ln_ffn_g"][i], lng["ln_ffn_b"][i] = ffn_bwd_a(
            dx2, s["xh2"], s["rs2"], _row(P["ln_ffn_g"], i), s["gate"], s["up"], W["ffn_w_down"], i)
        tn("ffn_w_down", s["hb"], dr2_b, lead=i)
        tn("ffn_w_gate", s["x1b"], dgate_b, lead=i)
        tn("ffn_w_up", s["x1b"], dup_b, lead=i)
        dx1 = ffn_bwd_b(dr2, dgate_b, dup_b, W["ffn_w_gate"], W["ffn_w_up"], i)
        wout = W["odd_w_out"] if i == 1 else W["even_w_out"]
        dr1, dmix_b, dl, lng["ln_mix_g"][i], lng["ln_mix_b"][i] = mix_bwd(
            dx1, s["xh1"], s["rs1"], _row(P["ln_mix_g"], i), wout)
        oname, iname = ("odd_w_out", "odd_w_in") if i == 1 else ("even_w_out", "even_w_in")
        tn(oname, s["l1"], dmix_b, r0=0)
        tn(oname, s["l2"], dmix_b, r0=512)
        if i == 1:
            (dy, dzc_b, small["conv_ln_g"], small["conv_ln_b"], small["sg_ln_g"], small["sg_ln_b"],
             small["sg_w"], dsb) = odd_post_bwd(dl, s["h"], s["xhc"], s["rsc"], s["xhv"], s["rsv"], s["sv"],
                                                P["conv_ln_g"], P["conv_ln_b"], P["sg_ln_g"], P["sg_ln_b"], P["sg_w"])
            small["sg_b"] = dsb[:, :, 0]
            da_b, dg_b, small["conv_dw"] = conv_bwd(dy, s["hc"], s["h"], P["conv_dw"])
            pieces = [(da_b, 0), (dg_b, 512), (dzc_b, 1024)]
            win = W["odd_w_in"]
        else:
            dq_b, dk_b, dv_b = attn_bwd(s["h"], dl, s["tb"])
            du_b, small["pool_w"], small["pool_scale"] = pool_bwd(dl, s["pooled"], P["pool_w"], P["pool_scale"])
            pieces = [(dq_b, 0), (dk_b, 512), (dv_b, 1024), (du_b, 1536)]
            win = W["even_w_in"]
        for a, off in pieces:
            tn(iname, s["xb"], a, c0=off)
        dx = dx_in(dr1, pieces, win)
    for k, v in lng.items():
        small[k] = jnp.concatenate(v, axis=0)
    return sq, dx, big, small


BIG = {
    "even_w_in": ((1, 1024, 2048), 2, 256, 256),
    "even_w_out": ((1, 1024, 1024), 1, 128, 128),
    "odd_w_in": ((1, 1024, 2048), 2, 256, 256),
    "odd_w_out": ((1, 1024, 1024), 1, 128, 128),
    "ffn_w_gate": ((2, 1024, 8 * FF_PAD), 2, FF_PAD, FF_SHARD),
    "ffn_w_up": ((2, 1024, 8 * FF_PAD), 2, FF_PAD, FF_SHARD),
    "ffn_w_down": ((2, 8 * FF_PAD, 1024), 1, FF_PAD, FF_SHARD),
    "ple_w_proj": ((2, 256, 1024), 2, 128, 128),
    "ple_w_gate": ((2, 1024, 1024), 1, 128, 128),
}
BIG_ORDER = ("even_w_in", "even_w_out", "ffn_w_gate", "ffn_w_up", "ffn_w_down", "ple_w_gate", "ple_w_proj",
             "odd_w_in", "odd_w_out")
ANY = pl.BlockSpec(memory_space=pl.ANY)


def _win_shape(spec):
    full, axis, w, _ = spec
    return tuple(w if d == axis else n for d, n in enumerate(full))


def _window(ref, axis, w, j):
    idx = [slice(None)] * len(ref.shape)
    idx[axis] = pl.ds(j, 1) if w == 1 else pl.ds(pl.multiple_of(j * w, w), w)
    return ref.at[tuple(idx)]


def _mesh_pos():
    return lax.axis_index("x"), lax.axis_index("y"), lax.axis_index("c")


def all_gather(name, blocks, specs):
    n = len(blocks)

    def body(*refs):
        blk, full = refs[:n], refs[n:2 * n]
        send, recv, loc = refs[2 * n:]
        x, y, c = _mesh_pos()
        me, sib = (x, y, c), (x, y, 1 - c)
        chips = [(1 - x, y), (x, 1 - y), (1 - x, 1 - y)]

        def win(a, dev):
            return _window(full[a], specs[a][1], specs[a][2], 4 * dev[0] + 2 * dev[1] + dev[2])

        def cp(a, k, block_dev, to, src=None):
            return pltpu.make_async_remote_copy(
                src_ref=win(a, block_dev) if src is None else src, dst_ref=win(a, block_dev),
                send_sem=send.at[a, k], recv_sem=recv.at[a, k], device_id=to, device_id_type=MESH_T)

        mine = [pltpu.make_async_copy(blk[a], win(a, me), loc.at[a]) for a in range(n)]
        for m in mine:
            m.start()
        first, passed = [], []
        for a in range(n):
            f = [cp(a, 0, me, sib, src=blk[a])]
            f += [cp(a, 1 + j, me, (*ch, c), src=blk[a]) for j, ch in enumerate(chips)]
            for d in f:
                d.start()
            first += f
        for a in range(n):
            for j, ch in enumerate(chips):
                cp(a, 1 + j, (*ch, c), me).wait_recv()
                fw = cp(a, 4 + j, (*ch, c), sib)
                fw.start()
                passed.append(fw)
        for a in range(n):
            cp(a, 0, sib, me).wait_recv()
            for j, ch in enumerate(chips):
                cp(a, 4 + j, (*ch, 1 - c), me).wait_recv()
        for d in first + passed:
            d.wait_send()
        for m in mine:
            m.wait()

    return pl.pallas_call(
        body, name=name,
        in_specs=[ANY] * n, out_specs=[ANY] * n,
        out_shape=[_sds(s[0], b.dtype) for s, b in zip(specs, blocks)],
        scratch_shapes=[pltpu.SemaphoreType.DMA((n, 7)), pltpu.SemaphoreType.DMA((n, 7)),
                        pltpu.SemaphoreType.DMA((n,))],
    )(*blocks)


def rs_sibling(grads, grads_b, specs):
    n = len(grads)

    def body(*refs):
        g, gb = refs[:n], refs[n:2 * n]
        own, land = refs[2 * n:3 * n], refs[3 * n:4 * n]
        send, recv, loc = refs[4 * n:]
        x, y, c = _mesh_pos()
        copies = []
        for a in range(n):
            _, axis, w, _ = specs[a]
            for q in range(4):
                lc = pltpu.make_async_copy(_window(g[a], axis, w, 2 * q + c), own[a].at[q], loc.at[a, q])
                rc = pltpu.make_async_remote_copy(
                    src_ref=_window(gb[a], axis, w, 2 * q + (1 - c)), dst_ref=land[a].at[q],
                    send_sem=send.at[a, q], recv_sem=recv.at[a, q], device_id=(x, y, 1 - c), device_id_type=MESH_T)
                lc.start()
                rc.start()
                copies.append((lc, rc))
        for lc, rc in copies:
            rc.wait()
            lc.wait()

    wins = [_win_shape(s) for s in specs]
    return pl.pallas_call(
        body, name="rs_sibling",
        in_specs=[ANY] * (2 * n), out_specs=[ANY] * (2 * n),
        out_shape=[_sds((4,) + ws, F32) for ws in wins] + [_sds((4,) + ws, BF16) for ws in wins],
        scratch_shapes=[pltpu.SemaphoreType.DMA((n, 4)), pltpu.SemaphoreType.DMA((n, 4)),
                        pltpu.SemaphoreType.DMA((n, 4))],
    )(*grads, *grads_b)


def rs_chips(s1, s1b):
    n = len(s1)

    def body(*refs):
        f, fb = refs[:n], refs[n:2 * n]
        own, land = refs[2 * n:3 * n], refs[3 * n:4 * n]
        send, recv, loc = refs[4 * n:]
        x, y, c = _mesh_pos()
        q = 2 * x + y
        copies = []
        for a in range(n):
            lc = pltpu.make_async_copy(f[a].at[q], own[a], loc.at[a])
            lc.start()
            for d in (1, 2, 3):
                qd = lax.rem(q + d, 4)
                rc = pltpu.make_async_remote_copy(
                    src_ref=fb[a].at[qd], dst_ref=land[a].at[3 - d],
                    send_sem=send.at[a, d - 1], recv_sem=recv.at[a, 3 - d],
                    device_id=(lax.div(qd, 2), lax.rem(qd, 2), c), device_id_type=MESH_T)
                rc.start()
                copies.append(rc)
            copies.append(lc)
        for cpy in copies:
            cpy.wait()

    wins = [a.shape[1:] for a in s1]
    return pl.pallas_call(
        body, name="rs_chips",
        in_specs=[ANY] * (2 * n), out_specs=[ANY] * (2 * n),
        out_shape=[_sds(ws, F32) for ws in wins] + [_sds((3,) + ws, BF16) for ws in wins],
        scratch_shapes=[pltpu.SemaphoreType.DMA((n, 3)), pltpu.SemaphoreType.DMA((n, 3)),
                        pltpu.SemaphoreType.DMA((n,))],
    )(*s1, *s1b)


def pack_weights(shards, specs):
    n = len(shards)

    def body(*refs):
        for a in range(n):
            src, dst = refs[a], refs[n + a]
            _, axis, w, valid = specs[a]
            if valid == w:
                dst[...] = src[...].astype(BF16)
            else:
                dst[...] = jnp.zeros(dst.shape, BF16)
                if axis == 2:
                    dst[:, :, 0:valid] = src[...].astype(BF16)
                else:
                    dst[:, 0:valid, :] = src[...].astype(BF16)

    return pl.pallas_call(
        body, name="pack_weights",
        out_shape=[_sds(_win_shape(s), BF16) for s in specs],
        compiler_params=_cp(),
    )(*shards)


def add_pairs(own, land):
    C = own.shape[-1]
    a2, b2 = own.reshape(-1, C), land.reshape(-1, C)
    N = a2.shape[0]
    tr = min(512, N)
    assert N % tr == 0

    def body(a_ref, b_ref, o_ref, ob_ref):
        s = a_ref[...] + b_ref[...].astype(F32)
        o_ref[...] = s
        ob_ref[...] = s.astype(BF16)

    spec = pl.BlockSpec((tr, C), lambda i: (i, 0))
    o, ob = pl.pallas_call(
        body, name="add_pairs", grid=(N // tr,), in_specs=[spec, spec], out_specs=[spec, spec],
        out_shape=[_sds((N, C), F32), _sds((N, C), BF16)], compiler_params=_cp(("arbitrary",)),
    )(a2, b2)
    return o.reshape(own.shape), ob.reshape(own.shape)


def _adamw(w, g, m, v):
    m = ADAM_B1 * m + (1.0 - ADAM_B1) * g
    v = ADAM_B2 * v + (1.0 - ADAM_B2) * (g * g)
    m_hat = m / (1.0 - ADAM_B1 ** ADAM_STEP)
    v_hat = v / (1.0 - ADAM_B2 ** ADAM_STEP)
    delta = -ADAM_LR * (m_hat / (jnp.sqrt(v_hat) + ADAM_EPS) + ADAM_WD * w)
    return delta, m, v


def reduce_adamw(own, land, w, m, v, spec):
    _, axis, win, valid = spec
    L, R, C = w.shape
    if axis == 2:
        tr = min(256, R)
        grid = (L, R // tr)
        wspec = pl.BlockSpec((None, tr, win), lambda l, i: (l, i, 0))
        lspec = pl.BlockSpec((3, None, tr, win), lambda l, i: (0, l, i, 0))
        sspec = pl.BlockSpec((None, tr, C), lambda l, i: (l, i, 0))
    else:
        grid = (L, 1)
        wspec = pl.BlockSpec((None, win, C), lambda l, i: (l, 0, 0))
        lspec = pl.BlockSpec((3, None, win, C), lambda l, i: (0, l, 0, 0))
        sspec = pl.BlockSpec((None, R, C), lambda l, i: (l, 0, 0))

    def body(own_ref, land_ref, w_ref, m_ref, v_ref, g_ref, d_ref, nm_ref, nv_ref):
        if axis == 2:
            rd = lambda r, *lead: r[(*lead, slice(None), slice(0, valid))]
        else:
            rd = lambda r, *lead: r[(*lead, slice(0, valid), slice(None))]
        g = rd(own_ref)
        for k in range(3):
            g = g + rd(land_ref, k).astype(F32)
        g_ref[...] = g
        d, nm, nv = _adamw(w_ref[...], g, m_ref[...], v_ref[...])
        d_ref[...] = d
        nm_ref[...] = nm
        nv_ref[...] = nv

    return pl.pallas_call(
        body, name="reduce_adamw", grid=grid,
        in_specs=[wspec, lspec, sspec, sspec, sspec], out_specs=[sspec] * 4,
        out_shape=[_sds(w.shape)] * 4, compiler_params=_cp(("arbitrary", "arbitrary")),
    )(own, land, w, m, v)


def small_reduce_adamw(gathered, wmv):
    n, k = len(gathered), len(wmv)

    def body(*refs):
        gs = refs[:n]
        ws = refs[n:n + 3 * k]
        outs = refs[n + 3 * k:]
        for a in range(n):
            g = gs[a][0]
            for dev in range(1, N_DEV):
                g = g + gs[a][dev]
            outs[a][...] = g
            if a < k:
                d, nm, nv = _adamw(ws[3 * a][...], g, ws[3 * a + 1][...], ws[3 * a + 2][...])
                outs[n + 3 * a][...] = d
                outs[n + 3 * a + 1][...] = nm
                outs[n + 3 * a + 2][...] = nv

    flat = [t for tup in wmv for t in tup]
    out_shape = [_sds(g.shape[1:]) for g in gathered] + [_sds(t.shape) for t in flat]
    return pl.pallas_call(body, name="small_reduce_adamw", out_shape=out_shape, compiler_params=_cp())(*gathered, *flat)


def small_adamw(gs, wmv):
    k = len(gs)

    def body(*refs):
        for a in range(k):
            g, w, m, v = refs[4 * a:4 * a + 4]
            d, nm, nv = _adamw(w[...], g[...], m[...], v[...])
            refs[4 * k + 3 * a][...] = d
            refs[4 * k + 3 * a + 1][...] = nm
            refs[4 * k + 3 * a + 2][...] = nv

    args = [t for g, tup in zip(gs, wmv) for t in (g,) + tuple(tup)]
    out_shape = [_sds(g.shape) for g in gs for _ in range(3)]
    return pl.pallas_call(body, name="small_adamw", out_shape=out_shape, compiler_params=_cp())(*args)


WEIGHT_NAMES = ("even_w_in", "even_w_out", "pool_w", "pool_scale", "odd_w_in", "odd_w_out", "conv_dw", "conv_ln_g",
                "conv_ln_b", "sg_ln_g", "sg_ln_b", "sg_w", "sg_b", "ln_mix_g", "ln_mix_b", "ffn_w_gate", "ffn_w_up",
                "ffn_w_down", "ln_ffn_g", "ln_ffn_b", "ple_w_proj", "ple_w_gate", "ple_b_gate")
REPLICATED = ("pool_w", "pool_scale", "sg_w", "sg_b", "ln_mix_g", "ln_mix_b", "ln_ffn_g", "ln_ffn_b", "ple_b_gate")
SHARDED_SMALL = ("conv_dw", "conv_ln_g", "conv_ln_b", "sg_ln_g", "sg_ln_b")
NATURAL = {"pool_w": (4, 128, 128), "pool_scale": (1, 512), "sg_w": (4, 128, 128), "sg_b": (4, 128),
           "ln_mix_g": (2, 1024), "ln_mix_b": (2, 1024), "ln_ffn_g": (2, 1024), "ln_ffn_b": (2, 1024),
           "ple_b_gate": (2, 1024)}


def kernel(x, p, even_w_in, even_w_out, pool_w, pool_scale, odd_w_in, odd_w_out, conv_dw, conv_ln_g, conv_ln_b, sg_ln_g, sg_ln_b, sg_w, sg_b, ln_mix_g, ln_mix_b, ffn_w_gate, ffn_w_up, ffn_w_down, ln_ffn_g, ln_ffn_b, ple_w_proj, ple_w_gate, ple_b_gate, loss_target, m_even_w_in, m_even_w_out, m_pool_w, m_pool_scale, m_odd_w_in, m_odd_w_out, m_conv_dw, m_conv_ln_g, m_conv_ln_b, m_sg_ln_g, m_sg_ln_b, m_sg_w, m_sg_b, m_ln_mix_g, m_ln_mix_b, m_ffn_w_gate, m_ffn_w_up, m_ffn_w_down, m_ln_ffn_g, m_ln_ffn_b, m_ple_w_proj, m_ple_w_gate, m_ple_b_gate, v_even_w_in, v_even_w_out, v_pool_w, v_pool_scale, v_odd_w_in, v_odd_w_out, v_conv_dw, v_conv_ln_g, v_conv_ln_b, v_sg_ln_g, v_sg_ln_b, v_sg_w, v_sg_b, v_ln_mix_g, v_ln_mix_b, v_ffn_w_gate, v_ffn_w_up, v_ffn_w_down, v_ln_ffn_g, v_ln_ffn_b, v_ple_w_proj, v_ple_w_gate, v_ple_b_gate):
    A = dict(locals())
    specs = [BIG[k] for k in BIG_ORDER]

    blocks = pack_weights([A[k] for k in BIG_ORDER], specs)
    small_blk = jnp.concatenate([conv_dw[0], conv_ln_g, conv_ln_b, sg_ln_g, sg_ln_b, jnp.zeros((5, 64), F32)], axis=0)
    fulls = all_gather("ag_weights", list(blocks) + [small_blk[None]], specs + [((N_DEV, 40, 64), 0, 1, 1)])
    W = {k: f for k, f in zip(BIG_ORDER, fulls)}
    for k in ("even_w_in", "even_w_out", "odd_w_in", "odd_w_out"):
        W[k] = W[k][0]
    sm = fulls[-1].transpose(1, 0, 2).reshape(40, 512)
    P = {k: A[k].reshape(NATURAL[k]) for k in REPLICATED}
    P.update(conv_dw=sm[0:31], conv_ln_g=sm[31:32], conv_ln_b=sm[32:33], sg_ln_g=sm[33:34], sg_ln_b=sm[34:35])

    sq, dx, big, small = local_step(x[0], p[:, 0], loss_target[0], W, P)
    loss = lax.psum(0.5 * jnp.sum(sq) / x.shape[-1], ("x", "y", "c"))

    own1, land1 = [], []
    r = rs_sibling([big[k][0] for k in BIG_ORDER], [big[k][1] for k in BIG_ORDER], specs)
    own1, land1 = r[:len(specs)], r[len(specs):]
    s1, s1b = zip(*[add_pairs(o, l) for o, l in zip(own1, land1)])
    r = rs_chips(list(s1), list(s1b))
    own2, land2 = r[:len(specs)], r[len(specs):]
    res = {}
    for k, spec, o, l in zip(BIG_ORDER, specs, own2, land2):
        res[k] = reduce_adamw(o, l, A[k], A["m_" + k], A["v_" + k], spec)

    names = REPLICATED + SHARDED_SMALL
    gathered = all_gather("ag_small_grads", [small[k][None] for k in names],
                          [((N_DEV,) + small[k].shape, 0, 1, 1) for k in names])
    wmv = [tuple(A[pre + k].reshape(NATURAL[k]) for pre in ("", "m_", "v_")) for k in REPLICATED]
    outs = small_reduce_adamw(gathered, wmv)
    gsum = dict(zip(names, outs[:len(names)]))
    for a, k in enumerate(REPLICATED):
        res[k] = tuple(t.reshape(A[k].shape) for t in (gsum[k],) + tuple(outs[len(names) + 3 * a:len(names) + 3 * a + 3]))
    j = 4 * lax.axis_index("x") + 2 * lax.axis_index("y") + lax.axis_index("c")
    gsh = [lax.dynamic_slice_in_dim(gsum[k], j * 64, 64, axis=1).reshape(A[k].shape) for k in SHARDED_SMALL]
    outs = small_adamw(gsh, [(A[k], A["m_" + k], A["v_" + k]) for k in SHARDED_SMALL])
    for a, k in enumerate(SHARDED_SMALL):
        res[k] = (gsh[a],) + tuple(outs[3 * a:3 * a + 3])

    out = [loss, dx[None]]
    for part in range(4):
        out += [res[k][part] for k in WEIGHT_NAMES]
    return tuple(out)
```

```python
import functools
import math

import jax
import jax.numpy as jnp
from jax import lax
from jax.experimental import pallas as pl
from jax.experimental.pallas import tpu as pltpu

F32, BF16 = jnp.float32, jnp.bfloat16
ALPHA = 4.0 ** 0.25
LN_EPS = 1e-5
QK_SCALE = 0.125
POOL_WINDOWS = (2, 4, 8, 16)
CONV_TAPS = 31
N_DEV = 8
FF_SHARD, FF_PAD = 352, 384
ADAM_LR, ADAM_B1, ADAM_B2, ADAM_EPS, ADAM_WD, ADAM_STEP = 0.001, 0.9, 0.999, 1e-08, 0.01, 10
VMEM_LIMIT = 56 * 1024 * 1024
MESH_T = pl.DeviceIdType.MESH


def _cp(sem=None):
    return pltpu.CompilerParams(dimension_semantics=sem, vmem_limit_bytes=VMEM_LIMIT)


def _dot(a, b):
    return jnp.dot(a, b, preferred_element_type=F32)


def _dot_nt(a, b):
    return lax.dot_general(a, b, (((1,), (1,)), ((), ())), preferred_element_type=F32)


def _dot_tn(a, b):
    return lax.dot_general(a, b, (((0,), (0,)), ((), ())), preferred_element_type=F32)


def _sigmoid(x):
    return 1.0 / (1.0 + jnp.exp(-x))


def _softplus(z):
    return jnp.maximum(z, 0.0) + jnp.log(1.0 + jnp.exp(-jnp.abs(z)))


_GELU_C = math.sqrt(2.0 / math.pi)


def _gelu(x):
    return 0.5 * x * (1.0 + jnp.tanh(_GELU_C * (x + 0.044715 * x * x * x)))


def _gelu_grad(x):
    t = jnp.tanh(_GELU_C * (x + 0.044715 * x * x * x))
    return 0.5 * (1.0 + t) + 0.5 * x * (1.0 - t * t) * _GELU_C * (1.0 + 3.0 * 0.044715 * x * x)


def _ln_fwd(r, g, b):
    mu = jnp.mean(r, axis=-1, keepdims=True)
    xc = r - mu
    var = jnp.mean(xc * xc, axis=-1, keepdims=True)
    rstd = lax.rsqrt(var + LN_EPS)
    xh = xc * rstd
    return xh * g + b, xh, rstd


def _ln_bwd(dy, xh, rstd, g):
    dxh = dy * g
    m1 = jnp.mean(dxh, axis=-1, keepdims=True)
    m2 = jnp.mean(dxh * xh, axis=-1, keepdims=True)
    return rstd * (dxh - m1 - xh * m2)


def _split2(x):
    hi = x.astype(BF16)
    lo = (x - hi.astype(F32)).astype(BF16)
    return hi, lo


def _colsum(x):
    return jnp.sum(x, axis=0, keepdims=True)


def _tok_call(name, body, tiled, full, out_tiled, out_acc=(), tm=256, scratch=()):
    def arr(t):
        return t[0] if isinstance(t, tuple) else t
    S = arr(tiled[0]).shape[0]
    tm = min(tm, S)

    def tspec(t):
        if isinstance(t, tuple):
            _, w, cb = t
            return pl.BlockSpec((tm, w), lambda i, cb=cb: (i, cb))
        return pl.BlockSpec((tm, t.shape[1]), lambda i: (i, 0))

    def fspec(t):
        if isinstance(t, tuple):
            a, l = t
            nd = a.ndim - 1
            return pl.BlockSpec((None,) + a.shape[1:], lambda i, l=l, nd=nd: (l,) + (0,) * nd)
        nd = t.ndim
        return pl.BlockSpec(t.shape, lambda i, nd=nd: (0,) * nd)

    def ospec(o):
        return pl.BlockSpec((tm, o.shape[1]), lambda i: (i, 0))

    def aspec(o):
        nd = len(o.shape)
        return pl.BlockSpec(o.shape, lambda i, nd=nd: (0,) * nd)

    outs = pl.pallas_call(
        body, name=name, grid=(S // tm,),
        in_specs=[tspec(t) for t in tiled] + [fspec(t) for t in full],
        out_specs=[ospec(o) for o in out_tiled] + [aspec(o) for o in out_acc],
        out_shape=list(out_tiled) + list(out_acc),
        scratch_shapes=list(scratch),
        compiler_params=_cp(("arbitrary",)),
    )(*[arr(t) for t in tiled], *[arr(t) for t in full])
    return outs


def _sds(shape, dtype=F32):
    return jax.ShapeDtypeStruct(tuple(shape), dtype)


def _acc(ref, val):
    @pl.when(pl.program_id(0) == 0)
    def _():
        ref[...] = val

    @pl.when(pl.program_id(0) != 0)
    def _():
        ref[...] += val


def mm_in(x, w, nb16=0):
    S, N = x.shape[0], w.shape[1]

    def body(x_ref, w_ref, h_ref, xb_ref, *hb_ref):
        xb = x_ref[...].astype(BF16)
        xb_ref[...] = xb
        h = _dot(xb, w_ref[...])
        h_ref[...] = h
        if nb16:
            hb_ref[0][...] = h[:, 0:nb16].astype(BF16)

    outs = [_sds((S, N)), _sds((S, x.shape[1]), BF16)] + ([_sds((S, nb16), BF16)] if nb16 else [])
    return _tok_call("mm_in", body, [x], [w], outs, tm=512)


def _stack_heads(x, hm0, dtype=BF16):
    return jnp.concatenate([jnp.where(hm0, x, 0), jnp.where(hm0, 0, x)], axis=0).astype(dtype)


def _unstack_k(x, T):
    return jnp.concatenate([x[0:T], x[T:2 * T]], axis=1)


def _cumsum_mm(x, u):
    n = x.shape[0]
    hi, lo = _split2(x)
    r = _dot(jnp.concatenate([hi, lo], axis=0), u)
    return r[0:n] + r[n:2 * n]


def attn_fwd(qkv, T=256):
    S = qkv.shape[0]
    T = min(T, S)
    nq = S // T

    def body(q_ref, k_ref, v_ref, o_ref, t_ref, acc_ref, c_ref, qh_ref):
        i = pl.program_id(0)
        hm0 = lax.broadcasted_iota(jnp.int32, (1, 128), 1) < 64
        r2 = lax.broadcasted_iota(jnp.int32, (2 * T, T), 0)
        c2 = lax.broadcasted_iota(jnp.int32, (2 * T, T), 1)
        causal = c2 < jnp.where(r2 >= T, r2 - T, r2)
        ur = lax.broadcasted_iota(jnp.int32, (T, T), 0)
        uc = lax.broadcasted_iota(jnp.int32, (T, T), 1)
        u_incl = (ur >= uc).astype(BF16)
        acc_ref[...] = jnp.zeros_like(acc_ref)
        c_ref[...] = jnp.zeros_like(c_ref)
        for pp in range(4):
            qh_ref[pp] = _stack_heads(q_ref[:, pp * 128:(pp + 1) * 128] * QK_SCALE, hm0)

        def block(kb, diag):
            ks = pl.multiple_of(kb * T, T)
            for pp in range(4):
                cs = slice(pp * 128, (pp + 1) * 128)
                z = _dot_nt(qh_ref[pp], k_ref[pl.ds(ks, T), cs])
                sp = _softplus(z)
                lk = jnp.where(causal, -sp, 0.0) if diag else -sp
                incl = _cumsum_mm(lk, u_incl)
                c = c_ref[pp]
                w = jnp.exp((z - sp) + c + (incl - lk))
                if diag:
                    w = jnp.where(causal, w, 0.0)
                acc_ref[:, cs] += _dot(_unstack_k(w.astype(BF16), T), _stack_heads(v_ref[pl.ds(ks, T), cs], hm0))
                c_ref[pp] = c + jnp.broadcast_to(incl[:, 0:1], (2 * T, T))

        block(i, True)

        def step(jj, carry):
            block(i - 1 - jj, False)
            return carry

        lax.fori_loop(0, i, step, 0)
        o_ref[...] = acc_ref[...].astype(BF16)
        for pp in range(4):
            for hd in range(2):
                t_ref[2 * pp + hd] = c_ref[pp, hd * T:(hd + 1) * T, 0:128]

    return pl.pallas_call(
        body, name="attn_fwd", grid=(nq,),
        in_specs=[pl.BlockSpec((T, 512), lambda i: (i, 0)),
                  pl.BlockSpec((S, 512), lambda i: (0, 1)),
                  pl.BlockSpec((S, 512), lambda i: (0, 2))],
        out_specs=[pl.BlockSpec((T, 512), lambda i: (i, 0)),
                   pl.BlockSpec((8, T, 128), lambda i: (0, i, 0))],
        out_shape=[_sds((S, 512), BF16), _sds((8, S, 128))],
        scratch_shapes=[pltpu.VMEM((T, 512), F32), pltpu.VMEM((4, 2 * T, T), F32), pltpu.VMEM((4, 2 * T, 128), BF16)],
        compiler_params=_cp(("arbitrary",)),
    )(qkv, qkv, qkv)


def pool_fwd(h, pool_w, pool_scale, CH=256):
    S = h.shape[0]
    CH = min(CH, S)

    def body(u_ref, w_ref, sc_ref, b_ref, pooled_ref, pad_ref):
        pad_ref[0:16, :] = jnp.zeros((16, 512), F32)
        pad_ref[16:16 + S, :] = u_ref[...]
        for g, win in enumerate(POOL_WINDOWS):
            cs = slice(g * 128, (g + 1) * 128)
            wq = w_ref[g].astype(BF16)
            for ch in range(S // CH):
                base = ch * CH
                acc = pad_ref[16 + base:16 + base + CH, cs]
                for sft in range(1, win):
                    acc = acc + pad_ref[16 + base - sft:16 + base - sft + CH, cs]
                t = base + lax.broadcasted_iota(jnp.int32, (CH, 1), 0)
                cnt = jnp.minimum(t + 1, win).astype(F32)
                pooled = (acc / cnt - pad_ref[16 + base:16 + base + CH, cs]).astype(BF16)
                pooled_ref[base:base + CH, cs] = pooled
                b_ref[base:base + CH, cs] = (_dot(pooled, wq) * sc_ref[:, cs]).astype(BF16)

    return pl.pallas_call(
        body, name="pool_fwd", grid=(1,),
        in_specs=[pl.BlockSpec((S, 512), lambda i: (0, 3)),
                  pl.BlockSpec((4, 128, 128), lambda i: (0, 0, 0)),
                  pl.BlockSpec((1, 512), lambda i: (0, 0))],
        out_specs=[pl.BlockSpec((S, 512), lambda i: (0, 0)), pl.BlockSpec((S, 512), lambda i: (0, 0))],
        out_shape=[_sds((S, 512), BF16), _sds((S, 512), BF16)],
        scratch_shapes=[pltpu.VMEM((S + 16, 512), F32)],
        compiler_params=_cp(("arbitrary",)),
    )(h, pool_w, pool_scale)


def conv_fwd(h, dw, CH=128):
    S = h.shape[0]

    def body(a_ref, g_ref, dw_ref, y_ref, hc_ref, pad_ref):
        hc = a_ref[...] * _sigmoid(g_ref[...])
        hc_ref[...] = hc
        pad_ref[0:32, :] = jnp.zeros((32, 128), F32)
        pad_ref[32:32 + S, :] = hc
        for ch in range(S // CH):
            base = ch * CH + 2
            acc = dw_ref[0:1, :] * pad_ref[base:base + CH, :]
            for k in range(1, CONV_TAPS):
                acc = acc + dw_ref[k:k + 1, :] * pad_ref[base + k:base + k + CH, :]
            y_ref[ch * CH:(ch + 1) * CH, :] = acc

    return pl.pallas_call(
        body, name="conv_fwd", grid=(4,),
        in_specs=[pl.BlockSpec((S, 128), lambda c: (0, c)),
                  pl.BlockSpec((S, 128), lambda c: (0, 4 + c)),
                  pl.BlockSpec((CONV_TAPS, 128), lambda c: (0, c))],
        out_specs=[pl.BlockSpec((S, 128), lambda c: (0, c)), pl.BlockSpec((S, 128), lambda c: (0, c))],
        out_shape=[_sds((S, 512)), _sds((S, 512))],
        scratch_shapes=[pltpu.VMEM((S + 32, 128), F32)],
        compiler_params=_cp(("arbitrary",)),
    )(h, h, dw)


def _masked_sg_w(w_ref, g):
    row = lax.broadcasted_iota(jnp.int32, (128, 128), 0)
    col = lax.broadcasted_iota(jnp.int32, (128, 128), 1)
    return jnp.where(row >= col, w_ref[g], 0.0).astype(BF16)


def odd_post(y, h, cl_g, cl_b, sl_g, sl_b, sg_w, sgb_bc, tm=256):
    S = y.shape[0]
    tm = min(tm, S)

    def body(y_ref, zc_ref, clg, clb, slg, slb, w_ref, sb_ref,
             c_ref, d_ref, xhc_ref, rsc_ref, xhv_ref, rsv_ref, sv_ref):
        lnc, xhc, rsc = _ln_fwd(y_ref[...], clg[...], clb[...])
        c_ref[...] = (lnc * _sigmoid(lnc)).astype(BF16)
        xhc_ref[...] = xhc
        rsc_ref[...] = rsc
        z = _gelu(zc_ref[...])
        vn, xhv, rsv = _ln_fwd(z[:, 512:], slg[...], slb[...])
        xhv_ref[...] = xhv
        rsv_ref[...] = rsv
        vnb = vn.astype(BF16)
        for g in range(4):
            wm = _masked_sg_w(w_ref, g)
            for ch in range(tm // 128):
                rs, cs = slice(ch * 128, (ch + 1) * 128), slice(g * 128, (g + 1) * 128)
                sv_ref[rs, cs] = _dot(wm, vnb[rs, cs]) + sb_ref[g]
        d_ref[...] = (z[:, :512] * sv_ref[...]).astype(BF16)

    return _tok_call(
        "odd_post", body, [y, (h, 1024, 1)], [cl_g, cl_b, sl_g, sl_b, sg_w, sgb_bc],
        [_sds((S, 512), BF16), _sds((S, 512), BF16), _sds((S, 512)), _sds((S, 1)),
         _sds((S, 512)), _sds((S, 1)), _sds((S, 512))], tm=tm)


def mm_out_ln(l1, l2, x, w, g, b):
    S, D = x.shape

    def body(l1_ref, l2_ref, x_ref, w_ref, g_ref, b_ref, y_ref, xh_ref, rs_ref):
        mix = _dot(l1_ref[...], w_ref[0:512, :]) + _dot(l2_ref[...], w_ref[512:1024, :])
        y, xh, rs = _ln_fwd(ALPHA * x_ref[...] + mix, g_ref[...], b_ref[...])
        y_ref[...] = y
        xh_ref[...] = xh
        rs_ref[...] = rs

    return _tok_call("mm_out_ln", body, [l1, l2, x], [w, g, b],
                     [_sds((S, D)), _sds((S, D)), _sds((S, 1))])


def ffn_up(x1, wg, wu, layer):
    S, D = x1.shape
    F = wg.shape[2]

    def body(x_ref, wg_ref, wu_ref, gate_ref, up_ref, hb_ref, xb_ref):
        xb = x_ref[...].astype(BF16)
        xb_ref[...] = xb
        gate = _dot(xb, wg_ref[...])
        up = _dot(xb, wu_ref[...])
        gate_ref[...] = gate
        up_ref[...] = up
        hb_ref[...] = (gate * _sigmoid(gate) * up).astype(BF16)

    return _tok_call("ffn_up", body, [x1], [(wg, layer), (wu, layer)],
                     [_sds((S, F)), _sds((S, F)), _sds((S, F), BF16), _sds((S, D), BF16)])


def ffn_down_ln(hb, x1, wd, layer, g, b):
    S, D = x1.shape

    def body(h_ref, x_ref, w_ref, g_ref, b_ref, y_ref, xh_ref, rs_ref):
        f = _dot(h_ref[...], w_ref[...])
        y, xh, rs = _ln_fwd(ALPHA * x_ref[...] + f, g_ref[...], b_ref[...])
        y_ref[...] = y
        xh_ref[...] = xh
        rs_ref[...] = rs

    return _tok_call("ffn_down_ln", body, [hb, x1], [(wd, layer), g, b],
                     [_sds((S, D)), _sds((S, D)), _sds((S, 1))])


def ple_fwd(x2, p, wpg, wpp, layer, bg, target=None):
    S, D = x2.shape
    last = target is not None

    def body(*refs):
        if last:
            x_ref, p_ref, t_ref, wg_ref, wp_ref, b_ref, x3_ref, sg_ref, pp_ref, xb_ref, pb_ref, dy_ref, ls_ref = refs
        else:
            x_ref, p_ref, wg_ref, wp_ref, b_ref, x3_ref, sg_ref, pp_ref, xb_ref, pb_ref = refs
        x = x_ref[...]
        xb = x.astype(BF16)
        pb = p_ref[...].astype(BF16)
        xb_ref[...] = xb
        pb_ref[...] = pb
        sg = _sigmoid(_dot(xb, wg_ref[...]) + b_ref[...])
        pp = _dot(pb, wp_ref[...])
        sg_ref[...] = sg
        pp_ref[...] = pp
        x3 = x + sg * pp
        x3_ref[...] = x3
        if last:
            err = x3 - t_ref[...]
            dy_ref[...] = err * (1.0 / D)
            _acc(ls_ref, _colsum(err * err))

    outs = [_sds((S, D)), _sds((S, D)), _sds((S, D)), _sds((S, D), BF16), _sds((S, p.shape[1]), BF16)]
    tiled = [x2, p] + ([target] if last else [])
    if last:
        outs.append(_sds((S, D)))
    return _tok_call("ple_fwd", body, tiled, [(wpg, layer), (wpp, layer), bg], outs,
                     [_sds((1, D))] if last else [])


def ple_bwd(dx3, sg, pp, wpg, layer):
    S, D = dx3.shape

    def body(d_ref, sg_ref, pp_ref, w_ref, dx_ref, dgp_ref, dpp_ref, dbg_ref):
        d, sg = d_ref[...], sg_ref[...]
        dgp = d * pp_ref[...] * sg * (1.0 - sg)
        dgpb = dgp.astype(BF16)
        dgp_ref[...] = dgpb
        dpp_ref[...] = (d * sg).astype(BF16)
        dx_ref[...] = d + _dot_nt(dgpb, w_ref[...])
        _acc(dbg_ref, _colsum(dgp))

    return _tok_call("ple_bwd", body, [dx3, sg, pp], [(wpg, layer)],
                     [_sds((S, D)), _sds((S, D), BF16), _sds((S, D), BF16)], [_sds((1, D))])


def ffn_bwd_a(dx2, xh, rs, g, gate, up, wd, layer):
    S, D = dx2.shape
    F = gate.shape[1]

    def body(d_ref, xh_ref, rs_ref, gate_ref, up_ref, g_ref, w_ref,
             dr_ref, drb_ref, dg_ref, du_ref, dlg_ref, dlb_ref):
        d, xh = d_ref[...], xh_ref[...]
        dr = _ln_bwd(d, xh, rs_ref[...], g_ref[...])
        drb = dr.astype(BF16)
        dr_ref[...] = dr
        drb_ref[...] = drb
        _acc(dlg_ref, _colsum(d * xh))
        _acc(dlb_ref, _colsum(d))
        dh = _dot_nt(drb, w_ref[...])
        gate, up = gate_ref[...], up_ref[...]
        s = _sigmoid(gate)
        dg_ref[...] = (dh * up * s * (1.0 + gate * (1.0 - s))).astype(BF16)
        du_ref[...] = (dh * gate * s).astype(BF16)

    return _tok_call("ffn_bwd_a", body, [dx2, xh, rs, gate, up], [g, (wd, layer)],
                     [_sds((S, D)), _sds((S, D), BF16), _sds((S, F), BF16), _sds((S, F), BF16)],
                     [_sds((1, D)), _sds((1, D))])


def ffn_bwd_b(dr, dgate_b, dup_b, wg, wu, layer):
    S, D = dr.shape

    def body(dr_ref, dg_ref, du_ref, wg_ref, wu_ref, dx_ref):
        dx_ref[...] = (ALPHA * dr_ref[...] + _dot_nt(dg_ref[...], wg_ref[...])
                       + _dot_nt(du_ref[...], wu_ref[...]))

    return _tok_call("ffn_bwd_b", body, [dr, dgate_b, dup_b], [(wg, layer), (wu, layer)], [_sds((S, D))])[0]


def mix_bwd(dx1, xh, rs, g, w):
    S, D = dx1.shape

    def body(d_ref, xh_ref, rs_ref, g_ref, w_ref, dr_ref, dmb_ref, dl_ref, dlg_ref, dlb_ref):
        d, xh = d_ref[...], xh_ref[...]
        dr = _ln_bwd(d, xh, rs_ref[...], g_ref[...])
        drb = dr.astype(BF16)
        dr_ref[...] = dr
        dmb_ref[...] = drb
        dl_ref[...] = _dot_nt(drb, w_ref[...])
        _acc(dlg_ref, _colsum(d * xh))
        _acc(dlb_ref, _colsum(d))

    return _tok_call("mix_bwd", body, [dx1, xh, rs], [g, w],
                     [_sds((S, D)), _sds((S, D), BF16), _sds((S, D))], [_sds((1, D)), _sds((1, D))])


def dx_in(dr, pieces, w):
    S, D = dr.shape
    offs = [o for _, o in pieces]
    widths = [a.shape[1] for a, _ in pieces]

    def body(*refs):
        dr_ref, prefs, w_ref, dx_ref = refs[0], refs[1:1 + len(pieces)], refs[-2], refs[-1]
        acc = ALPHA * dr_ref[...]
        for pr, o, n in zip(prefs, offs, widths):
            acc = acc + _dot_nt(pr[...], w_ref[:, o:o + n])
        dx_ref[...] = acc

    return _tok_call("dx_in", body, [dr] + [a for a, _ in pieces], [w], [_sds((S, D))])[0]


def odd_post_bwd(dl, h, xhc, rsc, xhv, rsv, sv, cl_g, cl_b, sl_g, sl_b, sg_w, tm=256):
    S = dl.shape[0]
    tm = min(tm, S)

    def body(dl_ref, zc_ref, xhc_ref, rsc_ref, xhv_ref, rsv_ref, sv_ref, clg, clb, slg, slb, w_ref,
             dy_ref, dzc_ref, dclg_ref, dclb_ref, dslg_ref, dslb_ref, dwm_ref, dsb_ref, dvn_ref):
        first = pl.program_id(0) == 0
        last = pl.program_id(0) == pl.num_programs(0) - 1
        dc, dd = dl_ref[:, 0:512], dl_ref[:, 512:1024]
        xhc = xhc_ref[...]
        lnc = xhc * clg[...] + clb[...]
        s = _sigmoid(lnc)
        dlnc = dc * s * (1.0 + lnc * (1.0 - s))
        dy_ref[...] = _ln_bwd(dlnc, xhc, rsc_ref[...], clg[...])
        _acc(dclg_ref, _colsum(dlnc * xhc))
        _acc(dclb_ref, _colsum(dlnc))
        zc = zc_ref[...]
        z = _gelu(zc)
        dsv = dd * z[:, :512]
        dsvb = dsv.astype(BF16)
        xhv = xhv_ref[...]
        vnb = (xhv * slg[...] + slb[...]).astype(BF16)

        @pl.when(first)
        def _():
            dwm_ref[...] = jnp.zeros_like(dwm_ref)
            dsb_ref[...] = jnp.zeros_like(dsb_ref)

        for g in range(4):
            wm = _masked_sg_w(w_ref, g)
            for ch in range(tm // 128):
                rs_, cs = slice(ch * 128, (ch + 1) * 128), slice(g * 128, (g + 1) * 128)
                dwm_ref[g] += _dot_nt(dsvb[rs_, cs], vnb[rs_, cs])
                dvn_ref[rs_, cs] = _dot_tn(wm, dsvb[rs_, cs])
                dsb_ref[g] += dsv[rs_, cs]
        dvn = dvn_ref[...]
        dvv = _ln_bwd(dvn, xhv, rsv_ref[...], slg[...])
        _acc(dslg_ref, _colsum(dvn * xhv))
        _acc(dslb_ref, _colsum(dvn))
        gg = _gelu_grad(zc)
        dzc_ref[:, 0:512] = (dd * sv_ref[...] * gg[:, :512]).astype(BF16)
        dzc_ref[:, 512:1024] = (dvv * gg[:, 512:]).astype(BF16)

        @pl.when(last)
        def _():
            row = lax.broadcasted_iota(jnp.int32, (128, 128), 0)
            col = lax.broadcasted_iota(jnp.int32, (128, 128), 1)
            for g in range(4):
                dwm_ref[g] = jnp.where(row >= col, dwm_ref[g], 0.0)
                dsb_ref[g] = jnp.broadcast_to(jnp.sum(dsb_ref[g], axis=1, keepdims=True), (128, 128))

    return _tok_call(
        "odd_post_bwd", body, [dl, (h, 1024, 1), xhc, rsc, xhv, rsv, sv], [cl_g, cl_b, sl_g, sl_b, sg_w],
        [_sds((S, 512)), _sds((S, 1024), BF16)],
        [_sds((1, 512)), _sds((1, 512)), _sds((1, 512)), _sds((1, 512)), _sds((4, 128, 128)), _sds((4, 128, 128))],
        tm=tm, scratch=[pltpu.VMEM((tm, 512), F32)])


def conv_bwd(dy, hc, h, dw, CH=128):
    S = dy.shape[0]

    def body(dy_ref, hc_ref, a_ref, g_ref, dw_ref, da_ref, dg_ref, ddw_ref, padh_ref, padd_ref, dhc_ref):
        padh_ref[0:32, :] = jnp.zeros((32, 128), F32)
        padh_ref[32:32 + S, :] = hc_ref[...]
        padd_ref[0:S, :] = dy_ref[...]
        padd_ref[S:S + 32, :] = jnp.zeros((32, 128), F32)
        taps = [jnp.zeros((1, 128), F32) for _ in range(CONV_TAPS)]
        for ch in range(S // CH):
            b0 = ch * CH
            dyc = padd_ref[b0:b0 + CH, :]
            acc = dw_ref[0:1, :] * padd_ref[b0 + 30:b0 + 30 + CH, :]
            taps[0] = taps[0] + _colsum(dyc * padh_ref[b0 + 2:b0 + 2 + CH, :])
            for k in range(1, CONV_TAPS):
                acc = acc + dw_ref[k:k + 1, :] * padd_ref[b0 + 30 - k:b0 + 30 - k + CH, :]
                taps[k] = taps[k] + _colsum(dyc * padh_ref[b0 + 2 + k:b0 + 2 + k + CH, :])
            dhc_ref[b0:b0 + CH, :] = acc
        for k in range(CONV_TAPS):
            ddw_ref[k:k + 1, :] = taps[k]
        dhc = dhc_ref[...]
        s = _sigmoid(g_ref[...])
        da_ref[...] = (dhc * s).astype(BF16)
        dg_ref[...] = (dhc * a_ref[...] * s * (1.0 - s)).astype(BF16)

    return pl.pallas_call(
        body, name="conv_bwd", grid=(4,),
        in_specs=[pl.BlockSpec((S, 128), lambda c: (0, c)),
                  pl.BlockSpec((S, 128), lambda c: (0, c)),
                  pl.BlockSpec((S, 128), lambda c: (0, c)),
                  pl.BlockSpec((S, 128), lambda c: (0, 4 + c)),
                  pl.BlockSpec((CONV_TAPS, 128), lambda c: (0, c))],
        out_specs=[pl.BlockSpec((S, 128), lambda c: (0, c)), pl.BlockSpec((S, 128), lambda c: (0, c)),
                   pl.BlockSpec((CONV_TAPS, 128), lambda c: (0, c))],
        out_shape=[_sds((S, 512), BF16), _sds((S, 512), BF16), _sds((CONV_TAPS, 512))],
        scratch_shapes=[pltpu.VMEM((S + 32, 128), F32), pltpu.VMEM((S + 32, 128), F32), pltpu.VMEM((S, 128), F32)],
        compiler_params=_cp(("arbitrary",)),
    )(dy, hc, h, h, dw)


def attn_bwd(qkv, dl, tb, T=256):
    S = qkv.shape[0]
    T = min(T, S)
    nq = S // T

    def body(q_ref, k_ref, v_ref, do_ref, t_ref, dq_ref, dk_ref, dv_ref,
             dka_ref, dva_ref, dqa_ref, pc_ref, gc_ref, tot_ref, qh_ref, doh_ref):
        i = pl.program_id(0)
        hm0 = lax.broadcasted_iota(jnp.int32, (1, 128), 1) < 64
        r2 = lax.broadcasted_iota(jnp.int32, (2 * T, T), 0)
        c2 = lax.broadcasted_iota(jnp.int32, (2 * T, T), 1)
        causal = c2 < jnp.where(r2 >= T, r2 - T, r2)
        ur = lax.broadcasted_iota(jnp.int32, (T, T), 0)
        uc = lax.broadcasted_iota(jnp.int32, (T, T), 1)
        u_le = (ur <= uc).astype(BF16)
        u_lt = (ur < uc).astype(BF16)

        @pl.when(i == 0)
        def _():
            dka_ref[...] = jnp.zeros_like(dka_ref)
            dva_ref[...] = jnp.zeros_like(dva_ref)

        dqa_ref[...] = jnp.zeros_like(dqa_ref)
        pc_ref[...] = jnp.zeros_like(pc_ref)
        gc_ref[...] = jnp.zeros_like(gc_ref)
        for pp in range(4):
            cs = slice(pp * 128, (pp + 1) * 128)
            qh_ref[pp] = _stack_heads(q_ref[:, cs] * QK_SCALE, hm0)
            doh_ref[pp] = _stack_heads(do_ref[:, cs], hm0)
            for hd in range(2):
                for half in range(T // 128):
                    tot_ref[pp, hd * T:(hd + 1) * T, half * 128:(half + 1) * 128] = t_ref[2 * pp + hd]

        def block(kb, diag):
            ks = pl.multiple_of(kb * T, T)
            for pp in range(4):
                cs = slice(pp * 128, (pp + 1) * 128)
                kb16 = k_ref[pl.ds(ks, T), cs]
                vb16 = v_ref[pl.ds(ks, T), cs]
                qh, doh = qh_ref[pp], doh_ref[pp]
                z = _dot_nt(qh, kb16)
                sp = _softplus(z)
                lk = jnp.where(causal, -sp, 0.0) if diag else -sp
                pre = _cumsum_mm(lk, u_le)
                sig = jnp.exp(z - sp)
                pc = pc_ref[pp]
                w = sig * jnp.exp(tot_ref[pp] - pc - pre)
                if diag:
                    w = jnp.where(causal, w, 0.0)
                gmat = _dot_nt(doh, vb16) * w
                gex = gc_ref[pp] + _cumsum_mm(gmat, u_lt)
                dz = gmat * (1.0 - sig) - sig * gex
                if diag:
                    dz = jnp.where(causal, dz, 0.0)
                dzb = dz.astype(BF16)
                dqa_ref[:, cs] += _dot(_unstack_k(dzb, T), _stack_heads(kb16, hm0))
                dka_ref[pl.ds(ks, T), cs] += _dot_tn(dzb, qh)
                dva_ref[pl.ds(ks, T), cs] += _dot_tn(w.astype(BF16), doh)
                pc_ref[pp] = pc + jnp.broadcast_to(pre[:, T - 1:T], (2 * T, T))
                gc_ref[pp] = jnp.broadcast_to(gex[:, T - 1:T] + gmat[:, T - 1:T], (2 * T, T))

        def step(kb, carry):
            block(kb, False)
            return carry

        lax.fori_loop(0, i, step, 0)
        block(i, True)
        dq_ref[...] = (dqa_ref[...] * QK_SCALE).astype(BF16)

        @pl.when(i == nq - 1)
        def _():
            dk_ref[...] = dka_ref[...].astype(BF16)
            dv_ref[...] = dva_ref[...].astype(BF16)

    return pl.pallas_call(
        body, name="attn_bwd", grid=(nq,),
        in_specs=[pl.BlockSpec((T, 512), lambda i: (i, 0)),
                  pl.BlockSpec((S, 512), lambda i: (0, 1)),
                  pl.BlockSpec((S, 512), lambda i: (0, 2)),
                  pl.BlockSpec((T, 512), lambda i: (i, 0)),
                  pl.BlockSpec((8, T, 128), lambda i: (0, i, 0))],
        out_specs=[pl.BlockSpec((T, 512), lambda i: (i, 0)),
                   pl.BlockSpec((S, 512), lambda i: (0, 0)),
                   pl.BlockSpec((S, 512), lambda i: (0, 0))],
        out_shape=[_sds((S, 512), BF16), _sds((S, 512), BF16), _sds((S, 512), BF16)],
        scratch_shapes=[pltpu.VMEM((S, 512), F32), pltpu.VMEM((S, 512), F32), pltpu.VMEM((T, 512), F32),
                        pltpu.VMEM((4, 2 * T, T), F32), pltpu.VMEM((4, 2 * T, T), F32), pltpu.VMEM((4, 2 * T, T), F32),
                        pltpu.VMEM((4, 2 * T, 128), BF16), pltpu.VMEM((4, 2 * T, 128), BF16)],
        compiler_params=_cp(("arbitrary",)),
    )(qkv, qkv, qkv, dl, tb)


def pool_bwd(dl, pooled_b, pool_w, pool_scale, CH=256):
    S = dl.shape[0]
    CH = min(CH, S)

    def body(db_ref, pooled_ref, w_ref, sc_ref, du_ref, dw_ref, dsc_ref, pad_ref, dp_ref):
        pad_ref[S:S + 16, :] = jnp.zeros((16, 128), F32)
        for g, win in enumerate(POOL_WINDOWS):
            cs = slice(g * 128, (g + 1) * 128)
            wq = w_ref[g].astype(BF16)
            dwg = jnp.zeros((128, 128), F32)
            dsc = jnp.zeros((1, 128), F32)
            for ch in range(S // CH):
                rs_ = slice(ch * CH, (ch + 1) * CH)
                db = db_ref[rs_, cs]
                pb = pooled_ref[rs_, cs]
                dsc = dsc + _colsum(db * _dot(pb, wq))
                dmsb = (db * sc_ref[:, cs]).astype(BF16)
                dwg = dwg + _dot_tn(pb, dmsb)
                dpool = _dot_nt(dmsb, wq)
                t = ch * CH + lax.broadcasted_iota(jnp.int32, (CH, 1), 0)
                cnt = jnp.minimum(t + 1, win).astype(F32)
                dp_ref[rs_, :] = dpool
                pad_ref[rs_, :] = dpool / cnt
            dw_ref[g] = dwg
            dsc_ref[:, cs] = dsc
            for ch in range(S // CH):
                base = ch * CH
                acc = pad_ref[base:base + CH, :]
                for sft in range(1, win):
                    acc = acc + pad_ref[base + sft:base + sft + CH, :]
                du_ref[base:base + CH, cs] = (acc - dp_ref[base:base + CH, :]).astype(BF16)

    return pl.pallas_call(
        body, name="pool_bwd", grid=(1,),
        in_specs=[pl.BlockSpec((S, 512), lambda i: (0, 1)),
                  pl.BlockSpec((S, 512), lambda i: (0, 0)),
                  pl.BlockSpec((4, 128, 128), lambda i: (0, 0, 0)),
                  pl.BlockSpec((1, 512), lambda i: (0, 0))],
        out_specs=[pl.BlockSpec((S, 512), lambda i: (0, 0)),
                   pl.BlockSpec((4, 128, 128), lambda i: (0, 0, 0)),
                   pl.BlockSpec((1, 512), lambda i: (0, 0))],
        out_shape=[_sds((S, 512), BF16), _sds((4, 128, 128)), _sds((1, 512))],
        scratch_shapes=[pltpu.VMEM((S + 16, 128), F32), pltpu.VMEM((S, 128), F32)],
        compiler_params=_cp(("arbitrary",)),
    )(dl, pooled_b, pool_w, pool_scale)


def tn_into(a, b, out, out_b, lead, r0, c0, tk=512, tn=512):
    S, K = a.shape
    N = b.shape[1]
    tk, tn = min(tk, K), min(tn, N)
    assert K % tk == 0 and N % tn == 0 and r0 % tk == 0 and c0 % tn == 0
    rb, cb = r0 // tk, c0 // tn
    fresh = isinstance(out, jax.ShapeDtypeStruct)

    def body(*refs):
        a_ref, b_ref, o_ref, ob_ref = refs[0], refs[1], refs[-2], refs[-1]
        r = _dot_tn(a_ref[...], b_ref[...])
        o_ref[...] = r
        ob_ref[...] = r.astype(BF16)

    ospec = pl.BlockSpec((None, tk, tn), lambda i, j: (lead, rb + i, cb + j))
    in_specs = [pl.BlockSpec((S, tk), lambda i, j: (0, i)), pl.BlockSpec((S, tn), lambda i, j: (0, j))]
    args = [a, b]
    aliases = {}
    if not fresh:
        in_specs += [pl.BlockSpec(memory_space=pl.ANY), pl.BlockSpec(memory_space=pl.ANY)]
        args += [out, out_b]
        aliases = {2: 0, 3: 1}
    shp = out.shape
    return pl.pallas_call(
        body, name="tn_grad", grid=(K // tk, N // tn),
        in_specs=in_specs, out_specs=[ospec, ospec],
        out_shape=[_sds(shp, F32), _sds(shp, BF16)],
        input_output_aliases=aliases,
        compiler_params=_cp(("arbitrary", "arbitrary")),
    )(*args)


def _row(a, i):
    return a[i:i + 1]


def _grad_slabs(shapes):
    return {k: (_sds(s, F32), _sds(s, BF16)) for k, s in shapes.items()}


def local_step(x, p, target, W, P):
    S, D = x.shape
    FP = W["ffn_w_gate"].shape[2]
    big = _grad_slabs({
        "even_w_in": (1, D, 2048), "odd_w_in": (1, D, 2048), "even_w_out": (1, 1024, D), "odd_w_out": (1, 1024, D),
        "ffn_w_gate": (2, D, FP), "ffn_w_up": (2, D, FP), "ffn_w_down": (2, FP, D),
        "ple_w_proj": (2, p.shape[2], D), "ple_w_gate": (2, D, D)})
    small = {}

    def tn(name, a, b, lead=0, r0=0, c0=0):
        big[name] = tuple(tn_into(a, b, big[name][0], big[name][1], lead, r0, c0))

    sgb_bc = jnp.broadcast_to(P["sg_b"][:, :, None], (4, 128, 128))

    saved = []
    xin = x
    for i in range(2):
        s = {}
        if i == 0:
            s["h"], s["xb"], s["qkv"] = mm_in(xin, W["even_w_in"], nb16=1536)
            s["l1"], s["tb"] = attn_fwd(s["qkv"])
            s["l2"], s["pooled"] = pool_fwd(s["h"], P["pool_w"], P["pool_scale"])
            wout = W["even_w_out"]
        else:
            s["h"], s["xb"] = mm_in(xin, W["odd_w_in"])
            s["y"], s["hc"] = conv_fwd(s["h"], P["conv_dw"])
            (s["l1"], s["l2"], s["xhc"], s["rsc"], s["xhv"], s["rsv"], s["sv"]) = odd_post(
                s["y"], s["h"], P["conv_ln_g"], P["conv_ln_b"], P["sg_ln_g"], P["sg_ln_b"], P["sg_w"], sgb_bc)
            wout = W["odd_w_out"]
        x1, s["xh1"], s["rs1"] = mm_out_ln(s["l1"], s["l2"], xin, wout, _row(P["ln_mix_g"], i), _row(P["ln_mix_b"], i))
        s["gate"], s["up"], s["hb"], s["x1b"] = ffn_up(x1, W["ffn_w_gate"], W["ffn_w_up"], i)
        x2, s["xh2"], s["rs2"] = ffn_down_ln(s["hb"], x1, W["ffn_w_down"], i, _row(P["ln_ffn_g"], i), _row(P["ln_ffn_b"], i))
        outs = ple_fwd(x2, p[i], W["ple_w_gate"], W["ple_w_proj"], i, _row(P["ple_b_gate"], i),
                       target if i == 1 else None)
        xin, s["sg"], s["pp"], s["x2b"], s["pb"] = outs[:5]
        if i == 1:
            dx, sq = outs[5], outs[6]
        saved.append(s)

    lng = {k: [None, None] for k in ("ln_mix_g", "ln_mix_b", "ln_ffn_g", "ln_ffn_b", "ple_b_gate")}
    for i in (1, 0):
        s = saved[i]
        dx2, dgp_b, dpp_b, lng["ple_b_gate"][i] = ple_bwd(dx, s["sg"], s["pp"], W["ple_w_gate"], i)
        tn("ple_w_gate", s["x2b"], dgp_b, lead=i)
        tn("ple_w_proj", s["pb"], dpp_b, lead=i)
        dr2, dr2_b, dgate_b, dup_b, lng["ln_ffn_g"][i], lng["ln_ffn_b"][i] = ffn_bwd_a(
            dx2, s["xh2"], s["rs2"], _row(P["ln_ffn_g"], i), s["gate"], s["up"], W["ffn_w_down"], i)
        tn("ffn_w_down", s["hb"], dr2_b, lead=i)
        tn("ffn_w_gate", s["x1b"], dgate_b, lead=i)
        tn("ffn_w_up", s["x1b"], dup_b, lead=i)
        dx1 = ffn_bwd_b(dr2, dgate_b, dup_b, W["ffn_w_gate"], W["ffn_w_up"], i)
        wout = W["odd_w_out"] if i == 1 else W["even_w_out"]
        dr1, dmix_b, dl, lng["ln_mix_g"][i], lng["ln_mix_b"][i] = mix_bwd(
            dx1, s["xh1"], s["rs1"], _row(P["ln_mix_g"], i), wout)
        oname, iname = ("odd_w_out", "odd_w_in") if i == 1 else ("even_w_out", "even_w_in")
        tn(oname, s["l1"], dmix_b, r0=0)
        tn(oname, s["l2"], dmix_b, r0=512)
        if i == 1:
            (dy, dzc_b, small["conv_ln_g"], small["conv_ln_b"], small["sg_ln_g"], small["sg_ln_b"],
             small["sg_w"], dsb) = odd_post_bwd(dl, s["h"], s["xhc"], s["rsc"], s["xhv"], s["rsv"], s["sv"],
                                                P["conv_ln_g"], P["conv_ln_b"], P["sg_ln_g"], P["sg_ln_b"], P["sg_w"])
            small["sg_b"] = dsb[:, :, 0]
            da_b, dg_b, small["conv_dw"] = conv_bwd(dy, s["hc"], s["h"], P["conv_dw"])
            pieces = [(da_b, 0), (dg_b, 512), (dzc_b, 1024)]
            win = W["odd_w_in"]
        else:
            dq_b, dk_b, dv_b = attn_bwd(s["qkv"], dl, s["tb"])
            du_b, small["pool_w"], small["pool_scale"] = pool_bwd(dl, s["pooled"], P["pool_w"], P["pool_scale"])
            pieces = [(dq_b, 0), (dk_b, 512), (dv_b, 1024), (du_b, 1536)]
            win = W["even_w_in"]
        for a, off in pieces:
            tn(iname, s["xb"], a, c0=off)
        dx = dx_in(dr1, pieces, win)
    for k, v in lng.items():
        small[k] = jnp.concatenate(v, axis=0)
    return sq, dx, big, small


BIG = {
    "even_w_in": ((1, 1024, 2048), 2, 256, 256),
    "even_w_out": ((1, 1024, 1024), 1, 128, 128),
    "odd_w_in": ((1, 1024, 2048), 2, 256, 256),
    "odd_w_out": ((1, 1024, 1024), 1, 128, 128),
    "ffn_w_gate": ((2, 1024, 8 * FF_PAD), 2, FF_PAD, FF_SHARD),
    "ffn_w_up": ((2, 1024, 8 * FF_PAD), 2, FF_PAD, FF_SHARD),
    "ffn_w_down": ((2, 8 * FF_PAD, 1024), 1, FF_PAD, FF_SHARD),
    "ple_w_proj": ((2, 256, 1024), 2, 128, 128),
    "ple_w_gate": ((2, 1024, 1024), 1, 128, 128),
}
BIG_ORDER = ("even_w_in", "even_w_out", "ffn_w_gate", "ffn_w_up", "ffn_w_down", "ple_w_gate", "ple_w_proj",
             "odd_w_in", "odd_w_out")
ANY = pl.BlockSpec(memory_space=pl.ANY)


def _win_shape(spec):
    full, axis, w, _ = spec
    return tuple(w if d == axis else n for d, n in enumerate(full))


def _window(ref, axis, w, j):
    idx = [slice(None)] * len(ref.shape)
    idx[axis] = pl.ds(j, 1) if w == 1 else pl.ds(pl.multiple_of(j * w, w), w)
    return ref.at[tuple(idx)]


def _mesh_pos():
    return lax.axis_index("x"), lax.axis_index("y"), lax.axis_index("c")


def all_gather(name, blocks, specs, in_place=False):
    n = len(blocks)

    def body(*refs):
        blk, full = refs[:n], refs[n:2 * n]
        send, recv, loc = refs[2 * n:]
        x, y, c = _mesh_pos()
        me, sib = (x, y, c), (x, y, 1 - c)
        chips = [(1 - x, y), (x, 1 - y), (1 - x, 1 - y)]

        def win(a, dev):
            return _window(full[a], specs[a][1], specs[a][2], 4 * dev[0] + 2 * dev[1] + dev[2])

        def cp(a, k, block_dev, to, src=None):
            return pltpu.make_async_remote_copy(
                src_ref=win(a, block_dev) if src is None else src, dst_ref=win(a, block_dev),
                send_sem=send.at[a, k], recv_sem=recv.at[a, k], device_id=to, device_id_type=MESH_T)

        mine = [] if in_place else [pltpu.make_async_copy(blk[a], win(a, me), loc.at[a]) for a in range(n)]
        for m in mine:
            m.start()
        first, passed = [], []
        for a in range(n):
            own = None if in_place else blk[a]
            f = [cp(a, 0, me, sib, src=own)]
            f += [cp(a, 1 + j, me, (*ch, c), src=own) for j, ch in enumerate(chips)]
            for d in f:
                d.start()
            first += f
        for a in range(n):
            for j, ch in enumerate(chips):
                cp(a, 1 + j, (*ch, c), me).wait_recv()
                fw = cp(a, 4 + j, (*ch, c), sib)
                fw.start()
                passed.append(fw)
        for a in range(n):
            cp(a, 0, sib, me).wait_recv()
            for j, ch in enumerate(chips):
                cp(a, 4 + j, (*ch, 1 - c), me).wait_recv()
        for d in first + passed:
            d.wait_send()
        for m in mine:
            m.wait()

    return pl.pallas_call(
        body, name=name,
        in_specs=[ANY] * n, out_specs=[ANY] * n,
        out_shape=[_sds(s[0], b.dtype) for s, b in zip(specs, blocks)],
        input_output_aliases={a: a for a in range(n)} if in_place else {},
        scratch_shapes=[pltpu.SemaphoreType.DMA((n, 7)), pltpu.SemaphoreType.DMA((n, 7)),
                        pltpu.SemaphoreType.DMA((n,))],
    )(*blocks)


def rs_sibling(grads_b, specs):
    n = len(grads_b)

    def body(*refs):
        gb, land = refs[:n], refs[n:2 * n]
        send, recv = refs[2 * n:]
        x, y, c = _mesh_pos()
        copies = []
        for a in range(n):
            _, axis, w, _ = specs[a]
            for q in range(4):
                rc = pltpu.make_async_remote_copy(
                    src_ref=_window(gb[a], axis, w, 2 * q + (1 - c)), dst_ref=land[a].at[q],
                    send_sem=send.at[a, q], recv_sem=recv.at[a, q], device_id=(x, y, 1 - c), device_id_type=MESH_T)
                rc.start()
                copies.append(rc)
        for rc in copies:
            rc.wait()

    return pl.pallas_call(
        body, name="rs_sibling",
        in_specs=[ANY] * n, out_specs=[ANY] * n,
        out_shape=[_sds((4,) + _win_shape(s), BF16) for s in specs],
        scratch_shapes=[pltpu.SemaphoreType.DMA((n, 4)), pltpu.SemaphoreType.DMA((n, 4))],
    )(*grads_b)


def rs_chips(s1b):
    n = len(s1b)

    def body(*refs):
        fb, land = refs[:n], refs[n:2 * n]
        send, recv = refs[2 * n:]
        x, y, c = _mesh_pos()
        q = 2 * x + y
        copies = []
        for a in range(n):
            for d in (1, 2, 3):
                qd = lax.rem(q + d, 4)
                rc = pltpu.make_async_remote_copy(
                    src_ref=fb[a].at[qd], dst_ref=land[a].at[3 - d],
                    send_sem=send.at[a, d - 1], recv_sem=recv.at[a, 3 - d],
                    device_id=(lax.div(qd, 2), lax.rem(qd, 2), c), device_id_type=MESH_T)
                rc.start()
                copies.append(rc)
        for rc in copies:
            rc.wait()

    return pl.pallas_call(
        body, name="rs_chips",
        in_specs=[ANY] * n, out_specs=[ANY] * n,
        out_shape=[_sds((3,) + a.shape[1:], BF16) for a in s1b],
        scratch_shapes=[pltpu.SemaphoreType.DMA((n, 3)), pltpu.SemaphoreType.DMA((n, 3))],
    )(*s1b)


def pack_weights(shards, small_blk, specs, j_arr):
    n = len(shards)

    def body(j_ref, *refs):
        for a in range(n):
            src, dst = refs[a], refs[n + 1 + a]
            _, axis, w, valid = specs[a]
            if valid == w:
                dst[...] = src[...].astype(BF16)
            else:
                dst[...] = jnp.zeros(dst.shape, BF16)
                if axis == 2:
                    dst[:, :, 0:valid] = src[...].astype(BF16)
                else:
                    dst[:, 0:valid, :] = src[...].astype(BF16)
        refs[2 * n + 1][...] = refs[n][...]

    def ispec(a):
        return pl.BlockSpec(a.shape, lambda i, j_ref: (0, 0, 0))

    def ospec(spec):
        axis = spec[1]
        return pl.BlockSpec(_win_shape(spec), lambda i, j_ref, axis=axis: tuple(j_ref[0] if d == axis else 0 for d in range(3)))

    args = list(shards) + [small_blk]
    return pl.pallas_call(
        body, name="pack_weights",
        grid_spec=pltpu.PrefetchScalarGridSpec(
            num_scalar_prefetch=1, grid=(1,),
            in_specs=[ispec(a) for a in args], out_specs=[ospec(s) for s in specs]),
        out_shape=[_sds(s[0], BF16) for s in specs[:n]] + [_sds(specs[n][0], F32)],
        compiler_params=_cp(("arbitrary",)),
    )(j_arr, *args)


def add_pairs(full, land, spec, c_arr):
    _, axis, w, _ = spec
    L, R, C = full.shape
    if axis == 2:
        tr = min(512, R)
        grid = (4, L, R // tr)
        fspec = pl.BlockSpec((None, tr, w), lambda q, l, i, c: (l, i, 2 * q + c[0]))
        lspec = pl.BlockSpec((None, None, tr, w), lambda q, l, i, c: (q, l, i, 0))
    else:
        grid = (4, L, 1)
        fspec = pl.BlockSpec((None, w, C), lambda q, l, i, c: (l, 2 * q + c[0], 0))
        lspec = pl.BlockSpec((None, None, w, C), lambda q, l, i, c: (q, l, 0, 0))

    def body(c_ref, a_ref, b_ref, o_ref, ob_ref):
        s = a_ref[...] + b_ref[...].astype(F32)
        o_ref[...] = s
        ob_ref[...] = s.astype(BF16)

    return pl.pallas_call(
        body, name="add_pairs",
        grid_spec=pltpu.PrefetchScalarGridSpec(
            num_scalar_prefetch=1, grid=grid, in_specs=[fspec, lspec], out_specs=[lspec, lspec]),
        out_shape=[_sds(land.shape, F32), _sds(land.shape, BF16)],
        compiler_params=_cp(("arbitrary",) * 3),
    )(c_arr, full, land)


def _adamw(w, g, m, v):
    m = ADAM_B1 * m + (1.0 - ADAM_B1) * g
    v = ADAM_B2 * v + (1.0 - ADAM_B2) * (g * g)
    m_hat = m / (1.0 - ADAM_B1 ** ADAM_STEP)
    v_hat = v / (1.0 - ADAM_B2 ** ADAM_STEP)
    delta = -ADAM_LR * (m_hat / (jnp.sqrt(v_hat) + ADAM_EPS) + ADAM_WD * w)
    return delta, m, v


def reduce_adamw(s1, land, w, m, v, spec, q_arr):
    _, axis, win, valid = spec
    L, R, C = w.shape
    if axis == 2:
        tr = min(256, R)
        grid = (L, R // tr)
        wspec = pl.BlockSpec((None, None, tr, win), lambda l, i, q: (q[0], l, i, 0))
        lspec = pl.BlockSpec((3, None, tr, win), lambda l, i, q: (0, l, i, 0))
        sspec = pl.BlockSpec((None, tr, C), lambda l, i, q: (l, i, 0))
    else:
        grid = (L, 1)
        wspec = pl.BlockSpec((None, None, win, C), lambda l, i, q: (q[0], l, 0, 0))
        lspec = pl.BlockSpec((3, None, win, C), lambda l, i, q: (0, l, 0, 0))
        sspec = pl.BlockSpec((None, R, C), lambda l, i, q: (l, 0, 0))

    def body(q_ref, own_ref, land_ref, w_ref, m_ref, v_ref, g_ref, d_ref, nm_ref, nv_ref):
        if axis == 2:
            rd = lambda r, *lead: r[(*lead, slice(None), slice(0, valid))]
        else:
            rd = lambda r, *lead: r[(*lead, slice(0, valid), slice(None))]
        g = rd(own_ref)
        for k in range(3):
            g = g + rd(land_ref, k).astype(F32)
        g_ref[...] = g
        d, nm, nv = _adamw(w_ref[...], g, m_ref[...], v_ref[...])
        d_ref[...] = d
        nm_ref[...] = nm
        nv_ref[...] = nv

    return pl.pallas_call(
        body, name="reduce_adamw",
        grid_spec=pltpu.PrefetchScalarGridSpec(
            num_scalar_prefetch=1, grid=grid,
            in_specs=[wspec, lspec, sspec, sspec, sspec], out_specs=[sspec] * 4),
        out_shape=[_sds(w.shape)] * 4, compiler_params=_cp(("arbitrary", "arbitrary")),
    )(q_arr, s1, land, w, m, v)


def small_reduce_adamw(gathered, wmv):
    n, k = len(gathered), len(wmv)

    def body(*refs):
        gs = refs[:n]
        ws = refs[n:n + 3 * k]
        outs = refs[n + 3 * k:]
        for a in range(n):
            g = gs[a][0]
            for dev in range(1, N_DEV):
                g = g + gs[a][dev]
            outs[a][...] = g
            if a < k:
                d, nm, nv = _adamw(ws[3 * a][...], g, ws[3 * a + 1][...], ws[3 * a + 2][...])
                outs[n + 3 * a][...] = d
                outs[n + 3 * a + 1][...] = nm
                outs[n + 3 * a + 2][...] = nv

    flat = [t for tup in wmv for t in tup]
    out_shape = [_sds(g.shape[1:]) for g in gathered] + [_sds(t.shape) for t in flat]
    return pl.pallas_call(body, name="small_reduce_adamw", out_shape=out_shape, compiler_params=_cp())(*gathered, *flat)


def small_adamw(gs, wmv):
    k = len(gs)

    def body(*refs):
        for a in range(k):
            g, w, m, v = refs[4 * a:4 * a + 4]
            d, nm, nv = _adamw(w[...], g[...], m[...], v[...])
            refs[4 * k + 3 * a][...] = d
            refs[4 * k + 3 * a + 1][...] = nm
            refs[4 * k + 3 * a + 2][...] = nv

    args = [t for g, tup in zip(gs, wmv) for t in (g,) + tuple(tup)]
    out_shape = [_sds(g.shape) for g in gs for _ in range(3)]
    return pl.pallas_call(body, name="small_adamw", out_shape=out_shape, compiler_params=_cp())(*args)


WEIGHT_NAMES = ("even_w_in", "even_w_out", "pool_w", "pool_scale", "odd_w_in", "odd_w_out", "conv_dw", "conv_ln_g",
                "conv_ln_b", "sg_ln_g", "sg_ln_b", "sg_w", "sg_b", "ln_mix_g", "ln_mix_b", "ffn_w_gate", "ffn_w_up",
                "ffn_w_down", "ln_ffn_g", "ln_ffn_b", "ple_w_proj", "ple_w_gate", "ple_b_gate")
REPLICATED = ("pool_w", "pool_scale", "sg_w", "sg_b", "ln_mix_g", "ln_mix_b", "ln_ffn_g", "ln_ffn_b", "ple_b_gate")
SHARDED_SMALL = ("conv_dw", "conv_ln_g", "conv_ln_b", "sg_ln_g", "sg_ln_b")
NATURAL = {"pool_w": (4, 128, 128), "pool_scale": (1, 512), "sg_w": (4, 128, 128), "sg_b": (4, 128),
           "ln_mix_g": (2, 1024), "ln_mix_b": (2, 1024), "ln_ffn_g": (2, 1024), "ln_ffn_b": (2, 1024),
           "ple_b_gate": (2, 1024)}


def kernel(x, p, even_w_in, even_w_out, pool_w, pool_scale, odd_w_in, odd_w_out, conv_dw, conv_ln_g, conv_ln_b, sg_ln_g, sg_ln_b, sg_w, sg_b, ln_mix_g, ln_mix_b, ffn_w_gate, ffn_w_up, ffn_w_down, ln_ffn_g, ln_ffn_b, ple_w_proj, ple_w_gate, ple_b_gate, loss_target, m_even_w_in, m_even_w_out, m_pool_w, m_pool_scale, m_odd_w_in, m_odd_w_out, m_conv_dw, m_conv_ln_g, m_conv_ln_b, m_sg_ln_g, m_sg_ln_b, m_sg_w, m_sg_b, m_ln_mix_g, m_ln_mix_b, m_ffn_w_gate, m_ffn_w_up, m_ffn_w_down, m_ln_ffn_g, m_ln_ffn_b, m_ple_w_proj, m_ple_w_gate, m_ple_b_gate, v_even_w_in, v_even_w_out, v_pool_w, v_pool_scale, v_odd_w_in, v_odd_w_out, v_conv_dw, v_conv_ln_g, v_conv_ln_b, v_sg_ln_g, v_sg_ln_b, v_sg_w, v_sg_b, v_ln_mix_g, v_ln_mix_b, v_ffn_w_gate, v_ffn_w_up, v_ffn_w_down, v_ln_ffn_g, v_ln_ffn_b, v_ple_w_proj, v_ple_w_gate, v_ple_b_gate):
    A = dict(locals())
    specs = [BIG[k] for k in BIG_ORDER]

    mx, my, mc = _mesh_pos()
    j_arr = (4 * mx + 2 * my + mc).astype(jnp.int32).reshape(1)
    q_arr = (2 * mx + my).astype(jnp.int32).reshape(1)
    c_arr = mc.astype(jnp.int32).reshape(1)
    small_blk = jnp.concatenate([conv_dw[0], conv_ln_g, conv_ln_b, sg_ln_g, sg_ln_b, jnp.zeros((5, 64), F32)], axis=0)
    ag_specs = specs + [((N_DEV, 40, 64), 0, 1, 1)]
    mine = pack_weights([A[k] for k in BIG_ORDER], small_blk[None], ag_specs, j_arr)
    fulls = all_gather("ag_weights", list(mine), ag_specs, in_place=True)
    W = {k: f for k, f in zip(BIG_ORDER, fulls)}
    for k in ("even_w_in", "even_w_out", "odd_w_in", "odd_w_out"):
        W[k] = W[k][0]
    sm = fulls[-1].transpose(1, 0, 2).reshape(40, 512)
    P = {k: A[k].reshape(NATURAL[k]) for k in REPLICATED}
    P.update(conv_dw=sm[0:31], conv_ln_g=sm[31:32], conv_ln_b=sm[32:33], sg_ln_g=sm[33:34], sg_ln_b=sm[34:35])

    sq, dx, big, small = local_step(x[0], p[:, 0], loss_target[0], W, P)
    loss = lax.psum(0.5 * jnp.sum(sq) / x.shape[-1], ("x", "y", "c"))

    land1 = rs_sibling([big[k][1] for k in BIG_ORDER], specs)
    s1, s1b = zip(*[add_pairs(big[k][0], l, spec, c_arr) for k, spec, l in zip(BIG_ORDER, specs, land1)])
    land2 = rs_chips(list(s1b))
    res = {}
    for k, spec, o, l in zip(BIG_ORDER, specs, s1, land2):
        res[k] = reduce_adamw(o, l, A[k], A["m_" + k], A["v_" + k], spec, q_arr)

    names = REPLICATED + SHARDED_SMALL
    gathered = all_gather("ag_small_grads", [small[k][None] for k in names],
                          [((N_DEV,) + small[k].shape, 0, 1, 1) for k in names])
    wmv = [tuple(A[pre + k].reshape(NATURAL[k]) for pre in ("", "m_", "v_")) for k in REPLICATED]
    outs = small_reduce_adamw(gathered, wmv)
    gsum = dict(zip(names, outs[:len(names)]))
    for a, k in enumerate(REPLICATED):
        res[k] = tuple(t.reshape(A[k].shape) for t in (gsum[k],) + tuple(outs[len(names) + 3 * a:len(names) + 3 * a + 3]))
    j = 4 * lax.axis_index("x") + 2 * lax.axis_index("y") + lax.axis_index("c")
    gsh = [lax.dynamic_slice_in_dim(gsum[k], j * 64, 64, axis=1).reshape(A[k].shape) for k in SHARDED_SMALL]
    outs = small_adamw(gsh, [(A[k], A["m_" + k], A["v_" + k]) for k in SHARDED_SMALL])
    for a, k in enumerate(SHARDED_SMALL):
        res[k] = (gsh[a],) + tuple(outs[3 * a:3 * a + 3])

    out = [loss, dx[None]]
    for part in range(4):
        out += [res[k][part] for k in WEIGHT_NAMES]
    return tuple(out)
```

```python
import functools
import math

import jax
import jax.numpy as jnp
from jax import lax
from jax.experimental import pallas as pl
from jax.experimental.pallas import tpu as pltpu

F32, BF16 = jnp.float32, jnp.bfloat16
ALPHA = 4.0 ** 0.25
LN_EPS = 1e-5
QK_SCALE = 0.125
POOL_WINDOWS = (2, 4, 8, 16)
CONV_TAPS = 31
N_DEV = 8
FF_SHARD, FF_PAD = 352, 384
ADAM_LR, ADAM_B1, ADAM_B2, ADAM_EPS, ADAM_WD, ADAM_STEP = 0.001, 0.9, 0.999, 1e-08, 0.01, 10
VMEM_LIMIT = 56 * 1024 * 1024
MESH_T = pl.DeviceIdType.MESH


def _cp(sem=None):
    return pltpu.CompilerParams(dimension_semantics=sem, vmem_limit_bytes=VMEM_LIMIT)


def _dot(a, b):
    return jnp.dot(a, b, preferred_element_type=F32)


def _dot_nt(a, b):
    return lax.dot_general(a, b, (((1,), (1,)), ((), ())), preferred_element_type=F32)


def _dot_tn(a, b):
    return lax.dot_general(a, b, (((0,), (0,)), ((), ())), preferred_element_type=F32)


def _sigmoid(x):
    return 1.0 / (1.0 + jnp.exp(-x))


def _softplus(z):
    return jnp.maximum(z, 0.0) + jnp.log(1.0 + jnp.exp(-jnp.abs(z)))


_GELU_C = math.sqrt(2.0 / math.pi)


def _gelu(x):
    return 0.5 * x * (1.0 + jnp.tanh(_GELU_C * (x + 0.044715 * x * x * x)))


def _gelu_grad(x):
    t = jnp.tanh(_GELU_C * (x + 0.044715 * x * x * x))
    return 0.5 * (1.0 + t) + 0.5 * x * (1.0 - t * t) * _GELU_C * (1.0 + 3.0 * 0.044715 * x * x)


def _ln_fwd(r, g, b):
    mu = jnp.mean(r, axis=-1, keepdims=True)
    xc = r - mu
    var = jnp.mean(xc * xc, axis=-1, keepdims=True)
    rstd = lax.rsqrt(var + LN_EPS)
    xh = xc * rstd
    return xh * g + b, xh, rstd


def _ln_bwd(dy, xh, rstd, g):
    dxh = dy * g
    m1 = jnp.mean(dxh, axis=-1, keepdims=True)
    m2 = jnp.mean(dxh * xh, axis=-1, keepdims=True)
    return rstd * (dxh - m1 - xh * m2)


def _split2(x):
    hi = x.astype(BF16)
    lo = (x - hi.astype(F32)).astype(BF16)
    return hi, lo


def _colsum(x):
    return jnp.sum(x, axis=0, keepdims=True)


def _tok_call(name, body, tiled, full, out_tiled, out_acc=(), tm=256, scratch=(), dep=None):
    def arr(t):
        return t[0] if isinstance(t, tuple) else t
    full = [t[0] if isinstance(t, tuple) and t[1] is None else t for t in full]
    S = arr(tiled[0]).shape[0]
    tm = min(tm, S)
    n_in = len(tiled) + len(full)
    deps = [] if dep is None else [dep]
    if deps:
        inner = body
        body = lambda *refs: inner(*refs[:n_in], *refs[n_in + 1:])

    def tspec(t):
        if isinstance(t, tuple):
            _, w, cb = t
            return pl.BlockSpec((tm, w), lambda i, cb=cb: (i, cb))
        return pl.BlockSpec((tm, t.shape[1]), lambda i: (i, 0))

    def fspec(t):
        if isinstance(t, tuple):
            a, l = t
            nd = a.ndim - 1
            return pl.BlockSpec((None,) + a.shape[1:], lambda i, l=l, nd=nd: (l,) + (0,) * nd)
        nd = t.ndim
        return pl.BlockSpec(t.shape, lambda i, nd=nd: (0,) * nd)

    def ospec(o):
        return pl.BlockSpec((tm, o.shape[1]), lambda i: (i, 0))

    def aspec(o):
        nd = len(o.shape)
        return pl.BlockSpec(o.shape, lambda i, nd=nd: (0,) * nd)

    outs = pl.pallas_call(
        body, name=name, grid=(S // tm,),
        in_specs=[tspec(t) for t in tiled] + [fspec(t) for t in full] + [ANY] * len(deps),
        out_specs=[ospec(o) for o in out_tiled] + [aspec(o) for o in out_acc],
        out_shape=list(out_tiled) + list(out_acc),
        scratch_shapes=list(scratch),
        compiler_params=_cp(("arbitrary",)),
    )(*[arr(t) for t in tiled], *[arr(t) for t in full], *deps)
    return outs


def _sds(shape, dtype=F32):
    return jax.ShapeDtypeStruct(tuple(shape), dtype)


def _acc(ref, val):
    @pl.when(pl.program_id(0) == 0)
    def _():
        ref[...] = val

    @pl.when(pl.program_id(0) != 0)
    def _():
        ref[...] += val


def mm_in(x, w, nb16=0):
    S, N = x.shape[0], w.shape[1]

    def body(x_ref, w_ref, h_ref, xb_ref, *hb_ref):
        xb = x_ref[...].astype(BF16)
        xb_ref[...] = xb
        h = _dot(xb, w_ref[...])
        h_ref[...] = h
        if nb16:
            hb_ref[0][...] = h[:, 0:nb16].astype(BF16)

    outs = [_sds((S, N)), _sds((S, x.shape[1]), BF16)] + ([_sds((S, nb16), BF16)] if nb16 else [])
    return _tok_call("mm_in", body, [x], [w], outs, tm=512)


def _stack_heads(x, hm0, dtype=BF16):
    return jnp.concatenate([jnp.where(hm0, x, 0), jnp.where(hm0, 0, x)], axis=0).astype(dtype)


def _unstack_k(x, T):
    return jnp.concatenate([x[0:T], x[T:2 * T]], axis=1)


def _cumsum_mm(x, u):
    n = x.shape[0]
    hi, lo = _split2(x)
    r = _dot(jnp.concatenate([hi, lo], axis=0), u)
    return r[0:n] + r[n:2 * n]


def attn_fwd(qkv, T=256):
    S = qkv.shape[0]
    T = min(T, S)
    nq = S // T

    def body(q_ref, k_ref, v_ref, o_ref, t_ref, acc_ref, c_ref, qh_ref):
        i = pl.program_id(0)
        hm0 = lax.broadcasted_iota(jnp.int32, (1, 128), 1) < 64
        r2 = lax.broadcasted_iota(jnp.int32, (2 * T, T), 0)
        c2 = lax.broadcasted_iota(jnp.int32, (2 * T, T), 1)
        causal = c2 < jnp.where(r2 >= T, r2 - T, r2)
        ur = lax.broadcasted_iota(jnp.int32, (T, T), 0)
        uc = lax.broadcasted_iota(jnp.int32, (T, T), 1)
        u_incl = (ur >= uc).astype(BF16)
        acc_ref[...] = jnp.zeros_like(acc_ref)
        c_ref[...] = jnp.zeros_like(c_ref)
        for pp in range(4):
            qh_ref[pp] = _stack_heads(q_ref[:, pp * 128:(pp + 1) * 128] * QK_SCALE, hm0)

        def block(kb, diag):
            ks = pl.multiple_of(kb * T, T)
            for pp in range(4):
                cs = slice(pp * 128, (pp + 1) * 128)
                z = _dot_nt(qh_ref[pp], k_ref[pl.ds(ks, T), cs])
                sp = _softplus(z)
                lk = jnp.where(causal, -sp, 0.0) if diag else -sp
                incl = _cumsum_mm(lk, u_incl)
                c = c_ref[pp]
                w = jnp.exp((z - sp) + c + (incl - lk))
                if diag:
                    w = jnp.where(causal, w, 0.0)
                acc_ref[:, cs] += _dot(_unstack_k(w.astype(BF16), T), _stack_heads(v_ref[pl.ds(ks, T), cs], hm0))
                c_ref[pp] = c + jnp.broadcast_to(incl[:, 0:1], (2 * T, T))

        block(i, True)

        def step(jj, carry):
            block(i - 1 - jj, False)
            return carry

        lax.fori_loop(0, i, step, 0)
        o_ref[...] = acc_ref[...].astype(BF16)
        for pp in range(4):
            for hd in range(2):
                t_ref[2 * pp + hd] = c_ref[pp, hd * T:(hd + 1) * T, 0:128]

    return pl.pallas_call(
        body, name="attn_fwd", grid=(nq,),
        in_specs=[pl.BlockSpec((T, 512), lambda i: (i, 0)),
                  pl.BlockSpec((S, 512), lambda i: (0, 1)),
                  pl.BlockSpec((S, 512), lambda i: (0, 2))],
        out_specs=[pl.BlockSpec((T, 512), lambda i: (i, 0)),
                   pl.BlockSpec((8, T, 128), lambda i: (0, i, 0))],
        out_shape=[_sds((S, 512), BF16), _sds((8, S, 128))],
        scratch_shapes=[pltpu.VMEM((T, 512), F32), pltpu.VMEM((4, 2 * T, T), F32), pltpu.VMEM((4, 2 * T, 128), BF16)],
        compiler_params=_cp(("arbitrary",)),
    )(qkv, qkv, qkv)


def pool_fwd(h, pool_w, pool_scale, CH=256):
    S = h.shape[0]
    CH = min(CH, S)

    def body(u_ref, w_ref, sc_ref, b_ref, pooled_ref, pad_ref):
        pad_ref[0:16, :] = jnp.zeros((16, 512), F32)
        pad_ref[16:16 + S, :] = u_ref[...]
        for g, win in enumerate(POOL_WINDOWS):
            cs = slice(g * 128, (g + 1) * 128)
            wq = w_ref[g].astype(BF16)
            for ch in range(S // CH):
                base = ch * CH
                acc = pad_ref[16 + base:16 + base + CH, cs]
                for sft in range(1, win):
                    acc = acc + pad_ref[16 + base - sft:16 + base - sft + CH, cs]
                t = base + lax.broadcasted_iota(jnp.int32, (CH, 1), 0)
                cnt = jnp.minimum(t + 1, win).astype(F32)
                pooled = (acc / cnt - pad_ref[16 + base:16 + base + CH, cs]).astype(BF16)
                pooled_ref[base:base + CH, cs] = pooled
                b_ref[base:base + CH, cs] = (_dot(pooled, wq) * sc_ref[:, cs]).astype(BF16)

    return pl.pallas_call(
        body, name="pool_fwd", grid=(1,),
        in_specs=[pl.BlockSpec((S, 512), lambda i: (0, 3)),
                  pl.BlockSpec((4, 128, 128), lambda i: (0, 0, 0)),
                  pl.BlockSpec((1, 512), lambda i: (0, 0))],
        out_specs=[pl.BlockSpec((S, 512), lambda i: (0, 0)), pl.BlockSpec((S, 512), lambda i: (0, 0))],
        out_shape=[_sds((S, 512), BF16), _sds((S, 512), BF16)],
        scratch_shapes=[pltpu.VMEM((S + 16, 512), F32)],
        compiler_params=_cp(("arbitrary",)),
    )(h, pool_w, pool_scale)


def conv_fwd(h, dw, CH=128):
    S = h.shape[0]

    def body(a_ref, g_ref, dw_ref, y_ref, hc_ref, pad_ref):
        hc = a_ref[...] * _sigmoid(g_ref[...])
        hc_ref[...] = hc
        pad_ref[0:32, :] = jnp.zeros((32, 128), F32)
        pad_ref[32:32 + S, :] = hc
        for ch in range(S // CH):
            base = ch * CH + 2
            acc = dw_ref[0:1, :] * pad_ref[base:base + CH, :]
            for k in range(1, CONV_TAPS):
                acc = acc + dw_ref[k:k + 1, :] * pad_ref[base + k:base + k + CH, :]
            y_ref[ch * CH:(ch + 1) * CH, :] = acc

    return pl.pallas_call(
        body, name="conv_fwd", grid=(4,),
        in_specs=[pl.BlockSpec((S, 128), lambda c: (0, c)),
                  pl.BlockSpec((S, 128), lambda c: (0, 4 + c)),
                  pl.BlockSpec((CONV_TAPS, 128), lambda c: (0, c))],
        out_specs=[pl.BlockSpec((S, 128), lambda c: (0, c)), pl.BlockSpec((S, 128), lambda c: (0, c))],
        out_shape=[_sds((S, 512)), _sds((S, 512))],
        scratch_shapes=[pltpu.VMEM((S + 32, 128), F32)],
        compiler_params=_cp(("arbitrary",)),
    )(h, h, dw)


def _masked_sg_w(w_ref, g):
    row = lax.broadcasted_iota(jnp.int32, (128, 128), 0)
    col = lax.broadcasted_iota(jnp.int32, (128, 128), 1)
    return jnp.where(row >= col, w_ref[g], 0.0).astype(BF16)


def odd_post(y, h, cl_g, cl_b, sl_g, sl_b, sg_w, sgb_bc, tm=256):
    S = y.shape[0]
    tm = min(tm, S)

    def body(y_ref, zc_ref, clg, clb, slg, slb, w_ref, sb_ref,
             c_ref, d_ref, xhc_ref, rsc_ref, xhv_ref, rsv_ref, sv_ref):
        lnc, xhc, rsc = _ln_fwd(y_ref[...], clg[...], clb[...])
        c_ref[...] = (lnc * _sigmoid(lnc)).astype(BF16)
        xhc_ref[...] = xhc
        rsc_ref[...] = rsc
        z = _gelu(zc_ref[...])
        vn, xhv, rsv = _ln_fwd(z[:, 512:], slg[...], slb[...])
        xhv_ref[...] = xhv
        rsv_ref[...] = rsv
        vnb = vn.astype(BF16)
        for g in range(4):
            wm = _masked_sg_w(w_ref, g)
            for ch in range(tm // 128):
                rs, cs = slice(ch * 128, (ch + 1) * 128), slice(g * 128, (g + 1) * 128)
                sv_ref[rs, cs] = _dot(wm, vnb[rs, cs]) + sb_ref[g]
        d_ref[...] = (z[:, :512] * sv_ref[...]).astype(BF16)

    return _tok_call(
        "odd_post", body, [y, (h, 1024, 1)], [cl_g, cl_b, sl_g, sl_b, sg_w, sgb_bc],
        [_sds((S, 512), BF16), _sds((S, 512), BF16), _sds((S, 512)), _sds((S, 1)),
         _sds((S, 512)), _sds((S, 1)), _sds((S, 512))], tm=tm)


def mm_out_ln(l1, l2, x, w, g, b, dep=None):
    S, D = x.shape

    def body(l1_ref, l2_ref, x_ref, w_ref, g_ref, b_ref, y_ref, xh_ref, rs_ref):
        mix = _dot(l1_ref[...], w_ref[0:512, :]) + _dot(l2_ref[...], w_ref[512:1024, :])
        y, xh, rs = _ln_fwd(ALPHA * x_ref[...] + mix, g_ref[...], b_ref[...])
        y_ref[...] = y
        xh_ref[...] = xh
        rs_ref[...] = rs

    return _tok_call("mm_out_ln", body, [l1, l2, x], [w, g, b],
                     [_sds((S, D)), _sds((S, D)), _sds((S, 1))], dep=dep)


def ffn_up(x1, wg, wu, layer):
    S, D = x1.shape
    F = wg.shape[-1]

    def body(x_ref, wg_ref, wu_ref, gate_ref, up_ref, hb_ref, xb_ref):
        xb = x_ref[...].astype(BF16)
        xb_ref[...] = xb
        gate = _dot(xb, wg_ref[...])
        up = _dot(xb, wu_ref[...])
        gate_ref[...] = gate
        up_ref[...] = up
        hb_ref[...] = (gate * _sigmoid(gate) * up).astype(BF16)

    return _tok_call("ffn_up", body, [x1], [(wg, layer), (wu, layer)],
                     [_sds((S, F)), _sds((S, F)), _sds((S, F), BF16), _sds((S, D), BF16)])


def ffn_down_ln(hb, x1, wd, layer, g, b):
    S, D = x1.shape

    def body(h_ref, x_ref, w_ref, g_ref, b_ref, y_ref, xh_ref, rs_ref):
        f = _dot(h_ref[...], w_ref[...])
        y, xh, rs = _ln_fwd(ALPHA * x_ref[...] + f, g_ref[...], b_ref[...])
        y_ref[...] = y
        xh_ref[...] = xh
        rs_ref[...] = rs

    return _tok_call("ffn_down_ln", body, [hb, x1], [(wd, layer), g, b],
                     [_sds((S, D)), _sds((S, D)), _sds((S, 1))])


def ple_fwd(x2, p, wpg, wpp, layer, bg, target=None, dep=None):
    S, D = x2.shape
    last = target is not None

    def body(*refs):
        if last:
            x_ref, p_ref, t_ref, wg_ref, wp_ref, b_ref, x3_ref, sg_ref, pp_ref, xb_ref, pb_ref, dy_ref, ls_ref = refs
        else:
            x_ref, p_ref, wg_ref, wp_ref, b_ref, x3_ref, sg_ref, pp_ref, xb_ref, pb_ref = refs
        x = x_ref[...]
        xb = x.astype(BF16)
        pb = p_ref[...].astype(BF16)
        xb_ref[...] = xb
        pb_ref[...] = pb
        sg = _sigmoid(_dot(xb, wg_ref[...]) + b_ref[...])
        pp = _dot(pb, wp_ref[...])
        sg_ref[...] = sg
        pp_ref[...] = pp
        x3 = x + sg * pp
        x3_ref[...] = x3
        if last:
            err = x3 - t_ref[...]
            dy_ref[...] = err * (1.0 / D)
            _acc(ls_ref, _colsum(err * err))

    outs = [_sds((S, D)), _sds((S, D)), _sds((S, D)), _sds((S, D), BF16), _sds((S, p.shape[1]), BF16)]
    tiled = [x2, p] + ([target] if last else [])
    if last:
        outs.append(_sds((S, D)))
    return _tok_call("ple_fwd", body, tiled, [(wpg, layer), (wpp, layer), bg], outs,
                     [_sds((1, D))] if last else [], dep=dep)


def ple_bwd(dx3, sg, pp, wpg, layer, dep=None):
    S, D = dx3.shape

    def body(d_ref, sg_ref, pp_ref, w_ref, dx_ref, dgp_ref, dpp_ref, dbg_ref):
        d, sg = d_ref[...], sg_ref[...]
        dgp = d * pp_ref[...] * sg * (1.0 - sg)
        dgpb = dgp.astype(BF16)
        dgp_ref[...] = dgpb
        dpp_ref[...] = (d * sg).astype(BF16)
        dx_ref[...] = d + _dot_nt(dgpb, w_ref[...])
        _acc(dbg_ref, _colsum(dgp))

    return _tok_call("ple_bwd", body, [dx3, sg, pp], [(wpg, layer)],
                     [_sds((S, D)), _sds((S, D), BF16), _sds((S, D), BF16)], [_sds((1, D))], dep=dep)


def ffn_bwd_a(dx2, xh, rs, g, gate, up, wd, layer):
    S, D = dx2.shape
    F = gate.shape[1]

    def body(d_ref, xh_ref, rs_ref, gate_ref, up_ref, g_ref, w_ref,
             dr_ref, drb_ref, dg_ref, du_ref, dlg_ref, dlb_ref):
        d, xh = d_ref[...], xh_ref[...]
        dr = _ln_bwd(d, xh, rs_ref[...], g_ref[...])
        drb = dr.astype(BF16)
        dr_ref[...] = dr
        drb_ref[...] = drb
        _acc(dlg_ref, _colsum(d * xh))
        _acc(dlb_ref, _colsum(d))
        dh = _dot_nt(drb, w_ref[...])
        gate, up = gate_ref[...], up_ref[...]
        s = _sigmoid(gate)
        dg_ref[...] = (dh * up * s * (1.0 + gate * (1.0 - s))).astype(BF16)
        du_ref[...] = (dh * gate * s).astype(BF16)

    return _tok_call("ffn_bwd_a", body, [dx2, xh, rs, gate, up], [g, (wd, layer)],
                     [_sds((S, D)), _sds((S, D), BF16), _sds((S, F), BF16), _sds((S, F), BF16)],
                     [_sds((1, D)), _sds((1, D))])


def ffn_bwd_b(dr, dgate_b, dup_b, wg, wu, layer, dep=None):
    S, D = dr.shape

    def body(dr_ref, dg_ref, du_ref, wg_ref, wu_ref, dx_ref):
        dx_ref[...] = (ALPHA * dr_ref[...] + _dot_nt(dg_ref[...], wg_ref[...])
                       + _dot_nt(du_ref[...], wu_ref[...]))

    return _tok_call("ffn_bwd_b", body, [dr, dgate_b, dup_b], [(wg, layer), (wu, layer)], [_sds((S, D))], dep=dep)[0]


def mix_bwd(dx1, xh, rs, g, w):
    S, D = dx1.shape

    def body(d_ref, xh_ref, rs_ref, g_ref, w_ref, dr_ref, dmb_ref, dl_ref, dlg_ref, dlb_ref):
        d, xh = d_ref[...], xh_ref[...]
        dr = _ln_bwd(d, xh, rs_ref[...], g_ref[...])
        drb = dr.astype(BF16)
        dr_ref[...] = dr
        dmb_ref[...] = drb
        dl_ref[...] = _dot_nt(drb, w_ref[...])
        _acc(dlg_ref, _colsum(d * xh))
        _acc(dlb_ref, _colsum(d))

    return _tok_call("mix_bwd", body, [dx1, xh, rs], [g, w],
                     [_sds((S, D)), _sds((S, D), BF16), _sds((S, D))], [_sds((1, D)), _sds((1, D))])


def dx_in(dr, pieces, w):
    S, D = dr.shape
    offs = [o for _, o in pieces]
    widths = [a.shape[1] for a, _ in pieces]

    def body(*refs):
        dr_ref, prefs, w_ref, dx_ref = refs[0], refs[1:1 + len(pieces)], refs[-2], refs[-1]
        acc = ALPHA * dr_ref[...]
        for pr, o, n in zip(prefs, offs, widths):
            acc = acc + _dot_nt(pr[...], w_ref[:, o:o + n])
        dx_ref[...] = acc

    return _tok_call("dx_in", body, [dr] + [a for a, _ in pieces], [w], [_sds((S, D))])[0]


def odd_post_bwd(dl, h, xhc, rsc, xhv, rsv, sv, cl_g, cl_b, sl_g, sl_b, sg_w, tm=256, dep=None):
    S = dl.shape[0]
    tm = min(tm, S)

    def body(dl_ref, zc_ref, xhc_ref, rsc_ref, xhv_ref, rsv_ref, sv_ref, clg, clb, slg, slb, w_ref,
             dy_ref, dzc_ref, dclg_ref, dclb_ref, dslg_ref, dslb_ref, dwm_ref, dsb_ref, dvn_ref):
        first = pl.program_id(0) == 0
        last = pl.program_id(0) == pl.num_programs(0) - 1
        dc, dd = dl_ref[:, 0:512], dl_ref[:, 512:1024]
        xhc = xhc_ref[...]
        lnc = xhc * clg[...] + clb[...]
        s = _sigmoid(lnc)
        dlnc = dc * s * (1.0 + lnc * (1.0 - s))
        dy_ref[...] = _ln_bwd(dlnc, xhc, rsc_ref[...], clg[...])
        _acc(dclg_ref, _colsum(dlnc * xhc))
        _acc(dclb_ref, _colsum(dlnc))
        zc = zc_ref[...]
        z = _gelu(zc)
        dsv = dd * z[:, :512]
        dsvb = dsv.astype(BF16)
        xhv = xhv_ref[...]
        vnb = (xhv * slg[...] + slb[...]).astype(BF16)

        @pl.when(first)
        def _():
            dwm_ref[...] = jnp.zeros_like(dwm_ref)
            dsb_ref[...] = jnp.zeros_like(dsb_ref)

        for g in range(4):
            wm = _masked_sg_w(w_ref, g)
            for ch in range(tm // 128):
                rs_, cs = slice(ch * 128, (ch + 1) * 128), slice(g * 128, (g + 1) * 128)
                dwm_ref[g] += _dot_nt(dsvb[rs_, cs], vnb[rs_, cs])
                dvn_ref[rs_, cs] = _dot_tn(wm, dsvb[rs_, cs])
                dsb_ref[g] += dsv[rs_, cs]
        dvn = dvn_ref[...]
        dvv = _ln_bwd(dvn, xhv, rsv_ref[...], slg[...])
        _acc(dslg_ref, _colsum(dvn * xhv))
        _acc(dslb_ref, _colsum(dvn))
        gg = _gelu_grad(zc)
        dzc_ref[:, 0:512] = (dd * sv_ref[...] * gg[:, :512]).astype(BF16)
        dzc_ref[:, 512:1024] = (dvv * gg[:, 512:]).astype(BF16)

        @pl.when(last)
        def _():
            row = lax.broadcasted_iota(jnp.int32, (128, 128), 0)
            col = lax.broadcasted_iota(jnp.int32, (128, 128), 1)
            for g in range(4):
                dwm_ref[g] = jnp.where(row >= col, dwm_ref[g], 0.0)
                dsb_ref[g] = jnp.broadcast_to(jnp.sum(dsb_ref[g], axis=1, keepdims=True), (128, 128))

    return _tok_call(
        "odd_post_bwd", body, [dl, (h, 1024, 1), xhc, rsc, xhv, rsv, sv], [cl_g, cl_b, sl_g, sl_b, sg_w],
        [_sds((S, 512)), _sds((S, 1024), BF16)],
        [_sds((1, 512)), _sds((1, 512)), _sds((1, 512)), _sds((1, 512)), _sds((4, 128, 128)), _sds((4, 128, 128))],
        tm=tm, scratch=[pltpu.VMEM((tm, 512), F32)], dep=dep)


def conv_bwd(dy, hc, h, dw, CH=128):
    S = dy.shape[0]

    def body(dy_ref, hc_ref, a_ref, g_ref, dw_ref, da_ref, dg_ref, ddw_ref, padh_ref, padd_ref, dhc_ref):
        padh_ref[0:32, :] = jnp.zeros((32, 128), F32)
        padh_ref[32:32 + S, :] = hc_ref[...]
        padd_ref[0:S, :] = dy_ref[...]
        padd_ref[S:S + 32, :] = jnp.zeros((32, 128), F32)
        taps = [jnp.zeros((1, 128), F32) for _ in range(CONV_TAPS)]
        for ch in range(S // CH):
            b0 = ch * CH
            dyc = padd_ref[b0:b0 + CH, :]
            acc = dw_ref[0:1, :] * padd_ref[b0 + 30:b0 + 30 + CH, :]
            taps[0] = taps[0] + _colsum(dyc * padh_ref[b0 + 2:b0 + 2 + CH, :])
            for k in range(1, CONV_TAPS):
                acc = acc + dw_ref[k:k + 1, :] * padd_ref[b0 + 30 - k:b0 + 30 - k + CH, :]
                taps[k] = taps[k] + _colsum(dyc * padh_ref[b0 + 2 + k:b0 + 2 + k + CH, :])
            dhc_ref[b0:b0 + CH, :] = acc
        for k in range(CONV_TAPS):
            ddw_ref[k:k + 1, :] = taps[k]
        dhc = dhc_ref[...]
        s = _sigmoid(g_ref[...])
        da_ref[...] = (dhc * s).astype(BF16)
        dg_ref[...] = (dhc * a_ref[...] * s * (1.0 - s)).astype(BF16)

    return pl.pallas_call(
        body, name="conv_bwd", grid=(4,),
        in_specs=[pl.BlockSpec((S, 128), lambda c: (0, c)),
                  pl.BlockSpec((S, 128), lambda c: (0, c)),
                  pl.BlockSpec((S, 128), lambda c: (0, c)),
                  pl.BlockSpec((S, 128), lambda c: (0, 4 + c)),
                  pl.BlockSpec((CONV_TAPS, 128), lambda c: (0, c))],
        out_specs=[pl.BlockSpec((S, 128), lambda c: (0, c)), pl.BlockSpec((S, 128), lambda c: (0, c)),
                   pl.BlockSpec((CONV_TAPS, 128), lambda c: (0, c))],
        out_shape=[_sds((S, 512), BF16), _sds((S, 512), BF16), _sds((CONV_TAPS, 512))],
        scratch_shapes=[pltpu.VMEM((S + 32, 128), F32), pltpu.VMEM((S + 32, 128), F32), pltpu.VMEM((S, 128), F32)],
        compiler_params=_cp(("arbitrary",)),
    )(dy, hc, h, h, dw)


def attn_bwd(qkv, dl, tb, T=256, dep=None):
    S = qkv.shape[0]
    T = min(T, S)
    nq = S // T

    def body(q_ref, k_ref, v_ref, do_ref, t_ref, dq_ref, dk_ref, dv_ref,
             dka_ref, dva_ref, dqa_ref, pc_ref, gc_ref, tot_ref, qh_ref, doh_ref):
        i = pl.program_id(0)
        hm0 = lax.broadcasted_iota(jnp.int32, (1, 128), 1) < 64
        r2 = lax.broadcasted_iota(jnp.int32, (2 * T, T), 0)
        c2 = lax.broadcasted_iota(jnp.int32, (2 * T, T), 1)
        causal = c2 < jnp.where(r2 >= T, r2 - T, r2)
        ur = lax.broadcasted_iota(jnp.int32, (T, T), 0)
        uc = lax.broadcasted_iota(jnp.int32, (T, T), 1)
        u_le = (ur <= uc).astype(BF16)
        u_lt = (ur < uc).astype(BF16)

        @pl.when(i == 0)
        def _():
            dka_ref[...] = jnp.zeros_like(dka_ref)
            dva_ref[...] = jnp.zeros_like(dva_ref)

        dqa_ref[...] = jnp.zeros_like(dqa_ref)
        pc_ref[...] = jnp.zeros_like(pc_ref)
        gc_ref[...] = jnp.zeros_like(gc_ref)
        for pp in range(4):
            cs = slice(pp * 128, (pp + 1) * 128)
            qh_ref[pp] = _stack_heads(q_ref[:, cs] * QK_SCALE, hm0)
            doh_ref[pp] = _stack_heads(do_ref[:, cs], hm0)
            for hd in range(2):
                for half in range(T // 128):
                    tot_ref[pp, hd * T:(hd + 1) * T, half * 128:(half + 1) * 128] = t_ref[2 * pp + hd]

        def block(kb, diag):
            ks = pl.multiple_of(kb * T, T)
            for pp in range(4):
                cs = slice(pp * 128, (pp + 1) * 128)
                kb16 = k_ref[pl.ds(ks, T), cs]
                vb16 = v_ref[pl.ds(ks, T), cs]
                qh, doh = qh_ref[pp], doh_ref[pp]
                z = _dot_nt(qh, kb16)
                sp = _softplus(z)
                lk = jnp.where(causal, -sp, 0.0) if diag else -sp
                pre = _cumsum_mm(lk, u_le)
                sig = jnp.exp(z - sp)
                pc = pc_ref[pp]
                w = sig * jnp.exp(tot_ref[pp] - pc - pre)
                if diag:
                    w = jnp.where(causal, w, 0.0)
                gmat = _dot_nt(doh, vb16) * w
                gex = gc_ref[pp] + _cumsum_mm(gmat, u_lt)
                dz = gmat * (1.0 - sig) - sig * gex
                if diag:
                    dz = jnp.where(causal, dz, 0.0)
                dzb = dz.astype(BF16)
                dqa_ref[:, cs] += _dot(_unstack_k(dzb, T), _stack_heads(kb16, hm0))
                dka_ref[pl.ds(ks, T), cs] += _dot_tn(dzb, qh)
                dva_ref[pl.ds(ks, T), cs] += _dot_tn(w.astype(BF16), doh)
                pc_ref[pp] = pc + jnp.broadcast_to(pre[:, T - 1:T], (2 * T, T))
                gc_ref[pp] = jnp.broadcast_to(gex[:, T - 1:T] + gmat[:, T - 1:T], (2 * T, T))

        def step(kb, carry):
            block(kb, False)
            return carry

        lax.fori_loop(0, i, step, 0)
        block(i, True)
        dq_ref[...] = (dqa_ref[...] * QK_SCALE).astype(BF16)

        @pl.when(i == nq - 1)
        def _():
            dk_ref[...] = dka_ref[...].astype(BF16)
            dv_ref[...] = dva_ref[...].astype(BF16)

    deps = [] if dep is None else [dep]
    call_body = body if dep is None else (lambda *refs: body(*refs[:5], *refs[6:]))
    return pl.pallas_call(
        call_body, name="attn_bwd", grid=(nq,),
        in_specs=[pl.BlockSpec((T, 512), lambda i: (i, 0)),
                  pl.BlockSpec((S, 512), lambda i: (0, 1)),
                  pl.BlockSpec((S, 512), lambda i: (0, 2)),
                  pl.BlockSpec((T, 512), lambda i: (i, 0)),
                  pl.BlockSpec((8, T, 128), lambda i: (0, i, 0))] + [ANY] * len(deps),
        out_specs=[pl.BlockSpec((T, 512), lambda i: (i, 0)),
                   pl.BlockSpec((S, 512), lambda i: (0, 0)),
                   pl.BlockSpec((S, 512), lambda i: (0, 0))],
        out_shape=[_sds((S, 512), BF16), _sds((S, 512), BF16), _sds((S, 512), BF16)],
        scratch_shapes=[pltpu.VMEM((S, 512), F32), pltpu.VMEM((S, 512), F32), pltpu.VMEM((T, 512), F32),
                        pltpu.VMEM((4, 2 * T, T), F32), pltpu.VMEM((4, 2 * T, T), F32), pltpu.VMEM((4, 2 * T, T), F32),
                        pltpu.VMEM((4, 2 * T, 128), BF16), pltpu.VMEM((4, 2 * T, 128), BF16)],
        compiler_params=_cp(("arbitrary",)),
    )(qkv, qkv, qkv, dl, tb, *deps)


def pool_bwd(dl, pooled_b, pool_w, pool_scale, CH=256):
    S = dl.shape[0]
    CH = min(CH, S)

    def body(db_ref, pooled_ref, w_ref, sc_ref, du_ref, dw_ref, dsc_ref, pad_ref, dp_ref):
        pad_ref[S:S + 16, :] = jnp.zeros((16, 128), F32)
        for g, win in enumerate(POOL_WINDOWS):
            cs = slice(g * 128, (g + 1) * 128)
            wq = w_ref[g].astype(BF16)
            dwg = jnp.zeros((128, 128), F32)
            dsc = jnp.zeros((1, 128), F32)
            for ch in range(S // CH):
                rs_ = slice(ch * CH, (ch + 1) * CH)
                db = db_ref[rs_, cs]
                pb = pooled_ref[rs_, cs]
                dsc = dsc + _colsum(db * _dot(pb, wq))
                dmsb = (db * sc_ref[:, cs]).astype(BF16)
                dwg = dwg + _dot_tn(pb, dmsb)
                dpool = _dot_nt(dmsb, wq)
                t = ch * CH + lax.broadcasted_iota(jnp.int32, (CH, 1), 0)
                cnt = jnp.minimum(t + 1, win).astype(F32)
                dp_ref[rs_, :] = dpool
                pad_ref[rs_, :] = dpool / cnt
            dw_ref[g] = dwg
            dsc_ref[:, cs] = dsc
            for ch in range(S // CH):
                base = ch * CH
                acc = pad_ref[base:base + CH, :]
                for sft in range(1, win):
                    acc = acc + pad_ref[base + sft:base + sft + CH, :]
                du_ref[base:base + CH, cs] = (acc - dp_ref[base:base + CH, :]).astype(BF16)

    return pl.pallas_call(
        body, name="pool_bwd", grid=(1,),
        in_specs=[pl.BlockSpec((S, 512), lambda i: (0, 1)),
                  pl.BlockSpec((S, 512), lambda i: (0, 0)),
                  pl.BlockSpec((4, 128, 128), lambda i: (0, 0, 0)),
                  pl.BlockSpec((1, 512), lambda i: (0, 0))],
        out_specs=[pl.BlockSpec((S, 512), lambda i: (0, 0)),
                   pl.BlockSpec((4, 128, 128), lambda i: (0, 0, 0)),
                   pl.BlockSpec((1, 512), lambda i: (0, 0))],
        out_shape=[_sds((S, 512), BF16), _sds((4, 128, 128)), _sds((1, 512))],
        scratch_shapes=[pltpu.VMEM((S + 16, 128), F32), pltpu.VMEM((S, 128), F32)],
        compiler_params=_cp(("arbitrary",)),
    )(dl, pooled_b, pool_w, pool_scale)


def tn_into(a, b, out, out_b, r0, c0, tk=512, tn=512):
    S, K = a.shape
    N = b.shape[1]
    tk, tn = min(tk, K), min(tn, N)
    assert K % tk == 0 and N % tn == 0 and r0 % tk == 0 and c0 % tn == 0
    rb, cb = r0 // tk, c0 // tn
    fresh = isinstance(out, jax.ShapeDtypeStruct)

    def body(*refs):
        a_ref, b_ref, o_ref, ob_ref = refs[0], refs[1], refs[-2], refs[-1]
        r = _dot_tn(a_ref[...], b_ref[...])
        o_ref[...] = r
        ob_ref[...] = r.astype(BF16)

    ospec = pl.BlockSpec((tk, tn), lambda i, j: (rb + i, cb + j))
    in_specs = [pl.BlockSpec((S, tk), lambda i, j: (0, i)), pl.BlockSpec((S, tn), lambda i, j: (0, j))]
    args = [a, b]
    aliases = {}
    if not fresh:
        in_specs += [pl.BlockSpec(memory_space=pl.ANY), pl.BlockSpec(memory_space=pl.ANY)]
        args += [out, out_b]
        aliases = {2: 0, 3: 1}
    shp = out.shape
    return pl.pallas_call(
        body, name="tn_grad", grid=(K // tk, N // tn),
        in_specs=in_specs, out_specs=[ospec, ospec],
        out_shape=[_sds(shp, F32), _sds(shp, BF16)],
        input_output_aliases=aliases,
        compiler_params=_cp(("arbitrary", "arbitrary")),
    )(*args)


def _row(a, i):
    return a[i:i + 1]


MIXER_NAMES = (("even_w_in", "even_w_out"), ("odd_w_in", "odd_w_out"))


def _tn_group(items):
    out = {}
    for name, (shape, parts) in items.items():
        g, gb = _sds(shape, F32), _sds(shape, BF16)
        for a, b, r0, c0 in parts:
            g, gb = tn_into(a, b, g, gb, r0, c0)
        out[name] = (g, gb)
    return out


def fwd_layer(i, xin, p_i, target, comm):
    s = {}
    W = comm.weights("mix", i, xin)
    w_in, w_out = (W[n] for n in MIXER_NAMES[i])
    if i == 0:
        s["h"], s["xb"], s["qkv"] = mm_in(xin, w_in, nb16=1536)
        s["l1"], s["tb"] = attn_fwd(s["qkv"])
        s["l2"], s["pooled"] = pool_fwd(s["h"], W["pool_w"], W["pool_scale"])
    else:
        s["h"], s["xb"] = mm_in(xin, w_in)
        s["y"], s["hc"] = conv_fwd(s["h"], W["conv_dw"])
        sgb_bc = jnp.broadcast_to(W["sg_b"][:, :, None], (4, 128, 128))
        (s["l1"], s["l2"], s["xhc"], s["rsc"], s["xhv"], s["rsv"], s["sv"]) = odd_post(
            s["y"], s["h"], W["conv_ln_g"], W["conv_ln_b"], W["sg_ln_g"], W["sg_ln_b"], W["sg_w"], sgb_bc)
    tok = comm.poke(s["l2"])
    x1, s["xh1"], s["rs1"] = mm_out_ln(s["l1"], s["l2"], xin, w_out, _row(W["ln_mix_g"], i), _row(W["ln_mix_b"], i),
                                       dep=tok)
    W = comm.weights("ffn", i, x1)
    s["gate"], s["up"], s["hb"], s["x1b"] = ffn_up(x1, W["ffn_w_gate%d" % i], W["ffn_w_up%d" % i], None)
    x2, s["xh2"], s["rs2"] = ffn_down_ln(s["hb"], x1, W["ffn_w_down%d" % i], None,
                                         _row(W["ln_ffn_g"], i), _row(W["ln_ffn_b"], i))
    tok = comm.poke(x2)
    outs = ple_fwd(x2, p_i, W["ple_w_gate%d" % i], W["ple_w_proj%d" % i], None, _row(W["ple_b_gate"], i), target,
                   dep=tok)
    s["sg"], s["pp"], s["x2b"], s["pb"] = outs[1:5]
    return outs[0], s, outs[5:]


def bwd_layer(i, dx, s, W, comm, tok=None):
    small = {}
    D = dx.shape[1]
    FP = W["ffn_w_gate%d" % i].shape[1]
    dx2, dgp_b, dpp_b, small["ple_b_gate"] = ple_bwd(dx, s["sg"], s["pp"], W["ple_w_gate%d" % i], None, dep=tok)
    dr2, dr2_b, dgate_b, dup_b, small["ln_ffn_g"], small["ln_ffn_b"] = ffn_bwd_a(
        dx2, s["xh2"], s["rs2"], _row(W["ln_ffn_g"], i), s["gate"], s["up"], W["ffn_w_down%d" % i], None)
    tok = comm.grads(_tn_group({
        "ple_w_gate%d" % i: ((D, D), [(s["x2b"], dgp_b, 0, 0)]),
        "ple_w_proj%d" % i: ((s["pb"].shape[1], D), [(s["pb"], dpp_b, 0, 0)]),
        "ffn_w_down%d" % i: ((FP, D), [(s["hb"], dr2_b, 0, 0)]),
        "ffn_w_gate%d" % i: ((D, FP), [(s["x1b"], dgate_b, 0, 0)]),
        "ffn_w_up%d" % i: ((D, FP), [(s["x1b"], dup_b, 0, 0)])}))
    dx1 = ffn_bwd_b(dr2, dgate_b, dup_b, W["ffn_w_gate%d" % i], W["ffn_w_up%d" % i], None, dep=tok)
    iname, oname = MIXER_NAMES[i]
    dr1, dmix_b, dl, small["ln_mix_g"], small["ln_mix_b"] = mix_bwd(
        dx1, s["xh1"], s["rs1"], _row(W["ln_mix_g"], i), W[oname])
    tok = comm.poke(dl)
    if i == 1:
        (dy, dzc_b, small["conv_ln_g"], small["conv_ln_b"], small["sg_ln_g"], small["sg_ln_b"],
         small["sg_w"], dsb) = odd_post_bwd(dl, s["h"], s["xhc"], s["rsc"], s["xhv"], s["rsv"], s["sv"],
                                            W["conv_ln_g"], W["conv_ln_b"], W["sg_ln_g"], W["sg_ln_b"], W["sg_w"],
                                            dep=tok)
        small["sg_b"] = dsb[:, :, 0]
        da_b, dg_b, small["conv_dw"] = conv_bwd(dy, s["hc"], s["h"], W["conv_dw"])
        pieces = [(da_b, 0), (dg_b, 512), (dzc_b, 1024)]
    else:
        dq_b, dk_b, dv_b = attn_bwd(s["qkv"], dl, s["tb"], dep=tok)
        du_b, small["pool_w"], small["pool_scale"] = pool_bwd(dl, s["pooled"], W["pool_w"], W["pool_scale"])
        pieces = [(dq_b, 0), (dk_b, 512), (dv_b, 1024), (du_b, 1536)]
    dxin = dx_in(dr1, pieces, W[iname])
    tok = comm.grads(_tn_group({
        oname: ((1024, D), [(s["l1"], dmix_b, 0, 0), (s["l2"], dmix_b, 512, 0)]),
        iname: ((D, 2048), [(s["xb"], a, 0, off) for a, off in pieces])}))
    return dxin, small, tok


def run_layers(x, p, target, comm):
    saved, xin = [], x
    for i in range(2):
        xin, s, extra = fwd_layer(i, xin, p[i], target if i == 1 else None, comm)
        saved.append(s)
    dx, sq = extra
    W = comm.all_weights()
    per_layer = [None, None]
    tok = None
    for i in (1, 0):
        dx, per_layer[i], tok = bwd_layer(i, dx, saved[i], W, comm, tok)
    small = {}
    for k in ("ln_mix_g", "ln_mix_b", "ln_ffn_g", "ln_ffn_b", "ple_b_gate"):
        small[k] = jnp.concatenate([per_layer[0][k], per_layer[1][k]], axis=0)
    for i in range(2):
        small.update({k: v for k, v in per_layer[i].items() if k not in small})
    return sq, dx, small


def _big_table():
    t = {}
    for nm in ("even", "odd"):
        t[nm + "_w_in"] = ((1024, 2048), 1, 256, 256, nm + "_w_in", 0)
        t[nm + "_w_out"] = ((1024, 1024), 0, 128, 128, nm + "_w_out", 0)
    for l in range(2):
        t["ffn_w_gate%d" % l] = ((1024, 8 * FF_PAD), 1, FF_PAD, FF_SHARD, "ffn_w_gate", l)
        t["ffn_w_up%d" % l] = ((1024, 8 * FF_PAD), 1, FF_PAD, FF_SHARD, "ffn_w_up", l)
        t["ffn_w_down%d" % l] = ((8 * FF_PAD, 1024), 0, FF_PAD, FF_SHARD, "ffn_w_down", l)
        t["ple_w_gate%d" % l] = ((1024, 1024), 0, 128, 128, "ple_w_gate", l)
        t["ple_w_proj%d" % l] = ((256, 1024), 1, 128, 128, "ple_w_proj", l)
    return t


BIG = _big_table()
SMALL_SPEC = ((N_DEV, 40, 64), 0, 1, 1)
_LAYER_GROUP = lambda l: ["ffn_w_gate%d" % l, "ffn_w_up%d" % l, "ffn_w_down%d" % l, "ple_w_gate%d" % l, "ple_w_proj%d" % l]
AG_GROUPS = (["even_w_in", "even_w_out"], _LAYER_GROUP(0), ["odd_w_in", "odd_w_out", "small"], _LAYER_GROUP(1))
ANY = pl.BlockSpec(memory_space=pl.ANY)
SEM = pl.BlockSpec(memory_space=pltpu.SEMAPHORE)


def _spec(name):
    return SMALL_SPEC if name == "small" else BIG[name]


def _win_shape(spec):
    full, axis, w = spec[:3]
    return tuple(w if d == axis else n for d, n in enumerate(full))


def _window(ref, axis, w, j):
    idx = [slice(None)] * len(ref.shape)
    idx[axis] = pl.ds(j, 1) if w == 1 else pl.ds(pl.multiple_of(j * w, w), w)
    return ref.at[tuple(idx)]


def _mesh_pos():
    return lax.axis_index("x"), lax.axis_index("y"), lax.axis_index("c")


def all_gather(name, blocks, specs, in_place=False):
    n = len(blocks)

    def body(*refs):
        blk, full = refs[:n], refs[n:2 * n]
        send, recv, loc = refs[2 * n:]
        x, y, c = _mesh_pos()
        me, sib = (x, y, c), (x, y, 1 - c)
        chips = [(1 - x, y), (x, 1 - y), (1 - x, 1 - y)]

        def win(a, dev):
            return _window(full[a], specs[a][1], specs[a][2], 4 * dev[0] + 2 * dev[1] + dev[2])

        def cp(a, k, block_dev, to, src=None):
            return pltpu.make_async_remote_copy(
                src_ref=win(a, block_dev) if src is None else src, dst_ref=win(a, block_dev),
                send_sem=send.at[a, k], recv_sem=recv.at[a, k], device_id=to, device_id_type=MESH_T)

        mine = [] if in_place else [pltpu.make_async_copy(blk[a], win(a, me), loc.at[a]) for a in range(n)]
        for m in mine:
            m.start()
        first, passed = [], []
        for a in range(n):
            own = None if in_place else blk[a]
            f = [cp(a, 0, me, sib, src=own)]
            f += [cp(a, 1 + j, me, (*ch, c), src=own) for j, ch in enumerate(chips)]
            for d in f:
                d.start()
            first += f
        for a in range(n):
            for j, ch in enumerate(chips):
                cp(a, 1 + j, (*ch, c), me).wait_recv()
                fw = cp(a, 4 + j, (*ch, c), sib)
                fw.start()
                passed.append(fw)
        for a in range(n):
            cp(a, 0, sib, me).wait_recv()
            for j, ch in enumerate(chips):
                cp(a, 4 + j, (*ch, 1 - c), me).wait_recv()
        for d in first + passed:
            d.wait_send()
        for m in mine:
            m.wait()

    return pl.pallas_call(
        body, name=name,
        in_specs=[ANY] * n, out_specs=[ANY] * n,
        out_shape=[_sds(s[0], b.dtype) for s, b in zip(specs, blocks)],
        input_output_aliases={a: a for a in range(n)} if in_place else {},
        scratch_shapes=[pltpu.SemaphoreType.DMA((n, 7)), pltpu.SemaphoreType.DMA((n, 7)),
                        pltpu.SemaphoreType.DMA((n,))],
    )(*blocks)


def split_call(name, arrays, starts=(), waits=(), sems_in=(), new=(), after=None):
    n, nn, ns = len(arrays), len(new), len(starts)
    flat_sems = [s for pair in sems_in for s in pair]

    def body(*refs):
        arr = list(refs[:n])
        sin = refs[n:n + len(flat_sems)]
        outs = refs[n + len(flat_sems) + (after is not None):]
        data = arr + list(outs[n:n + nn])
        for p, k, kind, mk in waits:
            d = mk(data, sin[2 * p].at[k], sin[2 * p + 1].at[k])
            d.wait_send() if kind == "send" else d.wait_recv()
        if ns:
            send, recv = outs[n + nn], outs[n + nn + 1]
            for k, mk in enumerate(starts):
                mk(data, send.at[k], recv.at[k]).start()
        outs[-1][...] = jnp.zeros((8, 128), F32)

    sem_out = [pltpu.SemaphoreType.DMA((ns,)), pltpu.SemaphoreType.DMA((ns,))] if ns else []
    res = pl.pallas_call(
        body, name=name,
        in_specs=[ANY] * n + [SEM] * len(flat_sems) + ([ANY] if after is not None else []),
        out_specs=[ANY] * (n + nn) + [SEM] * len(sem_out) + [pl.BlockSpec(memory_space=pltpu.VMEM)],
        out_shape=[_sds(a.shape, a.dtype) for a in arrays] + list(new) + sem_out + [_sds((8, 128), F32)],
        input_output_aliases={a: a for a in range(n)},
        compiler_params=pltpu.CompilerParams(has_side_effects=pltpu.SideEffectType.DATAFLOW_SIDE_EFFECTING),
    )(*arrays, *flat_sems, *([after] if after is not None else []))
    return list(res[:n + nn]), (tuple(res[n + nn:n + nn + 2]) if ns else None), res[-1]


def _remote(src, dst, send_sem, recv_sem, dev):
    return pltpu.make_async_remote_copy(src_ref=src, dst_ref=dst, send_sem=send_sem, recv_sem=recv_sem,
                                        device_id=dev, device_id_type=MESH_T)


class Gatherer:
    def __init__(self, names, arrays):
        self.names = list(names)
        self.arr = dict(zip(names, arrays))
        self.fwd_sems = {}
        self.next_fwd = self.next_fin = 0
        self.token = None

    @staticmethod
    def _mk_first(ai, spec, k):
        def mk(refs, ss, rs):
            x, y, c = _mesh_pos()
            dev = [(x, y, 1 - c), (1 - x, y, c), (x, 1 - y, c), (1 - x, 1 - y, c)][k]
            win = _window(refs[ai], spec[1], spec[2], 4 * x + 2 * y + c)
            return _remote(win, win, ss, rs, dev)
        return mk

    @staticmethod
    def _mk_fwd(ai, spec, j):
        def mk(refs, ss, rs):
            x, y, c = _mesh_pos()
            px, py = [(1 - x, y), (x, 1 - y), (1 - x, 1 - y)][j]
            win = _window(refs[ai], spec[1], spec[2], 4 * px + 2 * py + c)
            return _remote(win, win, ss, rs, (x, y, 1 - c))
        return mk

    def start(self):
        starts = [self._mk_first(ai, _spec(nm), k) for ai, nm in enumerate(self.names) for k in range(4)]
        arrs, self.first_sems, self.token = split_call("ag_start", [self.arr[nm] for nm in self.names], starts=starts)
        self.arr = dict(zip(self.names, arrs))

    def forward(self, after=None):
        g = self.next_fwd
        if g >= len(AG_GROUPS):
            return
        self.next_fwd += 1
        names = AG_GROUPS[g]
        waits = [(0, 4 * self.names.index(nm) + 1 + j, "recv", self._mk_fwd(ai, _spec(nm), j))
                 for ai, nm in enumerate(names) for j in range(3)]
        starts = [self._mk_fwd(ai, _spec(nm), j) for ai, nm in enumerate(names) for j in range(3)]
        arrs, self.fwd_sems[g], self.token = split_call(
            "ag_forward%d" % g, [self.arr[nm] for nm in names], starts=starts, waits=waits,
            sems_in=[self.first_sems], after=after)
        self.arr.update(zip(names, arrs))
        return self.token

    def finish(self, after=None):
        g = self.next_fin
        self.next_fin += 1
        if self.next_fwd <= g:
            self.forward(after)
        names = AG_GROUPS[g]
        waits = []
        for ai, nm in enumerate(names):
            base = 4 * self.names.index(nm)
            waits.append((0, base, "recv", self._mk_first(ai, _spec(nm), 0)))
            waits += [(1, 3 * ai + j, "recv", self._mk_fwd(ai, _spec(nm), j)) for j in range(3)]
            waits += [(0, base + k, "send", self._mk_first(ai, _spec(nm), k)) for k in range(4)]
            waits += [(1, 3 * ai + j, "send", self._mk_fwd(ai, _spec(nm), j)) for j in range(3)]
        arrs, _, self.token = split_call(
            "ag_finish%d" % g, [self.arr[nm] for nm in names], waits=waits,
            sems_in=[self.first_sems, self.fwd_sems[g]], after=after)
        self.arr.update(zip(names, arrs))
        return {nm: self.arr[nm] for nm in names}


class Reducer:
    def __init__(self, c_arr, q_arr, adam):
        self.c_arr, self.q_arr, self.adam = c_arr, q_arr, adam
        self.groups = []
        self.n = 0
        self.last = None

    @staticmethod
    def _mk1(gi, li, spec, q):
        def mk(refs, ss, rs):
            x, y, c = _mesh_pos()
            return _remote(_window(refs[gi], spec[1], spec[2], 2 * q + (1 - c)), refs[li].at[q], ss, rs, (x, y, 1 - c))
        return mk

    @staticmethod
    def _mk2(si, li, d):
        def mk(refs, ss, rs):
            x, y, c = _mesh_pos()
            qd = lax.rem(2 * x + y + d, 4)
            return _remote(refs[si].at[qd], refs[li].at[3 - d], ss, rs, (lax.div(qd, 2), lax.rem(qd, 2), c))
        return mk

    def add(self, grads, after=None):
        names = list(grads)
        m = len(names)
        starts = [self._mk1(ai, m + ai, BIG[nm], q) for ai, nm in enumerate(names) for q in range(4)]
        new = [_sds((4,) + _win_shape(BIG[nm]), BF16) for nm in names]
        res, sems, tok = split_call("rs1_start%d" % self.n, [grads[nm][1] for nm in names], starts=starts, new=new,
                                    after=after)
        self.groups.append(dict(names=names, g=[grads[nm][0] for nm in names], starts=starts, buf=res, sems=sems,
                                stage=1, age=0, idx=self.n))
        self.n += 1
        return tok

    def step(self, after, drain=False):
        tok = None
        fresh = []
        for grp in self.groups:
            names, m = grp["names"], len(grp["names"])
            if grp["stage"] == 1:
                waits = [(0, k, kind, mk) for k, mk in enumerate(grp["starts"]) for kind in ("send", "recv")]
                res, _, _ = split_call("rs1_wait%d" % grp["idx"], grp["buf"], waits=waits, sems_in=[grp["sems"]], after=after)
                s1, s1b = zip(*[add_pairs(g, l, BIG[nm], self.c_arr) for nm, g, l in zip(names, grp["g"], res[m:])])
                starts = [self._mk2(ai, m + ai, d) for ai in range(m) for d in (1, 2, 3)]
                new = [_sds((3,) + a.shape[1:], BF16) for a in s1b]
                res, sems, tok = split_call("rs2_start%d" % grp["idx"], list(s1b), starts=starts, new=new, after=tok)
                grp.update(stage=2, age=0, s1=s1, starts=starts, buf=res, sems=sems)
                fresh.append(grp["idx"])
        for grp in self.groups:
            names, m = grp["names"], len(grp["names"])
            if grp["stage"] == 2 and grp["idx"] not in fresh and (grp["age"] >= 2 or drain):
                waits = [(0, k, kind, mk) for k, mk in enumerate(grp["starts"]) for kind in ("send", "recv")]
                behind = self.last if self.last is not None else after
                res, _, _ = split_call("rs2_wait%d" % grp["idx"], grp["buf"], waits=waits, sems_in=[grp["sems"]], after=behind)
                for nm, s1, land in zip(names, grp["s1"], res[m:]):
                    self.last = self.adam(nm, s1, land, self.last)
                grp["stage"] = 3
            grp["age"] += 1
        return tok

    def drain(self, after):
        while any(grp["stage"] < 3 for grp in self.groups):
            self.step(after, drain=True)


def pack_weights(args, arg_names, small_blk, names, j_arr):
    n_in = len(args)

    def body(j_ref, *refs):
        for o, nm in enumerate(names):
            dst = refs[n_in + 1 + o]
            if nm == "small":
                dst[...] = refs[n_in][...]
                continue
            _, axis, w, valid, arg, layer = BIG[nm]
            src = refs[arg_names.index(arg)][layer].astype(BF16)
            if valid == w:
                dst[...] = src
            else:
                dst[...] = jnp.zeros(dst.shape, BF16)
                if axis == 1:
                    dst[:, 0:valid] = src
                else:
                    dst[0:valid, :] = src

    def ispec(a):
        return pl.BlockSpec(a.shape, lambda i, j_ref: (0, 0, 0))

    def ospec(spec):
        axis, nd = spec[1], len(spec[0])
        return pl.BlockSpec(_win_shape(spec),
                            lambda i, j_ref, axis=axis, nd=nd: tuple(j_ref[0] if d == axis else 0 for d in range(nd)))

    specs = [_spec(nm) for nm in names]
    return pl.pallas_call(
        body, name="pack_weights",
        grid_spec=pltpu.PrefetchScalarGridSpec(
            num_scalar_prefetch=1, grid=(1,),
            in_specs=[ispec(a) for a in list(args) + [small_blk]], out_specs=[ospec(s) for s in specs]),
        out_shape=[_sds(s[0], F32 if nm == "small" else BF16) for nm, s in zip(names, specs)],
        compiler_params=_cp(("arbitrary",)),
    )(j_arr, *args, small_blk)


def add_pairs(full, land, spec, c_arr):
    axis, w = spec[1], spec[2]
    R, C = full.shape
    if axis == 1:
        tr = min(512, R)
        grid = (4, R // tr)
        fspec = pl.BlockSpec((tr, w), lambda q, i, c: (i, 2 * q + c[0]))
        lspec = pl.BlockSpec((None, tr, w), lambda q, i, c: (q, i, 0))
    else:
        grid = (4, 1)
        fspec = pl.BlockSpec((w, C), lambda q, i, c: (2 * q + c[0], 0))
        lspec = pl.BlockSpec((None, w, C), lambda q, i, c: (q, 0, 0))

    def body(c_ref, a_ref, b_ref, o_ref, ob_ref):
        s = a_ref[...] + b_ref[...].astype(F32)
        o_ref[...] = s
        ob_ref[...] = s.astype(BF16)

    return pl.pallas_call(
        body, name="add_pairs",
        grid_spec=pltpu.PrefetchScalarGridSpec(
            num_scalar_prefetch=1, grid=grid, in_specs=[fspec, lspec], out_specs=[lspec, lspec]),
        out_shape=[_sds(land.shape, F32), _sds(land.shape, BF16)],
        compiler_params=_cp(("arbitrary",) * 2),
    )(c_arr, full, land)


def _adamw(w, g, m, v):
    m = ADAM_B1 * m + (1.0 - ADAM_B1) * g
    v = ADAM_B2 * v + (1.0 - ADAM_B2) * (g * g)
    m_hat = m / (1.0 - ADAM_B1 ** ADAM_STEP)
    v_hat = v / (1.0 - ADAM_B2 ** ADAM_STEP)
    delta = -ADAM_LR * (m_hat / (jnp.sqrt(v_hat) + ADAM_EPS) + ADAM_WD * w)
    return delta, m, v


def reduce_adamw(s1, land, w, m, v, spec, q_arr, prev=None, dep=None):
    axis, win, valid, layer = spec[1], spec[2], spec[3], spec[5]
    L, R, C = w.shape
    if axis == 1:
        tr = min(256, R)
        grid = (R // tr,)
        wspec = pl.BlockSpec((None, tr, win), lambda i, q: (q[0], i, 0))
        lspec = pl.BlockSpec((3, tr, win), lambda i, q: (0, i, 0))
        sspec = pl.BlockSpec((None, tr, C), lambda i, q: (layer, i, 0))
    else:
        grid = (1,)
        wspec = pl.BlockSpec((None, win, C), lambda i, q: (q[0], 0, 0))
        lspec = pl.BlockSpec((3, win, C), lambda i, q: (0, 0, 0))
        sspec = pl.BlockSpec((None, R, C), lambda i, q: (layer, 0, 0))

    def body(q_ref, own_ref, land_ref, w_ref, m_ref, v_ref, *rest):
        g_ref, d_ref, nm_ref, nv_ref = rest[-4:]
        if axis == 1:
            rd = lambda r, *lead: r[(*lead, slice(None), slice(0, valid))]
        else:
            rd = lambda r, *lead: r[(*lead, slice(0, valid), slice(None))]
        g = rd(own_ref)
        for k in range(3):
            g = g + rd(land_ref, k).astype(F32)
        g_ref[...] = g
        d, nm, nv = _adamw(w_ref[...], g, m_ref[...], v_ref[...])
        d_ref[...] = d
        nm_ref[...] = nm
        nv_ref[...] = nv

    extra = (list(prev) if prev is not None else []) + ([dep] if dep is not None else [])
    return pl.pallas_call(
        body, name="reduce_adamw",
        grid_spec=pltpu.PrefetchScalarGridSpec(
            num_scalar_prefetch=1, grid=grid,
            in_specs=[wspec, lspec, sspec, sspec, sspec] + [ANY] * len(extra), out_specs=[sspec] * 4),
        out_shape=[_sds(w.shape)] * 4,
        input_output_aliases={6 + k: k for k in range(4 if prev is not None else 0)},
        compiler_params=_cp(("arbitrary",)),
    )(q_arr, s1, land, w, m, v, *extra)


def small_reduce_adamw(gathered, wmv):
    n, k = len(gathered), len(wmv)

    def body(*refs):
        gs = refs[:n]
        ws = refs[n:n + 3 * k]
        outs = refs[n + 3 * k:]
        for a in range(n):
            g = gs[a][0]
            for dev in range(1, N_DEV):
                g = g + gs[a][dev]
            outs[a][...] = g
            if a < k:
                d, nm, nv = _adamw(ws[3 * a][...], g, ws[3 * a + 1][...], ws[3 * a + 2][...])
                outs[n + 3 * a][...] = d
                outs[n + 3 * a + 1][...] = nm
                outs[n + 3 * a + 2][...] = nv

    flat = [t for tup in wmv for t in tup]
    out_shape = [_sds(g.shape[1:]) for g in gathered] + [_sds(t.shape) for t in flat]
    return pl.pallas_call(body, name="small_reduce_adamw", out_shape=out_shape, compiler_params=_cp())(*gathered, *flat)


def small_adamw(gs, wmv):
    k = len(gs)

    def body(*refs):
        for a in range(k):
            g, w, m, v = refs[4 * a:4 * a + 4]
            d, nm, nv = _adamw(w[...], g[...], m[...], v[...])
            refs[4 * k + 3 * a][...] = d
            refs[4 * k + 3 * a + 1][...] = nm
            refs[4 * k + 3 * a + 2][...] = nv

    args = [t for g, tup in zip(gs, wmv) for t in (g,) + tuple(tup)]
    out_shape = [_sds(g.shape) for g in gs for _ in range(3)]
    return pl.pallas_call(body, name="small_adamw", out_shape=out_shape, compiler_params=_cp())(*args)


WEIGHT_NAMES = ("even_w_in", "even_w_out", "pool_w", "pool_scale", "odd_w_in", "odd_w_out", "conv_dw", "conv_ln_g",
                "conv_ln_b", "sg_ln_g", "sg_ln_b", "sg_w", "sg_b", "ln_mix_g", "ln_mix_b", "ffn_w_gate", "ffn_w_up",
                "ffn_w_down", "ln_ffn_g", "ln_ffn_b", "ple_w_proj", "ple_w_gate", "ple_b_gate")
PACK_ARGS = ("even_w_in", "even_w_out", "odd_w_in", "odd_w_out", "ffn_w_gate", "ffn_w_up", "ffn_w_down",
             "ple_w_gate", "ple_w_proj")
REPLICATED = ("pool_w", "pool_scale", "sg_w", "sg_b", "ln_mix_g", "ln_mix_b", "ln_ffn_g", "ln_ffn_b", "ple_b_gate")
SHARDED_SMALL = ("conv_dw", "conv_ln_g", "conv_ln_b", "sg_ln_g", "sg_ln_b")
NATURAL = {"pool_w": (4, 128, 128), "pool_scale": (1, 512), "sg_w": (4, 128, 128), "sg_b": (4, 128),
           "ln_mix_g": (2, 1024), "ln_mix_b": (2, 1024), "ln_ffn_g": (2, 1024), "ln_ffn_b": (2, 1024),
           "ple_b_gate": (2, 1024)}


def kernel(x, p, even_w_in, even_w_out, pool_w, pool_scale, odd_w_in, odd_w_out, conv_dw, conv_ln_g, conv_ln_b, sg_ln_g, sg_ln_b, sg_w, sg_b, ln_mix_g, ln_mix_b, ffn_w_gate, ffn_w_up, ffn_w_down, ln_ffn_g, ln_ffn_b, ple_w_proj, ple_w_gate, ple_b_gate, loss_target, m_even_w_in, m_even_w_out, m_pool_w, m_pool_scale, m_odd_w_in, m_odd_w_out, m_conv_dw, m_conv_ln_g, m_conv_ln_b, m_sg_ln_g, m_sg_ln_b, m_sg_w, m_sg_b, m_ln_mix_g, m_ln_mix_b, m_ffn_w_gate, m_ffn_w_up, m_ffn_w_down, m_ln_ffn_g, m_ln_ffn_b, m_ple_w_proj, m_ple_w_gate, m_ple_b_gate, v_even_w_in, v_even_w_out, v_pool_w, v_pool_scale, v_odd_w_in, v_odd_w_out, v_conv_dw, v_conv_ln_g, v_conv_ln_b, v_sg_ln_g, v_sg_ln_b, v_sg_w, v_sg_b, v_ln_mix_g, v_ln_mix_b, v_ffn_w_gate, v_ffn_w_up, v_ffn_w_down, v_ln_ffn_g, v_ln_ffn_b, v_ple_w_proj, v_ple_w_gate, v_ple_b_gate):
    A = dict(locals())
    mx, my, mc = _mesh_pos()
    j_arr = (4 * mx + 2 * my + mc).astype(jnp.int32).reshape(1)
    q_arr = (2 * mx + my).astype(jnp.int32).reshape(1)
    c_arr = mc.astype(jnp.int32).reshape(1)
    res = {}

    def adam(nm, s1, land, dep):
        arg = BIG[nm][4]
        res[arg] = reduce_adamw(s1, land, A[arg], A["m_" + arg], A["v_" + arg], BIG[nm], q_arr, res.get(arg), dep)
        return res[arg][0]

    class Comm:
        def __init__(self):
            names = [nm for g in AG_GROUPS for nm in g]
            small_blk = jnp.concatenate([conv_dw[0], conv_ln_g, conv_ln_b, sg_ln_g, sg_ln_b, jnp.zeros((5, 64), F32)], axis=0)
            mine = pack_weights([A[k] for k in PACK_ARGS], PACK_ARGS, small_blk[None], names, j_arr)
            self.gat = Gatherer(names, mine)
            self.gat.start()
            self.red = Reducer(c_arr, q_arr, adam)
            self.W = {k: A[k].reshape(NATURAL[k]) for k in REPLICATED}

        def weights(self, stage, i, after):
            got = self.gat.finish(after)
            if "small" in got:
                sm = got.pop("small").transpose(1, 0, 2).reshape(40, 512)
                got.update(conv_dw=sm[0:31], conv_ln_g=sm[31:32], conv_ln_b=sm[32:33], sg_ln_g=sm[33:34], sg_ln_b=sm[34:35])
            self.W.update(got)
            return self.W

        def all_weights(self):
            return self.W

        def poke(self, after):
            if self.gat.next_fin < len(AG_GROUPS):
                return self.gat.forward(after)
            return self.red.step(after)

        def grads(self, grads):
            tok = self.red.step(next(iter(grads.values()))[0])
            return self.red.add(grads, after=tok)

    comm = Comm()
    sq, dx, small = run_layers(x[0], p[:, 0], loss_target[0], comm)
    loss = lax.psum(0.5 * jnp.sum(sq) / x.shape[-1], ("x", "y", "c"))

    comm.red.step(dx)

    names = REPLICATED + SHARDED_SMALL
    gathered = all_gather("ag_small_grads", [small[k][None] for k in names],
                          [((N_DEV,) + small[k].shape, 0, 1, 1) for k in names])
    wmv = [tuple(A[pre + k].reshape(NATURAL[k]) for pre in ("", "m_", "v_")) for k in REPLICATED]
    outs = small_reduce_adamw(gathered, wmv)
    gsum = dict(zip(names, outs[:len(names)]))
    for a, k in enumerate(REPLICATED):
        res[k] = tuple(t.reshape(A[k].shape) for t in (gsum[k],) + tuple(outs[len(names) + 3 * a:len(names) + 3 * a + 3]))
    j = 4 * lax.axis_index("x") + 2 * lax.axis_index("y") + lax.axis_index("c")
    gsh = [lax.dynamic_slice_in_dim(gsum[k], j * 64, 64, axis=1).reshape(A[k].shape) for k in SHARDED_SMALL]
    outs = small_adamw(gsh, [(A[k], A["m_" + k], A["v_" + k]) for k in SHARDED_SMALL])
    for a, k in enumerate(SHARDED_SMALL):
        res[k] = (gsh[a],) + tuple(outs[3 * a:3 * a + 3])
    comm.red.last = outs[0]
    comm.red.drain(dx)

    out = [loss, dx[None]]
    for part in range(4):
        out += [res[k][part] for k in WEIGHT_NAMES]
    return tuple(out)
```

```python
import functools
import math

import jax
import jax.numpy as jnp
from jax import lax
from jax.experimental import pallas as pl
from jax.experimental.pallas import tpu as pltpu

F32, BF16 = jnp.float32, jnp.bfloat16
ALPHA = 4.0 ** 0.25
LN_EPS = 1e-5
QK_SCALE = 0.125
POOL_WINDOWS = (2, 4, 8, 16)
CONV_TAPS = 31
N_DEV = 8
FF_SHARD, FF_PAD = 352, 384
ADAM_LR, ADAM_B1, ADAM_B2, ADAM_EPS, ADAM_WD, ADAM_STEP = 0.001, 0.9, 0.999, 1e-08, 0.01, 10
VMEM_LIMIT = 56 * 1024 * 1024
MESH_T = pl.DeviceIdType.MESH


def _cp(sem=None):
    return pltpu.CompilerParams(dimension_semantics=sem, vmem_limit_bytes=VMEM_LIMIT)


def _dot(a, b):
    return jnp.dot(a, b, preferred_element_type=F32)


def _dot_nt(a, b):
    return lax.dot_general(a, b, (((1,), (1,)), ((), ())), preferred_element_type=F32)


def _dot_tn(a, b):
    return lax.dot_general(a, b, (((0,), (0,)), ((), ())), preferred_element_type=F32)


def _sigmoid(x):
    return 1.0 / (1.0 + jnp.exp(-x))


def _softplus(z):
    return jnp.maximum(z, 0.0) + jnp.log(1.0 + jnp.exp(-jnp.abs(z)))


_GELU_C = math.sqrt(2.0 / math.pi)


def _gelu(x):
    return 0.5 * x * (1.0 + jnp.tanh(_GELU_C * (x + 0.044715 * x * x * x)))


def _gelu_grad(x):
    t = jnp.tanh(_GELU_C * (x + 0.044715 * x * x * x))
    return 0.5 * (1.0 + t) + 0.5 * x * (1.0 - t * t) * _GELU_C * (1.0 + 3.0 * 0.044715 * x * x)


def _ln_fwd(r, g, b):
    mu = jnp.mean(r, axis=-1, keepdims=True)
    xc = r - mu
    var = jnp.mean(xc * xc, axis=-1, keepdims=True)
    rstd = lax.rsqrt(var + LN_EPS)
    xh = xc * rstd
    return xh * g + b, xh, rstd


def _ln_bwd(dy, xh, rstd, g):
    dxh = dy * g
    m1 = jnp.mean(dxh, axis=-1, keepdims=True)
    m2 = jnp.mean(dxh * xh, axis=-1, keepdims=True)
    return rstd * (dxh - m1 - xh * m2)


def _split2(x):
    hi = x.astype(BF16)
    lo = (x - hi.astype(F32)).astype(BF16)
    return hi, lo


def _colsum(x):
    return jnp.sum(x, axis=0, keepdims=True)


def _tok_call(name, body, tiled, full, out_tiled, out_acc=(), tm=256, scratch=(), dep=None):
    def arr(t):
        return t[0] if isinstance(t, tuple) else t
    full = [t[0] if isinstance(t, tuple) and t[1] is None else t for t in full]
    S = arr(tiled[0]).shape[0]
    tm = min(tm, S)
    n_in = len(tiled) + len(full)
    deps = [] if dep is None else [dep]
    if deps:
        inner = body
        body = lambda *refs: inner(*refs[:n_in], *refs[n_in + 1:])

    def tspec(t):
        if isinstance(t, tuple):
            _, w, cb = t
            return pl.BlockSpec((tm, w), lambda i, cb=cb: (i, cb))
        return pl.BlockSpec((tm, t.shape[1]), lambda i: (i, 0))

    def fspec(t):
        if isinstance(t, tuple):
            a, l = t
            nd = a.ndim - 1
            return pl.BlockSpec((None,) + a.shape[1:], lambda i, l=l, nd=nd: (l,) + (0,) * nd)
        nd = t.ndim
        return pl.BlockSpec(t.shape, lambda i, nd=nd: (0,) * nd)

    def ospec(o):
        return pl.BlockSpec((tm, o.shape[1]), lambda i: (i, 0))

    def aspec(o):
        nd = len(o.shape)
        return pl.BlockSpec(o.shape, lambda i, nd=nd: (0,) * nd)

    outs = pl.pallas_call(
        body, name=name, grid=(S // tm,),
        in_specs=[tspec(t) for t in tiled] + [fspec(t) for t in full] + [ANY] * len(deps),
        out_specs=[ospec(o) for o in out_tiled] + [aspec(o) for o in out_acc],
        out_shape=list(out_tiled) + list(out_acc),
        scratch_shapes=list(scratch),
        compiler_params=_cp(("arbitrary",)),
    )(*[arr(t) for t in tiled], *[arr(t) for t in full], *deps)
    return outs


def _sds(shape, dtype=F32):
    return jax.ShapeDtypeStruct(tuple(shape), dtype)


def _acc(ref, val):
    @pl.when(pl.program_id(0) == 0)
    def _():
        ref[...] = val

    @pl.when(pl.program_id(0) != 0)
    def _():
        ref[...] += val


def mm_in(x, w, nb16=0):
    S, N = x.shape[0], w.shape[1]

    def body(x_ref, w_ref, h_ref, xb_ref, *hb_ref):
        xb = x_ref[...].astype(BF16)
        xb_ref[...] = xb
        h = _dot(xb, w_ref[...])
        h_ref[...] = h
        if nb16:
            hb_ref[0][...] = h[:, 0:nb16].astype(BF16)

    outs = [_sds((S, N)), _sds((S, x.shape[1]), BF16)] + ([_sds((S, nb16), BF16)] if nb16 else [])
    return _tok_call("mm_in", body, [x], [w], outs, tm=512)


def _stack_heads(x, hm0, dtype=BF16):
    return jnp.concatenate([jnp.where(hm0, x, 0), jnp.where(hm0, 0, x)], axis=0).astype(dtype)


def _unstack_k(x, T):
    return jnp.concatenate([x[0:T], x[T:2 * T]], axis=1)


def _cumsum_mm(x, u):
    n = x.shape[0]
    hi, lo = _split2(x)
    r = _dot(jnp.concatenate([hi, lo], axis=0), u)
    return r[0:n] + r[n:2 * n]


def attn_fwd(qkv, T=256):
    S = qkv.shape[0]
    T = min(T, S)
    nq = S // T

    def body(q_ref, k_ref, v_ref, o_ref, t_ref, acc_ref, c_ref, qh_ref):
        i = pl.program_id(0)
        hm0 = lax.broadcasted_iota(jnp.int32, (1, 128), 1) < 64
        r2 = lax.broadcasted_iota(jnp.int32, (2 * T, T), 0)
        c2 = lax.broadcasted_iota(jnp.int32, (2 * T, T), 1)
        causal = c2 < jnp.where(r2 >= T, r2 - T, r2)
        ur = lax.broadcasted_iota(jnp.int32, (T, T), 0)
        uc = lax.broadcasted_iota(jnp.int32, (T, T), 1)
        u_incl = (ur >= uc).astype(BF16)
        acc_ref[...] = jnp.zeros_like(acc_ref)
        c_ref[...] = jnp.zeros_like(c_ref)
        for pp in range(4):
            qh_ref[pp] = _stack_heads(q_ref[:, pp * 128:(pp + 1) * 128] * QK_SCALE, hm0)

        def block(kb, diag):
            ks = pl.multiple_of(kb * T, T)
            for pp in range(4):
                cs = slice(pp * 128, (pp + 1) * 128)
                z = _dot_nt(qh_ref[pp], k_ref[pl.ds(ks, T), cs])
                sp = _softplus(z)
                lk = jnp.where(causal, -sp, 0.0) if diag else -sp
                incl = _cumsum_mm(lk, u_incl)
                c = c_ref[pp]
                w = jnp.exp((z - sp) + c + (incl - lk))
                if diag:
                    w = jnp.where(causal, w, 0.0)
                acc_ref[:, cs] += _dot(_unstack_k(w.astype(BF16), T), _stack_heads(v_ref[pl.ds(ks, T), cs], hm0))
                c_ref[pp] = c + jnp.broadcast_to(incl[:, 0:1], (2 * T, T))

        block(i, True)

        def step(jj, carry):
            block(i - 1 - jj, False)
            return carry

        lax.fori_loop(0, i, step, 0)
        o_ref[...] = acc_ref[...].astype(BF16)
        for pp in range(4):
            for hd in range(2):
                t_ref[2 * pp + hd] = c_ref[pp, hd * T:(hd + 1) * T, 0:128]

    return pl.pallas_call(
        body, name="attn_fwd", grid=(nq,),
        in_specs=[pl.BlockSpec((T, 512), lambda i: (i, 0)),
                  pl.BlockSpec((S, 512), lambda i: (0, 1)),
                  pl.BlockSpec((S, 512), lambda i: (0, 2))],
        out_specs=[pl.BlockSpec((T, 512), lambda i: (i, 0)),
                   pl.BlockSpec((8, T, 128), lambda i: (0, i, 0))],
        out_shape=[_sds((S, 512), BF16), _sds((8, S, 128))],
        scratch_shapes=[pltpu.VMEM((T, 512), F32), pltpu.VMEM((4, 2 * T, T), F32), pltpu.VMEM((4, 2 * T, 128), BF16)],
        compiler_params=_cp(("arbitrary",)),
    )(qkv, qkv, qkv)


def pool_fwd(h, pool_w, pool_scale, CH=256):
    S = h.shape[0]
    CH = min(CH, S)

    def body(u_ref, w_ref, sc_ref, b_ref, pooled_ref, pad_ref):
        pad_ref[0:16, :] = jnp.zeros((16, 512), F32)
        pad_ref[16:16 + S, :] = u_ref[...]
        for g, win in enumerate(POOL_WINDOWS):
            cs = slice(g * 128, (g + 1) * 128)
            wq = w_ref[g].astype(BF16)
            for ch in range(S // CH):
                base = ch * CH
                acc = pad_ref[16 + base:16 + base + CH, cs]
                for sft in range(1, win):
                    acc = acc + pad_ref[16 + base - sft:16 + base - sft + CH, cs]
                t = base + lax.broadcasted_iota(jnp.int32, (CH, 1), 0)
                cnt = jnp.minimum(t + 1, win).astype(F32)
                pooled = (acc / cnt - pad_ref[16 + base:16 + base + CH, cs]).astype(BF16)
                pooled_ref[base:base + CH, cs] = pooled
                b_ref[base:base + CH, cs] = (_dot(pooled, wq) * sc_ref[:, cs]).astype(BF16)

    return pl.pallas_call(
        body, name="pool_fwd", grid=(1,),
        in_specs=[pl.BlockSpec((S, 512), lambda i: (0, 3)),
                  pl.BlockSpec((4, 128, 128), lambda i: (0, 0, 0)),
                  pl.BlockSpec((1, 512), lambda i: (0, 0))],
        out_specs=[pl.BlockSpec((S, 512), lambda i: (0, 0)), pl.BlockSpec((S, 512), lambda i: (0, 0))],
        out_shape=[_sds((S, 512), BF16), _sds((S, 512), BF16)],
        scratch_shapes=[pltpu.VMEM((S + 16, 512), F32)],
        compiler_params=_cp(("arbitrary",)),
    )(h, pool_w, pool_scale)


def conv_fwd(h, dw, CH=128):
    S = h.shape[0]

    def body(a_ref, g_ref, dw_ref, y_ref, hc_ref, pad_ref):
        hc = a_ref[...] * _sigmoid(g_ref[...])
        hc_ref[...] = hc
        pad_ref[0:32, :] = jnp.zeros((32, 128), F32)
        pad_ref[32:32 + S, :] = hc
        for ch in range(S // CH):
            base = ch * CH + 2
            acc = dw_ref[0:1, :] * pad_ref[base:base + CH, :]
            for k in range(1, CONV_TAPS):
                acc = acc + dw_ref[k:k + 1, :] * pad_ref[base + k:base + k + CH, :]
            y_ref[ch * CH:(ch + 1) * CH, :] = acc

    return pl.pallas_call(
        body, name="conv_fwd", grid=(4,),
        in_specs=[pl.BlockSpec((S, 128), lambda c: (0, c)),
                  pl.BlockSpec((S, 128), lambda c: (0, 4 + c)),
                  pl.BlockSpec((CONV_TAPS, 128), lambda c: (0, c))],
        out_specs=[pl.BlockSpec((S, 128), lambda c: (0, c)), pl.BlockSpec((S, 128), lambda c: (0, c))],
        out_shape=[_sds((S, 512)), _sds((S, 512))],
        scratch_shapes=[pltpu.VMEM((S + 32, 128), F32)],
        compiler_params=_cp(("arbitrary",)),
    )(h, h, dw)


def _masked_sg_w(w_ref, g):
    row = lax.broadcasted_iota(jnp.int32, (128, 128), 0)
    col = lax.broadcasted_iota(jnp.int32, (128, 128), 1)
    return jnp.where(row >= col, w_ref[g], 0.0).astype(BF16)


def odd_post(y, h, cl_g, cl_b, sl_g, sl_b, sg_w, sgb_bc, tm=256):
    S = y.shape[0]
    tm = min(tm, S)

    def body(y_ref, zc_ref, clg, clb, slg, slb, w_ref, sb_ref,
             c_ref, d_ref, xhc_ref, rsc_ref, xhv_ref, rsv_ref, sv_ref):
        lnc, xhc, rsc = _ln_fwd(y_ref[...], clg[...], clb[...])
        c_ref[...] = (lnc * _sigmoid(lnc)).astype(BF16)
        xhc_ref[...] = xhc
        rsc_ref[...] = rsc
        z = _gelu(zc_ref[...])
        vn, xhv, rsv = _ln_fwd(z[:, 512:], slg[...], slb[...])
        xhv_ref[...] = xhv
        rsv_ref[...] = rsv
        vnb = vn.astype(BF16)
        for g in range(4):
            wm = _masked_sg_w(w_ref, g)
            for ch in range(tm // 128):
                rs, cs = slice(ch * 128, (ch + 1) * 128), slice(g * 128, (g + 1) * 128)
                sv_ref[rs, cs] = _dot(wm, vnb[rs, cs]) + sb_ref[g]
        d_ref[...] = (z[:, :512] * sv_ref[...]).astype(BF16)

    return _tok_call(
        "odd_post", body, [y, (h, 1024, 1)], [cl_g, cl_b, sl_g, sl_b, sg_w, sgb_bc],
        [_sds((S, 512), BF16), _sds((S, 512), BF16), _sds((S, 512)), _sds((S, 1)),
         _sds((S, 512)), _sds((S, 1)), _sds((S, 512))], tm=tm)


def mm_out_ln(l1, l2, x, w, g, b, dep=None):
    S, D = x.shape

    def body(l1_ref, l2_ref, x_ref, w_ref, g_ref, b_ref, y_ref, xh_ref, rs_ref):
        mix = _dot(l1_ref[...], w_ref[0:512, :]) + _dot(l2_ref[...], w_ref[512:1024, :])
        y, xh, rs = _ln_fwd(ALPHA * x_ref[...] + mix, g_ref[...], b_ref[...])
        y_ref[...] = y
        xh_ref[...] = xh
        rs_ref[...] = rs

    return _tok_call("mm_out_ln", body, [l1, l2, x], [w, g, b],
                     [_sds((S, D)), _sds((S, D)), _sds((S, 1))], dep=dep)


def ffn_up(x1, wg, wu, layer):
    S, D = x1.shape
    F = wg.shape[-1]

    def body(x_ref, wg_ref, wu_ref, gate_ref, up_ref, hb_ref, xb_ref):
        xb = x_ref[...].astype(BF16)
        xb_ref[...] = xb
        gate = _dot(xb, wg_ref[...])
        up = _dot(xb, wu_ref[...])
        gate_ref[...] = gate
        up_ref[...] = up
        hb_ref[...] = (gate * _sigmoid(gate) * up).astype(BF16)

    return _tok_call("ffn_up", body, [x1], [(wg, layer), (wu, layer)],
                     [_sds((S, F)), _sds((S, F)), _sds((S, F), BF16), _sds((S, D), BF16)])


def ffn_down_ln(hb, x1, wd, layer, g, b):
    S, D = x1.shape

    def body(h_ref, x_ref, w_ref, g_ref, b_ref, y_ref, xh_ref, rs_ref):
        f = _dot(h_ref[...], w_ref[...])
        y, xh, rs = _ln_fwd(ALPHA * x_ref[...] + f, g_ref[...], b_ref[...])
        y_ref[...] = y
        xh_ref[...] = xh
        rs_ref[...] = rs

    return _tok_call("ffn_down_ln", body, [hb, x1], [(wd, layer), g, b],
                     [_sds((S, D)), _sds((S, D)), _sds((S, 1))])


def ple_fwd(x2, p, wpg, wpp, layer, bg, target=None, dep=None):
    S, D = x2.shape
    last = target is not None

    def body(*refs):
        if last:
            x_ref, p_ref, t_ref, wg_ref, wp_ref, b_ref, x3_ref, sg_ref, pp_ref, xb_ref, pb_ref, dy_ref, ls_ref = refs
        else:
            x_ref, p_ref, wg_ref, wp_ref, b_ref, x3_ref, sg_ref, pp_ref, xb_ref, pb_ref = refs
        x = x_ref[...]
        xb = x.astype(BF16)
        pb = p_ref[...].astype(BF16)
        xb_ref[...] = xb
        pb_ref[...] = pb
        sg = _sigmoid(_dot(xb, wg_ref[...]) + b_ref[...])
        pp = _dot(pb, wp_ref[...])
        sg_ref[...] = sg
        pp_ref[...] = pp
        x3 = x + sg * pp
        x3_ref[...] = x3
        if last:
            err = x3 - t_ref[...]
            dy_ref[...] = err * (1.0 / D)
            _acc(ls_ref, _colsum(err * err))

    outs = [_sds((S, D)), _sds((S, D)), _sds((S, D)), _sds((S, D), BF16), _sds((S, p.shape[1]), BF16)]
    tiled = [x2, p] + ([target] if last else [])
    if last:
        outs.append(_sds((S, D)))
    return _tok_call("ple_fwd", body, tiled, [(wpg, layer), (wpp, layer), bg], outs,
                     [_sds((1, D))] if last else [], dep=dep)


def ple_bwd(dx3, sg, pp, wpg, layer, dep=None):
    S, D = dx3.shape

    def body(d_ref, sg_ref, pp_ref, w_ref, dx_ref, dgp_ref, dpp_ref, dbg_ref):
        d, sg = d_ref[...], sg_ref[...]
        dgp = d * pp_ref[...] * sg * (1.0 - sg)
        dgpb = dgp.astype(BF16)
        dgp_ref[...] = dgpb
        dpp_ref[...] = (d * sg).astype(BF16)
        dx_ref[...] = d + _dot_nt(dgpb, w_ref[...])
        _acc(dbg_ref, _colsum(dgp))

    return _tok_call("ple_bwd", body, [dx3, sg, pp], [(wpg, layer)],
                     [_sds((S, D)), _sds((S, D), BF16), _sds((S, D), BF16)], [_sds((1, D))], dep=dep)


def ffn_bwd_a(dx2, xh, rs, g, gate, up, wd, layer):
    S, D = dx2.shape
    F = gate.shape[1]

    def body(d_ref, xh_ref, rs_ref, gate_ref, up_ref, g_ref, w_ref,
             dr_ref, drb_ref, dg_ref, du_ref, dlg_ref, dlb_ref):
        d, xh = d_ref[...], xh_ref[...]
        dr = _ln_bwd(d, xh, rs_ref[...], g_ref[...])
        drb = dr.astype(BF16)
        dr_ref[...] = dr
        drb_ref[...] = drb
        _acc(dlg_ref, _colsum(d * xh))
        _acc(dlb_ref, _colsum(d))
        dh = _dot_nt(drb, w_ref[...])
        gate, up = gate_ref[...], up_ref[...]
        s = _sigmoid(gate)
        dg_ref[...] = (dh * up * s * (1.0 + gate * (1.0 - s))).astype(BF16)
        du_ref[...] = (dh * gate * s).astype(BF16)

    return _tok_call("ffn_bwd_a", body, [dx2, xh, rs, gate, up], [g, (wd, layer)],
                     [_sds((S, D)), _sds((S, D), BF16), _sds((S, F), BF16), _sds((S, F), BF16)],
                     [_sds((1, D)), _sds((1, D))])


def ffn_bwd_b(dr, dgate_b, dup_b, wg, wu, layer, dep=None):
    S, D = dr.shape

    def body(dr_ref, dg_ref, du_ref, wg_ref, wu_ref, dx_ref):
        dx_ref[...] = (ALPHA * dr_ref[...] + _dot_nt(dg_ref[...], wg_ref[...])
                       + _dot_nt(du_ref[...], wu_ref[...]))

    return _tok_call("ffn_bwd_b", body, [dr, dgate_b, dup_b], [(wg, layer), (wu, layer)], [_sds((S, D))], dep=dep)[0]


def mix_bwd(dx1, xh, rs, g, w):
    S, D = dx1.shape

    def body(d_ref, xh_ref, rs_ref, g_ref, w_ref, dr_ref, dmb_ref, dl_ref, dlg_ref, dlb_ref):
        d, xh = d_ref[...], xh_ref[...]
        dr = _ln_bwd(d, xh, rs_ref[...], g_ref[...])
        drb = dr.astype(BF16)
        dr_ref[...] = dr
        dmb_ref[...] = drb
        dl_ref[...] = _dot_nt(drb, w_ref[...])
        _acc(dlg_ref, _colsum(d * xh))
        _acc(dlb_ref, _colsum(d))

    return _tok_call("mix_bwd", body, [dx1, xh, rs], [g, w],
                     [_sds((S, D)), _sds((S, D), BF16), _sds((S, D))], [_sds((1, D)), _sds((1, D))])


def dx_in(dr, pieces, w):
    S, D = dr.shape
    offs = [o for _, o in pieces]
    widths = [a.shape[1] for a, _ in pieces]

    def body(*refs):
        dr_ref, prefs, w_ref, dx_ref = refs[0], refs[1:1 + len(pieces)], refs[-2], refs[-1]
        acc = ALPHA * dr_ref[...]
        for pr, o, n in zip(prefs, offs, widths):
            acc = acc + _dot_nt(pr[...], w_ref[:, o:o + n])
        dx_ref[...] = acc

    return _tok_call("dx_in", body, [dr] + [a for a, _ in pieces], [w], [_sds((S, D))])[0]


def odd_post_bwd(dl, h, xhc, rsc, xhv, rsv, sv, cl_g, cl_b, sl_g, sl_b, sg_w, tm=256, dep=None):
    S = dl.shape[0]
    tm = min(tm, S)

    def body(dl_ref, zc_ref, xhc_ref, rsc_ref, xhv_ref, rsv_ref, sv_ref, clg, clb, slg, slb, w_ref,
             dy_ref, dzc_ref, dclg_ref, dclb_ref, dslg_ref, dslb_ref, dwm_ref, dsb_ref, dvn_ref):
        first = pl.program_id(0) == 0
        last = pl.program_id(0) == pl.num_programs(0) - 1
        dc, dd = dl_ref[:, 0:512], dl_ref[:, 512:1024]
        xhc = xhc_ref[...]
        lnc = xhc * clg[...] + clb[...]
        s = _sigmoid(lnc)
        dlnc = dc * s * (1.0 + lnc * (1.0 - s))
        dy_ref[...] = _ln_bwd(dlnc, xhc, rsc_ref[...], clg[...])
        _acc(dclg_ref, _colsum(dlnc * xhc))
        _acc(dclb_ref, _colsum(dlnc))
        zc = zc_ref[...]
        z = _gelu(zc)
        dsv = dd * z[:, :512]
        dsvb = dsv.astype(BF16)
        xhv = xhv_ref[...]
        vnb = (xhv * slg[...] + slb[...]).astype(BF16)

        @pl.when(first)
        def _():
            dwm_ref[...] = jnp.zeros_like(dwm_ref)
            dsb_ref[...] = jnp.zeros_like(dsb_ref)

        for g in range(4):
            wm = _masked_sg_w(w_ref, g)
            for ch in range(tm // 128):
                rs_, cs = slice(ch * 128, (ch + 1) * 128), slice(g * 128, (g + 1) * 128)
                dwm_ref[g] += _dot_nt(dsvb[rs_, cs], vnb[rs_, cs])
                dvn_ref[rs_, cs] = _dot_tn(wm, dsvb[rs_, cs])
                dsb_ref[g] += dsv[rs_, cs]
        dvn = dvn_ref[...]
        dvv = _ln_bwd(dvn, xhv, rsv_ref[...], slg[...])
        _acc(dslg_ref, _colsum(dvn * xhv))
        _acc(dslb_ref, _colsum(dvn))
        gg = _gelu_grad(zc)
        dzc_ref[:, 0:512] = (dd * sv_ref[...] * gg[:, :512]).astype(BF16)
        dzc_ref[:, 512:1024] = (dvv * gg[:, 512:]).astype(BF16)

        @pl.when(last)
        def _():
            row = lax.broadcasted_iota(jnp.int32, (128, 128), 0)
            col = lax.broadcasted_iota(jnp.int32, (128, 128), 1)
            for g in range(4):
                dwm_ref[g] = jnp.where(row >= col, dwm_ref[g], 0.0)
                dsb_ref[g] = jnp.broadcast_to(jnp.sum(dsb_ref[g], axis=1, keepdims=True), (128, 128))

    return _tok_call(
        "odd_post_bwd", body, [dl, (h, 1024, 1), xhc, rsc, xhv, rsv, sv], [cl_g, cl_b, sl_g, sl_b, sg_w],
        [_sds((S, 512)), _sds((S, 1024), BF16)],
        [_sds((1, 512)), _sds((1, 512)), _sds((1, 512)), _sds((1, 512)), _sds((4, 128, 128)), _sds((4, 128, 128))],
        tm=tm, scratch=[pltpu.VMEM((tm, 512), F32)], dep=dep)


def conv_bwd(dy, hc, h, dw, CH=128):
    S = dy.shape[0]

    def body(dy_ref, hc_ref, a_ref, g_ref, dw_ref, da_ref, dg_ref, ddw_ref, padh_ref, padd_ref, dhc_ref):
        padh_ref[0:32, :] = jnp.zeros((32, 128), F32)
        padh_ref[32:32 + S, :] = hc_ref[...]
        padd_ref[0:S, :] = dy_ref[...]
        padd_ref[S:S + 32, :] = jnp.zeros((32, 128), F32)
        taps = [jnp.zeros((1, 128), F32) for _ in range(CONV_TAPS)]
        for ch in range(S // CH):
            b0 = ch * CH
            dyc = padd_ref[b0:b0 + CH, :]
            acc = dw_ref[0:1, :] * padd_ref[b0 + 30:b0 + 30 + CH, :]
            taps[0] = taps[0] + _colsum(dyc * padh_ref[b0 + 2:b0 + 2 + CH, :])
            for k in range(1, CONV_TAPS):
                acc = acc + dw_ref[k:k + 1, :] * padd_ref[b0 + 30 - k:b0 + 30 - k + CH, :]
                taps[k] = taps[k] + _colsum(dyc * padh_ref[b0 + 2 + k:b0 + 2 + k + CH, :])
            dhc_ref[b0:b0 + CH, :] = acc
        for k in range(CONV_TAPS):
            ddw_ref[k:k + 1, :] = taps[k]
        dhc = dhc_ref[...]
        s = _sigmoid(g_ref[...])
        da_ref[...] = (dhc * s).astype(BF16)
        dg_ref[...] = (dhc * a_ref[...] * s * (1.0 - s)).astype(BF16)

    return pl.pallas_call(
        body, name="conv_bwd", grid=(4,),
        in_specs=[pl.BlockSpec((S, 128), lambda c: (0, c)),
                  pl.BlockSpec((S, 128), lambda c: (0, c)),
                  pl.BlockSpec((S, 128), lambda c: (0, c)),
                  pl.BlockSpec((S, 128), lambda c: (0, 4 + c)),
                  pl.BlockSpec((CONV_TAPS, 128), lambda c: (0, c))],
        out_specs=[pl.BlockSpec((S, 128), lambda c: (0, c)), pl.BlockSpec((S, 128), lambda c: (0, c)),
                   pl.BlockSpec((CONV_TAPS, 128), lambda c: (0, c))],
        out_shape=[_sds((S, 512), BF16), _sds((S, 512), BF16), _sds((CONV_TAPS, 512))],
        scratch_shapes=[pltpu.VMEM((S + 32, 128), F32), pltpu.VMEM((S + 32, 128), F32), pltpu.VMEM((S, 128), F32)],
        compiler_params=_cp(("arbitrary",)),
    )(dy, hc, h, h, dw)


def attn_bwd(qkv, dl, tb, T=256, dep=None):
    S = qkv.shape[0]
    T = min(T, S)
    nq = S // T

    def body(q_ref, k_ref, v_ref, do_ref, t_ref, dq_ref, dk_ref, dv_ref,
             dka_ref, dva_ref, dqa_ref, pc_ref, gc_ref, tot_ref, qh_ref, doh_ref):
        i = pl.program_id(0)
        hm0 = lax.broadcasted_iota(jnp.int32, (1, 128), 1) < 64
        r2 = lax.broadcasted_iota(jnp.int32, (2 * T, T), 0)
        c2 = lax.broadcasted_iota(jnp.int32, (2 * T, T), 1)
        causal = c2 < jnp.where(r2 >= T, r2 - T, r2)
        ur = lax.broadcasted_iota(jnp.int32, (T, T), 0)
        uc = lax.broadcasted_iota(jnp.int32, (T, T), 1)
        u_le = (ur <= uc).astype(BF16)
        u_lt = (ur < uc).astype(BF16)

        @pl.when(i == 0)
        def _():
            dka_ref[...] = jnp.zeros_like(dka_ref)
            dva_ref[...] = jnp.zeros_like(dva_ref)

        dqa_ref[...] = jnp.zeros_like(dqa_ref)
        pc_ref[...] = jnp.zeros_like(pc_ref)
        gc_ref[...] = jnp.zeros_like(gc_ref)
        for pp in range(4):
            cs = slice(pp * 128, (pp + 1) * 128)
            qh_ref[pp] = _stack_heads(q_ref[:, cs] * QK_SCALE, hm0)
            doh_ref[pp] = _stack_heads(do_ref[:, cs], hm0)
            for hd in range(2):
                for half in range(T // 128):
                    tot_ref[pp, hd * T:(hd + 1) * T, half * 128:(half + 1) * 128] = t_ref[2 * pp + hd]

        def block(kb, diag):
            ks = pl.multiple_of(kb * T, T)
            for pp in range(4):
                cs = slice(pp * 128, (pp + 1) * 128)
                kb16 = k_ref[pl.ds(ks, T), cs]
                vb16 = v_ref[pl.ds(ks, T), cs]
                qh, doh = qh_ref[pp], doh_ref[pp]
                z = _dot_nt(qh, kb16)
                sp = _softplus(z)
                lk = jnp.where(causal, -sp, 0.0) if diag else -sp
                pre = _cumsum_mm(lk, u_le)
                sig = jnp.exp(z - sp)
                pc = pc_ref[pp]
                w = sig * jnp.exp(tot_ref[pp] - pc - pre)
                if diag:
                    w = jnp.where(causal, w, 0.0)
                gmat = _dot_nt(doh, vb16) * w
                gex = gc_ref[pp] + _cumsum_mm(gmat, u_lt)
                dz = gmat * (1.0 - sig) - sig * gex
                if diag:
                    dz = jnp.where(causal, dz, 0.0)
                dzb = dz.astype(BF16)
                dqa_ref[:, cs] += _dot(_unstack_k(dzb, T), _stack_heads(kb16, hm0))
                dka_ref[pl.ds(ks, T), cs] += _dot_tn(dzb, qh)
                dva_ref[pl.ds(ks, T), cs] += _dot_tn(w.astype(BF16), doh)
                pc_ref[pp] = pc + jnp.broadcast_to(pre[:, T - 1:T], (2 * T, T))
                gc_ref[pp] = jnp.broadcast_to(gex[:, T - 1:T] + gmat[:, T - 1:T], (2 * T, T))

        def step(kb, carry):
            block(kb, False)
            return carry

        lax.fori_loop(0, i, step, 0)
        block(i, True)
        dq_ref[...] = (dqa_ref[...] * QK_SCALE).astype(BF16)

        @pl.when(i == nq - 1)
        def _():
            dk_ref[...] = dka_ref[...].astype(BF16)
            dv_ref[...] = dva_ref[...].astype(BF16)

    deps = [] if dep is None else [dep]
    call_body = body if dep is None else (lambda *refs: body(*refs[:5], *refs[6:]))
    return pl.pallas_call(
        call_body, name="attn_bwd", grid=(nq,),
        in_specs=[pl.BlockSpec((T, 512), lambda i: (i, 0)),
                  pl.BlockSpec((S, 512), lambda i: (0, 1)),
                  pl.BlockSpec((S, 512), lambda i: (0, 2)),
                  pl.BlockSpec((T, 512), lambda i: (i, 0)),
                  pl.BlockSpec((8, T, 128), lambda i: (0, i, 0))] + [ANY] * len(deps),
        out_specs=[pl.BlockSpec((T, 512), lambda i: (i, 0)),
                   pl.BlockSpec((S, 512), lambda i: (0, 0)),
                   pl.BlockSpec((S, 512), lambda i: (0, 0))],
        out_shape=[_sds((S, 512), BF16), _sds((S, 512), BF16), _sds((S, 512), BF16)],
        scratch_shapes=[pltpu.VMEM((S, 512), F32), pltpu.VMEM((S, 512), F32), pltpu.VMEM((T, 512), F32),
                        pltpu.VMEM((4, 2 * T, T), F32), pltpu.VMEM((4, 2 * T, T), F32), pltpu.VMEM((4, 2 * T, T), F32),
                        pltpu.VMEM((4, 2 * T, 128), BF16), pltpu.VMEM((4, 2 * T, 128), BF16)],
        compiler_params=_cp(("arbitrary",)),
    )(qkv, qkv, qkv, dl, tb, *deps)


def pool_bwd(dl, pooled_b, pool_w, pool_scale, CH=256):
    S = dl.shape[0]
    CH = min(CH, S)

    def body(db_ref, pooled_ref, w_ref, sc_ref, du_ref, dw_ref, dsc_ref, pad_ref, dp_ref):
        pad_ref[S:S + 16, :] = jnp.zeros((16, 128), F32)
        for g, win in enumerate(POOL_WINDOWS):
            cs = slice(g * 128, (g + 1) * 128)
            wq = w_ref[g].astype(BF16)
            dwg = jnp.zeros((128, 128), F32)
            dsc = jnp.zeros((1, 128), F32)
            for ch in range(S // CH):
                rs_ = slice(ch * CH, (ch + 1) * CH)
                db = db_ref[rs_, cs]
                pb = pooled_ref[rs_, cs]
                dsc = dsc + _colsum(db * _dot(pb, wq))
                dmsb = (db * sc_ref[:, cs]).astype(BF16)
                dwg = dwg + _dot_tn(pb, dmsb)
                dpool = _dot_nt(dmsb, wq)
                t = ch * CH + lax.broadcasted_iota(jnp.int32, (CH, 1), 0)
                cnt = jnp.minimum(t + 1, win).astype(F32)
                dp_ref[rs_, :] = dpool
                pad_ref[rs_, :] = dpool / cnt
            dw_ref[g] = dwg
            dsc_ref[:, cs] = dsc
            for ch in range(S // CH):
                base = ch * CH
                acc = pad_ref[base:base + CH, :]
                for sft in range(1, win):
                    acc = acc + pad_ref[base + sft:base + sft + CH, :]
                du_ref[base:base + CH, cs] = (acc - dp_ref[base:base + CH, :]).astype(BF16)

    return pl.pallas_call(
        body, name="pool_bwd", grid=(1,),
        in_specs=[pl.BlockSpec((S, 512), lambda i: (0, 1)),
                  pl.BlockSpec((S, 512), lambda i: (0, 0)),
                  pl.BlockSpec((4, 128, 128), lambda i: (0, 0, 0)),
                  pl.BlockSpec((1, 512), lambda i: (0, 0))],
        out_specs=[pl.BlockSpec((S, 512), lambda i: (0, 0)),
                   pl.BlockSpec((4, 128, 128), lambda i: (0, 0, 0)),
                   pl.BlockSpec((1, 512), lambda i: (0, 0))],
        out_shape=[_sds((S, 512), BF16), _sds((4, 128, 128)), _sds((1, 512))],
        scratch_shapes=[pltpu.VMEM((S + 16, 128), F32), pltpu.VMEM((S, 128), F32)],
        compiler_params=_cp(("arbitrary",)),
    )(dl, pooled_b, pool_w, pool_scale)


def tn_into(a, b, out, out_b, r0, c0, tk=1024, tn=512):
    S, K = a.shape
    N = b.shape[1]
    tk, tn = min(tk, K), min(tn, N)
    assert K % tk == 0 and N % tn == 0 and r0 % tk == 0 and c0 % tn == 0
    rb, cb = r0 // tk, c0 // tn
    fresh = isinstance(out, jax.ShapeDtypeStruct)

    def body(*refs):
        a_ref, b_ref, o_ref, ob_ref = refs[0], refs[1], refs[-2], refs[-1]
        r = _dot_tn(a_ref[...], b_ref[...])
        o_ref[...] = r
        ob_ref[...] = r.astype(BF16)

    ospec = pl.BlockSpec((tk, tn), lambda i, j: (rb + i, cb + j))
    in_specs = [pl.BlockSpec((S, tk), lambda i, j: (0, i)), pl.BlockSpec((S, tn), lambda i, j: (0, j))]
    args = [a, b]
    aliases = {}
    if not fresh:
        in_specs += [pl.BlockSpec(memory_space=pl.ANY), pl.BlockSpec(memory_space=pl.ANY)]
        args += [out, out_b]
        aliases = {2: 0, 3: 1}
    shp = out.shape
    return pl.pallas_call(
        body, name="tn_grad", grid=(K // tk, N // tn),
        in_specs=in_specs, out_specs=[ospec, ospec],
        out_shape=[_sds(shp, F32), _sds(shp, BF16)],
        input_output_aliases=aliases,
        compiler_params=_cp(("arbitrary", "arbitrary")),
    )(*args)


def _row(a, i):
    return a[i:i + 1]


MIXER_NAMES = (("even_w_in", "even_w_out"), ("odd_w_in", "odd_w_out"))


def _tn_group(items):
    out = {}
    for name, (shape, parts) in items.items():
        g, gb = _sds(shape, F32), _sds(shape, BF16)
        for a, b, r0, c0 in parts:
            g, gb = tn_into(a, b, g, gb, r0, c0)
        out[name] = (g, gb)
    return out


def fwd_layer(i, xin, p_i, target, comm):
    s = {}
    W = comm.weights(("mix", i), xin)
    w_in = W[MIXER_NAMES[i][0]]
    if i == 0:
        s["h"], s["xb"], s["qkv"] = mm_in(xin, w_in, nb16=1536)
        comm.poke(("in", i), s["h"])
        s["l1"], s["tb"] = attn_fwd(s["qkv"])
        s["l2"], s["pooled"] = pool_fwd(s["h"], W["pool_w"], W["pool_scale"])
    else:
        s["h"], s["xb"] = mm_in(xin, w_in)
        comm.poke(("in", i), s["h"])
        s["y"], s["hc"] = conv_fwd(s["h"], W["conv_dw"])
        sgb_bc = jnp.broadcast_to(W["sg_b"][:, :, None], (4, 128, 128))
        (s["l1"], s["l2"], s["xhc"], s["rsc"], s["xhv"], s["rsv"], s["sv"]) = odd_post(
            s["y"], s["h"], W["conv_ln_g"], W["conv_ln_b"], W["sg_ln_g"], W["sg_ln_b"], W["sg_w"], sgb_bc)
    tok = comm.poke(("mixed", i), s["l2"])
    W = comm.weights(("out", i), s["l2"])
    x1, s["xh1"], s["rs1"] = mm_out_ln(s["l1"], s["l2"], xin, W[MIXER_NAMES[i][1]], _row(W["ln_mix_g"], i),
                                       _row(W["ln_mix_b"], i), dep=tok)
    W = comm.weights(("ffn", i), x1)
    s["gate"], s["up"], s["hb"], s["x1b"] = ffn_up(x1, W["ffn_w_gate%d" % i], W["ffn_w_up%d" % i], None)
    x2, s["xh2"], s["rs2"] = ffn_down_ln(s["hb"], x1, W["ffn_w_down%d" % i], None,
                                         _row(W["ln_ffn_g"], i), _row(W["ln_ffn_b"], i))
    tok = comm.poke(("ffn", i), x2)
    outs = ple_fwd(x2, p_i, W["ple_w_gate%d" % i], W["ple_w_proj%d" % i], None, _row(W["ple_b_gate"], i), target,
                   dep=tok)
    s["sg"], s["pp"], s["x2b"], s["pb"] = outs[1:5]
    return outs[0], s, outs[5:]


def bwd_layer(i, dx, s, W, comm, tok=None):
    small = {}
    D = dx.shape[1]
    FP = W["ffn_w_gate%d" % i].shape[1]
    dx2, dgp_b, dpp_b, small["ple_b_gate"] = ple_bwd(dx, s["sg"], s["pp"], W["ple_w_gate%d" % i], None, dep=tok)
    dr2, dr2_b, dgate_b, dup_b, small["ln_ffn_g"], small["ln_ffn_b"] = ffn_bwd_a(
        dx2, s["xh2"], s["rs2"], _row(W["ln_ffn_g"], i), s["gate"], s["up"], W["ffn_w_down%d" % i], None)
    tok = comm.grads(_tn_group({
        "ple_w_gate%d" % i: ((D, D), [(s["x2b"], dgp_b, 0, 0)]),
        "ple_w_proj%d" % i: ((s["pb"].shape[1], D), [(s["pb"], dpp_b, 0, 0)]),
        "ffn_w_down%d" % i: ((FP, D), [(s["hb"], dr2_b, 0, 0)]),
        "ffn_w_gate%d" % i: ((D, FP), [(s["x1b"], dgate_b, 0, 0)]),
        "ffn_w_up%d" % i: ((D, FP), [(s["x1b"], dup_b, 0, 0)])}))
    dx1 = ffn_bwd_b(dr2, dgate_b, dup_b, W["ffn_w_gate%d" % i], W["ffn_w_up%d" % i], None, dep=tok)
    iname, oname = MIXER_NAMES[i]
    dr1, dmix_b, dl, small["ln_mix_g"], small["ln_mix_b"] = mix_bwd(
        dx1, s["xh1"], s["rs1"], _row(W["ln_mix_g"], i), W[oname])
    tok = comm.poke(("bwd", i), dl)
    if i == 1:
        (dy, dzc_b, small["conv_ln_g"], small["conv_ln_b"], small["sg_ln_g"], small["sg_ln_b"],
         small["sg_w"], dsb) = odd_post_bwd(dl, s["h"], s["xhc"], s["rsc"], s["xhv"], s["rsv"], s["sv"],
                                            W["conv_ln_g"], W["conv_ln_b"], W["sg_ln_g"], W["sg_ln_b"], W["sg_w"],
                                            dep=tok)
        small["sg_b"] = dsb[:, :, 0]
        da_b, dg_b, small["conv_dw"] = conv_bwd(dy, s["hc"], s["h"], W["conv_dw"])
        pieces = [(da_b, 0), (dg_b, 512), (dzc_b, 1024)]
    else:
        dq_b, dk_b, dv_b = attn_bwd(s["qkv"], dl, s["tb"], dep=tok)
        du_b, small["pool_w"], small["pool_scale"] = pool_bwd(dl, s["pooled"], W["pool_w"], W["pool_scale"])
        pieces = [(dq_b, 0), (dk_b, 512), (dv_b, 1024), (du_b, 1536)]
    dxin = dx_in(dr1, pieces, W[iname])
    tok = comm.grads(_tn_group({
        oname: ((1024, D), [(s["l1"], dmix_b, 0, 0), (s["l2"], dmix_b, 512, 0)]),
        iname: ((D, 2048), [(s["xb"], a, 0, off) for a, off in pieces])}))
    return dxin, small, tok


def run_layers(x, p, target, comm):
    saved, xin = [], x
    for i in range(2):
        xin, s, extra = fwd_layer(i, xin, p[i], target if i == 1 else None, comm)
        saved.append(s)
    dx, sq = extra
    W = comm.all_weights()
    per_layer = [None, None]
    tok = None
    for i in (1, 0):
        dx, per_layer[i], tok = bwd_layer(i, dx, saved[i], W, comm, tok)
    small = {}
    for k in ("ln_mix_g", "ln_mix_b", "ln_ffn_g", "ln_ffn_b", "ple_b_gate"):
        small[k] = jnp.concatenate([per_layer[0][k], per_layer[1][k]], axis=0)
    for i in range(2):
        small.update({k: v for k, v in per_layer[i].items() if k not in small})
    return sq, dx, small


def _big_table():
    t = {}
    for nm in ("even", "odd"):
        t[nm + "_w_in"] = ((1024, 2048), 1, 256, 256, nm + "_w_in", 0)
        t[nm + "_w_out"] = ((1024, 1024), 0, 128, 128, nm + "_w_out", 0)
    for l in range(2):
        t["ffn_w_gate%d" % l] = ((1024, 8 * FF_PAD), 1, FF_PAD, FF_SHARD, "ffn_w_gate", l)
        t["ffn_w_up%d" % l] = ((1024, 8 * FF_PAD), 1, FF_PAD, FF_SHARD, "ffn_w_up", l)
        t["ffn_w_down%d" % l] = ((8 * FF_PAD, 1024), 0, FF_PAD, FF_SHARD, "ffn_w_down", l)
        t["ple_w_gate%d" % l] = ((1024, 1024), 0, 128, 128, "ple_w_gate", l)
        t["ple_w_proj%d" % l] = ((256, 1024), 1, 128, 128, "ple_w_proj", l)
    return t


BIG = _big_table()
SMALL_SPEC = ((N_DEV, 40, 64), 0, 1, 1)
_LAYER_GROUP = lambda l: ["ffn_w_gate%d" % l, "ffn_w_up%d" % l, "ffn_w_down%d" % l, "ple_w_gate%d" % l, "ple_w_proj%d" % l]
AG_GROUPS = (["even_w_in"], ["even_w_out"], _LAYER_GROUP(0), ["odd_w_in", "odd_w_out", "small"], _LAYER_GROUP(1))
AG_NEED = {("mix", 0): 0, ("out", 0): 1, ("ffn", 0): 2, ("mix", 1): 3, ("ffn", 1): 4}
AG_PASS = {("in", 0): 1, ("mixed", 0): 2, ("ffn", 0): 3, ("mixed", 1): 4}
ANY = pl.BlockSpec(memory_space=pl.ANY)
SEM = pl.BlockSpec(memory_space=pltpu.SEMAPHORE)


def _spec(name):
    return SMALL_SPEC if name == "small" else BIG[name]


def _win_shape(spec):
    full, axis, w = spec[:3]
    return tuple(w if d == axis else n for d, n in enumerate(full))


def _window(ref, axis, w, j):
    idx = [slice(None)] * len(ref.shape)
    idx[axis] = pl.ds(j, 1) if w == 1 else pl.ds(pl.multiple_of(j * w, w), w)
    return ref.at[tuple(idx)]


def _mesh_pos():
    return lax.axis_index("x"), lax.axis_index("y"), lax.axis_index("c")


def split_call(name, arrays, starts=(), waits=(), sems_in=(), new=(), after=None):
    n, nn, ns = len(arrays), len(new), len(starts)
    flat_sems = [s for pair in sems_in for s in pair]

    def body(*refs):
        arr = list(refs[:n])
        sin = refs[n:n + len(flat_sems)]
        outs = refs[n + len(flat_sems) + (after is not None):]
        data = arr + list(outs[n:n + nn])
        for p, k, kind, mk in waits:
            d = mk(data, sin[2 * p].at[k], sin[2 * p + 1].at[k])
            d.wait_send() if kind == "send" else d.wait_recv()
        if ns:
            send, recv = outs[n + nn], outs[n + nn + 1]
            for k, mk in enumerate(starts):
                mk(data, send.at[k], recv.at[k]).start()
        outs[-1][...] = jnp.zeros((8, 128), F32)

    sem_out = [pltpu.SemaphoreType.DMA((ns,)), pltpu.SemaphoreType.DMA((ns,))] if ns else []
    res = pl.pallas_call(
        body, name=name,
        in_specs=[ANY] * n + [SEM] * len(flat_sems) + ([ANY] if after is not None else []),
        out_specs=[ANY] * (n + nn) + [SEM] * len(sem_out) + [pl.BlockSpec(memory_space=pltpu.VMEM)],
        out_shape=[_sds(a.shape, a.dtype) for a in arrays] + list(new) + sem_out + [_sds((8, 128), F32)],
        input_output_aliases={a: a for a in range(n)},
        compiler_params=pltpu.CompilerParams(has_side_effects=pltpu.SideEffectType.DATAFLOW_SIDE_EFFECTING),
    )(*arrays, *flat_sems, *([after] if after is not None else []))
    return list(res[:n + nn]), (tuple(res[n + nn:n + nn + 2]) if ns else None), res[-1]


def _remote(src, dst, send_sem, recv_sem, dev):
    return pltpu.make_async_remote_copy(src_ref=src, dst_ref=dst, send_sem=send_sem, recv_sem=recv_sem,
                                        device_id=dev, device_id_type=MESH_T)


class Gatherer:
    def __init__(self, groups, arrays, specs, prefix):
        self.groups, self.specs, self.prefix = groups, specs, prefix
        self.names = [nm for g in groups for nm in g]
        self.arr = dict(zip(self.names, arrays))
        self.fwd_sems = {}
        self.forwarded = set()

    @staticmethod
    def _mk_first(ai, spec, k):
        def mk(refs, ss, rs):
            x, y, c = _mesh_pos()
            dev = [(x, y, 1 - c), (1 - x, y, c), (x, 1 - y, c), (1 - x, 1 - y, c)][k]
            win = _window(refs[ai], spec[1], spec[2], 4 * x + 2 * y + c)
            return _remote(win, win, ss, rs, dev)
        return mk

    @staticmethod
    def _mk_fwd(ai, spec, j):
        def mk(refs, ss, rs):
            x, y, c = _mesh_pos()
            px, py = [(1 - x, y), (x, 1 - y), (1 - x, 1 - y)][j]
            win = _window(refs[ai], spec[1], spec[2], 4 * px + 2 * py + c)
            return _remote(win, win, ss, rs, (x, y, 1 - c))
        return mk

    def start(self, after=None):
        starts = [self._mk_first(ai, self.specs[nm], k) for ai, nm in enumerate(self.names) for k in range(4)]
        arrs, self.first_sems, tok = split_call(self.prefix + "_start", [self.arr[nm] for nm in self.names],
                                                starts=starts, after=after)
        self.arr = dict(zip(self.names, arrs))
        return tok

    def forward(self, g, after=None):
        if g in self.forwarded:
            return None
        self.forwarded.add(g)
        names = self.groups[g]
        waits = [(0, 4 * self.names.index(nm) + 1 + j, "recv", self._mk_fwd(ai, self.specs[nm], j))
                 for ai, nm in enumerate(names) for j in range(3)]
        starts = [self._mk_fwd(ai, self.specs[nm], j) for ai, nm in enumerate(names) for j in range(3)]
        arrs, self.fwd_sems[g], tok = split_call(
            "%s_forward%d" % (self.prefix, g), [self.arr[nm] for nm in names], starts=starts, waits=waits,
            sems_in=[self.first_sems], after=after)
        self.arr.update(zip(names, arrs))
        return tok

    def finish(self, g, after=None):
        self.forward(g, after)
        names = self.groups[g]
        waits = []
        for ai, nm in enumerate(names):
            base = 4 * self.names.index(nm)
            waits.append((0, base, "recv", self._mk_first(ai, self.specs[nm], 0)))
            waits += [(1, 3 * ai + j, "recv", self._mk_fwd(ai, self.specs[nm], j)) for j in range(3)]
            waits += [(0, base + k, "send", self._mk_first(ai, self.specs[nm], k)) for k in range(4)]
            waits += [(1, 3 * ai + j, "send", self._mk_fwd(ai, self.specs[nm], j)) for j in range(3)]
        arrs, _, _ = split_call(
            "%s_finish%d" % (self.prefix, g), [self.arr[nm] for nm in names], waits=waits,
            sems_in=[self.first_sems, self.fwd_sems[g]], after=after)
        self.arr.update(zip(names, arrs))
        return {nm: self.arr[nm] for nm in names}


class Reducer:
    def __init__(self, cq_arr, adam):
        self.cq_arr, self.adam = cq_arr, adam
        self.groups = []
        self.n = 0
        self.last = None

    @staticmethod
    def _mk1(gi, li, spec, q):
        def mk(refs, ss, rs):
            x, y, c = _mesh_pos()
            return _remote(_window(refs[gi], spec[1], spec[2], 2 * q + (1 - c)), refs[li].at[q], ss, rs, (x, y, 1 - c))
        return mk

    @staticmethod
    def _mk2(si, li, d):
        def mk(refs, ss, rs):
            x, y, c = _mesh_pos()
            qd = lax.rem(2 * x + y + d, 4)
            return _remote(refs[si].at[d - 1], refs[li].at[3 - d], ss, rs, (lax.div(qd, 2), lax.rem(qd, 2), c))
        return mk

    def add(self, grads, after=None):
        names = list(grads)
        m = len(names)
        starts = [self._mk1(ai, m + ai, BIG[nm], q) for ai, nm in enumerate(names) for q in range(4)]
        new = [_sds((4,) + _win_shape(BIG[nm]), BF16) for nm in names]
        res, sems, tok = split_call("rs1_start%d" % self.n, [grads[nm][1] for nm in names], starts=starts, new=new,
                                    after=after)
        self.groups.append(dict(names=names, g=[grads[nm][0] for nm in names], starts=starts, buf=res, sems=sems,
                                stage=1, age=0, idx=self.n))
        self.n += 1
        return tok

    def step(self, after):
        tok = None
        for grp in self.groups:
            names, m = grp["names"], len(grp["names"])
            if grp["stage"] == 1:
                waits = [(0, k, kind, mk) for k, mk in enumerate(grp["starts"]) for kind in ("send", "recv")]
                res, _, _ = split_call("rs1_wait%d" % grp["idx"], grp["buf"], waits=waits, sems_in=[grp["sems"]], after=after)
                land1 = res[m:]
                s1b = [add_pairs(g, l, BIG[nm], self.cq_arr) for nm, g, l in zip(names, grp["g"], land1)]
                starts = [self._mk2(ai, m + ai, d) for ai in range(m) for d in (1, 2, 3)]
                new = [_sds(a.shape, BF16) for a in s1b]
                res, sems, tok = split_call("rs2_start%d" % grp["idx"], s1b, starts=starts, new=new, after=tok)
                grp.update(stage=2, land1=land1, starts=starts, buf=res, sems=sems)
        return tok

    def finish_oldest(self):
        for grp in self.groups:
            if grp["stage"] == 2:
                names, m = grp["names"], len(grp["names"])
                waits = [(0, k, kind, mk) for k, mk in enumerate(grp["starts"]) for kind in ("send", "recv")]
                res, _, _ = split_call("rs2_wait%d" % grp["idx"], grp["buf"], waits=waits, sems_in=[grp["sems"]],
                                       after=self.last)
                for nm, g, l1, l2 in zip(names, grp["g"], grp["land1"], res[m:]):
                    self.last = self.adam(nm, g, l1, l2, self.last)
                grp["stage"] = 3
                return True
        return False


def pack_weights(args, arg_names, small_blk, names, j_arr):
    n_in = len(args)

    def body(j_ref, *refs):
        for o, nm in enumerate(names):
            dst = refs[n_in + 1 + o]
            if nm == "small":
                dst[...] = refs[n_in][...]
                continue
            _, axis, w, valid, arg, layer = BIG[nm]
            src = refs[arg_names.index(arg)][layer].astype(BF16)
            if valid == w:
                dst[...] = src
            else:
                dst[...] = jnp.zeros(dst.shape, BF16)
                if axis == 1:
                    dst[:, 0:valid] = src
                else:
                    dst[0:valid, :] = src

    def ispec(a):
        return pl.BlockSpec(a.shape, lambda i, j_ref: (0, 0, 0))

    def ospec(spec):
        axis, nd = spec[1], len(spec[0])
        return pl.BlockSpec(_win_shape(spec),
                            lambda i, j_ref, axis=axis, nd=nd: tuple(j_ref[0] if d == axis else 0 for d in range(nd)))

    specs = [_spec(nm) for nm in names]
    return pl.pallas_call(
        body, name="pack_weights",
        grid_spec=pltpu.PrefetchScalarGridSpec(
            num_scalar_prefetch=1, grid=(1,),
            in_specs=[ispec(a) for a in list(args) + [small_blk]], out_specs=[ospec(s) for s in specs]),
        out_shape=[_sds(s[0], F32 if nm == "small" else BF16) for nm, s in zip(names, specs)],
        compiler_params=_cp(("arbitrary",)),
    )(j_arr, *args, small_blk)


def add_pairs(full, land, spec, cq_arr):
    axis, w = spec[1], spec[2]
    R, C = full.shape

    def chip(d, cq):
        return lax.rem(cq[1] + d + 1, 4)

    if axis == 1:
        tr = min(512, R)
        grid = (3, R // tr)
        fspec = pl.BlockSpec((tr, w), lambda d, i, cq: (i, 2 * chip(d, cq) + cq[0]))
        lspec = pl.BlockSpec((None, tr, w), lambda d, i, cq: (chip(d, cq), i, 0))
        ospec = pl.BlockSpec((None, tr, w), lambda d, i, cq: (d, i, 0))
    else:
        grid = (3, 1)
        fspec = pl.BlockSpec((w, C), lambda d, i, cq: (2 * chip(d, cq) + cq[0], 0))
        lspec = pl.BlockSpec((None, w, C), lambda d, i, cq: (chip(d, cq), 0, 0))
        ospec = pl.BlockSpec((None, w, C), lambda d, i, cq: (d, 0, 0))

    def body(cq_ref, a_ref, b_ref, ob_ref):
        ob_ref[...] = (a_ref[...] + b_ref[...].astype(F32)).astype(BF16)

    return pl.pallas_call(
        body, name="add_pairs",
        grid_spec=pltpu.PrefetchScalarGridSpec(
            num_scalar_prefetch=1, grid=grid, in_specs=[fspec, lspec], out_specs=[ospec]),
        out_shape=[_sds((3,) + land.shape[1:], BF16)],
        compiler_params=_cp(("arbitrary",) * 2),
    )(cq_arr, full, land)[0]


def _adamw(w, g, m, v):
    m = ADAM_B1 * m + (1.0 - ADAM_B1) * g
    v = ADAM_B2 * v + (1.0 - ADAM_B2) * (g * g)
    m_hat = m / (1.0 - ADAM_B1 ** ADAM_STEP)
    v_hat = v / (1.0 - ADAM_B2 ** ADAM_STEP)
    delta = -ADAM_LR * (m_hat / (jnp.sqrt(v_hat) + ADAM_EPS) + ADAM_WD * w)
    return delta, m, v


def reduce_adamw(full, land1, land, w, m, v, spec, cq_arr, prev=None, dep=None):
    axis, win, valid, layer = spec[1], spec[2], spec[3], spec[5]
    L, R, C = w.shape
    if axis == 1:
        tr = min(256, R)
        grid = (R // tr,)
        fspec = pl.BlockSpec((tr, win), lambda i, cq: (i, 2 * cq[1] + cq[0]))
        wspec = pl.BlockSpec((None, tr, win), lambda i, cq: (cq[1], i, 0))
        lspec = pl.BlockSpec((3, tr, win), lambda i, cq: (0, i, 0))
        sspec = pl.BlockSpec((None, tr, C), lambda i, cq: (layer, i, 0))
    else:
        grid = (1,)
        fspec = pl.BlockSpec((win, full.shape[1]), lambda i, cq: (2 * cq[1] + cq[0], 0))
        wspec = pl.BlockSpec((None, win, C), lambda i, cq: (cq[1], 0, 0))
        lspec = pl.BlockSpec((3, win, C), lambda i, cq: (0, 0, 0))
        sspec = pl.BlockSpec((None, R, C), lambda i, cq: (layer, 0, 0))

    def body(cq_ref, full_ref, own_ref, land_ref, w_ref, m_ref, v_ref, *rest):
        g_ref, d_ref, nm_ref, nv_ref = rest[-4:]
        if axis == 1:
            rd = lambda r, *lead: r[(*lead, slice(None), slice(0, valid))]
        else:
            rd = lambda r, *lead: r[(*lead, slice(0, valid), slice(None))]
        g = rd(full_ref) + rd(own_ref).astype(F32)
        for k in range(3):
            g = g + rd(land_ref, k).astype(F32)
        g_ref[...] = g
        d, nm, nv = _adamw(w_ref[...], g, m_ref[...], v_ref[...])
        d_ref[...] = d
        nm_ref[...] = nm
        nv_ref[...] = nv

    extra = (list(prev) if prev is not None else []) + ([dep] if dep is not None else [])
    return pl.pallas_call(
        body, name="reduce_adamw",
        grid_spec=pltpu.PrefetchScalarGridSpec(
            num_scalar_prefetch=1, grid=grid,
            in_specs=[fspec, wspec, lspec, sspec, sspec, sspec] + [ANY] * len(extra), out_specs=[sspec] * 4),
        out_shape=[_sds(w.shape)] * 4,
        input_output_aliases={7 + k: k for k in range(4 if prev is not None else 0)},
        compiler_params=_cp(("arbitrary",)),
    )(cq_arr, full, land1, land, w, m, v, *extra)


def place_slot(packed, j_arr):
    R = packed.shape[0]

    def body(j_ref, src, dst):
        dst[...] = src[...]

    return pl.pallas_call(
        body, name="place_slot",
        grid_spec=pltpu.PrefetchScalarGridSpec(
            num_scalar_prefetch=1, grid=(1,),
            in_specs=[pl.BlockSpec((R, 128), lambda i, j: (0, 0))],
            out_specs=[pl.BlockSpec((None, R, 128), lambda i, j: (j[0], 0, 0))]),
        out_shape=[_sds((N_DEV, R, 128))], compiler_params=_cp(("arbitrary",)),
    )(j_arr, packed)[0]


def sum_slots(gathered):
    def body(g_ref, o_ref):
        g = g_ref[0]
        for dev in range(1, N_DEV):
            g = g + g_ref[dev]
        o_ref[...] = g

    return pl.pallas_call(body, name="sum_slots", out_shape=_sds(gathered.shape[1:]), compiler_params=_cp())(gathered)


def small_adamw(gs, wmv):
    k = len(gs)

    def body(*refs):
        for a in range(k):
            g, w, m, v = refs[4 * a:4 * a + 4]
            d, nm, nv = _adamw(w[...], g[...], m[...], v[...])
            refs[4 * k + 3 * a][...] = d
            refs[4 * k + 3 * a + 1][...] = nm
            refs[4 * k + 3 * a + 2][...] = nv

    args = [t for g, tup in zip(gs, wmv) for t in (g,) + tuple(tup)]
    out_shape = [_sds(g.shape) for g in gs for _ in range(3)]
    return pl.pallas_call(body, name="small_adamw", out_shape=out_shape, compiler_params=_cp())(*args)


WEIGHT_NAMES = ("even_w_in", "even_w_out", "pool_w", "pool_scale", "odd_w_in", "odd_w_out", "conv_dw", "conv_ln_g",
                "conv_ln_b", "sg_ln_g", "sg_ln_b", "sg_w", "sg_b", "ln_mix_g", "ln_mix_b", "ffn_w_gate", "ffn_w_up",
                "ffn_w_down", "ln_ffn_g", "ln_ffn_b", "ple_w_proj", "ple_w_gate", "ple_b_gate")
PACK_ARGS = ("even_w_in", "even_w_out", "odd_w_in", "odd_w_out", "ffn_w_gate", "ffn_w_up", "ffn_w_down",
             "ple_w_gate", "ple_w_proj")
REPLICATED = ("pool_w", "pool_scale", "sg_w", "sg_b", "ln_mix_g", "ln_mix_b", "ln_ffn_g", "ln_ffn_b", "ple_b_gate")
SHARDED_SMALL = ("conv_dw", "conv_ln_g", "conv_ln_b", "sg_ln_g", "sg_ln_b")
NATURAL = {"pool_w": (4, 128, 128), "pool_scale": (1, 512), "sg_w": (4, 128, 128), "sg_b": (4, 128),
           "ln_mix_g": (2, 1024), "ln_mix_b": (2, 1024), "ln_ffn_g": (2, 1024), "ln_ffn_b": (2, 1024),
           "ple_b_gate": (2, 1024)}


def kernel(x, p, even_w_in, even_w_out, pool_w, pool_scale, odd_w_in, odd_w_out, conv_dw, conv_ln_g, conv_ln_b, sg_ln_g, sg_ln_b, sg_w, sg_b, ln_mix_g, ln_mix_b, ffn_w_gate, ffn_w_up, ffn_w_down, ln_ffn_g, ln_ffn_b, ple_w_proj, ple_w_gate, ple_b_gate, loss_target, m_even_w_in, m_even_w_out, m_pool_w, m_pool_scale, m_odd_w_in, m_odd_w_out, m_conv_dw, m_conv_ln_g, m_conv_ln_b, m_sg_ln_g, m_sg_ln_b, m_sg_w, m_sg_b, m_ln_mix_g, m_ln_mix_b, m_ffn_w_gate, m_ffn_w_up, m_ffn_w_down, m_ln_ffn_g, m_ln_ffn_b, m_ple_w_proj, m_ple_w_gate, m_ple_b_gate, v_even_w_in, v_even_w_out, v_pool_w, v_pool_scale, v_odd_w_in, v_odd_w_out, v_conv_dw, v_conv_ln_g, v_conv_ln_b, v_sg_ln_g, v_sg_ln_b, v_sg_w, v_sg_b, v_ln_mix_g, v_ln_mix_b, v_ffn_w_gate, v_ffn_w_up, v_ffn_w_down, v_ln_ffn_g, v_ln_ffn_b, v_ple_w_proj, v_ple_w_gate, v_ple_b_gate):
    A = dict(locals())
    mx, my, mc = _mesh_pos()
    j = 4 * mx + 2 * my + mc
    j_arr = j.astype(jnp.int32).reshape(1)
    cq_arr = jnp.stack([mc, 2 * mx + my]).astype(jnp.int32)
    res = {}

    def adam(nm, full, land1, land2, dep):
        arg = BIG[nm][4]
        res[arg] = reduce_adamw(full, land1, land2, A[arg], A["m_" + arg], A["v_" + arg], BIG[nm], cq_arr,
                                res.get(arg), dep)
        return res[arg][0]

    class Comm:
        def __init__(self):
            names = [nm for g in AG_GROUPS for nm in g]
            small_blk = jnp.concatenate([conv_dw[0], conv_ln_g, conv_ln_b, sg_ln_g, sg_ln_b, jnp.zeros((5, 64), F32)], axis=0)
            mine = pack_weights([A[k] for k in PACK_ARGS], PACK_ARGS, small_blk[None], names, j_arr)
            self.gat = Gatherer(AG_GROUPS, mine, {nm: _spec(nm) for nm in names}, "ag")
            self.gat.start()
            self.red = Reducer(cq_arr, adam)
            self.W = {k: A[k].reshape(NATURAL[k]) for k in REPLICATED}

        def weights(self, stage, after):
            if stage in AG_NEED:
                got = self.gat.finish(AG_NEED[stage], after)
                if "small" in got:
                    sm = got.pop("small").transpose(1, 0, 2).reshape(40, 512)
                    got.update(conv_dw=sm[0:31], conv_ln_g=sm[31:32], conv_ln_b=sm[32:33], sg_ln_g=sm[33:34],
                               sg_ln_b=sm[34:35])
                self.W.update(got)
            return self.W

        def all_weights(self):
            return self.W

        def poke(self, tag, after):
            if tag in AG_PASS:
                return self.gat.forward(AG_PASS[tag], after)
            if tag[0] == "bwd":
                return self.red.step(after)
            return None

        def grads(self, grads):
            tok = self.red.step(next(iter(grads.values()))[0])
            return self.red.add(grads, after=tok)

    comm = Comm()
    sq, dx, small = run_layers(x[0], p[:, 0], loss_target[0], comm)
    loss = lax.psum(0.5 * jnp.sum(sq) / x.shape[-1], ("x", "y", "c"))
    red = comm.red
    tok = red.step(dx)

    names = REPLICATED + SHARDED_SMALL
    flat = jnp.concatenate([small[k].reshape(-1) for k in names])
    rows = -(-flat.shape[0] // 1024) * 8
    packed = jnp.pad(flat, (0, rows * 128 - flat.shape[0])).reshape(rows, 128)
    sg = Gatherer((["g"],), [place_slot(packed, j_arr)], {"g": ((N_DEV, rows, 128), 0, 1, 1)}, "sg")
    sg.start(after=tok)
    red.finish_oldest()
    sg.forward(0, after=red.last)
    red.finish_oldest()
    red.finish_oldest()
    gsum_flat = sum_slots(sg.finish(0, after=red.last)["g"]).reshape(-1)
    gs, off = [], 0
    for k in names:
        n = math.prod(small[k].shape)
        g = gsum_flat[off:off + n].reshape(small[k].shape)
        off += n
        if k in SHARDED_SMALL:
            g = lax.dynamic_slice_in_dim(g, j * 64, 64, axis=1)
        gs.append(g.reshape(A[k].shape))
    outs = small_adamw(gs, [(A[k], A["m_" + k], A["v_" + k]) for k in names])
    for a, k in enumerate(names):
        res[k] = (gs[a],) + tuple(outs[3 * a:3 * a + 3])
    red.last = outs[0]
    while red.finish_oldest():
        pass

    out = [loss, dx[None]]
    for part in range(4):
        out += [res[k][part] for k in WEIGHT_NAMES]
    return tuple(out)
```

```python
import functools
import math

import jax
import jax.numpy as jnp
from jax import lax
from jax.experimental import pallas as pl
from jax.experimental.pallas import tpu as pltpu

F32, BF16 = jnp.float32, jnp.bfloat16
ALPHA = 4.0 ** 0.25
LN_EPS = 1e-5
QK_SCALE = 0.125
POOL_WINDOWS = (2, 4, 8, 16)
CONV_TAPS = 31
N_DEV = 8
FF_SHARD, FF_PAD = 352, 384
ADAM_LR, ADAM_B1, ADAM_B2, ADAM_EPS, ADAM_WD, ADAM_STEP = 0.001, 0.9, 0.999, 1e-08, 0.01, 10
VMEM_LIMIT = 56 * 1024 * 1024
MESH_T = pl.DeviceIdType.MESH


def _cp(sem=None):
    return pltpu.CompilerParams(dimension_semantics=sem, vmem_limit_bytes=VMEM_LIMIT)


def _dot(a, b):
    return jnp.dot(a, b, preferred_element_type=F32)


def _dot_nt(a, b):
    return lax.dot_general(a, b, (((1,), (1,)), ((), ())), preferred_element_type=F32)


def _dot_tn(a, b):
    return lax.dot_general(a, b, (((0,), (0,)), ((), ())), preferred_element_type=F32)


def _sigmoid(x):
    return 1.0 / (1.0 + jnp.exp(-x))


def _softplus(z):
    return jnp.maximum(z, 0.0) + jnp.log(1.0 + jnp.exp(-jnp.abs(z)))


_GELU_C = math.sqrt(2.0 / math.pi)


def _gelu(x):
    return 0.5 * x * (1.0 + jnp.tanh(_GELU_C * (x + 0.044715 * x * x * x)))


def _gelu_grad(x):
    t = jnp.tanh(_GELU_C * (x + 0.044715 * x * x * x))
    return 0.5 * (1.0 + t) + 0.5 * x * (1.0 - t * t) * _GELU_C * (1.0 + 3.0 * 0.044715 * x * x)


def _ln_fwd(r, g, b):
    mu = jnp.mean(r, axis=-1, keepdims=True)
    xc = r - mu
    var = jnp.mean(xc * xc, axis=-1, keepdims=True)
    rstd = lax.rsqrt(var + LN_EPS)
    xh = xc * rstd
    return xh * g + b, xh, rstd


def _ln_bwd(dy, xh, rstd, g):
    dxh = dy * g
    m1 = jnp.mean(dxh, axis=-1, keepdims=True)
    m2 = jnp.mean(dxh * xh, axis=-1, keepdims=True)
    return rstd * (dxh - m1 - xh * m2)


def _split2(x):
    hi = x.astype(BF16)
    lo = (x - hi.astype(F32)).astype(BF16)
    return hi, lo


def _colsum(x):
    return jnp.sum(x, axis=0, keepdims=True)


def _tok_call(name, body, tiled, full, out_tiled, out_acc=(), tm=256, scratch=(), dep=None):
    def arr(t):
        return t[0] if isinstance(t, tuple) else t
    full = [t[0] if isinstance(t, tuple) and t[1] is None else t for t in full]
    S = arr(tiled[0]).shape[0]
    tm = min(tm, S)
    n_in = len(tiled) + len(full)
    deps = [] if dep is None else [dep]
    if deps:
        inner = body
        body = lambda *refs: inner(*refs[:n_in], *refs[n_in + 1:])

    def tspec(t):
        if isinstance(t, tuple):
            _, w, cb = t
            return pl.BlockSpec((tm, w), lambda i, cb=cb: (i, cb))
        return pl.BlockSpec((tm, t.shape[1]), lambda i: (i, 0))

    def fspec(t):
        if isinstance(t, tuple):
            a, l = t
            nd = a.ndim - 1
            return pl.BlockSpec((None,) + a.shape[1:], lambda i, l=l, nd=nd: (l,) + (0,) * nd)
        nd = t.ndim
        return pl.BlockSpec(t.shape, lambda i, nd=nd: (0,) * nd)

    def ospec(o):
        return pl.BlockSpec((tm, o.shape[1]), lambda i: (i, 0))

    def aspec(o):
        nd = len(o.shape)
        return pl.BlockSpec(o.shape, lambda i, nd=nd: (0,) * nd)

    outs = pl.pallas_call(
        body, name=name, grid=(S // tm,),
        in_specs=[tspec(t) for t in tiled] + [fspec(t) for t in full] + [ANY] * len(deps),
        out_specs=[ospec(o) for o in out_tiled] + [aspec(o) for o in out_acc],
        out_shape=list(out_tiled) + list(out_acc),
        scratch_shapes=list(scratch),
        compiler_params=_cp(("arbitrary",)),
    )(*[arr(t) for t in tiled], *[arr(t) for t in full], *deps)
    return outs


def _sds(shape, dtype=F32):
    return jax.ShapeDtypeStruct(tuple(shape), dtype)


def _acc(ref, val):
    @pl.when(pl.program_id(0) == 0)
    def _():
        ref[...] = val

    @pl.when(pl.program_id(0) != 0)
    def _():
        ref[...] += val


def mm_in(x, w, nb16=0):
    S, N = x.shape[0], w.shape[1]

    def body(x_ref, w_ref, h_ref, xb_ref, *hb_ref):
        xb = x_ref[...].astype(BF16)
        xb_ref[...] = xb
        h = _dot(xb, w_ref[...])
        h_ref[...] = h
        if nb16:
            hb_ref[0][...] = h[:, 0:nb16].astype(BF16)

    outs = [_sds((S, N)), _sds((S, x.shape[1]), BF16)] + ([_sds((S, nb16), BF16)] if nb16 else [])
    return _tok_call("mm_in", body, [x], [w], outs, tm=512)


def _stack_heads(x, hm0, dtype=BF16):
    return jnp.concatenate([jnp.where(hm0, x, 0), jnp.where(hm0, 0, x)], axis=0).astype(dtype)


def _unstack_k(x, T):
    return jnp.concatenate([x[0:T], x[T:2 * T]], axis=1)


def _cumsum_mm(x, u):
    n = x.shape[0]
    hi, lo = _split2(x)
    r = _dot(jnp.concatenate([hi, lo], axis=0), u)
    return r[0:n] + r[n:2 * n]


def attn_fwd(qkv, T=256):
    S = qkv.shape[0]
    T = min(T, S)
    nq = S // T

    def body(q_ref, k_ref, v_ref, o_ref, t_ref, acc_ref, c_ref, qh_ref):
        i = pl.program_id(0)
        hm0 = lax.broadcasted_iota(jnp.int32, (1, 128), 1) < 64
        r2 = lax.broadcasted_iota(jnp.int32, (2 * T, T), 0)
        c2 = lax.broadcasted_iota(jnp.int32, (2 * T, T), 1)
        causal = c2 < jnp.where(r2 >= T, r2 - T, r2)
        ur = lax.broadcasted_iota(jnp.int32, (T, T), 0)
        uc = lax.broadcasted_iota(jnp.int32, (T, T), 1)
        u_incl = (ur >= uc).astype(BF16)
        acc_ref[...] = jnp.zeros_like(acc_ref)
        c_ref[...] = jnp.zeros_like(c_ref)
        for pp in range(4):
            qh_ref[pp] = _stack_heads(q_ref[:, pp * 128:(pp + 1) * 128] * QK_SCALE, hm0)

        def block(kb, diag):
            ks = pl.multiple_of(kb * T, T)
            for pp in range(4):
                cs = slice(pp * 128, (pp + 1) * 128)
                z = _dot_nt(qh_ref[pp], k_ref[pl.ds(ks, T), cs])
                sp = _softplus(z)
                if diag:
                    sp = jnp.where(causal, sp, 0.0)
                incl = _cumsum_mm(sp, u_incl)
                c = c_ref[pp]
                w = jnp.exp(z - incl - c)
                if diag:
                    w = jnp.where(causal, w, 0.0)
                acc_ref[:, cs] += _dot(_unstack_k(w.astype(BF16), T), _stack_heads(v_ref[pl.ds(ks, T), cs], hm0))
                c_ref[pp] = c + jnp.broadcast_to(incl[:, 0:1], (2 * T, T))

        block(i, True)

        def step(jj, carry):
            block(i - 1 - jj, False)
            return carry

        lax.fori_loop(0, i, step, 0)
        o_ref[...] = acc_ref[...].astype(BF16)
        for pp in range(4):
            for hd in range(2):
                t_ref[2 * pp + hd] = c_ref[pp, hd * T:(hd + 1) * T, 0:128]

    return pl.pallas_call(
        body, name="attn_fwd", grid=(nq,),
        in_specs=[pl.BlockSpec((T, 512), lambda i: (i, 0)),
                  pl.BlockSpec((S, 512), lambda i: (0, 1)),
                  pl.BlockSpec((S, 512), lambda i: (0, 2))],
        out_specs=[pl.BlockSpec((T, 512), lambda i: (i, 0)),
                   pl.BlockSpec((8, T, 128), lambda i: (0, i, 0))],
        out_shape=[_sds((S, 512), BF16), _sds((8, S, 128))],
        scratch_shapes=[pltpu.VMEM((T, 512), F32), pltpu.VMEM((4, 2 * T, T), F32), pltpu.VMEM((4, 2 * T, 128), BF16)],
        compiler_params=_cp(("arbitrary",)),
    )(qkv, qkv, qkv)


def pool_fwd(h, pool_w, pool_scale, CH=256):
    S = h.shape[0]
    CH = min(CH, S)

    def body(u_ref, w_ref, sc_ref, b_ref, pooled_ref, pad_ref):
        pad_ref[0:16, :] = jnp.zeros((16, 512), F32)
        pad_ref[16:16 + S, :] = u_ref[...]
        for g, win in enumerate(POOL_WINDOWS):
            cs = slice(g * 128, (g + 1) * 128)
            wq = w_ref[g].astype(BF16)
            for ch in range(S // CH):
                base = ch * CH
                acc = pad_ref[16 + base:16 + base + CH, cs]
                for sft in range(1, win):
                    acc = acc + pad_ref[16 + base - sft:16 + base - sft + CH, cs]
                t = base + lax.broadcasted_iota(jnp.int32, (CH, 1), 0)
                cnt = jnp.minimum(t + 1, win).astype(F32)
                pooled = (acc / cnt - pad_ref[16 + base:16 + base + CH, cs]).astype(BF16)
                pooled_ref[base:base + CH, cs] = pooled
                b_ref[base:base + CH, cs] = (_dot(pooled, wq) * sc_ref[:, cs]).astype(BF16)

    return pl.pallas_call(
        body, name="pool_fwd", grid=(1,),
        in_specs=[pl.BlockSpec((S, 512), lambda i: (0, 3)),
                  pl.BlockSpec((4, 128, 128), lambda i: (0, 0, 0)),
                  pl.BlockSpec((1, 512), lambda i: (0, 0))],
        out_specs=[pl.BlockSpec((S, 512), lambda i: (0, 0)), pl.BlockSpec((S, 512), lambda i: (0, 0))],
        out_shape=[_sds((S, 512), BF16), _sds((S, 512), BF16)],
        scratch_shapes=[pltpu.VMEM((S + 16, 512), F32)],
        compiler_params=_cp(("arbitrary",)),
    )(h, pool_w, pool_scale)


def conv_fwd(h, dw, CH=128):
    S = h.shape[0]

    def body(a_ref, g_ref, dw_ref, y_ref, hc_ref, pad_ref):
        hc = a_ref[...] * _sigmoid(g_ref[...])
        hc_ref[...] = hc
        pad_ref[0:32, :] = jnp.zeros((32, 128), F32)
        pad_ref[32:32 + S, :] = hc
        for ch in range(S // CH):
            base = ch * CH + 2
            acc = dw_ref[0:1, :] * pad_ref[base:base + CH, :]
            for k in range(1, CONV_TAPS):
                acc = acc + dw_ref[k:k + 1, :] * pad_ref[base + k:base + k + CH, :]
            y_ref[ch * CH:(ch + 1) * CH, :] = acc

    return pl.pallas_call(
        body, name="conv_fwd", grid=(4,),
        in_specs=[pl.BlockSpec((S, 128), lambda c: (0, c)),
                  pl.BlockSpec((S, 128), lambda c: (0, 4 + c)),
                  pl.BlockSpec((CONV_TAPS, 128), lambda c: (0, c))],
        out_specs=[pl.BlockSpec((S, 128), lambda c: (0, c)), pl.BlockSpec((S, 128), lambda c: (0, c))],
        out_shape=[_sds((S, 512)), _sds((S, 512))],
        scratch_shapes=[pltpu.VMEM((S + 32, 128), F32)],
        compiler_params=_cp(("arbitrary",)),
    )(h, h, dw)


def _masked_sg_w(w_ref, g):
    row = lax.broadcasted_iota(jnp.int32, (128, 128), 0)
    col = lax.broadcasted_iota(jnp.int32, (128, 128), 1)
    return jnp.where(row >= col, w_ref[g], 0.0).astype(BF16)


def odd_post(y, h, cl_g, cl_b, sl_g, sl_b, sg_w, sgb_bc, tm=256):
    S = y.shape[0]
    tm = min(tm, S)

    def body(y_ref, zc_ref, clg, clb, slg, slb, w_ref, sb_ref,
             c_ref, d_ref, xhc_ref, rsc_ref, xhv_ref, rsv_ref, sv_ref):
        lnc, xhc, rsc = _ln_fwd(y_ref[...], clg[...], clb[...])
        c_ref[...] = (lnc * _sigmoid(lnc)).astype(BF16)
        xhc_ref[...] = xhc
        rsc_ref[...] = rsc
        z = _gelu(zc_ref[...])
        vn, xhv, rsv = _ln_fwd(z[:, 512:], slg[...], slb[...])
        xhv_ref[...] = xhv
        rsv_ref[...] = rsv
        vnb = vn.astype(BF16)
        for g in range(4):
            wm = _masked_sg_w(w_ref, g)
            for ch in range(tm // 128):
                rs, cs = slice(ch * 128, (ch + 1) * 128), slice(g * 128, (g + 1) * 128)
                sv_ref[rs, cs] = _dot(wm, vnb[rs, cs]) + sb_ref[g]
        d_ref[...] = (z[:, :512] * sv_ref[...]).astype(BF16)

    return _tok_call(
        "odd_post", body, [y, (h, 1024, 1)], [cl_g, cl_b, sl_g, sl_b, sg_w, sgb_bc],
        [_sds((S, 512), BF16), _sds((S, 512), BF16), _sds((S, 512)), _sds((S, 1)),
         _sds((S, 512)), _sds((S, 1)), _sds((S, 512))], tm=tm)


def mm_out_ln(l1, l2, x, w, g, b, dep=None):
    S, D = x.shape

    def body(l1_ref, l2_ref, x_ref, w_ref, g_ref, b_ref, y_ref, xh_ref, rs_ref):
        mix = _dot(l1_ref[...], w_ref[0:512, :]) + _dot(l2_ref[...], w_ref[512:1024, :])
        y, xh, rs = _ln_fwd(ALPHA * x_ref[...] + mix, g_ref[...], b_ref[...])
        y_ref[...] = y
        xh_ref[...] = xh
        rs_ref[...] = rs

    return _tok_call("mm_out_ln", body, [l1, l2, x], [w, g, b],
                     [_sds((S, D)), _sds((S, D)), _sds((S, 1))], dep=dep)


def ffn_up(x1, wg, wu, layer):
    S, D = x1.shape
    F = wg.shape[-1]

    def body(x_ref, wg_ref, wu_ref, gate_ref, up_ref, hb_ref, xb_ref):
        xb = x_ref[...].astype(BF16)
        xb_ref[...] = xb
        gate = _dot(xb, wg_ref[...])
        up = _dot(xb, wu_ref[...])
        gate_ref[...] = gate.astype(BF16)
        up_ref[...] = up.astype(BF16)
        hb_ref[...] = (gate * _sigmoid(gate) * up).astype(BF16)

    return _tok_call("ffn_up", body, [x1], [(wg, layer), (wu, layer)],
                     [_sds((S, F), BF16), _sds((S, F), BF16), _sds((S, F), BF16), _sds((S, D), BF16)])


def ffn_down_ln(hb, x1, wd, layer, g, b):
    S, D = x1.shape

    def body(h_ref, x_ref, w_ref, g_ref, b_ref, y_ref, xh_ref, rs_ref):
        f = _dot(h_ref[...], w_ref[...])
        y, xh, rs = _ln_fwd(ALPHA * x_ref[...] + f, g_ref[...], b_ref[...])
        y_ref[...] = y
        xh_ref[...] = xh
        rs_ref[...] = rs

    return _tok_call("ffn_down_ln", body, [hb, x1], [(wd, layer), g, b],
                     [_sds((S, D)), _sds((S, D)), _sds((S, 1))])


def ple_fwd(x2, p, wpg, wpp, layer, bg, target=None, dep=None):
    S, D = x2.shape
    last = target is not None

    def body(*refs):
        if last:
            x_ref, p_ref, t_ref, wg_ref, wp_ref, b_ref, x3_ref, sg_ref, pp_ref, xb_ref, pb_ref, dy_ref, ls_ref = refs
        else:
            x_ref, p_ref, wg_ref, wp_ref, b_ref, x3_ref, sg_ref, pp_ref, xb_ref, pb_ref = refs
        x = x_ref[...]
        xb = x.astype(BF16)
        pb = p_ref[...].astype(BF16)
        xb_ref[...] = xb
        pb_ref[...] = pb
        sg = _sigmoid(_dot(xb, wg_ref[...]) + b_ref[...])
        pp = _dot(pb, wp_ref[...])
        sg_ref[...] = sg.astype(BF16)
        pp_ref[...] = pp.astype(BF16)
        x3 = x + sg * pp
        x3_ref[...] = x3
        if last:
            err = x3 - t_ref[...]
            dy_ref[...] = err * (1.0 / D)
            _acc(ls_ref, _colsum(err * err))

    outs = [_sds((S, D)), _sds((S, D), BF16), _sds((S, D), BF16), _sds((S, D), BF16), _sds((S, p.shape[1]), BF16)]
    tiled = [x2, p] + ([target] if last else [])
    if last:
        outs.append(_sds((S, D)))
    return _tok_call("ple_fwd", body, tiled, [(wpg, layer), (wpp, layer), bg], outs,
                     [_sds((1, D))] if last else [], dep=dep)


def ple_bwd(dx3, sg, pp, wpg, layer, dep=None):
    S, D = dx3.shape

    def body(d_ref, sg_ref, pp_ref, w_ref, dx_ref, dgp_ref, dpp_ref, dbg_ref):
        d, sg = d_ref[...], sg_ref[...].astype(F32)
        dgp = d * pp_ref[...].astype(F32) * sg * (1.0 - sg)
        dgpb = dgp.astype(BF16)
        dgp_ref[...] = dgpb
        dpp_ref[...] = (d * sg).astype(BF16)
        dx_ref[...] = d + _dot_nt(dgpb, w_ref[...])
        _acc(dbg_ref, _colsum(dgp))

    return _tok_call("ple_bwd", body, [dx3, sg, pp], [(wpg, layer)],
                     [_sds((S, D)), _sds((S, D), BF16), _sds((S, D), BF16)], [_sds((1, D))], dep=dep)


def ffn_bwd_a(dx2, xh, rs, g, gate, up, wd, layer):
    S, D = dx2.shape
    F = gate.shape[1]

    def body(d_ref, xh_ref, rs_ref, gate_ref, up_ref, g_ref, w_ref,
             dr_ref, drb_ref, dg_ref, du_ref, dlg_ref, dlb_ref):
        d, xh = d_ref[...], xh_ref[...]
        dr = _ln_bwd(d, xh, rs_ref[...], g_ref[...])
        drb = dr.astype(BF16)
        dr_ref[...] = dr
        drb_ref[...] = drb
        _acc(dlg_ref, _colsum(d * xh))
        _acc(dlb_ref, _colsum(d))
        dh = _dot_nt(drb, w_ref[...])
        gate, up = gate_ref[...].astype(F32), up_ref[...].astype(F32)
        s = _sigmoid(gate)
        dg_ref[...] = (dh * up * s * (1.0 + gate * (1.0 - s))).astype(BF16)
        du_ref[...] = (dh * gate * s).astype(BF16)

    return _tok_call("ffn_bwd_a", body, [dx2, xh, rs, gate, up], [g, (wd, layer)],
                     [_sds((S, D)), _sds((S, D), BF16), _sds((S, F), BF16), _sds((S, F), BF16)],
                     [_sds((1, D)), _sds((1, D))])


def ffn_bwd_b(dr, dgate_b, dup_b, wg, wu, layer, dep=None):
    S, D = dr.shape

    def body(dr_ref, dg_ref, du_ref, wg_ref, wu_ref, dx_ref):
        dx_ref[...] = (ALPHA * dr_ref[...] + _dot_nt(dg_ref[...], wg_ref[...])
                       + _dot_nt(du_ref[...], wu_ref[...]))

    return _tok_call("ffn_bwd_b", body, [dr, dgate_b, dup_b], [(wg, layer), (wu, layer)], [_sds((S, D))], dep=dep)[0]


def mix_bwd(dx1, xh, rs, g, w):
    S, D = dx1.shape

    def body(d_ref, xh_ref, rs_ref, g_ref, w_ref, dr_ref, dmb_ref, dl_ref, dlg_ref, dlb_ref):
        d, xh = d_ref[...], xh_ref[...]
        dr = _ln_bwd(d, xh, rs_ref[...], g_ref[...])
        drb = dr.astype(BF16)
        dr_ref[...] = dr
        dmb_ref[...] = drb
        dl_ref[...] = _dot_nt(drb, w_ref[...])
        _acc(dlg_ref, _colsum(d * xh))
        _acc(dlb_ref, _colsum(d))

    return _tok_call("mix_bwd", body, [dx1, xh, rs], [g, w],
                     [_sds((S, D)), _sds((S, D), BF16), _sds((S, D))], [_sds((1, D)), _sds((1, D))])


def dx_in(dr, pieces, w):
    S, D = dr.shape
    offs = [o for _, o in pieces]
    widths = [a.shape[1] for a, _ in pieces]

    def body(*refs):
        dr_ref, prefs, w_ref, dx_ref = refs[0], refs[1:1 + len(pieces)], refs[-2], refs[-1]
        acc = ALPHA * dr_ref[...]
        for pr, o, n in zip(prefs, offs, widths):
            acc = acc + _dot_nt(pr[...], w_ref[:, o:o + n])
        dx_ref[...] = acc

    return _tok_call("dx_in", body, [dr] + [a for a, _ in pieces], [w], [_sds((S, D))])[0]


def odd_post_bwd(dl, h, xhc, rsc, xhv, rsv, sv, cl_g, cl_b, sl_g, sl_b, sg_w, tm=256, dep=None):
    S = dl.shape[0]
    tm = min(tm, S)

    def body(dl_ref, zc_ref, xhc_ref, rsc_ref, xhv_ref, rsv_ref, sv_ref, clg, clb, slg, slb, w_ref,
             dy_ref, dzc_ref, dclg_ref, dclb_ref, dslg_ref, dslb_ref, dwm_ref, dsb_ref, dvn_ref):
        first = pl.program_id(0) == 0
        last = pl.program_id(0) == pl.num_programs(0) - 1
        dc, dd = dl_ref[:, 0:512], dl_ref[:, 512:1024]
        xhc = xhc_ref[...]
        lnc = xhc * clg[...] + clb[...]
        s = _sigmoid(lnc)
        dlnc = dc * s * (1.0 + lnc * (1.0 - s))
        dy_ref[...] = _ln_bwd(dlnc, xhc, rsc_ref[...], clg[...])
        _acc(dclg_ref, _colsum(dlnc * xhc))
        _acc(dclb_ref, _colsum(dlnc))
        zc = zc_ref[...]
        z = _gelu(zc)
        dsv = dd * z[:, :512]
        dsvb = dsv.astype(BF16)
        xhv = xhv_ref[...]
        vnb = (xhv * slg[...] + slb[...]).astype(BF16)

        @pl.when(first)
        def _():
            dwm_ref[...] = jnp.zeros_like(dwm_ref)
            dsb_ref[...] = jnp.zeros_like(dsb_ref)

        for g in range(4):
            wm = _masked_sg_w(w_ref, g)
            for ch in range(tm // 128):
                rs_, cs = slice(ch * 128, (ch + 1) * 128), slice(g * 128, (g + 1) * 128)
                dwm_ref[g] += _dot_nt(dsvb[rs_, cs], vnb[rs_, cs])
                dvn_ref[rs_, cs] = _dot_tn(wm, dsvb[rs_, cs])
                dsb_ref[g] += dsv[rs_, cs]
        dvn = dvn_ref[...]
        dvv = _ln_bwd(dvn, xhv, rsv_ref[...], slg[...])
        _acc(dslg_ref, _colsum(dvn * xhv))
        _acc(dslb_ref, _colsum(dvn))
        gg = _gelu_grad(zc)
        dzc_ref[:, 0:512] = (dd * sv_ref[...] * gg[:, :512]).astype(BF16)
        dzc_ref[:, 512:1024] = (dvv * gg[:, 512:]).astype(BF16)

        @pl.when(last)
        def _():
            row = lax.broadcasted_iota(jnp.int32, (128, 128), 0)
            col = lax.broadcasted_iota(jnp.int32, (128, 128), 1)
            for g in range(4):
                dwm_ref[g] = jnp.where(row >= col, dwm_ref[g], 0.0)
                dsb_ref[g] = jnp.broadcast_to(jnp.sum(dsb_ref[g], axis=1, keepdims=True), (128, 128))

    return _tok_call(
        "odd_post_bwd", body, [dl, (h, 1024, 1), xhc, rsc, xhv, rsv, sv], [cl_g, cl_b, sl_g, sl_b, sg_w],
        [_sds((S, 512)), _sds((S, 1024), BF16)],
        [_sds((1, 512)), _sds((1, 512)), _sds((1, 512)), _sds((1, 512)), _sds((4, 128, 128)), _sds((4, 128, 128))],
        tm=tm, scratch=[pltpu.VMEM((tm, 512), F32)], dep=dep)


def conv_bwd(dy, hc, h, dw, CH=128):
    S = dy.shape[0]

    def body(dy_ref, hc_ref, a_ref, g_ref, dw_ref, da_ref, dg_ref, ddw_ref, padh_ref, padd_ref, dhc_ref):
        padh_ref[0:32, :] = jnp.zeros((32, 128), F32)
        padh_ref[32:32 + S, :] = hc_ref[...]
        padd_ref[0:S, :] = dy_ref[...]
        padd_ref[S:S + 32, :] = jnp.zeros((32, 128), F32)
        taps = [jnp.zeros((1, 128), F32) for _ in range(CONV_TAPS)]
        for ch in range(S // CH):
            b0 = ch * CH
            dyc = padd_ref[b0:b0 + CH, :]
            acc = dw_ref[0:1, :] * padd_ref[b0 + 30:b0 + 30 + CH, :]
            taps[0] = taps[0] + _colsum(dyc * padh_ref[b0 + 2:b0 + 2 + CH, :])
            for k in range(1, CONV_TAPS):
                acc = acc + dw_ref[k:k + 1, :] * padd_ref[b0 + 30 - k:b0 + 30 - k + CH, :]
                taps[k] = taps[k] + _colsum(dyc * padh_ref[b0 + 2 + k:b0 + 2 + k + CH, :])
            dhc_ref[b0:b0 + CH, :] = acc
        for k in range(CONV_TAPS):
            ddw_ref[k:k + 1, :] = taps[k]
        dhc = dhc_ref[...]
        s = _sigmoid(g_ref[...])
        da_ref[...] = (dhc * s).astype(BF16)
        dg_ref[...] = (dhc * a_ref[...] * s * (1.0 - s)).astype(BF16)

    return pl.pallas_call(
        body, name="conv_bwd", grid=(4,),
        in_specs=[pl.BlockSpec((S, 128), lambda c: (0, c)),
                  pl.BlockSpec((S, 128), lambda c: (0, c)),
                  pl.BlockSpec((S, 128), lambda c: (0, c)),
                  pl.BlockSpec((S, 128), lambda c: (0, 4 + c)),
                  pl.BlockSpec((CONV_TAPS, 128), lambda c: (0, c))],
        out_specs=[pl.BlockSpec((S, 128), lambda c: (0, c)), pl.BlockSpec((S, 128), lambda c: (0, c)),
                   pl.BlockSpec((CONV_TAPS, 128), lambda c: (0, c))],
        out_shape=[_sds((S, 512), BF16), _sds((S, 512), BF16), _sds((CONV_TAPS, 512))],
        scratch_shapes=[pltpu.VMEM((S + 32, 128), F32), pltpu.VMEM((S + 32, 128), F32), pltpu.VMEM((S, 128), F32)],
        compiler_params=_cp(("arbitrary",)),
    )(dy, hc, h, h, dw)


def attn_bwd(qkv, dl, tb, T=256, dep=None):
    S = qkv.shape[0]
    T = min(T, S)
    nq = S // T

    def body(q_ref, k_ref, v_ref, do_ref, t_ref, dq_ref, dk_ref, dv_ref,
             dka_ref, dva_ref, dqa_ref, pc_ref, gc_ref, qh_ref, doh_ref):
        i = pl.program_id(0)
        hm0 = lax.broadcasted_iota(jnp.int32, (1, 128), 1) < 64
        r2 = lax.broadcasted_iota(jnp.int32, (2 * T, T), 0)
        c2 = lax.broadcasted_iota(jnp.int32, (2 * T, T), 1)
        causal = c2 < jnp.where(r2 >= T, r2 - T, r2)
        ur = lax.broadcasted_iota(jnp.int32, (T, T), 0)
        uc = lax.broadcasted_iota(jnp.int32, (T, T), 1)
        u_le = (ur <= uc).astype(BF16)
        u_lt = (ur < uc).astype(BF16)

        @pl.when(i == 0)
        def _():
            dka_ref[...] = jnp.zeros_like(dka_ref)
            dva_ref[...] = jnp.zeros_like(dva_ref)

        dqa_ref[...] = jnp.zeros_like(dqa_ref)
        gc_ref[...] = jnp.zeros_like(gc_ref)
        for pp in range(4):
            cs = slice(pp * 128, (pp + 1) * 128)
            qh_ref[pp] = _stack_heads(q_ref[:, cs] * QK_SCALE, hm0)
            doh_ref[pp] = _stack_heads(do_ref[:, cs], hm0)
            for hd in range(2):
                for half in range(T // 128):
                    pc_ref[pp, hd * T:(hd + 1) * T, half * 128:(half + 1) * 128] = t_ref[2 * pp + hd]

        def block(kb, diag):
            ks = pl.multiple_of(kb * T, T)
            for pp in range(4):
                cs = slice(pp * 128, (pp + 1) * 128)
                kb16 = k_ref[pl.ds(ks, T), cs]
                vb16 = v_ref[pl.ds(ks, T), cs]
                qh, doh = qh_ref[pp], doh_ref[pp]
                z = _dot_nt(qh, kb16)
                sp = _softplus(z)
                a = z - sp
                sig = jnp.exp(a)
                if diag:
                    sp = jnp.where(causal, sp, 0.0)
                pre = _cumsum_mm(sp, u_le)
                rem = pc_ref[pp]
                w = jnp.exp(a - rem + pre)
                if diag:
                    w = jnp.where(causal, w, 0.0)
                gmat = _dot_nt(doh, vb16) * w
                gex = gc_ref[pp] + _cumsum_mm(gmat, u_lt)
                dz = gmat * (1.0 - sig) - sig * gex
                if diag:
                    dz = jnp.where(causal, dz, 0.0)
                dzb = dz.astype(BF16)
                dqa_ref[:, cs] += _dot(_unstack_k(dzb, T), _stack_heads(kb16, hm0))
                dka_ref[pl.ds(ks, T), cs] += _dot_tn(dzb, qh)
                dva_ref[pl.ds(ks, T), cs] += _dot_tn(w.astype(BF16), doh)
                pc_ref[pp] = rem - jnp.broadcast_to(pre[:, T - 1:T], (2 * T, T))
                gc_ref[pp] = jnp.broadcast_to(gex[:, T - 1:T] + gmat[:, T - 1:T], (2 * T, T))

        def step(kb, carry):
            block(kb, False)
            return carry

        lax.fori_loop(0, i, step, 0)
        block(i, True)
        dq_ref[...] = (dqa_ref[...] * QK_SCALE).astype(BF16)

        @pl.when(i == nq - 1)
        def _():
            dk_ref[...] = dka_ref[...].astype(BF16)
            dv_ref[...] = dva_ref[...].astype(BF16)

    deps = [] if dep is None else [dep]
    call_body = body if dep is None else (lambda *refs: body(*refs[:5], *refs[6:]))
    return pl.pallas_call(
        call_body, name="attn_bwd", grid=(nq,),
        in_specs=[pl.BlockSpec((T, 512), lambda i: (i, 0)),
                  pl.BlockSpec((S, 512), lambda i: (0, 1)),
                  pl.BlockSpec((S, 512), lambda i: (0, 2)),
                  pl.BlockSpec((T, 512), lambda i: (i, 0)),
                  pl.BlockSpec((8, T, 128), lambda i: (0, i, 0))] + [ANY] * len(deps),
        out_specs=[pl.BlockSpec((T, 512), lambda i: (i, 0)),
                   pl.BlockSpec((S, 512), lambda i: (0, 0)),
                   pl.BlockSpec((S, 512), lambda i: (0, 0))],
        out_shape=[_sds((S, 512), BF16), _sds((S, 512), BF16), _sds((S, 512), BF16)],
        scratch_shapes=[pltpu.VMEM((S, 512), F32), pltpu.VMEM((S, 512), F32), pltpu.VMEM((T, 512), F32),
                        pltpu.VMEM((4, 2 * T, T), F32), pltpu.VMEM((4, 2 * T, T), F32),
                        pltpu.VMEM((4, 2 * T, 128), BF16), pltpu.VMEM((4, 2 * T, 128), BF16)],
        compiler_params=_cp(("arbitrary",)),
    )(qkv, qkv, qkv, dl, tb, *deps)


def pool_bwd(dl, pooled_b, pool_w, pool_scale, CH=256):
    S = dl.shape[0]
    CH = min(CH, S)

    def body(db_ref, pooled_ref, w_ref, sc_ref, du_ref, dw_ref, dsc_ref, pad_ref, dp_ref):
        pad_ref[S:S + 16, :] = jnp.zeros((16, 128), F32)
        for g, win in enumerate(POOL_WINDOWS):
            cs = slice(g * 128, (g + 1) * 128)
            wq = w_ref[g].astype(BF16)
            dwg = jnp.zeros((128, 128), F32)
            dsc = jnp.zeros((1, 128), F32)
            for ch in range(S // CH):
                rs_ = slice(ch * CH, (ch + 1) * CH)
                db = db_ref[rs_, cs]
                pb = pooled_ref[rs_, cs]
                dsc = dsc + _colsum(db * _dot(pb, wq))
                dmsb = (db * sc_ref[:, cs]).astype(BF16)
                dwg = dwg + _dot_tn(pb, dmsb)
                dpool = _dot_nt(dmsb, wq)
                t = ch * CH + lax.broadcasted_iota(jnp.int32, (CH, 1), 0)
                cnt = jnp.minimum(t + 1, win).astype(F32)
                dp_ref[rs_, :] = dpool
                pad_ref[rs_, :] = dpool / cnt
            dw_ref[g] = dwg
            dsc_ref[:, cs] = dsc
            for ch in range(S // CH):
                base = ch * CH
                acc = pad_ref[base:base + CH, :]
                for sft in range(1, win):
                    acc = acc + pad_ref[base + sft:base + sft + CH, :]
                du_ref[base:base + CH, cs] = (acc - dp_ref[base:base + CH, :]).astype(BF16)

    return pl.pallas_call(
        body, name="pool_bwd", grid=(1,),
        in_specs=[pl.BlockSpec((S, 512), lambda i: (0, 1)),
                  pl.BlockSpec((S, 512), lambda i: (0, 0)),
                  pl.BlockSpec((4, 128, 128), lambda i: (0, 0, 0)),
                  pl.BlockSpec((1, 512), lambda i: (0, 0))],
        out_specs=[pl.BlockSpec((S, 512), lambda i: (0, 0)),
                   pl.BlockSpec((4, 128, 128), lambda i: (0, 0, 0)),
                   pl.BlockSpec((1, 512), lambda i: (0, 0))],
        out_shape=[_sds((S, 512), BF16), _sds((4, 128, 128)), _sds((1, 512))],
        scratch_shapes=[pltpu.VMEM((S + 16, 128), F32), pltpu.VMEM((S, 128), F32)],
        compiler_params=_cp(("arbitrary",)),
    )(dl, pooled_b, pool_w, pool_scale)


def tn_into(a, b, out, out_b, r0, c0, tk=1024, tn=512):
    S, K = a.shape
    N = b.shape[1]
    tk, tn = min(tk, K), min(tn, N)
    assert K % tk == 0 and N % tn == 0 and r0 % tk == 0 and c0 % tn == 0
    rb, cb = r0 // tk, c0 // tn
    fresh = isinstance(out, jax.ShapeDtypeStruct)

    def body(*refs):
        a_ref, b_ref, o_ref, ob_ref = refs[0], refs[1], refs[-2], refs[-1]
        r = _dot_tn(a_ref[...], b_ref[...])
        o_ref[...] = r
        ob_ref[...] = r.astype(BF16)

    ospec = pl.BlockSpec((tk, tn), lambda i, j: (rb + i, cb + j))
    in_specs = [pl.BlockSpec((S, tk), lambda i, j: (0, i)), pl.BlockSpec((S, tn), lambda i, j: (0, j))]
    args = [a, b]
    aliases = {}
    if not fresh:
        in_specs += [pl.BlockSpec(memory_space=pl.ANY), pl.BlockSpec(memory_space=pl.ANY)]
        args += [out, out_b]
        aliases = {2: 0, 3: 1}
    shp = out.shape
    return pl.pallas_call(
        body, name="tn_grad", grid=(K // tk, N // tn),
        in_specs=in_specs, out_specs=[ospec, ospec],
        out_shape=[_sds(shp, F32), _sds(shp, BF16)],
        input_output_aliases=aliases,
        compiler_params=_cp(("arbitrary", "arbitrary")),
    )(*args)


def _row(a, i):
    return a[i:i + 1]


MIXER_NAMES = (("even_w_in", "even_w_out"), ("odd_w_in", "odd_w_out"))


def _tn_group(items):
    out = {}
    for name, (shape, parts) in items.items():
        g, gb = _sds(shape, F32), _sds(shape, BF16)
        for a, b, r0, c0 in parts:
            g, gb = tn_into(a, b, g, gb, r0, c0)
        out[name] = (g, gb)
    return out


def fwd_layer(i, xin, p_i, target, comm):
    s = {}
    W = comm.weights(("mix", i), xin)
    w_in = W[MIXER_NAMES[i][0]]
    if i == 0:
        s["h"], s["xb"], s["qkv"] = mm_in(xin, w_in, nb16=1536)
        comm.poke(("in", i), s["h"])
        s["l1"], s["tb"] = attn_fwd(s["qkv"])
        s["l2"], s["pooled"] = pool_fwd(s["h"], W["pool_w"], W["pool_scale"])
    else:
        s["h"], s["xb"] = mm_in(xin, w_in)
        comm.poke(("in", i), s["h"])
        s["y"], s["hc"] = conv_fwd(s["h"], W["conv_dw"])
        sgb_bc = jnp.broadcast_to(W["sg_b"][:, :, None], (4, 128, 128))
        (s["l1"], s["l2"], s["xhc"], s["rsc"], s["xhv"], s["rsv"], s["sv"]) = odd_post(
            s["y"], s["h"], W["conv_ln_g"], W["conv_ln_b"], W["sg_ln_g"], W["sg_ln_b"], W["sg_w"], sgb_bc)
    tok = comm.poke(("mixed", i), s["l2"])
    W = comm.weights(("out", i), s["l2"])
    x1, s["xh1"], s["rs1"] = mm_out_ln(s["l1"], s["l2"], xin, W[MIXER_NAMES[i][1]], _row(W["ln_mix_g"], i),
                                       _row(W["ln_mix_b"], i), dep=tok)
    W = comm.weights(("ffn", i), x1)
    s["gate"], s["up"], s["hb"], s["x1b"] = ffn_up(x1, W["ffn_w_gate%d" % i], W["ffn_w_up%d" % i], None)
    x2, s["xh2"], s["rs2"] = ffn_down_ln(s["hb"], x1, W["ffn_w_down%d" % i], None,
                                         _row(W["ln_ffn_g"], i), _row(W["ln_ffn_b"], i))
    tok = comm.poke(("ffn", i), x2)
    outs = ple_fwd(x2, p_i, W["ple_w_gate%d" % i], W["ple_w_proj%d" % i], None, _row(W["ple_b_gate"], i), target,
                   dep=tok)
    s["sg"], s["pp"], s["x2b"], s["pb"] = outs[1:5]
    return outs[0], s, outs[5:]


def bwd_layer(i, dx, s, W, comm, tok=None):
    small = {}
    D = dx.shape[1]
    FP = W["ffn_w_gate%d" % i].shape[1]
    dx2, dgp_b, dpp_b, small["ple_b_gate"] = ple_bwd(dx, s["sg"], s["pp"], W["ple_w_gate%d" % i], None, dep=tok)
    dr2, dr2_b, dgate_b, dup_b, small["ln_ffn_g"], small["ln_ffn_b"] = ffn_bwd_a(
        dx2, s["xh2"], s["rs2"], _row(W["ln_ffn_g"], i), s["gate"], s["up"], W["ffn_w_down%d" % i], None)
    tok = comm.grads(_tn_group({
        "ple_w_gate%d" % i: ((D, D), [(s["x2b"], dgp_b, 0, 0)]),
        "ple_w_proj%d" % i: ((s["pb"].shape[1], D), [(s["pb"], dpp_b, 0, 0)]),
        "ffn_w_down%d" % i: ((FP, D), [(s["hb"], dr2_b, 0, 0)]),
        "ffn_w_gate%d" % i: ((D, FP), [(s["x1b"], dgate_b, 0, 0)]),
        "ffn_w_up%d" % i: ((D, FP), [(s["x1b"], dup_b, 0, 0)])}))
    dx1 = ffn_bwd_b(dr2, dgate_b, dup_b, W["ffn_w_gate%d" % i], W["ffn_w_up%d" % i], None, dep=tok)
    iname, oname = MIXER_NAMES[i]
    dr1, dmix_b, dl, small["ln_mix_g"], small["ln_mix_b"] = mix_bwd(
        dx1, s["xh1"], s["rs1"], _row(W["ln_mix_g"], i), W[oname])
    tok = comm.poke(("bwd", i), dl)
    if i == 1:
        (dy, dzc_b, small["conv_ln_g"], small["conv_ln_b"], small["sg_ln_g"], small["sg_ln_b"],
         small["sg_w"], dsb) = odd_post_bwd(dl, s["h"], s["xhc"], s["rsc"], s["xhv"], s["rsv"], s["sv"],
                                            W["conv_ln_g"], W["conv_ln_b"], W["sg_ln_g"], W["sg_ln_b"], W["sg_w"],
                                            dep=tok)
        small["sg_b"] = dsb[:, :, 0]
        da_b, dg_b, small["conv_dw"] = conv_bwd(dy, s["hc"], s["h"], W["conv_dw"])
        pieces = [(da_b, 0), (dg_b, 512), (dzc_b, 1024)]
    else:
        dq_b, dk_b, dv_b = attn_bwd(s["qkv"], dl, s["tb"], dep=tok)
        du_b, small["pool_w"], small["pool_scale"] = pool_bwd(dl, s["pooled"], W["pool_w"], W["pool_scale"])
        pieces = [(dq_b, 0), (dk_b, 512), (dv_b, 1024), (du_b, 1536)]
    dxin = dx_in(dr1, pieces, W[iname])
    tok = comm.grads(_tn_group({
        oname: ((1024, D), [(s["l1"], dmix_b, 0, 0), (s["l2"], dmix_b, 512, 0)]),
        iname: ((D, 2048), [(s["xb"], a, 0, off) for a, off in pieces])}))
    return dxin, small, tok


def run_layers(x, p, target, comm):
    saved, xin = [], x
    for i in range(2):
        xin, s, extra = fwd_layer(i, xin, p[i], target if i == 1 else None, comm)
        saved.append(s)
    dx, sq = extra
    W = comm.all_weights()
    per_layer = [None, None]
    tok = None
    for i in (1, 0):
        dx, per_layer[i], tok = bwd_layer(i, dx, saved[i], W, comm, tok)
    small = {}
    for k in ("ln_mix_g", "ln_mix_b", "ln_ffn_g", "ln_ffn_b", "ple_b_gate"):
        small[k] = jnp.concatenate([per_layer[0][k], per_layer[1][k]], axis=0)
    for i in range(2):
        small.update({k: v for k, v in per_layer[i].items() if k not in small})
    return sq, dx, small


def _big_table():
    t = {}
    for nm in ("even", "odd"):
        t[nm + "_w_in"] = ((1024, 2048), 1, 256, 256, nm + "_w_in", 0)
        t[nm + "_w_out"] = ((1024, 1024), 0, 128, 128, nm + "_w_out", 0)
    for l in range(2):
        t["ffn_w_gate%d" % l] = ((1024, 8 * FF_PAD), 1, FF_PAD, FF_SHARD, "ffn_w_gate", l)
        t["ffn_w_up%d" % l] = ((1024, 8 * FF_PAD), 1, FF_PAD, FF_SHARD, "ffn_w_up", l)
        t["ffn_w_down%d" % l] = ((8 * FF_PAD, 1024), 0, FF_PAD, FF_SHARD, "ffn_w_down", l)
        t["ple_w_gate%d" % l] = ((1024, 1024), 0, 128, 128, "ple_w_gate", l)
        t["ple_w_proj%d" % l] = ((256, 1024), 1, 128, 128, "ple_w_proj", l)
    return t


BIG = _big_table()
SMALL_SPEC = ((N_DEV, 40, 64), 0, 1, 1)
_LAYER_GROUP = lambda l: ["ffn_w_gate%d" % l, "ffn_w_up%d" % l, "ffn_w_down%d" % l, "ple_w_gate%d" % l, "ple_w_proj%d" % l]
AG_GROUPS = (["even_w_in"], ["even_w_out"], _LAYER_GROUP(0), ["odd_w_in", "odd_w_out", "small"], _LAYER_GROUP(1))
AG_NEED = {("mix", 0): 0, ("out", 0): 1, ("ffn", 0): 2, ("mix", 1): 3, ("ffn", 1): 4}
AG_PASS = {("in", 0): 1, ("mixed", 0): 2, ("ffn", 0): 3, ("mixed", 1): 4}
ANY = pl.BlockSpec(memory_space=pl.ANY)
SEM = pl.BlockSpec(memory_space=pltpu.SEMAPHORE)


def _spec(name):
    return SMALL_SPEC if name == "small" else BIG[name]


def _win_shape(spec):
    full, axis, w = spec[:3]
    return tuple(w if d == axis else n for d, n in enumerate(full))


def _window(ref, axis, w, j):
    idx = [slice(None)] * len(ref.shape)
    idx[axis] = pl.ds(j, 1) if w == 1 else pl.ds(pl.multiple_of(j * w, w), w)
    return ref.at[tuple(idx)]


def _mesh_pos():
    return lax.axis_index("x"), lax.axis_index("y"), lax.axis_index("c")


def split_call(name, arrays, starts=(), waits=(), sems_in=(), new=(), after=None):
    n, nn, ns = len(arrays), len(new), len(starts)
    flat_sems = [s for pair in sems_in for s in pair]

    def body(*refs):
        arr = list(refs[:n])
        sin = refs[n:n + len(flat_sems)]
        outs = refs[n + len(flat_sems) + (after is not None):]
        data = arr + list(outs[n:n + nn])
        for p, k, kind, mk in waits:
            d = mk(data, sin[2 * p].at[k], sin[2 * p + 1].at[k])
            d.wait_send() if kind == "send" else d.wait_recv()
        if ns:
            send, recv = outs[n + nn], outs[n + nn + 1]
            for k, mk in enumerate(starts):
                mk(data, send.at[k], recv.at[k]).start()
        outs[-1][...] = jnp.zeros((8, 128), F32)

    sem_out = [pltpu.SemaphoreType.DMA((ns,)), pltpu.SemaphoreType.DMA((ns,))] if ns else []
    res = pl.pallas_call(
        body, name=name,
        in_specs=[ANY] * n + [SEM] * len(flat_sems) + ([ANY] if after is not None else []),
        out_specs=[ANY] * (n + nn) + [SEM] * len(sem_out) + [pl.BlockSpec(memory_space=pltpu.VMEM)],
        out_shape=[_sds(a.shape, a.dtype) for a in arrays] + list(new) + sem_out + [_sds((8, 128), F32)],
        input_output_aliases={a: a for a in range(n)},
        compiler_params=pltpu.CompilerParams(has_side_effects=pltpu.SideEffectType.DATAFLOW_SIDE_EFFECTING),
    )(*arrays, *flat_sems, *([after] if after is not None else []))
    return list(res[:n + nn]), (tuple(res[n + nn:n + nn + 2]) if ns else None), res[-1]


def _remote(src, dst, send_sem, recv_sem, dev):
    return pltpu.make_async_remote_copy(src_ref=src, dst_ref=dst, send_sem=send_sem, recv_sem=recv_sem,
                                        device_id=dev, device_id_type=MESH_T)


class Gatherer:
    def __init__(self, groups, arrays, specs, prefix):
        self.groups, self.specs, self.prefix = groups, specs, prefix
        self.names = [nm for g in groups for nm in g]
        self.arr = dict(zip(self.names, arrays))
        self.fwd_sems = {}
        self.forwarded = set()

    @staticmethod
    def _mk_first(ai, spec, k):
        def mk(refs, ss, rs):
            x, y, c = _mesh_pos()
            dev = [(x, y, 1 - c), (1 - x, y, c), (x, 1 - y, c), (1 - x, 1 - y, c)][k]
            win = _window(refs[ai], spec[1], spec[2], 4 * x + 2 * y + c)
            return _remote(win, win, ss, rs, dev)
        return mk

    @staticmethod
    def _mk_fwd(ai, spec, j):
        def mk(refs, ss, rs):
            x, y, c = _mesh_pos()
            px, py = [(1 - x, y), (x, 1 - y), (1 - x, 1 - y)][j]
            win = _window(refs[ai], spec[1], spec[2], 4 * px + 2 * py + c)
            return _remote(win, win, ss, rs, (x, y, 1 - c))
        return mk

    def start(self, after=None):
        starts = [self._mk_first(ai, self.specs[nm], k) for ai, nm in enumerate(self.names) for k in range(4)]
        arrs, self.first_sems, tok = split_call(self.prefix + "_start", [self.arr[nm] for nm in self.names],
                                                starts=starts, after=after)
        self.arr = dict(zip(self.names, arrs))
        return tok

    def forward(self, g, after=None):
        if g in self.forwarded:
            return None
        self.forwarded.add(g)
        names = self.groups[g]
        waits = [(0, 4 * self.names.index(nm) + 1 + j, "recv", self._mk_fwd(ai, self.specs[nm], j))
                 for ai, nm in enumerate(names) for j in range(3)]
        starts = [self._mk_fwd(ai, self.specs[nm], j) for ai, nm in enumerate(names) for j in range(3)]
        arrs, self.fwd_sems[g], tok = split_call(
            "%s_forward%d" % (self.prefix, g), [self.arr[nm] for nm in names], starts=starts, waits=waits,
            sems_in=[self.first_sems], after=after)
        self.arr.update(zip(names, arrs))
        return tok

    def finish(self, g, after=None):
        self.forward(g, after)
        names = self.groups[g]
        waits = []
        for ai, nm in enumerate(names):
            base = 4 * self.names.index(nm)
            waits.append((0, base, "recv", self._mk_first(ai, self.specs[nm], 0)))
            waits += [(1, 3 * ai + j, "recv", self._mk_fwd(ai, self.specs[nm], j)) for j in range(3)]
            waits += [(0, base + k, "send", self._mk_first(ai, self.specs[nm], k)) for k in range(4)]
            waits += [(1, 3 * ai + j, "send", self._mk_fwd(ai, self.specs[nm], j)) for j in range(3)]
        arrs, _, _ = split_call(
            "%s_finish%d" % (self.prefix, g), [self.arr[nm] for nm in names], waits=waits,
            sems_in=[self.first_sems, self.fwd_sems[g]], after=after)
        self.arr.update(zip(names, arrs))
        return {nm: self.arr[nm] for nm in names}


class Reducer:
    def __init__(self, cq_arr, adam):
        self.cq_arr, self.adam = cq_arr, adam
        self.groups = []
        self.n = 0
        self.last = None

    @staticmethod
    def _mk1(gi, li, spec, q):
        def mk(refs, ss, rs):
            x, y, c = _mesh_pos()
            return _remote(_window(refs[gi], spec[1], spec[2], 2 * q + (1 - c)), refs[li].at[q], ss, rs, (x, y, 1 - c))
        return mk

    @staticmethod
    def _mk2(si, li, d):
        def mk(refs, ss, rs):
            x, y, c = _mesh_pos()
            qd = lax.rem(2 * x + y + d, 4)
            return _remote(refs[si].at[d - 1], refs[li].at[3 - d], ss, rs, (lax.div(qd, 2), lax.rem(qd, 2), c))
        return mk

    def add(self, grads, after=None):
        names = list(grads)
        m = len(names)
        starts = [self._mk1(ai, m + ai, BIG[nm], q) for ai, nm in enumerate(names) for q in range(4)]
        new = [_sds((4,) + _win_shape(BIG[nm]), BF16) for nm in names]
        res, sems, tok = split_call("rs1_start%d" % self.n, [grads[nm][1] for nm in names], starts=starts, new=new,
                                    after=after)
        self.groups.append(dict(names=names, g=[grads[nm][0] for nm in names], starts=starts, buf=res, sems=sems,
                                stage=1, age=0, idx=self.n))
        self.n += 1
        return tok

    def step(self, after):
        tok = None
        for grp in self.groups:
            names, m = grp["names"], len(grp["names"])
            if grp["stage"] == 1:
                waits = [(0, k, kind, mk) for k, mk in enumerate(grp["starts"]) for kind in ("send", "recv")]
                res, _, _ = split_call("rs1_wait%d" % grp["idx"], grp["buf"], waits=waits, sems_in=[grp["sems"]], after=after)
                land1 = res[m:]
                s1b = [add_pairs(g, l, BIG[nm], self.cq_arr) for nm, g, l in zip(names, grp["g"], land1)]
                starts = [self._mk2(ai, m + ai, d) for ai in range(m) for d in (1, 2, 3)]
                new = [_sds(a.shape, BF16) for a in s1b]
                res, sems, tok = split_call("rs2_start%d" % grp["idx"], s1b, starts=starts, new=new, after=tok)
                grp.update(stage=2, land1=land1, starts=starts, buf=res, sems=sems)
        return tok

    def finish_oldest(self):
        for grp in self.groups:
            if grp["stage"] == 2:
                names, m = grp["names"], len(grp["names"])
                waits = [(0, k, kind, mk) for k, mk in enumerate(grp["starts"]) for kind in ("send", "recv")]
                res, _, _ = split_call("rs2_wait%d" % grp["idx"], grp["buf"], waits=waits, sems_in=[grp["sems"]],
                                       after=self.last)
                for nm, g, l1, l2 in zip(names, grp["g"], grp["land1"], res[m:]):
                    self.last = self.adam(nm, g, l1, l2, self.last)
                grp["stage"] = 3
                return True
        return False


def pack_weights(args, arg_names, small_blk, names, j_arr):
    n_in = len(args)

    def body(j_ref, *refs):
        for o, nm in enumerate(names):
            dst = refs[n_in + 1 + o]
            if nm == "small":
                dst[...] = refs[n_in][...]
                continue
            _, axis, w, valid, arg, layer = BIG[nm]
            src = refs[arg_names.index(arg)][layer].astype(BF16)
            if valid == w:
                dst[...] = src
            else:
                dst[...] = jnp.zeros(dst.shape, BF16)
                if axis == 1:
                    dst[:, 0:valid] = src
                else:
                    dst[0:valid, :] = src

    def ispec(a):
        return pl.BlockSpec(a.shape, lambda i, j_ref: (0, 0, 0))

    def ospec(spec):
        axis, nd = spec[1], len(spec[0])
        return pl.BlockSpec(_win_shape(spec),
                            lambda i, j_ref, axis=axis, nd=nd: tuple(j_ref[0] if d == axis else 0 for d in range(nd)))

    specs = [_spec(nm) for nm in names]
    return pl.pallas_call(
        body, name="pack_weights",
        grid_spec=pltpu.PrefetchScalarGridSpec(
            num_scalar_prefetch=1, grid=(1,),
            in_specs=[ispec(a) for a in list(args) + [small_blk]], out_specs=[ospec(s) for s in specs]),
        out_shape=[_sds(s[0], F32 if nm == "small" else BF16) for nm, s in zip(names, specs)],
        compiler_params=_cp(("arbitrary",)),
    )(j_arr, *args, small_blk)


def add_pairs(full, land, spec, cq_arr):
    axis, w = spec[1], spec[2]
    R, C = full.shape

    def chip(d, cq):
        return lax.rem(cq[1] + d + 1, 4)

    if axis == 1:
        tr = R
        grid = (3, R // tr)
        fspec = pl.BlockSpec((tr, w), lambda d, i, cq: (i, 2 * chip(d, cq) + cq[0]))
        lspec = pl.BlockSpec((None, tr, w), lambda d, i, cq: (chip(d, cq), i, 0))
        ospec = pl.BlockSpec((None, tr, w), lambda d, i, cq: (d, i, 0))
    else:
        grid = (3, 1)
        fspec = pl.BlockSpec((w, C), lambda d, i, cq: (2 * chip(d, cq) + cq[0], 0))
        lspec = pl.BlockSpec((None, w, C), lambda d, i, cq: (chip(d, cq), 0, 0))
        ospec = pl.BlockSpec((None, w, C), lambda d, i, cq: (d, 0, 0))

    def body(cq_ref, a_ref, b_ref, ob_ref):
        ob_ref[...] = (a_ref[...] + b_ref[...].astype(F32)).astype(BF16)

    return pl.pallas_call(
        body, name="add_pairs",
        grid_spec=pltpu.PrefetchScalarGridSpec(
            num_scalar_prefetch=1, grid=grid, in_specs=[fspec, lspec], out_specs=[ospec]),
        out_shape=[_sds((3,) + land.shape[1:], BF16)],
        compiler_params=_cp(("arbitrary",) * 2),
    )(cq_arr, full, land)[0]


def _adamw(w, g, m, v):
    m = ADAM_B1 * m + (1.0 - ADAM_B1) * g
    v = ADAM_B2 * v + (1.0 - ADAM_B2) * (g * g)
    m_hat = m / (1.0 - ADAM_B1 ** ADAM_STEP)
    v_hat = v / (1.0 - ADAM_B2 ** ADAM_STEP)
    delta = -ADAM_LR * (m_hat / (jnp.sqrt(v_hat) + ADAM_EPS) + ADAM_WD * w)
    return delta, m, v


def reduce_adamw(full, land1, land, w, m, v, spec, cq_arr, prev=None, dep=None):
    axis, win, valid, layer = spec[1], spec[2], spec[3], spec[5]
    L, R, C = w.shape
    if axis == 1:
        tr = min(1024, R)
        grid = (R // tr,)
        fspec = pl.BlockSpec((tr, win), lambda i, cq: (i, 2 * cq[1] + cq[0]))
        wspec = pl.BlockSpec((None, tr, win), lambda i, cq: (cq[1], i, 0))
        lspec = pl.BlockSpec((3, tr, win), lambda i, cq: (0, i, 0))
        sspec = pl.BlockSpec((None, tr, C), lambda i, cq: (layer, i, 0))
    else:
        grid = (1,)
        fspec = pl.BlockSpec((win, full.shape[1]), lambda i, cq: (2 * cq[1] + cq[0], 0))
        wspec = pl.BlockSpec((None, win, C), lambda i, cq: (cq[1], 0, 0))
        lspec = pl.BlockSpec((3, win, C), lambda i, cq: (0, 0, 0))
        sspec = pl.BlockSpec((None, R, C), lambda i, cq: (layer, 0, 0))

    def body(cq_ref, full_ref, own_ref, land_ref, w_ref, m_ref, v_ref, *rest):
        g_ref, d_ref, nm_ref, nv_ref = rest[-4:]
        if axis == 1:
            rd = lambda r, *lead: r[(*lead, slice(None), slice(0, valid))]
        else:
            rd = lambda r, *lead: r[(*lead, slice(0, valid), slice(None))]
        g = rd(full_ref) + rd(own_ref).astype(F32)
        for k in range(3):
            g = g + rd(land_ref, k).astype(F32)
        g_ref[...] = g
        d, nm, nv = _adamw(w_ref[...], g, m_ref[...], v_ref[...])
        d_ref[...] = d
        nm_ref[...] = nm
        nv_ref[...] = nv

    extra = (list(prev) if prev is not None else []) + ([dep] if dep is not None else [])
    return pl.pallas_call(
        body, name="reduce_adamw",
        grid_spec=pltpu.PrefetchScalarGridSpec(
            num_scalar_prefetch=1, grid=grid,
            in_specs=[fspec, wspec, lspec, sspec, sspec, sspec] + [ANY] * len(extra), out_specs=[sspec] * 4),
        out_shape=[_sds(w.shape)] * 4,
        input_output_aliases={7 + k: k for k in range(4 if prev is not None else 0)},
        compiler_params=_cp(("arbitrary",)),
    )(cq_arr, full, land1, land, w, m, v, *extra)


def place_slot(packed, j_arr):
    R = packed.shape[0]

    def body(j_ref, src, dst):
        dst[...] = src[...]

    return pl.pallas_call(
        body, name="place_slot",
        grid_spec=pltpu.PrefetchScalarGridSpec(
            num_scalar_prefetch=1, grid=(1,),
            in_specs=[pl.BlockSpec((R, 128), lambda i, j: (0, 0))],
            out_specs=[pl.BlockSpec((None, R, 128), lambda i, j: (j[0], 0, 0))]),
        out_shape=[_sds((N_DEV, R, 128))], compiler_params=_cp(("arbitrary",)),
    )(j_arr, packed)[0]


def sum_slots(gathered):
    def body(g_ref, o_ref):
        g = g_ref[0]
        for dev in range(1, N_DEV):
            g = g + g_ref[dev]
        o_ref[...] = g

    return pl.pallas_call(body, name="sum_slots", out_shape=_sds(gathered.shape[1:]), compiler_params=_cp())(gathered)


def small_adamw(gs, wmv):
    k = len(gs)

    def body(*refs):
        for a in range(k):
            g, w, m, v = refs[4 * a:4 * a + 4]
            d, nm, nv = _adamw(w[...], g[...], m[...], v[...])
            refs[4 * k + 3 * a][...] = d
            refs[4 * k + 3 * a + 1][...] = nm
            refs[4 * k + 3 * a + 2][...] = nv

    args = [t for g, tup in zip(gs, wmv) for t in (g,) + tuple(tup)]
    out_shape = [_sds(g.shape) for g in gs for _ in range(3)]
    return pl.pallas_call(body, name="small_adamw", out_shape=out_shape, compiler_params=_cp())(*args)


WEIGHT_NAMES = ("even_w_in", "even_w_out", "pool_w", "pool_scale", "odd_w_in", "odd_w_out", "conv_dw", "conv_ln_g",
                "conv_ln_b", "sg_ln_g", "sg_ln_b", "sg_w", "sg_b", "ln_mix_g", "ln_mix_b", "ffn_w_gate", "ffn_w_up",
                "ffn_w_down", "ln_ffn_g", "ln_ffn_b", "ple_w_proj", "ple_w_gate", "ple_b_gate")
PACK_ARGS = ("even_w_in", "even_w_out", "odd_w_in", "odd_w_out", "ffn_w_gate", "ffn_w_up", "ffn_w_down",
             "ple_w_gate", "ple_w_proj")
REPLICATED = ("pool_w", "pool_scale", "sg_w", "sg_b", "ln_mix_g", "ln_mix_b", "ln_ffn_g", "ln_ffn_b", "ple_b_gate")
SHARDED_SMALL = ("conv_dw", "conv_ln_g", "conv_ln_b", "sg_ln_g", "sg_ln_b")
NATURAL = {"pool_w": (4, 128, 128), "pool_scale": (1, 512), "sg_w": (4, 128, 128), "sg_b": (4, 128),
           "ln_mix_g": (2, 1024), "ln_mix_b": (2, 1024), "ln_ffn_g": (2, 1024), "ln_ffn_b": (2, 1024),
           "ple_b_gate": (2, 1024)}


def kernel(x, p, even_w_in, even_w_out, pool_w, pool_scale, odd_w_in, odd_w_out, conv_dw, conv_ln_g, conv_ln_b, sg_ln_g, sg_ln_b, sg_w, sg_b, ln_mix_g, ln_mix_b, ffn_w_gate, ffn_w_up, ffn_w_down, ln_ffn_g, ln_ffn_b, ple_w_proj, ple_w_gate, ple_b_gate, loss_target, m_even_w_in, m_even_w_out, m_pool_w, m_pool_scale, m_odd_w_in, m_odd_w_out, m_conv_dw, m_conv_ln_g, m_conv_ln_b, m_sg_ln_g, m_sg_ln_b, m_sg_w, m_sg_b, m_ln_mix_g, m_ln_mix_b, m_ffn_w_gate, m_ffn_w_up, m_ffn_w_down, m_ln_ffn_g, m_ln_ffn_b, m_ple_w_proj, m_ple_w_gate, m_ple_b_gate, v_even_w_in, v_even_w_out, v_pool_w, v_pool_scale, v_odd_w_in, v_odd_w_out, v_conv_dw, v_conv_ln_g, v_conv_ln_b, v_sg_ln_g, v_sg_ln_b, v_sg_w, v_sg_b, v_ln_mix_g, v_ln_mix_b, v_ffn_w_gate, v_ffn_w_up, v_ffn_w_down, v_ln_ffn_g, v_ln_ffn_b, v_ple_w_proj, v_ple_w_gate, v_ple_b_gate):
    A = dict(locals())
    mx, my, mc = _mesh_pos()
    j = 4 * mx + 2 * my + mc
    j_arr = j.astype(jnp.int32).reshape(1)
    cq_arr = jnp.stack([mc, 2 * mx + my]).astype(jnp.int32)
    res = {}

    def adam(nm, full, land1, land2, dep):
        arg = BIG[nm][4]
        res[arg] = reduce_adamw(full, land1, land2, A[arg], A["m_" + arg], A["v_" + arg], BIG[nm], cq_arr,
                                res.get(arg), dep)
        return res[arg][0]

    class Comm:
        def __init__(self):
            names = [nm for g in AG_GROUPS for nm in g]
            small_blk = jnp.concatenate([conv_dw[0], conv_ln_g, conv_ln_b, sg_ln_g, sg_ln_b, jnp.zeros((5, 64), F32)], axis=0)
            mine = pack_weights([A[k] for k in PACK_ARGS], PACK_ARGS, small_blk[None], names, j_arr)
            self.gat = Gatherer(AG_GROUPS, mine, {nm: _spec(nm) for nm in names}, "ag")
            self.gat.start()
            self.red = Reducer(cq_arr, adam)
            self.W = {k: A[k].reshape(NATURAL[k]) for k in REPLICATED}

        def weights(self, stage, after):
            if stage in AG_NEED:
                got = self.gat.finish(AG_NEED[stage], after)
                if "small" in got:
                    sm = got.pop("small").transpose(1, 0, 2).reshape(40, 512)
                    got.update(conv_dw=sm[0:31], conv_ln_g=sm[31:32], conv_ln_b=sm[32:33], sg_ln_g=sm[33:34],
                               sg_ln_b=sm[34:35])
                self.W.update(got)
            return self.W

        def all_weights(self):
            return self.W

        def poke(self, tag, after):
            if tag in AG_PASS:
                return self.gat.forward(AG_PASS[tag], after)
            if tag[0] == "bwd":
                return self.red.step(after)
            return None

        def grads(self, grads):
            tok = self.red.step(next(iter(grads.values()))[0])
            return self.red.add(grads, after=tok)

    comm = Comm()
    sq, dx, small = run_layers(x[0], p[:, 0], loss_target[0], comm)
    loss = lax.psum(0.5 * jnp.sum(sq) / x.shape[-1], ("x", "y", "c"))
    red = comm.red
    tok = red.step(dx)

    names = REPLICATED + SHARDED_SMALL
    flat = jnp.concatenate([small[k].reshape(-1) for k in names])
    rows = -(-flat.shape[0] // 1024) * 8
    packed = jnp.pad(flat, (0, rows * 128 - flat.shape[0])).reshape(rows, 128)
    sg = Gatherer((["g"],), [place_slot(packed, j_arr)], {"g": ((N_DEV, rows, 128), 0, 1, 1)}, "sg")
    sg.start(after=tok)
    red.finish_oldest()
    sg.forward(0, after=red.last)
    red.finish_oldest()
    red.finish_oldest()
    gsum_flat = sum_slots(sg.finish(0, after=red.last)["g"]).reshape(-1)
    gs, off = [], 0
    for k in names:
        n = math.prod(small[k].shape)
        g = gsum_flat[off:off + n].reshape(small[k].shape)
        off += n
        if k in SHARDED_SMALL:
            g = lax.dynamic_slice_in_dim(g, j * 64, 64, axis=1)
        gs.append(g.reshape(A[k].shape))
    outs = small_adamw(gs, [(A[k], A["m_" + k], A["v_" + k]) for k in names])
    for a, k in enumerate(names):
        res[k] = (gs[a],) + tuple(outs[3 * a:3 * a + 3])
    red.last = outs[0]
    while red.finish_oldest():
        pass

    out = [loss, dx[None]]
    for part in range(4):
        out += [res[k][part] for k in WEIGHT_NAMES]
    return tuple(out)
```

```python
import functools
import math

import jax
import jax.numpy as jnp
from jax import lax
from jax.experimental import pallas as pl
from jax.experimental.pallas import tpu as pltpu

F32, BF16 = jnp.float32, jnp.bfloat16
ALPHA = 4.0 ** 0.25
LN_EPS = 1e-5
QK_SCALE = 0.125
POOL_WINDOWS = (2, 4, 8, 16)
CONV_TAPS = 31
N_DEV = 8
FF_SHARD, FF_PAD = 352, 384
ADAM_LR, ADAM_B1, ADAM_B2, ADAM_EPS, ADAM_WD, ADAM_STEP = 0.001, 0.9, 0.999, 1e-08, 0.01, 10
VMEM_LIMIT = 56 * 1024 * 1024
MESH_T = pl.DeviceIdType.MESH


def _cp(sem=None):
    return pltpu.CompilerParams(dimension_semantics=sem, vmem_limit_bytes=VMEM_LIMIT)


def _dot(a, b):
    return jnp.dot(a, b, preferred_element_type=F32)


def _dot_nt(a, b):
    return lax.dot_general(a, b, (((1,), (1,)), ((), ())), preferred_element_type=F32)


def _dot_tn(a, b):
    return lax.dot_general(a, b, (((0,), (0,)), ((), ())), preferred_element_type=F32)


def _sigmoid(x):
    return 1.0 / (1.0 + jnp.exp(-x))


def _softplus(z):
    return jnp.maximum(z, 0.0) + jnp.log(1.0 + jnp.exp(-jnp.abs(z)))


_GELU_C = math.sqrt(2.0 / math.pi)


def _gelu(x):
    return 0.5 * x * (1.0 + jnp.tanh(_GELU_C * (x + 0.044715 * x * x * x)))


def _gelu_grad(x):
    t = jnp.tanh(_GELU_C * (x + 0.044715 * x * x * x))
    return 0.5 * (1.0 + t) + 0.5 * x * (1.0 - t * t) * _GELU_C * (1.0 + 3.0 * 0.044715 * x * x)


def _ln_fwd(r, g, b):
    mu = jnp.mean(r, axis=-1, keepdims=True)
    xc = r - mu
    var = jnp.mean(xc * xc, axis=-1, keepdims=True)
    rstd = lax.rsqrt(var + LN_EPS)
    xh = xc * rstd
    return xh * g + b, xh, rstd


def _ln_bwd(dy, xh, rstd, g):
    dxh = dy * g
    m1 = jnp.mean(dxh, axis=-1, keepdims=True)
    m2 = jnp.mean(dxh * xh, axis=-1, keepdims=True)
    return rstd * (dxh - m1 - xh * m2)


def _split2(x):
    hi = x.astype(BF16)
    lo = (x - hi.astype(F32)).astype(BF16)
    return hi, lo


def _colsum(x):
    return jnp.sum(x, axis=0, keepdims=True)


def _tok_call(name, body, tiled, full, out_tiled, out_acc=(), tm=256, scratch=(), dep=None):
    def arr(t):
        return t[0] if isinstance(t, tuple) else t
    full = [t[0] if isinstance(t, tuple) and t[1] is None else t for t in full]
    S = arr(tiled[0]).shape[0]
    tm = min(tm, S)
    n_in = len(tiled) + len(full)
    deps = [] if dep is None else [dep]
    if deps:
        inner = body
        body = lambda *refs: inner(*refs[:n_in], *refs[n_in + 1:])

    def tspec(t):
        if isinstance(t, tuple):
            _, w, cb = t
            return pl.BlockSpec((tm, w), lambda i, cb=cb: (i, cb))
        return pl.BlockSpec((tm, t.shape[1]), lambda i: (i, 0))

    def fspec(t):
        if isinstance(t, tuple):
            a, l = t
            nd = a.ndim - 1
            return pl.BlockSpec((None,) + a.shape[1:], lambda i, l=l, nd=nd: (l,) + (0,) * nd)
        nd = t.ndim
        return pl.BlockSpec(t.shape, lambda i, nd=nd: (0,) * nd)

    def ospec(o):
        return pl.BlockSpec((tm, o.shape[1]), lambda i: (i, 0))

    def aspec(o):
        nd = len(o.shape)
        return pl.BlockSpec(o.shape, lambda i, nd=nd: (0,) * nd)

    outs = pl.pallas_call(
        body, name=name, grid=(S // tm,),
        in_specs=[tspec(t) for t in tiled] + [fspec(t) for t in full] + [ANY] * len(deps),
        out_specs=[ospec(o) for o in out_tiled] + [aspec(o) for o in out_acc],
        out_shape=list(out_tiled) + list(out_acc),
        scratch_shapes=list(scratch),
        compiler_params=_cp(("arbitrary",)),
    )(*[arr(t) for t in tiled], *[arr(t) for t in full], *deps)
    return outs


def _sds(shape, dtype=F32):
    return jax.ShapeDtypeStruct(tuple(shape), dtype)


def _acc(ref, val):
    @pl.when(pl.program_id(0) == 0)
    def _():
        ref[...] = val

    @pl.when(pl.program_id(0) != 0)
    def _():
        ref[...] += val


def mm_in(x, w, nb16=0):
    S, N = x.shape[0], w.shape[1]

    def body(x_ref, w_ref, h_ref, xb_ref, *hb_ref):
        xb = x_ref[...].astype(BF16)
        xb_ref[...] = xb
        h = _dot(xb, w_ref[...])
        h_ref[...] = h
        if nb16:
            hb_ref[0][...] = h[:, 0:nb16].astype(BF16)

    outs = [_sds((S, N)), _sds((S, x.shape[1]), BF16)] + ([_sds((S, nb16), BF16)] if nb16 else [])
    return _tok_call("mm_in", body, [x], [w], outs, tm=512)


def _stack_heads(x, hm0, dtype=BF16):
    return jnp.concatenate([jnp.where(hm0, x, 0), jnp.where(hm0, 0, x)], axis=0).astype(dtype)


def _unstack_k(x, T):
    return jnp.concatenate([x[0:T], x[T:2 * T]], axis=1)


def _cumsum_mm(x, u):
    n = x.shape[0]
    hi, lo = _split2(x)
    r = _dot(jnp.concatenate([hi, lo], axis=0), u)
    return r[0:n] + r[n:2 * n]


def attn_fwd(qkv, T=256):
    S = qkv.shape[0]
    T = min(T, S)
    nq = S // T

    def body(q_ref, k_ref, v_ref, o_ref, t_ref, acc_ref, c_ref, qh_ref):
        i = pl.program_id(0)
        hm0 = lax.broadcasted_iota(jnp.int32, (1, 128), 1) < 64
        r2 = lax.broadcasted_iota(jnp.int32, (2 * T, T), 0)
        c2 = lax.broadcasted_iota(jnp.int32, (2 * T, T), 1)
        causal = c2 < jnp.where(r2 >= T, r2 - T, r2)
        ur = lax.broadcasted_iota(jnp.int32, (T, T), 0)
        uc = lax.broadcasted_iota(jnp.int32, (T, T), 1)
        u_incl = (ur >= uc).astype(BF16)
        acc_ref[...] = jnp.zeros_like(acc_ref)
        c_ref[...] = jnp.zeros_like(c_ref)
        for pp in range(4):
            qh_ref[pp] = _stack_heads(q_ref[:, pp * 128:(pp + 1) * 128] * QK_SCALE, hm0)

        def block(kb, diag):
            ks = pl.multiple_of(kb * T, T)
            for pp in range(4):
                cs = slice(pp * 128, (pp + 1) * 128)
                z = _dot_nt(qh_ref[pp], k_ref[pl.ds(ks, T), cs])
                sp = _softplus(z)
                if diag:
                    sp = jnp.where(causal, sp, 0.0)
                incl = _cumsum_mm(sp, u_incl)
                c = c_ref[pp]
                w = jnp.exp(z - incl - c)
                if diag:
                    w = jnp.where(causal, w, 0.0)
                acc_ref[:, cs] += _dot(_unstack_k(w.astype(BF16), T), _stack_heads(v_ref[pl.ds(ks, T), cs], hm0))
                c_ref[pp] = c + jnp.broadcast_to(incl[:, 0:1], (2 * T, T))

        block(i, True)

        def step(jj, carry):
            block(i - 1 - jj, False)
            return carry

        lax.fori_loop(0, i, step, 0)
        o_ref[...] = acc_ref[...].astype(BF16)
        for pp in range(4):
            for hd in range(2):
                t_ref[2 * pp + hd] = c_ref[pp, hd * T:(hd + 1) * T, 0:128]

    return pl.pallas_call(
        body, name="attn_fwd", grid=(nq,),
        in_specs=[pl.BlockSpec((T, 512), lambda i: (i, 0)),
                  pl.BlockSpec((S, 512), lambda i: (0, 1)),
                  pl.BlockSpec((S, 512), lambda i: (0, 2))],
        out_specs=[pl.BlockSpec((T, 512), lambda i: (i, 0)),
                   pl.BlockSpec((8, T, 128), lambda i: (0, i, 0))],
        out_shape=[_sds((S, 512), BF16), _sds((8, S, 128))],
        scratch_shapes=[pltpu.VMEM((T, 512), F32), pltpu.VMEM((4, 2 * T, T), F32), pltpu.VMEM((4, 2 * T, 128), BF16)],
        compiler_params=_cp(("arbitrary",)),
    )(qkv, qkv, qkv)


def pool_fwd(h, pool_w, pool_scale, CH=256):
    S = h.shape[0]
    CH = min(CH, S)

    def body(u_ref, w_ref, sc_ref, b_ref, pooled_ref, pad_ref):
        pad_ref[0:16, :] = jnp.zeros((16, 512), F32)
        pad_ref[16:16 + S, :] = u_ref[...]
        for g, win in enumerate(POOL_WINDOWS):
            cs = slice(g * 128, (g + 1) * 128)
            wq = w_ref[g].astype(BF16)
            for ch in range(S // CH):
                base = ch * CH
                acc = pad_ref[16 + base:16 + base + CH, cs]
                for sft in range(1, win):
                    acc = acc + pad_ref[16 + base - sft:16 + base - sft + CH, cs]
                t = base + lax.broadcasted_iota(jnp.int32, (CH, 1), 0)
                cnt = jnp.minimum(t + 1, win).astype(F32)
                pooled = (acc / cnt - pad_ref[16 + base:16 + base + CH, cs]).astype(BF16)
                pooled_ref[base:base + CH, cs] = pooled
                b_ref[base:base + CH, cs] = (_dot(pooled, wq) * sc_ref[:, cs]).astype(BF16)

    return pl.pallas_call(
        body, name="pool_fwd", grid=(1,),
        in_specs=[pl.BlockSpec((S, 512), lambda i: (0, 3)),
                  pl.BlockSpec((4, 128, 128), lambda i: (0, 0, 0)),
                  pl.BlockSpec((1, 512), lambda i: (0, 0))],
        out_specs=[pl.BlockSpec((S, 512), lambda i: (0, 0)), pl.BlockSpec((S, 512), lambda i: (0, 0))],
        out_shape=[_sds((S, 512), BF16), _sds((S, 512), BF16)],
        scratch_shapes=[pltpu.VMEM((S + 16, 512), F32)],
        compiler_params=_cp(("arbitrary",)),
    )(h, pool_w, pool_scale)


def conv_fwd(h, dw, CH=128):
    S = h.shape[0]

    def body(a_ref, g_ref, dw_ref, y_ref, hc_ref, pad_ref):
        hc = a_ref[...] * _sigmoid(g_ref[...])
        hc_ref[...] = hc
        pad_ref[0:32, :] = jnp.zeros((32, 128), F32)
        pad_ref[32:32 + S, :] = hc
        for ch in range(S // CH):
            base = ch * CH + 2
            acc = dw_ref[0:1, :] * pad_ref[base:base + CH, :]
            for k in range(1, CONV_TAPS):
                acc = acc + dw_ref[k:k + 1, :] * pad_ref[base + k:base + k + CH, :]
            y_ref[ch * CH:(ch + 1) * CH, :] = acc

    return pl.pallas_call(
        body, name="conv_fwd", grid=(4,),
        in_specs=[pl.BlockSpec((S, 128), lambda c: (0, c)),
                  pl.BlockSpec((S, 128), lambda c: (0, 4 + c)),
                  pl.BlockSpec((CONV_TAPS, 128), lambda c: (0, c))],
        out_specs=[pl.BlockSpec((S, 128), lambda c: (0, c)), pl.BlockSpec((S, 128), lambda c: (0, c))],
        out_shape=[_sds((S, 512)), _sds((S, 512))],
        scratch_shapes=[pltpu.VMEM((S + 32, 128), F32)],
        compiler_params=_cp(("arbitrary",)),
    )(h, h, dw)


def _masked_sg_w(w_ref, g):
    row = lax.broadcasted_iota(jnp.int32, (128, 128), 0)
    col = lax.broadcasted_iota(jnp.int32, (128, 128), 1)
    return jnp.where(row >= col, w_ref[g], 0.0).astype(BF16)


def odd_post(y, h, cl_g, cl_b, sl_g, sl_b, sg_w, sgb_bc, tm=256):
    S = y.shape[0]
    tm = min(tm, S)

    def body(y_ref, zc_ref, clg, clb, slg, slb, w_ref, sb_ref,
             c_ref, d_ref, xhc_ref, rsc_ref, xhv_ref, rsv_ref, sv_ref):
        lnc, xhc, rsc = _ln_fwd(y_ref[...], clg[...], clb[...])
        c_ref[...] = (lnc * _sigmoid(lnc)).astype(BF16)
        xhc_ref[...] = xhc
        rsc_ref[...] = rsc
        z = _gelu(zc_ref[...])
        vn, xhv, rsv = _ln_fwd(z[:, 512:], slg[...], slb[...])
        xhv_ref[...] = xhv
        rsv_ref[...] = rsv
        vnb = vn.astype(BF16)
        for g in range(4):
            wm = _masked_sg_w(w_ref, g)
            for ch in range(tm // 128):
                rs, cs = slice(ch * 128, (ch + 1) * 128), slice(g * 128, (g + 1) * 128)
                sv_ref[rs, cs] = _dot(wm, vnb[rs, cs]) + sb_ref[g]
        d_ref[...] = (z[:, :512] * sv_ref[...]).astype(BF16)

    return _tok_call(
        "odd_post", body, [y, (h, 1024, 1)], [cl_g, cl_b, sl_g, sl_b, sg_w, sgb_bc],
        [_sds((S, 512), BF16), _sds((S, 512), BF16), _sds((S, 512)), _sds((S, 1)),
         _sds((S, 512)), _sds((S, 1)), _sds((S, 512))], tm=tm)


def mm_out_ln(l1, l2, x, w, g, b, dep=None):
    S, D = x.shape

    def body(l1_ref, l2_ref, x_ref, w_ref, g_ref, b_ref, y_ref, xh_ref, rs_ref):
        mix = _dot(l1_ref[...], w_ref[0:512, :]) + _dot(l2_ref[...], w_ref[512:1024, :])
        y, xh, rs = _ln_fwd(ALPHA * x_ref[...] + mix, g_ref[...], b_ref[...])
        y_ref[...] = y
        xh_ref[...] = xh
        rs_ref[...] = rs

    return _tok_call("mm_out_ln", body, [l1, l2, x], [w, g, b],
                     [_sds((S, D)), _sds((S, D)), _sds((S, 1))], dep=dep)


def ffn_up(x1, wg, wu, layer):
    S, D = x1.shape
    F = wg.shape[-1]

    def body(x_ref, wg_ref, wu_ref, gate_ref, up_ref, hb_ref, xb_ref):
        xb = x_ref[...].astype(BF16)
        xb_ref[...] = xb
        gate = _dot(xb, wg_ref[...])
        up = _dot(xb, wu_ref[...])
        gate_ref[...] = gate.astype(BF16)
        up_ref[...] = up.astype(BF16)
        hb_ref[...] = (gate * _sigmoid(gate) * up).astype(BF16)

    return _tok_call("ffn_up", body, [x1], [(wg, layer), (wu, layer)],
                     [_sds((S, F), BF16), _sds((S, F), BF16), _sds((S, F), BF16), _sds((S, D), BF16)])


def ffn_down_ln(hb, x1, wd, layer, g, b):
    S, D = x1.shape

    def body(h_ref, x_ref, w_ref, g_ref, b_ref, y_ref, xh_ref, rs_ref):
        f = _dot(h_ref[...], w_ref[...])
        y, xh, rs = _ln_fwd(ALPHA * x_ref[...] + f, g_ref[...], b_ref[...])
        y_ref[...] = y
        xh_ref[...] = xh
        rs_ref[...] = rs

    return _tok_call("ffn_down_ln", body, [hb, x1], [(wd, layer), g, b],
                     [_sds((S, D)), _sds((S, D)), _sds((S, 1))])


def ple_fwd(x2, p, wpg, wpp, layer, bg, target=None, dep=None):
    S, D = x2.shape
    last = target is not None

    def body(*refs):
        if last:
            x_ref, p_ref, t_ref, wg_ref, wp_ref, b_ref, x3_ref, sg_ref, pp_ref, xb_ref, pb_ref, dy_ref, ls_ref = refs
        else:
            x_ref, p_ref, wg_ref, wp_ref, b_ref, x3_ref, sg_ref, pp_ref, xb_ref, pb_ref = refs
        x = x_ref[...]
        xb = x.astype(BF16)
        pb = p_ref[...].astype(BF16)
        xb_ref[...] = xb
        pb_ref[...] = pb
        sg = _sigmoid(_dot(xb, wg_ref[...]) + b_ref[...])
        pp = _dot(pb, wp_ref[...])
        sg_ref[...] = sg.astype(BF16)
        pp_ref[...] = pp.astype(BF16)
        x3 = x + sg * pp
        x3_ref[...] = x3
        if last:
            err = x3 - t_ref[...]
            dy_ref[...] = err * (1.0 / D)
            _acc(ls_ref, _colsum(err * err))

    outs = [_sds((S, D)), _sds((S, D), BF16), _sds((S, D), BF16), _sds((S, D), BF16), _sds((S, p.shape[1]), BF16)]
    tiled = [x2, p] + ([target] if last else [])
    if last:
        outs.append(_sds((S, D)))
    return _tok_call("ple_fwd", body, tiled, [(wpg, layer), (wpp, layer), bg], outs,
                     [_sds((1, D))] if last else [], dep=dep)


def ple_bwd(dx3, sg, pp, wpg, layer, dep=None):
    S, D = dx3.shape

    def body(d_ref, sg_ref, pp_ref, w_ref, dx_ref, dgp_ref, dpp_ref, dbg_ref):
        d, sg = d_ref[...], sg_ref[...].astype(F32)
        dgp = d * pp_ref[...].astype(F32) * sg * (1.0 - sg)
        dgpb = dgp.astype(BF16)
        dgp_ref[...] = dgpb
        dpp_ref[...] = (d * sg).astype(BF16)
        dx_ref[...] = d + _dot_nt(dgpb, w_ref[...])
        _acc(dbg_ref, _colsum(dgp))

    return _tok_call("ple_bwd", body, [dx3, sg, pp], [(wpg, layer)],
                     [_sds((S, D)), _sds((S, D), BF16), _sds((S, D), BF16)], [_sds((1, D))], dep=dep)


def ffn_bwd_a(dx2, xh, rs, g, gate, up, wd, layer):
    S, D = dx2.shape
    F = gate.shape[1]

    def body(d_ref, xh_ref, rs_ref, gate_ref, up_ref, g_ref, w_ref,
             dr_ref, drb_ref, dg_ref, du_ref, dlg_ref, dlb_ref):
        d, xh = d_ref[...], xh_ref[...]
        dr = _ln_bwd(d, xh, rs_ref[...], g_ref[...])
        drb = dr.astype(BF16)
        dr_ref[...] = dr
        drb_ref[...] = drb
        _acc(dlg_ref, _colsum(d * xh))
        _acc(dlb_ref, _colsum(d))
        dh = _dot_nt(drb, w_ref[...])
        gate, up = gate_ref[...].astype(F32), up_ref[...].astype(F32)
        s = _sigmoid(gate)
        dg_ref[...] = (dh * up * s * (1.0 + gate * (1.0 - s))).astype(BF16)
        du_ref[...] = (dh * gate * s).astype(BF16)

    return _tok_call("ffn_bwd_a", body, [dx2, xh, rs, gate, up], [g, (wd, layer)],
                     [_sds((S, D)), _sds((S, D), BF16), _sds((S, F), BF16), _sds((S, F), BF16)],
                     [_sds((1, D)), _sds((1, D))])


def ffn_bwd_b(dr, dgate_b, dup_b, wg, wu, layer, dep=None):
    S, D = dr.shape

    def body(dr_ref, dg_ref, du_ref, wg_ref, wu_ref, dx_ref):
        dx_ref[...] = (ALPHA * dr_ref[...] + _dot_nt(dg_ref[...], wg_ref[...])
                       + _dot_nt(du_ref[...], wu_ref[...]))

    return _tok_call("ffn_bwd_b", body, [dr, dgate_b, dup_b], [(wg, layer), (wu, layer)], [_sds((S, D))], dep=dep)[0]


def mix_bwd(dx1, xh, rs, g, w):
    S, D = dx1.shape

    def body(d_ref, xh_ref, rs_ref, g_ref, w_ref, dr_ref, dmb_ref, dl_ref, dlg_ref, dlb_ref):
        d, xh = d_ref[...], xh_ref[...]
        dr = _ln_bwd(d, xh, rs_ref[...], g_ref[...])
        drb = dr.astype(BF16)
        dr_ref[...] = dr
        dmb_ref[...] = drb
        dl_ref[...] = _dot_nt(drb, w_ref[...])
        _acc(dlg_ref, _colsum(d * xh))
        _acc(dlb_ref, _colsum(d))

    return _tok_call("mix_bwd", body, [dx1, xh, rs], [g, w],
                     [_sds((S, D)), _sds((S, D), BF16), _sds((S, D))], [_sds((1, D)), _sds((1, D))])


def dx_in(dr, pieces, w):
    S, D = dr.shape
    offs = [o for _, o in pieces]
    widths = [a.shape[1] for a, _ in pieces]

    def body(*refs):
        dr_ref, prefs, w_ref, dx_ref = refs[0], refs[1:1 + len(pieces)], refs[-2], refs[-1]
        acc = ALPHA * dr_ref[...]
        for pr, o, n in zip(prefs, offs, widths):
            acc = acc + _dot_nt(pr[...], w_ref[:, o:o + n])
        dx_ref[...] = acc

    return _tok_call("dx_in", body, [dr] + [a for a, _ in pieces], [w], [_sds((S, D))])[0]


def odd_post_bwd(dl, h, xhc, rsc, xhv, rsv, sv, cl_g, cl_b, sl_g, sl_b, sg_w, tm=256, dep=None):
    S = dl.shape[0]
    tm = min(tm, S)

    def body(dl_ref, zc_ref, xhc_ref, rsc_ref, xhv_ref, rsv_ref, sv_ref, clg, clb, slg, slb, w_ref,
             dy_ref, dzc_ref, dclg_ref, dclb_ref, dslg_ref, dslb_ref, dwm_ref, dsb_ref, dvn_ref):
        first = pl.program_id(0) == 0
        last = pl.program_id(0) == pl.num_programs(0) - 1
        dc, dd = dl_ref[:, 0:512], dl_ref[:, 512:1024]
        xhc = xhc_ref[...]
        lnc = xhc * clg[...] + clb[...]
        s = _sigmoid(lnc)
        dlnc = dc * s * (1.0 + lnc * (1.0 - s))
        dy_ref[...] = _ln_bwd(dlnc, xhc, rsc_ref[...], clg[...])
        _acc(dclg_ref, _colsum(dlnc * xhc))
        _acc(dclb_ref, _colsum(dlnc))
        zc = zc_ref[...]
        z = _gelu(zc)
        dsv = dd * z[:, :512]
        dsvb = dsv.astype(BF16)
        xhv = xhv_ref[...]
        vnb = (xhv * slg[...] + slb[...]).astype(BF16)

        @pl.when(first)
        def _():
            dwm_ref[...] = jnp.zeros_like(dwm_ref)
            dsb_ref[...] = jnp.zeros_like(dsb_ref)

        for g in range(4):
            wm = _masked_sg_w(w_ref, g)
            for ch in range(tm // 128):
                rs_, cs = slice(ch * 128, (ch + 1) * 128), slice(g * 128, (g + 1) * 128)
                dwm_ref[g] += _dot_nt(dsvb[rs_, cs], vnb[rs_, cs])
                dvn_ref[rs_, cs] = _dot_tn(wm, dsvb[rs_, cs])
                dsb_ref[g] += dsv[rs_, cs]
        dvn = dvn_ref[...]
        dvv = _ln_bwd(dvn, xhv, rsv_ref[...], slg[...])
        _acc(dslg_ref, _colsum(dvn * xhv))
        _acc(dslb_ref, _colsum(dvn))
        gg = _gelu_grad(zc)
        dzc_ref[:, 0:512] = (dd * sv_ref[...] * gg[:, :512]).astype(BF16)
        dzc_ref[:, 512:1024] = (dvv * gg[:, 512:]).astype(BF16)

        @pl.when(last)
        def _():
            row = lax.broadcasted_iota(jnp.int32, (128, 128), 0)
            col = lax.broadcasted_iota(jnp.int32, (128, 128), 1)
            for g in range(4):
                dwm_ref[g] = jnp.where(row >= col, dwm_ref[g], 0.0)
                dsb_ref[g] = jnp.broadcast_to(jnp.sum(dsb_ref[g], axis=1, keepdims=True), (128, 128))

    return _tok_call(
        "odd_post_bwd", body, [dl, (h, 1024, 1), xhc, rsc, xhv, rsv, sv], [cl_g, cl_b, sl_g, sl_b, sg_w],
        [_sds((S, 512)), _sds((S, 1024), BF16)],
        [_sds((1, 512)), _sds((1, 512)), _sds((1, 512)), _sds((1, 512)), _sds((4, 128, 128)), _sds((4, 128, 128))],
        tm=tm, scratch=[pltpu.VMEM((tm, 512), F32)], dep=dep)


def conv_bwd(dy, hc, h, dw, CH=128):
    S = dy.shape[0]

    def body(dy_ref, hc_ref, a_ref, g_ref, dw_ref, da_ref, dg_ref, ddw_ref, padh_ref, padd_ref, dhc_ref):
        padh_ref[0:32, :] = jnp.zeros((32, 128), F32)
        padh_ref[32:32 + S, :] = hc_ref[...]
        padd_ref[0:S, :] = dy_ref[...]
        padd_ref[S:S + 32, :] = jnp.zeros((32, 128), F32)
        taps = [jnp.zeros((1, 128), F32) for _ in range(CONV_TAPS)]
        for ch in range(S // CH):
            b0 = ch * CH
            dyc = padd_ref[b0:b0 + CH, :]
            acc = dw_ref[0:1, :] * padd_ref[b0 + 30:b0 + 30 + CH, :]
            taps[0] = taps[0] + _colsum(dyc * padh_ref[b0 + 2:b0 + 2 + CH, :])
            for k in range(1, CONV_TAPS):
                acc = acc + dw_ref[k:k + 1, :] * padd_ref[b0 + 30 - k:b0 + 30 - k + CH, :]
                taps[k] = taps[k] + _colsum(dyc * padh_ref[b0 + 2 + k:b0 + 2 + k + CH, :])
            dhc_ref[b0:b0 + CH, :] = acc
        for k in range(CONV_TAPS):
            ddw_ref[k:k + 1, :] = taps[k]
        dhc = dhc_ref[...]
        s = _sigmoid(g_ref[...])
        da_ref[...] = (dhc * s).astype(BF16)
        dg_ref[...] = (dhc * a_ref[...] * s * (1.0 - s)).astype(BF16)

    return pl.pallas_call(
        body, name="conv_bwd", grid=(4,),
        in_specs=[pl.BlockSpec((S, 128), lambda c: (0, c)),
                  pl.BlockSpec((S, 128), lambda c: (0, c)),
                  pl.BlockSpec((S, 128), lambda c: (0, c)),
                  pl.BlockSpec((S, 128), lambda c: (0, 4 + c)),
                  pl.BlockSpec((CONV_TAPS, 128), lambda c: (0, c))],
        out_specs=[pl.BlockSpec((S, 128), lambda c: (0, c)), pl.BlockSpec((S, 128), lambda c: (0, c)),
                   pl.BlockSpec((CONV_TAPS, 128), lambda c: (0, c))],
        out_shape=[_sds((S, 512), BF16), _sds((S, 512), BF16), _sds((CONV_TAPS, 512))],
        scratch_shapes=[pltpu.VMEM((S + 32, 128), F32), pltpu.VMEM((S + 32, 128), F32), pltpu.VMEM((S, 128), F32)],
        compiler_params=_cp(("arbitrary",)),
    )(dy, hc, h, h, dw)


def attn_bwd(qkv, dl, tb, T=256, dep=None):
    S = qkv.shape[0]
    T = min(T, S)
    nq = S // T

    def body(q_ref, k_ref, v_ref, do_ref, t_ref, dq_ref, dk_ref, dv_ref,
             dka_ref, dva_ref, dqa_ref, pc_ref, gc_ref, qh_ref, doh_ref):
        i = pl.program_id(0)
        hm0 = lax.broadcasted_iota(jnp.int32, (1, 128), 1) < 64
        r2 = lax.broadcasted_iota(jnp.int32, (2 * T, T), 0)
        c2 = lax.broadcasted_iota(jnp.int32, (2 * T, T), 1)
        causal = c2 < jnp.where(r2 >= T, r2 - T, r2)
        ur = lax.broadcasted_iota(jnp.int32, (T, T), 0)
        uc = lax.broadcasted_iota(jnp.int32, (T, T), 1)
        u_le = (ur <= uc).astype(BF16)
        u_lt = (ur < uc).astype(BF16)

        @pl.when(i == 0)
        def _():
            dka_ref[...] = jnp.zeros_like(dka_ref)
            dva_ref[...] = jnp.zeros_like(dva_ref)

        dqa_ref[...] = jnp.zeros_like(dqa_ref)
        gc_ref[...] = jnp.zeros_like(gc_ref)
        for pp in range(4):
            cs = slice(pp * 128, (pp + 1) * 128)
            qh_ref[pp] = _stack_heads(q_ref[:, cs] * QK_SCALE, hm0)
            doh_ref[pp] = _stack_heads(do_ref[:, cs], hm0)
            for hd in range(2):
                for half in range(T // 128):
                    pc_ref[pp, hd * T:(hd + 1) * T, half * 128:(half + 1) * 128] = t_ref[2 * pp + hd]

        def block(kb, diag):
            ks = pl.multiple_of(kb * T, T)
            for pp in range(4):
                cs = slice(pp * 128, (pp + 1) * 128)
                kb16 = k_ref[pl.ds(ks, T), cs]
                vb16 = v_ref[pl.ds(ks, T), cs]
                qh, doh = qh_ref[pp], doh_ref[pp]
                z = _dot_nt(qh, kb16)
                sp = _softplus(z)
                a = z - sp
                sig = jnp.exp(a)
                if diag:
                    sp = jnp.where(causal, sp, 0.0)
                pre = _cumsum_mm(sp, u_le)
                rem = pc_ref[pp]
                w = jnp.exp(a - rem + pre)
                if diag:
                    w = jnp.where(causal, w, 0.0)
                gmat = _dot_nt(doh, vb16) * w
                gex = gc_ref[pp] + _cumsum_mm(gmat, u_lt)
                dz = gmat * (1.0 - sig) - sig * gex
                if diag:
                    dz = jnp.where(causal, dz, 0.0)
                dzb = dz.astype(BF16)
                dqa_ref[:, cs] += _dot(_unstack_k(dzb, T), _stack_heads(kb16, hm0))
                dka_ref[pl.ds(ks, T), cs] += _dot_tn(dzb, qh)
                dva_ref[pl.ds(ks, T), cs] += _dot_tn(w.astype(BF16), doh)
                pc_ref[pp] = rem - jnp.broadcast_to(pre[:, T - 1:T], (2 * T, T))
                gc_ref[pp] = jnp.broadcast_to(gex[:, T - 1:T] + gmat[:, T - 1:T], (2 * T, T))

        def step(kb, carry):
            block(kb, False)
            return carry

        lax.fori_loop(0, i, step, 0)
        block(i, True)
        dq_ref[...] = (dqa_ref[...] * QK_SCALE).astype(BF16)

        @pl.when(i == nq - 1)
        def _():
            dk_ref[...] = dka_ref[...].astype(BF16)
            dv_ref[...] = dva_ref[...].astype(BF16)

    deps = [] if dep is None else [dep]
    call_body = body if dep is None else (lambda *refs: body(*refs[:5], *refs[6:]))
    return pl.pallas_call(
        call_body, name="attn_bwd", grid=(nq,),
        in_specs=[pl.BlockSpec((T, 512), lambda i: (i, 0)),
                  pl.BlockSpec((S, 512), lambda i: (0, 1)),
                  pl.BlockSpec((S, 512), lambda i: (0, 2)),
                  pl.BlockSpec((T, 512), lambda i: (i, 0)),
                  pl.BlockSpec((8, T, 128), lambda i: (0, i, 0))] + [ANY] * len(deps),
        out_specs=[pl.BlockSpec((T, 512), lambda i: (i, 0)),
                   pl.BlockSpec((S, 512), lambda i: (0, 0)),
                   pl.BlockSpec((S, 512), lambda i: (0, 0))],
        out_shape=[_sds((S, 512), BF16), _sds((S, 512), BF16), _sds((S, 512), BF16)],
        scratch_shapes=[pltpu.VMEM((S, 512), F32), pltpu.VMEM((S, 512), F32), pltpu.VMEM((T, 512), F32),
                        pltpu.VMEM((4, 2 * T, T), F32), pltpu.VMEM((4, 2 * T, T), F32),
                        pltpu.VMEM((4, 2 * T, 128), BF16), pltpu.VMEM((4, 2 * T, 128), BF16)],
        compiler_params=_cp(("arbitrary",)),
    )(qkv, qkv, qkv, dl, tb, *deps)


def pool_bwd(dl, pooled_b, pool_w, pool_scale, CH=256):
    S = dl.shape[0]
    CH = min(CH, S)

    def body(db_ref, pooled_ref, w_ref, sc_ref, du_ref, dw_ref, dsc_ref, pad_ref, dp_ref):
        pad_ref[S:S + 16, :] = jnp.zeros((16, 128), F32)
        for g, win in enumerate(POOL_WINDOWS):
            cs = slice(g * 128, (g + 1) * 128)
            wq = w_ref[g].astype(BF16)
            dwg = jnp.zeros((128, 128), F32)
            dsc = jnp.zeros((1, 128), F32)
            for ch in range(S // CH):
                rs_ = slice(ch * CH, (ch + 1) * CH)
                db = db_ref[rs_, cs]
                pb = pooled_ref[rs_, cs]
                dsc = dsc + _colsum(db * _dot(pb, wq))
                dmsb = (db * sc_ref[:, cs]).astype(BF16)
                dwg = dwg + _dot_tn(pb, dmsb)
                dpool = _dot_nt(dmsb, wq)
                t = ch * CH + lax.broadcasted_iota(jnp.int32, (CH, 1), 0)
                cnt = jnp.minimum(t + 1, win).astype(F32)
                dp_ref[rs_, :] = dpool
                pad_ref[rs_, :] = dpool / cnt
            dw_ref[g] = dwg
            dsc_ref[:, cs] = dsc
            for ch in range(S // CH):
                base = ch * CH
                acc = pad_ref[base:base + CH, :]
                for sft in range(1, win):
                    acc = acc + pad_ref[base + sft:base + sft + CH, :]
                du_ref[base:base + CH, cs] = (acc - dp_ref[base:base + CH, :]).astype(BF16)

    return pl.pallas_call(
        body, name="pool_bwd", grid=(1,),
        in_specs=[pl.BlockSpec((S, 512), lambda i: (0, 1)),
                  pl.BlockSpec((S, 512), lambda i: (0, 0)),
                  pl.BlockSpec((4, 128, 128), lambda i: (0, 0, 0)),
                  pl.BlockSpec((1, 512), lambda i: (0, 0))],
        out_specs=[pl.BlockSpec((S, 512), lambda i: (0, 0)),
                   pl.BlockSpec((4, 128, 128), lambda i: (0, 0, 0)),
                   pl.BlockSpec((1, 512), lambda i: (0, 0))],
        out_shape=[_sds((S, 512), BF16), _sds((4, 128, 128)), _sds((1, 512))],
        scratch_shapes=[pltpu.VMEM((S + 16, 128), F32), pltpu.VMEM((S, 128), F32)],
        compiler_params=_cp(("arbitrary",)),
    )(dl, pooled_b, pool_w, pool_scale)


def tn_into(a, b, out, out_b, r0, c0, tk=1024, tn=512):
    S, K = a.shape
    N = b.shape[1]
    tk, tn = min(tk, K), min(tn, N)
    assert K % tk == 0 and N % tn == 0 and r0 % tk == 0 and c0 % tn == 0
    rb, cb = r0 // tk, c0 // tn
    fresh = isinstance(out, jax.ShapeDtypeStruct)

    def body(*refs):
        a_ref, b_ref, o_ref, ob_ref = refs[0], refs[1], refs[-2], refs[-1]
        r = _dot_tn(a_ref[...], b_ref[...])
        o_ref[...] = r
        ob_ref[...] = r.astype(BF16)

    ospec = pl.BlockSpec((tk, tn), lambda i, j: (rb + i, cb + j))
    in_specs = [pl.BlockSpec((S, tk), lambda i, j: (0, i)), pl.BlockSpec((S, tn), lambda i, j: (0, j))]
    args = [a, b]
    aliases = {}
    if not fresh:
        in_specs += [pl.BlockSpec(memory_space=pl.ANY), pl.BlockSpec(memory_space=pl.ANY)]
        args += [out, out_b]
        aliases = {2: 0, 3: 1}
    shp = out.shape
    return pl.pallas_call(
        body, name="tn_grad", grid=(K // tk, N // tn),
        in_specs=in_specs, out_specs=[ospec, ospec],
        out_shape=[_sds(shp, F32), _sds(shp, BF16)],
        input_output_aliases=aliases,
        compiler_params=_cp(("arbitrary", "arbitrary")),
    )(*args)


def _row(a, i):
    return a[i:i + 1]


MIXER_NAMES = (("even_w_in", "even_w_out"), ("odd_w_in", "odd_w_out"))


def _tn_group(items):
    out = {}
    for name, (shape, parts) in items.items():
        g, gb = _sds(shape, F32), _sds(shape, BF16)
        for a, b, r0, c0 in parts:
            g, gb = tn_into(a, b, g, gb, r0, c0)
        out[name] = (g, gb)
    return out


def fwd_layer(i, xin, p_i, target, comm):
    s = {}
    W = comm.weights(("mix", i), xin)
    w_in = W[MIXER_NAMES[i][0]]
    if i == 0:
        s["h"], s["xb"], s["qkv"] = mm_in(xin, w_in, nb16=1536)
        comm.poke(("in", i), s["h"])
        s["l1"], s["tb"] = attn_fwd(s["qkv"])
        s["l2"], s["pooled"] = pool_fwd(s["h"], W["pool_w"], W["pool_scale"])
    else:
        s["h"], s["xb"] = mm_in(xin, w_in)
        comm.poke(("in", i), s["h"])
        s["y"], s["hc"] = conv_fwd(s["h"], W["conv_dw"])
        sgb_bc = jnp.broadcast_to(W["sg_b"][:, :, None], (4, 128, 128))
        (s["l1"], s["l2"], s["xhc"], s["rsc"], s["xhv"], s["rsv"], s["sv"]) = odd_post(
            s["y"], s["h"], W["conv_ln_g"], W["conv_ln_b"], W["sg_ln_g"], W["sg_ln_b"], W["sg_w"], sgb_bc)
    tok = comm.poke(("mixed", i), s["l2"])
    W = comm.weights(("out", i), s["l2"])
    x1, s["xh1"], s["rs1"] = mm_out_ln(s["l1"], s["l2"], xin, W[MIXER_NAMES[i][1]], _row(W["ln_mix_g"], i),
                                       _row(W["ln_mix_b"], i), dep=tok)
    W = comm.weights(("ffn", i), x1)
    s["gate"], s["up"], s["hb"], s["x1b"] = ffn_up(x1, W["ffn_w_gate%d" % i], W["ffn_w_up%d" % i], None)
    x2, s["xh2"], s["rs2"] = ffn_down_ln(s["hb"], x1, W["ffn_w_down%d" % i], None,
                                         _row(W["ln_ffn_g"], i), _row(W["ln_ffn_b"], i))
    tok = comm.poke(("ffn", i), x2)
    outs = ple_fwd(x2, p_i, W["ple_w_gate%d" % i], W["ple_w_proj%d" % i], None, _row(W["ple_b_gate"], i), target,
                   dep=tok)
    s["sg"], s["pp"], s["x2b"], s["pb"] = outs[1:5]
    return outs[0], s, outs[5:]


def bwd_layer(i, dx, s, W, comm, tok=None):
    small = {}
    D = dx.shape[1]
    FP = W["ffn_w_gate%d" % i].shape[1]
    dx2, dgp_b, dpp_b, small["ple_b_gate"] = ple_bwd(dx, s["sg"], s["pp"], W["ple_w_gate%d" % i], None, dep=tok)
    dr2, dr2_b, dgate_b, dup_b, small["ln_ffn_g"], small["ln_ffn_b"] = ffn_bwd_a(
        dx2, s["xh2"], s["rs2"], _row(W["ln_ffn_g"], i), s["gate"], s["up"], W["ffn_w_down%d" % i], None)
    tok = comm.grads(_tn_group({
        "ple_w_gate%d" % i: ((D, D), [(s["x2b"], dgp_b, 0, 0)]),
        "ple_w_proj%d" % i: ((s["pb"].shape[1], D), [(s["pb"], dpp_b, 0, 0)]),
        "ffn_w_down%d" % i: ((FP, D), [(s["hb"], dr2_b, 0, 0)]),
        "ffn_w_gate%d" % i: ((D, FP), [(s["x1b"], dgate_b, 0, 0)]),
        "ffn_w_up%d" % i: ((D, FP), [(s["x1b"], dup_b, 0, 0)])}))
    dx1 = ffn_bwd_b(dr2, dgate_b, dup_b, W["ffn_w_gate%d" % i], W["ffn_w_up%d" % i], None, dep=tok)
    iname, oname = MIXER_NAMES[i]
    dr1, dmix_b, dl, small["ln_mix_g"], small["ln_mix_b"] = mix_bwd(
        dx1, s["xh1"], s["rs1"], _row(W["ln_mix_g"], i), W[oname])
    tok = comm.poke(("bwd", i), dl)
    if i == 1:
        (dy, dzc_b, small["conv_ln_g"], small["conv_ln_b"], small["sg_ln_g"], small["sg_ln_b"],
         small["sg_w"], dsb) = odd_post_bwd(dl, s["h"], s["xhc"], s["rsc"], s["xhv"], s["rsv"], s["sv"],
                                            W["conv_ln_g"], W["conv_ln_b"], W["sg_ln_g"], W["sg_ln_b"], W["sg_w"],
                                            dep=tok)
        small["sg_b"] = dsb[:, :, 0]
        da_b, dg_b, small["conv_dw"] = conv_bwd(dy, s["hc"], s["h"], W["conv_dw"])
        pieces = [(da_b, 0), (dg_b, 512), (dzc_b, 1024)]
    else:
        dq_b, dk_b, dv_b = attn_bwd(s["qkv"], dl, s["tb"], dep=tok)
        du_b, small["pool_w"], small["pool_scale"] = pool_bwd(dl, s["pooled"], W["pool_w"], W["pool_scale"])
        pieces = [(dq_b, 0), (dk_b, 512), (dv_b, 1024), (du_b, 1536)]
    dxin = dx_in(dr1, pieces, W[iname])
    tok = comm.grads(_tn_group({
        oname: ((1024, D), [(s["l1"], dmix_b, 0, 0), (s["l2"], dmix_b, 512, 0)]),
        iname: ((D, 2048), [(s["xb"], a, 0, off) for a, off in pieces])}))
    return dxin, small, tok


def run_layers(x, p, target, comm):
    saved, xin = [], x
    for i in range(2):
        xin, s, extra = fwd_layer(i, xin, p[i], target if i == 1 else None, comm)
        saved.append(s)
    dx, sq = extra
    W = comm.all_weights()
    per_layer = [None, None]
    tok = None
    for i in (1, 0):
        dx, per_layer[i], tok = bwd_layer(i, dx, saved[i], W, comm, tok)
    small = {}
    for k in ("ln_mix_g", "ln_mix_b", "ln_ffn_g", "ln_ffn_b", "ple_b_gate"):
        small[k] = jnp.concatenate([per_layer[0][k], per_layer[1][k]], axis=0)
    for i in range(2):
        small.update({k: v for k, v in per_layer[i].items() if k not in small})
    return sq, dx, small


def _big_table():
    t = {}
    for nm in ("even", "odd"):
        t[nm + "_w_in"] = ((1024, 2048), 1, 256, 256, nm + "_w_in", 0)
        t[nm + "_w_out"] = ((1024, 1024), 0, 128, 128, nm + "_w_out", 0)
    for l in range(2):
        t["ffn_w_gate%d" % l] = ((1024, 8 * FF_PAD), 1, FF_PAD, FF_SHARD, "ffn_w_gate", l)
        t["ffn_w_up%d" % l] = ((1024, 8 * FF_PAD), 1, FF_PAD, FF_SHARD, "ffn_w_up", l)
        t["ffn_w_down%d" % l] = ((8 * FF_PAD, 1024), 0, FF_PAD, FF_SHARD, "ffn_w_down", l)
        t["ple_w_gate%d" % l] = ((1024, 1024), 0, 128, 128, "ple_w_gate", l)
        t["ple_w_proj%d" % l] = ((256, 1024), 1, 128, 128, "ple_w_proj", l)
    return t


BIG = _big_table()
TRANSPOSED_ARGS = ("ffn_w_gate", "ffn_w_up")
SMALL_SPEC = ((N_DEV, 40, 64), 0, 1, 1)
_LAYER_GROUP = lambda l: ["ffn_w_gate%d" % l, "ffn_w_up%d" % l, "ffn_w_down%d" % l, "ple_w_gate%d" % l, "ple_w_proj%d" % l]
AG_GROUPS = (["even_w_in"], ["even_w_out"], _LAYER_GROUP(0), ["odd_w_in", "odd_w_out", "small"], _LAYER_GROUP(1))
AG_NEED = {("mix", 0): 0, ("out", 0): 1, ("ffn", 0): 2, ("mix", 1): 3, ("ffn", 1): 4}
AG_PASS = {("in", 0): 1, ("mixed", 0): 2, ("ffn", 0): 3, ("mixed", 1): 4}
ANY = pl.BlockSpec(memory_space=pl.ANY)
SEM = pl.BlockSpec(memory_space=pltpu.SEMAPHORE)


def _spec(name):
    return SMALL_SPEC if name == "small" else BIG[name]


def _win_shape(spec):
    full, axis, w = spec[:3]
    return tuple(w if d == axis else n for d, n in enumerate(full))


def _window(ref, axis, w, j):
    idx = [slice(None)] * len(ref.shape)
    idx[axis] = pl.ds(j, 1) if w == 1 else pl.ds(pl.multiple_of(j * w, w), w)
    return ref.at[tuple(idx)]


def _mesh_pos():
    return lax.axis_index("x"), lax.axis_index("y"), lax.axis_index("c")


def split_call(name, arrays, starts=(), waits=(), sems_in=(), new=(), after=None):
    n, nn, ns = len(arrays), len(new), len(starts)
    flat_sems = [s for pair in sems_in for s in pair]

    def body(*refs):
        arr = list(refs[:n])
        sin = refs[n:n + len(flat_sems)]
        outs = refs[n + len(flat_sems) + (after is not None):]
        data = arr + list(outs[n:n + nn])
        for p, k, kind, mk in waits:
            d = mk(data, sin[2 * p].at[k], sin[2 * p + 1].at[k])
            d.wait_send() if kind == "send" else d.wait_recv()
        if ns:
            send, recv = outs[n + nn], outs[n + nn + 1]
            for k, mk in enumerate(starts):
                mk(data, send.at[k], recv.at[k]).start()
        outs[-1][...] = jnp.zeros((8, 128), F32)

    sem_out = [pltpu.SemaphoreType.DMA((ns,)), pltpu.SemaphoreType.DMA((ns,))] if ns else []
    res = pl.pallas_call(
        body, name=name,
        in_specs=[ANY] * n + [SEM] * len(flat_sems) + ([ANY] if after is not None else []),
        out_specs=[ANY] * (n + nn) + [SEM] * len(sem_out) + [pl.BlockSpec(memory_space=pltpu.VMEM)],
        out_shape=[_sds(a.shape, a.dtype) for a in arrays] + list(new) + sem_out + [_sds((8, 128), F32)],
        input_output_aliases={a: a for a in range(n)},
        compiler_params=pltpu.CompilerParams(has_side_effects=pltpu.SideEffectType.DATAFLOW_SIDE_EFFECTING),
    )(*arrays, *flat_sems, *([after] if after is not None else []))
    return list(res[:n + nn]), (tuple(res[n + nn:n + nn + 2]) if ns else None), res[-1]


def _remote(src, dst, send_sem, recv_sem, dev):
    return pltpu.make_async_remote_copy(src_ref=src, dst_ref=dst, send_sem=send_sem, recv_sem=recv_sem,
                                        device_id=dev, device_id_type=MESH_T)


class Gatherer:
    def __init__(self, groups, arrays, specs, prefix):
        self.groups, self.specs, self.prefix = groups, specs, prefix
        self.names = [nm for g in groups for nm in g]
        self.arr = dict(zip(self.names, arrays))
        self.fwd_sems = {}
        self.forwarded = set()

    @staticmethod
    def _mk_first(ai, spec, k):
        def mk(refs, ss, rs):
            x, y, c = _mesh_pos()
            dev = [(x, y, 1 - c), (1 - x, y, c), (x, 1 - y, c), (1 - x, 1 - y, c)][k]
            win = _window(refs[ai], spec[1], spec[2], 4 * x + 2 * y + c)
            return _remote(win, win, ss, rs, dev)
        return mk

    @staticmethod
    def _mk_fwd(ai, spec, j):
        def mk(refs, ss, rs):
            x, y, c = _mesh_pos()
            px, py = [(1 - x, y), (x, 1 - y), (1 - x, 1 - y)][j]
            win = _window(refs[ai], spec[1], spec[2], 4 * px + 2 * py + c)
            return _remote(win, win, ss, rs, (x, y, 1 - c))
        return mk

    def start(self, after=None):
        starts = [self._mk_first(ai, self.specs[nm], k) for ai, nm in enumerate(self.names) for k in range(4)]
        arrs, self.first_sems, tok = split_call(self.prefix + "_start", [self.arr[nm] for nm in self.names],
                                                starts=starts, after=after)
        self.arr = dict(zip(self.names, arrs))
        return tok

    def forward(self, g, after=None):
        if g in self.forwarded:
            return None
        self.forwarded.add(g)
        names = self.groups[g]
        waits = [(0, 4 * self.names.index(nm) + 1 + j, "recv", self._mk_fwd(ai, self.specs[nm], j))
                 for ai, nm in enumerate(names) for j in range(3)]
        starts = [self._mk_fwd(ai, self.specs[nm], j) for ai, nm in enumerate(names) for j in range(3)]
        arrs, self.fwd_sems[g], tok = split_call(
            "%s_forward%d" % (self.prefix, g), [self.arr[nm] for nm in names], starts=starts, waits=waits,
            sems_in=[self.first_sems], after=after)
        self.arr.update(zip(names, arrs))
        return tok

    def finish(self, g, after=None):
        self.forward(g, after)
        names = self.groups[g]
        waits = []
        for ai, nm in enumerate(names):
            base = 4 * self.names.index(nm)
            waits.append((0, base, "recv", self._mk_first(ai, self.specs[nm], 0)))
            waits += [(1, 3 * ai + j, "recv", self._mk_fwd(ai, self.specs[nm], j)) for j in range(3)]
            waits += [(0, base + k, "send", self._mk_first(ai, self.specs[nm], k)) for k in range(4)]
            waits += [(1, 3 * ai + j, "send", self._mk_fwd(ai, self.specs[nm], j)) for j in range(3)]
        arrs, _, _ = split_call(
            "%s_finish%d" % (self.prefix, g), [self.arr[nm] for nm in names], waits=waits,
            sems_in=[self.first_sems, self.fwd_sems[g]], after=after)
        self.arr.update(zip(names, arrs))
        return {nm: self.arr[nm] for nm in names}


class Reducer:
    def __init__(self, cq_arr, adam):
        self.cq_arr, self.adam = cq_arr, adam
        self.groups = []
        self.n = 0
        self.last = None

    @staticmethod
    def _mk1(gi, li, spec, q):
        def mk(refs, ss, rs):
            x, y, c = _mesh_pos()
            return _remote(_window(refs[gi], spec[1], spec[2], 2 * q + (1 - c)), refs[li].at[q], ss, rs, (x, y, 1 - c))
        return mk

    @staticmethod
    def _mk2(si, li, d):
        def mk(refs, ss, rs):
            x, y, c = _mesh_pos()
            qd = lax.rem(2 * x + y + d, 4)
            return _remote(refs[si].at[d - 1], refs[li].at[3 - d], ss, rs, (lax.div(qd, 2), lax.rem(qd, 2), c))
        return mk

    def add(self, grads, after=None):
        names = list(grads)
        m = len(names)
        starts = [self._mk1(ai, m + ai, BIG[nm], q) for ai, nm in enumerate(names) for q in range(4)]
        new = [_sds((4,) + _win_shape(BIG[nm]), BF16) for nm in names]
        res, sems, tok = split_call("rs1_start%d" % self.n, [grads[nm][1] for nm in names], starts=starts, new=new,
                                    after=after)
        self.groups.append(dict(names=names, g=[grads[nm][0] for nm in names], starts=starts, buf=res, sems=sems,
                                stage=1, age=0, idx=self.n))
        self.n += 1
        return tok

    def step(self, after):
        tok = None
        for grp in self.groups:
            names, m = grp["names"], len(grp["names"])
            if grp["stage"] == 1:
                waits = [(0, k, kind, mk) for k, mk in enumerate(grp["starts"]) for kind in ("send", "recv")]
                res, _, _ = split_call("rs1_wait%d" % grp["idx"], grp["buf"], waits=waits, sems_in=[grp["sems"]], after=after)
                land1 = res[m:]
                s1b = [add_pairs(g, l, BIG[nm], self.cq_arr) for nm, g, l in zip(names, grp["g"], land1)]
                starts = [self._mk2(ai, m + ai, d) for ai in range(m) for d in (1, 2, 3)]
                new = [_sds(a.shape, BF16) for a in s1b]
                res, sems, tok = split_call("rs2_start%d" % grp["idx"], s1b, starts=starts, new=new, after=tok)
                grp.update(stage=2, land1=land1, starts=starts, buf=res, sems=sems)
        return tok

    def finish_oldest(self):
        for grp in self.groups:
            if grp["stage"] == 2:
                names, m = grp["names"], len(grp["names"])
                waits = [(0, k, kind, mk) for k, mk in enumerate(grp["starts"]) for kind in ("send", "recv")]
                res, _, _ = split_call("rs2_wait%d" % grp["idx"], grp["buf"], waits=waits, sems_in=[grp["sems"]],
                                       after=self.last)
                for nm, g, l1, l2 in zip(names, grp["g"], grp["land1"], res[m:]):
                    self.last = self.adam(nm, g, l1, l2, self.last)
                grp["stage"] = 3
                return True
        return False


def pack_weights(args, arg_names, small_blk, names, j_arr):
    n_in = len(args)

    def body(j_ref, *refs):
        for o, nm in enumerate(names):
            dst = refs[n_in + 1 + o]
            if nm == "small":
                dst[...] = refs[n_in][...]
                continue
            _, axis, w, valid, arg, layer = BIG[nm]
            if arg in TRANSPOSED_ARGS:
                s = refs[arg_names.index(arg)][layer]
                s = jnp.concatenate([s, jnp.zeros((w - valid, s.shape[1]), F32)], axis=0)
                dst[...] = s.T.astype(BF16)
                continue
            src = refs[arg_names.index(arg)][layer].astype(BF16)
            if valid == w:
                dst[...] = src
            else:
                dst[...] = jnp.zeros(dst.shape, BF16)
                if axis == 1:
                    dst[:, 0:valid] = src
                else:
                    dst[0:valid, :] = src

    def ispec(a):
        return pl.BlockSpec(a.shape, lambda i, j_ref: (0, 0, 0))

    def ospec(spec):
        axis, nd = spec[1], len(spec[0])
        return pl.BlockSpec(_win_shape(spec),
                            lambda i, j_ref, axis=axis, nd=nd: tuple(j_ref[0] if d == axis else 0 for d in range(nd)))

    specs = [_spec(nm) for nm in names]
    return pl.pallas_call(
        body, name="pack_weights",
        grid_spec=pltpu.PrefetchScalarGridSpec(
            num_scalar_prefetch=1, grid=(1,),
            in_specs=[ispec(a) for a in list(args) + [small_blk]], out_specs=[ospec(s) for s in specs]),
        out_shape=[_sds(s[0], F32 if nm == "small" else BF16) for nm, s in zip(names, specs)],
        compiler_params=_cp(("arbitrary",)),
    )(j_arr, *args, small_blk)


def add_pairs(full, land, spec, cq_arr):
    axis, w = spec[1], spec[2]
    R, C = full.shape

    def chip(d, cq):
        return lax.rem(cq[1] + d + 1, 4)

    if axis == 1:
        tr = R
        grid = (3, R // tr)
        fspec = pl.BlockSpec((tr, w), lambda d, i, cq: (i, 2 * chip(d, cq) + cq[0]))
        lspec = pl.BlockSpec((None, tr, w), lambda d, i, cq: (chip(d, cq), i, 0))
        ospec = pl.BlockSpec((None, tr, w), lambda d, i, cq: (d, i, 0))
    else:
        grid = (3, 1)
        fspec = pl.BlockSpec((w, C), lambda d, i, cq: (2 * chip(d, cq) + cq[0], 0))
        lspec = pl.BlockSpec((None, w, C), lambda d, i, cq: (chip(d, cq), 0, 0))
        ospec = pl.BlockSpec((None, w, C), lambda d, i, cq: (d, 0, 0))

    def body(cq_ref, a_ref, b_ref, ob_ref):
        ob_ref[...] = (a_ref[...] + b_ref[...].astype(F32)).astype(BF16)

    return pl.pallas_call(
        body, name="add_pairs",
        grid_spec=pltpu.PrefetchScalarGridSpec(
            num_scalar_prefetch=1, grid=grid, in_specs=[fspec, lspec], out_specs=[ospec]),
        out_shape=[_sds((3,) + land.shape[1:], BF16)],
        compiler_params=_cp(("arbitrary",) * 2),
    )(cq_arr, full, land)[0]


def _adamw(w, g, m, v):
    m = ADAM_B1 * m + (1.0 - ADAM_B1) * g
    v = ADAM_B2 * v + (1.0 - ADAM_B2) * (g * g)
    m_hat = m / (1.0 - ADAM_B1 ** ADAM_STEP)
    v_hat = v / (1.0 - ADAM_B2 ** ADAM_STEP)
    delta = -ADAM_LR * (m_hat / (jnp.sqrt(v_hat) + ADAM_EPS) + ADAM_WD * w)
    return delta, m, v


def reduce_adamw(full, land1, land, w, m, v, spec, cq_arr, prev=None, dep=None):
    axis, win, valid, layer = spec[1], spec[2], spec[3], spec[5]
    L, R, C = w.shape
    transposed = spec[4] in TRANSPOSED_ARGS
    if transposed:
        grid = (1,)
        fspec = pl.BlockSpec((C, win), lambda i, cq: (0, 2 * cq[1] + cq[0]))
        wspec = pl.BlockSpec((None, C, win), lambda i, cq: (cq[1], 0, 0))
        lspec = pl.BlockSpec((3, C, win), lambda i, cq: (0, 0, 0))
        sspec = pl.BlockSpec((None, R, C), lambda i, cq: (layer, 0, 0))
    elif axis == 1:
        tr = min(1024, R)
        grid = (R // tr,)
        fspec = pl.BlockSpec((tr, win), lambda i, cq: (i, 2 * cq[1] + cq[0]))
        wspec = pl.BlockSpec((None, tr, win), lambda i, cq: (cq[1], i, 0))
        lspec = pl.BlockSpec((3, tr, win), lambda i, cq: (0, i, 0))
        sspec = pl.BlockSpec((None, tr, C), lambda i, cq: (layer, i, 0))
    else:
        grid = (1,)
        fspec = pl.BlockSpec((win, full.shape[1]), lambda i, cq: (2 * cq[1] + cq[0], 0))
        wspec = pl.BlockSpec((None, win, C), lambda i, cq: (cq[1], 0, 0))
        lspec = pl.BlockSpec((3, win, C), lambda i, cq: (0, 0, 0))
        sspec = pl.BlockSpec((None, R, C), lambda i, cq: (layer, 0, 0))

    def body(cq_ref, full_ref, own_ref, land_ref, w_ref, m_ref, v_ref, *rest):
        g_ref, d_ref, nm_ref, nv_ref = rest[-4:]
        if transposed:
            rd = lambda r, *lead: r[lead] if lead else r[...]
        elif axis == 1:
            rd = lambda r, *lead: r[(*lead, slice(None), slice(0, valid))]
        else:
            rd = lambda r, *lead: r[(*lead, slice(0, valid), slice(None))]
        g = rd(full_ref) + rd(own_ref).astype(F32)
        for k in range(3):
            g = g + rd(land_ref, k).astype(F32)
        if transposed:
            g = g.T[0:valid, :]
        g_ref[...] = g
        d, nm, nv = _adamw(w_ref[...], g, m_ref[...], v_ref[...])
        d_ref[...] = d
        nm_ref[...] = nm
        nv_ref[...] = nv

    extra = (list(prev) if prev is not None else []) + ([dep] if dep is not None else [])
    return pl.pallas_call(
        body, name="reduce_adamw",
        grid_spec=pltpu.PrefetchScalarGridSpec(
            num_scalar_prefetch=1, grid=grid,
            in_specs=[fspec, wspec, lspec, sspec, sspec, sspec] + [ANY] * len(extra), out_specs=[sspec] * 4),
        out_shape=[_sds(w.shape)] * 4,
        input_output_aliases={7 + k: k for k in range(4 if prev is not None else 0)},
        compiler_params=_cp(("arbitrary",)),
    )(cq_arr, full, land1, land, w, m, v, *extra)


def place_slot(packed, j_arr):
    R = packed.shape[0]

    def body(j_ref, src, dst):
        dst[...] = src[...]

    return pl.pallas_call(
        body, name="place_slot",
        grid_spec=pltpu.PrefetchScalarGridSpec(
            num_scalar_prefetch=1, grid=(1,),
            in_specs=[pl.BlockSpec((R, 128), lambda i, j: (0, 0))],
            out_specs=[pl.BlockSpec((None, R, 128), lambda i, j: (j[0], 0, 0))]),
        out_shape=[_sds((N_DEV, R, 128))], compiler_params=_cp(("arbitrary",)),
    )(j_arr, packed)[0]


def sum_slots(gathered):
    def body(g_ref, o_ref):
        g = g_ref[0]
        for dev in range(1, N_DEV):
            g = g + g_ref[dev]
        o_ref[...] = g

    return pl.pallas_call(body, name="sum_slots", out_shape=_sds(gathered.shape[1:]), compiler_params=_cp())(gathered)


def small_adamw(gs, wmv):
    k = len(gs)

    def body(*refs):
        for a in range(k):
            g, w, m, v = refs[4 * a:4 * a + 4]
            d, nm, nv = _adamw(w[...], g[...], m[...], v[...])
            refs[4 * k + 3 * a][...] = d
            refs[4 * k + 3 * a + 1][...] = nm
            refs[4 * k + 3 * a + 2][...] = nv

    args = [t for g, tup in zip(gs, wmv) for t in (g,) + tuple(tup)]
    out_shape = [_sds(g.shape) for g in gs for _ in range(3)]
    return pl.pallas_call(body, name="small_adamw", out_shape=out_shape, compiler_params=_cp())(*args)


WEIGHT_NAMES = ("even_w_in", "even_w_out", "pool_w", "pool_scale", "odd_w_in", "odd_w_out", "conv_dw", "conv_ln_g",
                "conv_ln_b", "sg_ln_g", "sg_ln_b", "sg_w", "sg_b", "ln_mix_g", "ln_mix_b", "ffn_w_gate", "ffn_w_up",
                "ffn_w_down", "ln_ffn_g", "ln_ffn_b", "ple_w_proj", "ple_w_gate", "ple_b_gate")
PACK_ARGS = ("even_w_in", "even_w_out", "odd_w_in", "odd_w_out", "ffn_w_gate", "ffn_w_up", "ffn_w_down",
             "ple_w_gate", "ple_w_proj")
REPLICATED = ("pool_w", "pool_scale", "sg_w", "sg_b", "ln_mix_g", "ln_mix_b", "ln_ffn_g", "ln_ffn_b", "ple_b_gate")
SHARDED_SMALL = ("conv_dw", "conv_ln_g", "conv_ln_b", "sg_ln_g", "sg_ln_b")
NATURAL = {"pool_w": (4, 128, 128), "pool_scale": (1, 512), "sg_w": (4, 128, 128), "sg_b": (4, 128),
           "ln_mix_g": (2, 1024), "ln_mix_b": (2, 1024), "ln_ffn_g": (2, 1024), "ln_ffn_b": (2, 1024),
           "ple_b_gate": (2, 1024)}


def kernel(x, p, even_w_in, even_w_out, pool_w, pool_scale, odd_w_in, odd_w_out, conv_dw, conv_ln_g, conv_ln_b, sg_ln_g, sg_ln_b, sg_w, sg_b, ln_mix_g, ln_mix_b, ffn_w_gate, ffn_w_up, ffn_w_down, ln_ffn_g, ln_ffn_b, ple_w_proj, ple_w_gate, ple_b_gate, loss_target, m_even_w_in, m_even_w_out, m_pool_w, m_pool_scale, m_odd_w_in, m_odd_w_out, m_conv_dw, m_conv_ln_g, m_conv_ln_b, m_sg_ln_g, m_sg_ln_b, m_sg_w, m_sg_b, m_ln_mix_g, m_ln_mix_b, m_ffn_w_gate, m_ffn_w_up, m_ffn_w_down, m_ln_ffn_g, m_ln_ffn_b, m_ple_w_proj, m_ple_w_gate, m_ple_b_gate, v_even_w_in, v_even_w_out, v_pool_w, v_pool_scale, v_odd_w_in, v_odd_w_out, v_conv_dw, v_conv_ln_g, v_conv_ln_b, v_sg_ln_g, v_sg_ln_b, v_sg_w, v_sg_b, v_ln_mix_g, v_ln_mix_b, v_ffn_w_gate, v_ffn_w_up, v_ffn_w_down, v_ln_ffn_g, v_ln_ffn_b, v_ple_w_proj, v_ple_w_gate, v_ple_b_gate):
    A = dict(locals())
    for arg in TRANSPOSED_ARGS:
        for pre in ("", "m_", "v_"):
            A[pre + arg] = jnp.swapaxes(A[pre + arg], 1, 2)
    mx, my, mc = _mesh_pos()
    j = 4 * mx + 2 * my + mc
    j_arr = j.astype(jnp.int32).reshape(1)
    cq_arr = jnp.stack([mc, 2 * mx + my]).astype(jnp.int32)
    res = {}

    def adam(nm, full, land1, land2, dep):
        arg = BIG[nm][4]
        res[arg] = reduce_adamw(full, land1, land2, A[arg], A["m_" + arg], A["v_" + arg], BIG[nm], cq_arr,
                                res.get(arg), dep)
        return res[arg][0]

    class Comm:
        def __init__(self):
            names = [nm for g in AG_GROUPS for nm in g]
            small_blk = jnp.concatenate([conv_dw[0], conv_ln_g, conv_ln_b, sg_ln_g, sg_ln_b, jnp.zeros((5, 64), F32)], axis=0)
            mine = pack_weights([A[k] for k in PACK_ARGS], PACK_ARGS, small_blk[None], names, j_arr)
            self.gat = Gatherer(AG_GROUPS, mine, {nm: _spec(nm) for nm in names}, "ag")
            self.gat.start()
            self.red = Reducer(cq_arr, adam)
            self.W = {k: A[k].reshape(NATURAL[k]) for k in REPLICATED}

        def weights(self, stage, after):
            if stage in AG_NEED:
                got = self.gat.finish(AG_NEED[stage], after)
                if "small" in got:
                    sm = got.pop("small").transpose(1, 0, 2).reshape(40, 512)
                    got.update(conv_dw=sm[0:31], conv_ln_g=sm[31:32], conv_ln_b=sm[32:33], sg_ln_g=sm[33:34],
                               sg_ln_b=sm[34:35])
                self.W.update(got)
            return self.W

        def all_weights(self):
            return self.W

        def poke(self, tag, after):
            if tag in AG_PASS:
                return self.gat.forward(AG_PASS[tag], after)
            if tag[0] == "bwd":
                return self.red.step(after)
            return None

        def grads(self, grads):
            tok = self.red.step(next(iter(grads.values()))[0])
            return self.red.add(grads, after=tok)

    comm = Comm()
    sq, dx, small = run_layers(x[0], p[:, 0], loss_target[0], comm)
    loss = lax.psum(0.5 * jnp.sum(sq) / x.shape[-1], ("x", "y", "c"))
    red = comm.red
    tok = red.step(dx)

    names = REPLICATED + SHARDED_SMALL
    flat = jnp.concatenate([small[k].reshape(-1) for k in names])
    rows = -(-flat.shape[0] // 1024) * 8
    packed = jnp.pad(flat, (0, rows * 128 - flat.shape[0])).reshape(rows, 128)
    sg = Gatherer((["g"],), [place_slot(packed, j_arr)], {"g": ((N_DEV, rows, 128), 0, 1, 1)}, "sg")
    sg.start(after=tok)
    red.finish_oldest()
    sg.forward(0, after=red.last)
    red.finish_oldest()
    red.finish_oldest()
    gsum_flat = sum_slots(sg.finish(0, after=red.last)["g"]).reshape(-1)
    gs, off = [], 0
    for k in names:
        n = math.prod(small[k].shape)
        g = gsum_flat[off:off + n].reshape(small[k].shape)
        off += n
        if k in SHARDED_SMALL:
            g = lax.dynamic_slice_in_dim(g, j * 64, 64, axis=1)
        gs.append(g.reshape(A[k].shape))
    outs = small_adamw(gs, [(A[k], A["m_" + k], A["v_" + k]) for k in names])
    for a, k in enumerate(names):
        res[k] = (gs[a],) + tuple(outs[3 * a:3 * a + 3])
    red.last = outs[0]
    while red.finish_oldest():
        pass

    for arg in TRANSPOSED_ARGS:
        res[arg] = [jnp.swapaxes(t, 1, 2) for t in res[arg]]
    out = [loss, dx[None]]
    for part in range(4):
        out += [res[k][part] for k in WEIGHT_NAMES]
    return tuple(out)
```

```python
import functools
import math

import jax
import jax.numpy as jnp
from jax import lax
from jax.experimental import pallas as pl
from jax.experimental.pallas import tpu as pltpu

F32, BF16 = jnp.float32, jnp.bfloat16
ALPHA = 4.0 ** 0.25
LN_EPS = 1e-5
QK_SCALE = 0.125
POOL_WINDOWS = (2, 4, 8, 16)
CONV_TAPS = 31
N_DEV = 8
FF_SHARD, FF_PAD = 352, 384
ADAM_LR, ADAM_B1, ADAM_B2, ADAM_EPS, ADAM_WD, ADAM_STEP = 0.001, 0.9, 0.999, 1e-08, 0.01, 10
VMEM_LIMIT = 56 * 1024 * 1024
MESH_T = pl.DeviceIdType.MESH


def _cp(sem=None):
    return pltpu.CompilerParams(dimension_semantics=sem, vmem_limit_bytes=VMEM_LIMIT)


def _dot(a, b):
    return jnp.dot(a, b, preferred_element_type=F32)


def _dot_nt(a, b):
    return lax.dot_general(a, b, (((1,), (1,)), ((), ())), preferred_element_type=F32)


def _dot_tn(a, b):
    return lax.dot_general(a, b, (((0,), (0,)), ((), ())), preferred_element_type=F32)


def _sigmoid(x):
    return 1.0 / (1.0 + jnp.exp(-x))


def _softplus(z):
    return jnp.maximum(z, 0.0) + jnp.log(1.0 + jnp.exp(-jnp.abs(z)))


_GELU_C = math.sqrt(2.0 / math.pi)


def _gelu(x):
    return 0.5 * x * (1.0 + jnp.tanh(_GELU_C * (x + 0.044715 * x * x * x)))


def _gelu_grad(x):
    t = jnp.tanh(_GELU_C * (x + 0.044715 * x * x * x))
    return 0.5 * (1.0 + t) + 0.5 * x * (1.0 - t * t) * _GELU_C * (1.0 + 3.0 * 0.044715 * x * x)


def _ln_fwd(r, g, b):
    mu = jnp.mean(r, axis=-1, keepdims=True)
    xc = r - mu
    var = jnp.mean(xc * xc, axis=-1, keepdims=True)
    rstd = lax.rsqrt(var + LN_EPS)
    xh = xc * rstd
    return xh * g + b, xh, rstd


def _ln_bwd(dy, xh, rstd, g):
    dxh = dy * g
    m1 = jnp.mean(dxh, axis=-1, keepdims=True)
    m2 = jnp.mean(dxh * xh, axis=-1, keepdims=True)
    return rstd * (dxh - m1 - xh * m2)


def _split2(x):
    hi = x.astype(BF16)
    lo = (x - hi.astype(F32)).astype(BF16)
    return hi, lo


def _colsum(x):
    return jnp.sum(x, axis=0, keepdims=True)


def _tok_call(name, body, tiled, full, out_tiled, out_acc=(), tm=256, scratch=(), dep=None):
    def arr(t):
        return t[0] if isinstance(t, tuple) else t
    full = [t[0] if isinstance(t, tuple) and t[1] is None else t for t in full]
    S = arr(tiled[0]).shape[0]
    tm = min(tm, S)
    n_in = len(tiled) + len(full)
    deps = [] if dep is None else [dep]
    if deps:
        inner = body
        body = lambda *refs: inner(*refs[:n_in], *refs[n_in + 1:])

    def tspec(t):
        if isinstance(t, tuple):
            _, w, cb = t
            return pl.BlockSpec((tm, w), lambda i, cb=cb: (i, cb))
        return pl.BlockSpec((tm, t.shape[1]), lambda i: (i, 0))

    def fspec(t):
        if isinstance(t, tuple):
            a, l = t
            nd = a.ndim - 1
            return pl.BlockSpec((None,) + a.shape[1:], lambda i, l=l, nd=nd: (l,) + (0,) * nd)
        nd = t.ndim
        return pl.BlockSpec(t.shape, lambda i, nd=nd: (0,) * nd)

    def ospec(o):
        return pl.BlockSpec((tm, o.shape[1]), lambda i: (i, 0))

    def aspec(o):
        nd = len(o.shape)
        return pl.BlockSpec(o.shape, lambda i, nd=nd: (0,) * nd)

    outs = pl.pallas_call(
        body, name=name, grid=(S // tm,),
        in_specs=[tspec(t) for t in tiled] + [fspec(t) for t in full] + [ANY] * len(deps),
        out_specs=[ospec(o) for o in out_tiled] + [aspec(o) for o in out_acc],
        out_shape=list(out_tiled) + list(out_acc),
        scratch_shapes=list(scratch),
        compiler_params=_cp(("arbitrary",)),
    )(*[arr(t) for t in tiled], *[arr(t) for t in full], *deps)
    return outs


def _sds(shape, dtype=F32):
    return jax.ShapeDtypeStruct(tuple(shape), dtype)


def _acc(ref, val):
    @pl.when(pl.program_id(0) == 0)
    def _():
        ref[...] = val

    @pl.when(pl.program_id(0) != 0)
    def _():
        ref[...] += val


def mm_in(x, w, nb16=0):
    S, N = x.shape[0], w.shape[1]

    def body(x_ref, w_ref, h_ref, xb_ref, *hb_ref):
        xb = x_ref[...].astype(BF16)
        xb_ref[...] = xb
        h = _dot(xb, w_ref[...])
        h_ref[...] = h
        if nb16:
            hb_ref[0][...] = h[:, 0:nb16].astype(BF16)

    outs = [_sds((S, N)), _sds((S, x.shape[1]), BF16)] + ([_sds((S, nb16), BF16)] if nb16 else [])
    return _tok_call("mm_in", body, [x], [w], outs, tm=512)


def _stack_heads(x, hm0, dtype=BF16):
    return jnp.concatenate([jnp.where(hm0, x, 0), jnp.where(hm0, 0, x)], axis=0).astype(dtype)


def _unstack_k(x, T):
    return jnp.concatenate([x[0:T], x[T:2 * T]], axis=1)


def _cumsum_mm(x, u):
    n = x.shape[0]
    hi, lo = _split2(x)
    r = _dot(jnp.concatenate([hi, lo], axis=0), u)
    return r[0:n] + r[n:2 * n]


def attn_fwd(qkv, T=256):
    S = qkv.shape[0]
    T = min(T, S)
    nq = S // T

    def body(q_ref, k_ref, v_ref, o_ref, t_ref, acc_ref, c_ref, qh_ref):
        i = pl.program_id(0)
        hm0 = lax.broadcasted_iota(jnp.int32, (1, 128), 1) < 64
        r2 = lax.broadcasted_iota(jnp.int32, (2 * T, T), 0)
        c2 = lax.broadcasted_iota(jnp.int32, (2 * T, T), 1)
        causal = c2 < jnp.where(r2 >= T, r2 - T, r2)
        ur = lax.broadcasted_iota(jnp.int32, (T, T), 0)
        uc = lax.broadcasted_iota(jnp.int32, (T, T), 1)
        u_incl = (ur >= uc).astype(BF16)
        acc_ref[...] = jnp.zeros_like(acc_ref)
        c_ref[...] = jnp.zeros_like(c_ref)
        for pp in range(4):
            qh_ref[pp] = _stack_heads(q_ref[:, pp * 128:(pp + 1) * 128] * QK_SCALE, hm0)

        def block(kb, diag):
            ks = pl.multiple_of(kb * T, T)
            for pp in range(4):
                cs = slice(pp * 128, (pp + 1) * 128)
                z = _dot_nt(qh_ref[pp], k_ref[pl.ds(ks, T), cs])
                sp = _softplus(z)
                if diag:
                    sp = jnp.where(causal, sp, 0.0)
                incl = _cumsum_mm(sp, u_incl)
                c = c_ref[pp]
                w = jnp.exp(z - incl - c)
                if diag:
                    w = jnp.where(causal, w, 0.0)
                acc_ref[:, cs] += _dot(_unstack_k(w.astype(BF16), T), _stack_heads(v_ref[pl.ds(ks, T), cs], hm0))
                c_ref[pp] = c + jnp.broadcast_to(incl[:, 0:1], (2 * T, T))

        block(i, True)

        def step(jj, carry):
            block(i - 1 - jj, False)
            return carry

        lax.fori_loop(0, i, step, 0)
        o_ref[...] = acc_ref[...].astype(BF16)
        for pp in range(4):
            for hd in range(2):
                t_ref[2 * pp + hd] = c_ref[pp, hd * T:(hd + 1) * T, 0:128]

    return pl.pallas_call(
        body, name="attn_fwd", grid=(nq,),
        in_specs=[pl.BlockSpec((T, 512), lambda i: (i, 0)),
                  pl.BlockSpec((S, 512), lambda i: (0, 1)),
                  pl.BlockSpec((S, 512), lambda i: (0, 2))],
        out_specs=[pl.BlockSpec((T, 512), lambda i: (i, 0)),
                   pl.BlockSpec((8, T, 128), lambda i: (0, i, 0))],
        out_shape=[_sds((S, 512), BF16), _sds((8, S, 128))],
        scratch_shapes=[pltpu.VMEM((T, 512), F32), pltpu.VMEM((4, 2 * T, T), F32), pltpu.VMEM((4, 2 * T, 128), BF16)],
        compiler_params=_cp(("arbitrary",)),
    )(qkv, qkv, qkv)


def pool_fwd(h, pool_w, pool_scale, CH=256):
    S = h.shape[0]
    CH = min(CH, S)

    def body(u_ref, w_ref, sc_ref, b_ref, pooled_ref, pad_ref):
        pad_ref[0:16, :] = jnp.zeros((16, 512), F32)
        pad_ref[16:16 + S, :] = u_ref[...]
        for g, win in enumerate(POOL_WINDOWS):
            cs = slice(g * 128, (g + 1) * 128)
            wq = w_ref[g].astype(BF16)
            for ch in range(S // CH):
                base = ch * CH
                acc = pad_ref[16 + base:16 + base + CH, cs]
                for sft in range(1, win):
                    acc = acc + pad_ref[16 + base - sft:16 + base - sft + CH, cs]
                t = base + lax.broadcasted_iota(jnp.int32, (CH, 1), 0)
                cnt = jnp.minimum(t + 1, win).astype(F32)
                pooled = (acc / cnt - pad_ref[16 + base:16 + base + CH, cs]).astype(BF16)
                pooled_ref[base:base + CH, cs] = pooled
                b_ref[base:base + CH, cs] = (_dot(pooled, wq) * sc_ref[:, cs]).astype(BF16)

    return pl.pallas_call(
        body, name="pool_fwd", grid=(1,),
        in_specs=[pl.BlockSpec((S, 512), lambda i: (0, 3)),
                  pl.BlockSpec((4, 128, 128), lambda i: (0, 0, 0)),
                  pl.BlockSpec((1, 512), lambda i: (0, 0))],
        out_specs=[pl.BlockSpec((S, 512), lambda i: (0, 0)), pl.BlockSpec((S, 512), lambda i: (0, 0))],
        out_shape=[_sds((S, 512), BF16), _sds((S, 512), BF16)],
        scratch_shapes=[pltpu.VMEM((S + 16, 512), F32)],
        compiler_params=_cp(("arbitrary",)),
    )(h, pool_w, pool_scale)


def conv_fwd(h, dw, CH=128):
    S = h.shape[0]

    def body(a_ref, g_ref, dw_ref, y_ref, hc_ref, pad_ref):
        hc = a_ref[...] * _sigmoid(g_ref[...])
        hc_ref[...] = hc
        pad_ref[0:32, :] = jnp.zeros((32, 128), F32)
        pad_ref[32:32 + S, :] = hc
        for ch in range(S // CH):
            base = ch * CH + 2
            acc = dw_ref[0:1, :] * pad_ref[base:base + CH, :]
            for k in range(1, CONV_TAPS):
                acc = acc + dw_ref[k:k + 1, :] * pad_ref[base + k:base + k + CH, :]
            y_ref[ch * CH:(ch + 1) * CH, :] = acc

    return pl.pallas_call(
        body, name="conv_fwd", grid=(4,),
        in_specs=[pl.BlockSpec((S, 128), lambda c: (0, c)),
                  pl.BlockSpec((S, 128), lambda c: (0, 4 + c)),
                  pl.BlockSpec((CONV_TAPS, 128), lambda c: (0, c))],
        out_specs=[pl.BlockSpec((S, 128), lambda c: (0, c)), pl.BlockSpec((S, 128), lambda c: (0, c))],
        out_shape=[_sds((S, 512)), _sds((S, 512))],
        scratch_shapes=[pltpu.VMEM((S + 32, 128), F32)],
        compiler_params=_cp(("arbitrary",)),
    )(h, h, dw)


def _masked_sg_w(w_ref, g):
    row = lax.broadcasted_iota(jnp.int32, (128, 128), 0)
    col = lax.broadcasted_iota(jnp.int32, (128, 128), 1)
    return jnp.where(row >= col, w_ref[g], 0.0).astype(BF16)


def odd_post(y, h, cl_g, cl_b, sl_g, sl_b, sg_w, sgb_bc, tm=256):
    S = y.shape[0]
    tm = min(tm, S)

    def body(y_ref, zc_ref, clg, clb, slg, slb, w_ref, sb_ref,
             c_ref, d_ref, xhc_ref, rsc_ref, xhv_ref, rsv_ref, sv_ref):
        lnc, xhc, rsc = _ln_fwd(y_ref[...], clg[...], clb[...])
        c_ref[...] = (lnc * _sigmoid(lnc)).astype(BF16)
        xhc_ref[...] = xhc
        rsc_ref[...] = rsc
        z = _gelu(zc_ref[...])
        vn, xhv, rsv = _ln_fwd(z[:, 512:], slg[...], slb[...])
        xhv_ref[...] = xhv
        rsv_ref[...] = rsv
        vnb = vn.astype(BF16)
        for g in range(4):
            wm = _masked_sg_w(w_ref, g)
            for ch in range(tm // 128):
                rs, cs = slice(ch * 128, (ch + 1) * 128), slice(g * 128, (g + 1) * 128)
                sv_ref[rs, cs] = _dot(wm, vnb[rs, cs]) + sb_ref[g]
        d_ref[...] = (z[:, :512] * sv_ref[...]).astype(BF16)

    return _tok_call(
        "odd_post", body, [y, (h, 1024, 1)], [cl_g, cl_b, sl_g, sl_b, sg_w, sgb_bc],
        [_sds((S, 512), BF16), _sds((S, 512), BF16), _sds((S, 512)), _sds((S, 1)),
         _sds((S, 512)), _sds((S, 1)), _sds((S, 512))], tm=tm)


def mm_out_ln(l1, l2, x, w, g, b, dep=None):
    S, D = x.shape

    def body(l1_ref, l2_ref, x_ref, w_ref, g_ref, b_ref, y_ref, xh_ref, rs_ref):
        mix = _dot(l1_ref[...], w_ref[0:512, :]) + _dot(l2_ref[...], w_ref[512:1024, :])
        y, xh, rs = _ln_fwd(ALPHA * x_ref[...] + mix, g_ref[...], b_ref[...])
        y_ref[...] = y
        xh_ref[...] = xh
        rs_ref[...] = rs

    return _tok_call("mm_out_ln", body, [l1, l2, x], [w, g, b],
                     [_sds((S, D)), _sds((S, D)), _sds((S, 1))], dep=dep)


def ffn_up(x1, wg, wu, layer):
    S, D = x1.shape
    F = wg.shape[-1]

    def body(x_ref, wg_ref, wu_ref, gate_ref, up_ref, hb_ref, xb_ref):
        xb = x_ref[...].astype(BF16)
        xb_ref[...] = xb
        gate = _dot(xb, wg_ref[...])
        up = _dot(xb, wu_ref[...])
        gate_ref[...] = gate.astype(BF16)
        up_ref[...] = up.astype(BF16)
        hb_ref[...] = (gate * _sigmoid(gate) * up).astype(BF16)

    return _tok_call("ffn_up", body, [x1], [(wg, layer), (wu, layer)],
                     [_sds((S, F), BF16), _sds((S, F), BF16), _sds((S, F), BF16), _sds((S, D), BF16)])


def ffn_down_ln(hb, x1, wd, layer, g, b):
    S, D = x1.shape

    def body(h_ref, x_ref, w_ref, g_ref, b_ref, y_ref, xh_ref, rs_ref):
        f = _dot(h_ref[...], w_ref[...])
        y, xh, rs = _ln_fwd(ALPHA * x_ref[...] + f, g_ref[...], b_ref[...])
        y_ref[...] = y
        xh_ref[...] = xh
        rs_ref[...] = rs

    return _tok_call("ffn_down_ln", body, [hb, x1], [(wd, layer), g, b],
                     [_sds((S, D)), _sds((S, D)), _sds((S, 1))])


def ple_fwd(x2, p, wpg, wpp, layer, bg, target=None, dep=None):
    S, D = x2.shape
    last = target is not None

    def body(*refs):
        if last:
            x_ref, p_ref, t_ref, wg_ref, wp_ref, b_ref, x3_ref, sg_ref, pp_ref, xb_ref, pb_ref, dy_ref, ls_ref = refs
        else:
            x_ref, p_ref, wg_ref, wp_ref, b_ref, x3_ref, sg_ref, pp_ref, xb_ref, pb_ref = refs
        x = x_ref[...]
        xb = x.astype(BF16)
        pb = p_ref[...].astype(BF16)
        xb_ref[...] = xb
        pb_ref[...] = pb
        sg = _sigmoid(_dot(xb, wg_ref[...]) + b_ref[...])
        pp = _dot(pb, wp_ref[...])
        sg_ref[...] = sg.astype(BF16)
        pp_ref[...] = pp.astype(BF16)
        x3 = x + sg * pp
        x3_ref[...] = x3
        if last:
            err = x3 - t_ref[...]
            dy_ref[...] = err * (1.0 / D)
            _acc(ls_ref, _colsum(err * err))

    outs = [_sds((S, D)), _sds((S, D), BF16), _sds((S, D), BF16), _sds((S, D), BF16), _sds((S, p.shape[1]), BF16)]
    tiled = [x2, p] + ([target] if last else [])
    if last:
        outs.append(_sds((S, D)))
    return _tok_call("ple_fwd", body, tiled, [(wpg, layer), (wpp, layer), bg], outs,
                     [_sds((1, D))] if last else [], dep=dep)


def ple_bwd(dx3, sg, pp, wpg, layer, dep=None):
    S, D = dx3.shape

    def body(d_ref, sg_ref, pp_ref, w_ref, dx_ref, dgp_ref, dpp_ref, dbg_ref):
        d, sg = d_ref[...], sg_ref[...].astype(F32)
        dgp = d * pp_ref[...].astype(F32) * sg * (1.0 - sg)
        dgpb = dgp.astype(BF16)
        dgp_ref[...] = dgpb
        dpp_ref[...] = (d * sg).astype(BF16)
        dx_ref[...] = d + _dot_nt(dgpb, w_ref[...])
        _acc(dbg_ref, _colsum(dgp))

    return _tok_call("ple_bwd", body, [dx3, sg, pp], [(wpg, layer)],
                     [_sds((S, D)), _sds((S, D), BF16), _sds((S, D), BF16)], [_sds((1, D))], dep=dep)


def ffn_bwd_a(dx2, xh, rs, g, gate, up, wd, layer):
    S, D = dx2.shape
    F = gate.shape[1]

    def body(d_ref, xh_ref, rs_ref, gate_ref, up_ref, g_ref, w_ref,
             dr_ref, drb_ref, dg_ref, du_ref, dlg_ref, dlb_ref):
        d, xh = d_ref[...], xh_ref[...]
        dr = _ln_bwd(d, xh, rs_ref[...], g_ref[...])
        drb = dr.astype(BF16)
        dr_ref[...] = dr
        drb_ref[...] = drb
        _acc(dlg_ref, _colsum(d * xh))
        _acc(dlb_ref, _colsum(d))
        dh = _dot_nt(drb, w_ref[...])
        gate, up = gate_ref[...].astype(F32), up_ref[...].astype(F32)
        s = _sigmoid(gate)
        dg_ref[...] = (dh * up * s * (1.0 + gate * (1.0 - s))).astype(BF16)
        du_ref[...] = (dh * gate * s).astype(BF16)

    return _tok_call("ffn_bwd_a", body, [dx2, xh, rs, gate, up], [g, (wd, layer)],
                     [_sds((S, D)), _sds((S, D), BF16), _sds((S, F), BF16), _sds((S, F), BF16)],
                     [_sds((1, D)), _sds((1, D))])


def ffn_bwd_b(dr, dgate_b, dup_b, wg, wu, layer, dep=None):
    S, D = dr.shape

    def body(dr_ref, dg_ref, du_ref, wg_ref, wu_ref, dx_ref):
        dx_ref[...] = (ALPHA * dr_ref[...] + _dot_nt(dg_ref[...], wg_ref[...])
                       + _dot_nt(du_ref[...], wu_ref[...]))

    return _tok_call("ffn_bwd_b", body, [dr, dgate_b, dup_b], [(wg, layer), (wu, layer)], [_sds((S, D))], dep=dep)[0]


def mix_bwd(dx1, xh, rs, g, w):
    S, D = dx1.shape

    def body(d_ref, xh_ref, rs_ref, g_ref, w_ref, dr_ref, dmb_ref, dl_ref, dlg_ref, dlb_ref):
        d, xh = d_ref[...], xh_ref[...]
        dr = _ln_bwd(d, xh, rs_ref[...], g_ref[...])
        drb = dr.astype(BF16)
        dr_ref[...] = dr
        dmb_ref[...] = drb
        dl_ref[...] = _dot_nt(drb, w_ref[...])
        _acc(dlg_ref, _colsum(d * xh))
        _acc(dlb_ref, _colsum(d))

    return _tok_call("mix_bwd", body, [dx1, xh, rs], [g, w],
                     [_sds((S, D)), _sds((S, D), BF16), _sds((S, D))], [_sds((1, D)), _sds((1, D))])


def dx_in(dr, pieces, w):
    S, D = dr.shape
    offs = [o for _, o in pieces]
    widths = [a.shape[1] for a, _ in pieces]

    def body(*refs):
        dr_ref, prefs, w_ref, dx_ref = refs[0], refs[1:1 + len(pieces)], refs[-2], refs[-1]
        acc = ALPHA * dr_ref[...]
        for pr, o, n in zip(prefs, offs, widths):
            acc = acc + _dot_nt(pr[...], w_ref[:, o:o + n])
        dx_ref[...] = acc

    return _tok_call("dx_in", body, [dr] + [a for a, _ in pieces], [w], [_sds((S, D))])[0]


def odd_post_bwd(dl, h, xhc, rsc, xhv, rsv, sv, cl_g, cl_b, sl_g, sl_b, sg_w, tm=256, dep=None):
    S = dl.shape[0]
    tm = min(tm, S)

    def body(dl_ref, zc_ref, xhc_ref, rsc_ref, xhv_ref, rsv_ref, sv_ref, clg, clb, slg, slb, w_ref,
             dy_ref, dzc_ref, dclg_ref, dclb_ref, dslg_ref, dslb_ref, dwm_ref, dsb_ref, dvn_ref):
        first = pl.program_id(0) == 0
        last = pl.program_id(0) == pl.num_programs(0) - 1
        dc, dd = dl_ref[:, 0:512], dl_ref[:, 512:1024]
        xhc = xhc_ref[...]
        lnc = xhc * clg[...] + clb[...]
        s = _sigmoid(lnc)
        dlnc = dc * s * (1.0 + lnc * (1.0 - s))
        dy_ref[...] = _ln_bwd(dlnc, xhc, rsc_ref[...], clg[...])
        _acc(dclg_ref, _colsum(dlnc * xhc))
        _acc(dclb_ref, _colsum(dlnc))
        zc = zc_ref[...]
        z = _gelu(zc)
        dsv = dd * z[:, :512]
        dsvb = dsv.astype(BF16)
        xhv = xhv_ref[...]
        vnb = (xhv * slg[...] + slb[...]).astype(BF16)

        @pl.when(first)
        def _():
            dwm_ref[...] = jnp.zeros_like(dwm_ref)
            dsb_ref[...] = jnp.zeros_like(dsb_ref)

        for g in range(4):
            wm = _masked_sg_w(w_ref, g)
            for ch in range(tm // 128):
                rs_, cs = slice(ch * 128, (ch + 1) * 128), slice(g * 128, (g + 1) * 128)
                dwm_ref[g] += _dot_nt(dsvb[rs_, cs], vnb[rs_, cs])
                dvn_ref[rs_, cs] = _dot_tn(wm, dsvb[rs_, cs])
                dsb_ref[g] += dsv[rs_, cs]
        dvn = dvn_ref[...]
        dvv = _ln_bwd(dvn, xhv, rsv_ref[...], slg[...])
        _acc(dslg_ref, _colsum(dvn * xhv))
        _acc(dslb_ref, _colsum(dvn))
        gg = _gelu_grad(zc)
        dzc_ref[:, 0:512] = (dd * sv_ref[...] * gg[:, :512]).astype(BF16)
        dzc_ref[:, 512:1024] = (dvv * gg[:, 512:]).astype(BF16)

        @pl.when(last)
        def _():
            row = lax.broadcasted_iota(jnp.int32, (128, 128), 0)
            col = lax.broadcasted_iota(jnp.int32, (128, 128), 1)
            for g in range(4):
                dwm_ref[g] = jnp.where(row >= col, dwm_ref[g], 0.0)
                dsb_ref[g] = jnp.broadcast_to(jnp.sum(dsb_ref[g], axis=1, keepdims=True), (128, 128))

    return _tok_call(
        "odd_post_bwd", body, [dl, (h, 1024, 1), xhc, rsc, xhv, rsv, sv], [cl_g, cl_b, sl_g, sl_b, sg_w],
        [_sds((S, 512)), _sds((S, 1024), BF16)],
        [_sds((1, 512)), _sds((1, 512)), _sds((1, 512)), _sds((1, 512)), _sds((4, 128, 128)), _sds((4, 128, 128))],
        tm=tm, scratch=[pltpu.VMEM((tm, 512), F32)], dep=dep)


def conv_bwd(dy, hc, h, dw, CH=128):
    S = dy.shape[0]

    def body(dy_ref, hc_ref, a_ref, g_ref, dw_ref, da_ref, dg_ref, ddw_ref, padh_ref, padd_ref, dhc_ref):
        padh_ref[0:32, :] = jnp.zeros((32, 128), F32)
        padh_ref[32:32 + S, :] = hc_ref[...]
        padd_ref[0:S, :] = dy_ref[...]
        padd_ref[S:S + 32, :] = jnp.zeros((32, 128), F32)
        taps = [jnp.zeros((1, 128), F32) for _ in range(CONV_TAPS)]
        for ch in range(S // CH):
            b0 = ch * CH
            dyc = padd_ref[b0:b0 + CH, :]
            acc = dw_ref[0:1, :] * padd_ref[b0 + 30:b0 + 30 + CH, :]
            taps[0] = taps[0] + _colsum(dyc * padh_ref[b0 + 2:b0 + 2 + CH, :])
            for k in range(1, CONV_TAPS):
                acc = acc + dw_ref[k:k + 1, :] * padd_ref[b0 + 30 - k:b0 + 30 - k + CH, :]
                taps[k] = taps[k] + _colsum(dyc * padh_ref[b0 + 2 + k:b0 + 2 + k + CH, :])
            dhc_ref[b0:b0 + CH, :] = acc
        for k in range(CONV_TAPS):
            ddw_ref[k:k + 1, :] = taps[k]
        dhc = dhc_ref[...]
        s = _sigmoid(g_ref[...])
        da_ref[...] = (dhc * s).astype(BF16)
        dg_ref[...] = (dhc * a_ref[...] * s * (1.0 - s)).astype(BF16)

    return pl.pallas_call(
        body, name="conv_bwd", grid=(4,),
        in_specs=[pl.BlockSpec((S, 128), lambda c: (0, c)),
                  pl.BlockSpec((S, 128), lambda c: (0, c)),
                  pl.BlockSpec((S, 128), lambda c: (0, c)),
                  pl.BlockSpec((S, 128), lambda c: (0, 4 + c)),
                  pl.BlockSpec((CONV_TAPS, 128), lambda c: (0, c))],
        out_specs=[pl.BlockSpec((S, 128), lambda c: (0, c)), pl.BlockSpec((S, 128), lambda c: (0, c)),
                   pl.BlockSpec((CONV_TAPS, 128), lambda c: (0, c))],
        out_shape=[_sds((S, 512), BF16), _sds((S, 512), BF16), _sds((CONV_TAPS, 512))],
        scratch_shapes=[pltpu.VMEM((S + 32, 128), F32), pltpu.VMEM((S + 32, 128), F32), pltpu.VMEM((S, 128), F32)],
        compiler_params=_cp(("arbitrary",)),
    )(dy, hc, h, h, dw)


def attn_bwd(qkv, dl, tb, T=256, dep=None):
    S = qkv.shape[0]
    T = min(T, S)
    nq = S // T

    def body(q_ref, k_ref, v_ref, do_ref, t_ref, dq_ref, dk_ref, dv_ref,
             dka_ref, dva_ref, dqa_ref, pc_ref, gc_ref, qh_ref, doh_ref):
        i = pl.program_id(0)
        hm0 = lax.broadcasted_iota(jnp.int32, (1, 128), 1) < 64
        r2 = lax.broadcasted_iota(jnp.int32, (2 * T, T), 0)
        c2 = lax.broadcasted_iota(jnp.int32, (2 * T, T), 1)
        causal = c2 < jnp.where(r2 >= T, r2 - T, r2)
        ur = lax.broadcasted_iota(jnp.int32, (T, T), 0)
        uc = lax.broadcasted_iota(jnp.int32, (T, T), 1)
        u_le = (ur <= uc).astype(BF16)
        u_lt = (ur < uc).astype(BF16)

        @pl.when(i == 0)
        def _():
            dka_ref[...] = jnp.zeros_like(dka_ref)
            dva_ref[...] = jnp.zeros_like(dva_ref)

        dqa_ref[...] = jnp.zeros_like(dqa_ref)
        gc_ref[...] = jnp.zeros_like(gc_ref)
        for pp in range(4):
            cs = slice(pp * 128, (pp + 1) * 128)
            qh_ref[pp] = _stack_heads(q_ref[:, cs] * QK_SCALE, hm0)
            doh_ref[pp] = _stack_heads(do_ref[:, cs], hm0)
            for hd in range(2):
                for half in range(T // 128):
                    pc_ref[pp, hd * T:(hd + 1) * T, half * 128:(half + 1) * 128] = t_ref[2 * pp + hd]

        def block(kb, diag):
            ks = pl.multiple_of(kb * T, T)
            for pp in range(4):
                cs = slice(pp * 128, (pp + 1) * 128)
                kb16 = k_ref[pl.ds(ks, T), cs]
                vb16 = v_ref[pl.ds(ks, T), cs]
                qh, doh = qh_ref[pp], doh_ref[pp]
                z = _dot_nt(qh, kb16)
                sp = _softplus(z)
                a = z - sp
                sig = jnp.exp(a)
                if diag:
                    sp = jnp.where(causal, sp, 0.0)
                pre = _cumsum_mm(sp, u_le)
                rem = pc_ref[pp]
                w = jnp.exp(a - rem + pre)
                if diag:
                    w = jnp.where(causal, w, 0.0)
                gmat = _dot_nt(doh, vb16) * w
                gex = gc_ref[pp] + _cumsum_mm(gmat, u_lt)
                dz = gmat * (1.0 - sig) - sig * gex
                if diag:
                    dz = jnp.where(causal, dz, 0.0)
                dzb = dz.astype(BF16)
                dqa_ref[:, cs] += _dot(_unstack_k(dzb, T), _stack_heads(kb16, hm0))
                dka_ref[pl.ds(ks, T), cs] += _dot_tn(dzb, qh)
                dva_ref[pl.ds(ks, T), cs] += _dot_tn(w.astype(BF16), doh)
                pc_ref[pp] = rem - jnp.broadcast_to(pre[:, T - 1:T], (2 * T, T))
                gc_ref[pp] = jnp.broadcast_to(gex[:, T - 1:T] + gmat[:, T - 1:T], (2 * T, T))

        def step(kb, carry):
            block(kb, False)
            return carry

        lax.fori_loop(0, i, step, 0)
        block(i, True)
        dq_ref[...] = (dqa_ref[...] * QK_SCALE).astype(BF16)

        @pl.when(i == nq - 1)
        def _():
            dk_ref[...] = dka_ref[...].astype(BF16)
            dv_ref[...] = dva_ref[...].astype(BF16)

    deps = [] if dep is None else [dep]
    call_body = body if dep is None else (lambda *refs: body(*refs[:5], *refs[6:]))
    return pl.pallas_call(
        call_body, name="attn_bwd", grid=(nq,),
        in_specs=[pl.BlockSpec((T, 512), lambda i: (i, 0)),
                  pl.BlockSpec((S, 512), lambda i: (0, 1)),
                  pl.BlockSpec((S, 512), lambda i: (0, 2)),
                  pl.BlockSpec((T, 512), lambda i: (i, 0)),
                  pl.BlockSpec((8, T, 128), lambda i: (0, i, 0))] + [ANY] * len(deps),
        out_specs=[pl.BlockSpec((T, 512), lambda i: (i, 0)),
                   pl.BlockSpec((S, 512), lambda i: (0, 0)),
                   pl.BlockSpec((S, 512), lambda i: (0, 0))],
        out_shape=[_sds((S, 512), BF16), _sds((S, 512), BF16), _sds((S, 512), BF16)],
        scratch_shapes=[pltpu.VMEM((S, 512), F32), pltpu.VMEM((S, 512), F32), pltpu.VMEM((T, 512), F32),
                        pltpu.VMEM((4, 2 * T, T), F32), pltpu.VMEM((4, 2 * T, T), F32),
                        pltpu.VMEM((4, 2 * T, 128), BF16), pltpu.VMEM((4, 2 * T, 128), BF16)],
        compiler_params=_cp(("arbitrary",)),
    )(qkv, qkv, qkv, dl, tb, *deps)


def pool_bwd(dl, pooled_b, pool_w, pool_scale, CH=256):
    S = dl.shape[0]
    CH = min(CH, S)

    def body(db_ref, pooled_ref, w_ref, sc_ref, du_ref, dw_ref, dsc_ref, pad_ref, dp_ref):
        pad_ref[S:S + 16, :] = jnp.zeros((16, 128), F32)
        for g, win in enumerate(POOL_WINDOWS):
            cs = slice(g * 128, (g + 1) * 128)
            wq = w_ref[g].astype(BF16)
            dwg = jnp.zeros((128, 128), F32)
            dsc = jnp.zeros((1, 128), F32)
            for ch in range(S // CH):
                rs_ = slice(ch * CH, (ch + 1) * CH)
                db = db_ref[rs_, cs]
                pb = pooled_ref[rs_, cs]
                dsc = dsc + _colsum(db * _dot(pb, wq))
                dmsb = (db * sc_ref[:, cs]).astype(BF16)
                dwg = dwg + _dot_tn(pb, dmsb)
                dpool = _dot_nt(dmsb, wq)
                t = ch * CH + lax.broadcasted_iota(jnp.int32, (CH, 1), 0)
                cnt = jnp.minimum(t + 1, win).astype(F32)
                dp_ref[rs_, :] = dpool
                pad_ref[rs_, :] = dpool / cnt
            dw_ref[g] = dwg
            dsc_ref[:, cs] = dsc
            for ch in range(S // CH):
                base = ch * CH
                acc = pad_ref[base:base + CH, :]
                for sft in range(1, win):
                    acc = acc + pad_ref[base + sft:base + sft + CH, :]
                du_ref[base:base + CH, cs] = (acc - dp_ref[base:base + CH, :]).astype(BF16)

    return pl.pallas_call(
        body, name="pool_bwd", grid=(1,),
        in_specs=[pl.BlockSpec((S, 512), lambda i: (0, 1)),
                  pl.BlockSpec((S, 512), lambda i: (0, 0)),
                  pl.BlockSpec((4, 128, 128), lambda i: (0, 0, 0)),
                  pl.BlockSpec((1, 512), lambda i: (0, 0))],
        out_specs=[pl.BlockSpec((S, 512), lambda i: (0, 0)),
                   pl.BlockSpec((4, 128, 128), lambda i: (0, 0, 0)),
                   pl.BlockSpec((1, 512), lambda i: (0, 0))],
        out_shape=[_sds((S, 512), BF16), _sds((4, 128, 128)), _sds((1, 512))],
        scratch_shapes=[pltpu.VMEM((S + 16, 128), F32), pltpu.VMEM((S, 128), F32)],
        compiler_params=_cp(("arbitrary",)),
    )(dl, pooled_b, pool_w, pool_scale)


def tn_into(a, b, out, r0, c0, tk=1024, tn=512):
    S, K = a.shape
    N = b.shape[1]
    tk, tn = min(tk, K), min(tn, N)
    assert K % tk == 0 and N % tn == 0 and r0 % tk == 0 and c0 % tn == 0
    rb, cb = r0 // tk, c0 // tn
    fresh = isinstance(out, jax.ShapeDtypeStruct)

    def body(*refs):
        a_ref, b_ref, o_ref = refs[0], refs[1], refs[-1]
        o_ref[...] = _dot_tn(a_ref[...], b_ref[...]).astype(BF16)

    ospec = pl.BlockSpec((tk, tn), lambda i, j: (rb + i, cb + j))
    in_specs = [pl.BlockSpec((S, tk), lambda i, j: (0, i)), pl.BlockSpec((S, tn), lambda i, j: (0, j))]
    args = [a, b]
    aliases = {}
    if not fresh:
        in_specs += [pl.BlockSpec(memory_space=pl.ANY)]
        args += [out]
        aliases = {2: 0}
    return pl.pallas_call(
        body, name="tn_grad", grid=(K // tk, N // tn),
        in_specs=in_specs, out_specs=[ospec],
        out_shape=[_sds(out.shape, BF16)],
        input_output_aliases=aliases,
        compiler_params=_cp(("arbitrary", "arbitrary")),
    )(*args)[0]


def _row(a, i):
    return a[i:i + 1]


MIXER_NAMES = (("even_w_in", "even_w_out"), ("odd_w_in", "odd_w_out"))


def _tn_group(items):
    out = {}
    for name, (shape, parts) in items.items():
        g = _sds(shape, BF16)
        for a, b, r0, c0 in parts:
            g = tn_into(a, b, g, r0, c0)
        out[name] = g
    return out


def fwd_layer(i, xin, p_i, target, comm):
    s = {}
    W = comm.weights(("mix", i), xin)
    w_in = W[MIXER_NAMES[i][0]]
    if i == 0:
        s["h"], s["xb"], s["qkv"] = mm_in(xin, w_in, nb16=1536)
        comm.poke(("in", i), s["h"])
        s["l1"], s["tb"] = attn_fwd(s["qkv"])
        s["l2"], s["pooled"] = pool_fwd(s["h"], W["pool_w"], W["pool_scale"])
    else:
        s["h"], s["xb"] = mm_in(xin, w_in)
        comm.poke(("in", i), s["h"])
        s["y"], s["hc"] = conv_fwd(s["h"], W["conv_dw"])
        sgb_bc = jnp.broadcast_to(W["sg_b"][:, :, None], (4, 128, 128))
        (s["l1"], s["l2"], s["xhc"], s["rsc"], s["xhv"], s["rsv"], s["sv"]) = odd_post(
            s["y"], s["h"], W["conv_ln_g"], W["conv_ln_b"], W["sg_ln_g"], W["sg_ln_b"], W["sg_w"], sgb_bc)
    tok = comm.poke(("mixed", i), s["l2"])
    W = comm.weights(("out", i), s["l2"])
    x1, s["xh1"], s["rs1"] = mm_out_ln(s["l1"], s["l2"], xin, W[MIXER_NAMES[i][1]], _row(W["ln_mix_g"], i),
                                       _row(W["ln_mix_b"], i), dep=tok)
    W = comm.weights(("ffn", i), x1)
    s["gate"], s["up"], s["hb"], s["x1b"] = ffn_up(x1, W["ffn_w_gate%d" % i], W["ffn_w_up%d" % i], None)
    x2, s["xh2"], s["rs2"] = ffn_down_ln(s["hb"], x1, W["ffn_w_down%d" % i], None,
                                         _row(W["ln_ffn_g"], i), _row(W["ln_ffn_b"], i))
    tok = comm.poke(("ffn", i), x2)
    outs = ple_fwd(x2, p_i, W["ple_w_gate%d" % i], W["ple_w_proj%d" % i], None, _row(W["ple_b_gate"], i), target,
                   dep=tok)
    s["sg"], s["pp"], s["x2b"], s["pb"] = outs[1:5]
    return outs[0], s, outs[5:]


def bwd_layer(i, dx, s, W, comm, tok=None):
    small = {}
    D = dx.shape[1]
    FP = W["ffn_w_gate%d" % i].shape[1]
    dx2, dgp_b, dpp_b, small["ple_b_gate"] = ple_bwd(dx, s["sg"], s["pp"], W["ple_w_gate%d" % i], None, dep=tok)
    dr2, dr2_b, dgate_b, dup_b, small["ln_ffn_g"], small["ln_ffn_b"] = ffn_bwd_a(
        dx2, s["xh2"], s["rs2"], _row(W["ln_ffn_g"], i), s["gate"], s["up"], W["ffn_w_down%d" % i], None)
    tok = comm.grads(_tn_group({
        "ple_w_gate%d" % i: ((D, D), [(s["x2b"], dgp_b, 0, 0)]),
        "ple_w_proj%d" % i: ((s["pb"].shape[1], D), [(s["pb"], dpp_b, 0, 0)]),
        "ffn_w_down%d" % i: ((FP, D), [(s["hb"], dr2_b, 0, 0)]),
        "ffn_w_gate%d" % i: ((D, FP), [(s["x1b"], dgate_b, 0, 0)]),
        "ffn_w_up%d" % i: ((D, FP), [(s["x1b"], dup_b, 0, 0)])}))
    dx1 = ffn_bwd_b(dr2, dgate_b, dup_b, W["ffn_w_gate%d" % i], W["ffn_w_up%d" % i], None, dep=tok)
    iname, oname = MIXER_NAMES[i]
    dr1, dmix_b, dl, small["ln_mix_g"], small["ln_mix_b"] = mix_bwd(
        dx1, s["xh1"], s["rs1"], _row(W["ln_mix_g"], i), W[oname])
    tok = comm.poke(("bwd", i), dl)
    if i == 1:
        (dy, dzc_b, small["conv_ln_g"], small["conv_ln_b"], small["sg_ln_g"], small["sg_ln_b"],
         small["sg_w"], dsb) = odd_post_bwd(dl, s["h"], s["xhc"], s["rsc"], s["xhv"], s["rsv"], s["sv"],
                                            W["conv_ln_g"], W["conv_ln_b"], W["sg_ln_g"], W["sg_ln_b"], W["sg_w"],
                                            dep=tok)
        small["sg_b"] = dsb[:, :, 0]
        da_b, dg_b, small["conv_dw"] = conv_bwd(dy, s["hc"], s["h"], W["conv_dw"])
        pieces = [(da_b, 0), (dg_b, 512), (dzc_b, 1024)]
    else:
        dq_b, dk_b, dv_b = attn_bwd(s["qkv"], dl, s["tb"], dep=tok)
        du_b, small["pool_w"], small["pool_scale"] = pool_bwd(dl, s["pooled"], W["pool_w"], W["pool_scale"])
        pieces = [(dq_b, 0), (dk_b, 512), (dv_b, 1024), (du_b, 1536)]
    dxin = dx_in(dr1, pieces, W[iname])
    tok = comm.grads(_tn_group({
        oname: ((1024, D), [(s["l1"], dmix_b, 0, 0), (s["l2"], dmix_b, 512, 0)]),
        iname: ((D, 2048), [(s["xb"], a, 0, off) for a, off in pieces])}))
    return dxin, small, tok


def run_layers(x, p, target, comm):
    saved, xin = [], x
    for i in range(2):
        xin, s, extra = fwd_layer(i, xin, p[i], target if i == 1 else None, comm)
        saved.append(s)
    dx, sq = extra
    W = comm.all_weights()
    per_layer = [None, None]
    tok = None
    for i in (1, 0):
        dx, per_layer[i], tok = bwd_layer(i, dx, saved[i], W, comm, tok)
    small = {}
    for k in ("ln_mix_g", "ln_mix_b", "ln_ffn_g", "ln_ffn_b", "ple_b_gate"):
        small[k] = jnp.concatenate([per_layer[0][k], per_layer[1][k]], axis=0)
    for i in range(2):
        small.update({k: v for k, v in per_layer[i].items() if k not in small})
    return sq, dx, small


def _big_table():
    t = {}
    for nm in ("even", "odd"):
        t[nm + "_w_in"] = ((1024, 2048), 1, 256, 256, nm + "_w_in", 0)
        t[nm + "_w_out"] = ((1024, 1024), 0, 128, 128, nm + "_w_out", 0)
    for l in range(2):
        t["ffn_w_gate%d" % l] = ((1024, 8 * FF_PAD), 1, FF_PAD, FF_SHARD, "ffn_w_gate", l)
        t["ffn_w_up%d" % l] = ((1024, 8 * FF_PAD), 1, FF_PAD, FF_SHARD, "ffn_w_up", l)
        t["ffn_w_down%d" % l] = ((8 * FF_PAD, 1024), 0, FF_PAD, FF_SHARD, "ffn_w_down", l)
        t["ple_w_gate%d" % l] = ((1024, 1024), 0, 128, 128, "ple_w_gate", l)
        t["ple_w_proj%d" % l] = ((256, 1024), 1, 128, 128, "ple_w_proj", l)
    return t


BIG = _big_table()
TRANSPOSED_ARGS = ("ffn_w_gate", "ffn_w_up")
SMALL_SPEC = ((N_DEV, 40, 64), 0, 1, 1)
_LAYER_GROUP = lambda l: ["ffn_w_gate%d" % l, "ffn_w_up%d" % l, "ffn_w_down%d" % l, "ple_w_gate%d" % l, "ple_w_proj%d" % l]
AG_GROUPS = (["even_w_in"], ["even_w_out"], _LAYER_GROUP(0), ["odd_w_in", "odd_w_out", "small"], _LAYER_GROUP(1))
AG_NEED = {("mix", 0): 0, ("out", 0): 1, ("ffn", 0): 2, ("mix", 1): 3, ("ffn", 1): 4}
AG_PASS = {("in", 0): 1, ("mixed", 0): 2, ("ffn", 0): 3, ("mixed", 1): 4}
ANY = pl.BlockSpec(memory_space=pl.ANY)
SEM = pl.BlockSpec(memory_space=pltpu.SEMAPHORE)


def _spec(name):
    return SMALL_SPEC if name == "small" else BIG[name]


def _win_shape(spec):
    full, axis, w = spec[:3]
    return tuple(w if d == axis else n for d, n in enumerate(full))


def _window(ref, axis, w, j):
    idx = [slice(None)] * len(ref.shape)
    idx[axis] = pl.ds(j, 1) if w == 1 else pl.ds(pl.multiple_of(j * w, w), w)
    return ref.at[tuple(idx)]


def _mesh_pos():
    return lax.axis_index("x"), lax.axis_index("y"), lax.axis_index("c")


def split_call(name, arrays, starts=(), waits=(), sems_in=(), new=(), after=None):
    n, nn, ns = len(arrays), len(new), len(starts)
    flat_sems = [s for pair in sems_in for s in pair]

    def body(*refs):
        arr = list(refs[:n])
        sin = refs[n:n + len(flat_sems)]
        outs = refs[n + len(flat_sems) + (after is not None):]
        data = arr + list(outs[n:n + nn])
        for p, k, kind, mk in waits:
            d = mk(data, sin[2 * p].at[k], sin[2 * p + 1].at[k])
            d.wait_send() if kind == "send" else d.wait_recv()
        if ns:
            send, recv = outs[n + nn], outs[n + nn + 1]
            for k, mk in enumerate(starts):
                mk(data, send.at[k], recv.at[k]).start()
        outs[-1][...] = jnp.zeros((8, 128), F32)

    sem_out = [pltpu.SemaphoreType.DMA((ns,)), pltpu.SemaphoreType.DMA((ns,))] if ns else []
    res = pl.pallas_call(
        body, name=name,
        in_specs=[ANY] * n + [SEM] * len(flat_sems) + ([ANY] if after is not None else []),
        out_specs=[ANY] * (n + nn) + [SEM] * len(sem_out) + [pl.BlockSpec(memory_space=pltpu.VMEM)],
        out_shape=[_sds(a.shape, a.dtype) for a in arrays] + list(new) + sem_out + [_sds((8, 128), F32)],
        input_output_aliases={a: a for a in range(n)},
        compiler_params=pltpu.CompilerParams(has_side_effects=pltpu.SideEffectType.DATAFLOW_SIDE_EFFECTING),
    )(*arrays, *flat_sems, *([after] if after is not None else []))
    return list(res[:n + nn]), (tuple(res[n + nn:n + nn + 2]) if ns else None), res[-1]


def _remote(src, dst, send_sem, recv_sem, dev):
    return pltpu.make_async_remote_copy(src_ref=src, dst_ref=dst, send_sem=send_sem, recv_sem=recv_sem,
                                        device_id=dev, device_id_type=MESH_T)


class Gatherer:
    def __init__(self, groups, arrays, specs, prefix):
        self.groups, self.specs, self.prefix = groups, specs, prefix
        self.names = [nm for g in groups for nm in g]
        self.arr = dict(zip(self.names, arrays))
        self.fwd_sems = {}
        self.forwarded = set()

    @staticmethod
    def _mk_first(ai, spec, k):
        def mk(refs, ss, rs):
            x, y, c = _mesh_pos()
            dev = [(x, y, 1 - c), (1 - x, y, c), (x, 1 - y, c), (1 - x, 1 - y, c)][k]
            win = _window(refs[ai], spec[1], spec[2], 4 * x + 2 * y + c)
            return _remote(win, win, ss, rs, dev)
        return mk

    @staticmethod
    def _mk_fwd(ai, spec, j):
        def mk(refs, ss, rs):
            x, y, c = _mesh_pos()
            px, py = [(1 - x, y), (x, 1 - y), (1 - x, 1 - y)][j]
            win = _window(refs[ai], spec[1], spec[2], 4 * px + 2 * py + c)
            return _remote(win, win, ss, rs, (x, y, 1 - c))
        return mk

    def start(self, after=None):
        starts = [self._mk_first(ai, self.specs[nm], k) for ai, nm in enumerate(self.names) for k in range(4)]
        arrs, self.first_sems, tok = split_call(self.prefix + "_start", [self.arr[nm] for nm in self.names],
                                                starts=starts, after=after)
        self.arr = dict(zip(self.names, arrs))
        return tok

    def forward(self, g, after=None):
        if g in self.forwarded:
            return None
        self.forwarded.add(g)
        names = self.groups[g]
        waits = [(0, 4 * self.names.index(nm) + 1 + j, "recv", self._mk_fwd(ai, self.specs[nm], j))
                 for ai, nm in enumerate(names) for j in range(3)]
        starts = [self._mk_fwd(ai, self.specs[nm], j) for ai, nm in enumerate(names) for j in range(3)]
        arrs, self.fwd_sems[g], tok = split_call(
            "%s_forward%d" % (self.prefix, g), [self.arr[nm] for nm in names], starts=starts, waits=waits,
            sems_in=[self.first_sems], after=after)
        self.arr.update(zip(names, arrs))
        return tok

    def finish(self, g, after=None):
        self.forward(g, after)
        names = self.groups[g]
        waits = []
        for ai, nm in enumerate(names):
            base = 4 * self.names.index(nm)
            waits.append((0, base, "recv", self._mk_first(ai, self.specs[nm], 0)))
            waits += [(1, 3 * ai + j, "recv", self._mk_fwd(ai, self.specs[nm], j)) for j in range(3)]
            waits += [(0, base + k, "send", self._mk_first(ai, self.specs[nm], k)) for k in range(4)]
            waits += [(1, 3 * ai + j, "send", self._mk_fwd(ai, self.specs[nm], j)) for j in range(3)]
        arrs, _, _ = split_call(
            "%s_finish%d" % (self.prefix, g), [self.arr[nm] for nm in names], waits=waits,
            sems_in=[self.first_sems, self.fwd_sems[g]], after=after)
        self.arr.update(zip(names, arrs))
        return {nm: self.arr[nm] for nm in names}


class Reducer:
    def __init__(self, cq_arr, adam):
        self.cq_arr, self.adam = cq_arr, adam
        self.groups = []
        self.n = 0
        self.last = None

    @staticmethod
    def _mk1(gi, li, spec, q):
        def mk(refs, ss, rs):
            x, y, c = _mesh_pos()
            return _remote(_window(refs[gi], spec[1], spec[2], 2 * q + (1 - c)), refs[li].at[q], ss, rs, (x, y, 1 - c))
        return mk

    @staticmethod
    def _mk2(si, li, d):
        def mk(refs, ss, rs):
            x, y, c = _mesh_pos()
            qd = lax.rem(2 * x + y + d, 4)
            return _remote(refs[si].at[d - 1], refs[li].at[3 - d], ss, rs, (lax.div(qd, 2), lax.rem(qd, 2), c))
        return mk

    def add(self, grads, after=None):
        names = list(grads)
        m = len(names)
        starts = [self._mk1(ai, m + ai, BIG[nm], q) for ai, nm in enumerate(names) for q in range(4)]
        new = [_sds((4,) + _win_shape(BIG[nm]), BF16) for nm in names]
        res, sems, tok = split_call("rs1_start%d" % self.n, [grads[nm] for nm in names], starts=starts, new=new,
                                    after=after)
        self.groups.append(dict(names=names, starts=starts, buf=res, sems=sems, stage=1, idx=self.n))
        self.n += 1
        return tok

    def step(self, after):
        tok = None
        for grp in self.groups:
            names, m = grp["names"], len(grp["names"])
            if grp["stage"] == 1:
                waits = [(0, k, kind, mk) for k, mk in enumerate(grp["starts"]) for kind in ("send", "recv")]
                res, _, _ = split_call("rs1_wait%d" % grp["idx"], grp["buf"], waits=waits, sems_in=[grp["sems"]], after=after)
                full, land1 = res[:m], res[m:]
                s1b = [add_pairs(g, l, BIG[nm], self.cq_arr) for nm, g, l in zip(names, full, land1)]
                starts = [self._mk2(ai, m + ai, d) for ai in range(m) for d in (1, 2, 3)]
                new = [_sds(a.shape, BF16) for a in s1b]
                res, sems, tok = split_call("rs2_start%d" % grp["idx"], s1b, starts=starts, new=new, after=tok)
                grp.update(stage=2, g=full, land1=land1, starts=starts, buf=res, sems=sems)
        return tok

    def finish_oldest(self):
        for grp in self.groups:
            if grp["stage"] == 2:
                names, m = grp["names"], len(grp["names"])
                waits = [(0, k, kind, mk) for k, mk in enumerate(grp["starts"]) for kind in ("send", "recv")]
                res, _, _ = split_call("rs2_wait%d" % grp["idx"], grp["buf"], waits=waits, sems_in=[grp["sems"]],
                                       after=self.last)
                for nm, g, l1, l2 in zip(names, grp["g"], grp["land1"], res[m:]):
                    self.last = self.adam(nm, g, l1, l2, self.last)
                grp["stage"] = 3
                return True
        return False


def pack_weights(args, arg_names, small_blk, names, j_arr):
    n_in = len(args)

    def body(j_ref, *refs):
        for o, nm in enumerate(names):
            dst = refs[n_in + 1 + o]
            if nm == "small":
                dst[...] = refs[n_in][...]
                continue
            _, axis, w, valid, arg, layer = BIG[nm]
            if arg in TRANSPOSED_ARGS:
                s = refs[arg_names.index(arg)][layer]
                s = jnp.concatenate([s, jnp.zeros((w - valid, s.shape[1]), F32)], axis=0)
                dst[...] = s.T.astype(BF16)
                continue
            src = refs[arg_names.index(arg)][layer].astype(BF16)
            if valid == w:
                dst[...] = src
            else:
                dst[...] = jnp.zeros(dst.shape, BF16)
                if axis == 1:
                    dst[:, 0:valid] = src
                else:
                    dst[0:valid, :] = src

    def ispec(a):
        return pl.BlockSpec(a.shape, lambda i, j_ref: (0, 0, 0))

    def ospec(spec):
        axis, nd = spec[1], len(spec[0])
        return pl.BlockSpec(_win_shape(spec),
                            lambda i, j_ref, axis=axis, nd=nd: tuple(j_ref[0] if d == axis else 0 for d in range(nd)))

    specs = [_spec(nm) for nm in names]
    return pl.pallas_call(
        body, name="pack_weights",
        grid_spec=pltpu.PrefetchScalarGridSpec(
            num_scalar_prefetch=1, grid=(1,),
            in_specs=[ispec(a) for a in list(args) + [small_blk]], out_specs=[ospec(s) for s in specs]),
        out_shape=[_sds(s[0], F32 if nm == "small" else BF16) for nm, s in zip(names, specs)],
        compiler_params=_cp(("arbitrary",)),
    )(j_arr, *args, small_blk)


def add_pairs(full, land, spec, cq_arr):
    axis, w = spec[1], spec[2]
    R, C = full.shape

    def chip(d, cq):
        return lax.rem(cq[1] + d + 1, 4)

    if axis == 1:
        tr = R
        grid = (3, R // tr)
        fspec = pl.BlockSpec((tr, w), lambda d, i, cq: (i, 2 * chip(d, cq) + cq[0]))
        lspec = pl.BlockSpec((None, tr, w), lambda d, i, cq: (chip(d, cq), i, 0))
        ospec = pl.BlockSpec((None, tr, w), lambda d, i, cq: (d, i, 0))
    else:
        grid = (3, 1)
        fspec = pl.BlockSpec((w, C), lambda d, i, cq: (2 * chip(d, cq) + cq[0], 0))
        lspec = pl.BlockSpec((None, w, C), lambda d, i, cq: (chip(d, cq), 0, 0))
        ospec = pl.BlockSpec((None, w, C), lambda d, i, cq: (d, 0, 0))

    def body(cq_ref, a_ref, b_ref, ob_ref):
        ob_ref[...] = (a_ref[...].astype(F32) + b_ref[...].astype(F32)).astype(BF16)

    return pl.pallas_call(
        body, name="add_pairs",
        grid_spec=pltpu.PrefetchScalarGridSpec(
            num_scalar_prefetch=1, grid=grid, in_specs=[fspec, lspec], out_specs=[ospec]),
        out_shape=[_sds((3,) + land.shape[1:], BF16)],
        compiler_params=_cp(("arbitrary",) * 2),
    )(cq_arr, full, land)[0]


def _adamw(w, g, m, v):
    m = ADAM_B1 * m + (1.0 - ADAM_B1) * g
    v = ADAM_B2 * v + (1.0 - ADAM_B2) * (g * g)
    m_hat = m / (1.0 - ADAM_B1 ** ADAM_STEP)
    v_hat = v / (1.0 - ADAM_B2 ** ADAM_STEP)
    delta = -ADAM_LR * (m_hat / (jnp.sqrt(v_hat) + ADAM_EPS) + ADAM_WD * w)
    return delta, m, v


def reduce_adamw(full, land1, land, w, m, v, spec, cq_arr, prev=None, dep=None):
    axis, win, valid, layer = spec[1], spec[2], spec[3], spec[5]
    L, R, C = w.shape
    transposed = spec[4] in TRANSPOSED_ARGS
    if transposed:
        grid = (1,)
        fspec = pl.BlockSpec((C, win), lambda i, cq: (0, 2 * cq[1] + cq[0]))
        wspec = pl.BlockSpec((None, C, win), lambda i, cq: (cq[1], 0, 0))
        lspec = pl.BlockSpec((3, C, win), lambda i, cq: (0, 0, 0))
        sspec = pl.BlockSpec((None, R, C), lambda i, cq: (layer, 0, 0))
    elif axis == 1:
        tr = min(1024, R)
        grid = (R // tr,)
        fspec = pl.BlockSpec((tr, win), lambda i, cq: (i, 2 * cq[1] + cq[0]))
        wspec = pl.BlockSpec((None, tr, win), lambda i, cq: (cq[1], i, 0))
        lspec = pl.BlockSpec((3, tr, win), lambda i, cq: (0, i, 0))
        sspec = pl.BlockSpec((None, tr, C), lambda i, cq: (layer, i, 0))
    else:
        grid = (1,)
        fspec = pl.BlockSpec((win, full.shape[1]), lambda i, cq: (2 * cq[1] + cq[0], 0))
        wspec = pl.BlockSpec((None, win, C), lambda i, cq: (cq[1], 0, 0))
        lspec = pl.BlockSpec((3, win, C), lambda i, cq: (0, 0, 0))
        sspec = pl.BlockSpec((None, R, C), lambda i, cq: (layer, 0, 0))

    def body(cq_ref, full_ref, own_ref, land_ref, w_ref, m_ref, v_ref, *rest):
        g_ref, d_ref, nm_ref, nv_ref = rest[-4:]
        if transposed:
            rd = lambda r, *lead: r[lead] if lead else r[...]
        elif axis == 1:
            rd = lambda r, *lead: r[(*lead, slice(None), slice(0, valid))]
        else:
            rd = lambda r, *lead: r[(*lead, slice(0, valid), slice(None))]
        g = rd(full_ref).astype(F32) + rd(own_ref).astype(F32)
        for k in range(3):
            g = g + rd(land_ref, k).astype(F32)
        if transposed:
            g = g.T[0:valid, :]
        g_ref[...] = g
        d, nm, nv = _adamw(w_ref[...], g, m_ref[...], v_ref[...])
        d_ref[...] = d
        nm_ref[...] = nm
        nv_ref[...] = nv

    extra = (list(prev) if prev is not None else []) + ([dep] if dep is not None else [])
    return pl.pallas_call(
        body, name="reduce_adamw",
        grid_spec=pltpu.PrefetchScalarGridSpec(
            num_scalar_prefetch=1, grid=grid,
            in_specs=[fspec, wspec, lspec, sspec, sspec, sspec] + [ANY] * len(extra), out_specs=[sspec] * 4),
        out_shape=[_sds(w.shape)] * 4,
        input_output_aliases={7 + k: k for k in range(4 if prev is not None else 0)},
        compiler_params=_cp(("arbitrary",)),
    )(cq_arr, full, land1, land, w, m, v, *extra)


def place_slot(packed, j_arr):
    R = packed.shape[0]

    def body(j_ref, src, dst):
        dst[...] = src[...]

    return pl.pallas_call(
        body, name="place_slot",
        grid_spec=pltpu.PrefetchScalarGridSpec(
            num_scalar_prefetch=1, grid=(1,),
            in_specs=[pl.BlockSpec((R, 128), lambda i, j: (0, 0))],
            out_specs=[pl.BlockSpec((None, R, 128), lambda i, j: (j[0], 0, 0))]),
        out_shape=[_sds((N_DEV, R, 128))], compiler_params=_cp(("arbitrary",)),
    )(j_arr, packed)[0]


def sum_slots(gathered):
    def body(g_ref, o_ref):
        g = g_ref[0]
        for dev in range(1, N_DEV):
            g = g + g_ref[dev]
        o_ref[...] = g

    return pl.pallas_call(body, name="sum_slots", out_shape=_sds(gathered.shape[1:]), compiler_params=_cp())(gathered)


def small_adamw(gs, wmv):
    k = len(gs)

    def body(*refs):
        for a in range(k):
            g, w, m, v = refs[4 * a:4 * a + 4]
            d, nm, nv = _adamw(w[...], g[...], m[...], v[...])
            refs[4 * k + 3 * a][...] = d
            refs[4 * k + 3 * a + 1][...] = nm
            refs[4 * k + 3 * a + 2][...] = nv

    args = [t for g, tup in zip(gs, wmv) for t in (g,) + tuple(tup)]
    out_shape = [_sds(g.shape) for g in gs for _ in range(3)]
    return pl.pallas_call(body, name="small_adamw", out_shape=out_shape, compiler_params=_cp())(*args)


WEIGHT_NAMES = ("even_w_in", "even_w_out", "pool_w", "pool_scale", "odd_w_in", "odd_w_out", "conv_dw", "conv_ln_g",
                "conv_ln_b", "sg_ln_g", "sg_ln_b", "sg_w", "sg_b", "ln_mix_g", "ln_mix_b", "ffn_w_gate", "ffn_w_up",
                "ffn_w_down", "ln_ffn_g", "ln_ffn_b", "ple_w_proj", "ple_w_gate", "ple_b_gate")
PACK_ARGS = ("even_w_in", "even_w_out", "odd_w_in", "odd_w_out", "ffn_w_gate", "ffn_w_up", "ffn_w_down",
             "ple_w_gate", "ple_w_proj")
REPLICATED = ("pool_w", "pool_scale", "sg_w", "sg_b", "ln_mix_g", "ln_mix_b", "ln_ffn_g", "ln_ffn_b", "ple_b_gate")
SHARDED_SMALL = ("conv_dw", "conv_ln_g", "conv_ln_b", "sg_ln_g", "sg_ln_b")
NATURAL = {"pool_w": (4, 128, 128), "pool_scale": (1, 512), "sg_w": (4, 128, 128), "sg_b": (4, 128),
           "ln_mix_g": (2, 1024), "ln_mix_b": (2, 1024), "ln_ffn_g": (2, 1024), "ln_ffn_b": (2, 1024),
           "ple_b_gate": (2, 1024)}


def kernel(x, p, even_w_in, even_w_out, pool_w, pool_scale, odd_w_in, odd_w_out, conv_dw, conv_ln_g, conv_ln_b, sg_ln_g, sg_ln_b, sg_w, sg_b, ln_mix_g, ln_mix_b, ffn_w_gate, ffn_w_up, ffn_w_down, ln_ffn_g, ln_ffn_b, ple_w_proj, ple_w_gate, ple_b_gate, loss_target, m_even_w_in, m_even_w_out, m_pool_w, m_pool_scale, m_odd_w_in, m_odd_w_out, m_conv_dw, m_conv_ln_g, m_conv_ln_b, m_sg_ln_g, m_sg_ln_b, m_sg_w, m_sg_b, m_ln_mix_g, m_ln_mix_b, m_ffn_w_gate, m_ffn_w_up, m_ffn_w_down, m_ln_ffn_g, m_ln_ffn_b, m_ple_w_proj, m_ple_w_gate, m_ple_b_gate, v_even_w_in, v_even_w_out, v_pool_w, v_pool_scale, v_odd_w_in, v_odd_w_out, v_conv_dw, v_conv_ln_g, v_conv_ln_b, v_sg_ln_g, v_sg_ln_b, v_sg_w, v_sg_b, v_ln_mix_g, v_ln_mix_b, v_ffn_w_gate, v_ffn_w_up, v_ffn_w_down, v_ln_ffn_g, v_ln_ffn_b, v_ple_w_proj, v_ple_w_gate, v_ple_b_gate):
    A = dict(locals())
    for arg in TRANSPOSED_ARGS:
        for pre in ("", "m_", "v_"):
            A[pre + arg] = jnp.swapaxes(A[pre + arg], 1, 2)
    mx, my, mc = _mesh_pos()
    j = 4 * mx + 2 * my + mc
    j_arr = j.astype(jnp.int32).reshape(1)
    cq_arr = jnp.stack([mc, 2 * mx + my]).astype(jnp.int32)
    res = {}

    def adam(nm, full, land1, land2, dep):
        arg = BIG[nm][4]
        res[arg] = reduce_adamw(full, land1, land2, A[arg], A["m_" + arg], A["v_" + arg], BIG[nm], cq_arr,
                                res.get(arg), dep)
        return res[arg][0]

    class Comm:
        def __init__(self):
            names = [nm for g in AG_GROUPS for nm in g]
            small_blk = jnp.concatenate([conv_dw[0], conv_ln_g, conv_ln_b, sg_ln_g, sg_ln_b, jnp.zeros((5, 64), F32)], axis=0)
            mine = pack_weights([A[k] for k in PACK_ARGS], PACK_ARGS, small_blk[None], names, j_arr)
            self.gat = Gatherer(AG_GROUPS, mine, {nm: _spec(nm) for nm in names}, "ag")
            self.gat.start()
            self.red = Reducer(cq_arr, adam)
            self.W = {k: A[k].reshape(NATURAL[k]) for k in REPLICATED}

        def weights(self, stage, after):
            if stage in AG_NEED:
                got = self.gat.finish(AG_NEED[stage], after)
                if "small" in got:
                    sm = got.pop("small").transpose(1, 0, 2).reshape(40, 512)
                    got.update(conv_dw=sm[0:31], conv_ln_g=sm[31:32], conv_ln_b=sm[32:33], sg_ln_g=sm[33:34],
                               sg_ln_b=sm[34:35])
                self.W.update(got)
            return self.W

        def all_weights(self):
            return self.W

        def poke(self, tag, after):
            if tag in AG_PASS:
                return self.gat.forward(AG_PASS[tag], after)
            if tag[0] == "bwd":
                return self.red.step(after)
            return None

        def grads(self, grads):
            tok = self.red.step(next(iter(grads.values())))
            return self.red.add(grads, after=tok)

    comm = Comm()
    sq, dx, small = run_layers(x[0], p[:, 0], loss_target[0], comm)
    loss = lax.psum(0.5 * jnp.sum(sq) / x.shape[-1], ("x", "y", "c"))
    red = comm.red
    tok = red.step(dx)

    names = REPLICATED + SHARDED_SMALL
    flat = jnp.concatenate([small[k].reshape(-1) for k in names])
    rows = -(-flat.shape[0] // 1024) * 8
    packed = jnp.pad(flat, (0, rows * 128 - flat.shape[0])).reshape(rows, 128)
    sg = Gatherer((["g"],), [place_slot(packed, j_arr)], {"g": ((N_DEV, rows, 128), 0, 1, 1)}, "sg")
    sg.start(after=tok)
    red.finish_oldest()
    red.finish_oldest()
    sg.forward(0, after=red.last)
    red.finish_oldest()
    gsum_flat = sum_slots(sg.finish(0, after=red.last)["g"]).reshape(-1)
    gs, off = [], 0
    for k in names:
        n = math.prod(small[k].shape)
        g = gsum_flat[off:off + n].reshape(small[k].shape)
        off += n
        if k in SHARDED_SMALL:
            g = lax.dynamic_slice_in_dim(g, j * 64, 64, axis=1)
        gs.append(g.reshape(A[k].shape))
    outs = small_adamw(gs, [(A[k], A["m_" + k], A["v_" + k]) for k in names])
    for a, k in enumerate(names):
        res[k] = (gs[a],) + tuple(outs[3 * a:3 * a + 3])
    red.last = outs[0]
    while red.finish_oldest():
        pass

    for arg in TRANSPOSED_ARGS:
        res[arg] = [jnp.swapaxes(t, 1, 2) for t in res[arg]]
    out = [loss, dx[None]]
    for part in range(4):
        out += [res[k][part] for k in WEIGHT_NAMES]
    return tuple(out)
```

```python
import functools
import math

import jax
import jax.numpy as jnp
from jax import lax
from jax.experimental import pallas as pl
from jax.experimental.pallas import tpu as pltpu

F32, BF16 = jnp.float32, jnp.bfloat16
ALPHA = 4.0 ** 0.25
LN_EPS = 1e-5
QK_SCALE = 0.125
POOL_WINDOWS = (2, 4, 8, 16)
CONV_TAPS = 31
N_DEV = 8
FF_SHARD, FF_PAD = 352, 384
ADAM_LR, ADAM_B1, ADAM_B2, ADAM_EPS, ADAM_WD, ADAM_STEP = 0.001, 0.9, 0.999, 1e-08, 0.01, 10
VMEM_LIMIT = 56 * 1024 * 1024
MESH_T = pl.DeviceIdType.MESH


def _cp(sem=None):
    return pltpu.CompilerParams(dimension_semantics=sem, vmem_limit_bytes=VMEM_LIMIT)


def _dot(a, b):
    return jnp.dot(a, b, preferred_element_type=F32)


def _dot_nt(a, b):
    return lax.dot_general(a, b, (((1,), (1,)), ((), ())), preferred_element_type=F32)


def _dot_tn(a, b):
    return lax.dot_general(a, b, (((0,), (0,)), ((), ())), preferred_element_type=F32)


def _sigmoid(x):
    return 1.0 / (1.0 + jnp.exp(-x))


def _softplus(z):
    return jnp.maximum(z, 0.0) + jnp.log(1.0 + jnp.exp(-jnp.abs(z)))


_GELU_C = math.sqrt(2.0 / math.pi)


def _gelu(x):
    return 0.5 * x * (1.0 + jnp.tanh(_GELU_C * (x + 0.044715 * x * x * x)))


def _gelu_grad(x):
    t = jnp.tanh(_GELU_C * (x + 0.044715 * x * x * x))
    return 0.5 * (1.0 + t) + 0.5 * x * (1.0 - t * t) * _GELU_C * (1.0 + 3.0 * 0.044715 * x * x)


def _ln_fwd(r, g, b):
    mu = jnp.mean(r, axis=-1, keepdims=True)
    xc = r - mu
    var = jnp.mean(xc * xc, axis=-1, keepdims=True)
    rstd = lax.rsqrt(var + LN_EPS)
    xh = xc * rstd
    return xh * g + b, xh, rstd


def _ln_bwd(dy, xh, rstd, g):
    dxh = dy * g
    m1 = jnp.mean(dxh, axis=-1, keepdims=True)
    m2 = jnp.mean(dxh * xh, axis=-1, keepdims=True)
    return rstd * (dxh - m1 - xh * m2)


def _split2(x):
    hi = x.astype(BF16)
    lo = (x - hi.astype(F32)).astype(BF16)
    return hi, lo


def _colsum(x):
    return jnp.sum(x, axis=0, keepdims=True)


def _tok_call(name, body, tiled, full, out_tiled, out_acc=(), tm=256, scratch=(), dep=None):
    def arr(t):
        return t[0] if isinstance(t, tuple) else t
    full = [t[0] if isinstance(t, tuple) and t[1] is None else t for t in full]
    S = arr(tiled[0]).shape[0]
    tm = min(tm, S)
    n_in = len(tiled) + len(full)
    deps = [] if dep is None else [dep]
    if deps:
        inner = body
        body = lambda *refs: inner(*refs[:n_in], *refs[n_in + 1:])

    def tspec(t):
        if isinstance(t, tuple):
            _, w, cb = t
            return pl.BlockSpec((tm, w), lambda i, cb=cb: (i, cb))
        return pl.BlockSpec((tm, t.shape[1]), lambda i: (i, 0))

    def fspec(t):
        if isinstance(t, tuple):
            a, l = t
            nd = a.ndim - 1
            return pl.BlockSpec((None,) + a.shape[1:], lambda i, l=l, nd=nd: (l,) + (0,) * nd)
        nd = t.ndim
        return pl.BlockSpec(t.shape, lambda i, nd=nd: (0,) * nd)

    def ospec(o):
        return pl.BlockSpec((tm, o.shape[1]), lambda i: (i, 0))

    def aspec(o):
        nd = len(o.shape)
        return pl.BlockSpec(o.shape, lambda i, nd=nd: (0,) * nd)

    outs = pl.pallas_call(
        body, name=name, grid=(S // tm,),
        in_specs=[tspec(t) for t in tiled] + [fspec(t) for t in full] + [ANY] * len(deps),
        out_specs=[ospec(o) for o in out_tiled] + [aspec(o) for o in out_acc],
        out_shape=list(out_tiled) + list(out_acc),
        scratch_shapes=list(scratch),
        compiler_params=_cp(("arbitrary",)),
    )(*[arr(t) for t in tiled], *[arr(t) for t in full], *deps)
    return outs


def _sds(shape, dtype=F32):
    return jax.ShapeDtypeStruct(tuple(shape), dtype)


def _acc(ref, val):
    @pl.when(pl.program_id(0) == 0)
    def _():
        ref[...] = val

    @pl.when(pl.program_id(0) != 0)
    def _():
        ref[...] += val


def mm_in(x, w, nb16=0):
    S, N = x.shape[0], w.shape[1]

    def body(x_ref, w_ref, h_ref, xb_ref, *hb_ref):
        xb = x_ref[...].astype(BF16)
        xb_ref[...] = xb
        h = _dot(xb, w_ref[...])
        h_ref[...] = h
        if nb16:
            hb_ref[0][...] = h[:, 0:nb16].astype(BF16)

    outs = [_sds((S, N)), _sds((S, x.shape[1]), BF16)] + ([_sds((S, nb16), BF16)] if nb16 else [])
    return _tok_call("mm_in", body, [x], [w], outs, tm=512)


def _stack_heads(x, hm0, dtype=BF16):
    return jnp.concatenate([jnp.where(hm0, x, 0), jnp.where(hm0, 0, x)], axis=0).astype(dtype)


def _unstack_k(x, T):
    return jnp.concatenate([x[0:T], x[T:2 * T]], axis=1)


def _cumsum_mm(x, u):
    n = x.shape[0]
    hi, lo = _split2(x)
    r = _dot(jnp.concatenate([hi, lo], axis=0), u)
    return r[0:n] + r[n:2 * n]


def attn_fwd(qkv, T=256):
    S = qkv.shape[0]
    T = min(T, S)
    nq = S // T

    def body(q_ref, k_ref, v_ref, o_ref, t_ref, acc_ref, c_ref, qh_ref):
        i = pl.program_id(0)
        hm0 = lax.broadcasted_iota(jnp.int32, (1, 128), 1) < 64
        r2 = lax.broadcasted_iota(jnp.int32, (2 * T, T), 0)
        c2 = lax.broadcasted_iota(jnp.int32, (2 * T, T), 1)
        causal = c2 < jnp.where(r2 >= T, r2 - T, r2)
        ur = lax.broadcasted_iota(jnp.int32, (T, T), 0)
        uc = lax.broadcasted_iota(jnp.int32, (T, T), 1)
        u_incl = (ur >= uc).astype(BF16)
        acc_ref[...] = jnp.zeros_like(acc_ref)
        c_ref[...] = jnp.zeros_like(c_ref)
        for pp in range(4):
            qh_ref[pp] = _stack_heads(q_ref[:, pp * 128:(pp + 1) * 128] * QK_SCALE, hm0)

        def block(kb, diag):
            ks = pl.multiple_of(kb * T, T)
            cols = [slice(pp * 128, (pp + 1) * 128) for pp in range(4)]
            zs = [_dot_nt(qh_ref[pp], k_ref[pl.ds(ks, T), cols[pp]]) for pp in range(4)]
            incls = []
            for pp in range(4):
                sp = _softplus(zs[pp])
                if diag:
                    sp = jnp.where(causal, sp, 0.0)
                incls.append(_cumsum_mm(sp, u_incl))
            for pp in range(4):
                c = c_ref[pp]
                w = jnp.exp(zs[pp] - incls[pp] - c)
                if diag:
                    w = jnp.where(causal, w, 0.0)
                acc_ref[:, cols[pp]] += _dot(_unstack_k(w.astype(BF16), T),
                                             _stack_heads(v_ref[pl.ds(ks, T), cols[pp]], hm0))
                c_ref[pp] = c + jnp.broadcast_to(incls[pp][:, 0:1], (2 * T, T))

        block(i, True)

        def step(jj, carry):
            block(i - 1 - jj, False)
            return carry

        lax.fori_loop(0, i, step, 0)
        o_ref[...] = acc_ref[...].astype(BF16)
        for pp in range(4):
            for hd in range(2):
                t_ref[2 * pp + hd] = c_ref[pp, hd * T:(hd + 1) * T, 0:128]

    return pl.pallas_call(
        body, name="attn_fwd", grid=(nq,),
        in_specs=[pl.BlockSpec((T, 512), lambda i: (i, 0)),
                  pl.BlockSpec((S, 512), lambda i: (0, 1)),
                  pl.BlockSpec((S, 512), lambda i: (0, 2))],
        out_specs=[pl.BlockSpec((T, 512), lambda i: (i, 0)),
                   pl.BlockSpec((8, T, 128), lambda i: (0, i, 0))],
        out_shape=[_sds((S, 512), BF16), _sds((8, S, 128))],
        scratch_shapes=[pltpu.VMEM((T, 512), F32), pltpu.VMEM((4, 2 * T, T), F32), pltpu.VMEM((4, 2 * T, 128), BF16)],
        compiler_params=_cp(("arbitrary",)),
    )(qkv, qkv, qkv)


def pool_fwd(h, pool_w, pool_scale, CH=256):
    S = h.shape[0]
    CH = min(CH, S)

    def body(u_ref, w_ref, sc_ref, b_ref, pooled_ref, pad_ref):
        pad_ref[0:16, :] = jnp.zeros((16, 512), F32)
        pad_ref[16:16 + S, :] = u_ref[...]
        for g, win in enumerate(POOL_WINDOWS):
            cs = slice(g * 128, (g + 1) * 128)
            wq = w_ref[g].astype(BF16)
            for ch in range(S // CH):
                base = ch * CH
                acc = pad_ref[16 + base:16 + base + CH, cs]
                for sft in range(1, win):
                    acc = acc + pad_ref[16 + base - sft:16 + base - sft + CH, cs]
                t = base + lax.broadcasted_iota(jnp.int32, (CH, 1), 0)
                cnt = jnp.minimum(t + 1, win).astype(F32)
                pooled = (acc / cnt - pad_ref[16 + base:16 + base + CH, cs]).astype(BF16)
                pooled_ref[base:base + CH, cs] = pooled
                b_ref[base:base + CH, cs] = (_dot(pooled, wq) * sc_ref[:, cs]).astype(BF16)

    return pl.pallas_call(
        body, name="pool_fwd", grid=(1,),
        in_specs=[pl.BlockSpec((S, 512), lambda i: (0, 3)),
                  pl.BlockSpec((4, 128, 128), lambda i: (0, 0, 0)),
                  pl.BlockSpec((1, 512), lambda i: (0, 0))],
        out_specs=[pl.BlockSpec((S, 512), lambda i: (0, 0)), pl.BlockSpec((S, 512), lambda i: (0, 0))],
        out_shape=[_sds((S, 512), BF16), _sds((S, 512), BF16)],
        scratch_shapes=[pltpu.VMEM((S + 16, 512), F32)],
        compiler_params=_cp(("arbitrary",)),
    )(h, pool_w, pool_scale)


def conv_fwd(h, dw, CH=128):
    S = h.shape[0]

    def body(a_ref, g_ref, dw_ref, y_ref, hc_ref, pad_ref):
        hc = a_ref[...] * _sigmoid(g_ref[...])
        hc_ref[...] = hc
        pad_ref[0:32, :] = jnp.zeros((32, 128), F32)
        pad_ref[32:32 + S, :] = hc
        for ch in range(S // CH):
            base = ch * CH + 2
            acc = dw_ref[0:1, :] * pad_ref[base:base + CH, :]
            for k in range(1, CONV_TAPS):
                acc = acc + dw_ref[k:k + 1, :] * pad_ref[base + k:base + k + CH, :]
            y_ref[ch * CH:(ch + 1) * CH, :] = acc

    return pl.pallas_call(
        body, name="conv_fwd", grid=(4,),
        in_specs=[pl.BlockSpec((S, 128), lambda c: (0, c)),
                  pl.BlockSpec((S, 128), lambda c: (0, 4 + c)),
                  pl.BlockSpec((CONV_TAPS, 128), lambda c: (0, c))],
        out_specs=[pl.BlockSpec((S, 128), lambda c: (0, c)), pl.BlockSpec((S, 128), lambda c: (0, c))],
        out_shape=[_sds((S, 512)), _sds((S, 512))],
        scratch_shapes=[pltpu.VMEM((S + 32, 128), F32)],
        compiler_params=_cp(("arbitrary",)),
    )(h, h, dw)


def _masked_sg_w(w_ref, g):
    row = lax.broadcasted_iota(jnp.int32, (128, 128), 0)
    col = lax.broadcasted_iota(jnp.int32, (128, 128), 1)
    return jnp.where(row >= col, w_ref[g], 0.0).astype(BF16)


def odd_post(y, h, cl_g, cl_b, sl_g, sl_b, sg_w, sgb_bc, tm=256):
    S = y.shape[0]
    tm = min(tm, S)

    def body(y_ref, zc_ref, clg, clb, slg, slb, w_ref, sb_ref,
             c_ref, d_ref, xhc_ref, rsc_ref, xhv_ref, rsv_ref, sv_ref):
        lnc, xhc, rsc = _ln_fwd(y_ref[...], clg[...], clb[...])
        c_ref[...] = (lnc * _sigmoid(lnc)).astype(BF16)
        xhc_ref[...] = xhc
        rsc_ref[...] = rsc
        z = _gelu(zc_ref[...])
        vn, xhv, rsv = _ln_fwd(z[:, 512:], slg[...], slb[...])
        xhv_ref[...] = xhv
        rsv_ref[...] = rsv
        vnb = vn.astype(BF16)
        for g in range(4):
            wm = _masked_sg_w(w_ref, g)
            for ch in range(tm // 128):
                rs, cs = slice(ch * 128, (ch + 1) * 128), slice(g * 128, (g + 1) * 128)
                sv_ref[rs, cs] = _dot(wm, vnb[rs, cs]) + sb_ref[g]
        d_ref[...] = (z[:, :512] * sv_ref[...]).astype(BF16)

    return _tok_call(
        "odd_post", body, [y, (h, 1024, 1)], [cl_g, cl_b, sl_g, sl_b, sg_w, sgb_bc],
        [_sds((S, 512), BF16), _sds((S, 512), BF16), _sds((S, 512)), _sds((S, 1)),
         _sds((S, 512)), _sds((S, 1)), _sds((S, 512))], tm=tm)


def mm_out_ln(l1, l2, x, w, g, b, dep=None):
    S, D = x.shape

    def body(l1_ref, l2_ref, x_ref, w_ref, g_ref, b_ref, y_ref, xh_ref, rs_ref):
        mix = _dot(l1_ref[...], w_ref[0:512, :]) + _dot(l2_ref[...], w_ref[512:1024, :])
        y, xh, rs = _ln_fwd(ALPHA * x_ref[...] + mix, g_ref[...], b_ref[...])
        y_ref[...] = y
        xh_ref[...] = xh
        rs_ref[...] = rs

    return _tok_call("mm_out_ln", body, [l1, l2, x], [w, g, b],
                     [_sds((S, D)), _sds((S, D)), _sds((S, 1))], dep=dep)


def ffn_up(x1, wg, wu, layer):
    S, D = x1.shape
    F = wg.shape[-1]

    def body(x_ref, wg_ref, wu_ref, gate_ref, up_ref, hb_ref, xb_ref):
        xb = x_ref[...].astype(BF16)
        xb_ref[...] = xb
        gate = _dot(xb, wg_ref[...])
        up = _dot(xb, wu_ref[...])
        gate_ref[...] = gate.astype(BF16)
        up_ref[...] = up.astype(BF16)
        hb_ref[...] = (gate * _sigmoid(gate) * up).astype(BF16)

    return _tok_call("ffn_up", body, [x1], [(wg, layer), (wu, layer)],
                     [_sds((S, F), BF16), _sds((S, F), BF16), _sds((S, F), BF16), _sds((S, D), BF16)])


def ffn_down_ln(hb, x1, wd, layer, g, b):
    S, D = x1.shape

    def body(h_ref, x_ref, w_ref, g_ref, b_ref, y_ref, xh_ref, rs_ref):
        f = _dot(h_ref[...], w_ref[...])
        y, xh, rs = _ln_fwd(ALPHA * x_ref[...] + f, g_ref[...], b_ref[...])
        y_ref[...] = y
        xh_ref[...] = xh
        rs_ref[...] = rs

    return _tok_call("ffn_down_ln", body, [hb, x1], [(wd, layer), g, b],
                     [_sds((S, D)), _sds((S, D)), _sds((S, 1))])


def ple_fwd(x2, p, wpg, wpp, layer, bg, target=None, dep=None):
    S, D = x2.shape
    last = target is not None

    def body(*refs):
        if last:
            x_ref, p_ref, t_ref, wg_ref, wp_ref, b_ref, x3_ref, sg_ref, pp_ref, xb_ref, pb_ref, dy_ref, ls_ref = refs
        else:
            x_ref, p_ref, wg_ref, wp_ref, b_ref, x3_ref, sg_ref, pp_ref, xb_ref, pb_ref = refs
        x = x_ref[...]
        xb = x.astype(BF16)
        pb = p_ref[...].astype(BF16)
        xb_ref[...] = xb
        pb_ref[...] = pb
        sg = _sigmoid(_dot(xb, wg_ref[...]) + b_ref[...])
        pp = _dot(pb, wp_ref[...])
        sg_ref[...] = sg.astype(BF16)
        pp_ref[...] = pp.astype(BF16)
        x3 = x + sg * pp
        x3_ref[...] = x3
        if last:
            err = x3 - t_ref[...]
            dy_ref[...] = err * (1.0 / D)
            _acc(ls_ref, _colsum(err * err))

    outs = [_sds((S, D)), _sds((S, D), BF16), _sds((S, D), BF16), _sds((S, D), BF16), _sds((S, p.shape[1]), BF16)]
    tiled = [x2, p] + ([target] if last else [])
    if last:
        outs.append(_sds((S, D)))
    return _tok_call("ple_fwd", body, tiled, [(wpg, layer), (wpp, layer), bg], outs,
                     [_sds((1, D))] if last else [], dep=dep)


def ple_bwd(dx3, sg, pp, wpg, layer, dep=None):
    S, D = dx3.shape

    def body(d_ref, sg_ref, pp_ref, w_ref, dx_ref, dgp_ref, dpp_ref, dbg_ref):
        d, sg = d_ref[...], sg_ref[...].astype(F32)
        dgp = d * pp_ref[...].astype(F32) * sg * (1.0 - sg)
        dgpb = dgp.astype(BF16)
        dgp_ref[...] = dgpb
        dpp_ref[...] = (d * sg).astype(BF16)
        dx_ref[...] = d + _dot_nt(dgpb, w_ref[...])
        _acc(dbg_ref, _colsum(dgp))

    return _tok_call("ple_bwd", body, [dx3, sg, pp], [(wpg, layer)],
                     [_sds((S, D)), _sds((S, D), BF16), _sds((S, D), BF16)], [_sds((1, D))], dep=dep)


def ffn_bwd_a(dx2, xh, rs, g, gate, up, wd, layer):
    S, D = dx2.shape
    F = gate.shape[1]

    def body(d_ref, xh_ref, rs_ref, gate_ref, up_ref, g_ref, w_ref,
             dr_ref, drb_ref, dg_ref, du_ref, dlg_ref, dlb_ref):
        d, xh = d_ref[...], xh_ref[...]
        dr = _ln_bwd(d, xh, rs_ref[...], g_ref[...])
        drb = dr.astype(BF16)
        dr_ref[...] = dr
        drb_ref[...] = drb
        _acc(dlg_ref, _colsum(d * xh))
        _acc(dlb_ref, _colsum(d))
        dh = _dot_nt(drb, w_ref[...])
        gate, up = gate_ref[...].astype(F32), up_ref[...].astype(F32)
        s = _sigmoid(gate)
        dg_ref[...] = (dh * up * s * (1.0 + gate * (1.0 - s))).astype(BF16)
        du_ref[...] = (dh * gate * s).astype(BF16)

    return _tok_call("ffn_bwd_a", body, [dx2, xh, rs, gate, up], [g, (wd, layer)],
                     [_sds((S, D)), _sds((S, D), BF16), _sds((S, F), BF16), _sds((S, F), BF16)],
                     [_sds((1, D)), _sds((1, D))])


def ffn_bwd_b(dr, dgate_b, dup_b, wg, wu, layer, dep=None):
    S, D = dr.shape

    def body(dr_ref, dg_ref, du_ref, wg_ref, wu_ref, dx_ref):
        dx_ref[...] = (ALPHA * dr_ref[...] + _dot_nt(dg_ref[...], wg_ref[...])
                       + _dot_nt(du_ref[...], wu_ref[...]))

    return _tok_call("ffn_bwd_b", body, [dr, dgate_b, dup_b], [(wg, layer), (wu, layer)], [_sds((S, D))], dep=dep)[0]


def mix_bwd(dx1, xh, rs, g, w):
    S, D = dx1.shape

    def body(d_ref, xh_ref, rs_ref, g_ref, w_ref, dr_ref, dmb_ref, dl_ref, dlg_ref, dlb_ref):
        d, xh = d_ref[...], xh_ref[...]
        dr = _ln_bwd(d, xh, rs_ref[...], g_ref[...])
        drb = dr.astype(BF16)
        dr_ref[...] = dr
        dmb_ref[...] = drb
        dl_ref[...] = _dot_nt(drb, w_ref[...])
        _acc(dlg_ref, _colsum(d * xh))
        _acc(dlb_ref, _colsum(d))

    return _tok_call("mix_bwd", body, [dx1, xh, rs], [g, w],
                     [_sds((S, D)), _sds((S, D), BF16), _sds((S, D))], [_sds((1, D)), _sds((1, D))])


def dx_in(dr, pieces, w):
    S, D = dr.shape
    offs = [o for _, o in pieces]
    widths = [a.shape[1] for a, _ in pieces]

    def body(*refs):
        dr_ref, prefs, w_ref, dx_ref = refs[0], refs[1:1 + len(pieces)], refs[-2], refs[-1]
        acc = ALPHA * dr_ref[...]
        for pr, o, n in zip(prefs, offs, widths):
            acc = acc + _dot_nt(pr[...], w_ref[:, o:o + n])
        dx_ref[...] = acc

    return _tok_call("dx_in", body, [dr] + [a for a, _ in pieces], [w], [_sds((S, D))])[0]


def odd_post_bwd(dl, h, xhc, rsc, xhv, rsv, sv, cl_g, cl_b, sl_g, sl_b, sg_w, tm=256, dep=None):
    S = dl.shape[0]
    tm = min(tm, S)

    def body(dl_ref, zc_ref, xhc_ref, rsc_ref, xhv_ref, rsv_ref, sv_ref, clg, clb, slg, slb, w_ref,
             dy_ref, dzc_ref, dclg_ref, dclb_ref, dslg_ref, dslb_ref, dwm_ref, dsb_ref, dvn_ref):
        first = pl.program_id(0) == 0
        last = pl.program_id(0) == pl.num_programs(0) - 1
        dc, dd = dl_ref[:, 0:512], dl_ref[:, 512:1024]
        xhc = xhc_ref[...]
        lnc = xhc * clg[...] + clb[...]
        s = _sigmoid(lnc)
        dlnc = dc * s * (1.0 + lnc * (1.0 - s))
        dy_ref[...] = _ln_bwd(dlnc, xhc, rsc_ref[...], clg[...])
        _acc(dclg_ref, _colsum(dlnc * xhc))
        _acc(dclb_ref, _colsum(dlnc))
        zc = zc_ref[...]
        z = _gelu(zc)
        dsv = dd * z[:, :512]
        dsvb = dsv.astype(BF16)
        xhv = xhv_ref[...]
        vnb = (xhv * slg[...] + slb[...]).astype(BF16)

        @pl.when(first)
        def _():
            dwm_ref[...] = jnp.zeros_like(dwm_ref)
            dsb_ref[...] = jnp.zeros_like(dsb_ref)

        for g in range(4):
            wm = _masked_sg_w(w_ref, g)
            for ch in range(tm // 128):
                rs_, cs = slice(ch * 128, (ch + 1) * 128), slice(g * 128, (g + 1) * 128)
                dwm_ref[g] += _dot_nt(dsvb[rs_, cs], vnb[rs_, cs])
                dvn_ref[rs_, cs] = _dot_tn(wm, dsvb[rs_, cs])
                dsb_ref[g] += dsv[rs_, cs]
        dvn = dvn_ref[...]
        dvv = _ln_bwd(dvn, xhv, rsv_ref[...], slg[...])
        _acc(dslg_ref, _colsum(dvn * xhv))
        _acc(dslb_ref, _colsum(dvn))
        gg = _gelu_grad(zc)
        dzc_ref[:, 0:512] = (dd * sv_ref[...] * gg[:, :512]).astype(BF16)
        dzc_ref[:, 512:1024] = (dvv * gg[:, 512:]).astype(BF16)

        @pl.when(last)
        def _():
            row = lax.broadcasted_iota(jnp.int32, (128, 128), 0)
            col = lax.broadcasted_iota(jnp.int32, (128, 128), 1)
            for g in range(4):
                dwm_ref[g] = jnp.where(row >= col, dwm_ref[g], 0.0)
                dsb_ref[g] = jnp.broadcast_to(jnp.sum(dsb_ref[g], axis=1, keepdims=True), (128, 128))

    return _tok_call(
        "odd_post_bwd", body, [dl, (h, 1024, 1), xhc, rsc, xhv, rsv, sv], [cl_g, cl_b, sl_g, sl_b, sg_w],
        [_sds((S, 512)), _sds((S, 1024), BF16)],
        [_sds((1, 512)), _sds((1, 512)), _sds((1, 512)), _sds((1, 512)), _sds((4, 128, 128)), _sds((4, 128, 128))],
        tm=tm, scratch=[pltpu.VMEM((tm, 512), F32)], dep=dep)


def conv_bwd(dy, hc, h, dw, CH=128):
    S = dy.shape[0]

    def body(dy_ref, hc_ref, a_ref, g_ref, dw_ref, da_ref, dg_ref, ddw_ref, padh_ref, padd_ref, dhc_ref):
        padh_ref[0:32, :] = jnp.zeros((32, 128), F32)
        padh_ref[32:32 + S, :] = hc_ref[...]
        padd_ref[0:S, :] = dy_ref[...]
        padd_ref[S:S + 32, :] = jnp.zeros((32, 128), F32)
        taps = [jnp.zeros((1, 128), F32) for _ in range(CONV_TAPS)]
        for ch in range(S // CH):
            b0 = ch * CH
            dyc = padd_ref[b0:b0 + CH, :]
            acc = dw_ref[0:1, :] * padd_ref[b0 + 30:b0 + 30 + CH, :]
            taps[0] = taps[0] + _colsum(dyc * padh_ref[b0 + 2:b0 + 2 + CH, :])
            for k in range(1, CONV_TAPS):
                acc = acc + dw_ref[k:k + 1, :] * padd_ref[b0 + 30 - k:b0 + 30 - k + CH, :]
                taps[k] = taps[k] + _colsum(dyc * padh_ref[b0 + 2 + k:b0 + 2 + k + CH, :])
            dhc_ref[b0:b0 + CH, :] = acc
        for k in range(CONV_TAPS):
            ddw_ref[k:k + 1, :] = taps[k]
        dhc = dhc_ref[...]
        s = _sigmoid(g_ref[...])
        da_ref[...] = (dhc * s).astype(BF16)
        dg_ref[...] = (dhc * a_ref[...] * s * (1.0 - s)).astype(BF16)

    return pl.pallas_call(
        body, name="conv_bwd", grid=(4,),
        in_specs=[pl.BlockSpec((S, 128), lambda c: (0, c)),
                  pl.BlockSpec((S, 128), lambda c: (0, c)),
                  pl.BlockSpec((S, 128), lambda c: (0, c)),
                  pl.BlockSpec((S, 128), lambda c: (0, 4 + c)),
                  pl.BlockSpec((CONV_TAPS, 128), lambda c: (0, c))],
        out_specs=[pl.BlockSpec((S, 128), lambda c: (0, c)), pl.BlockSpec((S, 128), lambda c: (0, c)),
                   pl.BlockSpec((CONV_TAPS, 128), lambda c: (0, c))],
        out_shape=[_sds((S, 512), BF16), _sds((S, 512), BF16), _sds((CONV_TAPS, 512))],
        scratch_shapes=[pltpu.VMEM((S + 32, 128), F32), pltpu.VMEM((S + 32, 128), F32), pltpu.VMEM((S, 128), F32)],
        compiler_params=_cp(("arbitrary",)),
    )(dy, hc, h, h, dw)


def attn_bwd(qkv, dl, tb, T=256, dep=None):
    S = qkv.shape[0]
    T = min(T, S)
    nq = S // T

    def body(q_ref, k_ref, v_ref, do_ref, t_ref, dq_ref, dk_ref, dv_ref,
             dka_ref, dva_ref, dqa_ref, pc_ref, gc_ref, qh_ref, doh_ref):
        i = pl.program_id(0)
        hm0 = lax.broadcasted_iota(jnp.int32, (1, 128), 1) < 64
        r2 = lax.broadcasted_iota(jnp.int32, (2 * T, T), 0)
        c2 = lax.broadcasted_iota(jnp.int32, (2 * T, T), 1)
        causal = c2 < jnp.where(r2 >= T, r2 - T, r2)
        ur = lax.broadcasted_iota(jnp.int32, (T, T), 0)
        uc = lax.broadcasted_iota(jnp.int32, (T, T), 1)
        u_le = (ur <= uc).astype(BF16)
        u_lt = (ur < uc).astype(BF16)

        @pl.when(i == 0)
        def _():
            dka_ref[...] = jnp.zeros_like(dka_ref)
            dva_ref[...] = jnp.zeros_like(dva_ref)

        dqa_ref[...] = jnp.zeros_like(dqa_ref)
        gc_ref[...] = jnp.zeros_like(gc_ref)
        for pp in range(4):
            cs = slice(pp * 128, (pp + 1) * 128)
            qh_ref[pp] = _stack_heads(q_ref[:, cs] * QK_SCALE, hm0)
            doh_ref[pp] = _stack_heads(do_ref[:, cs], hm0)
            for hd in range(2):
                for half in range(T // 128):
                    pc_ref[pp, hd * T:(hd + 1) * T, half * 128:(half + 1) * 128] = t_ref[2 * pp + hd]

        def block(kb, diag):
            ks = pl.multiple_of(kb * T, T)
            cols = [slice(pp * 128, (pp + 1) * 128) for pp in range(4)]
            zs = [_dot_nt(qh_ref[pp], k_ref[pl.ds(ks, T), cols[pp]]) for pp in range(4)]
            dws = [_dot_nt(doh_ref[pp], v_ref[pl.ds(ks, T), cols[pp]]) for pp in range(4)]
            a_s, pres = [], []
            for pp in range(4):
                sp = _softplus(zs[pp])
                a_s.append(zs[pp] - sp)
                if diag:
                    sp = jnp.where(causal, sp, 0.0)
                pres.append(_cumsum_mm(sp, u_le))
            ws, gmats, gsums = [], [], []
            for pp in range(4):
                rem = pc_ref[pp]
                w = jnp.exp(a_s[pp] - rem + pres[pp])
                if diag:
                    w = jnp.where(causal, w, 0.0)
                gmat = dws[pp] * w
                ws.append(w.astype(BF16))
                gmats.append(gmat)
                gsums.append(_cumsum_mm(gmat, u_lt))
                pc_ref[pp] = rem - jnp.broadcast_to(pres[pp][:, T - 1:T], (2 * T, T))
            for pp in range(4):
                cs = cols[pp]
                sig = jnp.exp(a_s[pp])
                gex = gc_ref[pp] + gsums[pp]
                dz = gmats[pp] * (1.0 - sig) - sig * gex
                if diag:
                    dz = jnp.where(causal, dz, 0.0)
                dzb = dz.astype(BF16)
                dqa_ref[:, cs] += _dot(_unstack_k(dzb, T), _stack_heads(k_ref[pl.ds(ks, T), cs], hm0))
                dka_ref[pl.ds(ks, T), cs] += _dot_tn(dzb, qh_ref[pp])
                dva_ref[pl.ds(ks, T), cs] += _dot_tn(ws[pp], doh_ref[pp])
                gc_ref[pp] = jnp.broadcast_to(gex[:, T - 1:T] + gmats[pp][:, T - 1:T], (2 * T, T))

        def step(kb, carry):
            block(kb, False)
            return carry

        lax.fori_loop(0, i, step, 0)
        block(i, True)
        dq_ref[...] = (dqa_ref[...] * QK_SCALE).astype(BF16)

        @pl.when(i == nq - 1)
        def _():
            dk_ref[...] = dka_ref[...].astype(BF16)
            dv_ref[...] = dva_ref[...].astype(BF16)

    deps = [] if dep is None else [dep]
    call_body = body if dep is None else (lambda *refs: body(*refs[:5], *refs[6:]))
    return pl.pallas_call(
        call_body, name="attn_bwd", grid=(nq,),
        in_specs=[pl.BlockSpec((T, 512), lambda i: (i, 0)),
                  pl.BlockSpec((S, 512), lambda i: (0, 1)),
                  pl.BlockSpec((S, 512), lambda i: (0, 2)),
                  pl.BlockSpec((T, 512), lambda i: (i, 0)),
                  pl.BlockSpec((8, T, 128), lambda i: (0, i, 0))] + [ANY] * len(deps),
        out_specs=[pl.BlockSpec((T, 512), lambda i: (i, 0)),
                   pl.BlockSpec((S, 512), lambda i: (0, 0)),
                   pl.BlockSpec((S, 512), lambda i: (0, 0))],
        out_shape=[_sds((S, 512), BF16), _sds((S, 512), BF16), _sds((S, 512), BF16)],
        scratch_shapes=[pltpu.VMEM((S, 512), F32), pltpu.VMEM((S, 512), F32), pltpu.VMEM((T, 512), F32),
                        pltpu.VMEM((4, 2 * T, T), F32), pltpu.VMEM((4, 2 * T, T), F32),
                        pltpu.VMEM((4, 2 * T, 128), BF16), pltpu.VMEM((4, 2 * T, 128), BF16)],
        compiler_params=_cp(("arbitrary",)),
    )(qkv, qkv, qkv, dl, tb, *deps)


def pool_bwd(dl, pooled_b, pool_w, pool_scale, CH=256):
    S = dl.shape[0]
    CH = min(CH, S)

    def body(db_ref, pooled_ref, w_ref, sc_ref, du_ref, dw_ref, dsc_ref, pad_ref, dp_ref):
        pad_ref[S:S + 16, :] = jnp.zeros((16, 128), F32)
        for g, win in enumerate(POOL_WINDOWS):
            cs = slice(g * 128, (g + 1) * 128)
            wq = w_ref[g].astype(BF16)
            dwg = jnp.zeros((128, 128), F32)
            dsc = jnp.zeros((1, 128), F32)
            for ch in range(S // CH):
                rs_ = slice(ch * CH, (ch + 1) * CH)
                db = db_ref[rs_, cs]
                pb = pooled_ref[rs_, cs]
                dsc = dsc + _colsum(db * _dot(pb, wq))
                dmsb = (db * sc_ref[:, cs]).astype(BF16)
                dwg = dwg + _dot_tn(pb, dmsb)
                dpool = _dot_nt(dmsb, wq)
                t = ch * CH + lax.broadcasted_iota(jnp.int32, (CH, 1), 0)
                cnt = jnp.minimum(t + 1, win).astype(F32)
                dp_ref[rs_, :] = dpool
                pad_ref[rs_, :] = dpool / cnt
            dw_ref[g] = dwg
            dsc_ref[:, cs] = dsc
            for ch in range(S // CH):
                base = ch * CH
                acc = pad_ref[base:base + CH, :]
                for sft in range(1, win):
                    acc = acc + pad_ref[base + sft:base + sft + CH, :]
                du_ref[base:base + CH, cs] = (acc - dp_ref[base:base + CH, :]).astype(BF16)

    return pl.pallas_call(
        body, name="pool_bwd", grid=(1,),
        in_specs=[pl.BlockSpec((S, 512), lambda i: (0, 1)),
                  pl.BlockSpec((S, 512), lambda i: (0, 0)),
                  pl.BlockSpec((4, 128, 128), lambda i: (0, 0, 0)),
                  pl.BlockSpec((1, 512), lambda i: (0, 0))],
        out_specs=[pl.BlockSpec((S, 512), lambda i: (0, 0)),
                   pl.BlockSpec((4, 128, 128), lambda i: (0, 0, 0)),
                   pl.BlockSpec((1, 512), lambda i: (0, 0))],
        out_shape=[_sds((S, 512), BF16), _sds((4, 128, 128)), _sds((1, 512))],
        scratch_shapes=[pltpu.VMEM((S + 16, 128), F32), pltpu.VMEM((S, 128), F32)],
        compiler_params=_cp(("arbitrary",)),
    )(dl, pooled_b, pool_w, pool_scale)


def tn_into(a, b, out, r0, c0, tk=1024, tn=512):
    S, K = a.shape
    N = b.shape[1]
    tk, tn = min(tk, K), min(tn, N)
    assert K % tk == 0 and N % tn == 0 and r0 % tk == 0 and c0 % tn == 0
    rb, cb = r0 // tk, c0 // tn
    fresh = isinstance(out, jax.ShapeDtypeStruct)

    def body(*refs):
        a_ref, b_ref, o_ref = refs[0], refs[1], refs[-1]
        o_ref[...] = _dot_tn(a_ref[...], b_ref[...]).astype(BF16)

    ospec = pl.BlockSpec((tk, tn), lambda i, j: (rb + i, cb + j))
    in_specs = [pl.BlockSpec((S, tk), lambda i, j: (0, i)), pl.BlockSpec((S, tn), lambda i, j: (0, j))]
    args = [a, b]
    aliases = {}
    if not fresh:
        in_specs += [pl.BlockSpec(memory_space=pl.ANY)]
        args += [out]
        aliases = {2: 0}
    return pl.pallas_call(
        body, name="tn_grad", grid=(K // tk, N // tn),
        in_specs=in_specs, out_specs=[ospec],
        out_shape=[_sds(out.shape, BF16)],
        input_output_aliases=aliases,
        compiler_params=_cp(("arbitrary", "arbitrary")),
    )(*args)[0]


def _row(a, i):
    return a[i:i + 1]


MIXER_NAMES = (("even_w_in", "even_w_out"), ("odd_w_in", "odd_w_out"))


def _tn_group(items):
    out = {}
    for name, (shape, parts) in items.items():
        g = _sds(shape, BF16)
        for a, b, r0, c0 in parts:
            g = tn_into(a, b, g, r0, c0)
        out[name] = g
    return out


def fwd_layer(i, xin, p_i, target, comm):
    s = {}
    W = comm.weights(("mix", i), xin)
    w_in = W[MIXER_NAMES[i][0]]
    if i == 0:
        s["h"], s["xb"], s["qkv"] = mm_in(xin, w_in, nb16=1536)
        comm.poke(("in", i), s["h"])
        s["l1"], s["tb"] = attn_fwd(s["qkv"])
        s["l2"], s["pooled"] = pool_fwd(s["h"], W["pool_w"], W["pool_scale"])
    else:
        s["h"], s["xb"] = mm_in(xin, w_in)
        comm.poke(("in", i), s["h"])
        s["y"], s["hc"] = conv_fwd(s["h"], W["conv_dw"])
        sgb_bc = jnp.broadcast_to(W["sg_b"][:, :, None], (4, 128, 128))
        (s["l1"], s["l2"], s["xhc"], s["rsc"], s["xhv"], s["rsv"], s["sv"]) = odd_post(
            s["y"], s["h"], W["conv_ln_g"], W["conv_ln_b"], W["sg_ln_g"], W["sg_ln_b"], W["sg_w"], sgb_bc)
    tok = comm.poke(("mixed", i), s["l1"])
    W = comm.weights(("out", i), s["l1"])
    x1, s["xh1"], s["rs1"] = mm_out_ln(s["l1"], s["l2"], xin, W[MIXER_NAMES[i][1]], _row(W["ln_mix_g"], i),
                                       _row(W["ln_mix_b"], i), dep=tok)
    W = comm.weights(("ffn", i), x1)
    s["gate"], s["up"], s["hb"], s["x1b"] = ffn_up(x1, W["ffn_w_gate%d" % i], W["ffn_w_up%d" % i], None)
    x2, s["xh2"], s["rs2"] = ffn_down_ln(s["hb"], x1, W["ffn_w_down%d" % i], None,
                                         _row(W["ln_ffn_g"], i), _row(W["ln_ffn_b"], i))
    tok = comm.poke(("ffn", i), x2)
    outs = ple_fwd(x2, p_i, W["ple_w_gate%d" % i], W["ple_w_proj%d" % i], None, _row(W["ple_b_gate"], i), target,
                   dep=tok)
    s["sg"], s["pp"], s["x2b"], s["pb"] = outs[1:5]
    return outs[0], s, outs[5:]


def bwd_layer(i, dx, s, W, comm, tok=None):
    small = {}
    D = dx.shape[1]
    FP = W["ffn_w_gate%d" % i].shape[1]
    dx2, dgp_b, dpp_b, small["ple_b_gate"] = ple_bwd(dx, s["sg"], s["pp"], W["ple_w_gate%d" % i], None, dep=tok)
    dr2, dr2_b, dgate_b, dup_b, small["ln_ffn_g"], small["ln_ffn_b"] = ffn_bwd_a(
        dx2, s["xh2"], s["rs2"], _row(W["ln_ffn_g"], i), s["gate"], s["up"], W["ffn_w_down%d" % i], None)
    tok = comm.grads(_tn_group({
        "ple_w_gate%d" % i: ((D, D), [(s["x2b"], dgp_b, 0, 0)]),
        "ple_w_proj%d" % i: ((s["pb"].shape[1], D), [(s["pb"], dpp_b, 0, 0)]),
        "ffn_w_down%d" % i: ((FP, D), [(s["hb"], dr2_b, 0, 0)]),
        "ffn_w_gate%d" % i: ((D, FP), [(s["x1b"], dgate_b, 0, 0)]),
        "ffn_w_up%d" % i: ((D, FP), [(s["x1b"], dup_b, 0, 0)])}))
    dx1 = ffn_bwd_b(dr2, dgate_b, dup_b, W["ffn_w_gate%d" % i], W["ffn_w_up%d" % i], None, dep=tok)
    iname, oname = MIXER_NAMES[i]
    dr1, dmix_b, dl, small["ln_mix_g"], small["ln_mix_b"] = mix_bwd(
        dx1, s["xh1"], s["rs1"], _row(W["ln_mix_g"], i), W[oname])
    tok = comm.poke(("bwd", i), dl)
    if i == 1:
        (dy, dzc_b, small["conv_ln_g"], small["conv_ln_b"], small["sg_ln_g"], small["sg_ln_b"],
         small["sg_w"], dsb) = odd_post_bwd(dl, s["h"], s["xhc"], s["rsc"], s["xhv"], s["rsv"], s["sv"],
                                            W["conv_ln_g"], W["conv_ln_b"], W["sg_ln_g"], W["sg_ln_b"], W["sg_w"],
                                            dep=tok)
        small["sg_b"] = dsb[:, :, 0]
        da_b, dg_b, small["conv_dw"] = conv_bwd(dy, s["hc"], s["h"], W["conv_dw"])
        pieces = [(da_b, 0), (dg_b, 512), (dzc_b, 1024)]
    else:
        dq_b, dk_b, dv_b = attn_bwd(s["qkv"], dl, s["tb"], dep=tok)
        du_b, small["pool_w"], small["pool_scale"] = pool_bwd(dl, s["pooled"], W["pool_w"], W["pool_scale"])
        pieces = [(dq_b, 0), (dk_b, 512), (dv_b, 1024), (du_b, 1536)]
    dxin = dx_in(dr1, pieces, W[iname])
    tok = comm.grads(_tn_group({
        oname: ((1024, D), [(s["l1"], dmix_b, 0, 0), (s["l2"], dmix_b, 512, 0)]),
        iname: ((D, 2048), [(s["xb"], a, 0, off) for a, off in pieces])}))
    return dxin, small, tok


def run_layers(x, p, target, comm):
    saved, xin = [], x
    for i in range(2):
        xin, s, extra = fwd_layer(i, xin, p[i], target if i == 1 else None, comm)
        saved.append(s)
    dx, sq = extra
    W = comm.all_weights()
    per_layer = [None, None]
    tok = None
    for i in (1, 0):
        dx, per_layer[i], tok = bwd_layer(i, dx, saved[i], W, comm, tok)
    small = {}
    for k in ("ln_mix_g", "ln_mix_b", "ln_ffn_g", "ln_ffn_b", "ple_b_gate"):
        small[k] = jnp.concatenate([per_layer[0][k], per_layer[1][k]], axis=0)
    for i in range(2):
        small.update({k: v for k, v in per_layer[i].items() if k not in small})
    return sq, dx, small


def _big_table():
    t = {}
    for nm in ("even", "odd"):
        t[nm + "_w_in"] = ((1024, 2048), 1, 256, 256, nm + "_w_in", 0)
        t[nm + "_w_out"] = ((1024, 1024), 0, 128, 128, nm + "_w_out", 0)
    for l in range(2):
        t["ffn_w_gate%d" % l] = ((1024, 8 * FF_PAD), 1, FF_PAD, FF_SHARD, "ffn_w_gate", l)
        t["ffn_w_up%d" % l] = ((1024, 8 * FF_PAD), 1, FF_PAD, FF_SHARD, "ffn_w_up", l)
        t["ffn_w_down%d" % l] = ((8 * FF_PAD, 1024), 0, FF_PAD, FF_SHARD, "ffn_w_down", l)
        t["ple_w_gate%d" % l] = ((1024, 1024), 0, 128, 128, "ple_w_gate", l)
        t["ple_w_proj%d" % l] = ((256, 1024), 1, 128, 128, "ple_w_proj", l)
    return t


BIG = _big_table()
TRANSPOSED_ARGS = ("ffn_w_gate", "ffn_w_up")
SMALL_SPEC = ((N_DEV, 40, 64), 0, 1, 1)
_LAYER_GROUP = lambda l: ["ffn_w_gate%d" % l, "ffn_w_up%d" % l, "ffn_w_down%d" % l, "ple_w_gate%d" % l, "ple_w_proj%d" % l]
AG_GROUPS = (["even_w_in"], ["even_w_out"], _LAYER_GROUP(0), ["odd_w_in", "odd_w_out", "small"], _LAYER_GROUP(1))
AG_NEED = {("mix", 0): 0, ("out", 0): 1, ("ffn", 0): 2, ("mix", 1): 3, ("ffn", 1): 4}
AG_PASS = {("in", 0): 1, ("mixed", 0): 2, ("ffn", 0): 3, ("mixed", 1): 4}
ANY = pl.BlockSpec(memory_space=pl.ANY)
SEM = pl.BlockSpec(memory_space=pltpu.SEMAPHORE)


def _spec(name):
    return SMALL_SPEC if name == "small" else BIG[name]


def _win_shape(spec):
    full, axis, w = spec[:3]
    return tuple(w if d == axis else n for d, n in enumerate(full))


def _window(ref, axis, w, j):
    idx = [slice(None)] * len(ref.shape)
    idx[axis] = pl.ds(j, 1) if w == 1 else pl.ds(pl.multiple_of(j * w, w), w)
    return ref.at[tuple(idx)]


def _mesh_pos():
    return lax.axis_index("x"), lax.axis_index("y"), lax.axis_index("c")


def split_call(name, arrays, starts=(), waits=(), sems_in=(), new=(), after=None):
    n, nn, ns = len(arrays), len(new), len(starts)
    flat_sems = [s for pair in sems_in for s in pair]

    def body(*refs):
        arr = list(refs[:n])
        sin = refs[n:n + len(flat_sems)]
        outs = refs[n + len(flat_sems) + (after is not None):]
        data = arr + list(outs[n:n + nn])
        for p, k, kind, mk in waits:
            d = mk(data, sin[2 * p].at[k], sin[2 * p + 1].at[k])
            d.wait_send() if kind == "send" else d.wait_recv()
        if ns:
            send, recv = outs[n + nn], outs[n + nn + 1]
            for k, mk in enumerate(starts):
                mk(data, send.at[k], recv.at[k]).start()
        outs[-1][...] = jnp.zeros((8, 128), F32)

    sem_out = [pltpu.SemaphoreType.DMA((ns,)), pltpu.SemaphoreType.DMA((ns,))] if ns else []
    res = pl.pallas_call(
        body, name=name,
        in_specs=[ANY] * n + [SEM] * len(flat_sems) + ([ANY] if after is not None else []),
        out_specs=[ANY] * (n + nn) + [SEM] * len(sem_out) + [pl.BlockSpec(memory_space=pltpu.VMEM)],
        out_shape=[_sds(a.shape, a.dtype) for a in arrays] + list(new) + sem_out + [_sds((8, 128), F32)],
        input_output_aliases={a: a for a in range(n)},
        compiler_params=pltpu.CompilerParams(has_side_effects=pltpu.SideEffectType.DATAFLOW_SIDE_EFFECTING),
    )(*arrays, *flat_sems, *([after] if after is not None else []))
    return list(res[:n + nn]), (tuple(res[n + nn:n + nn + 2]) if ns else None), res[-1]


def _remote(src, dst, send_sem, recv_sem, dev):
    return pltpu.make_async_remote_copy(src_ref=src, dst_ref=dst, send_sem=send_sem, recv_sem=recv_sem,
                                        device_id=dev, device_id_type=MESH_T)


class Gatherer:
    def __init__(self, groups, arrays, specs, prefix):
        self.groups, self.specs, self.prefix = groups, specs, prefix
        self.names = [nm for g in groups for nm in g]
        self.arr = dict(zip(self.names, arrays))
        self.fwd_sems = {}
        self.forwarded = set()

    @staticmethod
    def _mk_first(ai, spec, k):
        def mk(refs, ss, rs):
            x, y, c = _mesh_pos()
            dev = [(x, y, 1 - c), (1 - x, y, c), (x, 1 - y, c), (1 - x, 1 - y, c)][k]
            win = _window(refs[ai], spec[1], spec[2], 4 * x + 2 * y + c)
            return _remote(win, win, ss, rs, dev)
        return mk

    @staticmethod
    def _mk_fwd(ai, spec, j):
        def mk(refs, ss, rs):
            x, y, c = _mesh_pos()
            px, py = [(1 - x, y), (x, 1 - y), (1 - x, 1 - y)][j]
            win = _window(refs[ai], spec[1], spec[2], 4 * px + 2 * py + c)
            return _remote(win, win, ss, rs, (x, y, 1 - c))
        return mk

    def start(self, after=None):
        starts = [self._mk_first(ai, self.specs[nm], k) for ai, nm in enumerate(self.names) for k in range(4)]
        arrs, self.first_sems, tok = split_call(self.prefix + "_start", [self.arr[nm] for nm in self.names],
                                                starts=starts, after=after)
        self.arr = dict(zip(self.names, arrs))
        return tok

    def forward(self, g, after=None):
        if g in self.forwarded:
            return None
        self.forwarded.add(g)
        names = self.groups[g]
        waits = [(0, 4 * self.names.index(nm) + 1 + j, "recv", self._mk_fwd(ai, self.specs[nm], j))
                 for ai, nm in enumerate(names) for j in range(3)]
        starts = [self._mk_fwd(ai, self.specs[nm], j) for ai, nm in enumerate(names) for j in range(3)]
        arrs, self.fwd_sems[g], tok = split_call(
            "%s_forward%d" % (self.prefix, g), [self.arr[nm] for nm in names], starts=starts, waits=waits,
            sems_in=[self.first_sems], after=after)
        self.arr.update(zip(names, arrs))
        return tok

    def finish(self, g, after=None):
        self.forward(g, after)
        names = self.groups[g]
        waits = []
        for ai, nm in enumerate(names):
            base = 4 * self.names.index(nm)
            waits.append((0, base, "recv", self._mk_first(ai, self.specs[nm], 0)))
            waits += [(1, 3 * ai + j, "recv", self._mk_fwd(ai, self.specs[nm], j)) for j in range(3)]
            waits += [(0, base + k, "send", self._mk_first(ai, self.specs[nm], k)) for k in range(4)]
            waits += [(1, 3 * ai + j, "send", self._mk_fwd(ai, self.specs[nm], j)) for j in range(3)]
        arrs, _, _ = split_call(
            "%s_finish%d" % (self.prefix, g), [self.arr[nm] for nm in names], waits=waits,
            sems_in=[self.first_sems, self.fwd_sems[g]], after=after)
        self.arr.update(zip(names, arrs))
        return {nm: self.arr[nm] for nm in names}


class Reducer:
    def __init__(self, cq_arr, adam):
        self.cq_arr, self.adam = cq_arr, adam
        self.groups = []
        self.n = 0
        self.last = None

    @staticmethod
    def _mk1(gi, li, spec, q):
        def mk(refs, ss, rs):
            x, y, c = _mesh_pos()
            return _remote(_window(refs[gi], spec[1], spec[2], 2 * q + (1 - c)), refs[li].at[q], ss, rs, (x, y, 1 - c))
        return mk

    @staticmethod
    def _mk2(si, li, d):
        def mk(refs, ss, rs):
            x, y, c = _mesh_pos()
            qd = lax.rem(2 * x + y + d, 4)
            return _remote(refs[si].at[d - 1], refs[li].at[3 - d], ss, rs, (lax.div(qd, 2), lax.rem(qd, 2), c))
        return mk

    def add(self, grads, after=None):
        names = list(grads)
        m = len(names)
        starts = [self._mk1(ai, m + ai, BIG[nm], q) for ai, nm in enumerate(names) for q in range(4)]
        new = [_sds((4,) + _win_shape(BIG[nm]), BF16) for nm in names]
        res, sems, tok = split_call("rs1_start%d" % self.n, [grads[nm] for nm in names], starts=starts, new=new,
                                    after=after)
        self.groups.append(dict(names=names, starts=starts, buf=res, sems=sems, stage=1, idx=self.n))
        self.n += 1
        return tok

    def step(self, after):
        tok = None
        for grp in self.groups:
            names, m = grp["names"], len(grp["names"])
            if grp["stage"] == 1:
                waits = [(0, k, kind, mk) for k, mk in enumerate(grp["starts"]) for kind in ("send", "recv")]
                res, _, _ = split_call("rs1_wait%d" % grp["idx"], grp["buf"], waits=waits, sems_in=[grp["sems"]], after=after)
                full, land1 = res[:m], res[m:]
                s1b = [add_pairs(g, l, BIG[nm], self.cq_arr) for nm, g, l in zip(names, full, land1)]
                starts = [self._mk2(ai, m + ai, d) for ai in range(m) for d in (1, 2, 3)]
                new = [_sds(a.shape, BF16) for a in s1b]
                res, sems, tok = split_call("rs2_start%d" % grp["idx"], s1b, starts=starts, new=new, after=tok)
                grp.update(stage=2, g=full, land1=land1, starts=starts, buf=res, sems=sems)
        return tok

    def finish_oldest(self):
        for grp in self.groups:
            if grp["stage"] == 2:
                names, m = grp["names"], len(grp["names"])
                waits = [(0, k, kind, mk) for k, mk in enumerate(grp["starts"]) for kind in ("send", "recv")]
                res, _, _ = split_call("rs2_wait%d" % grp["idx"], grp["buf"], waits=waits, sems_in=[grp["sems"]],
                                       after=self.last)
                for nm, g, l1, l2 in zip(names, grp["g"], grp["land1"], res[m:]):
                    self.last = self.adam(nm, g, l1, l2, self.last)
                grp["stage"] = 3
                return True
        return False


def pack_weights(args, arg_names, small_blk, names, j_arr):
    n_in = len(args)

    def body(j_ref, *refs):
        for o, nm in enumerate(names):
            dst = refs[n_in + 1 + o]
            if nm == "small":
                dst[...] = refs[n_in][...]
                continue
            _, axis, w, valid, arg, layer = BIG[nm]
            if arg in TRANSPOSED_ARGS:
                s = refs[arg_names.index(arg)][layer]
                s = jnp.concatenate([s, jnp.zeros((w - valid, s.shape[1]), F32)], axis=0)
                dst[...] = s.T.astype(BF16)
                continue
            src = refs[arg_names.index(arg)][layer].astype(BF16)
            if valid == w:
                dst[...] = src
            else:
                dst[...] = jnp.zeros(dst.shape, BF16)
                if axis == 1:
                    dst[:, 0:valid] = src
                else:
                    dst[0:valid, :] = src

    def ispec(a):
        return pl.BlockSpec(a.shape, lambda i, j_ref: (0, 0, 0))

    def ospec(spec):
        axis, nd = spec[1], len(spec[0])
        return pl.BlockSpec(_win_shape(spec),
                            lambda i, j_ref, axis=axis, nd=nd: tuple(j_ref[0] if d == axis else 0 for d in range(nd)))

    specs = [_spec(nm) for nm in names]
    return pl.pallas_call(
        body, name="pack_weights",
        grid_spec=pltpu.PrefetchScalarGridSpec(
            num_scalar_prefetch=1, grid=(1,),
            in_specs=[ispec(a) for a in list(args) + [small_blk]], out_specs=[ospec(s) for s in specs]),
        out_shape=[_sds(s[0], F32 if nm == "small" else BF16) for nm, s in zip(names, specs)],
        compiler_params=_cp(("arbitrary",)),
    )(j_arr, *args, small_blk)


def add_pairs(full, land, spec, cq_arr):
    axis, w = spec[1], spec[2]
    R, C = full.shape

    def chip(d, cq):
        return lax.rem(cq[1] + d + 1, 4)

    if axis == 1:
        tr = R
        grid = (3, R // tr)
        fspec = pl.BlockSpec((tr, w), lambda d, i, cq: (i, 2 * chip(d, cq) + cq[0]))
        lspec = pl.BlockSpec((None, tr, w), lambda d, i, cq: (chip(d, cq), i, 0))
        ospec = pl.BlockSpec((None, tr, w), lambda d, i, cq: (d, i, 0))
    else:
        grid = (3, 1)
        fspec = pl.BlockSpec((w, C), lambda d, i, cq: (2 * chip(d, cq) + cq[0], 0))
        lspec = pl.BlockSpec((None, w, C), lambda d, i, cq: (chip(d, cq), 0, 0))
        ospec = pl.BlockSpec((None, w, C), lambda d, i, cq: (d, 0, 0))

    def body(cq_ref, a_ref, b_ref, ob_ref):
        ob_ref[...] = (a_ref[...].astype(F32) + b_ref[...].astype(F32)).astype(BF16)

    return pl.pallas_call(
        body, name="add_pairs",
        grid_spec=pltpu.PrefetchScalarGridSpec(
            num_scalar_prefetch=1, grid=grid, in_specs=[fspec, lspec], out_specs=[ospec]),
        out_shape=[_sds((3,) + land.shape[1:], BF16)],
        compiler_params=_cp(("arbitrary",) * 2),
    )(cq_arr, full, land)[0]


def _adamw(w, g, m, v):
    m = ADAM_B1 * m + (1.0 - ADAM_B1) * g
    v = ADAM_B2 * v + (1.0 - ADAM_B2) * (g * g)
    m_hat = m / (1.0 - ADAM_B1 ** ADAM_STEP)
    v_hat = v / (1.0 - ADAM_B2 ** ADAM_STEP)
    delta = -ADAM_LR * (m_hat / (jnp.sqrt(v_hat) + ADAM_EPS) + ADAM_WD * w)
    return delta, m, v


def reduce_adamw(full, land1, land, w, m, v, spec, cq_arr, prev=None, dep=None):
    axis, win, valid, layer = spec[1], spec[2], spec[3], spec[5]
    L, R, C = w.shape
    transposed = spec[4] in TRANSPOSED_ARGS
    if transposed:
        grid = (1,)
        fspec = pl.BlockSpec((C, win), lambda i, cq: (0, 2 * cq[1] + cq[0]))
        wspec = pl.BlockSpec((None, C, win), lambda i, cq: (cq[1], 0, 0))
        lspec = pl.BlockSpec((3, C, win), lambda i, cq: (0, 0, 0))
        sspec = pl.BlockSpec((None, R, C), lambda i, cq: (layer, 0, 0))
    elif axis == 1:
        tr = min(1024, R)
        grid = (R // tr,)
        fspec = pl.BlockSpec((tr, win), lambda i, cq: (i, 2 * cq[1] + cq[0]))
        wspec = pl.BlockSpec((None, tr, win), lambda i, cq: (cq[1], i, 0))
        lspec = pl.BlockSpec((3, tr, win), lambda i, cq: (0, i, 0))
        sspec = pl.BlockSpec((None, tr, C), lambda i, cq: (layer, i, 0))
    else:
        grid = (1,)
        fspec = pl.BlockSpec((win, full.shape[1]), lambda i, cq: (2 * cq[1] + cq[0], 0))
        wspec = pl.BlockSpec((None, win, C), lambda i, cq: (cq[1], 0, 0))
        lspec = pl.BlockSpec((3, win, C), lambda i, cq: (0, 0, 0))
        sspec = pl.BlockSpec((None, R, C), lambda i, cq: (layer, 0, 0))

    def body(cq_ref, full_ref, own_ref, land_ref, w_ref, m_ref, v_ref, *rest):
        g_ref, d_ref, nm_ref, nv_ref = rest[-4:]
        if transposed:
            rd = lambda r, *lead: r[lead] if lead else r[...]
        elif axis == 1:
            rd = lambda r, *lead: r[(*lead, slice(None), slice(0, valid))]
        else:
            rd = lambda r, *lead: r[(*lead, slice(0, valid), slice(None))]
        g = rd(full_ref).astype(F32) + rd(own_ref).astype(F32)
        for k in range(3):
            g = g + rd(land_ref, k).astype(F32)
        if transposed:
            g = g.T[0:valid, :]
        g_ref[...] = g
        d, nm, nv = _adamw(w_ref[...], g, m_ref[...], v_ref[...])
        d_ref[...] = d
        nm_ref[...] = nm
        nv_ref[...] = nv

    extra = (list(prev) if prev is not None else []) + ([dep] if dep is not None else [])
    return pl.pallas_call(
        body, name="reduce_adamw",
        grid_spec=pltpu.PrefetchScalarGridSpec(
            num_scalar_prefetch=1, grid=grid,
            in_specs=[fspec, wspec, lspec, sspec, sspec, sspec] + [ANY] * len(extra), out_specs=[sspec] * 4),
        out_shape=[_sds(w.shape)] * 4,
        input_output_aliases={7 + k: k for k in range(4 if prev is not None else 0)},
        compiler_params=_cp(("arbitrary",)),
    )(cq_arr, full, land1, land, w, m, v, *extra)


def place_slot(packed, j_arr):
    R = packed.shape[0]

    def body(j_ref, src, dst):
        dst[...] = src[...]

    return pl.pallas_call(
        body, name="place_slot",
        grid_spec=pltpu.PrefetchScalarGridSpec(
            num_scalar_prefetch=1, grid=(1,),
            in_specs=[pl.BlockSpec((R, 128), lambda i, j: (0, 0))],
            out_specs=[pl.BlockSpec((None, R, 128), lambda i, j: (j[0], 0, 0))]),
        out_shape=[_sds((N_DEV, R, 128))], compiler_params=_cp(("arbitrary",)),
    )(j_arr, packed)[0]


def sum_slots(gathered):
    def body(g_ref, o_ref):
        g = g_ref[0]
        for dev in range(1, N_DEV):
            g = g + g_ref[dev]
        o_ref[...] = g

    return pl.pallas_call(body, name="sum_slots", out_shape=_sds(gathered.shape[1:]), compiler_params=_cp())(gathered)


def small_adamw(gs, wmv):
    k = len(gs)

    def body(*refs):
        for a in range(k):
            g, w, m, v = refs[4 * a:4 * a + 4]
            d, nm, nv = _adamw(w[...], g[...], m[...], v[...])
            refs[4 * k + 3 * a][...] = d
            refs[4 * k + 3 * a + 1][...] = nm
            refs[4 * k + 3 * a + 2][...] = nv

    args = [t for g, tup in zip(gs, wmv) for t in (g,) + tuple(tup)]
    out_shape = [_sds(g.shape) for g in gs for _ in range(3)]
    return pl.pallas_call(body, name="small_adamw", out_shape=out_shape, compiler_params=_cp())(*args)


WEIGHT_NAMES = ("even_w_in", "even_w_out", "pool_w", "pool_scale", "odd_w_in", "odd_w_out", "conv_dw", "conv_ln_g",
                "conv_ln_b", "sg_ln_g", "sg_ln_b", "sg_w", "sg_b", "ln_mix_g", "ln_mix_b", "ffn_w_gate", "ffn_w_up",
                "ffn_w_down", "ln_ffn_g", "ln_ffn_b", "ple_w_proj", "ple_w_gate", "ple_b_gate")
PACK_ARGS = ("even_w_in", "even_w_out", "odd_w_in", "odd_w_out", "ffn_w_gate", "ffn_w_up", "ffn_w_down",
             "ple_w_gate", "ple_w_proj")
REPLICATED = ("pool_w", "pool_scale", "sg_w", "sg_b", "ln_mix_g", "ln_mix_b", "ln_ffn_g", "ln_ffn_b", "ple_b_gate")
SHARDED_SMALL = ("conv_dw", "conv_ln_g", "conv_ln_b", "sg_ln_g", "sg_ln_b")
NATURAL = {"pool_w": (4, 128, 128), "pool_scale": (1, 512), "sg_w": (4, 128, 128), "sg_b": (4, 128),
           "ln_mix_g": (2, 1024), "ln_mix_b": (2, 1024), "ln_ffn_g": (2, 1024), "ln_ffn_b": (2, 1024),
           "ple_b_gate": (2, 1024)}


def kernel(x, p, even_w_in, even_w_out, pool_w, pool_scale, odd_w_in, odd_w_out, conv_dw, conv_ln_g, conv_ln_b, sg_ln_g, sg_ln_b, sg_w, sg_b, ln_mix_g, ln_mix_b, ffn_w_gate, ffn_w_up, ffn_w_down, ln_ffn_g, ln_ffn_b, ple_w_proj, ple_w_gate, ple_b_gate, loss_target, m_even_w_in, m_even_w_out, m_pool_w, m_pool_scale, m_odd_w_in, m_odd_w_out, m_conv_dw, m_conv_ln_g, m_conv_ln_b, m_sg_ln_g, m_sg_ln_b, m_sg_w, m_sg_b, m_ln_mix_g, m_ln_mix_b, m_ffn_w_gate, m_ffn_w_up, m_ffn_w_down, m_ln_ffn_g, m_ln_ffn_b, m_ple_w_proj, m_ple_w_gate, m_ple_b_gate, v_even_w_in, v_even_w_out, v_pool_w, v_pool_scale, v_odd_w_in, v_odd_w_out, v_conv_dw, v_conv_ln_g, v_conv_ln_b, v_sg_ln_g, v_sg_ln_b, v_sg_w, v_sg_b, v_ln_mix_g, v_ln_mix_b, v_ffn_w_gate, v_ffn_w_up, v_ffn_w_down, v_ln_ffn_g, v_ln_ffn_b, v_ple_w_proj, v_ple_w_gate, v_ple_b_gate):
    A = dict(locals())
    for arg in TRANSPOSED_ARGS:
        for pre in ("", "m_", "v_"):
            A[pre + arg] = jnp.swapaxes(A[pre + arg], 1, 2)
    mx, my, mc = _mesh_pos()
    j = 4 * mx + 2 * my + mc
    j_arr = j.astype(jnp.int32).reshape(1)
    cq_arr = jnp.stack([mc, 2 * mx + my]).astype(jnp.int32)
    res = {}

    def adam(nm, full, land1, land2, dep):
        arg = BIG[nm][4]
        res[arg] = reduce_adamw(full, land1, land2, A[arg], A["m_" + arg], A["v_" + arg], BIG[nm], cq_arr,
                                res.get(arg), dep)
        return res[arg][0]

    class Comm:
        def __init__(self):
            names = [nm for g in AG_GROUPS for nm in g]
            small_blk = jnp.concatenate([conv_dw[0], conv_ln_g, conv_ln_b, sg_ln_g, sg_ln_b, jnp.zeros((5, 64), F32)], axis=0)
            mine = pack_weights([A[k] for k in PACK_ARGS], PACK_ARGS, small_blk[None], names, j_arr)
            self.gat = Gatherer(AG_GROUPS, mine, {nm: _spec(nm) for nm in names}, "ag")
            self.gat.start()
            self.red = Reducer(cq_arr, adam)
            self.W = {k: A[k].reshape(NATURAL[k]) for k in REPLICATED}

        def weights(self, stage, after):
            if stage in AG_NEED:
                got = self.gat.finish(AG_NEED[stage], after)
                if "small" in got:
                    sm = got.pop("small").transpose(1, 0, 2).reshape(40, 512)
                    got.update(conv_dw=sm[0:31], conv_ln_g=sm[31:32], conv_ln_b=sm[32:33], sg_ln_g=sm[33:34],
                               sg_ln_b=sm[34:35])
                self.W.update(got)
            return self.W

        def all_weights(self):
            return self.W

        def poke(self, tag, after):
            if tag in AG_PASS:
                return self.gat.forward(AG_PASS[tag], after)
            if tag[0] == "bwd":
                return self.red.step(after)
            return None

        def grads(self, grads):
            tok = self.red.step(next(iter(grads.values())))
            return self.red.add(grads, after=tok)

    comm = Comm()
    sq, dx, small = run_layers(x[0], p[:, 0], loss_target[0], comm)
    loss = lax.psum(0.5 * jnp.sum(sq) / x.shape[-1], ("x", "y", "c"))
    red = comm.red
    tok = red.step(dx)

    names = REPLICATED + SHARDED_SMALL
    flat = jnp.concatenate([small[k].reshape(-1) for k in names])
    rows = -(-flat.shape[0] // 1024) * 8
    packed = jnp.pad(flat, (0, rows * 128 - flat.shape[0])).reshape(rows, 128)
    sg = Gatherer((["g"],), [place_slot(packed, j_arr)], {"g": ((N_DEV, rows, 128), 0, 1, 1)}, "sg")
    sg.start(after=tok)
    red.finish_oldest()
    red.finish_oldest()
    sg.forward(0, after=red.last)
    red.finish_oldest()
    gsum_flat = sum_slots(sg.finish(0, after=red.last)["g"]).reshape(-1)
    gs, off = [], 0
    for k in names:
        n = math.prod(small[k].shape)
        g = gsum_flat[off:off + n].reshape(small[k].shape)
        off += n
        if k in SHARDED_SMALL:
            g = lax.dynamic_slice_in_dim(g, j * 64, 64, axis=1)
        gs.append(g.reshape(A[k].shape))
    outs = small_adamw(gs, [(A[k], A["m_" + k], A["v_" + k]) for k in names])
    for a, k in enumerate(names):
        res[k] = (gs[a],) + tuple(outs[3 * a:3 * a + 3])
    red.last = outs[0]
    while red.finish_oldest():
        pass

    for arg in TRANSPOSED_ARGS:
        res[arg] = [jnp.swapaxes(t, 1, 2) for t in res[arg]]
    out = [loss, dx[None]]
    for part in range(4):
        out += [res[k][part] for k in WEIGHT_NAMES]
    return tuple(out)
```

```python
import functools
import math

import jax
import jax.numpy as jnp
from jax import lax
from jax.experimental import pallas as pl
from jax.experimental.pallas import tpu as pltpu

F32, BF16 = jnp.float32, jnp.bfloat16
ALPHA = 4.0 ** 0.25
LN_EPS = 1e-5
QK_SCALE = 0.125
POOL_WINDOWS = (2, 4, 8, 16)
CONV_TAPS = 31
N_DEV = 8
FF_SHARD, FF_PAD = 352, 384
ADAM_LR, ADAM_B1, ADAM_B2, ADAM_EPS, ADAM_WD, ADAM_STEP = 0.001, 0.9, 0.999, 1e-08, 0.01, 10
VMEM_LIMIT = 56 * 1024 * 1024
MESH_T = pl.DeviceIdType.MESH


def _cp(sem=None):
    return pltpu.CompilerParams(dimension_semantics=sem, vmem_limit_bytes=VMEM_LIMIT)


def _dot(a, b):
    return jnp.dot(a, b, preferred_element_type=F32)


def _dot_nt(a, b):
    return lax.dot_general(a, b, (((1,), (1,)), ((), ())), preferred_element_type=F32)


def _dot_tn(a, b):
    return lax.dot_general(a, b, (((0,), (0,)), ((), ())), preferred_element_type=F32)


def _sigmoid(x):
    return 1.0 / (1.0 + jnp.exp(-x))


def _softplus(z):
    return jnp.maximum(z, 0.0) + jnp.log(1.0 + jnp.exp(-jnp.abs(z)))


_GELU_C = math.sqrt(2.0 / math.pi)


def _gelu(x):
    return 0.5 * x * (1.0 + jnp.tanh(_GELU_C * (x + 0.044715 * x * x * x)))


def _gelu_grad(x):
    t = jnp.tanh(_GELU_C * (x + 0.044715 * x * x * x))
    return 0.5 * (1.0 + t) + 0.5 * x * (1.0 - t * t) * _GELU_C * (1.0 + 3.0 * 0.044715 * x * x)


def _ln_fwd(r, g, b):
    mu = jnp.mean(r, axis=-1, keepdims=True)
    xc = r - mu
    var = jnp.mean(xc * xc, axis=-1, keepdims=True)
    rstd = lax.rsqrt(var + LN_EPS)
    xh = xc * rstd
    return xh * g + b, xh, rstd


def _ln_bwd(dy, xh, rstd, g):
    dxh = dy * g
    m1 = jnp.mean(dxh, axis=-1, keepdims=True)
    m2 = jnp.mean(dxh * xh, axis=-1, keepdims=True)
    return rstd * (dxh - m1 - xh * m2)


def _split2(x):
    hi = x.astype(BF16)
    lo = (x - hi.astype(F32)).astype(BF16)
    return hi, lo


def _colsum(x):
    return jnp.sum(x, axis=0, keepdims=True)


def _tok_call(name, body, tiled, full, out_tiled, out_acc=(), tm=256, scratch=(), dep=None):
    def arr(t):
        return t[0] if isinstance(t, tuple) else t
    full = [t[0] if isinstance(t, tuple) and t[1] is None else t for t in full]
    S = arr(tiled[0]).shape[0]
    tm = min(tm, S)
    n_in = len(tiled) + len(full)
    deps = [] if dep is None else [dep]
    if deps:
        inner = body
        body = lambda *refs: inner(*refs[:n_in], *refs[n_in + 1:])

    def tspec(t):
        if isinstance(t, tuple):
            _, w, cb = t
            return pl.BlockSpec((tm, w), lambda i, cb=cb: (i, cb))
        return pl.BlockSpec((tm, t.shape[1]), lambda i: (i, 0))

    def fspec(t):
        if isinstance(t, tuple):
            a, l = t
            nd = a.ndim - 1
            return pl.BlockSpec((None,) + a.shape[1:], lambda i, l=l, nd=nd: (l,) + (0,) * nd)
        nd = t.ndim
        return pl.BlockSpec(t.shape, lambda i, nd=nd: (0,) * nd)

    def ospec(o):
        return pl.BlockSpec((tm, o.shape[1]), lambda i: (i, 0))

    def aspec(o):
        nd = len(o.shape)
        return pl.BlockSpec(o.shape, lambda i, nd=nd: (0,) * nd)

    outs = pl.pallas_call(
        body, name=name, grid=(S // tm,),
        in_specs=[tspec(t) for t in tiled] + [fspec(t) for t in full] + [ANY] * len(deps),
        out_specs=[ospec(o) for o in out_tiled] + [aspec(o) for o in out_acc],
        out_shape=list(out_tiled) + list(out_acc),
        scratch_shapes=list(scratch),
        compiler_params=_cp(("arbitrary",)),
    )(*[arr(t) for t in tiled], *[arr(t) for t in full], *deps)
    return outs


def _sds(shape, dtype=F32):
    return jax.ShapeDtypeStruct(tuple(shape), dtype)


def _acc(ref, val):
    @pl.when(pl.program_id(0) == 0)
    def _():
        ref[...] = val

    @pl.when(pl.program_id(0) != 0)
    def _():
        ref[...] += val


def mm_in(x, w, nb16=0):
    S, N = x.shape[0], w.shape[1]

    def body(x_ref, w_ref, h_ref, xb_ref, *hb_ref):
        xb = x_ref[...].astype(BF16)
        xb_ref[...] = xb
        h = _dot(xb, w_ref[...])
        h_ref[...] = h
        if nb16:
            hb_ref[0][...] = h[:, 0:nb16].astype(BF16)

    outs = [_sds((S, N)), _sds((S, x.shape[1]), BF16)] + ([_sds((S, nb16), BF16)] if nb16 else [])
    return _tok_call("mm_in", body, [x], [w], outs, tm=512)


def _stack_heads(x, hm0, dtype=BF16):
    return jnp.concatenate([jnp.where(hm0, x, 0), jnp.where(hm0, 0, x)], axis=0).astype(dtype)


def _unstack_k(x, T):
    return jnp.concatenate([x[0:T], x[T:2 * T]], axis=1)


def _cumsum_mm(x, u):
    n = x.shape[0]
    hi, lo = _split2(x)
    r = _dot(jnp.concatenate([hi, lo], axis=0), u)
    return r[0:n] + r[n:2 * n]


def attn_fwd(qkv, T=256):
    S = qkv.shape[0]
    T = min(T, S)
    nq = S // T

    def body(q_ref, k_ref, v_ref, o_ref, t_ref, acc_ref, c_ref, qh_ref):
        i = pl.program_id(0)
        hm0 = lax.broadcasted_iota(jnp.int32, (1, 128), 1) < 64
        r2 = lax.broadcasted_iota(jnp.int32, (2 * T, T), 0)
        c2 = lax.broadcasted_iota(jnp.int32, (2 * T, T), 1)
        causal = c2 < jnp.where(r2 >= T, r2 - T, r2)
        ur = lax.broadcasted_iota(jnp.int32, (T, T), 0)
        uc = lax.broadcasted_iota(jnp.int32, (T, T), 1)
        u_incl = (ur >= uc).astype(BF16)
        acc_ref[...] = jnp.zeros_like(acc_ref)
        c_ref[...] = jnp.zeros_like(c_ref)
        for pp in range(4):
            qh_ref[pp] = _stack_heads(q_ref[:, pp * 128:(pp + 1) * 128] * QK_SCALE, hm0)

        def block(kb, diag):
            ks = pl.multiple_of(kb * T, T)
            cols = [slice(pp * 128, (pp + 1) * 128) for pp in range(4)]
            zs = [_dot_nt(qh_ref[pp], k_ref[pl.ds(ks, T), cols[pp]]) for pp in range(4)]
            incls = []
            for pp in range(4):
                sp = _softplus(zs[pp])
                if diag:
                    sp = jnp.where(causal, sp, 0.0)
                incls.append(_cumsum_mm(sp, u_incl))
            for pp in range(4):
                c = c_ref[pp]
                w = jnp.exp(zs[pp] - incls[pp] - c)
                if diag:
                    w = jnp.where(causal, w, 0.0)
                acc_ref[:, cols[pp]] += _dot(_unstack_k(w.astype(BF16), T),
                                             _stack_heads(v_ref[pl.ds(ks, T), cols[pp]], hm0))
                c_ref[pp] = c + jnp.broadcast_to(incls[pp][:, 0:1], (2 * T, T))

        block(i, True)

        def step(jj, carry):
            block(i - 1 - jj, False)
            return carry

        lax.fori_loop(0, i, step, 0)
        o_ref[...] = acc_ref[...].astype(BF16)
        for pp in range(4):
            for hd in range(2):
                t_ref[2 * pp + hd] = c_ref[pp, hd * T:(hd + 1) * T, 0:128]

    return pl.pallas_call(
        body, name="attn_fwd", grid=(nq,),
        in_specs=[pl.BlockSpec((T, 512), lambda i: (i, 0)),
                  pl.BlockSpec((S, 512), lambda i: (0, 1)),
                  pl.BlockSpec((S, 512), lambda i: (0, 2))],
        out_specs=[pl.BlockSpec((T, 512), lambda i: (i, 0)),
                   pl.BlockSpec((8, T, 128), lambda i: (0, i, 0))],
        out_shape=[_sds((S, 512), BF16), _sds((8, S, 128))],
        scratch_shapes=[pltpu.VMEM((T, 512), F32), pltpu.VMEM((4, 2 * T, T), F32), pltpu.VMEM((4, 2 * T, 128), BF16)],
        compiler_params=_cp(("arbitrary",)),
    )(qkv, qkv, qkv)


def pool_fwd(h, pool_w, pool_scale, CH=256):
    S = h.shape[0]
    CH = min(CH, S)

    def body(u_ref, w_ref, sc_ref, b_ref, pooled_ref, pad_ref):
        pad_ref[0:16, :] = jnp.zeros((16, 512), F32)
        pad_ref[16:16 + S, :] = u_ref[...]
        for g, win in enumerate(POOL_WINDOWS):
            cs = slice(g * 128, (g + 1) * 128)
            wq = w_ref[g].astype(BF16)
            for ch in range(S // CH):
                base = ch * CH
                acc = pad_ref[16 + base:16 + base + CH, cs]
                for sft in range(1, win):
                    acc = acc + pad_ref[16 + base - sft:16 + base - sft + CH, cs]
                t = base + lax.broadcasted_iota(jnp.int32, (CH, 1), 0)
                cnt = jnp.minimum(t + 1, win).astype(F32)
                pooled = (acc / cnt - pad_ref[16 + base:16 + base + CH, cs]).astype(BF16)
                pooled_ref[base:base + CH, cs] = pooled
                b_ref[base:base + CH, cs] = (_dot(pooled, wq) * sc_ref[:, cs]).astype(BF16)

    return pl.pallas_call(
        body, name="pool_fwd", grid=(1,),
        in_specs=[pl.BlockSpec((S, 512), lambda i: (0, 3)),
                  pl.BlockSpec((4, 128, 128), lambda i: (0, 0, 0)),
                  pl.BlockSpec((1, 512), lambda i: (0, 0))],
        out_specs=[pl.BlockSpec((S, 512), lambda i: (0, 0)), pl.BlockSpec((S, 512), lambda i: (0, 0))],
        out_shape=[_sds((S, 512), BF16), _sds((S, 512), BF16)],
        scratch_shapes=[pltpu.VMEM((S + 16, 512), F32)],
        compiler_params=_cp(("arbitrary",)),
    )(h, pool_w, pool_scale)


def conv_fwd(h, dw, CH=128):
    S = h.shape[0]

    def body(a_ref, g_ref, dw_ref, y_ref, hc_ref, pad_ref):
        hc = a_ref[...] * _sigmoid(g_ref[...])
        hc_ref[...] = hc
        pad_ref[0:32, :] = jnp.zeros((32, 128), F32)
        pad_ref[32:32 + S, :] = hc
        for ch in range(S // CH):
            base = ch * CH + 2
            acc = dw_ref[0:1, :] * pad_ref[base:base + CH, :]
            for k in range(1, CONV_TAPS):
                acc = acc + dw_ref[k:k + 1, :] * pad_ref[base + k:base + k + CH, :]
            y_ref[ch * CH:(ch + 1) * CH, :] = acc

    return pl.pallas_call(
        body, name="conv_fwd", grid=(4,),
        in_specs=[pl.BlockSpec((S, 128), lambda c: (0, c)),
                  pl.BlockSpec((S, 128), lambda c: (0, 4 + c)),
                  pl.BlockSpec((CONV_TAPS, 128), lambda c: (0, c))],
        out_specs=[pl.BlockSpec((S, 128), lambda c: (0, c)), pl.BlockSpec((S, 128), lambda c: (0, c))],
        out_shape=[_sds((S, 512)), _sds((S, 512))],
        scratch_shapes=[pltpu.VMEM((S + 32, 128), F32)],
        compiler_params=_cp(("arbitrary",)),
    )(h, h, dw)


def _masked_sg_w(w_ref, g):
    row = lax.broadcasted_iota(jnp.int32, (128, 128), 0)
    col = lax.broadcasted_iota(jnp.int32, (128, 128), 1)
    return jnp.where(row >= col, w_ref[g], 0.0).astype(BF16)


def odd_post(y, h, cl_g, cl_b, sl_g, sl_b, sg_w, sgb_bc, tm=256):
    S = y.shape[0]
    tm = min(tm, S)

    def body(y_ref, zc_ref, clg, clb, slg, slb, w_ref, sb_ref,
             c_ref, d_ref, xhc_ref, rsc_ref, xhv_ref, rsv_ref, sv_ref):
        lnc, xhc, rsc = _ln_fwd(y_ref[...], clg[...], clb[...])
        c_ref[...] = (lnc * _sigmoid(lnc)).astype(BF16)
        xhc_ref[...] = xhc
        rsc_ref[...] = rsc
        z = _gelu(zc_ref[...])
        vn, xhv, rsv = _ln_fwd(z[:, 512:], slg[...], slb[...])
        xhv_ref[...] = xhv
        rsv_ref[...] = rsv
        vnb = vn.astype(BF16)
        for g in range(4):
            wm = _masked_sg_w(w_ref, g)
            for ch in range(tm // 128):
                rs, cs = slice(ch * 128, (ch + 1) * 128), slice(g * 128, (g + 1) * 128)
                sv_ref[rs, cs] = _dot(wm, vnb[rs, cs]) + sb_ref[g]
        d_ref[...] = (z[:, :512] * sv_ref[...]).astype(BF16)

    return _tok_call(
        "odd_post", body, [y, (h, 1024, 1)], [cl_g, cl_b, sl_g, sl_b, sg_w, sgb_bc],
        [_sds((S, 512), BF16), _sds((S, 512), BF16), _sds((S, 512)), _sds((S, 1)),
         _sds((S, 512)), _sds((S, 1)), _sds((S, 512))], tm=tm)


def mm_out_ln(l1, l2, x, w, g, b, dep=None):
    S, D = x.shape

    def body(l1_ref, l2_ref, x_ref, w_ref, g_ref, b_ref, y_ref, xh_ref, rs_ref):
        mix = _dot(l1_ref[...], w_ref[0:512, :]) + _dot(l2_ref[...], w_ref[512:1024, :])
        y, xh, rs = _ln_fwd(ALPHA * x_ref[...] + mix, g_ref[...], b_ref[...])
        y_ref[...] = y
        xh_ref[...] = xh
        rs_ref[...] = rs

    return _tok_call("mm_out_ln", body, [l1, l2, x], [w, g, b],
                     [_sds((S, D)), _sds((S, D)), _sds((S, 1))], dep=dep)


def ffn_up(x1, wg, wu, layer, dep=None):
    S, D = x1.shape
    F = wg.shape[-1]

    def body(x_ref, wg_ref, wu_ref, gate_ref, up_ref, hb_ref, xb_ref):
        xb = x_ref[...].astype(BF16)
        xb_ref[...] = xb
        gate = _dot(xb, wg_ref[...])
        up = _dot(xb, wu_ref[...])
        gate_ref[...] = gate.astype(BF16)
        up_ref[...] = up.astype(BF16)
        hb_ref[...] = (gate * _sigmoid(gate) * up).astype(BF16)

    return _tok_call("ffn_up", body, [x1], [(wg, layer), (wu, layer)],
                     [_sds((S, F), BF16), _sds((S, F), BF16), _sds((S, F), BF16), _sds((S, D), BF16)], dep=dep)


def ffn_down_ln(hb, x1, wd, layer, g, b):
    S, D = x1.shape

    def body(h_ref, x_ref, w_ref, g_ref, b_ref, y_ref, xh_ref, rs_ref):
        f = _dot(h_ref[...], w_ref[...])
        y, xh, rs = _ln_fwd(ALPHA * x_ref[...] + f, g_ref[...], b_ref[...])
        y_ref[...] = y
        xh_ref[...] = xh
        rs_ref[...] = rs

    return _tok_call("ffn_down_ln", body, [hb, x1], [(wd, layer), g, b],
                     [_sds((S, D)), _sds((S, D)), _sds((S, 1))])


def ple_fwd(x2, p, wpg, wpp, layer, bg, target=None, dep=None):
    S, D = x2.shape
    last = target is not None

    def body(*refs):
        if last:
            x_ref, p_ref, t_ref, wg_ref, wp_ref, b_ref, x3_ref, sg_ref, pp_ref, xb_ref, pb_ref, dy_ref, ls_ref = refs
        else:
            x_ref, p_ref, wg_ref, wp_ref, b_ref, x3_ref, sg_ref, pp_ref, xb_ref, pb_ref = refs
        x = x_ref[...]
        xb = x.astype(BF16)
        pb = p_ref[...].astype(BF16)
        xb_ref[...] = xb
        pb_ref[...] = pb
        sg = _sigmoid(_dot(xb, wg_ref[...]) + b_ref[...])
        pp = _dot(pb, wp_ref[...])
        sg_ref[...] = sg.astype(BF16)
        pp_ref[...] = pp.astype(BF16)
        x3 = x + sg * pp
        x3_ref[...] = x3
        if last:
            err = x3 - t_ref[...]
            dy_ref[...] = err * (1.0 / D)
            _acc(ls_ref, _colsum(err * err))

    outs = [_sds((S, D)), _sds((S, D), BF16), _sds((S, D), BF16), _sds((S, D), BF16), _sds((S, p.shape[1]), BF16)]
    tiled = [x2, p] + ([target] if last else [])
    if last:
        outs.append(_sds((S, D)))
    return _tok_call("ple_fwd", body, tiled, [(wpg, layer), (wpp, layer), bg], outs,
                     [_sds((1, D))] if last else [], dep=dep)


def ple_bwd(dx3, sg, pp, wpg, layer, dep=None):
    S, D = dx3.shape

    def body(d_ref, sg_ref, pp_ref, w_ref, dx_ref, dgp_ref, dpp_ref, dbg_ref):
        d, sg = d_ref[...], sg_ref[...].astype(F32)
        dgp = d * pp_ref[...].astype(F32) * sg * (1.0 - sg)
        dgpb = dgp.astype(BF16)
        dgp_ref[...] = dgpb
        dpp_ref[...] = (d * sg).astype(BF16)
        dx_ref[...] = d + _dot_nt(dgpb, w_ref[...])
        _acc(dbg_ref, _colsum(dgp))

    return _tok_call("ple_bwd", body, [dx3, sg, pp], [(wpg, layer)],
                     [_sds((S, D)), _sds((S, D), BF16), _sds((S, D), BF16)], [_sds((1, D))], dep=dep)


def ln_bwd(dx2, xh, rs, g):
    S, D = dx2.shape

    def body(d_ref, xh_ref, rs_ref, g_ref, dr_ref, drb_ref, dlg_ref, dlb_ref):
        d, xh = d_ref[...], xh_ref[...]
        dr = _ln_bwd(d, xh, rs_ref[...], g_ref[...])
        dr_ref[...] = dr
        drb_ref[...] = dr.astype(BF16)
        _acc(dlg_ref, _colsum(d * xh))
        _acc(dlb_ref, _colsum(d))

    return _tok_call("ln_bwd", body, [dx2, xh, rs], [g], [_sds((S, D)), _sds((S, D), BF16)],
                     [_sds((1, D)), _sds((1, D))])


def ffn_bwd(dr_b, x1b, gate, up, hb, wg, wu, wd, TH=256):
    S, D = dr_b.shape
    F = gate.shape[1]

    def body(dr_hbm, x_hbm, gate_ref, up_ref, hb_ref, wg_ref, wu_ref, wd_ref,
             dx_ref, dwg_ref, dwu_ref, dwd_ref, dr_v, x_v, sem, dg_s, du_s):
        @pl.when(pl.program_id(0) == 0)
        def _():
            c1 = pltpu.make_async_copy(dr_hbm, dr_v, sem.at[0])
            c2 = pltpu.make_async_copy(x_hbm, x_v, sem.at[1])
            c1.start()
            c2.start()
            c1.wait()
            c2.wait()
            dx_ref[...] = jnp.zeros_like(dx_ref)

        for ch in range(S // CH):
            rows = slice(ch * CH, (ch + 1) * CH)
            dh = _dot_nt(dr_v[rows, :], wd_ref[...])
            g, u = gate_ref[rows, :].astype(F32), up_ref[rows, :].astype(F32)
            s = _sigmoid(g)
            dgb = (dh * u * s * (1.0 + g * (1.0 - s))).astype(BF16)
            dub = (dh * g * s).astype(BF16)
            dg_s[rows, :] = dgb
            du_s[rows, :] = dub
            dx_ref[rows, :] += _dot_nt(dgb, wg_ref[...]) + _dot_nt(dub, wu_ref[...])
        x = x_v[...]
        dwg_ref[...] = _dot_tn(x, dg_s[...]).astype(BF16)
        dwu_ref[...] = _dot_tn(x, du_s[...]).astype(BF16)
        dwd_ref[...] = _dot_tn(hb_ref[...], dr_v[...]).astype(BF16)

    CH = min(512, S)
    col = lambda rows: pl.BlockSpec((rows, TH), lambda j: (0, j))
    row = pl.BlockSpec((TH, D), lambda j: (j, 0))
    return pl.pallas_call(
        body, name="ffn_bwd", grid=(F // TH,),
        in_specs=[ANY, ANY, col(S), col(S), col(S), col(D), col(D), row],
        out_specs=[pl.BlockSpec((S, D), lambda j: (0, 0)), col(D), col(D), row],
        out_shape=[_sds((S, D)), _sds((D, F), BF16), _sds((D, F), BF16), _sds((F, D), BF16)],
        scratch_shapes=[pltpu.VMEM((S, D), BF16), pltpu.VMEM((S, D), BF16), pltpu.SemaphoreType.DMA((2,)),
                        pltpu.VMEM((S, TH), BF16), pltpu.VMEM((S, TH), BF16)],
        compiler_params=pltpu.CompilerParams(dimension_semantics=("arbitrary",), vmem_limit_bytes=60 * 1024 * 1024),
    )(dr_b, x1b, gate, up, hb, wg, wu, wd)


def mix_bwd(dxp, dr2, xh, rs, g, w, dep=None):
    S, D = dxp.shape

    def body(dxp_ref, dr2_ref, xh_ref, rs_ref, g_ref, w_ref, dr_ref, dmb_ref, dl_ref, dlg_ref, dlb_ref):
        d, xh = ALPHA * dr2_ref[...] + dxp_ref[...], xh_ref[...]
        dr = _ln_bwd(d, xh, rs_ref[...], g_ref[...])
        drb = dr.astype(BF16)
        dr_ref[...] = dr
        dmb_ref[...] = drb
        dl_ref[...] = _dot_nt(drb, w_ref[...])
        _acc(dlg_ref, _colsum(d * xh))
        _acc(dlb_ref, _colsum(d))

    return _tok_call("mix_bwd", body, [dxp, dr2, xh, rs], [g, w],
                     [_sds((S, D)), _sds((S, D), BF16), _sds((S, D))], [_sds((1, D)), _sds((1, D))], dep=dep)


def dx_in(dr, pieces, w):
    S, D = dr.shape
    offs = [o for _, o in pieces]
    widths = [a.shape[1] for a, _ in pieces]

    def body(*refs):
        dr_ref, prefs, w_ref, dx_ref = refs[0], refs[1:1 + len(pieces)], refs[-2], refs[-1]
        acc = ALPHA * dr_ref[...]
        for pr, o, n in zip(prefs, offs, widths):
            acc = acc + _dot_nt(pr[...], w_ref[:, o:o + n])
        dx_ref[...] = acc

    return _tok_call("dx_in", body, [dr] + [a for a, _ in pieces], [w], [_sds((S, D))])[0]


def odd_post_bwd(dl, h, xhc, rsc, xhv, rsv, sv, cl_g, cl_b, sl_g, sl_b, sg_w, tm=256, dep=None):
    S = dl.shape[0]
    tm = min(tm, S)

    def body(dl_ref, zc_ref, xhc_ref, rsc_ref, xhv_ref, rsv_ref, sv_ref, clg, clb, slg, slb, w_ref,
             dy_ref, dzc_ref, dclg_ref, dclb_ref, dslg_ref, dslb_ref, dwm_ref, dsb_ref, dvn_ref):
        first = pl.program_id(0) == 0
        last = pl.program_id(0) == pl.num_programs(0) - 1
        dc, dd = dl_ref[:, 0:512], dl_ref[:, 512:1024]
        xhc = xhc_ref[...]
        lnc = xhc * clg[...] + clb[...]
        s = _sigmoid(lnc)
        dlnc = dc * s * (1.0 + lnc * (1.0 - s))
        dy_ref[...] = _ln_bwd(dlnc, xhc, rsc_ref[...], clg[...])
        _acc(dclg_ref, _colsum(dlnc * xhc))
        _acc(dclb_ref, _colsum(dlnc))
        zc = zc_ref[...]
        z = _gelu(zc)
        dsv = dd * z[:, :512]
        dsvb = dsv.astype(BF16)
        xhv = xhv_ref[...]
        vnb = (xhv * slg[...] + slb[...]).astype(BF16)

        @pl.when(first)
        def _():
            dwm_ref[...] = jnp.zeros_like(dwm_ref)
            dsb_ref[...] = jnp.zeros_like(dsb_ref)

        for g in range(4):
            wm = _masked_sg_w(w_ref, g)
            for ch in range(tm // 128):
                rs_, cs = slice(ch * 128, (ch + 1) * 128), slice(g * 128, (g + 1) * 128)
                dwm_ref[g] += _dot_nt(dsvb[rs_, cs], vnb[rs_, cs])
                dvn_ref[rs_, cs] = _dot_tn(wm, dsvb[rs_, cs])
                dsb_ref[g] += dsv[rs_, cs]
        dvn = dvn_ref[...]
        dvv = _ln_bwd(dvn, xhv, rsv_ref[...], slg[...])
        _acc(dslg_ref, _colsum(dvn * xhv))
        _acc(dslb_ref, _colsum(dvn))
        gg = _gelu_grad(zc)
        dzc_ref[:, 0:512] = (dd * sv_ref[...] * gg[:, :512]).astype(BF16)
        dzc_ref[:, 512:1024] = (dvv * gg[:, 512:]).astype(BF16)

        @pl.when(last)
        def _():
            row = lax.broadcasted_iota(jnp.int32, (128, 128), 0)
            col = lax.broadcasted_iota(jnp.int32, (128, 128), 1)
            for g in range(4):
                dwm_ref[g] = jnp.where(row >= col, dwm_ref[g], 0.0)
                dsb_ref[g] = jnp.broadcast_to(jnp.sum(dsb_ref[g], axis=1, keepdims=True), (128, 128))

    return _tok_call(
        "odd_post_bwd", body, [dl, (h, 1024, 1), xhc, rsc, xhv, rsv, sv], [cl_g, cl_b, sl_g, sl_b, sg_w],
        [_sds((S, 512)), _sds((S, 1024), BF16)],
        [_sds((1, 512)), _sds((1, 512)), _sds((1, 512)), _sds((1, 512)), _sds((4, 128, 128)), _sds((4, 128, 128))],
        tm=tm, scratch=[pltpu.VMEM((tm, 512), F32)], dep=dep)


def conv_bwd(dy, hc, h, dw, CH=128):
    S = dy.shape[0]

    def body(dy_ref, hc_ref, a_ref, g_ref, dw_ref, da_ref, dg_ref, ddw_ref, padh_ref, padd_ref, dhc_ref):
        padh_ref[0:32, :] = jnp.zeros((32, 128), F32)
        padh_ref[32:32 + S, :] = hc_ref[...]
        padd_ref[0:S, :] = dy_ref[...]
        padd_ref[S:S + 32, :] = jnp.zeros((32, 128), F32)
        taps = [jnp.zeros((1, 128), F32) for _ in range(CONV_TAPS)]
        for ch in range(S // CH):
            b0 = ch * CH
            dyc = padd_ref[b0:b0 + CH, :]
            acc = dw_ref[0:1, :] * padd_ref[b0 + 30:b0 + 30 + CH, :]
            taps[0] = taps[0] + _colsum(dyc * padh_ref[b0 + 2:b0 + 2 + CH, :])
            for k in range(1, CONV_TAPS):
                acc = acc + dw_ref[k:k + 1, :] * padd_ref[b0 + 30 - k:b0 + 30 - k + CH, :]
                taps[k] = taps[k] + _colsum(dyc * padh_ref[b0 + 2 + k:b0 + 2 + k + CH, :])
            dhc_ref[b0:b0 + CH, :] = acc
        for k in range(CONV_TAPS):
            ddw_ref[k:k + 1, :] = taps[k]
        dhc = dhc_ref[...]
        s = _sigmoid(g_ref[...])
        da_ref[...] = (dhc * s).astype(BF16)
        dg_ref[...] = (dhc * a_ref[...] * s * (1.0 - s)).astype(BF16)

    return pl.pallas_call(
        body, name="conv_bwd", grid=(4,),
        in_specs=[pl.BlockSpec((S, 128), lambda c: (0, c)),
                  pl.BlockSpec((S, 128), lambda c: (0, c)),
                  pl.BlockSpec((S, 128), lambda c: (0, c)),
                  pl.BlockSpec((S, 128), lambda c: (0, 4 + c)),
                  pl.BlockSpec((CONV_TAPS, 128), lambda c: (0, c))],
        out_specs=[pl.BlockSpec((S, 128), lambda c: (0, c)), pl.BlockSpec((S, 128), lambda c: (0, c)),
                   pl.BlockSpec((CONV_TAPS, 128), lambda c: (0, c))],
        out_shape=[_sds((S, 512), BF16), _sds((S, 512), BF16), _sds((CONV_TAPS, 512))],
        scratch_shapes=[pltpu.VMEM((S + 32, 128), F32), pltpu.VMEM((S + 32, 128), F32), pltpu.VMEM((S, 128), F32)],
        compiler_params=_cp(("arbitrary",)),
    )(dy, hc, h, h, dw)


def attn_bwd(qkv, dl, tb, T=256, dep=None):
    S = qkv.shape[0]
    T = min(T, S)
    nq = S // T

    def body(q_ref, k_ref, v_ref, do_ref, t_ref, dq_ref, dk_ref, dv_ref,
             dka_ref, dva_ref, dqa_ref, pc_ref, gc_ref, qh_ref, doh_ref):
        i = pl.program_id(0)
        hm0 = lax.broadcasted_iota(jnp.int32, (1, 128), 1) < 64
        r2 = lax.broadcasted_iota(jnp.int32, (2 * T, T), 0)
        c2 = lax.broadcasted_iota(jnp.int32, (2 * T, T), 1)
        causal = c2 < jnp.where(r2 >= T, r2 - T, r2)
        ur = lax.broadcasted_iota(jnp.int32, (T, T), 0)
        uc = lax.broadcasted_iota(jnp.int32, (T, T), 1)
        u_le = (ur <= uc).astype(BF16)
        u_lt = (ur < uc).astype(BF16)

        @pl.when(i == 0)
        def _():
            dka_ref[...] = jnp.zeros_like(dka_ref)
            dva_ref[...] = jnp.zeros_like(dva_ref)

        dqa_ref[...] = jnp.zeros_like(dqa_ref)
        gc_ref[...] = jnp.zeros_like(gc_ref)
        for pp in range(4):
            cs = slice(pp * 128, (pp + 1) * 128)
            qh_ref[pp] = _stack_heads(q_ref[:, cs] * QK_SCALE, hm0)
            doh_ref[pp] = _stack_heads(do_ref[:, cs], hm0)
            for hd in range(2):
                for half in range(T // 128):
                    pc_ref[pp, hd * T:(hd + 1) * T, half * 128:(half + 1) * 128] = t_ref[2 * pp + hd]

        def block(kb, diag):
            ks = pl.multiple_of(kb * T, T)
            cols = [slice(pp * 128, (pp + 1) * 128) for pp in range(4)]
            zs = [_dot_nt(qh_ref[pp], k_ref[pl.ds(ks, T), cols[pp]]) for pp in range(4)]
            dws = [_dot_nt(doh_ref[pp], v_ref[pl.ds(ks, T), cols[pp]]) for pp in range(4)]
            a_s, pres = [], []
            for pp in range(4):
                sp = _softplus(zs[pp])
                a_s.append(zs[pp] - sp)
                if diag:
                    sp = jnp.where(causal, sp, 0.0)
                pres.append(_cumsum_mm(sp, u_le))
            ws, gmats, gsums = [], [], []
            for pp in range(4):
                rem = pc_ref[pp]
                w = jnp.exp(a_s[pp] - rem + pres[pp])
                if diag:
                    w = jnp.where(causal, w, 0.0)
                gmat = dws[pp] * w
                ws.append(w.astype(BF16))
                gmats.append(gmat)
                gsums.append(_cumsum_mm(gmat, u_lt))
                pc_ref[pp] = rem - jnp.broadcast_to(pres[pp][:, T - 1:T], (2 * T, T))
            for pp in range(4):
                cs = cols[pp]
                sig = jnp.exp(a_s[pp])
                gex = gc_ref[pp] + gsums[pp]
                dz = gmats[pp] * (1.0 - sig) - sig * gex
                if diag:
                    dz = jnp.where(causal, dz, 0.0)
                dzb = dz.astype(BF16)
                dqa_ref[:, cs] += _dot(_unstack_k(dzb, T), _stack_heads(k_ref[pl.ds(ks, T), cs], hm0))
                dka_ref[pl.ds(ks, T), cs] += _dot_tn(dzb, qh_ref[pp])
                dva_ref[pl.ds(ks, T), cs] += _dot_tn(ws[pp], doh_ref[pp])
                gc_ref[pp] = jnp.broadcast_to(gex[:, T - 1:T] + gmats[pp][:, T - 1:T], (2 * T, T))

        def step(kb, carry):
            block(kb, False)
            return carry

        lax.fori_loop(0, i, step, 0)
        block(i, True)
        dq_ref[...] = (dqa_ref[...] * QK_SCALE).astype(BF16)

        @pl.when(i == nq - 1)
        def _():
            dk_ref[...] = dka_ref[...].astype(BF16)
            dv_ref[...] = dva_ref[...].astype(BF16)

    deps = [] if dep is None else [dep]
    call_body = body if dep is None else (lambda *refs: body(*refs[:5], *refs[6:]))
    return pl.pallas_call(
        call_body, name="attn_bwd", grid=(nq,),
        in_specs=[pl.BlockSpec((T, 512), lambda i: (i, 0)),
                  pl.BlockSpec((S, 512), lambda i: (0, 1)),
                  pl.BlockSpec((S, 512), lambda i: (0, 2)),
                  pl.BlockSpec((T, 512), lambda i: (i, 0)),
                  pl.BlockSpec((8, T, 128), lambda i: (0, i, 0))] + [ANY] * len(deps),
        out_specs=[pl.BlockSpec((T, 512), lambda i: (i, 0)),
                   pl.BlockSpec((S, 512), lambda i: (0, 0)),
                   pl.BlockSpec((S, 512), lambda i: (0, 0))],
        out_shape=[_sds((S, 512), BF16), _sds((S, 512), BF16), _sds((S, 512), BF16)],
        scratch_shapes=[pltpu.VMEM((S, 512), F32), pltpu.VMEM((S, 512), F32), pltpu.VMEM((T, 512), F32),
                        pltpu.VMEM((4, 2 * T, T), F32), pltpu.VMEM((4, 2 * T, T), F32),
                        pltpu.VMEM((4, 2 * T, 128), BF16), pltpu.VMEM((4, 2 * T, 128), BF16)],
        compiler_params=_cp(("arbitrary",)),
    )(qkv, qkv, qkv, dl, tb, *deps)


def pool_bwd(dl, pooled_b, pool_w, pool_scale, CH=256):
    S = dl.shape[0]
    CH = min(CH, S)

    def body(db_ref, pooled_ref, w_ref, sc_ref, du_ref, dw_ref, dsc_ref, pad_ref, dp_ref):
        pad_ref[S:S + 16, :] = jnp.zeros((16, 128), F32)
        for g, win in enumerate(POOL_WINDOWS):
            cs = slice(g * 128, (g + 1) * 128)
            wq = w_ref[g].astype(BF16)
            dwg = jnp.zeros((128, 128), F32)
            dsc = jnp.zeros((1, 128), F32)
            for ch in range(S // CH):
                rs_ = slice(ch * CH, (ch + 1) * CH)
                db = db_ref[rs_, cs]
                pb = pooled_ref[rs_, cs]
                dsc = dsc + _colsum(db * _dot(pb, wq))
                dmsb = (db * sc_ref[:, cs]).astype(BF16)
                dwg = dwg + _dot_tn(pb, dmsb)
                dpool = _dot_nt(dmsb, wq)
                t = ch * CH + lax.broadcasted_iota(jnp.int32, (CH, 1), 0)
                cnt = jnp.minimum(t + 1, win).astype(F32)
                dp_ref[rs_, :] = dpool
                pad_ref[rs_, :] = dpool / cnt
            dw_ref[g] = dwg
            dsc_ref[:, cs] = dsc
            for ch in range(S // CH):
                base = ch * CH
                acc = pad_ref[base:base + CH, :]
                for sft in range(1, win):
                    acc = acc + pad_ref[base + sft:base + sft + CH, :]
                du_ref[base:base + CH, cs] = (acc - dp_ref[base:base + CH, :]).astype(BF16)

    return pl.pallas_call(
        body, name="pool_bwd", grid=(1,),
        in_specs=[pl.BlockSpec((S, 512), lambda i: (0, 1)),
                  pl.BlockSpec((S, 512), lambda i: (0, 0)),
                  pl.BlockSpec((4, 128, 128), lambda i: (0, 0, 0)),
                  pl.BlockSpec((1, 512), lambda i: (0, 0))],
        out_specs=[pl.BlockSpec((S, 512), lambda i: (0, 0)),
                   pl.BlockSpec((4, 128, 128), lambda i: (0, 0, 0)),
                   pl.BlockSpec((1, 512), lambda i: (0, 0))],
        out_shape=[_sds((S, 512), BF16), _sds((4, 128, 128)), _sds((1, 512))],
        scratch_shapes=[pltpu.VMEM((S + 16, 128), F32), pltpu.VMEM((S, 128), F32)],
        compiler_params=_cp(("arbitrary",)),
    )(dl, pooled_b, pool_w, pool_scale)


def tn_into(a, b, out, r0, c0, tk=1024, tn=512):
    S, K = a.shape
    N = b.shape[1]
    tk, tn = min(tk, K), min(tn, N)
    assert K % tk == 0 and N % tn == 0 and r0 % tk == 0 and c0 % tn == 0
    rb, cb = r0 // tk, c0 // tn
    fresh = isinstance(out, jax.ShapeDtypeStruct)

    def body(*refs):
        a_ref, b_ref, o_ref = refs[0], refs[1], refs[-1]
        o_ref[...] = _dot_tn(a_ref[...], b_ref[...]).astype(BF16)

    ospec = pl.BlockSpec((tk, tn), lambda i, j: (rb + i, cb + j))
    in_specs = [pl.BlockSpec((S, tk), lambda i, j: (0, i)), pl.BlockSpec((S, tn), lambda i, j: (0, j))]
    args = [a, b]
    aliases = {}
    if not fresh:
        in_specs += [pl.BlockSpec(memory_space=pl.ANY)]
        args += [out]
        aliases = {2: 0}
    return pl.pallas_call(
        body, name="tn_grad", grid=(K // tk, N // tn),
        in_specs=in_specs, out_specs=[ospec],
        out_shape=[_sds(out.shape, BF16)],
        input_output_aliases=aliases,
        compiler_params=_cp(("arbitrary", "arbitrary")),
    )(*args)[0]


def _row(a, i):
    return a[i:i + 1]


MIXER_NAMES = (("even_w_in", "even_w_out"), ("odd_w_in", "odd_w_out"))


def _tn_group(items):
    out = {}
    for name, (shape, parts) in items.items():
        g = _sds(shape, BF16)
        for a, b, r0, c0 in parts:
            g = tn_into(a, b, g, r0, c0)
        out[name] = g
    return out


def fwd_layer(i, xin, p_i, target, comm):
    s = {}
    W = comm.weights(("mix", i), xin)
    w_in = W[MIXER_NAMES[i][0]]
    if i == 0:
        s["h"], s["xb"], s["qkv"] = mm_in(xin, w_in, nb16=1536)
        comm.poke(("in", i), s["h"])
        s["l1"], s["tb"] = attn_fwd(s["qkv"])
        s["l2"], s["pooled"] = pool_fwd(s["h"], W["pool_w"], W["pool_scale"])
    else:
        s["h"], s["xb"] = mm_in(xin, w_in)
        comm.poke(("in", i), s["h"])
        s["y"], s["hc"] = conv_fwd(s["h"], W["conv_dw"])
        sgb_bc = jnp.broadcast_to(W["sg_b"][:, :, None], (4, 128, 128))
        (s["l1"], s["l2"], s["xhc"], s["rsc"], s["xhv"], s["rsv"], s["sv"]) = odd_post(
            s["y"], s["h"], W["conv_ln_g"], W["conv_ln_b"], W["sg_ln_g"], W["sg_ln_b"], W["sg_w"], sgb_bc)
    tok = comm.poke(("mixed", i), s["l1"])
    W = comm.weights(("out", i), s["l1"])
    x1, s["xh1"], s["rs1"] = mm_out_ln(s["l1"], s["l2"], xin, W[MIXER_NAMES[i][1]], _row(W["ln_mix_g"], i),
                                       _row(W["ln_mix_b"], i), dep=tok)
    W = comm.weights(("ffn", i), x1)
    tok = comm.poke(("up", i), x1)
    s["gate"], s["up"], s["hb"], s["x1b"] = ffn_up(x1, W["ffn_w_gate%d" % i], W["ffn_w_up%d" % i], None, dep=tok)
    W = comm.weights(("down", i), s["hb"])
    x2, s["xh2"], s["rs2"] = ffn_down_ln(s["hb"], x1, W["ffn_w_down%d" % i], None,
                                         _row(W["ln_ffn_g"], i), _row(W["ln_ffn_b"], i))
    tok = comm.poke(("ffn", i), x2)
    outs = ple_fwd(x2, p_i, W["ple_w_gate%d" % i], W["ple_w_proj%d" % i], None, _row(W["ple_b_gate"], i), target,
                   dep=tok)
    s["sg"], s["pp"], s["x2b"], s["pb"] = outs[1:5]
    return outs[0], s, outs[5:]


def bwd_layer(i, dx, s, W, comm, tok=None):
    small = {}
    D = dx.shape[1]
    dx2, dgp_b, dpp_b, small["ple_b_gate"] = ple_bwd(dx, s["sg"], s["pp"], W["ple_w_gate%d" % i], None, dep=tok)
    dr2, dr2_b, small["ln_ffn_g"], small["ln_ffn_b"] = ln_bwd(dx2, s["xh2"], s["rs2"], _row(W["ln_ffn_g"], i))
    dxp, dwg, dwu, dwd = ffn_bwd(dr2_b, s["x1b"], s["gate"], s["up"], s["hb"], W["ffn_w_gate%d" % i],
                                 W["ffn_w_up%d" % i], W["ffn_w_down%d" % i])
    grads = _tn_group({
        "ple_w_gate%d" % i: ((D, D), [(s["x2b"], dgp_b, 0, 0)]),
        "ple_w_proj%d" % i: ((s["pb"].shape[1], D), [(s["pb"], dpp_b, 0, 0)])})
    grads.update({"ffn_w_down%d" % i: dwd, "ffn_w_gate%d" % i: dwg, "ffn_w_up%d" % i: dwu})
    tok = comm.grads(grads)
    iname, oname = MIXER_NAMES[i]
    dr1, dmix_b, dl, small["ln_mix_g"], small["ln_mix_b"] = mix_bwd(
        dxp, dr2, s["xh1"], s["rs1"], _row(W["ln_mix_g"], i), W[oname], dep=tok)
    tok = comm.poke(("bwd", i), dl)
    if i == 1:
        (dy, dzc_b, small["conv_ln_g"], small["conv_ln_b"], small["sg_ln_g"], small["sg_ln_b"],
         small["sg_w"], dsb) = odd_post_bwd(dl, s["h"], s["xhc"], s["rsc"], s["xhv"], s["rsv"], s["sv"],
                                            W["conv_ln_g"], W["conv_ln_b"], W["sg_ln_g"], W["sg_ln_b"], W["sg_w"],
                                            dep=tok)
        small["sg_b"] = dsb[:, :, 0]
        da_b, dg_b, small["conv_dw"] = conv_bwd(dy, s["hc"], s["h"], W["conv_dw"])
        pieces = [(da_b, 0), (dg_b, 512), (dzc_b, 1024)]
    else:
        dq_b, dk_b, dv_b = attn_bwd(s["qkv"], dl, s["tb"], dep=tok)
        du_b, small["pool_w"], small["pool_scale"] = pool_bwd(dl, s["pooled"], W["pool_w"], W["pool_scale"])
        pieces = [(dq_b, 0), (dk_b, 512), (dv_b, 1024), (du_b, 1536)]
    dxin = dx_in(dr1, pieces, W[iname])
    tok = comm.grads(_tn_group({
        oname: ((1024, D), [(s["l1"], dmix_b, 0, 0), (s["l2"], dmix_b, 512, 0)]),
        iname: ((D, 2048), [(s["xb"], a, 0, off) for a, off in pieces])}))
    return dxin, small, tok


def run_layers(x, p, target, comm):
    saved, xin = [], x
    for i in range(2):
        xin, s, extra = fwd_layer(i, xin, p[i], target if i == 1 else None, comm)
        saved.append(s)
    dx, sq = extra
    W = comm.all_weights()
    per_layer = [None, None]
    tok = None
    for i in (1, 0):
        dx, per_layer[i], tok = bwd_layer(i, dx, saved[i], W, comm, tok)
    small = {}
    for k in ("ln_mix_g", "ln_mix_b", "ln_ffn_g", "ln_ffn_b", "ple_b_gate"):
        small[k] = jnp.concatenate([per_layer[0][k], per_layer[1][k]], axis=0)
    for i in range(2):
        small.update({k: v for k, v in per_layer[i].items() if k not in small})
    return sq, dx, small


def _big_table():
    t = {}
    for nm in ("even", "odd"):
        t[nm + "_w_in"] = ((1024, 2048), 1, 256, 256, nm + "_w_in", 0)
        t[nm + "_w_out"] = ((1024, 1024), 0, 128, 128, nm + "_w_out", 0)
    for l in range(2):
        t["ffn_w_gate%d" % l] = ((1024, 8 * FF_PAD), 1, FF_PAD, FF_SHARD, "ffn_w_gate", l)
        t["ffn_w_up%d" % l] = ((1024, 8 * FF_PAD), 1, FF_PAD, FF_SHARD, "ffn_w_up", l)
        t["ffn_w_down%d" % l] = ((8 * FF_PAD, 1024), 0, FF_PAD, FF_SHARD, "ffn_w_down", l)
        t["ple_w_gate%d" % l] = ((1024, 1024), 0, 128, 128, "ple_w_gate", l)
        t["ple_w_proj%d" % l] = ((256, 1024), 1, 128, 128, "ple_w_proj", l)
    return t


BIG = _big_table()
TRANSPOSED_ARGS = ("ffn_w_gate", "ffn_w_up")
SMALL_SPEC = ((N_DEV, 40, 64), 0, 1, 1)
_UP_GROUP = lambda l: ["ffn_w_gate%d" % l, "ffn_w_up%d" % l]
_DOWN_GROUP = lambda l: ["ffn_w_down%d" % l, "ple_w_gate%d" % l, "ple_w_proj%d" % l]
AG_GROUPS = (["even_w_in"], ["even_w_out"], _UP_GROUP(0), _DOWN_GROUP(0), ["odd_w_in", "odd_w_out", "small"],
             _UP_GROUP(1), _DOWN_GROUP(1))
AG_NEED = {("mix", 0): 0, ("out", 0): 1, ("ffn", 0): 2, ("down", 0): 3, ("mix", 1): 4, ("ffn", 1): 5, ("down", 1): 6}
AG_PASS = {("in", 0): 1, ("mixed", 0): 2, ("up", 0): 3, ("ffn", 0): 4, ("mixed", 1): 5, ("up", 1): 6}
ANY = pl.BlockSpec(memory_space=pl.ANY)
SEM = pl.BlockSpec(memory_space=pltpu.SEMAPHORE)


def _spec(name):
    return SMALL_SPEC if name == "small" else BIG[name]


def _win_shape(spec):
    full, axis, w = spec[:3]
    return tuple(w if d == axis else n for d, n in enumerate(full))


def _window(ref, axis, w, j):
    idx = [slice(None)] * len(ref.shape)
    idx[axis] = pl.ds(j, 1) if w == 1 else pl.ds(pl.multiple_of(j * w, w), w)
    return ref.at[tuple(idx)]


def _mesh_pos():
    return lax.axis_index("x"), lax.axis_index("y"), lax.axis_index("c")


def split_call(name, arrays, starts=(), waits=(), sems_in=(), new=(), after=None):
    n, nn, ns = len(arrays), len(new), len(starts)
    flat_sems = [s for pair in sems_in for s in pair]

    def body(*refs):
        arr = list(refs[:n])
        sin = refs[n:n + len(flat_sems)]
        outs = refs[n + len(flat_sems) + (after is not None):]
        data = arr + list(outs[n:n + nn])
        for p, k, kind, mk in waits:
            d = mk(data, sin[2 * p].at[k], sin[2 * p + 1].at[k])
            d.wait_send() if kind == "send" else d.wait_recv()
        if ns:
            send, recv = outs[n + nn], outs[n + nn + 1]
            for k, mk in enumerate(starts):
                mk(data, send.at[k], recv.at[k]).start()
        outs[-1][...] = jnp.zeros((8, 128), F32)

    sem_out = [pltpu.SemaphoreType.DMA((ns,)), pltpu.SemaphoreType.DMA((ns,))] if ns else []
    res = pl.pallas_call(
        body, name=name,
        in_specs=[ANY] * n + [SEM] * len(flat_sems) + ([ANY] if after is not None else []),
        out_specs=[ANY] * (n + nn) + [SEM] * len(sem_out) + [pl.BlockSpec(memory_space=pltpu.VMEM)],
        out_shape=[_sds(a.shape, a.dtype) for a in arrays] + list(new) + sem_out + [_sds((8, 128), F32)],
        input_output_aliases={a: a for a in range(n)},
        compiler_params=pltpu.CompilerParams(has_side_effects=pltpu.SideEffectType.DATAFLOW_SIDE_EFFECTING),
    )(*arrays, *flat_sems, *([after] if after is not None else []))
    return list(res[:n + nn]), (tuple(res[n + nn:n + nn + 2]) if ns else None), res[-1]


def _remote(src, dst, send_sem, recv_sem, dev):
    return pltpu.make_async_remote_copy(src_ref=src, dst_ref=dst, send_sem=send_sem, recv_sem=recv_sem,
                                        device_id=dev, device_id_type=MESH_T)


class Gatherer:
    def __init__(self, groups, arrays, specs, prefix):
        self.groups, self.specs, self.prefix = groups, specs, prefix
        self.names = [nm for g in groups for nm in g]
        self.arr = dict(zip(self.names, arrays))
        self.fwd_sems = {}
        self.forwarded = set()

    @staticmethod
    def _mk_first(ai, spec, k):
        def mk(refs, ss, rs):
            x, y, c = _mesh_pos()
            dev = [(x, y, 1 - c), (1 - x, y, c), (x, 1 - y, c), (1 - x, 1 - y, c)][k]
            win = _window(refs[ai], spec[1], spec[2], 4 * x + 2 * y + c)
            return _remote(win, win, ss, rs, dev)
        return mk

    @staticmethod
    def _mk_fwd(ai, spec, j):
        def mk(refs, ss, rs):
            x, y, c = _mesh_pos()
            px, py = [(1 - x, y), (x, 1 - y), (1 - x, 1 - y)][j]
            win = _window(refs[ai], spec[1], spec[2], 4 * px + 2 * py + c)
            return _remote(win, win, ss, rs, (x, y, 1 - c))
        return mk

    def start(self, after=None):
        starts = [self._mk_first(ai, self.specs[nm], k) for ai, nm in enumerate(self.names) for k in range(4)]
        arrs, self.first_sems, tok = split_call(self.prefix + "_start", [self.arr[nm] for nm in self.names],
                                                starts=starts, after=after)
        self.arr = dict(zip(self.names, arrs))
        return tok

    def forward(self, g, after=None):
        if g in self.forwarded:
            return None
        self.forwarded.add(g)
        names = self.groups[g]
        waits = [(0, 4 * self.names.index(nm) + 1 + j, "recv", self._mk_fwd(ai, self.specs[nm], j))
                 for ai, nm in enumerate(names) for j in range(3)]
        starts = [self._mk_fwd(ai, self.specs[nm], j) for ai, nm in enumerate(names) for j in range(3)]
        arrs, self.fwd_sems[g], tok = split_call(
            "%s_forward%d" % (self.prefix, g), [self.arr[nm] for nm in names], starts=starts, waits=waits,
            sems_in=[self.first_sems], after=after)
        self.arr.update(zip(names, arrs))
        return tok

    def finish(self, g, after=None):
        self.forward(g, after)
        names = self.groups[g]
        waits = []
        for ai, nm in enumerate(names):
            base = 4 * self.names.index(nm)
            waits.append((0, base, "recv", self._mk_first(ai, self.specs[nm], 0)))
            waits += [(1, 3 * ai + j, "recv", self._mk_fwd(ai, self.specs[nm], j)) for j in range(3)]
            waits += [(0, base + k, "send", self._mk_first(ai, self.specs[nm], k)) for k in range(4)]
            waits += [(1, 3 * ai + j, "send", self._mk_fwd(ai, self.specs[nm], j)) for j in range(3)]
        arrs, _, _ = split_call(
            "%s_finish%d" % (self.prefix, g), [self.arr[nm] for nm in names], waits=waits,
            sems_in=[self.first_sems, self.fwd_sems[g]], after=after)
        self.arr.update(zip(names, arrs))
        return {nm: self.arr[nm] for nm in names}


class Reducer:
    def __init__(self, cq_arr, adam):
        self.cq_arr, self.adam = cq_arr, adam
        self.groups = []
        self.n = 0
        self.last = None

    @staticmethod
    def _mk1(gi, li, spec, q):
        def mk(refs, ss, rs):
            x, y, c = _mesh_pos()
            return _remote(_window(refs[gi], spec[1], spec[2], 2 * q + (1 - c)), refs[li].at[q], ss, rs, (x, y, 1 - c))
        return mk

    @staticmethod
    def _mk2(si, li, d):
        def mk(refs, ss, rs):
            x, y, c = _mesh_pos()
            qd = lax.rem(2 * x + y + d, 4)
            return _remote(refs[si].at[d - 1], refs[li].at[3 - d], ss, rs, (lax.div(qd, 2), lax.rem(qd, 2), c))
        return mk

    def add(self, grads, after=None):
        names = list(grads)
        m = len(names)
        starts = [self._mk1(ai, m + ai, BIG[nm], q) for ai, nm in enumerate(names) for q in range(4)]
        new = [_sds((4,) + _win_shape(BIG[nm]), BF16) for nm in names]
        res, sems, tok = split_call("rs1_start%d" % self.n, [grads[nm] for nm in names], starts=starts, new=new,
                                    after=after)
        self.groups.append(dict(names=names, starts=starts, buf=res, sems=sems, stage=1, idx=self.n))
        self.n += 1
        return tok

    def step(self, after):
        tok = None
        for grp in self.groups:
            names, m = grp["names"], len(grp["names"])
            if grp["stage"] == 1:
                waits = [(0, k, kind, mk) for k, mk in enumerate(grp["starts"]) for kind in ("send", "recv")]
                res, _, _ = split_call("rs1_wait%d" % grp["idx"], grp["buf"], waits=waits, sems_in=[grp["sems"]], after=after)
                full, land1 = res[:m], res[m:]
                s1b = [add_pairs(g, l, BIG[nm], self.cq_arr) for nm, g, l in zip(names, full, land1)]
                starts = [self._mk2(ai, m + ai, d) for ai in range(m) for d in (1, 2, 3)]
                new = [_sds(a.shape, BF16) for a in s1b]
                res, sems, tok = split_call("rs2_start%d" % grp["idx"], s1b, starts=starts, new=new, after=tok)
                grp.update(stage=2, g=full, land1=land1, starts=starts, buf=res, sems=sems)
        return tok

    def finish_oldest(self):
        for grp in self.groups:
            if grp["stage"] == 2:
                names, m = grp["names"], len(grp["names"])
                waits = [(0, k, kind, mk) for k, mk in enumerate(grp["starts"]) for kind in ("send", "recv")]
                res, _, _ = split_call("rs2_wait%d" % grp["idx"], grp["buf"], waits=waits, sems_in=[grp["sems"]],
                                       after=self.last)
                for nm, g, l1, l2 in zip(names, grp["g"], grp["land1"], res[m:]):
                    self.last = self.adam(nm, g, l1, l2, self.last)
                grp["stage"] = 3
                return True
        return False


def pack_weights(args, arg_names, small_blk, names, j_arr):
    n_in = len(args)

    def body(j_ref, *refs):
        for o, nm in enumerate(names):
            dst = refs[n_in + 1 + o]
            if nm == "small":
                dst[...] = refs[n_in][...]
                continue
            _, axis, w, valid, arg, layer = BIG[nm]
            if arg in TRANSPOSED_ARGS:
                s = refs[arg_names.index(arg)][layer]
                s = jnp.concatenate([s, jnp.zeros((w - valid, s.shape[1]), F32)], axis=0)
                dst[...] = s.T.astype(BF16)
                continue
            src = refs[arg_names.index(arg)][layer].astype(BF16)
            if valid == w:
                dst[...] = src
            else:
                dst[...] = jnp.zeros(dst.shape, BF16)
                if axis == 1:
                    dst[:, 0:valid] = src
                else:
                    dst[0:valid, :] = src

    def ispec(a):
        return pl.BlockSpec(a.shape, lambda i, j_ref: (0, 0, 0))

    def ospec(spec):
        axis, nd = spec[1], len(spec[0])
        return pl.BlockSpec(_win_shape(spec),
                            lambda i, j_ref, axis=axis, nd=nd: tuple(j_ref[0] if d == axis else 0 for d in range(nd)))

    specs = [_spec(nm) for nm in names]
    return pl.pallas_call(
        body, name="pack_weights",
        grid_spec=pltpu.PrefetchScalarGridSpec(
            num_scalar_prefetch=1, grid=(1,),
            in_specs=[ispec(a) for a in list(args) + [small_blk]], out_specs=[ospec(s) for s in specs]),
        out_shape=[_sds(s[0], F32 if nm == "small" else BF16) for nm, s in zip(names, specs)],
        compiler_params=_cp(("arbitrary",)),
    )(j_arr, *args, small_blk)


def add_pairs(full, land, spec, cq_arr):
    axis, w = spec[1], spec[2]
    R, C = full.shape

    def chip(d, cq):
        return lax.rem(cq[1] + d + 1, 4)

    if axis == 1:
        tr = R
        grid = (3, R // tr)
        fspec = pl.BlockSpec((tr, w), lambda d, i, cq: (i, 2 * chip(d, cq) + cq[0]))
        lspec = pl.BlockSpec((None, tr, w), lambda d, i, cq: (chip(d, cq), i, 0))
        ospec = pl.BlockSpec((None, tr, w), lambda d, i, cq: (d, i, 0))
    else:
        grid = (3, 1)
        fspec = pl.BlockSpec((w, C), lambda d, i, cq: (2 * chip(d, cq) + cq[0], 0))
        lspec = pl.BlockSpec((None, w, C), lambda d, i, cq: (chip(d, cq), 0, 0))
        ospec = pl.BlockSpec((None, w, C), lambda d, i, cq: (d, 0, 0))

    def body(cq_ref, a_ref, b_ref, ob_ref):
        ob_ref[...] = (a_ref[...].astype(F32) + b_ref[...].astype(F32)).astype(BF16)

    return pl.pallas_call(
        body, name="add_pairs",
        grid_spec=pltpu.PrefetchScalarGridSpec(
            num_scalar_prefetch=1, grid=grid, in_specs=[fspec, lspec], out_specs=[ospec]),
        out_shape=[_sds((3,) + land.shape[1:], BF16)],
        compiler_params=_cp(("arbitrary",) * 2),
    )(cq_arr, full, land)[0]


def _adamw(w, g, m, v):
    m = ADAM_B1 * m + (1.0 - ADAM_B1) * g
    v = ADAM_B2 * v + (1.0 - ADAM_B2) * (g * g)
    m_hat = m / (1.0 - ADAM_B1 ** ADAM_STEP)
    v_hat = v / (1.0 - ADAM_B2 ** ADAM_STEP)
    delta = -ADAM_LR * (m_hat / (jnp.sqrt(v_hat) + ADAM_EPS) + ADAM_WD * w)
    return delta, m, v


def reduce_adamw(full, land1, land, w, m, v, spec, cq_arr, prev=None, dep=None):
    axis, win, valid, layer = spec[1], spec[2], spec[3], spec[5]
    L, R, C = w.shape
    transposed = spec[4] in TRANSPOSED_ARGS
    if transposed:
        grid = (1,)
        fspec = pl.BlockSpec((C, win), lambda i, cq: (0, 2 * cq[1] + cq[0]))
        wspec = pl.BlockSpec((None, C, win), lambda i, cq: (cq[1], 0, 0))
        lspec = pl.BlockSpec((3, C, win), lambda i, cq: (0, 0, 0))
        sspec = pl.BlockSpec((None, R, C), lambda i, cq: (layer, 0, 0))
    elif axis == 1:
        tr = min(1024, R)
        grid = (R // tr,)
        fspec = pl.BlockSpec((tr, win), lambda i, cq: (i, 2 * cq[1] + cq[0]))
        wspec = pl.BlockSpec((None, tr, win), lambda i, cq: (cq[1], i, 0))
        lspec = pl.BlockSpec((3, tr, win), lambda i, cq: (0, i, 0))
        sspec = pl.BlockSpec((None, tr, C), lambda i, cq: (layer, i, 0))
    else:
        grid = (1,)
        fspec = pl.BlockSpec((win, full.shape[1]), lambda i, cq: (2 * cq[1] + cq[0], 0))
        wspec = pl.BlockSpec((None, win, C), lambda i, cq: (cq[1], 0, 0))
        lspec = pl.BlockSpec((3, win, C), lambda i, cq: (0, 0, 0))
        sspec = pl.BlockSpec((None, R, C), lambda i, cq: (layer, 0, 0))

    def body(cq_ref, full_ref, own_ref, land_ref, w_ref, m_ref, v_ref, *rest):
        g_ref, d_ref, nm_ref, nv_ref = rest[-4:]
        if transposed:
            rd = lambda r, *lead: r[lead] if lead else r[...]
        elif axis == 1:
            rd = lambda r, *lead: r[(*lead, slice(None), slice(0, valid))]
        else:
            rd = lambda r, *lead: r[(*lead, slice(0, valid), slice(None))]
        g = rd(full_ref).astype(F32) + rd(own_ref).astype(F32)
        for k in range(3):
            g = g + rd(land_ref, k).astype(F32)
        if transposed:
            g = g.T[0:valid, :]
        g_ref[...] = g
        d, nm, nv = _adamw(w_ref[...], g, m_ref[...], v_ref[...])
        d_ref[...] = d
        nm_ref[...] = nm
        nv_ref[...] = nv

    extra = (list(prev) if prev is not None else []) + ([dep] if dep is not None else [])
    return pl.pallas_call(
        body, name="reduce_adamw",
        grid_spec=pltpu.PrefetchScalarGridSpec(
            num_scalar_prefetch=1, grid=grid,
            in_specs=[fspec, wspec, lspec, sspec, sspec, sspec] + [ANY] * len(extra), out_specs=[sspec] * 4),
        out_shape=[_sds(w.shape)] * 4,
        input_output_aliases={7 + k: k for k in range(4 if prev is not None else 0)},
        compiler_params=_cp(("arbitrary",)),
    )(cq_arr, full, land1, land, w, m, v, *extra)


def place_slot(packed, j_arr):
    R = packed.shape[0]

    def body(j_ref, src, dst):
        dst[...] = src[...]

    return pl.pallas_call(
        body, name="place_slot",
        grid_spec=pltpu.PrefetchScalarGridSpec(
            num_scalar_prefetch=1, grid=(1,),
            in_specs=[pl.BlockSpec((R, 128), lambda i, j: (0, 0))],
            out_specs=[pl.BlockSpec((None, R, 128), lambda i, j: (j[0], 0, 0))]),
        out_shape=[_sds((N_DEV, R, 128))], compiler_params=_cp(("arbitrary",)),
    )(j_arr, packed)[0]


def sum_slots(gathered):
    def body(g_ref, o_ref):
        g = g_ref[0]
        for dev in range(1, N_DEV):
            g = g + g_ref[dev]
        o_ref[...] = g

    return pl.pallas_call(body, name="sum_slots", out_shape=_sds(gathered.shape[1:]), compiler_params=_cp())(gathered)


def small_adamw(gs, wmv):
    k = len(gs)

    def body(*refs):
        for a in range(k):
            g, w, m, v = refs[4 * a:4 * a + 4]
            d, nm, nv = _adamw(w[...], g[...], m[...], v[...])
            refs[4 * k + 3 * a][...] = d
            refs[4 * k + 3 * a + 1][...] = nm
            refs[4 * k + 3 * a + 2][...] = nv

    args = [t for g, tup in zip(gs, wmv) for t in (g,) + tuple(tup)]
    out_shape = [_sds(g.shape) for g in gs for _ in range(3)]
    return pl.pallas_call(body, name="small_adamw", out_shape=out_shape, compiler_params=_cp())(*args)


WEIGHT_NAMES = ("even_w_in", "even_w_out", "pool_w", "pool_scale", "odd_w_in", "odd_w_out", "conv_dw", "conv_ln_g",
                "conv_ln_b", "sg_ln_g", "sg_ln_b", "sg_w", "sg_b", "ln_mix_g", "ln_mix_b", "ffn_w_gate", "ffn_w_up",
                "ffn_w_down", "ln_ffn_g", "ln_ffn_b", "ple_w_proj", "ple_w_gate", "ple_b_gate")
PACK_ARGS = ("even_w_in", "even_w_out", "odd_w_in", "odd_w_out", "ffn_w_gate", "ffn_w_up", "ffn_w_down",
             "ple_w_gate", "ple_w_proj")
REPLICATED = ("pool_w", "pool_scale", "sg_w", "sg_b", "ln_mix_g", "ln_mix_b", "ln_ffn_g", "ln_ffn_b", "ple_b_gate")
SHARDED_SMALL = ("conv_dw", "conv_ln_g", "conv_ln_b", "sg_ln_g", "sg_ln_b")
NATURAL = {"pool_w": (4, 128, 128), "pool_scale": (1, 512), "sg_w": (4, 128, 128), "sg_b": (4, 128),
           "ln_mix_g": (2, 1024), "ln_mix_b": (2, 1024), "ln_ffn_g": (2, 1024), "ln_ffn_b": (2, 1024),
           "ple_b_gate": (2, 1024)}


def kernel(x, p, even_w_in, even_w_out, pool_w, pool_scale, odd_w_in, odd_w_out, conv_dw, conv_ln_g, conv_ln_b, sg_ln_g, sg_ln_b, sg_w, sg_b, ln_mix_g, ln_mix_b, ffn_w_gate, ffn_w_up, ffn_w_down, ln_ffn_g, ln_ffn_b, ple_w_proj, ple_w_gate, ple_b_gate, loss_target, m_even_w_in, m_even_w_out, m_pool_w, m_pool_scale, m_odd_w_in, m_odd_w_out, m_conv_dw, m_conv_ln_g, m_conv_ln_b, m_sg_ln_g, m_sg_ln_b, m_sg_w, m_sg_b, m_ln_mix_g, m_ln_mix_b, m_ffn_w_gate, m_ffn_w_up, m_ffn_w_down, m_ln_ffn_g, m_ln_ffn_b, m_ple_w_proj, m_ple_w_gate, m_ple_b_gate, v_even_w_in, v_even_w_out, v_pool_w, v_pool_scale, v_odd_w_in, v_odd_w_out, v_conv_dw, v_conv_ln_g, v_conv_ln_b, v_sg_ln_g, v_sg_ln_b, v_sg_w, v_sg_b, v_ln_mix_g, v_ln_mix_b, v_ffn_w_gate, v_ffn_w_up, v_ffn_w_down, v_ln_ffn_g, v_ln_ffn_b, v_ple_w_proj, v_ple_w_gate, v_ple_b_gate):
    A = dict(locals())
    for arg in TRANSPOSED_ARGS:
        for pre in ("", "m_", "v_"):
            A[pre + arg] = jnp.swapaxes(A[pre + arg], 1, 2)
    mx, my, mc = _mesh_pos()
    j = 4 * mx + 2 * my + mc
    j_arr = j.astype(jnp.int32).reshape(1)
    cq_arr = jnp.stack([mc, 2 * mx + my]).astype(jnp.int32)
    res = {}

    def adam(nm, full, land1, land2, dep):
        arg = BIG[nm][4]
        res[arg] = reduce_adamw(full, land1, land2, A[arg], A["m_" + arg], A["v_" + arg], BIG[nm], cq_arr,
                                res.get(arg), dep)
        return res[arg][0]

    class Comm:
        def __init__(self):
            names = [nm for g in AG_GROUPS for nm in g]
            small_blk = jnp.concatenate([conv_dw[0], conv_ln_g, conv_ln_b, sg_ln_g, sg_ln_b, jnp.zeros((5, 64), F32)], axis=0)
            mine = pack_weights([A[k] for k in PACK_ARGS], PACK_ARGS, small_blk[None], names, j_arr)
            self.gat = Gatherer(AG_GROUPS, mine, {nm: _spec(nm) for nm in names}, "ag")
            self.gat.start()
            self.red = Reducer(cq_arr, adam)
            self.W = {k: A[k].reshape(NATURAL[k]) for k in REPLICATED}

        def weights(self, stage, after):
            if stage in AG_NEED:
                got = self.gat.finish(AG_NEED[stage], after)
                if "small" in got:
                    sm = got.pop("small").transpose(1, 0, 2).reshape(40, 512)
                    got.update(conv_dw=sm[0:31], conv_ln_g=sm[31:32], conv_ln_b=sm[32:33], sg_ln_g=sm[33:34],
                               sg_ln_b=sm[34:35])
                self.W.update(got)
            return self.W

        def all_weights(self):
            return self.W

        def poke(self, tag, after):
            if tag in AG_PASS:
                return self.gat.forward(AG_PASS[tag], after)
            if tag[0] == "bwd":
                return self.red.step(after)
            return None

        def grads(self, grads):
            tok = self.red.step(next(iter(grads.values())))
            return self.red.add(grads, after=tok)

    comm = Comm()
    sq, dx, small = run_layers(x[0], p[:, 0], loss_target[0], comm)
    loss = lax.psum(0.5 * jnp.sum(sq) / x.shape[-1], ("x", "y", "c"))
    red = comm.red
    tok = red.step(dx)

    names = REPLICATED + SHARDED_SMALL
    flat = jnp.concatenate([small[k].reshape(-1) for k in names])
    rows = -(-flat.shape[0] // 1024) * 8
    packed = jnp.pad(flat, (0, rows * 128 - flat.shape[0])).reshape(rows, 128)
    sg = Gatherer((["g"],), [place_slot(packed, j_arr)], {"g": ((N_DEV, rows, 128), 0, 1, 1)}, "sg")
    red.last = sg.start(after=tok)
    red.finish_oldest()
    red.finish_oldest()
    sg.forward(0, after=red.last)
    red.finish_oldest()
    gsum_flat = sum_slots(sg.finish(0, after=red.last)["g"]).reshape(-1)
    gs, off = [], 0
    for k in names:
        n = math.prod(small[k].shape)
        g = gsum_flat[off:off + n].reshape(small[k].shape)
        off += n
        if k in SHARDED_SMALL:
            g = lax.dynamic_slice_in_dim(g, j * 64, 64, axis=1)
        gs.append(g.reshape(A[k].shape))
    outs = small_adamw(gs, [(A[k], A["m_" + k], A["v_" + k]) for k in names])
    for a, k in enumerate(names):
        res[k] = (gs[a],) + tuple(outs[3 * a:3 * a + 3])
    red.last = outs[0]
    while red.finish_oldest():
        pass

    for arg in TRANSPOSED_ARGS:
        res[arg] = [jnp.swapaxes(t, 1, 2) for t in res[arg]]
    out = [loss, dx[None]]
    for part in range(4):
        out += [res[k][part] for k in WEIGHT_NAMES]
    return tuple(out)
```

```python
import functools
import math

import jax
import jax.numpy as jnp
from jax import lax
from jax.experimental import pallas as pl
from jax.experimental.pallas import tpu as pltpu

F32, BF16 = jnp.float32, jnp.bfloat16
ALPHA = 4.0 ** 0.25
LN_EPS = 1e-5
QK_SCALE = 0.125
POOL_WINDOWS = (2, 4, 8, 16)
CONV_TAPS = 31
N_DEV = 8
FF_SHARD, FF_PAD = 352, 384
ADAM_LR, ADAM_B1, ADAM_B2, ADAM_EPS, ADAM_WD, ADAM_STEP = 0.001, 0.9, 0.999, 1e-08, 0.01, 10
VMEM_LIMIT = 56 * 1024 * 1024
MESH_T = pl.DeviceIdType.MESH


def _cp(sem=None):
    return pltpu.CompilerParams(dimension_semantics=sem, vmem_limit_bytes=VMEM_LIMIT)


def _dot(a, b):
    return jnp.dot(a, b, preferred_element_type=F32)


def _dot_nt(a, b):
    return lax.dot_general(a, b, (((1,), (1,)), ((), ())), preferred_element_type=F32)


def _dot_tn(a, b):
    return lax.dot_general(a, b, (((0,), (0,)), ((), ())), preferred_element_type=F32)


def _sigmoid(x):
    return 1.0 / (1.0 + jnp.exp(-x))


def _softplus(z):
    return jnp.maximum(z, 0.0) + jnp.log(1.0 + jnp.exp(-jnp.abs(z)))


_GELU_C = math.sqrt(2.0 / math.pi)


def _gelu(x):
    return 0.5 * x * (1.0 + jnp.tanh(_GELU_C * (x + 0.044715 * x * x * x)))


def _gelu_grad(x):
    t = jnp.tanh(_GELU_C * (x + 0.044715 * x * x * x))
    return 0.5 * (1.0 + t) + 0.5 * x * (1.0 - t * t) * _GELU_C * (1.0 + 3.0 * 0.044715 * x * x)


def _ln_fwd(r, g, b):
    mu = jnp.mean(r, axis=-1, keepdims=True)
    xc = r - mu
    var = jnp.mean(xc * xc, axis=-1, keepdims=True)
    rstd = lax.rsqrt(var + LN_EPS)
    xh = xc * rstd
    return xh * g + b, xh, rstd


def _ln_bwd(dy, xh, rstd, g):
    dxh = dy * g
    m1 = jnp.mean(dxh, axis=-1, keepdims=True)
    m2 = jnp.mean(dxh * xh, axis=-1, keepdims=True)
    return rstd * (dxh - m1 - xh * m2)


def _split2(x):
    hi = x.astype(BF16)
    lo = (x - hi.astype(F32)).astype(BF16)
    return hi, lo


def _colsum(x):
    return jnp.sum(x, axis=0, keepdims=True)


def _tok_call(name, body, tiled, full, out_tiled, out_acc=(), tm=256, scratch=(), dep=None):
    def arr(t):
        return t[0] if isinstance(t, tuple) else t
    full = [t[0] if isinstance(t, tuple) and t[1] is None else t for t in full]
    S = arr(tiled[0]).shape[0]
    tm = min(tm, S)
    n_in = len(tiled) + len(full)
    deps = [] if dep is None else [dep]
    if deps:
        inner = body
        body = lambda *refs: inner(*refs[:n_in], *refs[n_in + 1:])

    def tspec(t):
        if isinstance(t, tuple):
            _, w, cb = t
            return pl.BlockSpec((tm, w), lambda i, cb=cb: (i, cb))
        return pl.BlockSpec((tm, t.shape[1]), lambda i: (i, 0))

    def fspec(t):
        if isinstance(t, tuple):
            a, l = t
            nd = a.ndim - 1
            return pl.BlockSpec((None,) + a.shape[1:], lambda i, l=l, nd=nd: (l,) + (0,) * nd)
        nd = t.ndim
        return pl.BlockSpec(t.shape, lambda i, nd=nd: (0,) * nd)

    def ospec(o):
        return pl.BlockSpec((tm, o.shape[1]), lambda i: (i, 0))

    def aspec(o):
        nd = len(o.shape)
        return pl.BlockSpec(o.shape, lambda i, nd=nd: (0,) * nd)

    outs = pl.pallas_call(
        body, name=name, grid=(S // tm,),
        in_specs=[tspec(t) for t in tiled] + [fspec(t) for t in full] + [ANY] * len(deps),
        out_specs=[ospec(o) for o in out_tiled] + [aspec(o) for o in out_acc],
        out_shape=list(out_tiled) + list(out_acc),
        scratch_shapes=list(scratch),
        compiler_params=_cp(("arbitrary",)),
    )(*[arr(t) for t in tiled], *[arr(t) for t in full], *deps)
    return outs


def _sds(shape, dtype=F32):
    return jax.ShapeDtypeStruct(tuple(shape), dtype)


def _acc(ref, val):
    @pl.when(pl.program_id(0) == 0)
    def _():
        ref[...] = val

    @pl.when(pl.program_id(0) != 0)
    def _():
        ref[...] += val


def mm_in(x, w, nb16=0):
    S, N = x.shape[0], w.shape[1]

    def body(x_ref, w_ref, h_ref, xb_ref, *hb_ref):
        xb = x_ref[...].astype(BF16)
        xb_ref[...] = xb
        h = _dot(xb, w_ref[...])
        h_ref[...] = h
        if nb16:
            hb_ref[0][...] = h[:, 0:nb16].astype(BF16)

    outs = [_sds((S, N)), _sds((S, x.shape[1]), BF16)] + ([_sds((S, nb16), BF16)] if nb16 else [])
    return _tok_call("mm_in", body, [x], [w], outs, tm=512)


def _stack_heads(x, hm0, dtype=BF16):
    return jnp.concatenate([jnp.where(hm0, x, 0), jnp.where(hm0, 0, x)], axis=0).astype(dtype)


def _unstack_k(x, T):
    return jnp.concatenate([x[0:T], x[T:2 * T]], axis=1)


def _cumsum_mm(x, u):
    n = x.shape[0]
    hi, lo = _split2(x)
    r = _dot(jnp.concatenate([hi, lo], axis=0), u)
    return r[0:n] + r[n:2 * n]


def attn_fwd(qkv, T=256):
    S = qkv.shape[0]
    T = min(T, S)
    nq = S // T

    def body(q_ref, k_ref, v_ref, o_ref, t_ref, acc_ref, c_ref, qh_ref):
        i = pl.program_id(0)
        hm0 = lax.broadcasted_iota(jnp.int32, (1, 128), 1) < 64
        r2 = lax.broadcasted_iota(jnp.int32, (2 * T, T), 0)
        c2 = lax.broadcasted_iota(jnp.int32, (2 * T, T), 1)
        causal = c2 < jnp.where(r2 >= T, r2 - T, r2)
        ur = lax.broadcasted_iota(jnp.int32, (T, T), 0)
        uc = lax.broadcasted_iota(jnp.int32, (T, T), 1)
        u_incl = (ur >= uc).astype(BF16)
        acc_ref[...] = jnp.zeros_like(acc_ref)
        c_ref[...] = jnp.zeros_like(c_ref)
        for pp in range(4):
            qh_ref[pp] = _stack_heads(q_ref[:, pp * 128:(pp + 1) * 128] * QK_SCALE, hm0)

        def block(kb, diag):
            ks = pl.multiple_of(kb * T, T)
            cols = [slice(pp * 128, (pp + 1) * 128) for pp in range(4)]
            zs = [_dot_nt(qh_ref[pp], k_ref[pl.ds(ks, T), cols[pp]]) for pp in range(4)]
            incls = []
            for pp in range(4):
                sp = _softplus(zs[pp])
                if diag:
                    sp = jnp.where(causal, sp, 0.0)
                incls.append(_cumsum_mm(sp, u_incl))
            for pp in range(4):
                c = c_ref[pp]
                w = jnp.exp(zs[pp] - incls[pp] - c)
                if diag:
                    w = jnp.where(causal, w, 0.0)
                acc_ref[:, cols[pp]] += _dot(_unstack_k(w.astype(BF16), T),
                                             _stack_heads(v_ref[pl.ds(ks, T), cols[pp]], hm0))
                c_ref[pp] = c + jnp.broadcast_to(incls[pp][:, 0:1], (2 * T, T))

        block(i, True)

        def step(jj, carry):
            block(i - 1 - jj, False)
            return carry

        lax.fori_loop(0, i, step, 0)
        o_ref[...] = acc_ref[...].astype(BF16)
        for pp in range(4):
            for hd in range(2):
                t_ref[2 * pp + hd] = c_ref[pp, hd * T:(hd + 1) * T, 0:128]

    return pl.pallas_call(
        body, name="attn_fwd", grid=(nq,),
        in_specs=[pl.BlockSpec((T, 512), lambda i: (i, 0)),
                  pl.BlockSpec((S, 512), lambda i: (0, 1)),
                  pl.BlockSpec((S, 512), lambda i: (0, 2))],
        out_specs=[pl.BlockSpec((T, 512), lambda i: (i, 0)),
                   pl.BlockSpec((8, T, 128), lambda i: (0, i, 0))],
        out_shape=[_sds((S, 512), BF16), _sds((8, S, 128))],
        scratch_shapes=[pltpu.VMEM((T, 512), F32), pltpu.VMEM((4, 2 * T, T), F32), pltpu.VMEM((4, 2 * T, 128), BF16)],
        compiler_params=_cp(("arbitrary",)),
    )(qkv, qkv, qkv)


def pool_fwd(h, pool_w, pool_scale, CH=256):
    S = h.shape[0]
    CH = min(CH, S)

    def body(u_ref, w_ref, sc_ref, b_ref, pooled_ref, pad_ref):
        pad_ref[0:16, :] = jnp.zeros((16, 512), F32)
        pad_ref[16:16 + S, :] = u_ref[...]
        for g, win in enumerate(POOL_WINDOWS):
            cs = slice(g * 128, (g + 1) * 128)
            wq = w_ref[g].astype(BF16)
            for ch in range(S // CH):
                base = ch * CH
                acc = pad_ref[16 + base:16 + base + CH, cs]
                for sft in range(1, win):
                    acc = acc + pad_ref[16 + base - sft:16 + base - sft + CH, cs]
                t = base + lax.broadcasted_iota(jnp.int32, (CH, 1), 0)
                cnt = jnp.minimum(t + 1, win).astype(F32)
                pooled = (acc / cnt - pad_ref[16 + base:16 + base + CH, cs]).astype(BF16)
                pooled_ref[base:base + CH, cs] = pooled
                b_ref[base:base + CH, cs] = (_dot(pooled, wq) * sc_ref[:, cs]).astype(BF16)

    return pl.pallas_call(
        body, name="pool_fwd", grid=(1,),
        in_specs=[pl.BlockSpec((S, 512), lambda i: (0, 3)),
                  pl.BlockSpec((4, 128, 128), lambda i: (0, 0, 0)),
                  pl.BlockSpec((1, 512), lambda i: (0, 0))],
        out_specs=[pl.BlockSpec((S, 512), lambda i: (0, 0)), pl.BlockSpec((S, 512), lambda i: (0, 0))],
        out_shape=[_sds((S, 512), BF16), _sds((S, 512), BF16)],
        scratch_shapes=[pltpu.VMEM((S + 16, 512), F32)],
        compiler_params=_cp(("arbitrary",)),
    )(h, pool_w, pool_scale)


def conv_fwd(h, dw, CH=128):
    S = h.shape[0]

    def body(a_ref, g_ref, dw_ref, y_ref, hc_ref, pad_ref):
        hc = a_ref[...] * _sigmoid(g_ref[...])
        hc_ref[...] = hc
        pad_ref[0:32, :] = jnp.zeros((32, 128), F32)
        pad_ref[32:32 + S, :] = hc
        for ch in range(S // CH):
            base = ch * CH + 2
            acc = dw_ref[0:1, :] * pad_ref[base:base + CH, :]
            for k in range(1, CONV_TAPS):
                acc = acc + dw_ref[k:k + 1, :] * pad_ref[base + k:base + k + CH, :]
            y_ref[ch * CH:(ch + 1) * CH, :] = acc

    return pl.pallas_call(
        body, name="conv_fwd", grid=(4,),
        in_specs=[pl.BlockSpec((S, 128), lambda c: (0, c)),
                  pl.BlockSpec((S, 128), lambda c: (0, 4 + c)),
                  pl.BlockSpec((CONV_TAPS, 128), lambda c: (0, c))],
        out_specs=[pl.BlockSpec((S, 128), lambda c: (0, c)), pl.BlockSpec((S, 128), lambda c: (0, c))],
        out_shape=[_sds((S, 512)), _sds((S, 512))],
        scratch_shapes=[pltpu.VMEM((S + 32, 128), F32)],
        compiler_params=_cp(("arbitrary",)),
    )(h, h, dw)


def _masked_sg_w(w_ref, g):
    row = lax.broadcasted_iota(jnp.int32, (128, 128), 0)
    col = lax.broadcasted_iota(jnp.int32, (128, 128), 1)
    return jnp.where(row >= col, w_ref[g], 0.0).astype(BF16)


def odd_post(y, h, cl_g, cl_b, sl_g, sl_b, sg_w, sgb_bc, tm=256):
    S = y.shape[0]
    tm = min(tm, S)

    def body(y_ref, zc_ref, clg, clb, slg, slb, w_ref, sb_ref,
             c_ref, d_ref, xhc_ref, rsc_ref, xhv_ref, rsv_ref, sv_ref):
        lnc, xhc, rsc = _ln_fwd(y_ref[...], clg[...], clb[...])
        c_ref[...] = (lnc * _sigmoid(lnc)).astype(BF16)
        xhc_ref[...] = xhc
        rsc_ref[...] = rsc
        z = _gelu(zc_ref[...])
        vn, xhv, rsv = _ln_fwd(z[:, 512:], slg[...], slb[...])
        xhv_ref[...] = xhv
        rsv_ref[...] = rsv
        vnb = vn.astype(BF16)
        for g in range(4):
            wm = _masked_sg_w(w_ref, g)
            for ch in range(tm // 128):
                rs, cs = slice(ch * 128, (ch + 1) * 128), slice(g * 128, (g + 1) * 128)
                sv_ref[rs, cs] = _dot(wm, vnb[rs, cs]) + sb_ref[g]
        d_ref[...] = (z[:, :512] * sv_ref[...]).astype(BF16)

    return _tok_call(
        "odd_post", body, [y, (h, 1024, 1)], [cl_g, cl_b, sl_g, sl_b, sg_w, sgb_bc],
        [_sds((S, 512), BF16), _sds((S, 512), BF16), _sds((S, 512)), _sds((S, 1)),
         _sds((S, 512)), _sds((S, 1)), _sds((S, 512))], tm=tm)


def mm_out_ln(l1, l2, x, w, g, b, dep=None):
    S, D = x.shape

    def body(l1_ref, l2_ref, x_ref, w_ref, g_ref, b_ref, y_ref, xh_ref, rs_ref):
        mix = _dot(l1_ref[...], w_ref[0:512, :]) + _dot(l2_ref[...], w_ref[512:1024, :])
        y, xh, rs = _ln_fwd(ALPHA * x_ref[...] + mix, g_ref[...], b_ref[...])
        y_ref[...] = y
        xh_ref[...] = xh
        rs_ref[...] = rs

    return _tok_call("mm_out_ln", body, [l1, l2, x], [w, g, b],
                     [_sds((S, D)), _sds((S, D)), _sds((S, 1))], dep=dep)


def ffn_up(x1, wg, wu, layer, dep=None):
    S, D = x1.shape
    F = wg.shape[-1]

    def body(x_ref, wg_ref, wu_ref, gate_ref, up_ref, hb_ref, xb_ref):
        xb = x_ref[...].astype(BF16)
        xb_ref[...] = xb
        gate = _dot(xb, wg_ref[...])
        up = _dot(xb, wu_ref[...])
        gate_ref[...] = gate.astype(BF16)
        up_ref[...] = up.astype(BF16)
        hb_ref[...] = (gate * _sigmoid(gate) * up).astype(BF16)

    return _tok_call("ffn_up", body, [x1], [(wg, layer), (wu, layer)],
                     [_sds((S, F), BF16), _sds((S, F), BF16), _sds((S, F), BF16), _sds((S, D), BF16)], dep=dep)


def ffn_down_ln(hb, x1, wd, layer, g, b):
    S, D = x1.shape

    def body(h_ref, x_ref, w_ref, g_ref, b_ref, y_ref, xh_ref, rs_ref):
        f = _dot(h_ref[...], w_ref[...])
        y, xh, rs = _ln_fwd(ALPHA * x_ref[...] + f, g_ref[...], b_ref[...])
        y_ref[...] = y
        xh_ref[...] = xh
        rs_ref[...] = rs

    return _tok_call("ffn_down_ln", body, [hb, x1], [(wd, layer), g, b],
                     [_sds((S, D)), _sds((S, D)), _sds((S, 1))])


def ple_fwd(x2, p, wpg, wpp, layer, bg, target=None, dep=None):
    S, D = x2.shape
    last = target is not None

    def body(*refs):
        if last:
            x_ref, p_ref, t_ref, wg_ref, wp_ref, b_ref, x3_ref, sg_ref, pp_ref, xb_ref, pb_ref, dy_ref, ls_ref = refs
        else:
            x_ref, p_ref, wg_ref, wp_ref, b_ref, x3_ref, sg_ref, pp_ref, xb_ref, pb_ref = refs
        x = x_ref[...]
        xb = x.astype(BF16)
        pb = p_ref[...].astype(BF16)
        xb_ref[...] = xb
        pb_ref[...] = pb
        sg = _sigmoid(_dot(xb, wg_ref[...]) + b_ref[...])
        pp = _dot(pb, wp_ref[...])
        sg_ref[...] = sg.astype(BF16)
        pp_ref[...] = pp.astype(BF16)
        x3 = x + sg * pp
        x3_ref[...] = x3
        if last:
            err = x3 - t_ref[...]
            dy_ref[...] = err * (1.0 / D)
            _acc(ls_ref, _colsum(err * err))

    outs = [_sds((S, D)), _sds((S, D), BF16), _sds((S, D), BF16), _sds((S, D), BF16), _sds((S, p.shape[1]), BF16)]
    tiled = [x2, p] + ([target] if last else [])
    if last:
        outs.append(_sds((S, D)))
    return _tok_call("ple_fwd", body, tiled, [(wpg, layer), (wpp, layer), bg], outs,
                     [_sds((1, D))] if last else [], dep=dep)


def ple_ln_bwd(dx3, sg, pp, wpg, xh, rs, g, dep=None):
    S, D = dx3.shape

    def body(d_ref, sg_ref, pp_ref, xh_ref, rs_ref, w_ref, g_ref,
             dgp_ref, dpp_ref, dr_ref, drb_ref, dbg_ref, dlg_ref, dlb_ref):
        d, sg = d_ref[...], sg_ref[...].astype(F32)
        dgp = d * pp_ref[...].astype(F32) * sg * (1.0 - sg)
        dgpb = dgp.astype(BF16)
        dgp_ref[...] = dgpb
        dpp_ref[...] = (d * sg).astype(BF16)
        _acc(dbg_ref, _colsum(dgp))
        dx2 = d + _dot_nt(dgpb, w_ref[...])
        xh = xh_ref[...]
        dr = _ln_bwd(dx2, xh, rs_ref[...], g_ref[...])
        dr_ref[...] = dr
        drb_ref[...] = dr.astype(BF16)
        _acc(dlg_ref, _colsum(dx2 * xh))
        _acc(dlb_ref, _colsum(dx2))

    return _tok_call("ple_ln_bwd", body, [dx3, sg, pp, xh, rs], [wpg, g],
                     [_sds((S, D), BF16), _sds((S, D), BF16), _sds((S, D)), _sds((S, D), BF16)],
                     [_sds((1, D)), _sds((1, D)), _sds((1, D))], dep=dep)


def ffn_bwd(dr_b, x1b, gate, up, hb, wg, wu, wd, TH=256):
    S, D = dr_b.shape
    F = gate.shape[1]

    def body(dr_hbm, x_hbm, gate_ref, up_ref, hb_ref, wg_ref, wu_ref, wd_ref,
             dx_ref, dwg_ref, dwu_ref, dwd_ref, dr_v, x_v, sem, dg_s, du_s):
        @pl.when(pl.program_id(0) == 0)
        def _():
            c1 = pltpu.make_async_copy(dr_hbm, dr_v, sem.at[0])
            c2 = pltpu.make_async_copy(x_hbm, x_v, sem.at[1])
            c1.start()
            c2.start()
            c1.wait()
            c2.wait()
            dx_ref[...] = jnp.zeros_like(dx_ref)

        for ch in range(S // CH):
            rows = slice(ch * CH, (ch + 1) * CH)
            dh = _dot_nt(dr_v[rows, :], wd_ref[...])
            g, u = gate_ref[rows, :].astype(F32), up_ref[rows, :].astype(F32)
            s = _sigmoid(g)
            dgb = (dh * u * s * (1.0 + g * (1.0 - s))).astype(BF16)
            dub = (dh * g * s).astype(BF16)
            dg_s[rows, :] = dgb
            du_s[rows, :] = dub
            dx_ref[rows, :] += _dot_nt(dgb, wg_ref[...]) + _dot_nt(dub, wu_ref[...])
        x = x_v[...]
        dwg_ref[...] = _dot_tn(x, dg_s[...]).astype(BF16)
        dwu_ref[...] = _dot_tn(x, du_s[...]).astype(BF16)
        dwd_ref[...] = _dot_tn(hb_ref[...], dr_v[...]).astype(BF16)

    CH = min(512, S)
    col = lambda rows: pl.BlockSpec((rows, TH), lambda j: (0, j))
    row = pl.BlockSpec((TH, D), lambda j: (j, 0))
    return pl.pallas_call(
        body, name="ffn_bwd", grid=(F // TH,),
        in_specs=[ANY, ANY, col(S), col(S), col(S), col(D), col(D), row],
        out_specs=[pl.BlockSpec((S, D), lambda j: (0, 0)), col(D), col(D), row],
        out_shape=[_sds((S, D)), _sds((D, F), BF16), _sds((D, F), BF16), _sds((F, D), BF16)],
        scratch_shapes=[pltpu.VMEM((S, D), BF16), pltpu.VMEM((S, D), BF16), pltpu.SemaphoreType.DMA((2,)),
                        pltpu.VMEM((S, TH), BF16), pltpu.VMEM((S, TH), BF16)],
        compiler_params=pltpu.CompilerParams(dimension_semantics=("arbitrary",), vmem_limit_bytes=60 * 1024 * 1024),
    )(dr_b, x1b, gate, up, hb, wg, wu, wd)


def mix_bwd(dxp, dr2, xh, rs, g, w, dep=None):
    S, D = dxp.shape

    def body(dxp_ref, dr2_ref, xh_ref, rs_ref, g_ref, w_ref, dr_ref, dmb_ref, dl_ref, dlg_ref, dlb_ref):
        d, xh = ALPHA * dr2_ref[...] + dxp_ref[...], xh_ref[...]
        dr = _ln_bwd(d, xh, rs_ref[...], g_ref[...])
        drb = dr.astype(BF16)
        dr_ref[...] = dr
        dmb_ref[...] = drb
        dl_ref[...] = _dot_nt(drb, w_ref[...])
        _acc(dlg_ref, _colsum(d * xh))
        _acc(dlb_ref, _colsum(d))

    return _tok_call("mix_bwd", body, [dxp, dr2, xh, rs], [g, w],
                     [_sds((S, D)), _sds((S, D), BF16), _sds((S, D))], [_sds((1, D)), _sds((1, D))], dep=dep)


def dx_in(dr, pieces, w):
    S, D = dr.shape
    offs = [o for _, o in pieces]
    widths = [a.shape[1] for a, _ in pieces]

    def body(*refs):
        dr_ref, prefs, w_ref, dx_ref = refs[0], refs[1:1 + len(pieces)], refs[-2], refs[-1]
        acc = ALPHA * dr_ref[...]
        for pr, o, n in zip(prefs, offs, widths):
            acc = acc + _dot_nt(pr[...], w_ref[:, o:o + n])
        dx_ref[...] = acc

    return _tok_call("dx_in", body, [dr] + [a for a, _ in pieces], [w], [_sds((S, D))])[0]


def odd_post_bwd(dl, h, xhc, rsc, xhv, rsv, sv, cl_g, cl_b, sl_g, sl_b, sg_w, tm=256, dep=None):
    S = dl.shape[0]
    tm = min(tm, S)

    def body(dl_ref, zc_ref, xhc_ref, rsc_ref, xhv_ref, rsv_ref, sv_ref, clg, clb, slg, slb, w_ref,
             dy_ref, dzc_ref, dclg_ref, dclb_ref, dslg_ref, dslb_ref, dwm_ref, dsb_ref, dvn_ref):
        first = pl.program_id(0) == 0
        last = pl.program_id(0) == pl.num_programs(0) - 1
        dc, dd = dl_ref[:, 0:512], dl_ref[:, 512:1024]
        xhc = xhc_ref[...]
        lnc = xhc * clg[...] + clb[...]
        s = _sigmoid(lnc)
        dlnc = dc * s * (1.0 + lnc * (1.0 - s))
        dy_ref[...] = _ln_bwd(dlnc, xhc, rsc_ref[...], clg[...])
        _acc(dclg_ref, _colsum(dlnc * xhc))
        _acc(dclb_ref, _colsum(dlnc))
        zc = zc_ref[...]
        z = _gelu(zc)
        dsv = dd * z[:, :512]
        dsvb = dsv.astype(BF16)
        xhv = xhv_ref[...]
        vnb = (xhv * slg[...] + slb[...]).astype(BF16)

        @pl.when(first)
        def _():
            dwm_ref[...] = jnp.zeros_like(dwm_ref)
            dsb_ref[...] = jnp.zeros_like(dsb_ref)

        for g in range(4):
            wm = _masked_sg_w(w_ref, g)
            for ch in range(tm // 128):
                rs_, cs = slice(ch * 128, (ch + 1) * 128), slice(g * 128, (g + 1) * 128)
                dwm_ref[g] += _dot_nt(dsvb[rs_, cs], vnb[rs_, cs])
                dvn_ref[rs_, cs] = _dot_tn(wm, dsvb[rs_, cs])
                dsb_ref[g] += dsv[rs_, cs]
        dvn = dvn_ref[...]
        dvv = _ln_bwd(dvn, xhv, rsv_ref[...], slg[...])
        _acc(dslg_ref, _colsum(dvn * xhv))
        _acc(dslb_ref, _colsum(dvn))
        gg = _gelu_grad(zc)
        dzc_ref[:, 0:512] = (dd * sv_ref[...] * gg[:, :512]).astype(BF16)
        dzc_ref[:, 512:1024] = (dvv * gg[:, 512:]).astype(BF16)

        @pl.when(last)
        def _():
            row = lax.broadcasted_iota(jnp.int32, (128, 128), 0)
            col = lax.broadcasted_iota(jnp.int32, (128, 128), 1)
            for g in range(4):
                dwm_ref[g] = jnp.where(row >= col, dwm_ref[g], 0.0)
                dsb_ref[g] = jnp.broadcast_to(jnp.sum(dsb_ref[g], axis=1, keepdims=True), (128, 128))

    return _tok_call(
        "odd_post_bwd", body, [dl, (h, 1024, 1), xhc, rsc, xhv, rsv, sv], [cl_g, cl_b, sl_g, sl_b, sg_w],
        [_sds((S, 512)), _sds((S, 1024), BF16)],
        [_sds((1, 512)), _sds((1, 512)), _sds((1, 512)), _sds((1, 512)), _sds((4, 128, 128)), _sds((4, 128, 128))],
        tm=tm, scratch=[pltpu.VMEM((tm, 512), F32)], dep=dep)


def conv_bwd(dy, hc, h, dw, CH=128):
    S = dy.shape[0]

    def body(dy_ref, hc_ref, a_ref, g_ref, dw_ref, da_ref, dg_ref, ddw_ref, padh_ref, padd_ref, dhc_ref):
        padh_ref[0:32, :] = jnp.zeros((32, 128), F32)
        padh_ref[32:32 + S, :] = hc_ref[...]
        padd_ref[0:S, :] = dy_ref[...]
        padd_ref[S:S + 32, :] = jnp.zeros((32, 128), F32)
        taps = [jnp.zeros((1, 128), F32) for _ in range(CONV_TAPS)]
        for ch in range(S // CH):
            b0 = ch * CH
            dyc = padd_ref[b0:b0 + CH, :]
            acc = dw_ref[0:1, :] * padd_ref[b0 + 30:b0 + 30 + CH, :]
            taps[0] = taps[0] + _colsum(dyc * padh_ref[b0 + 2:b0 + 2 + CH, :])
            for k in range(1, CONV_TAPS):
                acc = acc + dw_ref[k:k + 1, :] * padd_ref[b0 + 30 - k:b0 + 30 - k + CH, :]
                taps[k] = taps[k] + _colsum(dyc * padh_ref[b0 + 2 + k:b0 + 2 + k + CH, :])
            dhc_ref[b0:b0 + CH, :] = acc
        for k in range(CONV_TAPS):
            ddw_ref[k:k + 1, :] = taps[k]
        dhc = dhc_ref[...]
        s = _sigmoid(g_ref[...])
        da_ref[...] = (dhc * s).astype(BF16)
        dg_ref[...] = (dhc * a_ref[...] * s * (1.0 - s)).astype(BF16)

    return pl.pallas_call(
        body, name="conv_bwd", grid=(4,),
        in_specs=[pl.BlockSpec((S, 128), lambda c: (0, c)),
                  pl.BlockSpec((S, 128), lambda c: (0, c)),
                  pl.BlockSpec((S, 128), lambda c: (0, c)),
                  pl.BlockSpec((S, 128), lambda c: (0, 4 + c)),
                  pl.BlockSpec((CONV_TAPS, 128), lambda c: (0, c))],
        out_specs=[pl.BlockSpec((S, 128), lambda c: (0, c)), pl.BlockSpec((S, 128), lambda c: (0, c)),
                   pl.BlockSpec((CONV_TAPS, 128), lambda c: (0, c))],
        out_shape=[_sds((S, 512), BF16), _sds((S, 512), BF16), _sds((CONV_TAPS, 512))],
        scratch_shapes=[pltpu.VMEM((S + 32, 128), F32), pltpu.VMEM((S + 32, 128), F32), pltpu.VMEM((S, 128), F32)],
        compiler_params=_cp(("arbitrary",)),
    )(dy, hc, h, h, dw)


def attn_bwd(qkv, dl, tb, T=256, dep=None):
    S = qkv.shape[0]
    T = min(T, S)
    nq = S // T

    def body(q_ref, k_ref, v_ref, do_ref, t_ref, dq_ref, dk_ref, dv_ref,
             dka_ref, dva_ref, dqa_ref, pc_ref, gc_ref, qh_ref, doh_ref):
        i = pl.program_id(0)
        hm0 = lax.broadcasted_iota(jnp.int32, (1, 128), 1) < 64
        r2 = lax.broadcasted_iota(jnp.int32, (2 * T, T), 0)
        c2 = lax.broadcasted_iota(jnp.int32, (2 * T, T), 1)
        causal = c2 < jnp.where(r2 >= T, r2 - T, r2)
        ur = lax.broadcasted_iota(jnp.int32, (T, T), 0)
        uc = lax.broadcasted_iota(jnp.int32, (T, T), 1)
        u_le = (ur <= uc).astype(BF16)
        u_lt = (ur < uc).astype(BF16)

        @pl.when(i == 0)
        def _():
            dka_ref[...] = jnp.zeros_like(dka_ref)
            dva_ref[...] = jnp.zeros_like(dva_ref)

        dqa_ref[...] = jnp.zeros_like(dqa_ref)
        gc_ref[...] = jnp.zeros_like(gc_ref)
        for pp in range(4):
            cs = slice(pp * 128, (pp + 1) * 128)
            qh_ref[pp] = _stack_heads(q_ref[:, cs] * QK_SCALE, hm0)
            doh_ref[pp] = _stack_heads(do_ref[:, cs], hm0)
            for hd in range(2):
                for half in range(T // 128):
                    pc_ref[pp, hd * T:(hd + 1) * T, half * 128:(half + 1) * 128] = t_ref[2 * pp + hd]

        def block(kb, diag):
            ks = pl.multiple_of(kb * T, T)
            cols = [slice(pp * 128, (pp + 1) * 128) for pp in range(4)]
            zs = [_dot_nt(qh_ref[pp], k_ref[pl.ds(ks, T), cols[pp]]) for pp in range(4)]
            dws = [_dot_nt(doh_ref[pp], v_ref[pl.ds(ks, T), cols[pp]]) for pp in range(4)]
            a_s, pres = [], []
            for pp in range(4):
                sp = _softplus(zs[pp])
                a_s.append(zs[pp] - sp)
                if diag:
                    sp = jnp.where(causal, sp, 0.0)
                pres.append(_cumsum_mm(sp, u_le))
            ws, gmats, gsums = [], [], []
            for pp in range(4):
                rem = pc_ref[pp]
                w = jnp.exp(a_s[pp] - rem + pres[pp])
                if diag:
                    w = jnp.where(causal, w, 0.0)
                gmat = dws[pp] * w
                ws.append(w.astype(BF16))
                gmats.append(gmat)
                gsums.append(_cumsum_mm(gmat, u_lt))
                pc_ref[pp] = rem - jnp.broadcast_to(pres[pp][:, T - 1:T], (2 * T, T))
            for pp in range(4):
                cs = cols[pp]
                sig = jnp.exp(a_s[pp])
                gex = gc_ref[pp] + gsums[pp]
                dz = gmats[pp] * (1.0 - sig) - sig * gex
                if diag:
                    dz = jnp.where(causal, dz, 0.0)
                dzb = dz.astype(BF16)
                dqa_ref[:, cs] += _dot(_unstack_k(dzb, T), _stack_heads(k_ref[pl.ds(ks, T), cs], hm0))
                dka_ref[pl.ds(ks, T), cs] += _dot_tn(dzb, qh_ref[pp])
                dva_ref[pl.ds(ks, T), cs] += _dot_tn(ws[pp], doh_ref[pp])
                gc_ref[pp] = jnp.broadcast_to(gex[:, T - 1:T] + gmats[pp][:, T - 1:T], (2 * T, T))

        def step(kb, carry):
            block(kb, False)
            return carry

        lax.fori_loop(0, i, step, 0)
        block(i, True)
        dq_ref[...] = (dqa_ref[...] * QK_SCALE).astype(BF16)

        @pl.when(i == nq - 1)
        def _():
            dk_ref[...] = dka_ref[...].astype(BF16)
            dv_ref[...] = dva_ref[...].astype(BF16)

    deps = [] if dep is None else [dep]
    call_body = body if dep is None else (lambda *refs: body(*refs[:5], *refs[6:]))
    return pl.pallas_call(
        call_body, name="attn_bwd", grid=(nq,),
        in_specs=[pl.BlockSpec((T, 512), lambda i: (i, 0)),
                  pl.BlockSpec((S, 512), lambda i: (0, 1)),
                  pl.BlockSpec((S, 512), lambda i: (0, 2)),
                  pl.BlockSpec((T, 512), lambda i: (i, 0)),
                  pl.BlockSpec((8, T, 128), lambda i: (0, i, 0))] + [ANY] * len(deps),
        out_specs=[pl.BlockSpec((T, 512), lambda i: (i, 0)),
                   pl.BlockSpec((S, 512), lambda i: (0, 0)),
                   pl.BlockSpec((S, 512), lambda i: (0, 0))],
        out_shape=[_sds((S, 512), BF16), _sds((S, 512), BF16), _sds((S, 512), BF16)],
        scratch_shapes=[pltpu.VMEM((S, 512), F32), pltpu.VMEM((S, 512), F32), pltpu.VMEM((T, 512), F32),
                        pltpu.VMEM((4, 2 * T, T), F32), pltpu.VMEM((4, 2 * T, T), F32),
                        pltpu.VMEM((4, 2 * T, 128), BF16), pltpu.VMEM((4, 2 * T, 128), BF16)],
        compiler_params=_cp(("arbitrary",)),
    )(qkv, qkv, qkv, dl, tb, *deps)


def pool_bwd(dl, pooled_b, pool_w, pool_scale, CH=256):
    S = dl.shape[0]
    CH = min(CH, S)

    def body(db_ref, pooled_ref, w_ref, sc_ref, du_ref, dw_ref, dsc_ref, pad_ref, dp_ref):
        pad_ref[S:S + 16, :] = jnp.zeros((16, 128), F32)
        for g, win in enumerate(POOL_WINDOWS):
            cs = slice(g * 128, (g + 1) * 128)
            wq = w_ref[g].astype(BF16)
            dwg = jnp.zeros((128, 128), F32)
            dsc = jnp.zeros((1, 128), F32)
            for ch in range(S // CH):
                rs_ = slice(ch * CH, (ch + 1) * CH)
                db = db_ref[rs_, cs]
                pb = pooled_ref[rs_, cs]
                dsc = dsc + _colsum(db * _dot(pb, wq))
                dmsb = (db * sc_ref[:, cs]).astype(BF16)
                dwg = dwg + _dot_tn(pb, dmsb)
                dpool = _dot_nt(dmsb, wq)
                t = ch * CH + lax.broadcasted_iota(jnp.int32, (CH, 1), 0)
                cnt = jnp.minimum(t + 1, win).astype(F32)
                dp_ref[rs_, :] = dpool
                pad_ref[rs_, :] = dpool / cnt
            dw_ref[g] = dwg
            dsc_ref[:, cs] = dsc
            for ch in range(S // CH):
                base = ch * CH
                acc = pad_ref[base:base + CH, :]
                for sft in range(1, win):
                    acc = acc + pad_ref[base + sft:base + sft + CH, :]
                du_ref[base:base + CH, cs] = (acc - dp_ref[base:base + CH, :]).astype(BF16)

    return pl.pallas_call(
        body, name="pool_bwd", grid=(1,),
        in_specs=[pl.BlockSpec((S, 512), lambda i: (0, 1)),
                  pl.BlockSpec((S, 512), lambda i: (0, 0)),
                  pl.BlockSpec((4, 128, 128), lambda i: (0, 0, 0)),
                  pl.BlockSpec((1, 512), lambda i: (0, 0))],
        out_specs=[pl.BlockSpec((S, 512), lambda i: (0, 0)),
                   pl.BlockSpec((4, 128, 128), lambda i: (0, 0, 0)),
                   pl.BlockSpec((1, 512), lambda i: (0, 0))],
        out_shape=[_sds((S, 512), BF16), _sds((4, 128, 128)), _sds((1, 512))],
        scratch_shapes=[pltpu.VMEM((S + 16, 128), F32), pltpu.VMEM((S, 128), F32)],
        compiler_params=_cp(("arbitrary",)),
    )(dl, pooled_b, pool_w, pool_scale)


def tn_into(a, b, out, r0, c0, tk=1024, tn=512):
    S, K = a.shape
    N = b.shape[1]
    tk, tn = min(tk, K), min(tn, N)
    assert K % tk == 0 and N % tn == 0 and r0 % tk == 0 and c0 % tn == 0
    rb, cb = r0 // tk, c0 // tn
    fresh = isinstance(out, jax.ShapeDtypeStruct)

    def body(*refs):
        a_ref, b_ref, o_ref = refs[0], refs[1], refs[-1]
        o_ref[...] = _dot_tn(a_ref[...], b_ref[...]).astype(BF16)

    ospec = pl.BlockSpec((tk, tn), lambda i, j: (rb + i, cb + j))
    in_specs = [pl.BlockSpec((S, tk), lambda i, j: (0, i)), pl.BlockSpec((S, tn), lambda i, j: (0, j))]
    args = [a, b]
    aliases = {}
    if not fresh:
        in_specs += [pl.BlockSpec(memory_space=pl.ANY)]
        args += [out]
        aliases = {2: 0}
    return pl.pallas_call(
        body, name="tn_grad", grid=(K // tk, N // tn),
        in_specs=in_specs, out_specs=[ospec],
        out_shape=[_sds(out.shape, BF16)],
        input_output_aliases=aliases,
        compiler_params=_cp(("arbitrary", "arbitrary")),
    )(*args)[0]


def _row(a, i):
    return a[i:i + 1]


MIXER_NAMES = (("even_w_in", "even_w_out"), ("odd_w_in", "odd_w_out"))


def _tn_group(items):
    out = {}
    for name, (shape, parts) in items.items():
        g = _sds(shape, BF16)
        for a, b, r0, c0 in parts:
            g = tn_into(a, b, g, r0, c0)
        out[name] = g
    return out


def fwd_layer(i, xin, p_i, target, comm):
    s = {}
    W = comm.weights(("mix", i), xin)
    w_in = W[MIXER_NAMES[i][0]]
    if i == 0:
        s["h"], s["xb"], s["qkv"] = mm_in(xin, w_in, nb16=1536)
        comm.poke(("in", i), s["h"])
        s["l1"], s["tb"] = attn_fwd(s["qkv"])
        s["l2"], s["pooled"] = pool_fwd(s["h"], W["pool_w"], W["pool_scale"])
    else:
        s["h"], s["xb"] = mm_in(xin, w_in)
        comm.poke(("in", i), s["h"])
        s["y"], s["hc"] = conv_fwd(s["h"], W["conv_dw"])
        sgb_bc = jnp.broadcast_to(W["sg_b"][:, :, None], (4, 128, 128))
        (s["l1"], s["l2"], s["xhc"], s["rsc"], s["xhv"], s["rsv"], s["sv"]) = odd_post(
            s["y"], s["h"], W["conv_ln_g"], W["conv_ln_b"], W["sg_ln_g"], W["sg_ln_b"], W["sg_w"], sgb_bc)
    tok = comm.poke(("mixed", i), s["l1"])
    W = comm.weights(("out", i), s["l1"])
    x1, s["xh1"], s["rs1"] = mm_out_ln(s["l1"], s["l2"], xin, W[MIXER_NAMES[i][1]], _row(W["ln_mix_g"], i),
                                       _row(W["ln_mix_b"], i), dep=tok)
    W = comm.weights(("ffn", i), x1)
    tok = comm.poke(("up", i), x1)
    s["gate"], s["up"], s["hb"], s["x1b"] = ffn_up(x1, W["ffn_w_gate%d" % i], W["ffn_w_up%d" % i], None, dep=tok)
    W = comm.weights(("down", i), s["hb"])
    x2, s["xh2"], s["rs2"] = ffn_down_ln(s["hb"], x1, W["ffn_w_down%d" % i], None,
                                         _row(W["ln_ffn_g"], i), _row(W["ln_ffn_b"], i))
    tok = comm.poke(("ffn", i), x2)
    outs = ple_fwd(x2, p_i, W["ple_w_gate%d" % i], W["ple_w_proj%d" % i], None, _row(W["ple_b_gate"], i), target,
                   dep=tok)
    s["sg"], s["pp"], s["x2b"], s["pb"] = outs[1:5]
    return outs[0], s, outs[5:]


def bwd_layer(i, dx, s, W, comm, tok=None):
    small = {}
    D = dx.shape[1]
    dgp_b, dpp_b, dr2, dr2_b, small["ple_b_gate"], small["ln_ffn_g"], small["ln_ffn_b"] = ple_ln_bwd(
        dx, s["sg"], s["pp"], W["ple_w_gate%d" % i], s["xh2"], s["rs2"], _row(W["ln_ffn_g"], i), dep=tok)
    dxp, dwg, dwu, dwd = ffn_bwd(dr2_b, s["x1b"], s["gate"], s["up"], s["hb"], W["ffn_w_gate%d" % i],
                                 W["ffn_w_up%d" % i], W["ffn_w_down%d" % i])
    grads = _tn_group({
        "ple_w_gate%d" % i: ((D, D), [(s["x2b"], dgp_b, 0, 0)]),
        "ple_w_proj%d" % i: ((s["pb"].shape[1], D), [(s["pb"], dpp_b, 0, 0)])})
    grads.update({"ffn_w_down%d" % i: dwd, "ffn_w_gate%d" % i: dwg, "ffn_w_up%d" % i: dwu})
    tok = comm.grads(grads)
    iname, oname = MIXER_NAMES[i]
    dr1, dmix_b, dl, small["ln_mix_g"], small["ln_mix_b"] = mix_bwd(
        dxp, dr2, s["xh1"], s["rs1"], _row(W["ln_mix_g"], i), W[oname], dep=tok)
    tok = comm.poke(("bwd", i), dl)
    if i == 1:
        (dy, dzc_b, small["conv_ln_g"], small["conv_ln_b"], small["sg_ln_g"], small["sg_ln_b"],
         small["sg_w"], dsb) = odd_post_bwd(dl, s["h"], s["xhc"], s["rsc"], s["xhv"], s["rsv"], s["sv"],
                                            W["conv_ln_g"], W["conv_ln_b"], W["sg_ln_g"], W["sg_ln_b"], W["sg_w"],
                                            dep=tok)
        small["sg_b"] = dsb[:, :, 0]
        da_b, dg_b, small["conv_dw"] = conv_bwd(dy, s["hc"], s["h"], W["conv_dw"])
        pieces = [(da_b, 0), (dg_b, 512), (dzc_b, 1024)]
    else:
        dq_b, dk_b, dv_b = attn_bwd(s["qkv"], dl, s["tb"], dep=tok)
        du_b, small["pool_w"], small["pool_scale"] = pool_bwd(dl, s["pooled"], W["pool_w"], W["pool_scale"])
        pieces = [(dq_b, 0), (dk_b, 512), (dv_b, 1024), (du_b, 1536)]
    dxin = dx_in(dr1, pieces, W[iname])
    tok = comm.grads(_tn_group({
        oname: ((1024, D), [(s["l1"], dmix_b, 0, 0), (s["l2"], dmix_b, 512, 0)]),
        iname: ((D, 2048), [(s["xb"], a, 0, off) for a, off in pieces])}))
    return dxin, small, tok


def run_layers(x, p, target, comm):
    saved, xin = [], x
    for i in range(2):
        xin, s, extra = fwd_layer(i, xin, p[i], target if i == 1 else None, comm)
        saved.append(s)
    dx, sq = extra
    W = comm.all_weights()
    per_layer = [None, None]
    tok = None
    for i in (1, 0):
        dx, per_layer[i], tok = bwd_layer(i, dx, saved[i], W, comm, tok)
    small = {}
    for k in ("ln_mix_g", "ln_mix_b", "ln_ffn_g", "ln_ffn_b", "ple_b_gate"):
        small[k] = jnp.concatenate([per_layer[0][k], per_layer[1][k]], axis=0)
    for i in range(2):
        small.update({k: v for k, v in per_layer[i].items() if k not in small})
    return sq, dx, small


def _big_table():
    t = {}
    for nm in ("even", "odd"):
        t[nm + "_w_in"] = ((1024, 2048), 1, 256, 256, nm + "_w_in", 0)
        t[nm + "_w_out"] = ((1024, 1024), 0, 128, 128, nm + "_w_out", 0)
    for l in range(2):
        t["ffn_w_gate%d" % l] = ((1024, 8 * FF_PAD), 1, FF_PAD, FF_SHARD, "ffn_w_gate", l)
        t["ffn_w_up%d" % l] = ((1024, 8 * FF_PAD), 1, FF_PAD, FF_SHARD, "ffn_w_up", l)
        t["ffn_w_down%d" % l] = ((8 * FF_PAD, 1024), 0, FF_PAD, FF_SHARD, "ffn_w_down", l)
        t["ple_w_gate%d" % l] = ((1024, 1024), 0, 128, 128, "ple_w_gate", l)
        t["ple_w_proj%d" % l] = ((256, 1024), 1, 128, 128, "ple_w_proj", l)
    return t


BIG = _big_table()
TRANSPOSED_ARGS = ("ffn_w_gate", "ffn_w_up")
SMALL_SPEC = ((N_DEV, 40, 64), 0, 1, 1)
_UP_GROUP = lambda l: ["ffn_w_gate%d" % l, "ffn_w_up%d" % l]
_DOWN_GROUP = lambda l: ["ffn_w_down%d" % l, "ple_w_gate%d" % l, "ple_w_proj%d" % l]
AG_GROUPS = (["even_w_in"], ["even_w_out"], _UP_GROUP(0), _DOWN_GROUP(0), ["odd_w_in", "odd_w_out", "small"],
             _UP_GROUP(1), _DOWN_GROUP(1))
AG_NEED = {("mix", 0): 0, ("out", 0): 1, ("ffn", 0): 2, ("down", 0): 3, ("mix", 1): 4, ("ffn", 1): 5, ("down", 1): 6}
AG_PASS = {("in", 0): 1, ("mixed", 0): 2, ("up", 0): 3, ("ffn", 0): 4, ("mixed", 1): 5, ("up", 1): 6}
ANY = pl.BlockSpec(memory_space=pl.ANY)
SEM = pl.BlockSpec(memory_space=pltpu.SEMAPHORE)


def _spec(name):
    return SMALL_SPEC if name == "small" else BIG[name]


def _win_shape(spec):
    full, axis, w = spec[:3]
    return tuple(w if d == axis else n for d, n in enumerate(full))


def _window(ref, axis, w, j):
    idx = [slice(None)] * len(ref.shape)
    idx[axis] = pl.ds(j, 1) if w == 1 else pl.ds(pl.multiple_of(j * w, w), w)
    return ref.at[tuple(idx)]


def _mesh_pos():
    return lax.axis_index("x"), lax.axis_index("y"), lax.axis_index("c")


def split_call(name, arrays, starts=(), waits=(), sems_in=(), new=(), after=None):
    n, nn, ns = len(arrays), len(new), len(starts)
    flat_sems = [s for pair in sems_in for s in pair]

    def body(*refs):
        arr = list(refs[:n])
        sin = refs[n:n + len(flat_sems)]
        outs = refs[n + len(flat_sems) + (after is not None):]
        data = arr + list(outs[n:n + nn])
        for p, k, kind, mk in waits:
            d = mk(data, sin[2 * p].at[k], sin[2 * p + 1].at[k])
            d.wait_send() if kind == "send" else d.wait_recv()
        if ns:
            send, recv = outs[n + nn], outs[n + nn + 1]
            for k, mk in enumerate(starts):
                mk(data, send.at[k], recv.at[k]).start()
        outs[-1][...] = jnp.zeros((8, 128), F32)

    sem_out = [pltpu.SemaphoreType.DMA((ns,)), pltpu.SemaphoreType.DMA((ns,))] if ns else []
    res = pl.pallas_call(
        body, name=name,
        in_specs=[ANY] * n + [SEM] * len(flat_sems) + ([ANY] if after is not None else []),
        out_specs=[ANY] * (n + nn) + [SEM] * len(sem_out) + [pl.BlockSpec(memory_space=pltpu.VMEM)],
        out_shape=[_sds(a.shape, a.dtype) for a in arrays] + list(new) + sem_out + [_sds((8, 128), F32)],
        input_output_aliases={a: a for a in range(n)},
        compiler_params=pltpu.CompilerParams(has_side_effects=pltpu.SideEffectType.DATAFLOW_SIDE_EFFECTING),
    )(*arrays, *flat_sems, *([after] if after is not None else []))
    return list(res[:n + nn]), (tuple(res[n + nn:n + nn + 2]) if ns else None), res[-1]


def _remote(src, dst, send_sem, recv_sem, dev):
    return pltpu.make_async_remote_copy(src_ref=src, dst_ref=dst, send_sem=send_sem, recv_sem=recv_sem,
                                        device_id=dev, device_id_type=MESH_T)


class Gatherer:
    def __init__(self, groups, arrays, specs, prefix):
        self.groups, self.specs, self.prefix = groups, specs, prefix
        self.names = [nm for g in groups for nm in g]
        self.arr = dict(zip(self.names, arrays))
        self.fwd_sems = {}
        self.forwarded = set()

    @staticmethod
    def _mk_first(ai, spec, k):
        def mk(refs, ss, rs):
            x, y, c = _mesh_pos()
            dev = [(x, y, 1 - c), (1 - x, y, c), (x, 1 - y, c), (1 - x, 1 - y, c)][k]
            win = _window(refs[ai], spec[1], spec[2], 4 * x + 2 * y + c)
            return _remote(win, win, ss, rs, dev)
        return mk

    @staticmethod
    def _mk_fwd(ai, spec, j):
        def mk(refs, ss, rs):
            x, y, c = _mesh_pos()
            px, py = [(1 - x, y), (x, 1 - y), (1 - x, 1 - y)][j]
            win = _window(refs[ai], spec[1], spec[2], 4 * px + 2 * py + c)
            return _remote(win, win, ss, rs, (x, y, 1 - c))
        return mk

    def start(self, after=None):
        starts = [self._mk_first(ai, self.specs[nm], k) for ai, nm in enumerate(self.names) for k in range(4)]
        arrs, self.first_sems, tok = split_call(self.prefix + "_start", [self.arr[nm] for nm in self.names],
                                                starts=starts, after=after)
        self.arr = dict(zip(self.names, arrs))
        return tok

    def forward(self, g, after=None):
        if g in self.forwarded:
            return None
        self.forwarded.add(g)
        names = self.groups[g]
        waits = [(0, 4 * self.names.index(nm) + 1 + j, "recv", self._mk_fwd(ai, self.specs[nm], j))
                 for ai, nm in enumerate(names) for j in range(3)]
        starts = [self._mk_fwd(ai, self.specs[nm], j) for ai, nm in enumerate(names) for j in range(3)]
        arrs, self.fwd_sems[g], tok = split_call(
            "%s_forward%d" % (self.prefix, g), [self.arr[nm] for nm in names], starts=starts, waits=waits,
            sems_in=[self.first_sems], after=after)
        self.arr.update(zip(names, arrs))
        return tok

    def finish(self, g, after=None):
        self.forward(g, after)
        names = self.groups[g]
        waits = []
        for ai, nm in enumerate(names):
            base = 4 * self.names.index(nm)
            waits.append((0, base, "recv", self._mk_first(ai, self.specs[nm], 0)))
            waits += [(1, 3 * ai + j, "recv", self._mk_fwd(ai, self.specs[nm], j)) for j in range(3)]
            waits += [(0, base + k, "send", self._mk_first(ai, self.specs[nm], k)) for k in range(4)]
            waits += [(1, 3 * ai + j, "send", self._mk_fwd(ai, self.specs[nm], j)) for j in range(3)]
        arrs, _, _ = split_call(
            "%s_finish%d" % (self.prefix, g), [self.arr[nm] for nm in names], waits=waits,
            sems_in=[self.first_sems, self.fwd_sems[g]], after=after)
        self.arr.update(zip(names, arrs))
        return {nm: self.arr[nm] for nm in names}


class Reducer:
    def __init__(self, cq_arr, adam):
        self.cq_arr, self.adam = cq_arr, adam
        self.groups = []
        self.n = 0
        self.last = None

    @staticmethod
    def _mk1(gi, li, spec, q):
        def mk(refs, ss, rs):
            x, y, c = _mesh_pos()
            return _remote(_window(refs[gi], spec[1], spec[2], 2 * q + (1 - c)), refs[li].at[q], ss, rs, (x, y, 1 - c))
        return mk

    @staticmethod
    def _mk2(si, li, d):
        def mk(refs, ss, rs):
            x, y, c = _mesh_pos()
            qd = lax.rem(2 * x + y + d, 4)
            return _remote(refs[si].at[d - 1], refs[li].at[3 - d], ss, rs, (lax.div(qd, 2), lax.rem(qd, 2), c))
        return mk

    def add(self, grads, after=None):
        names = list(grads)
        m = len(names)
        starts = [self._mk1(ai, m + ai, BIG[nm], q) for ai, nm in enumerate(names) for q in range(4)]
        new = [_sds((4,) + _win_shape(BIG[nm]), BF16) for nm in names]
        res, sems, tok = split_call("rs1_start%d" % self.n, [grads[nm] for nm in names], starts=starts, new=new,
                                    after=after)
        self.groups.append(dict(names=names, starts=starts, buf=res, sems=sems, stage=1, idx=self.n))
        self.n += 1
        return tok

    def step(self, after):
        tok = None
        for grp in self.groups:
            names, m = grp["names"], len(grp["names"])
            if grp["stage"] == 1:
                waits = [(0, k, kind, mk) for k, mk in enumerate(grp["starts"]) for kind in ("send", "recv")]
                res, _, _ = split_call("rs1_wait%d" % grp["idx"], grp["buf"], waits=waits, sems_in=[grp["sems"]], after=after)
                full, land1 = res[:m], res[m:]
                s1b = list(add_pairs(full, land1, [BIG[nm] for nm in names], self.cq_arr))
                starts = [self._mk2(ai, m + ai, d) for ai in range(m) for d in (1, 2, 3)]
                new = [_sds(a.shape, BF16) for a in s1b]
                res, sems, tok = split_call("rs2_start%d" % grp["idx"], s1b, starts=starts, new=new, after=tok)
                grp.update(stage=2, g=full, land1=land1, starts=starts, buf=res, sems=sems)
        return tok

    def finish_oldest(self):
        for grp in self.groups:
            if grp["stage"] == 2:
                names, m = grp["names"], len(grp["names"])
                waits = [(0, k, kind, mk) for k, mk in enumerate(grp["starts"]) for kind in ("send", "recv")]
                res, _, _ = split_call("rs2_wait%d" % grp["idx"], grp["buf"], waits=waits, sems_in=[grp["sems"]],
                                       after=self.last)
                for nm, g, l1, l2 in zip(names, grp["g"], grp["land1"], res[m:]):
                    self.last = self.adam(nm, g, l1, l2, self.last)
                grp["stage"] = 3
                return True
        return False


def pack_weights(args, arg_names, small_blk, names, j_arr):
    n_in = len(args)

    def body(j_ref, *refs):
        for o, nm in enumerate(names):
            dst = refs[n_in + 1 + o]
            if nm == "small":
                dst[...] = refs[n_in][...]
                continue
            _, axis, w, valid, arg, layer = BIG[nm]
            if arg in TRANSPOSED_ARGS:
                s = refs[arg_names.index(arg)][layer]
                s = jnp.concatenate([s, jnp.zeros((w - valid, s.shape[1]), F32)], axis=0)
                dst[...] = s.T.astype(BF16)
                continue
            src = refs[arg_names.index(arg)][layer].astype(BF16)
            if valid == w:
                dst[...] = src
            else:
                dst[...] = jnp.zeros(dst.shape, BF16)
                if axis == 1:
                    dst[:, 0:valid] = src
                else:
                    dst[0:valid, :] = src

    def ispec(a):
        return pl.BlockSpec(a.shape, lambda i, j_ref: (0, 0, 0))

    def ospec(spec):
        axis, nd = spec[1], len(spec[0])
        return pl.BlockSpec(_win_shape(spec),
                            lambda i, j_ref, axis=axis, nd=nd: tuple(j_ref[0] if d == axis else 0 for d in range(nd)))

    specs = [_spec(nm) for nm in names]
    return pl.pallas_call(
        body, name="pack_weights",
        grid_spec=pltpu.PrefetchScalarGridSpec(
            num_scalar_prefetch=1, grid=(1,),
            in_specs=[ispec(a) for a in list(args) + [small_blk]], out_specs=[ospec(s) for s in specs]),
        out_shape=[_sds(s[0], F32 if nm == "small" else BF16) for nm, s in zip(names, specs)],
        compiler_params=_cp(("arbitrary",)),
    )(j_arr, *args, small_blk)


def add_pairs(fulls, lands, specs, cq_arr):
    def chip(d, cq):
        return lax.rem(cq[1] + d + 1, 4)

    in_specs, args = [], []
    for full, land, spec in zip(fulls, lands, specs):
        axis, w = spec[1], spec[2]
        R, C = full.shape
        for d in range(3):
            if axis == 1:
                in_specs.append(pl.BlockSpec((R, w), lambda i, cq, d=d: (0, 2 * chip(d, cq) + cq[0])))
                in_specs.append(pl.BlockSpec((None, R, w), lambda i, cq, d=d: (chip(d, cq), 0, 0)))
            else:
                in_specs.append(pl.BlockSpec((w, C), lambda i, cq, d=d: (2 * chip(d, cq) + cq[0], 0)))
                in_specs.append(pl.BlockSpec((None, w, C), lambda i, cq, d=d: (chip(d, cq), 0, 0)))
            args += [full, land]
    out_shape = [_sds((3,) + land.shape[1:], BF16) for land in lands]
    n = len(fulls)

    def body(cq_ref, *refs):
        for a in range(n):
            for d in range(3):
                own, got = refs[6 * a + 2 * d], refs[6 * a + 2 * d + 1]
                refs[6 * n + a][d] = (own[...].astype(F32) + got[...].astype(F32)).astype(BF16)

    return pl.pallas_call(
        body, name="add_pairs",
        grid_spec=pltpu.PrefetchScalarGridSpec(
            num_scalar_prefetch=1, grid=(1,), in_specs=in_specs,
            out_specs=[pl.BlockSpec(o.shape, lambda i, cq: (0, 0, 0)) for o in out_shape]),
        out_shape=out_shape,
        compiler_params=_cp(("arbitrary",)),
    )(cq_arr, *args)


def _adamw(w, g, m, v):
    m = ADAM_B1 * m + (1.0 - ADAM_B1) * g
    v = ADAM_B2 * v + (1.0 - ADAM_B2) * (g * g)
    m_hat = m / (1.0 - ADAM_B1 ** ADAM_STEP)
    v_hat = v / (1.0 - ADAM_B2 ** ADAM_STEP)
    delta = -ADAM_LR * (m_hat / (jnp.sqrt(v_hat) + ADAM_EPS) + ADAM_WD * w)
    return delta, m, v


def reduce_adamw(full, land1, land, w, m, v, spec, cq_arr, prev=None, dep=None):
    axis, win, valid, layer = spec[1], spec[2], spec[3], spec[5]
    L, R, C = w.shape
    transposed = spec[4] in TRANSPOSED_ARGS
    if transposed:
        grid = (1,)
        fspec = pl.BlockSpec((C, win), lambda i, cq: (0, 2 * cq[1] + cq[0]))
        wspec = pl.BlockSpec((None, C, win), lambda i, cq: (cq[1], 0, 0))
        lspec = pl.BlockSpec((3, C, win), lambda i, cq: (0, 0, 0))
        sspec = pl.BlockSpec((None, R, C), lambda i, cq: (layer, 0, 0))
    elif axis == 1:
        tr = min(1024, R)
        grid = (R // tr,)
        fspec = pl.BlockSpec((tr, win), lambda i, cq: (i, 2 * cq[1] + cq[0]))
        wspec = pl.BlockSpec((None, tr, win), lambda i, cq: (cq[1], i, 0))
        lspec = pl.BlockSpec((3, tr, win), lambda i, cq: (0, i, 0))
        sspec = pl.BlockSpec((None, tr, C), lambda i, cq: (layer, i, 0))
    else:
        grid = (1,)
        fspec = pl.BlockSpec((win, full.shape[1]), lambda i, cq: (2 * cq[1] + cq[0], 0))
        wspec = pl.BlockSpec((None, win, C), lambda i, cq: (cq[1], 0, 0))
        lspec = pl.BlockSpec((3, win, C), lambda i, cq: (0, 0, 0))
        sspec = pl.BlockSpec((None, R, C), lambda i, cq: (layer, 0, 0))

    def body(cq_ref, full_ref, own_ref, land_ref, w_ref, m_ref, v_ref, *rest):
        g_ref, d_ref, nm_ref, nv_ref = rest[-4:]
        if transposed:
            rd = lambda r, *lead: r[lead] if lead else r[...]
        elif axis == 1:
            rd = lambda r, *lead: r[(*lead, slice(None), slice(0, valid))]
        else:
            rd = lambda r, *lead: r[(*lead, slice(0, valid), slice(None))]
        g = rd(full_ref).astype(F32) + rd(own_ref).astype(F32)
        for k in range(3):
            g = g + rd(land_ref, k).astype(F32)
        if transposed:
            g = g.T[0:valid, :]
        g_ref[...] = g
        d, nm, nv = _adamw(w_ref[...], g, m_ref[...], v_ref[...])
        d_ref[...] = d
        nm_ref[...] = nm
        nv_ref[...] = nv

    extra = (list(prev) if prev is not None else []) + ([dep] if dep is not None else [])
    return pl.pallas_call(
        body, name="reduce_adamw",
        grid_spec=pltpu.PrefetchScalarGridSpec(
            num_scalar_prefetch=1, grid=grid,
            in_specs=[fspec, wspec, lspec, sspec, sspec, sspec] + [ANY] * len(extra), out_specs=[sspec] * 4),
        out_shape=[_sds(w.shape)] * 4,
        input_output_aliases={7 + k: k for k in range(4 if prev is not None else 0)},
        compiler_params=_cp(("arbitrary",)),
    )(cq_arr, full, land1, land, w, m, v, *extra)


def place_slot(packed, j_arr):
    R = packed.shape[0]

    def body(j_ref, src, dst):
        dst[...] = src[...]

    return pl.pallas_call(
        body, name="place_slot",
        grid_spec=pltpu.PrefetchScalarGridSpec(
            num_scalar_prefetch=1, grid=(1,),
            in_specs=[pl.BlockSpec((R, 128), lambda i, j: (0, 0))],
            out_specs=[pl.BlockSpec((None, R, 128), lambda i, j: (j[0], 0, 0))]),
        out_shape=[_sds((N_DEV, R, 128))], compiler_params=_cp(("arbitrary",)),
    )(j_arr, packed)[0]


def sum_slots(gathered):
    def body(g_ref, o_ref):
        g = g_ref[0]
        for dev in range(1, N_DEV):
            g = g + g_ref[dev]
        o_ref[...] = g

    return pl.pallas_call(body, name="sum_slots", out_shape=_sds(gathered.shape[1:]), compiler_params=_cp())(gathered)


def small_adamw(gs, wmv):
    k = len(gs)

    def body(*refs):
        for a in range(k):
            g, w, m, v = refs[4 * a:4 * a + 4]
            d, nm, nv = _adamw(w[...], g[...], m[...], v[...])
            refs[4 * k + 3 * a][...] = d
            refs[4 * k + 3 * a + 1][...] = nm
            refs[4 * k + 3 * a + 2][...] = nv

    args = [t for g, tup in zip(gs, wmv) for t in (g,) + tuple(tup)]
    out_shape = [_sds(g.shape) for g in gs for _ in range(3)]
    return pl.pallas_call(body, name="small_adamw", out_shape=out_shape, compiler_params=_cp())(*args)


WEIGHT_NAMES = ("even_w_in", "even_w_out", "pool_w", "pool_scale", "odd_w_in", "odd_w_out", "conv_dw", "conv_ln_g",
                "conv_ln_b", "sg_ln_g", "sg_ln_b", "sg_w", "sg_b", "ln_mix_g", "ln_mix_b", "ffn_w_gate", "ffn_w_up",
                "ffn_w_down", "ln_ffn_g", "ln_ffn_b", "ple_w_proj", "ple_w_gate", "ple_b_gate")
PACK_ARGS = ("even_w_in", "even_w_out", "odd_w_in", "odd_w_out", "ffn_w_gate", "ffn_w_up", "ffn_w_down",
             "ple_w_gate", "ple_w_proj")
REPLICATED = ("pool_w", "pool_scale", "sg_w", "sg_b", "ln_mix_g", "ln_mix_b", "ln_ffn_g", "ln_ffn_b", "ple_b_gate")
SHARDED_SMALL = ("conv_dw", "conv_ln_g", "conv_ln_b", "sg_ln_g", "sg_ln_b")
NATURAL = {"pool_w": (4, 128, 128), "pool_scale": (1, 512), "sg_w": (4, 128, 128), "sg_b": (4, 128),
           "ln_mix_g": (2, 1024), "ln_mix_b": (2, 1024), "ln_ffn_g": (2, 1024), "ln_ffn_b": (2, 1024),
           "ple_b_gate": (2, 1024)}


def kernel(x, p, even_w_in, even_w_out, pool_w, pool_scale, odd_w_in, odd_w_out, conv_dw, conv_ln_g, conv_ln_b, sg_ln_g, sg_ln_b, sg_w, sg_b, ln_mix_g, ln_mix_b, ffn_w_gate, ffn_w_up, ffn_w_down, ln_ffn_g, ln_ffn_b, ple_w_proj, ple_w_gate, ple_b_gate, loss_target, m_even_w_in, m_even_w_out, m_pool_w, m_pool_scale, m_odd_w_in, m_odd_w_out, m_conv_dw, m_conv_ln_g, m_conv_ln_b, m_sg_ln_g, m_sg_ln_b, m_sg_w, m_sg_b, m_ln_mix_g, m_ln_mix_b, m_ffn_w_gate, m_ffn_w_up, m_ffn_w_down, m_ln_ffn_g, m_ln_ffn_b, m_ple_w_proj, m_ple_w_gate, m_ple_b_gate, v_even_w_in, v_even_w_out, v_pool_w, v_pool_scale, v_odd_w_in, v_odd_w_out, v_conv_dw, v_conv_ln_g, v_conv_ln_b, v_sg_ln_g, v_sg_ln_b, v_sg_w, v_sg_b, v_ln_mix_g, v_ln_mix_b, v_ffn_w_gate, v_ffn_w_up, v_ffn_w_down, v_ln_ffn_g, v_ln_ffn_b, v_ple_w_proj, v_ple_w_gate, v_ple_b_gate):
    A = dict(locals())
    for arg in TRANSPOSED_ARGS:
        for pre in ("", "m_", "v_"):
            A[pre + arg] = jnp.swapaxes(A[pre + arg], 1, 2)
    mx, my, mc = _mesh_pos()
    j = 4 * mx + 2 * my + mc
    j_arr = j.astype(jnp.int32).reshape(1)
    cq_arr = jnp.stack([mc, 2 * mx + my]).astype(jnp.int32)
    res = {}

    def adam(nm, full, land1, land2, dep):
        arg = BIG[nm][4]
        res[arg] = reduce_adamw(full, land1, land2, A[arg], A["m_" + arg], A["v_" + arg], BIG[nm], cq_arr,
                                res.get(arg), dep)
        return res[arg][0]

    class Comm:
        def __init__(self):
            names = [nm for g in AG_GROUPS for nm in g]
            small_blk = jnp.concatenate([conv_dw[0], conv_ln_g, conv_ln_b, sg_ln_g, sg_ln_b, jnp.zeros((5, 64), F32)], axis=0)
            mine = pack_weights([A[k] for k in PACK_ARGS], PACK_ARGS, small_blk[None], names, j_arr)
            self.gat = Gatherer(AG_GROUPS, mine, {nm: _spec(nm) for nm in names}, "ag")
            self.gat.start()
            self.red = Reducer(cq_arr, adam)
            self.W = {k: A[k].reshape(NATURAL[k]) for k in REPLICATED}

        def weights(self, stage, after):
            if stage in AG_NEED:
                got = self.gat.finish(AG_NEED[stage], after)
                if "small" in got:
                    sm = got.pop("small").transpose(1, 0, 2).reshape(40, 512)
                    got.update(conv_dw=sm[0:31], conv_ln_g=sm[31:32], conv_ln_b=sm[32:33], sg_ln_g=sm[33:34],
                               sg_ln_b=sm[34:35])
                self.W.update(got)
            return self.W

        def all_weights(self):
            return self.W

        def poke(self, tag, after):
            if tag in AG_PASS:
                return self.gat.forward(AG_PASS[tag], after)
            if tag[0] == "bwd":
                return self.red.step(after)
            return None

        def grads(self, grads):
            tok = self.red.step(next(iter(grads.values())))
            return self.red.add(grads, after=tok)

    comm = Comm()
    sq, dx, small = run_layers(x[0], p[:, 0], loss_target[0], comm)
    loss = lax.psum(0.5 * jnp.sum(sq) / x.shape[-1], ("x", "y", "c"))
    red = comm.red
    tok = red.step(dx)

    names = REPLICATED + SHARDED_SMALL
    flat = jnp.concatenate([small[k].reshape(-1) for k in names])
    rows = -(-flat.shape[0] // 1024) * 8
    packed = jnp.pad(flat, (0, rows * 128 - flat.shape[0])).reshape(rows, 128)
    sg = Gatherer((["g"],), [place_slot(packed, j_arr)], {"g": ((N_DEV, rows, 128), 0, 1, 1)}, "sg")
    red.last = sg.start(after=tok)
    red.finish_oldest()
    red.finish_oldest()
    sg.forward(0, after=red.last)
    red.finish_oldest()
    gsum_flat = sum_slots(sg.finish(0, after=red.last)["g"]).reshape(-1)
    gs, off = [], 0
    for k in names:
        n = math.prod(small[k].shape)
        g = gsum_flat[off:off + n].reshape(small[k].shape)
        off += n
        if k in SHARDED_SMALL:
            g = lax.dynamic_slice_in_dim(g, j * 64, 64, axis=1)
        gs.append(g.reshape(A[k].shape))
    outs = small_adamw(gs, [(A[k], A["m_" + k], A["v_" + k]) for k in names])
    for a, k in enumerate(names):
        res[k] = (gs[a],) + tuple(outs[3 * a:3 * a + 3])
    red.last = outs[0]
    while red.finish_oldest():
        pass

    for arg in TRANSPOSED_ARGS:
        res[arg] = [jnp.swapaxes(t, 1, 2) for t in res[arg]]
    out = [loss, dx[None]]
    for part in range(4):
        out += [res[k][part] for k in WEIGHT_NAMES]
    return tuple(out)
```

```python
import functools
import math

import jax
import jax.numpy as jnp
from jax import lax
from jax.experimental import pallas as pl
from jax.experimental.pallas import tpu as pltpu

F32, BF16 = jnp.float32, jnp.bfloat16
ALPHA = 4.0 ** 0.25
LN_EPS = 1e-5
QK_SCALE = 0.125
POOL_WINDOWS = (2, 4, 8, 16)
CONV_TAPS = 31
N_DEV = 8
FF_SHARD, FF_PAD = 352, 384
ADAM_LR, ADAM_B1, ADAM_B2, ADAM_EPS, ADAM_WD, ADAM_STEP = 0.001, 0.9, 0.999, 1e-08, 0.01, 10
VMEM_LIMIT = 56 * 1024 * 1024
MESH_T = pl.DeviceIdType.MESH


def _cp(sem=None):
    return pltpu.CompilerParams(dimension_semantics=sem, vmem_limit_bytes=VMEM_LIMIT)


def _dot(a, b):
    return jnp.dot(a, b, preferred_element_type=F32)


def _dot_nt(a, b):
    return lax.dot_general(a, b, (((1,), (1,)), ((), ())), preferred_element_type=F32)


def _dot_tn(a, b):
    return lax.dot_general(a, b, (((0,), (0,)), ((), ())), preferred_element_type=F32)


def _sigmoid(x):
    return 1.0 / (1.0 + jnp.exp(-x))


def _softplus(z):
    return jnp.maximum(z, 0.0) + jnp.log(1.0 + jnp.exp(-jnp.abs(z)))


_GELU_C = math.sqrt(2.0 / math.pi)


def _gelu(x):
    return 0.5 * x * (1.0 + jnp.tanh(_GELU_C * (x + 0.044715 * x * x * x)))


def _gelu_grad(x):
    t = jnp.tanh(_GELU_C * (x + 0.044715 * x * x * x))
    return 0.5 * (1.0 + t) + 0.5 * x * (1.0 - t * t) * _GELU_C * (1.0 + 3.0 * 0.044715 * x * x)


def _ln_fwd(r, g, b):
    mu = jnp.mean(r, axis=-1, keepdims=True)
    xc = r - mu
    var = jnp.mean(xc * xc, axis=-1, keepdims=True)
    rstd = lax.rsqrt(var + LN_EPS)
    xh = xc * rstd
    return xh * g + b, xh, rstd


def _ln_bwd(dy, xh, rstd, g):
    dxh = dy * g
    m1 = jnp.mean(dxh, axis=-1, keepdims=True)
    m2 = jnp.mean(dxh * xh, axis=-1, keepdims=True)
    return rstd * (dxh - m1 - xh * m2)


def _split2(x):
    hi = x.astype(BF16)
    lo = (x - hi.astype(F32)).astype(BF16)
    return hi, lo


def _colsum(x):
    return jnp.sum(x, axis=0, keepdims=True)


def _tok_call(name, body, tiled, full, out_tiled, out_acc=(), tm=256, scratch=(), dep=None):
    def arr(t):
        return t[0] if isinstance(t, tuple) else t
    full = [t[0] if isinstance(t, tuple) and t[1] is None else t for t in full]
    S = arr(tiled[0]).shape[0]
    tm = min(tm, S)
    n_in = len(tiled) + len(full)
    deps = [] if dep is None else [dep]
    if deps:
        inner = body
        body = lambda *refs: inner(*refs[:n_in], *refs[n_in + 1:])

    def tspec(t):
        if isinstance(t, tuple):
            _, w, cb = t
            return pl.BlockSpec((tm, w), lambda i, cb=cb: (i, cb))
        return pl.BlockSpec((tm, t.shape[1]), lambda i: (i, 0))

    def fspec(t):
        if isinstance(t, tuple):
            a, l = t
            nd = a.ndim - 1
            return pl.BlockSpec((None,) + a.shape[1:], lambda i, l=l, nd=nd: (l,) + (0,) * nd)
        nd = t.ndim
        return pl.BlockSpec(t.shape, lambda i, nd=nd: (0,) * nd)

    def ospec(o):
        return pl.BlockSpec((tm, o.shape[1]), lambda i: (i, 0))

    def aspec(o):
        nd = len(o.shape)
        return pl.BlockSpec(o.shape, lambda i, nd=nd: (0,) * nd)

    outs = pl.pallas_call(
        body, name=name, grid=(S // tm,),
        in_specs=[tspec(t) for t in tiled] + [fspec(t) for t in full] + [ANY] * len(deps),
        out_specs=[ospec(o) for o in out_tiled] + [aspec(o) for o in out_acc],
        out_shape=list(out_tiled) + list(out_acc),
        scratch_shapes=list(scratch),
        compiler_params=_cp(("arbitrary",)),
    )(*[arr(t) for t in tiled], *[arr(t) for t in full], *deps)
    return outs


def _sds(shape, dtype=F32):
    return jax.ShapeDtypeStruct(tuple(shape), dtype)


def _acc(ref, val):
    @pl.when(pl.program_id(0) == 0)
    def _():
        ref[...] = val

    @pl.when(pl.program_id(0) != 0)
    def _():
        ref[...] += val


def mm_in(x, w, nb16=0):
    S, N = x.shape[0], w.shape[1]

    def body(x_ref, w_ref, h_ref, xb_ref, *hb_ref):
        xb = x_ref[...].astype(BF16)
        xb_ref[...] = xb
        h = _dot(xb, w_ref[...])
        h_ref[...] = h
        if nb16:
            hb_ref[0][...] = h[:, 0:nb16].astype(BF16)

    outs = [_sds((S, N)), _sds((S, x.shape[1]), BF16)] + ([_sds((S, nb16), BF16)] if nb16 else [])
    return _tok_call("mm_in", body, [x], [w], outs, tm=512)


def _stack_heads(x, hm0, dtype=BF16):
    return jnp.concatenate([jnp.where(hm0, x, 0), jnp.where(hm0, 0, x)], axis=0).astype(dtype)


def _unstack_k(x, T):
    return jnp.concatenate([x[0:T], x[T:2 * T]], axis=1)


def _cumsum_mm(x, u):
    n = x.shape[0]
    hi, lo = _split2(x)
    r = _dot(jnp.concatenate([hi, lo], axis=0), u)
    return r[0:n] + r[n:2 * n]


def attn_fwd(qkv, T=256):
    S = qkv.shape[0]
    T = min(T, S)
    nq = S // T

    def body(q_ref, k_ref, v_ref, o_ref, t_ref, acc_ref, c_ref, qh_ref):
        i = pl.program_id(0)
        hm0 = lax.broadcasted_iota(jnp.int32, (1, 128), 1) < 64
        r2 = lax.broadcasted_iota(jnp.int32, (2 * T, T), 0)
        c2 = lax.broadcasted_iota(jnp.int32, (2 * T, T), 1)
        causal = c2 < jnp.where(r2 >= T, r2 - T, r2)
        ur = lax.broadcasted_iota(jnp.int32, (T, T), 0)
        uc = lax.broadcasted_iota(jnp.int32, (T, T), 1)
        u_incl = (ur >= uc).astype(BF16)
        acc_ref[...] = jnp.zeros_like(acc_ref)
        c_ref[...] = jnp.zeros_like(c_ref)
        for pp in range(4):
            qh_ref[pp] = _stack_heads(q_ref[:, pp * 128:(pp + 1) * 128] * QK_SCALE, hm0)

        def block(kb, diag):
            ks = pl.multiple_of(kb * T, T)
            cols = [slice(pp * 128, (pp + 1) * 128) for pp in range(4)]
            zs = [_dot_nt(qh_ref[pp], k_ref[pl.ds(ks, T), cols[pp]]) for pp in range(4)]
            incls = []
            for pp in range(4):
                sp = _softplus(zs[pp])
                if diag:
                    sp = jnp.where(causal, sp, 0.0)
                incls.append(_cumsum_mm(sp, u_incl))
            for pp in range(4):
                c = c_ref[pp]
                w = jnp.exp(zs[pp] - incls[pp] - c)
                if diag:
                    w = jnp.where(causal, w, 0.0)
                acc_ref[:, cols[pp]] += _dot(_unstack_k(w.astype(BF16), T),
                                             _stack_heads(v_ref[pl.ds(ks, T), cols[pp]], hm0))
                c_ref[pp] = c + jnp.broadcast_to(incls[pp][:, 0:1], (2 * T, T))

        block(i, True)

        def step(jj, carry):
            block(i - 1 - jj, False)
            return carry

        lax.fori_loop(0, i, step, 0)
        o_ref[...] = acc_ref[...].astype(BF16)
        for pp in range(4):
            for hd in range(2):
                t_ref[2 * pp + hd] = c_ref[pp, hd * T:(hd + 1) * T, 0:128]

    return pl.pallas_call(
        body, name="attn_fwd", grid=(nq,),
        in_specs=[pl.BlockSpec((T, 512), lambda i: (i, 0)),
                  pl.BlockSpec((S, 512), lambda i: (0, 1)),
                  pl.BlockSpec((S, 512), lambda i: (0, 2))],
        out_specs=[pl.BlockSpec((T, 512), lambda i: (i, 0)),
                   pl.BlockSpec((8, T, 128), lambda i: (0, i, 0))],
        out_shape=[_sds((S, 512), BF16), _sds((8, S, 128))],
        scratch_shapes=[pltpu.VMEM((T, 512), F32), pltpu.VMEM((4, 2 * T, T), F32), pltpu.VMEM((4, 2 * T, 128), BF16)],
        compiler_params=_cp(("arbitrary",)),
    )(qkv, qkv, qkv)


def pool_fwd(h, pool_w, pool_scale, CH=256):
    S = h.shape[0]
    CH = min(CH, S)

    def body(u_ref, w_ref, sc_ref, b_ref, pooled_ref, pad_ref):
        pad_ref[0:16, :] = jnp.zeros((16, 512), F32)
        pad_ref[16:16 + S, :] = u_ref[...]
        for g, win in enumerate(POOL_WINDOWS):
            cs = slice(g * 128, (g + 1) * 128)
            wq = w_ref[g].astype(BF16)
            for ch in range(S // CH):
                base = ch * CH
                acc = pad_ref[16 + base:16 + base + CH, cs]
                for sft in range(1, win):
                    acc = acc + pad_ref[16 + base - sft:16 + base - sft + CH, cs]
                t = base + lax.broadcasted_iota(jnp.int32, (CH, 1), 0)
                cnt = jnp.minimum(t + 1, win).astype(F32)
                pooled = (acc / cnt - pad_ref[16 + base:16 + base + CH, cs]).astype(BF16)
                pooled_ref[base:base + CH, cs] = pooled
                b_ref[base:base + CH, cs] = (_dot(pooled, wq) * sc_ref[:, cs]).astype(BF16)

    return pl.pallas_call(
        body, name="pool_fwd", grid=(1,),
        in_specs=[pl.BlockSpec((S, 512), lambda i: (0, 3)),
                  pl.BlockSpec((4, 128, 128), lambda i: (0, 0, 0)),
                  pl.BlockSpec((1, 512), lambda i: (0, 0))],
        out_specs=[pl.BlockSpec((S, 512), lambda i: (0, 0)), pl.BlockSpec((S, 512), lambda i: (0, 0))],
        out_shape=[_sds((S, 512), BF16), _sds((S, 512), BF16)],
        scratch_shapes=[pltpu.VMEM((S + 16, 512), F32)],
        compiler_params=_cp(("arbitrary",)),
    )(h, pool_w, pool_scale)


def conv_fwd(h, dw, CH=128):
    S = h.shape[0]

    def body(a_ref, g_ref, dw_ref, y_ref, hc_ref, pad_ref):
        hc = a_ref[...] * _sigmoid(g_ref[...])
        hc_ref[...] = hc
        pad_ref[0:32, :] = jnp.zeros((32, 128), F32)
        pad_ref[32:32 + S, :] = hc
        for ch in range(S // CH):
            base = ch * CH + 2
            acc = dw_ref[0:1, :] * pad_ref[base:base + CH, :]
            for k in range(1, CONV_TAPS):
                acc = acc + dw_ref[k:k + 1, :] * pad_ref[base + k:base + k + CH, :]
            y_ref[ch * CH:(ch + 1) * CH, :] = acc

    return pl.pallas_call(
        body, name="conv_fwd", grid=(4,),
        in_specs=[pl.BlockSpec((S, 128), lambda c: (0, c)),
                  pl.BlockSpec((S, 128), lambda c: (0, 4 + c)),
                  pl.BlockSpec((CONV_TAPS, 128), lambda c: (0, c))],
        out_specs=[pl.BlockSpec((S, 128), lambda c: (0, c)), pl.BlockSpec((S, 128), lambda c: (0, c))],
        out_shape=[_sds((S, 512)), _sds((S, 512))],
        scratch_shapes=[pltpu.VMEM((S + 32, 128), F32)],
        compiler_params=_cp(("arbitrary",)),
    )(h, h, dw)


def _masked_sg_w(w_ref, g):
    row = lax.broadcasted_iota(jnp.int32, (128, 128), 0)
    col = lax.broadcasted_iota(jnp.int32, (128, 128), 1)
    return jnp.where(row >= col, w_ref[g], 0.0).astype(BF16)


def odd_post(y, h, cl_g, cl_b, sl_g, sl_b, sg_w, sgb_bc, tm=256):
    S = y.shape[0]
    tm = min(tm, S)

    def body(y_ref, zc_ref, clg, clb, slg, slb, w_ref, sb_ref,
             c_ref, d_ref, xhc_ref, rsc_ref, xhv_ref, rsv_ref, sv_ref):
        lnc, xhc, rsc = _ln_fwd(y_ref[...], clg[...], clb[...])
        c_ref[...] = (lnc * _sigmoid(lnc)).astype(BF16)
        xhc_ref[...] = xhc
        rsc_ref[...] = rsc
        z = _gelu(zc_ref[...])
        vn, xhv, rsv = _ln_fwd(z[:, 512:], slg[...], slb[...])
        xhv_ref[...] = xhv
        rsv_ref[...] = rsv
        vnb = vn.astype(BF16)
        for g in range(4):
            wm = _masked_sg_w(w_ref, g)
            for ch in range(tm // 128):
                rs, cs = slice(ch * 128, (ch + 1) * 128), slice(g * 128, (g + 1) * 128)
                sv_ref[rs, cs] = _dot(wm, vnb[rs, cs]) + sb_ref[g]
        d_ref[...] = (z[:, :512] * sv_ref[...]).astype(BF16)

    return _tok_call(
        "odd_post", body, [y, (h, 1024, 1)], [cl_g, cl_b, sl_g, sl_b, sg_w, sgb_bc],
        [_sds((S, 512), BF16), _sds((S, 512), BF16), _sds((S, 512)), _sds((S, 1)),
         _sds((S, 512)), _sds((S, 1)), _sds((S, 512))], tm=tm)


def mm_out_ln(l1, l2, x, w, g, b, dep=None):
    S, D = x.shape

    def body(l1_ref, l2_ref, x_ref, w_ref, g_ref, b_ref, y_ref, xh_ref, rs_ref):
        mix = _dot(l1_ref[...], w_ref[0:512, :]) + _dot(l2_ref[...], w_ref[512:1024, :])
        y, xh, rs = _ln_fwd(ALPHA * x_ref[...] + mix, g_ref[...], b_ref[...])
        y_ref[...] = y
        xh_ref[...] = xh
        rs_ref[...] = rs

    return _tok_call("mm_out_ln", body, [l1, l2, x], [w, g, b],
                     [_sds((S, D)), _sds((S, D)), _sds((S, 1))], dep=dep)


def ffn_up(x1, wg, wu, layer, dep=None):
    S, D = x1.shape
    F = wg.shape[-1]

    def body(x_ref, wg_ref, wu_ref, gate_ref, up_ref, hb_ref, xb_ref):
        xb = x_ref[...].astype(BF16)
        xb_ref[...] = xb
        gate = _dot(xb, wg_ref[...])
        up = _dot(xb, wu_ref[...])
        gate_ref[...] = gate.astype(BF16)
        up_ref[...] = up.astype(BF16)
        hb_ref[...] = (gate * _sigmoid(gate) * up).astype(BF16)

    return _tok_call("ffn_up", body, [x1], [(wg, layer), (wu, layer)],
                     [_sds((S, F), BF16), _sds((S, F), BF16), _sds((S, F), BF16), _sds((S, D), BF16)], dep=dep)


def ffn_down_ln(hb, x1, wd, layer, g, b):
    S, D = x1.shape

    def body(h_ref, x_ref, w_ref, g_ref, b_ref, y_ref, xh_ref, rs_ref):
        f = _dot(h_ref[...], w_ref[...])
        y, xh, rs = _ln_fwd(ALPHA * x_ref[...] + f, g_ref[...], b_ref[...])
        y_ref[...] = y
        xh_ref[...] = xh
        rs_ref[...] = rs

    return _tok_call("ffn_down_ln", body, [hb, x1], [(wd, layer), g, b],
                     [_sds((S, D)), _sds((S, D)), _sds((S, 1))])


def ple_fwd(x2, p, wpg, wpp, layer, bg, target=None, dep=None):
    S, D = x2.shape
    last = target is not None

    def body(*refs):
        if last:
            x_ref, p_ref, t_ref, wg_ref, wp_ref, b_ref, x3_ref, sg_ref, pp_ref, xb_ref, pb_ref, dy_ref, ls_ref = refs
        else:
            x_ref, p_ref, wg_ref, wp_ref, b_ref, x3_ref, sg_ref, pp_ref, xb_ref, pb_ref = refs
        x = x_ref[...]
        xb = x.astype(BF16)
        pb = p_ref[...].astype(BF16)
        xb_ref[...] = xb
        pb_ref[...] = pb
        sg = _sigmoid(_dot(xb, wg_ref[...]) + b_ref[...])
        pp = _dot(pb, wp_ref[...])
        sg_ref[...] = sg.astype(BF16)
        pp_ref[...] = pp.astype(BF16)
        x3 = x + sg * pp
        x3_ref[...] = x3
        if last:
            err = x3 - t_ref[...]
            dy_ref[...] = err * (1.0 / D)
            _acc(ls_ref, _colsum(err * err))

    outs = [_sds((S, D)), _sds((S, D), BF16), _sds((S, D), BF16), _sds((S, D), BF16), _sds((S, p.shape[1]), BF16)]
    tiled = [x2, p] + ([target] if last else [])
    if last:
        outs.append(_sds((S, D)))
    return _tok_call("ple_fwd", body, tiled, [(wpg, layer), (wpp, layer), bg], outs,
                     [_sds((1, D))] if last else [], dep=dep)


def ple_ln_bwd(dx3, sg, pp, x2b, pb, wpg, xh, rs, g, dep=None):
    S, D = dx3.shape

    def body(d_ref, sg_ref, pp_ref, x2b_ref, pb_ref, xh_ref, rs_ref, w_ref, g_ref,
             dr_ref, drb_ref, dbg_ref, dlg_ref, dlb_ref, dwg_ref, dwp_ref, accg_ref, accp_ref):
        fin_g = _sum_steps(accg_ref, dwg_ref)
        fin_p = _sum_steps(accp_ref, dwp_ref)
        d, sg = d_ref[...], sg_ref[...].astype(F32)
        dgp = d * pp_ref[...].astype(F32) * sg * (1.0 - sg)
        dgpb = dgp.astype(BF16)
        accg_ref[...] += _dot_tn(x2b_ref[...], dgpb)
        accp_ref[...] += _dot_tn(pb_ref[...], (d * sg).astype(BF16))
        _acc(dbg_ref, _colsum(dgp))
        dx2 = d + _dot_nt(dgpb, w_ref[...])
        xh = xh_ref[...]
        dr = _ln_bwd(dx2, xh, rs_ref[...], g_ref[...])
        dr_ref[...] = dr
        drb_ref[...] = dr.astype(BF16)
        _acc(dlg_ref, _colsum(dx2 * xh))
        _acc(dlb_ref, _colsum(dx2))
        fin_g()
        fin_p()

    P = pb.shape[1]
    return _tok_call("ple_ln_bwd", body, [dx3, sg, pp, x2b, pb, xh, rs], [wpg, g],
                     [_sds((S, D)), _sds((S, D), BF16)],
                     [_sds((1, D)), _sds((1, D)), _sds((1, D)), _sds((D, D), BF16), _sds((P, D), BF16)],
                     scratch=[pltpu.VMEM((D, D), F32), pltpu.VMEM((P, D), F32)], dep=dep)


def ffn_bwd(dr_b, x1b, gate, up, hb, wg, wu, wd, TH=256):
    S, D = dr_b.shape
    F = gate.shape[1]

    def body(dr_hbm, x_hbm, gate_ref, up_ref, hb_ref, wg_ref, wu_ref, wd_ref,
             dx_ref, dwg_ref, dwu_ref, dwd_ref, dr_v, x_v, sem, dg_s, du_s):
        @pl.when(pl.program_id(0) == 0)
        def _():
            c1 = pltpu.make_async_copy(dr_hbm, dr_v, sem.at[0])
            c2 = pltpu.make_async_copy(x_hbm, x_v, sem.at[1])
            c1.start()
            c2.start()
            c1.wait()
            c2.wait()
            dx_ref[...] = jnp.zeros_like(dx_ref)

        for ch in range(S // CH):
            rows = slice(ch * CH, (ch + 1) * CH)
            dh = _dot_nt(dr_v[rows, :], wd_ref[...])
            g, u = gate_ref[rows, :].astype(F32), up_ref[rows, :].astype(F32)
            s = _sigmoid(g)
            dgb = (dh * u * s * (1.0 + g * (1.0 - s))).astype(BF16)
            dub = (dh * g * s).astype(BF16)
            dg_s[rows, :] = dgb
            du_s[rows, :] = dub
            dx_ref[rows, :] += _dot_nt(dgb, wg_ref[...]) + _dot_nt(dub, wu_ref[...])
        x = x_v[...]
        dwg_ref[...] = _dot_tn(x, dg_s[...]).astype(BF16)
        dwu_ref[...] = _dot_tn(x, du_s[...]).astype(BF16)
        dwd_ref[...] = _dot_tn(hb_ref[...], dr_v[...]).astype(BF16)

    CH = min(512, S)
    col = lambda rows: pl.BlockSpec((rows, TH), lambda j: (0, j))
    row = pl.BlockSpec((TH, D), lambda j: (j, 0))
    return pl.pallas_call(
        body, name="ffn_bwd", grid=(F // TH,),
        in_specs=[ANY, ANY, col(S), col(S), col(S), col(D), col(D), row],
        out_specs=[pl.BlockSpec((S, D), lambda j: (0, 0)), col(D), col(D), row],
        out_shape=[_sds((S, D)), _sds((D, F), BF16), _sds((D, F), BF16), _sds((F, D), BF16)],
        scratch_shapes=[pltpu.VMEM((S, D), BF16), pltpu.VMEM((S, D), BF16), pltpu.SemaphoreType.DMA((2,)),
                        pltpu.VMEM((S, TH), BF16), pltpu.VMEM((S, TH), BF16)],
        compiler_params=pltpu.CompilerParams(dimension_semantics=("arbitrary",), vmem_limit_bytes=60 * 1024 * 1024),
    )(dr_b, x1b, gate, up, hb, wg, wu, wd)


def _sum_steps(acc_ref, out_ref):
    @pl.when(pl.program_id(0) == 0)
    def _():
        acc_ref[...] = jnp.zeros_like(acc_ref)

    def finish():
        @pl.when(pl.program_id(0) == pl.num_programs(0) - 1)
        def _():
            out_ref[...] = acc_ref[...].astype(BF16)
    return finish


def mix_bwd(dxp, dr2, xh, rs, l1, l2, g, w, dep=None):
    S, D = dxp.shape
    K1 = l1.shape[1]

    def body(dxp_ref, dr2_ref, xh_ref, rs_ref, l1_ref, l2_ref, g_ref, w_ref,
             dr_ref, dl_ref, dlg_ref, dlb_ref, dw_ref, acc_ref):
        finish = _sum_steps(acc_ref, dw_ref)
        d, xh = ALPHA * dr2_ref[...] + dxp_ref[...], xh_ref[...]
        dr = _ln_bwd(d, xh, rs_ref[...], g_ref[...])
        drb = dr.astype(BF16)
        dr_ref[...] = dr
        dl_ref[...] = _dot_nt(drb, w_ref[...])
        _acc(dlg_ref, _colsum(d * xh))
        _acc(dlb_ref, _colsum(d))
        acc_ref[0:K1, :] += _dot_tn(l1_ref[...], drb)
        acc_ref[K1:, :] += _dot_tn(l2_ref[...], drb)
        finish()

    return _tok_call("mix_bwd", body, [dxp, dr2, xh, rs, l1, l2], [g, w],
                     [_sds((S, D)), _sds((S, D))], [_sds((1, D)), _sds((1, D)), _sds(w.shape, BF16)],
                     scratch=[pltpu.VMEM(w.shape, F32)], dep=dep)


def dx_in(dr, pieces, w, xb):
    S, D = dr.shape
    offs = [o for _, o in pieces]
    widths = [a.shape[1] for a, _ in pieces]
    npc = len(pieces)

    def body(*refs):
        dr_ref, prefs, xb_ref, w_ref = refs[0], refs[1:1 + npc], refs[1 + npc], refs[2 + npc]
        dx_ref, dw_ref, acc_ref = refs[3 + npc:]
        finish = _sum_steps(acc_ref, dw_ref)
        acc = ALPHA * dr_ref[...]
        xb_t = xb_ref[...]
        for pr, o, n in zip(prefs, offs, widths):
            piece = pr[...]
            acc = acc + _dot_nt(piece, w_ref[:, o:o + n])
            acc_ref[:, o:o + n] += _dot_tn(xb_t, piece)
        dx_ref[...] = acc
        finish()

    return _tok_call("dx_in", body, [dr] + [a for a, _ in pieces] + [xb], [w], [_sds((S, D))],
                     [_sds(w.shape, BF16)], scratch=[pltpu.VMEM(w.shape, F32)])


def odd_post_bwd(dl, h, xhc, rsc, xhv, rsv, sv, cl_g, cl_b, sl_g, sl_b, sg_w, tm=256, dep=None):
    S = dl.shape[0]
    tm = min(tm, S)

    def body(dl_ref, zc_ref, xhc_ref, rsc_ref, xhv_ref, rsv_ref, sv_ref, clg, clb, slg, slb, w_ref,
             dy_ref, dzc_ref, dclg_ref, dclb_ref, dslg_ref, dslb_ref, dwm_ref, dsb_ref, dvn_ref):
        first = pl.program_id(0) == 0
        last = pl.program_id(0) == pl.num_programs(0) - 1
        dc, dd = dl_ref[:, 0:512], dl_ref[:, 512:1024]
        xhc = xhc_ref[...]
        lnc = xhc * clg[...] + clb[...]
        s = _sigmoid(lnc)
        dlnc = dc * s * (1.0 + lnc * (1.0 - s))
        dy_ref[...] = _ln_bwd(dlnc, xhc, rsc_ref[...], clg[...])
        _acc(dclg_ref, _colsum(dlnc * xhc))
        _acc(dclb_ref, _colsum(dlnc))
        zc = zc_ref[...]
        z = _gelu(zc)
        dsv = dd * z[:, :512]
        dsvb = dsv.astype(BF16)
        xhv = xhv_ref[...]
        vnb = (xhv * slg[...] + slb[...]).astype(BF16)

        @pl.when(first)
        def _():
            dwm_ref[...] = jnp.zeros_like(dwm_ref)
            dsb_ref[...] = jnp.zeros_like(dsb_ref)

        for g in range(4):
            wm = _masked_sg_w(w_ref, g)
            for ch in range(tm // 128):
                rs_, cs = slice(ch * 128, (ch + 1) * 128), slice(g * 128, (g + 1) * 128)
                dwm_ref[g] += _dot_nt(dsvb[rs_, cs], vnb[rs_, cs])
                dvn_ref[rs_, cs] = _dot_tn(wm, dsvb[rs_, cs])
                dsb_ref[g] += dsv[rs_, cs]
        dvn = dvn_ref[...]
        dvv = _ln_bwd(dvn, xhv, rsv_ref[...], slg[...])
        _acc(dslg_ref, _colsum(dvn * xhv))
        _acc(dslb_ref, _colsum(dvn))
        gg = _gelu_grad(zc)
        dzc_ref[:, 0:512] = (dd * sv_ref[...] * gg[:, :512]).astype(BF16)
        dzc_ref[:, 512:1024] = (dvv * gg[:, 512:]).astype(BF16)

        @pl.when(last)
        def _():
            row = lax.broadcasted_iota(jnp.int32, (128, 128), 0)
            col = lax.broadcasted_iota(jnp.int32, (128, 128), 1)
            for g in range(4):
                dwm_ref[g] = jnp.where(row >= col, dwm_ref[g], 0.0)
                dsb_ref[g] = jnp.broadcast_to(jnp.sum(dsb_ref[g], axis=1, keepdims=True), (128, 128))

    return _tok_call(
        "odd_post_bwd", body, [dl, (h, 1024, 1), xhc, rsc, xhv, rsv, sv], [cl_g, cl_b, sl_g, sl_b, sg_w],
        [_sds((S, 512)), _sds((S, 1024), BF16)],
        [_sds((1, 512)), _sds((1, 512)), _sds((1, 512)), _sds((1, 512)), _sds((4, 128, 128)), _sds((4, 128, 128))],
        tm=tm, scratch=[pltpu.VMEM((tm, 512), F32)], dep=dep)


def conv_bwd(dy, hc, h, dw, CH=128):
    S = dy.shape[0]

    def body(dy_ref, hc_ref, a_ref, g_ref, dw_ref, da_ref, dg_ref, ddw_ref, padh_ref, padd_ref, dhc_ref):
        padh_ref[0:32, :] = jnp.zeros((32, 128), F32)
        padh_ref[32:32 + S, :] = hc_ref[...]
        padd_ref[0:S, :] = dy_ref[...]
        padd_ref[S:S + 32, :] = jnp.zeros((32, 128), F32)
        taps = [jnp.zeros((1, 128), F32) for _ in range(CONV_TAPS)]
        for ch in range(S // CH):
            b0 = ch * CH
            dyc = padd_ref[b0:b0 + CH, :]
            acc = dw_ref[0:1, :] * padd_ref[b0 + 30:b0 + 30 + CH, :]
            taps[0] = taps[0] + _colsum(dyc * padh_ref[b0 + 2:b0 + 2 + CH, :])
            for k in range(1, CONV_TAPS):
                acc = acc + dw_ref[k:k + 1, :] * padd_ref[b0 + 30 - k:b0 + 30 - k + CH, :]
                taps[k] = taps[k] + _colsum(dyc * padh_ref[b0 + 2 + k:b0 + 2 + k + CH, :])
            dhc_ref[b0:b0 + CH, :] = acc
        for k in range(CONV_TAPS):
            ddw_ref[k:k + 1, :] = taps[k]
        dhc = dhc_ref[...]
        s = _sigmoid(g_ref[...])
        da_ref[...] = (dhc * s).astype(BF16)
        dg_ref[...] = (dhc * a_ref[...] * s * (1.0 - s)).astype(BF16)

    return pl.pallas_call(
        body, name="conv_bwd", grid=(4,),
        in_specs=[pl.BlockSpec((S, 128), lambda c: (0, c)),
                  pl.BlockSpec((S, 128), lambda c: (0, c)),
                  pl.BlockSpec((S, 128), lambda c: (0, c)),
                  pl.BlockSpec((S, 128), lambda c: (0, 4 + c)),
                  pl.BlockSpec((CONV_TAPS, 128), lambda c: (0, c))],
        out_specs=[pl.BlockSpec((S, 128), lambda c: (0, c)), pl.BlockSpec((S, 128), lambda c: (0, c)),
                   pl.BlockSpec((CONV_TAPS, 128), lambda c: (0, c))],
        out_shape=[_sds((S, 512), BF16), _sds((S, 512), BF16), _sds((CONV_TAPS, 512))],
        scratch_shapes=[pltpu.VMEM((S + 32, 128), F32), pltpu.VMEM((S + 32, 128), F32), pltpu.VMEM((S, 128), F32)],
        compiler_params=_cp(("arbitrary",)),
    )(dy, hc, h, h, dw)


def attn_bwd(qkv, dl, tb, T=256, dep=None):
    S = qkv.shape[0]
    T = min(T, S)
    nq = S // T

    def body(q_ref, k_ref, v_ref, do_ref, t_ref, dq_ref, dk_ref, dv_ref,
             dka_ref, dva_ref, dqa_ref, pc_ref, gc_ref, qh_ref, doh_ref):
        i = pl.program_id(0)
        hm0 = lax.broadcasted_iota(jnp.int32, (1, 128), 1) < 64
        r2 = lax.broadcasted_iota(jnp.int32, (2 * T, T), 0)
        c2 = lax.broadcasted_iota(jnp.int32, (2 * T, T), 1)
        causal = c2 < jnp.where(r2 >= T, r2 - T, r2)
        ur = lax.broadcasted_iota(jnp.int32, (T, T), 0)
        uc = lax.broadcasted_iota(jnp.int32, (T, T), 1)
        u_le = (ur <= uc).astype(BF16)
        u_lt = (ur < uc).astype(BF16)

        @pl.when(i == 0)
        def _():
            dka_ref[...] = jnp.zeros_like(dka_ref)
            dva_ref[...] = jnp.zeros_like(dva_ref)

        dqa_ref[...] = jnp.zeros_like(dqa_ref)
        gc_ref[...] = jnp.zeros_like(gc_ref)
        for pp in range(4):
            cs = slice(pp * 128, (pp + 1) * 128)
            qh_ref[pp] = _stack_heads(q_ref[:, cs] * QK_SCALE, hm0)
            doh_ref[pp] = _stack_heads(do_ref[:, cs], hm0)
            for hd in range(2):
                for half in range(T // 128):
                    pc_ref[pp, hd * T:(hd + 1) * T, half * 128:(half + 1) * 128] = t_ref[2 * pp + hd]

        def block(kb, diag):
            ks = pl.multiple_of(kb * T, T)
            cols = [slice(pp * 128, (pp + 1) * 128) for pp in range(4)]
            zs = [_dot_nt(qh_ref[pp], k_ref[pl.ds(ks, T), cols[pp]]) for pp in range(4)]
            dws = [_dot_nt(doh_ref[pp], v_ref[pl.ds(ks, T), cols[pp]]) for pp in range(4)]
            a_s, pres = [], []
            for pp in range(4):
                sp = _softplus(zs[pp])
                a_s.append(zs[pp] - sp)
                if diag:
                    sp = jnp.where(causal, sp, 0.0)
                pres.append(_cumsum_mm(sp, u_le))
            ws, gmats, gsums = [], [], []
            for pp in range(4):
                rem = pc_ref[pp]
                w = jnp.exp(a_s[pp] - rem + pres[pp])
                if diag:
                    w = jnp.where(causal, w, 0.0)
                gmat = dws[pp] * w
                ws.append(w.astype(BF16))
                gmats.append(gmat)
                gsums.append(_cumsum_mm(gmat, u_lt))
                pc_ref[pp] = rem - jnp.broadcast_to(pres[pp][:, T - 1:T], (2 * T, T))
            for pp in range(4):
                cs = cols[pp]
                sig = jnp.exp(a_s[pp])
                gex = gc_ref[pp] + gsums[pp]
                dz = gmats[pp] * (1.0 - sig) - sig * gex
                if diag:
                    dz = jnp.where(causal, dz, 0.0)
                dzb = dz.astype(BF16)
                dqa_ref[:, cs] += _dot(_unstack_k(dzb, T), _stack_heads(k_ref[pl.ds(ks, T), cs], hm0))
                dka_ref[pl.ds(ks, T), cs] += _dot_tn(dzb, qh_ref[pp])
                dva_ref[pl.ds(ks, T), cs] += _dot_tn(ws[pp], doh_ref[pp])
                gc_ref[pp] = jnp.broadcast_to(gex[:, T - 1:T] + gmats[pp][:, T - 1:T], (2 * T, T))

        def step(kb, carry):
            block(kb, False)
            return carry

        lax.fori_loop(0, i, step, 0)
        block(i, True)
        dq_ref[...] = (dqa_ref[...] * QK_SCALE).astype(BF16)

        @pl.when(i == nq - 1)
        def _():
            dk_ref[...] = dka_ref[...].astype(BF16)
            dv_ref[...] = dva_ref[...].astype(BF16)

    deps = [] if dep is None else [dep]
    call_body = body if dep is None else (lambda *refs: body(*refs[:5], *refs[6:]))
    return pl.pallas_call(
        call_body, name="attn_bwd", grid=(nq,),
        in_specs=[pl.BlockSpec((T, 512), lambda i: (i, 0)),
                  pl.BlockSpec((S, 512), lambda i: (0, 1)),
                  pl.BlockSpec((S, 512), lambda i: (0, 2)),
                  pl.BlockSpec((T, 512), lambda i: (i, 0)),
                  pl.BlockSpec((8, T, 128), lambda i: (0, i, 0))] + [ANY] * len(deps),
        out_specs=[pl.BlockSpec((T, 512), lambda i: (i, 0)),
                   pl.BlockSpec((S, 512), lambda i: (0, 0)),
                   pl.BlockSpec((S, 512), lambda i: (0, 0))],
        out_shape=[_sds((S, 512), BF16), _sds((S, 512), BF16), _sds((S, 512), BF16)],
        scratch_shapes=[pltpu.VMEM((S, 512), F32), pltpu.VMEM((S, 512), F32), pltpu.VMEM((T, 512), F32),
                        pltpu.VMEM((4, 2 * T, T), F32), pltpu.VMEM((4, 2 * T, T), F32),
                        pltpu.VMEM((4, 2 * T, 128), BF16), pltpu.VMEM((4, 2 * T, 128), BF16)],
        compiler_params=_cp(("arbitrary",)),
    )(qkv, qkv, qkv, dl, tb, *deps)


def pool_bwd(dl, pooled_b, pool_w, pool_scale, CH=256):
    S = dl.shape[0]
    CH = min(CH, S)

    def body(db_ref, pooled_ref, w_ref, sc_ref, du_ref, dw_ref, dsc_ref, pad_ref, dp_ref):
        pad_ref[S:S + 16, :] = jnp.zeros((16, 128), F32)
        for g, win in enumerate(POOL_WINDOWS):
            cs = slice(g * 128, (g + 1) * 128)
            wq = w_ref[g].astype(BF16)
            dwg = jnp.zeros((128, 128), F32)
            dsc = jnp.zeros((1, 128), F32)
            for ch in range(S // CH):
                rs_ = slice(ch * CH, (ch + 1) * CH)
                db = db_ref[rs_, cs]
                pb = pooled_ref[rs_, cs]
                dsc = dsc + _colsum(db * _dot(pb, wq))
                dmsb = (db * sc_ref[:, cs]).astype(BF16)
                dwg = dwg + _dot_tn(pb, dmsb)
                dpool = _dot_nt(dmsb, wq)
                t = ch * CH + lax.broadcasted_iota(jnp.int32, (CH, 1), 0)
                cnt = jnp.minimum(t + 1, win).astype(F32)
                dp_ref[rs_, :] = dpool
                pad_ref[rs_, :] = dpool / cnt
            dw_ref[g] = dwg
            dsc_ref[:, cs] = dsc
            for ch in range(S // CH):
                base = ch * CH
                acc = pad_ref[base:base + CH, :]
                for sft in range(1, win):
                    acc = acc + pad_ref[base + sft:base + sft + CH, :]
                du_ref[base:base + CH, cs] = (acc - dp_ref[base:base + CH, :]).astype(BF16)

    return pl.pallas_call(
        body, name="pool_bwd", grid=(1,),
        in_specs=[pl.BlockSpec((S, 512), lambda i: (0, 1)),
                  pl.BlockSpec((S, 512), lambda i: (0, 0)),
                  pl.BlockSpec((4, 128, 128), lambda i: (0, 0, 0)),
                  pl.BlockSpec((1, 512), lambda i: (0, 0))],
        out_specs=[pl.BlockSpec((S, 512), lambda i: (0, 0)),
                   pl.BlockSpec((4, 128, 128), lambda i: (0, 0, 0)),
                   pl.BlockSpec((1, 512), lambda i: (0, 0))],
        out_shape=[_sds((S, 512), BF16), _sds((4, 128, 128)), _sds((1, 512))],
        scratch_shapes=[pltpu.VMEM((S + 16, 128), F32), pltpu.VMEM((S, 128), F32)],
        compiler_params=_cp(("arbitrary",)),
    )(dl, pooled_b, pool_w, pool_scale)


def _row(a, i):
    return a[i:i + 1]


MIXER_NAMES = (("even_w_in", "even_w_out"), ("odd_w_in", "odd_w_out"))


def fwd_layer(i, xin, p_i, target, comm):
    s = {}
    W = comm.weights(("mix", i), xin)
    w_in = W[MIXER_NAMES[i][0]]
    if i == 0:
        s["h"], s["xb"], s["qkv"] = mm_in(xin, w_in, nb16=1536)
        comm.poke(("in", i), s["h"])
        s["l1"], s["tb"] = attn_fwd(s["qkv"])
        s["l2"], s["pooled"] = pool_fwd(s["h"], W["pool_w"], W["pool_scale"])
    else:
        s["h"], s["xb"] = mm_in(xin, w_in)
        comm.poke(("in", i), s["h"])
        s["y"], s["hc"] = conv_fwd(s["h"], W["conv_dw"])
        sgb_bc = jnp.broadcast_to(W["sg_b"][:, :, None], (4, 128, 128))
        (s["l1"], s["l2"], s["xhc"], s["rsc"], s["xhv"], s["rsv"], s["sv"]) = odd_post(
            s["y"], s["h"], W["conv_ln_g"], W["conv_ln_b"], W["sg_ln_g"], W["sg_ln_b"], W["sg_w"], sgb_bc)
    tok = comm.poke(("mixed", i), s["l1"])
    W = comm.weights(("out", i), s["l1"])
    x1, s["xh1"], s["rs1"] = mm_out_ln(s["l1"], s["l2"], xin, W[MIXER_NAMES[i][1]], _row(W["ln_mix_g"], i),
                                       _row(W["ln_mix_b"], i), dep=tok)
    W = comm.weights(("ffn", i), x1)
    tok = comm.poke(("up", i), x1)
    s["gate"], s["up"], s["hb"], s["x1b"] = ffn_up(x1, W["ffn_w_gate%d" % i], W["ffn_w_up%d" % i], None, dep=tok)
    W = comm.weights(("down", i), s["hb"])
    x2, s["xh2"], s["rs2"] = ffn_down_ln(s["hb"], x1, W["ffn_w_down%d" % i], None,
                                         _row(W["ln_ffn_g"], i), _row(W["ln_ffn_b"], i))
    tok = comm.poke(("ffn", i), x2)
    outs = ple_fwd(x2, p_i, W["ple_w_gate%d" % i], W["ple_w_proj%d" % i], None, _row(W["ple_b_gate"], i), target,
                   dep=tok)
    s["sg"], s["pp"], s["x2b"], s["pb"] = outs[1:5]
    return outs[0], s, outs[5:]


def bwd_layer(i, dx, s, W, comm, tok=None):
    small = {}
    dr2, dr2_b, small["ple_b_gate"], small["ln_ffn_g"], small["ln_ffn_b"], dwpg, dwpp = ple_ln_bwd(
        dx, s["sg"], s["pp"], s["x2b"], s["pb"], W["ple_w_gate%d" % i], s["xh2"], s["rs2"],
        _row(W["ln_ffn_g"], i), dep=tok)
    dxp, dwg, dwu, dwd = ffn_bwd(dr2_b, s["x1b"], s["gate"], s["up"], s["hb"], W["ffn_w_gate%d" % i],
                                 W["ffn_w_up%d" % i], W["ffn_w_down%d" % i])
    tok = comm.grads({"ple_w_gate%d" % i: dwpg, "ple_w_proj%d" % i: dwpp, "ffn_w_down%d" % i: dwd,
                      "ffn_w_gate%d" % i: dwg, "ffn_w_up%d" % i: dwu})
    iname, oname = MIXER_NAMES[i]
    dr1, dl, small["ln_mix_g"], small["ln_mix_b"], dwout = mix_bwd(
        dxp, dr2, s["xh1"], s["rs1"], s["l1"], s["l2"], _row(W["ln_mix_g"], i), W[oname], dep=tok)
    tok = comm.poke(("bwd", i), dl)
    if i == 1:
        (dy, dzc_b, small["conv_ln_g"], small["conv_ln_b"], small["sg_ln_g"], small["sg_ln_b"],
         small["sg_w"], dsb) = odd_post_bwd(dl, s["h"], s["xhc"], s["rsc"], s["xhv"], s["rsv"], s["sv"],
                                            W["conv_ln_g"], W["conv_ln_b"], W["sg_ln_g"], W["sg_ln_b"], W["sg_w"],
                                            dep=tok)
        small["sg_b"] = dsb[:, :, 0]
        da_b, dg_b, small["conv_dw"] = conv_bwd(dy, s["hc"], s["h"], W["conv_dw"])
        pieces = [(da_b, 0), (dg_b, 512), (dzc_b, 1024)]
    else:
        dq_b, dk_b, dv_b = attn_bwd(s["qkv"], dl, s["tb"], dep=tok)
        du_b, small["pool_w"], small["pool_scale"] = pool_bwd(dl, s["pooled"], W["pool_w"], W["pool_scale"])
        pieces = [(dq_b, 0), (dk_b, 512), (dv_b, 1024), (du_b, 1536)]
    dxin, dwin = dx_in(dr1, pieces, W[iname], s["xb"])
    tok = comm.grads({oname: dwout, iname: dwin})
    return dxin, small, tok


def run_layers(x, p, target, comm):
    saved, xin = [], x
    for i in range(2):
        xin, s, extra = fwd_layer(i, xin, p[i], target if i == 1 else None, comm)
        saved.append(s)
    dx, sq = extra
    W = comm.all_weights()
    per_layer = [None, None]
    tok = None
    for i in (1, 0):
        dx, per_layer[i], tok = bwd_layer(i, dx, saved[i], W, comm, tok)
    small = {}
    for k in ("ln_mix_g", "ln_mix_b", "ln_ffn_g", "ln_ffn_b", "ple_b_gate"):
        small[k] = jnp.concatenate([per_layer[0][k], per_layer[1][k]], axis=0)
    for i in range(2):
        small.update({k: v for k, v in per_layer[i].items() if k not in small})
    return sq, dx, small


def _big_table():
    t = {}
    for nm in ("even", "odd"):
        t[nm + "_w_in"] = ((1024, 2048), 1, 256, 256, nm + "_w_in", 0)
        t[nm + "_w_out"] = ((1024, 1024), 0, 128, 128, nm + "_w_out", 0)
    for l in range(2):
        t["ffn_w_gate%d" % l] = ((1024, 8 * FF_PAD), 1, FF_PAD, FF_SHARD, "ffn_w_gate", l)
        t["ffn_w_up%d" % l] = ((1024, 8 * FF_PAD), 1, FF_PAD, FF_SHARD, "ffn_w_up", l)
        t["ffn_w_down%d" % l] = ((8 * FF_PAD, 1024), 0, FF_PAD, FF_SHARD, "ffn_w_down", l)
        t["ple_w_gate%d" % l] = ((1024, 1024), 0, 128, 128, "ple_w_gate", l)
        t["ple_w_proj%d" % l] = ((256, 1024), 1, 128, 128, "ple_w_proj", l)
    return t


BIG = _big_table()
TRANSPOSED_ARGS = ("ffn_w_gate", "ffn_w_up")
SMALL_SPEC = ((N_DEV, 40, 64), 0, 1, 1)
_UP_GROUP = lambda l: ["ffn_w_gate%d" % l, "ffn_w_up%d" % l]
_DOWN_GROUP = lambda l: ["ffn_w_down%d" % l, "ple_w_gate%d" % l, "ple_w_proj%d" % l]
AG_GROUPS = (["even_w_in"], ["even_w_out"], _UP_GROUP(0), _DOWN_GROUP(0), ["odd_w_in", "odd_w_out", "small"],
             _UP_GROUP(1), _DOWN_GROUP(1))
AG_NEED = {("mix", 0): 0, ("out", 0): 1, ("ffn", 0): 2, ("down", 0): 3, ("mix", 1): 4, ("ffn", 1): 5, ("down", 1): 6}
AG_PASS = {("in", 0): 1, ("mixed", 0): 2, ("up", 0): 3, ("ffn", 0): 4, ("mixed", 1): 5, ("up", 1): 6}
ANY = pl.BlockSpec(memory_space=pl.ANY)
SEM = pl.BlockSpec(memory_space=pltpu.SEMAPHORE)


def _spec(name):
    return SMALL_SPEC if name == "small" else BIG[name]


def _win_shape(spec):
    full, axis, w = spec[:3]
    return tuple(w if d == axis else n for d, n in enumerate(full))


def _window(ref, axis, w, j):
    idx = [slice(None)] * len(ref.shape)
    idx[axis] = pl.ds(j, 1) if w == 1 else pl.ds(pl.multiple_of(j * w, w), w)
    return ref.at[tuple(idx)]


def _mesh_pos():
    return lax.axis_index("x"), lax.axis_index("y"), lax.axis_index("c")


def split_call(name, arrays, starts=(), waits=(), sems_in=(), new=(), after=None):
    n, nn, ns = len(arrays), len(new), len(starts)
    flat_sems = [s for pair in sems_in for s in pair]

    def body(*refs):
        arr = list(refs[:n])
        sin = refs[n:n + len(flat_sems)]
        outs = refs[n + len(flat_sems) + (after is not None):]
        data = arr + list(outs[n:n + nn])
        for p, k, kind, mk in waits:
            d = mk(data, sin[2 * p].at[k], sin[2 * p + 1].at[k])
            d.wait_send() if kind == "send" else d.wait_recv()
        if ns:
            send, recv = outs[n + nn], outs[n + nn + 1]
            for k, mk in enumerate(starts):
                mk(data, send.at[k], recv.at[k]).start()
        outs[-1][...] = jnp.zeros((8, 128), F32)

    sem_out = [pltpu.SemaphoreType.DMA((ns,)), pltpu.SemaphoreType.DMA((ns,))] if ns else []
    res = pl.pallas_call(
        body, name=name,
        in_specs=[ANY] * n + [SEM] * len(flat_sems) + ([ANY] if after is not None else []),
        out_specs=[ANY] * (n + nn) + [SEM] * len(sem_out) + [pl.BlockSpec(memory_space=pltpu.VMEM)],
        out_shape=[_sds(a.shape, a.dtype) for a in arrays] + list(new) + sem_out + [_sds((8, 128), F32)],
        input_output_aliases={a: a for a in range(n)},
        compiler_params=pltpu.CompilerParams(has_side_effects=pltpu.SideEffectType.DATAFLOW_SIDE_EFFECTING),
    )(*arrays, *flat_sems, *([after] if after is not None else []))
    return list(res[:n + nn]), (tuple(res[n + nn:n + nn + 2]) if ns else None), res[-1]


def _remote(src, dst, send_sem, recv_sem, dev):
    return pltpu.make_async_remote_copy(src_ref=src, dst_ref=dst, send_sem=send_sem, recv_sem=recv_sem,
                                        device_id=dev, device_id_type=MESH_T)


class Gatherer:
    def __init__(self, groups, arrays, specs, prefix):
        self.groups, self.specs, self.prefix = groups, specs, prefix
        self.names = [nm for g in groups for nm in g]
        self.arr = dict(zip(self.names, arrays))
        self.fwd_sems = {}
        self.forwarded = set()

    @staticmethod
    def _mk_first(ai, spec, k):
        def mk(refs, ss, rs):
            x, y, c = _mesh_pos()
            dev = [(x, y, 1 - c), (1 - x, y, c), (x, 1 - y, c), (1 - x, 1 - y, c)][k]
            win = _window(refs[ai], spec[1], spec[2], 4 * x + 2 * y + c)
            return _remote(win, win, ss, rs, dev)
        return mk

    @staticmethod
    def _mk_fwd(ai, spec, j):
        def mk(refs, ss, rs):
            x, y, c = _mesh_pos()
            px, py = [(1 - x, y), (x, 1 - y), (1 - x, 1 - y)][j]
            win = _window(refs[ai], spec[1], spec[2], 4 * px + 2 * py + c)
            return _remote(win, win, ss, rs, (x, y, 1 - c))
        return mk

    def start(self, after=None):
        starts = [self._mk_first(ai, self.specs[nm], k) for ai, nm in enumerate(self.names) for k in range(4)]
        arrs, self.first_sems, tok = split_call(self.prefix + "_start", [self.arr[nm] for nm in self.names],
                                                starts=starts, after=after)
        self.arr = dict(zip(self.names, arrs))
        return tok

    def forward(self, g, after=None):
        if g in self.forwarded:
            return None
        self.forwarded.add(g)
        names = self.groups[g]
        waits = [(0, 4 * self.names.index(nm) + 1 + j, "recv", self._mk_fwd(ai, self.specs[nm], j))
                 for ai, nm in enumerate(names) for j in range(3)]
        starts = [self._mk_fwd(ai, self.specs[nm], j) for ai, nm in enumerate(names) for j in range(3)]
        arrs, self.fwd_sems[g], tok = split_call(
            "%s_forward%d" % (self.prefix, g), [self.arr[nm] for nm in names], starts=starts, waits=waits,
            sems_in=[self.first_sems], after=after)
        self.arr.update(zip(names, arrs))
        return tok

    def finish(self, g, after=None):
        self.forward(g, after)
        names = self.groups[g]
        waits = []
        for ai, nm in enumerate(names):
            base = 4 * self.names.index(nm)
            waits.append((0, base, "recv", self._mk_first(ai, self.specs[nm], 0)))
            waits += [(1, 3 * ai + j, "recv", self._mk_fwd(ai, self.specs[nm], j)) for j in range(3)]
            waits += [(0, base + k, "send", self._mk_first(ai, self.specs[nm], k)) for k in range(4)]
            waits += [(1, 3 * ai + j, "send", self._mk_fwd(ai, self.specs[nm], j)) for j in range(3)]
        arrs, _, _ = split_call(
            "%s_finish%d" % (self.prefix, g), [self.arr[nm] for nm in names], waits=waits,
            sems_in=[self.first_sems, self.fwd_sems[g]], after=after)
        self.arr.update(zip(names, arrs))
        return {nm: self.arr[nm] for nm in names}


class Reducer:
    def __init__(self, cq_arr, adam):
        self.cq_arr, self.adam = cq_arr, adam
        self.groups = []
        self.n = 0
        self.last = None

    @staticmethod
    def _mk1(gi, li, spec, q):
        def mk(refs, ss, rs):
            x, y, c = _mesh_pos()
            return _remote(_window(refs[gi], spec[1], spec[2], 2 * q + (1 - c)), refs[li].at[q], ss, rs, (x, y, 1 - c))
        return mk

    @staticmethod
    def _mk2(si, li, d):
        def mk(refs, ss, rs):
            x, y, c = _mesh_pos()
            qd = lax.rem(2 * x + y + d, 4)
            return _remote(refs[si].at[d - 1], refs[li].at[3 - d], ss, rs, (lax.div(qd, 2), lax.rem(qd, 2), c))
        return mk

    def add(self, grads, after=None):
        names = list(grads)
        m = len(names)
        starts = [self._mk1(ai, m + ai, BIG[nm], q) for ai, nm in enumerate(names) for q in range(4)]
        new = [_sds((4,) + _win_shape(BIG[nm]), BF16) for nm in names]
        res, sems, tok = split_call("rs1_start%d" % self.n, [grads[nm] for nm in names], starts=starts, new=new,
                                    after=after)
        self.groups.append(dict(names=names, starts=starts, buf=res, sems=sems, stage=1, idx=self.n))
        self.n += 1
        return tok

    def step(self, after):
        tok = None
        for grp in self.groups:
            names, m = grp["names"], len(grp["names"])
            if grp["stage"] == 1:
                waits = [(0, k, kind, mk) for k, mk in enumerate(grp["starts"]) for kind in ("send", "recv")]
                res, _, _ = split_call("rs1_wait%d" % grp["idx"], grp["buf"], waits=waits, sems_in=[grp["sems"]], after=after)
                full, land1 = res[:m], res[m:]
                s1b = list(add_pairs(full, land1, [BIG[nm] for nm in names], self.cq_arr))
                starts = [self._mk2(ai, m + ai, d) for ai in range(m) for d in (1, 2, 3)]
                new = [_sds(a.shape, BF16) for a in s1b]
                res, sems, tok = split_call("rs2_start%d" % grp["idx"], s1b, starts=starts, new=new, after=tok)
                grp.update(stage=2, g=full, land1=land1, starts=starts, buf=res, sems=sems)
        return tok

    def finish_oldest(self):
        for grp in self.groups:
            if grp["stage"] == 2:
                names, m = grp["names"], len(grp["names"])
                waits = [(0, k, kind, mk) for k, mk in enumerate(grp["starts"]) for kind in ("send", "recv")]
                res, _, _ = split_call("rs2_wait%d" % grp["idx"], grp["buf"], waits=waits, sems_in=[grp["sems"]],
                                       after=self.last)
                for nm, g, l1, l2 in zip(names, grp["g"], grp["land1"], res[m:]):
                    self.last = self.adam(nm, g, l1, l2, self.last)
                grp["stage"] = 3
                return True
        return False


def pack_weights(args, arg_names, small_blk, names, j_arr):
    n_in = len(args)

    def body(j_ref, *refs):
        for o, nm in enumerate(names):
            dst = refs[n_in + 1 + o]
            if nm == "small":
                dst[...] = refs[n_in][...]
                continue
            _, axis, w, valid, arg, layer = BIG[nm]
            if arg in TRANSPOSED_ARGS:
                s = refs[arg_names.index(arg)][layer]
                s = jnp.concatenate([s, jnp.zeros((w - valid, s.shape[1]), F32)], axis=0)
                dst[...] = s.T.astype(BF16)
                continue
            src = refs[arg_names.index(arg)][layer].astype(BF16)
            if valid == w:
                dst[...] = src
            else:
                dst[...] = jnp.zeros(dst.shape, BF16)
                if axis == 1:
                    dst[:, 0:valid] = src
                else:
                    dst[0:valid, :] = src

    def ispec(a):
        return pl.BlockSpec(a.shape, lambda i, j_ref: (0, 0, 0))

    def ospec(spec):
        axis, nd = spec[1], len(spec[0])
        return pl.BlockSpec(_win_shape(spec),
                            lambda i, j_ref, axis=axis, nd=nd: tuple(j_ref[0] if d == axis else 0 for d in range(nd)))

    specs = [_spec(nm) for nm in names]
    return pl.pallas_call(
        body, name="pack_weights",
        grid_spec=pltpu.PrefetchScalarGridSpec(
            num_scalar_prefetch=1, grid=(1,),
            in_specs=[ispec(a) for a in list(args) + [small_blk]], out_specs=[ospec(s) for s in specs]),
        out_shape=[_sds(s[0], F32 if nm == "small" else BF16) for nm, s in zip(names, specs)],
        compiler_params=_cp(("arbitrary",)),
    )(j_arr, *args, small_blk)


def add_pairs(fulls, lands, specs, cq_arr):
    def chip(d, cq):
        return lax.rem(cq[1] + d + 1, 4)

    in_specs, args = [], []
    for full, land, spec in zip(fulls, lands, specs):
        axis, w = spec[1], spec[2]
        R, C = full.shape
        for d in range(3):
            if axis == 1:
                in_specs.append(pl.BlockSpec((R, w), lambda i, cq, d=d: (0, 2 * chip(d, cq) + cq[0])))
                in_specs.append(pl.BlockSpec((None, R, w), lambda i, cq, d=d: (chip(d, cq), 0, 0)))
            else:
                in_specs.append(pl.BlockSpec((w, C), lambda i, cq, d=d: (2 * chip(d, cq) + cq[0], 0)))
                in_specs.append(pl.BlockSpec((None, w, C), lambda i, cq, d=d: (chip(d, cq), 0, 0)))
            args += [full, land]
    out_shape = [_sds((3,) + land.shape[1:], BF16) for land in lands]
    n = len(fulls)

    def body(cq_ref, *refs):
        for a in range(n):
            for d in range(3):
                own, got = refs[6 * a + 2 * d], refs[6 * a + 2 * d + 1]
                refs[6 * n + a][d] = (own[...].astype(F32) + got[...].astype(F32)).astype(BF16)

    return pl.pallas_call(
        body, name="add_pairs",
        grid_spec=pltpu.PrefetchScalarGridSpec(
            num_scalar_prefetch=1, grid=(1,), in_specs=in_specs,
            out_specs=[pl.BlockSpec(o.shape, lambda i, cq: (0, 0, 0)) for o in out_shape]),
        out_shape=out_shape,
        compiler_params=_cp(("arbitrary",)),
    )(cq_arr, *args)


def _adamw(w, g, m, v):
    m = ADAM_B1 * m + (1.0 - ADAM_B1) * g
    v = ADAM_B2 * v + (1.0 - ADAM_B2) * (g * g)
    m_hat = m / (1.0 - ADAM_B1 ** ADAM_STEP)
    v_hat = v / (1.0 - ADAM_B2 ** ADAM_STEP)
    delta = -ADAM_LR * (m_hat / (jnp.sqrt(v_hat) + ADAM_EPS) + ADAM_WD * w)
    return delta, m, v


def reduce_adamw(full, land1, land, w, m, v, spec, cq_arr, prev=None, dep=None):
    axis, win, valid, layer = spec[1], spec[2], spec[3], spec[5]
    L, R, C = w.shape
    transposed = spec[4] in TRANSPOSED_ARGS
    if transposed:
        grid = (1,)
        fspec = pl.BlockSpec((C, win), lambda i, cq: (0, 2 * cq[1] + cq[0]))
        wspec = pl.BlockSpec((None, C, win), lambda i, cq: (cq[1], 0, 0))
        lspec = pl.BlockSpec((3, C, win), lambda i, cq: (0, 0, 0))
        sspec = pl.BlockSpec((None, R, C), lambda i, cq: (layer, 0, 0))
    elif axis == 1:
        tr = min(1024, R)
        grid = (R // tr,)
        fspec = pl.BlockSpec((tr, win), lambda i, cq: (i, 2 * cq[1] + cq[0]))
        wspec = pl.BlockSpec((None, tr, win), lambda i, cq: (cq[1], i, 0))
        lspec = pl.BlockSpec((3, tr, win), lambda i, cq: (0, i, 0))
        sspec = pl.BlockSpec((None, tr, C), lambda i, cq: (layer, i, 0))
    else:
        grid = (1,)
        fspec = pl.BlockSpec((win, full.shape[1]), lambda i, cq: (2 * cq[1] + cq[0], 0))
        wspec = pl.BlockSpec((None, win, C), lambda i, cq: (cq[1], 0, 0))
        lspec = pl.BlockSpec((3, win, C), lambda i, cq: (0, 0, 0))
        sspec = pl.BlockSpec((None, R, C), lambda i, cq: (layer, 0, 0))

    def body(cq_ref, full_ref, own_ref, land_ref, w_ref, m_ref, v_ref, *rest):
        g_ref, d_ref, nm_ref, nv_ref = rest[-4:]
        if transposed:
            rd = lambda r, *lead: r[lead] if lead else r[...]
        elif axis == 1:
            rd = lambda r, *lead: r[(*lead, slice(None), slice(0, valid))]
        else:
            rd = lambda r, *lead: r[(*lead, slice(0, valid), slice(None))]
        g = rd(full_ref).astype(F32) + rd(own_ref).astype(F32)
        for k in range(3):
            g = g + rd(land_ref, k).astype(F32)
        if transposed:
            g = g.T[0:valid, :]
        g_ref[...] = g
        d, nm, nv = _adamw(w_ref[...], g, m_ref[...], v_ref[...])
        d_ref[...] = d
        nm_ref[...] = nm
        nv_ref[...] = nv

    extra = (list(prev) if prev is not None else []) + ([dep] if dep is not None else [])
    return pl.pallas_call(
        body, name="reduce_adamw",
        grid_spec=pltpu.PrefetchScalarGridSpec(
            num_scalar_prefetch=1, grid=grid,
            in_specs=[fspec, wspec, lspec, sspec, sspec, sspec] + [ANY] * len(extra), out_specs=[sspec] * 4),
        out_shape=[_sds(w.shape)] * 4,
        input_output_aliases={7 + k: k for k in range(4 if prev is not None else 0)},
        compiler_params=_cp(("arbitrary",)),
    )(cq_arr, full, land1, land, w, m, v, *extra)


def place_slot(packed, j_arr):
    R = packed.shape[0]

    def body(j_ref, src, dst):
        dst[...] = src[...]

    return pl.pallas_call(
        body, name="place_slot",
        grid_spec=pltpu.PrefetchScalarGridSpec(
            num_scalar_prefetch=1, grid=(1,),
            in_specs=[pl.BlockSpec((R, 128), lambda i, j: (0, 0))],
            out_specs=[pl.BlockSpec((None, R, 128), lambda i, j: (j[0], 0, 0))]),
        out_shape=[_sds((N_DEV, R, 128))], compiler_params=_cp(("arbitrary",)),
    )(j_arr, packed)[0]


def sum_slots(gathered):
    def body(g_ref, o_ref):
        g = g_ref[0]
        for dev in range(1, N_DEV):
            g = g + g_ref[dev]
        o_ref[...] = g

    return pl.pallas_call(body, name="sum_slots", out_shape=_sds(gathered.shape[1:]), compiler_params=_cp())(gathered)


def small_adamw(gs, wmv):
    k = len(gs)

    def body(*refs):
        for a in range(k):
            g, w, m, v = refs[4 * a:4 * a + 4]
            d, nm, nv = _adamw(w[...], g[...], m[...], v[...])
            refs[4 * k + 3 * a][...] = d
            refs[4 * k + 3 * a + 1][...] = nm
            refs[4 * k + 3 * a + 2][...] = nv

    args = [t for g, tup in zip(gs, wmv) for t in (g,) + tuple(tup)]
    out_shape = [_sds(g.shape) for g in gs for _ in range(3)]
    return pl.pallas_call(body, name="small_adamw", out_shape=out_shape, compiler_params=_cp())(*args)


WEIGHT_NAMES = ("even_w_in", "even_w_out", "pool_w", "pool_scale", "odd_w_in", "odd_w_out", "conv_dw", "conv_ln_g",
                "conv_ln_b", "sg_ln_g", "sg_ln_b", "sg_w", "sg_b", "ln_mix_g", "ln_mix_b", "ffn_w_gate", "ffn_w_up",
                "ffn_w_down", "ln_ffn_g", "ln_ffn_b", "ple_w_proj", "ple_w_gate", "ple_b_gate")
PACK_ARGS = ("even_w_in", "even_w_out", "odd_w_in", "odd_w_out", "ffn_w_gate", "ffn_w_up", "ffn_w_down",
             "ple_w_gate", "ple_w_proj")
REPLICATED = ("pool_w", "pool_scale", "sg_w", "sg_b", "ln_mix_g", "ln_mix_b", "ln_ffn_g", "ln_ffn_b", "ple_b_gate")
SHARDED_SMALL = ("conv_dw", "conv_ln_g", "conv_ln_b", "sg_ln_g", "sg_ln_b")
NATURAL = {"pool_w": (4, 128, 128), "pool_scale": (1, 512), "sg_w": (4, 128, 128), "sg_b": (4, 128),
           "ln_mix_g": (2, 1024), "ln_mix_b": (2, 1024), "ln_ffn_g": (2, 1024), "ln_ffn_b": (2, 1024),
           "ple_b_gate": (2, 1024)}


def kernel(x, p, even_w_in, even_w_out, pool_w, pool_scale, odd_w_in, odd_w_out, conv_dw, conv_ln_g, conv_ln_b, sg_ln_g, sg_ln_b, sg_w, sg_b, ln_mix_g, ln_mix_b, ffn_w_gate, ffn_w_up, ffn_w_down, ln_ffn_g, ln_ffn_b, ple_w_proj, ple_w_gate, ple_b_gate, loss_target, m_even_w_in, m_even_w_out, m_pool_w, m_pool_scale, m_odd_w_in, m_odd_w_out, m_conv_dw, m_conv_ln_g, m_conv_ln_b, m_sg_ln_g, m_sg_ln_b, m_sg_w, m_sg_b, m_ln_mix_g, m_ln_mix_b, m_ffn_w_gate, m_ffn_w_up, m_ffn_w_down, m_ln_ffn_g, m_ln_ffn_b, m_ple_w_proj, m_ple_w_gate, m_ple_b_gate, v_even_w_in, v_even_w_out, v_pool_w, v_pool_scale, v_odd_w_in, v_odd_w_out, v_conv_dw, v_conv_ln_g, v_conv_ln_b, v_sg_ln_g, v_sg_ln_b, v_sg_w, v_sg_b, v_ln_mix_g, v_ln_mix_b, v_ffn_w_gate, v_ffn_w_up, v_ffn_w_down, v_ln_ffn_g, v_ln_ffn_b, v_ple_w_proj, v_ple_w_gate, v_ple_b_gate):
    A = dict(locals())
    for arg in TRANSPOSED_ARGS:
        for pre in ("", "m_", "v_"):
            A[pre + arg] = jnp.swapaxes(A[pre + arg], 1, 2)
    mx, my, mc = _mesh_pos()
    j = 4 * mx + 2 * my + mc
    j_arr = j.astype(jnp.int32).reshape(1)
    cq_arr = jnp.stack([mc, 2 * mx + my]).astype(jnp.int32)
    res = {}

    def adam(nm, full, land1, land2, dep):
        arg = BIG[nm][4]
        res[arg] = reduce_adamw(full, land1, land2, A[arg], A["m_" + arg], A["v_" + arg], BIG[nm], cq_arr,
                                res.get(arg), dep)
        return res[arg][0]

    class Comm:
        def __init__(self):
            names = [nm for g in AG_GROUPS for nm in g]
            small_blk = jnp.concatenate([conv_dw[0], conv_ln_g, conv_ln_b, sg_ln_g, sg_ln_b, jnp.zeros((5, 64), F32)], axis=0)
            mine = pack_weights([A[k] for k in PACK_ARGS], PACK_ARGS, small_blk[None], names, j_arr)
            self.gat = Gatherer(AG_GROUPS, mine, {nm: _spec(nm) for nm in names}, "ag")
            self.gat.start()
            self.red = Reducer(cq_arr, adam)
            self.W = {k: A[k].reshape(NATURAL[k]) for k in REPLICATED}

        def weights(self, stage, after):
            if stage in AG_NEED:
                got = self.gat.finish(AG_NEED[stage], after)
                if "small" in got:
                    sm = got.pop("small").transpose(1, 0, 2).reshape(40, 512)
                    got.update(conv_dw=sm[0:31], conv_ln_g=sm[31:32], conv_ln_b=sm[32:33], sg_ln_g=sm[33:34],
                               sg_ln_b=sm[34:35])
                self.W.update(got)
            return self.W

        def all_weights(self):
            return self.W

        def poke(self, tag, after):
            if tag in AG_PASS:
                return self.gat.forward(AG_PASS[tag], after)
            if tag[0] == "bwd":
                return self.red.step(after)
            return None

        def grads(self, grads):
            tok = self.red.step(next(iter(grads.values())))
            return self.red.add(grads, after=tok)

    comm = Comm()
    sq, dx, small = run_layers(x[0], p[:, 0], loss_target[0], comm)
    loss = lax.psum(0.5 * jnp.sum(sq) / x.shape[-1], ("x", "y", "c"))
    red = comm.red
    tok = red.step(dx)

    names = REPLICATED + SHARDED_SMALL
    flat = jnp.concatenate([small[k].reshape(-1) for k in names])
    rows = -(-flat.shape[0] // 1024) * 8
    packed = jnp.pad(flat, (0, rows * 128 - flat.shape[0])).reshape(rows, 128)
    sg = Gatherer((["g"],), [place_slot(packed, j_arr)], {"g": ((N_DEV, rows, 128), 0, 1, 1)}, "sg")
    red.last = sg.start(after=tok)
    red.finish_oldest()
    red.finish_oldest()
    sg.forward(0, after=red.last)
    red.finish_oldest()
    gsum_flat = sum_slots(sg.finish(0, after=red.last)["g"]).reshape(-1)
    gs, off = [], 0
    for k in names:
        n = math.prod(small[k].shape)
        g = gsum_flat[off:off + n].reshape(small[k].shape)
        off += n
        if k in SHARDED_SMALL:
            g = lax.dynamic_slice_in_dim(g, j * 64, 64, axis=1)
        gs.append(g.reshape(A[k].shape))
    outs = small_adamw(gs, [(A[k], A["m_" + k], A["v_" + k]) for k in names])
    for a, k in enumerate(names):
        res[k] = (gs[a],) + tuple(outs[3 * a:3 * a + 3])
    red.last = outs[0]
    while red.finish_oldest():
        pass

    for arg in TRANSPOSED_ARGS:
        res[arg] = [jnp.swapaxes(t, 1, 2) for t in res[arg]]
    out = [loss, dx[None]]
    for part in range(4):
        out += [res[k][part] for k in WEIGHT_NAMES]
    return tuple(out)
```

```python
import functools
import math

import jax
import jax.numpy as jnp
from jax import lax
from jax.experimental import pallas as pl
from jax.experimental.pallas import tpu as pltpu

F32, BF16 = jnp.float32, jnp.bfloat16
ALPHA = 4.0 ** 0.25
LN_EPS = 1e-5
QK_SCALE = 0.125
POOL_WINDOWS = (2, 4, 8, 16)
CONV_TAPS = 31
N_DEV = 8
FF_SHARD, FF_PAD = 352, 384
ADAM_LR, ADAM_B1, ADAM_B2, ADAM_EPS, ADAM_WD, ADAM_STEP = 0.001, 0.9, 0.999, 1e-08, 0.01, 10
VMEM_LIMIT = 56 * 1024 * 1024
MESH_T = pl.DeviceIdType.MESH


def _cp(sem=None):
    return pltpu.CompilerParams(dimension_semantics=sem, vmem_limit_bytes=VMEM_LIMIT)


def _dot(a, b):
    return jnp.dot(a, b, preferred_element_type=F32)


def _dot_nt(a, b):
    return lax.dot_general(a, b, (((1,), (1,)), ((), ())), preferred_element_type=F32)


def _dot_tn(a, b):
    return lax.dot_general(a, b, (((0,), (0,)), ((), ())), preferred_element_type=F32)


def _sigmoid(x):
    return 1.0 / (1.0 + jnp.exp(-x))


def _softplus(z):
    return jnp.maximum(z, 0.0) + jnp.log(1.0 + jnp.exp(-jnp.abs(z)))


_GELU_C = math.sqrt(2.0 / math.pi)


def _gelu(x):
    return 0.5 * x * (1.0 + jnp.tanh(_GELU_C * (x + 0.044715 * x * x * x)))


def _gelu_grad(x):
    t = jnp.tanh(_GELU_C * (x + 0.044715 * x * x * x))
    return 0.5 * (1.0 + t) + 0.5 * x * (1.0 - t * t) * _GELU_C * (1.0 + 3.0 * 0.044715 * x * x)


def _ln_fwd(r, g, b):
    mu = jnp.mean(r, axis=-1, keepdims=True)
    xc = r - mu
    var = jnp.mean(xc * xc, axis=-1, keepdims=True)
    rstd = lax.rsqrt(var + LN_EPS)
    xh = xc * rstd
    return xh * g + b, xh, rstd


def _ln_bwd(dy, xh, rstd, g):
    dxh = dy * g
    m1 = jnp.mean(dxh, axis=-1, keepdims=True)
    m2 = jnp.mean(dxh * xh, axis=-1, keepdims=True)
    return rstd * (dxh - m1 - xh * m2)


def _split2(x):
    hi = x.astype(BF16)
    lo = (x - hi.astype(F32)).astype(BF16)
    return hi, lo


def _colsum(x):
    return jnp.sum(x, axis=0, keepdims=True)


def _tok_call(name, body, tiled, full, out_tiled, out_acc=(), tm=256, scratch=(), dep=None):
    def arr(t):
        return t[0] if isinstance(t, tuple) else t
    full = [t[0] if isinstance(t, tuple) and t[1] is None else t for t in full]
    S = arr(tiled[0]).shape[0]
    tm = min(tm, S)
    n_in = len(tiled) + len(full)
    deps = [] if dep is None else [dep]
    if deps:
        inner = body
        body = lambda *refs: inner(*refs[:n_in], *refs[n_in + 1:])

    def tspec(t):
        if isinstance(t, tuple):
            _, w, cb = t
            return pl.BlockSpec((tm, w), lambda i, cb=cb: (i, cb))
        return pl.BlockSpec((tm, t.shape[1]), lambda i: (i, 0))

    def fspec(t):
        if isinstance(t, tuple):
            a, l = t
            nd = a.ndim - 1
            return pl.BlockSpec((None,) + a.shape[1:], lambda i, l=l, nd=nd: (l,) + (0,) * nd)
        nd = t.ndim
        return pl.BlockSpec(t.shape, lambda i, nd=nd: (0,) * nd)

    def ospec(o):
        return pl.BlockSpec((tm, o.shape[1]), lambda i: (i, 0))

    def aspec(o):
        nd = len(o.shape)
        return pl.BlockSpec(o.shape, lambda i, nd=nd: (0,) * nd)

    outs = pl.pallas_call(
        body, name=name, grid=(S // tm,),
        in_specs=[tspec(t) for t in tiled] + [fspec(t) for t in full] + [ANY] * len(deps),
        out_specs=[ospec(o) for o in out_tiled] + [aspec(o) for o in out_acc],
        out_shape=list(out_tiled) + list(out_acc),
        scratch_shapes=list(scratch),
        compiler_params=_cp(("arbitrary",)),
    )(*[arr(t) for t in tiled], *[arr(t) for t in full], *deps)
    return outs


def _sds(shape, dtype=F32):
    return jax.ShapeDtypeStruct(tuple(shape), dtype)


def _acc(ref, val):
    @pl.when(pl.program_id(0) == 0)
    def _():
        ref[...] = val

    @pl.when(pl.program_id(0) != 0)
    def _():
        ref[...] += val


def mm_in(x, w, nb16=0):
    S, N = x.shape[0], w.shape[1]

    def body(x_ref, w_ref, h_ref, xb_ref, *hb_ref):
        xb = x_ref[...].astype(BF16)
        xb_ref[...] = xb
        h = _dot(xb, w_ref[...])
        h_ref[...] = h
        if nb16:
            hb_ref[0][...] = h[:, 0:nb16].astype(BF16)

    outs = [_sds((S, N)), _sds((S, x.shape[1]), BF16)] + ([_sds((S, nb16), BF16)] if nb16 else [])
    return _tok_call("mm_in", body, [x], [w], outs, tm=512)


def _stack_heads(x, hm0, dtype=BF16):
    return jnp.concatenate([jnp.where(hm0, x, 0), jnp.where(hm0, 0, x)], axis=0).astype(dtype)


def _unstack_k(x, T):
    return jnp.concatenate([x[0:T], x[T:2 * T]], axis=1)


def _cumsum_mm(x, u):
    n = x.shape[0]
    hi, lo = _split2(x)
    r = _dot(jnp.concatenate([hi, lo], axis=0), u)
    return r[0:n] + r[n:2 * n]


def attn_fwd(qkv, T=256):
    S = qkv.shape[0]
    T = min(T, S)
    nq = S // T

    def body(q_ref, k_ref, v_ref, o_ref, t_ref, acc_ref, c_ref, qh_ref):
        i = pl.program_id(0)
        hm0 = lax.broadcasted_iota(jnp.int32, (1, 128), 1) < 64
        r2 = lax.broadcasted_iota(jnp.int32, (2 * T, T), 0)
        c2 = lax.broadcasted_iota(jnp.int32, (2 * T, T), 1)
        causal = c2 < jnp.where(r2 >= T, r2 - T, r2)
        ur = lax.broadcasted_iota(jnp.int32, (T, T), 0)
        uc = lax.broadcasted_iota(jnp.int32, (T, T), 1)
        u_incl = (ur >= uc).astype(BF16)
        acc_ref[...] = jnp.zeros_like(acc_ref)
        c_ref[...] = jnp.zeros_like(c_ref)
        for pp in range(4):
            qh_ref[pp] = _stack_heads(q_ref[:, pp * 128:(pp + 1) * 128] * QK_SCALE, hm0)

        def block(kb, diag):
            ks = pl.multiple_of(kb * T, T)
            cols = [slice(pp * 128, (pp + 1) * 128) for pp in range(4)]
            zs = [_dot_nt(qh_ref[pp], k_ref[pl.ds(ks, T), cols[pp]]) for pp in range(4)]
            incls = []
            for pp in range(4):
                sp = _softplus(zs[pp])
                if diag:
                    sp = jnp.where(causal, sp, 0.0)
                incls.append(_cumsum_mm(sp, u_incl))
            for pp in range(4):
                c = c_ref[pp]
                w = jnp.exp(zs[pp] - incls[pp] - c)
                if diag:
                    w = jnp.where(causal, w, 0.0)
                acc_ref[:, cols[pp]] += _dot(_unstack_k(w.astype(BF16), T),
                                             _stack_heads(v_ref[pl.ds(ks, T), cols[pp]], hm0))
                c_ref[pp] = c + jnp.broadcast_to(incls[pp][:, 0:1], (2 * T, T))

        block(i, True)

        def step(jj, carry):
            block(i - 1 - jj, False)
            return carry

        lax.fori_loop(0, i, step, 0)
        o_ref[...] = acc_ref[...].astype(BF16)
        for pp in range(4):
            for hd in range(2):
                t_ref[2 * pp + hd] = c_ref[pp, hd * T:(hd + 1) * T, 0:128]

    return pl.pallas_call(
        body, name="attn_fwd", grid=(nq,),
        in_specs=[pl.BlockSpec((T, 512), lambda i: (i, 0)),
                  pl.BlockSpec((S, 512), lambda i: (0, 1)),
                  pl.BlockSpec((S, 512), lambda i: (0, 2))],
        out_specs=[pl.BlockSpec((T, 512), lambda i: (i, 0)),
                   pl.BlockSpec((8, T, 128), lambda i: (0, i, 0))],
        out_shape=[_sds((S, 512), BF16), _sds((8, S, 128))],
        scratch_shapes=[pltpu.VMEM((T, 512), F32), pltpu.VMEM((4, 2 * T, T), F32), pltpu.VMEM((4, 2 * T, 128), BF16)],
        compiler_params=_cp(("arbitrary",)),
    )(qkv, qkv, qkv)


def pool_fwd(h, pool_w, pool_scale, CH=256):
    S = h.shape[0]
    CH = min(CH, S)

    def body(u_ref, w_ref, sc_ref, b_ref, pooled_ref, pad_ref):
        pad_ref[0:16, :] = jnp.zeros((16, 512), F32)
        pad_ref[16:16 + S, :] = u_ref[...]
        for g, win in enumerate(POOL_WINDOWS):
            cs = slice(g * 128, (g + 1) * 128)
            wq = w_ref[g].astype(BF16)
            for ch in range(S // CH):
                base = ch * CH
                acc = pad_ref[16 + base:16 + base + CH, cs]
                for sft in range(1, win):
                    acc = acc + pad_ref[16 + base - sft:16 + base - sft + CH, cs]
                t = base + lax.broadcasted_iota(jnp.int32, (CH, 1), 0)
                cnt = jnp.minimum(t + 1, win).astype(F32)
                pooled = (acc / cnt - pad_ref[16 + base:16 + base + CH, cs]).astype(BF16)
                pooled_ref[base:base + CH, cs] = pooled
                b_ref[base:base + CH, cs] = (_dot(pooled, wq) * sc_ref[:, cs]).astype(BF16)

    return pl.pallas_call(
        body, name="pool_fwd", grid=(1,),
        in_specs=[pl.BlockSpec((S, 512), lambda i: (0, 3)),
                  pl.BlockSpec((4, 128, 128), lambda i: (0, 0, 0)),
                  pl.BlockSpec((1, 512), lambda i: (0, 0))],
        out_specs=[pl.BlockSpec((S, 512), lambda i: (0, 0)), pl.BlockSpec((S, 512), lambda i: (0, 0))],
        out_shape=[_sds((S, 512), BF16), _sds((S, 512), BF16)],
        scratch_shapes=[pltpu.VMEM((S + 16, 512), F32)],
        compiler_params=_cp(("arbitrary",)),
    )(h, pool_w, pool_scale)


def conv_fwd(h, dw, CH=128):
    S = h.shape[0]

    def body(a_ref, g_ref, dw_ref, y_ref, hc_ref, pad_ref):
        hc = a_ref[...] * _sigmoid(g_ref[...])
        hc_ref[...] = hc
        pad_ref[0:32, :] = jnp.zeros((32, 128), F32)
        pad_ref[32:32 + S, :] = hc
        for ch in range(S // CH):
            base = ch * CH + 2
            acc = dw_ref[0:1, :] * pad_ref[base:base + CH, :]
            for k in range(1, CONV_TAPS):
                acc = acc + dw_ref[k:k + 1, :] * pad_ref[base + k:base + k + CH, :]
            y_ref[ch * CH:(ch + 1) * CH, :] = acc

    return pl.pallas_call(
        body, name="conv_fwd", grid=(4,),
        in_specs=[pl.BlockSpec((S, 128), lambda c: (0, c)),
                  pl.BlockSpec((S, 128), lambda c: (0, 4 + c)),
                  pl.BlockSpec((CONV_TAPS, 128), lambda c: (0, c))],
        out_specs=[pl.BlockSpec((S, 128), lambda c: (0, c)), pl.BlockSpec((S, 128), lambda c: (0, c))],
        out_shape=[_sds((S, 512)), _sds((S, 512))],
        scratch_shapes=[pltpu.VMEM((S + 32, 128), F32)],
        compiler_params=_cp(("arbitrary",)),
    )(h, h, dw)


def _masked_sg_w(w_ref, g):
    row = lax.broadcasted_iota(jnp.int32, (128, 128), 0)
    col = lax.broadcasted_iota(jnp.int32, (128, 128), 1)
    return jnp.where(row >= col, w_ref[g], 0.0).astype(BF16)


def odd_post(y, h, cl_g, cl_b, sl_g, sl_b, sg_w, sgb_bc, tm=256):
    S = y.shape[0]
    tm = min(tm, S)

    def body(y_ref, zc_ref, clg, clb, slg, slb, w_ref, sb_ref,
             c_ref, d_ref, xhc_ref, rsc_ref, xhv_ref, rsv_ref, sv_ref):
        lnc, xhc, rsc = _ln_fwd(y_ref[...], clg[...], clb[...])
        c_ref[...] = (lnc * _sigmoid(lnc)).astype(BF16)
        xhc_ref[...] = xhc
        rsc_ref[...] = rsc
        z = _gelu(zc_ref[...])
        vn, xhv, rsv = _ln_fwd(z[:, 512:], slg[...], slb[...])
        xhv_ref[...] = xhv
        rsv_ref[...] = rsv
        vnb = vn.astype(BF16)
        for g in range(4):
            wm = _masked_sg_w(w_ref, g)
            for ch in range(tm // 128):
                rs, cs = slice(ch * 128, (ch + 1) * 128), slice(g * 128, (g + 1) * 128)
                sv_ref[rs, cs] = _dot(wm, vnb[rs, cs]) + sb_ref[g]
        d_ref[...] = (z[:, :512] * sv_ref[...]).astype(BF16)

    return _tok_call(
        "odd_post", body, [y, (h, 1024, 1)], [cl_g, cl_b, sl_g, sl_b, sg_w, sgb_bc],
        [_sds((S, 512), BF16), _sds((S, 512), BF16), _sds((S, 512)), _sds((S, 1)),
         _sds((S, 512)), _sds((S, 1)), _sds((S, 512))], tm=tm)


def mm_out_ln(l1, l2, x, w, g, b, dep=None):
    S, D = x.shape

    def body(l1_ref, l2_ref, x_ref, w_ref, g_ref, b_ref, y_ref, xh_ref, rs_ref):
        mix = _dot(l1_ref[...], w_ref[0:512, :]) + _dot(l2_ref[...], w_ref[512:1024, :])
        y, xh, rs = _ln_fwd(ALPHA * x_ref[...] + mix, g_ref[...], b_ref[...])
        y_ref[...] = y
        xh_ref[...] = xh
        rs_ref[...] = rs

    return _tok_call("mm_out_ln", body, [l1, l2, x], [w, g, b],
                     [_sds((S, D)), _sds((S, D)), _sds((S, 1))], dep=dep)


def ffn_up(x1, wg, wu, layer, dep=None):
    S, D = x1.shape
    F = wg.shape[-1]

    def body(x_ref, wg_ref, wu_ref, gate_ref, up_ref, hb_ref, xb_ref):
        xb = x_ref[...].astype(BF16)
        xb_ref[...] = xb
        gate = _dot(xb, wg_ref[...])
        up = _dot(xb, wu_ref[...])
        gate_ref[...] = gate.astype(BF16)
        up_ref[...] = up.astype(BF16)
        hb_ref[...] = (gate * _sigmoid(gate) * up).astype(BF16)

    return _tok_call("ffn_up", body, [x1], [(wg, layer), (wu, layer)],
                     [_sds((S, F), BF16), _sds((S, F), BF16), _sds((S, F), BF16), _sds((S, D), BF16)], dep=dep)


def ffn_down_ln(hb, x1, wd, layer, g, b):
    S, D = x1.shape

    def body(h_ref, x_ref, w_ref, g_ref, b_ref, y_ref, xh_ref, rs_ref):
        f = _dot(h_ref[...], w_ref[...])
        y, xh, rs = _ln_fwd(ALPHA * x_ref[...] + f, g_ref[...], b_ref[...])
        y_ref[...] = y
        xh_ref[...] = xh
        rs_ref[...] = rs

    return _tok_call("ffn_down_ln", body, [hb, x1], [(wd, layer), g, b],
                     [_sds((S, D)), _sds((S, D)), _sds((S, 1))])


def ple_fwd(x2, p, wpg, wpp, layer, bg, target=None, dep=None):
    S, D = x2.shape
    last = target is not None

    def body(*refs):
        if last:
            x_ref, p_ref, t_ref, wg_ref, wp_ref, b_ref, x3_ref, sg_ref, pp_ref, xb_ref, pb_ref, dy_ref, ls_ref = refs
        else:
            x_ref, p_ref, wg_ref, wp_ref, b_ref, x3_ref, sg_ref, pp_ref, xb_ref, pb_ref = refs
        x = x_ref[...]
        xb = x.astype(BF16)
        pb = p_ref[...].astype(BF16)
        xb_ref[...] = xb
        pb_ref[...] = pb
        sg = _sigmoid(_dot(xb, wg_ref[...]) + b_ref[...])
        pp = _dot(pb, wp_ref[...])
        sg_ref[...] = sg.astype(BF16)
        pp_ref[...] = pp.astype(BF16)
        x3 = x + sg * pp
        x3_ref[...] = x3
        if last:
            err = x3 - t_ref[...]
            dy_ref[...] = err * (1.0 / D)
            _acc(ls_ref, _colsum(err * err))

    outs = [_sds((S, D)), _sds((S, D), BF16), _sds((S, D), BF16), _sds((S, D), BF16), _sds((S, p.shape[1]), BF16)]
    tiled = [x2, p] + ([target] if last else [])
    if last:
        outs.append(_sds((S, D)))
    return _tok_call("ple_fwd", body, tiled, [(wpg, layer), (wpp, layer), bg], outs,
                     [_sds((1, D))] if last else [], dep=dep)


def ple_ln_bwd(dx3, sg, pp, x2b, pb, wpg, xh, rs, g, dep=None):
    S, D = dx3.shape

    def body(d_ref, sg_ref, pp_ref, x2b_ref, pb_ref, xh_ref, rs_ref, w_ref, g_ref,
             dr_ref, drb_ref, dbg_ref, dlg_ref, dlb_ref, dwg_ref, dwp_ref, accg_ref, accp_ref):
        fin_g = _sum_steps(accg_ref, dwg_ref)
        fin_p = _sum_steps(accp_ref, dwp_ref)
        d, sg = d_ref[...], sg_ref[...].astype(F32)
        dgp = d * pp_ref[...].astype(F32) * sg * (1.0 - sg)
        dgpb = dgp.astype(BF16)
        accg_ref[...] += _dot_tn(x2b_ref[...], dgpb)
        accp_ref[...] += _dot_tn(pb_ref[...], (d * sg).astype(BF16))
        _acc(dbg_ref, _colsum(dgp))
        dx2 = d + _dot_nt(dgpb, w_ref[...])
        xh = xh_ref[...]
        dr = _ln_bwd(dx2, xh, rs_ref[...], g_ref[...])
        dr_ref[...] = dr
        drb_ref[...] = dr.astype(BF16)
        _acc(dlg_ref, _colsum(dx2 * xh))
        _acc(dlb_ref, _colsum(dx2))
        fin_g()
        fin_p()

    P = pb.shape[1]
    return _tok_call("ple_ln_bwd", body, [dx3, sg, pp, x2b, pb, xh, rs], [wpg, g],
                     [_sds((S, D)), _sds((S, D), BF16)],
                     [_sds((1, D)), _sds((1, D)), _sds((1, D)), _sds((D, D), BF16), _sds((P, D), BF16)],
                     scratch=[pltpu.VMEM((D, D), F32), pltpu.VMEM((P, D), F32)], dep=dep)


def ffn_bwd(dr_b, x1b, gate, up, hb, wg, wu, wd, TH=256):
    S, D = dr_b.shape
    F = gate.shape[1]

    def body(dr_hbm, x_hbm, gate_ref, up_ref, hb_ref, wg_ref, wu_ref, wd_ref,
             dx_ref, dwg_ref, dwu_ref, dwd_ref, dr_v, x_v, sem, dg_s, du_s):
        @pl.when(pl.program_id(0) == 0)
        def _():
            c1 = pltpu.make_async_copy(dr_hbm, dr_v, sem.at[0])
            c2 = pltpu.make_async_copy(x_hbm, x_v, sem.at[1])
            c1.start()
            c2.start()
            c1.wait()
            c2.wait()
            dx_ref[...] = jnp.zeros_like(dx_ref)

        for ch in range(S // CH):
            rows = slice(ch * CH, (ch + 1) * CH)
            dh = _dot_nt(dr_v[rows, :], wd_ref[...])
            g, u = gate_ref[rows, :].astype(F32), up_ref[rows, :].astype(F32)
            s = _sigmoid(g)
            dgb = (dh * u * s * (1.0 + g * (1.0 - s))).astype(BF16)
            dub = (dh * g * s).astype(BF16)
            dg_s[rows, :] = dgb
            du_s[rows, :] = dub
            dx_ref[rows, :] += _dot_nt(dgb, wg_ref[...]) + _dot_nt(dub, wu_ref[...])
        x = x_v[...]
        dwg_ref[...] = _dot_tn(x, dg_s[...]).astype(BF16)
        dwu_ref[...] = _dot_tn(x, du_s[...]).astype(BF16)
        dwd_ref[...] = _dot_tn(hb_ref[...], dr_v[...]).astype(BF16)

    CH = min(512, S)
    col = lambda rows: pl.BlockSpec((rows, TH), lambda j: (0, j))
    row = pl.BlockSpec((TH, D), lambda j: (j, 0))
    return pl.pallas_call(
        body, name="ffn_bwd", grid=(F // TH,),
        in_specs=[ANY, ANY, col(S), col(S), col(S), col(D), col(D), row],
        out_specs=[pl.BlockSpec((S, D), lambda j: (0, 0)), col(D), col(D), row],
        out_shape=[_sds((S, D)), _sds((D, F), BF16), _sds((D, F), BF16), _sds((F, D), BF16)],
        scratch_shapes=[pltpu.VMEM((S, D), BF16), pltpu.VMEM((S, D), BF16), pltpu.SemaphoreType.DMA((2,)),
                        pltpu.VMEM((S, TH), BF16), pltpu.VMEM((S, TH), BF16)],
        compiler_params=pltpu.CompilerParams(dimension_semantics=("arbitrary",), vmem_limit_bytes=60 * 1024 * 1024),
    )(dr_b, x1b, gate, up, hb, wg, wu, wd)


def _sum_steps(acc_ref, out_ref):
    @pl.when(pl.program_id(0) == 0)
    def _():
        acc_ref[...] = jnp.zeros_like(acc_ref)

    def finish():
        @pl.when(pl.program_id(0) == pl.num_programs(0) - 1)
        def _():
            out_ref[...] = acc_ref[...].astype(BF16)
    return finish


def mix_bwd(dxp, dr2, xh, rs, l1, l2, g, w, dep=None):
    S, D = dxp.shape
    K1 = l1.shape[1]

    def body(dxp_ref, dr2_ref, xh_ref, rs_ref, l1_ref, l2_ref, g_ref, w_ref,
             dr_ref, dl_ref, dlg_ref, dlb_ref, dw_ref, acc_ref):
        finish = _sum_steps(acc_ref, dw_ref)
        d, xh = ALPHA * dr2_ref[...] + dxp_ref[...], xh_ref[...]
        dr = _ln_bwd(d, xh, rs_ref[...], g_ref[...])
        drb = dr.astype(BF16)
        dr_ref[...] = dr
        dl_ref[...] = _dot_nt(drb, w_ref[...])
        _acc(dlg_ref, _colsum(d * xh))
        _acc(dlb_ref, _colsum(d))
        acc_ref[0:K1, :] += _dot_tn(l1_ref[...], drb)
        acc_ref[K1:, :] += _dot_tn(l2_ref[...], drb)
        finish()

    return _tok_call("mix_bwd", body, [dxp, dr2, xh, rs, l1, l2], [g, w],
                     [_sds((S, D)), _sds((S, D))], [_sds((1, D)), _sds((1, D)), _sds(w.shape, BF16)],
                     scratch=[pltpu.VMEM(w.shape, F32)], dep=dep)


def dx_in(dr, pieces, w, xb):
    S, D = dr.shape
    offs = [o for _, o in pieces]
    widths = [a.shape[1] for a, _ in pieces]
    npc = len(pieces)

    def body(*refs):
        dr_ref, prefs, xb_ref, w_ref = refs[0], refs[1:1 + npc], refs[1 + npc], refs[2 + npc]
        dx_ref, dw_ref, acc_ref = refs[3 + npc:]
        finish = _sum_steps(acc_ref, dw_ref)
        acc = ALPHA * dr_ref[...]
        xb_t = xb_ref[...]
        for pr, o, n in zip(prefs, offs, widths):
            piece = pr[...]
            acc = acc + _dot_nt(piece, w_ref[:, o:o + n])
            acc_ref[:, o:o + n] += _dot_tn(xb_t, piece)
        dx_ref[...] = acc
        finish()

    return _tok_call("dx_in", body, [dr] + [a for a, _ in pieces] + [xb], [w], [_sds((S, D))],
                     [_sds(w.shape, BF16)], scratch=[pltpu.VMEM(w.shape, F32)])


def odd_post_bwd(dl, h, xhc, rsc, xhv, rsv, sv, cl_g, cl_b, sl_g, sl_b, sg_w, tm=256, dep=None):
    S = dl.shape[0]
    tm = min(tm, S)

    def body(dl_ref, zc_ref, xhc_ref, rsc_ref, xhv_ref, rsv_ref, sv_ref, clg, clb, slg, slb, w_ref,
             dy_ref, dzc_ref, dclg_ref, dclb_ref, dslg_ref, dslb_ref, dwm_ref, dsb_ref, dvn_ref):
        first = pl.program_id(0) == 0
        last = pl.program_id(0) == pl.num_programs(0) - 1
        dc, dd = dl_ref[:, 0:512], dl_ref[:, 512:1024]
        xhc = xhc_ref[...]
        lnc = xhc * clg[...] + clb[...]
        s = _sigmoid(lnc)
        dlnc = dc * s * (1.0 + lnc * (1.0 - s))
        dy_ref[...] = _ln_bwd(dlnc, xhc, rsc_ref[...], clg[...])
        _acc(dclg_ref, _colsum(dlnc * xhc))
        _acc(dclb_ref, _colsum(dlnc))
        zc = zc_ref[...]
        z = _gelu(zc)
        dsv = dd * z[:, :512]
        dsvb = dsv.astype(BF16)
        xhv = xhv_ref[...]
        vnb = (xhv * slg[...] + slb[...]).astype(BF16)

        @pl.when(first)
        def _():
            dwm_ref[...] = jnp.zeros_like(dwm_ref)
            dsb_ref[...] = jnp.zeros_like(dsb_ref)

        for g in range(4):
            wm = _masked_sg_w(w_ref, g)
            for ch in range(tm // 128):
                rs_, cs = slice(ch * 128, (ch + 1) * 128), slice(g * 128, (g + 1) * 128)
                dwm_ref[g] += _dot_nt(dsvb[rs_, cs], vnb[rs_, cs])
                dvn_ref[rs_, cs] = _dot_tn(wm, dsvb[rs_, cs])
                dsb_ref[g] += dsv[rs_, cs]
        dvn = dvn_ref[...]
        dvv = _ln_bwd(dvn, xhv, rsv_ref[...], slg[...])
        _acc(dslg_ref, _colsum(dvn * xhv))
        _acc(dslb_ref, _colsum(dvn))
        gg = _gelu_grad(zc)
        dzc_ref[:, 0:512] = (dd * sv_ref[...] * gg[:, :512]).astype(BF16)
        dzc_ref[:, 512:1024] = (dvv * gg[:, 512:]).astype(BF16)

        @pl.when(last)
        def _():
            row = lax.broadcasted_iota(jnp.int32, (128, 128), 0)
            col = lax.broadcasted_iota(jnp.int32, (128, 128), 1)
            for g in range(4):
                dwm_ref[g] = jnp.where(row >= col, dwm_ref[g], 0.0)
                dsb_ref[g] = jnp.broadcast_to(jnp.sum(dsb_ref[g], axis=1, keepdims=True), (128, 128))

    return _tok_call(
        "odd_post_bwd", body, [dl, (h, 1024, 1), xhc, rsc, xhv, rsv, sv], [cl_g, cl_b, sl_g, sl_b, sg_w],
        [_sds((S, 512)), _sds((S, 1024), BF16)],
        [_sds((1, 512)), _sds((1, 512)), _sds((1, 512)), _sds((1, 512)), _sds((4, 128, 128)), _sds((4, 128, 128))],
        tm=tm, scratch=[pltpu.VMEM((tm, 512), F32)], dep=dep)


def conv_bwd(dy, hc, h, dw, CH=128):
    S = dy.shape[0]

    def body(dy_ref, hc_ref, a_ref, g_ref, dw_ref, da_ref, dg_ref, ddw_ref, padh_ref, padd_ref, dhc_ref):
        padh_ref[0:32, :] = jnp.zeros((32, 128), F32)
        padh_ref[32:32 + S, :] = hc_ref[...]
        padd_ref[0:S, :] = dy_ref[...]
        padd_ref[S:S + 32, :] = jnp.zeros((32, 128), F32)
        taps = [jnp.zeros((1, 128), F32) for _ in range(CONV_TAPS)]
        for ch in range(S // CH):
            b0 = ch * CH
            dyc = padd_ref[b0:b0 + CH, :]
            acc = dw_ref[0:1, :] * padd_ref[b0 + 30:b0 + 30 + CH, :]
            taps[0] = taps[0] + _colsum(dyc * padh_ref[b0 + 2:b0 + 2 + CH, :])
            for k in range(1, CONV_TAPS):
                acc = acc + dw_ref[k:k + 1, :] * padd_ref[b0 + 30 - k:b0 + 30 - k + CH, :]
                taps[k] = taps[k] + _colsum(dyc * padh_ref[b0 + 2 + k:b0 + 2 + k + CH, :])
            dhc_ref[b0:b0 + CH, :] = acc
        for k in range(CONV_TAPS):
            ddw_ref[k:k + 1, :] = taps[k]
        dhc = dhc_ref[...]
        s = _sigmoid(g_ref[...])
        da_ref[...] = (dhc * s).astype(BF16)
        dg_ref[...] = (dhc * a_ref[...] * s * (1.0 - s)).astype(BF16)

    return pl.pallas_call(
        body, name="conv_bwd", grid=(4,),
        in_specs=[pl.BlockSpec((S, 128), lambda c: (0, c)),
                  pl.BlockSpec((S, 128), lambda c: (0, c)),
                  pl.BlockSpec((S, 128), lambda c: (0, c)),
                  pl.BlockSpec((S, 128), lambda c: (0, 4 + c)),
                  pl.BlockSpec((CONV_TAPS, 128), lambda c: (0, c))],
        out_specs=[pl.BlockSpec((S, 128), lambda c: (0, c)), pl.BlockSpec((S, 128), lambda c: (0, c)),
                   pl.BlockSpec((CONV_TAPS, 128), lambda c: (0, c))],
        out_shape=[_sds((S, 512), BF16), _sds((S, 512), BF16), _sds((CONV_TAPS, 512))],
        scratch_shapes=[pltpu.VMEM((S + 32, 128), F32), pltpu.VMEM((S + 32, 128), F32), pltpu.VMEM((S, 128), F32)],
        compiler_params=_cp(("arbitrary",)),
    )(dy, hc, h, h, dw)


def attn_bwd(qkv, dl, tb, T=256, dep=None):
    S = qkv.shape[0]
    T = min(T, S)
    nq = S // T

    def body(q_ref, k_ref, v_ref, do_ref, t_ref, dq_ref, dk_ref, dv_ref,
             dka_ref, dva_ref, dqa_ref, pc_ref, gc_ref, qh_ref, doh_ref):
        i = pl.program_id(0)
        hm0 = lax.broadcasted_iota(jnp.int32, (1, 128), 1) < 64
        r2 = lax.broadcasted_iota(jnp.int32, (2 * T, T), 0)
        c2 = lax.broadcasted_iota(jnp.int32, (2 * T, T), 1)
        causal = c2 < jnp.where(r2 >= T, r2 - T, r2)
        ur = lax.broadcasted_iota(jnp.int32, (T, T), 0)
        uc = lax.broadcasted_iota(jnp.int32, (T, T), 1)
        u_le = (ur <= uc).astype(BF16)
        u_lt = (ur < uc).astype(BF16)

        @pl.when(i == 0)
        def _():
            dka_ref[...] = jnp.zeros_like(dka_ref)
            dva_ref[...] = jnp.zeros_like(dva_ref)

        dqa_ref[...] = jnp.zeros_like(dqa_ref)
        gc_ref[...] = jnp.zeros_like(gc_ref)
        for pp in range(4):
            cs = slice(pp * 128, (pp + 1) * 128)
            qh_ref[pp] = _stack_heads(q_ref[:, cs] * QK_SCALE, hm0)
            doh_ref[pp] = _stack_heads(do_ref[:, cs], hm0)
            for hd in range(2):
                for half in range(T // 128):
                    pc_ref[pp, hd * T:(hd + 1) * T, half * 128:(half + 1) * 128] = t_ref[2 * pp + hd]

        def block(kb, diag):
            ks = pl.multiple_of(kb * T, T)
            cols = [slice(pp * 128, (pp + 1) * 128) for pp in range(4)]
            zs = [_dot_nt(qh_ref[pp], k_ref[pl.ds(ks, T), cols[pp]]) for pp in range(4)]
            dws = [_dot_nt(doh_ref[pp], v_ref[pl.ds(ks, T), cols[pp]]) for pp in range(4)]
            a_s, pres = [], []
            for pp in range(4):
                sp = _softplus(zs[pp])
                a_s.append(zs[pp] - sp)
                if diag:
                    sp = jnp.where(causal, sp, 0.0)
                pres.append(_cumsum_mm(sp, u_le))
            ws, gmats, gsums = [], [], []
            for pp in range(4):
                rem = pc_ref[pp]
                w = jnp.exp(a_s[pp] - rem + pres[pp])
                if diag:
                    w = jnp.where(causal, w, 0.0)
                gmat = dws[pp] * w
                ws.append(w.astype(BF16))
                gmats.append(gmat)
                gsums.append(_cumsum_mm(gmat, u_lt))
                pc_ref[pp] = rem - jnp.broadcast_to(pres[pp][:, T - 1:T], (2 * T, T))
            for pp in range(4):
                cs = cols[pp]
                sig = jnp.exp(a_s[pp])
                gex = gc_ref[pp] + gsums[pp]
                dz = gmats[pp] * (1.0 - sig) - sig * gex
                if diag:
                    dz = jnp.where(causal, dz, 0.0)
                dzb = dz.astype(BF16)
                dqa_ref[:, cs] += _dot(_unstack_k(dzb, T), _stack_heads(k_ref[pl.ds(ks, T), cs], hm0))
                dka_ref[pl.ds(ks, T), cs] += _dot_tn(dzb, qh_ref[pp])
                dva_ref[pl.ds(ks, T), cs] += _dot_tn(ws[pp], doh_ref[pp])
                gc_ref[pp] = jnp.broadcast_to(gex[:, T - 1:T] + gmats[pp][:, T - 1:T], (2 * T, T))

        def step(kb, carry):
            block(kb, False)
            return carry

        lax.fori_loop(0, i, step, 0)
        block(i, True)
        dq_ref[...] = (dqa_ref[...] * QK_SCALE).astype(BF16)

        @pl.when(i == nq - 1)
        def _():
            dk_ref[...] = dka_ref[...].astype(BF16)
            dv_ref[...] = dva_ref[...].astype(BF16)

    deps = [] if dep is None else [dep]
    call_body = body if dep is None else (lambda *refs: body(*refs[:5], *refs[6:]))
    return pl.pallas_call(
        call_body, name="attn_bwd", grid=(nq,),
        in_specs=[pl.BlockSpec((T, 512), lambda i: (i, 0)),
                  pl.BlockSpec((S, 512), lambda i: (0, 1)),
                  pl.BlockSpec((S, 512), lambda i: (0, 2)),
                  pl.BlockSpec((T, 512), lambda i: (i, 0)),
                  pl.BlockSpec((8, T, 128), lambda i: (0, i, 0))] + [ANY] * len(deps),
        out_specs=[pl.BlockSpec((T, 512), lambda i: (i, 0)),
                   pl.BlockSpec((S, 512), lambda i: (0, 0)),
                   pl.BlockSpec((S, 512), lambda i: (0, 0))],
        out_shape=[_sds((S, 512), BF16), _sds((S, 512), BF16), _sds((S, 512), BF16)],
        scratch_shapes=[pltpu.VMEM((S, 512), F32), pltpu.VMEM((S, 512), F32), pltpu.VMEM((T, 512), F32),
                        pltpu.VMEM((4, 2 * T, T), F32), pltpu.VMEM((4, 2 * T, T), F32),
                        pltpu.VMEM((4, 2 * T, 128), BF16), pltpu.VMEM((4, 2 * T, 128), BF16)],
        compiler_params=_cp(("arbitrary",)),
    )(qkv, qkv, qkv, dl, tb, *deps)


def pool_bwd(dl, pooled_b, pool_w, pool_scale, CH=256):
    S = dl.shape[0]
    CH = min(CH, S)

    def body(db_ref, pooled_ref, w_ref, sc_ref, du_ref, dw_ref, dsc_ref, pad_ref, dp_ref):
        pad_ref[S:S + 16, :] = jnp.zeros((16, 128), F32)
        for g, win in enumerate(POOL_WINDOWS):
            cs = slice(g * 128, (g + 1) * 128)
            wq = w_ref[g].astype(BF16)
            dwg = jnp.zeros((128, 128), F32)
            dsc = jnp.zeros((1, 128), F32)
            for ch in range(S // CH):
                rs_ = slice(ch * CH, (ch + 1) * CH)
                db = db_ref[rs_, cs]
                pb = pooled_ref[rs_, cs]
                dsc = dsc + _colsum(db * _dot(pb, wq))
                dmsb = (db * sc_ref[:, cs]).astype(BF16)
                dwg = dwg + _dot_tn(pb, dmsb)
                dpool = _dot_nt(dmsb, wq)
                t = ch * CH + lax.broadcasted_iota(jnp.int32, (CH, 1), 0)
                cnt = jnp.minimum(t + 1, win).astype(F32)
                dp_ref[rs_, :] = dpool
                pad_ref[rs_, :] = dpool / cnt
            dw_ref[g] = dwg
            dsc_ref[:, cs] = dsc
            for ch in range(S // CH):
                base = ch * CH
                acc = pad_ref[base:base + CH, :]
                for sft in range(1, win):
                    acc = acc + pad_ref[base + sft:base + sft + CH, :]
                du_ref[base:base + CH, cs] = (acc - dp_ref[base:base + CH, :]).astype(BF16)

    return pl.pallas_call(
        body, name="pool_bwd", grid=(1,),
        in_specs=[pl.BlockSpec((S, 512), lambda i: (0, 1)),
                  pl.BlockSpec((S, 512), lambda i: (0, 0)),
                  pl.BlockSpec((4, 128, 128), lambda i: (0, 0, 0)),
                  pl.BlockSpec((1, 512), lambda i: (0, 0))],
        out_specs=[pl.BlockSpec((S, 512), lambda i: (0, 0)),
                   pl.BlockSpec((4, 128, 128), lambda i: (0, 0, 0)),
                   pl.BlockSpec((1, 512), lambda i: (0, 0))],
        out_shape=[_sds((S, 512), BF16), _sds((4, 128, 128)), _sds((1, 512))],
        scratch_shapes=[pltpu.VMEM((S + 16, 128), F32), pltpu.VMEM((S, 128), F32)],
        compiler_params=_cp(("arbitrary",)),
    )(dl, pooled_b, pool_w, pool_scale)


def _row(a, i):
    return a[i:i + 1]


MIXER_NAMES = (("even_w_in", "even_w_out"), ("odd_w_in", "odd_w_out"))


def fwd_layer(i, xin, p_i, target, comm):
    s = {}
    W = comm.weights(("mix", i), xin)
    w_in = W[MIXER_NAMES[i][0]]
    if i == 0:
        s["h"], s["xb"], s["qkv"] = mm_in(xin, w_in, nb16=1536)
        comm.poke(("in", i), s["h"])
        s["l1"], s["tb"] = attn_fwd(s["qkv"])
        s["l2"], s["pooled"] = pool_fwd(s["h"], W["pool_w"], W["pool_scale"])
    else:
        s["h"], s["xb"] = mm_in(xin, w_in)
        comm.poke(("in", i), s["h"])
        s["y"], s["hc"] = conv_fwd(s["h"], W["conv_dw"])
        sgb_bc = jnp.broadcast_to(W["sg_b"][:, :, None], (4, 128, 128))
        (s["l1"], s["l2"], s["xhc"], s["rsc"], s["xhv"], s["rsv"], s["sv"]) = odd_post(
            s["y"], s["h"], W["conv_ln_g"], W["conv_ln_b"], W["sg_ln_g"], W["sg_ln_b"], W["sg_w"], sgb_bc)
    tok = comm.poke(("mixed", i), s["l1"])
    W = comm.weights(("out", i), s["l1"])
    x1, s["xh1"], s["rs1"] = mm_out_ln(s["l1"], s["l2"], xin, W[MIXER_NAMES[i][1]], _row(W["ln_mix_g"], i),
                                       _row(W["ln_mix_b"], i), dep=tok)
    W = comm.weights(("ffn", i), x1)
    tok = comm.poke(("up", i), x1)
    s["gate"], s["up"], s["hb"], s["x1b"] = ffn_up(x1, W["ffn_w_gate%d" % i], W["ffn_w_up%d" % i], None, dep=tok)
    W = comm.weights(("down", i), s["hb"])
    x2, s["xh2"], s["rs2"] = ffn_down_ln(s["hb"], x1, W["ffn_w_down%d" % i], None,
                                         _row(W["ln_ffn_g"], i), _row(W["ln_ffn_b"], i))
    tok = comm.poke(("ffn", i), x2)
    outs = ple_fwd(x2, p_i, W["ple_w_gate%d" % i], W["ple_w_proj%d" % i], None, _row(W["ple_b_gate"], i), target,
                   dep=tok)
    s["sg"], s["pp"], s["x2b"], s["pb"] = outs[1:5]
    return outs[0], s, outs[5:]


def bwd_layer(i, dx, s, W, comm, tok=None):
    small = {}
    dr2, dr2_b, small["ple_b_gate"], small["ln_ffn_g"], small["ln_ffn_b"], dwpg, dwpp = ple_ln_bwd(
        dx, s["sg"], s["pp"], s["x2b"], s["pb"], W["ple_w_gate%d" % i], s["xh2"], s["rs2"],
        _row(W["ln_ffn_g"], i), dep=tok)
    dxp, dwg, dwu, dwd = ffn_bwd(dr2_b, s["x1b"], s["gate"], s["up"], s["hb"], W["ffn_w_gate%d" % i],
                                 W["ffn_w_up%d" % i], W["ffn_w_down%d" % i])
    tok = comm.grads({"ple_w_gate%d" % i: dwpg, "ple_w_proj%d" % i: dwpp, "ffn_w_down%d" % i: dwd,
                      "ffn_w_gate%d" % i: dwg, "ffn_w_up%d" % i: dwu})
    iname, oname = MIXER_NAMES[i]
    dr1, dl, small["ln_mix_g"], small["ln_mix_b"], dwout = mix_bwd(
        dxp, dr2, s["xh1"], s["rs1"], s["l1"], s["l2"], _row(W["ln_mix_g"], i), W[oname], dep=tok)
    tok = comm.poke(("bwd", i), dl)
    if i == 1:
        (dy, dzc_b, small["conv_ln_g"], small["conv_ln_b"], small["sg_ln_g"], small["sg_ln_b"],
         small["sg_w"], dsb) = odd_post_bwd(dl, s["h"], s["xhc"], s["rsc"], s["xhv"], s["rsv"], s["sv"],
                                            W["conv_ln_g"], W["conv_ln_b"], W["sg_ln_g"], W["sg_ln_b"], W["sg_w"],
                                            dep=tok)
        small["sg_b"] = dsb[:, :, 0]
        da_b, dg_b, small["conv_dw"] = conv_bwd(dy, s["hc"], s["h"], W["conv_dw"])
        pieces = [(da_b, 0), (dg_b, 512), (dzc_b, 1024)]
    else:
        dq_b, dk_b, dv_b = attn_bwd(s["qkv"], dl, s["tb"], dep=tok)
        du_b, small["pool_w"], small["pool_scale"] = pool_bwd(dl, s["pooled"], W["pool_w"], W["pool_scale"])
        pieces = [(dq_b, 0), (dk_b, 512), (dv_b, 1024), (du_b, 1536)]
    dxin, dwin = dx_in(dr1, pieces, W[iname], s["xb"])
    tok = comm.grads({oname: dwout, iname: dwin})
    return dxin, small, tok


def run_layers(x, p, target, comm):
    saved, xin = [], x
    for i in range(2):
        xin, s, extra = fwd_layer(i, xin, p[i], target if i == 1 else None, comm)
        saved.append(s)
    dx, sq = extra
    W = comm.all_weights()
    per_layer = [None, None]
    tok = None
    for i in (1, 0):
        dx, per_layer[i], tok = bwd_layer(i, dx, saved[i], W, comm, tok)
    small = {}
    for k in ("ln_mix_g", "ln_mix_b", "ln_ffn_g", "ln_ffn_b", "ple_b_gate"):
        small[k] = jnp.concatenate([per_layer[0][k], per_layer[1][k]], axis=0)
    for i in range(2):
        small.update({k: v for k, v in per_layer[i].items() if k not in small})
    return sq, dx, small


def _big_table():
    t = {}
    for nm in ("even", "odd"):
        t[nm + "_w_in"] = ((1024, 2048), 1, 256, 256, nm + "_w_in", 0)
        t[nm + "_w_out"] = ((1024, 1024), 0, 128, 128, nm + "_w_out", 0)
    for l in range(2):
        t["ffn_w_gate%d" % l] = ((1024, 8 * FF_PAD), 1, FF_PAD, FF_SHARD, "ffn_w_gate", l)
        t["ffn_w_up%d" % l] = ((1024, 8 * FF_PAD), 1, FF_PAD, FF_SHARD, "ffn_w_up", l)
        t["ffn_w_down%d" % l] = ((8 * FF_PAD, 1024), 0, FF_PAD, FF_SHARD, "ffn_w_down", l)
        t["ple_w_gate%d" % l] = ((1024, 1024), 0, 128, 128, "ple_w_gate", l)
        t["ple_w_proj%d" % l] = ((256, 1024), 1, 128, 128, "ple_w_proj", l)
    return t


BIG = _big_table()
TRANSPOSED_ARGS = ("ffn_w_gate", "ffn_w_up")
SMALL_SPEC = ((N_DEV, 40, 64), 0, 1, 1)
_UP_GROUP = lambda l: ["ffn_w_gate%d" % l, "ffn_w_up%d" % l]
_DOWN_GROUP = lambda l: ["ffn_w_down%d" % l, "ple_w_gate%d" % l, "ple_w_proj%d" % l]
AG_GROUPS = (["even_w_in"], ["even_w_out"], _UP_GROUP(0), _DOWN_GROUP(0), ["odd_w_in", "odd_w_out", "small"],
             _UP_GROUP(1), _DOWN_GROUP(1))
AG_NEED = {("mix", 0): 0, ("out", 0): 1, ("ffn", 0): 2, ("down", 0): 3, ("mix", 1): 4, ("ffn", 1): 5, ("down", 1): 6}
AG_PASS = {("in", 0): 1, ("mixed", 0): 2, ("up", 0): 3, ("ffn", 0): 4, ("mixed", 1): 5, ("up", 1): 6}
ANY = pl.BlockSpec(memory_space=pl.ANY)
SEM = pl.BlockSpec(memory_space=pltpu.SEMAPHORE)


def _spec(name):
    return SMALL_SPEC if name == "small" else BIG[name]


def _win_shape(spec):
    full, axis, w = spec[:3]
    return tuple(w if d == axis else n for d, n in enumerate(full))


def _window(ref, axis, w, j):
    idx = [slice(None)] * len(ref.shape)
    idx[axis] = pl.ds(j, 1) if w == 1 else pl.ds(pl.multiple_of(j * w, w), w)
    return ref.at[tuple(idx)]


def _mesh_pos():
    return lax.axis_index("x"), lax.axis_index("y"), lax.axis_index("c")


def split_call(name, arrays, starts=(), waits=(), sems_in=(), new=(), after=None):
    n, nn, ns = len(arrays), len(new), len(starts)
    flat_sems = [s for pair in sems_in for s in pair]

    def body(*refs):
        arr = list(refs[:n])
        sin = refs[n:n + len(flat_sems)]
        outs = refs[n + len(flat_sems) + (after is not None):]
        data = arr + list(outs[n:n + nn])
        for p, k, kind, mk in waits:
            d = mk(data, sin[2 * p].at[k], sin[2 * p + 1].at[k])
            d.wait_send() if kind == "send" else d.wait_recv()
        if ns:
            send, recv = outs[n + nn], outs[n + nn + 1]
            for k, mk in enumerate(starts):
                mk(data, send.at[k], recv.at[k]).start()
        outs[-1][...] = jnp.zeros((8, 128), F32)

    sem_out = [pltpu.SemaphoreType.DMA((ns,)), pltpu.SemaphoreType.DMA((ns,))] if ns else []
    res = pl.pallas_call(
        body, name=name,
        in_specs=[ANY] * n + [SEM] * len(flat_sems) + ([ANY] if after is not None else []),
        out_specs=[ANY] * (n + nn) + [SEM] * len(sem_out) + [pl.BlockSpec(memory_space=pltpu.VMEM)],
        out_shape=[_sds(a.shape, a.dtype) for a in arrays] + list(new) + sem_out + [_sds((8, 128), F32)],
        input_output_aliases={a: a for a in range(n)},
        compiler_params=pltpu.CompilerParams(has_side_effects=pltpu.SideEffectType.DATAFLOW_SIDE_EFFECTING),
    )(*arrays, *flat_sems, *([after] if after is not None else []))
    return list(res[:n + nn]), (tuple(res[n + nn:n + nn + 2]) if ns else None), res[-1]


def _remote(src, dst, send_sem, recv_sem, dev):
    return pltpu.make_async_remote_copy(src_ref=src, dst_ref=dst, send_sem=send_sem, recv_sem=recv_sem,
                                        device_id=dev, device_id_type=MESH_T)


class Gatherer:
    def __init__(self, groups, arrays, specs, prefix):
        self.groups, self.specs, self.prefix = groups, specs, prefix
        self.names = [nm for g in groups for nm in g]
        self.arr = dict(zip(self.names, arrays))
        self.fwd_sems = {}
        self.forwarded = set()

    @staticmethod
    def _mk_first(ai, spec, k):
        def mk(refs, ss, rs):
            x, y, c = _mesh_pos()
            dev = [(x, y, 1 - c), (1 - x, y, c), (x, 1 - y, c), (1 - x, 1 - y, c)][k]
            win = _window(refs[ai], spec[1], spec[2], 4 * x + 2 * y + c)
            return _remote(win, win, ss, rs, dev)
        return mk

    @staticmethod
    def _mk_fwd(ai, spec, j):
        def mk(refs, ss, rs):
            x, y, c = _mesh_pos()
            px, py = [(1 - x, y), (x, 1 - y), (1 - x, 1 - y)][j]
            win = _window(refs[ai], spec[1], spec[2], 4 * px + 2 * py + c)
            return _remote(win, win, ss, rs, (x, y, 1 - c))
        return mk

    def start(self, after=None):
        starts = [self._mk_first(ai, self.specs[nm], k) for ai, nm in enumerate(self.names) for k in range(4)]
        arrs, self.first_sems, tok = split_call(self.prefix + "_start", [self.arr[nm] for nm in self.names],
                                                starts=starts, after=after)
        self.arr = dict(zip(self.names, arrs))
        return tok

    def forward(self, g, after=None):
        if g in self.forwarded:
            return None
        self.forwarded.add(g)
        names = self.groups[g]
        waits = [(0, 4 * self.names.index(nm) + 1 + j, "recv", self._mk_fwd(ai, self.specs[nm], j))
                 for ai, nm in enumerate(names) for j in range(3)]
        starts = [self._mk_fwd(ai, self.specs[nm], j) for ai, nm in enumerate(names) for j in range(3)]
        arrs, self.fwd_sems[g], tok = split_call(
            "%s_forward%d" % (self.prefix, g), [self.arr[nm] for nm in names], starts=starts, waits=waits,
            sems_in=[self.first_sems], after=after)
        self.arr.update(zip(names, arrs))
        return tok

    def finish(self, g, after=None):
        self.forward(g, after)
        names = self.groups[g]
        waits = []
        for ai, nm in enumerate(names):
            base = 4 * self.names.index(nm)
            waits.append((0, base, "recv", self._mk_first(ai, self.specs[nm], 0)))
            waits += [(1, 3 * ai + j, "recv", self._mk_fwd(ai, self.specs[nm], j)) for j in range(3)]
            waits += [(0, base + k, "send", self._mk_first(ai, self.specs[nm], k)) for k in range(4)]
            waits += [(1, 3 * ai + j, "send", self._mk_fwd(ai, self.specs[nm], j)) for j in range(3)]
        arrs, _, _ = split_call(
            "%s_finish%d" % (self.prefix, g), [self.arr[nm] for nm in names], waits=waits,
            sems_in=[self.first_sems, self.fwd_sems[g]], after=after)
        self.arr.update(zip(names, arrs))
        return {nm: self.arr[nm] for nm in names}


class Reducer:
    def __init__(self, cq_arr, adam):
        self.cq_arr, self.adam = cq_arr, adam
        self.groups = []
        self.n = 0
        self.last = None

    @staticmethod
    def _mk1(gi, li, spec, q):
        def mk(refs, ss, rs):
            x, y, c = _mesh_pos()
            return _remote(_window(refs[gi], spec[1], spec[2], 2 * q + (1 - c)), refs[li].at[q], ss, rs, (x, y, 1 - c))
        return mk

    @staticmethod
    def _mk2(si, li, d):
        def mk(refs, ss, rs):
            x, y, c = _mesh_pos()
            qd = lax.rem(2 * x + y + d, 4)
            return _remote(refs[si].at[d - 1], refs[li].at[3 - d], ss, rs, (lax.div(qd, 2), lax.rem(qd, 2), c))
        return mk

    def add(self, grads, after=None):
        names = list(grads)
        m = len(names)
        starts = [self._mk1(ai, m + ai, BIG[nm], q) for ai, nm in enumerate(names) for q in range(4)]
        new = [_sds((4,) + _win_shape(BIG[nm]), BF16) for nm in names]
        res, sems, tok = split_call("rs1_start%d" % self.n, [grads[nm] for nm in names], starts=starts, new=new,
                                    after=after)
        self.groups.append(dict(names=names, starts=starts, buf=res, sems=sems, stage=1, idx=self.n))
        self.n += 1
        return tok

    def step(self, after):
        tok = None
        for grp in self.groups:
            names, m = grp["names"], len(grp["names"])
            if grp["stage"] == 1:
                waits = [(0, k, kind, mk) for k, mk in enumerate(grp["starts"]) for kind in ("send", "recv")]
                res, _, _ = split_call("rs1_wait%d" % grp["idx"], grp["buf"], waits=waits, sems_in=[grp["sems"]], after=after)
                full, land1 = res[:m], res[m:]
                s1b = []
                for lo in range(0, m, 4):
                    s1b += list(add_pairs(full[lo:lo + 4], land1[lo:lo + 4], [BIG[nm] for nm in names[lo:lo + 4]],
                                          self.cq_arr))
                starts = [self._mk2(ai, m + ai, d) for ai in range(m) for d in (1, 2, 3)]
                new = [_sds(a.shape, BF16) for a in s1b]
                res, sems, tok = split_call("rs2_start%d" % grp["idx"], s1b, starts=starts, new=new, after=tok)
                grp.update(stage=2, g=full, land1=land1, starts=starts, buf=res, sems=sems)
        return tok

    def finish_oldest(self):
        for grp in self.groups:
            if grp["stage"] == 2:
                names, m = grp["names"], len(grp["names"])
                waits = [(0, k, kind, mk) for k, mk in enumerate(grp["starts"]) for kind in ("send", "recv")]
                res, _, _ = split_call("rs2_wait%d" % grp["idx"], grp["buf"], waits=waits, sems_in=[grp["sems"]],
                                       after=self.last)
                for nm, g, l1, l2 in zip(names, grp["g"], grp["land1"], res[m:]):
                    self.last = self.adam(nm, g, l1, l2, self.last)
                grp["stage"] = 3
                return True
        return False


def pack_weights(args, arg_names, small_blk, names, j_arr):
    n_in = len(args)

    def body(j_ref, *refs):
        for o, nm in enumerate(names):
            dst = refs[n_in + 1 + o]
            if nm == "small":
                dst[...] = refs[n_in][...]
                continue
            _, axis, w, valid, arg, layer = BIG[nm]
            if arg in TRANSPOSED_ARGS:
                s = refs[arg_names.index(arg)][layer]
                s = jnp.concatenate([s, jnp.zeros((w - valid, s.shape[1]), F32)], axis=0)
                dst[...] = s.T.astype(BF16)
                continue
            src = refs[arg_names.index(arg)][layer].astype(BF16)
            if valid == w:
                dst[...] = src
            else:
                dst[...] = jnp.zeros(dst.shape, BF16)
                if axis == 1:
                    dst[:, 0:valid] = src
                else:
                    dst[0:valid, :] = src

    def ispec(a):
        return pl.BlockSpec(a.shape, lambda i, j_ref: (0, 0, 0))

    def ospec(spec):
        axis, nd = spec[1], len(spec[0])
        return pl.BlockSpec(_win_shape(spec),
                            lambda i, j_ref, axis=axis, nd=nd: tuple(j_ref[0] if d == axis else 0 for d in range(nd)))

    specs = [_spec(nm) for nm in names]
    return pl.pallas_call(
        body, name="pack_weights",
        grid_spec=pltpu.PrefetchScalarGridSpec(
            num_scalar_prefetch=1, grid=(1,),
            in_specs=[ispec(a) for a in list(args) + [small_blk]], out_specs=[ospec(s) for s in specs]),
        out_shape=[_sds(s[0], F32 if nm == "small" else BF16) for nm, s in zip(names, specs)],
        compiler_params=_cp(("arbitrary",)),
    )(j_arr, *args, small_blk)


def add_pairs(fulls, lands, specs, cq_arr):
    def chip(d, cq):
        return lax.rem(cq[1] + d + 1, 4)

    in_specs, args = [], []
    for full, land, spec in zip(fulls, lands, specs):
        axis, w = spec[1], spec[2]
        R, C = full.shape
        for d in range(3):
            if axis == 1:
                in_specs.append(pl.BlockSpec((R, w), lambda i, cq, d=d: (0, 2 * chip(d, cq) + cq[0])))
                in_specs.append(pl.BlockSpec((None, R, w), lambda i, cq, d=d: (chip(d, cq), 0, 0)))
            else:
                in_specs.append(pl.BlockSpec((w, C), lambda i, cq, d=d: (2 * chip(d, cq) + cq[0], 0)))
                in_specs.append(pl.BlockSpec((None, w, C), lambda i, cq, d=d: (chip(d, cq), 0, 0)))
            args += [full, land]
    out_shape = [_sds((3,) + land.shape[1:], BF16) for land in lands]
    n = len(fulls)

    def body(cq_ref, *refs):
        for a in range(n):
            for d in range(3):
                own, got = refs[6 * a + 2 * d], refs[6 * a + 2 * d + 1]
                refs[6 * n + a][d] = (own[...].astype(F32) + got[...].astype(F32)).astype(BF16)

    return pl.pallas_call(
        body, name="add_pairs",
        grid_spec=pltpu.PrefetchScalarGridSpec(
            num_scalar_prefetch=1, grid=(1,), in_specs=in_specs,
            out_specs=[pl.BlockSpec(o.shape, lambda i, cq: (0, 0, 0)) for o in out_shape]),
        out_shape=out_shape,
        compiler_params=_cp(("arbitrary",)),
    )(cq_arr, *args)


def _adamw(w, g, m, v):
    m = ADAM_B1 * m + (1.0 - ADAM_B1) * g
    v = ADAM_B2 * v + (1.0 - ADAM_B2) * (g * g)
    m_hat = m / (1.0 - ADAM_B1 ** ADAM_STEP)
    v_hat = v / (1.0 - ADAM_B2 ** ADAM_STEP)
    delta = -ADAM_LR * (m_hat / (jnp.sqrt(v_hat) + ADAM_EPS) + ADAM_WD * w)
    return delta, m, v


def reduce_adamw(full, land1, land, w, m, v, spec, cq_arr, prev=None, dep=None):
    axis, win, valid, layer = spec[1], spec[2], spec[3], spec[5]
    L, R, C = w.shape
    transposed = spec[4] in TRANSPOSED_ARGS
    if transposed:
        grid = (1,)
        fspec = pl.BlockSpec((C, win), lambda i, cq: (0, 2 * cq[1] + cq[0]))
        wspec = pl.BlockSpec((None, C, win), lambda i, cq: (cq[1], 0, 0))
        lspec = pl.BlockSpec((3, C, win), lambda i, cq: (0, 0, 0))
        sspec = pl.BlockSpec((None, R, C), lambda i, cq: (layer, 0, 0))
    elif axis == 1:
        tr = min(1024, R)
        grid = (R // tr,)
        fspec = pl.BlockSpec((tr, win), lambda i, cq: (i, 2 * cq[1] + cq[0]))
        wspec = pl.BlockSpec((None, tr, win), lambda i, cq: (cq[1], i, 0))
        lspec = pl.BlockSpec((3, tr, win), lambda i, cq: (0, i, 0))
        sspec = pl.BlockSpec((None, tr, C), lambda i, cq: (layer, i, 0))
    else:
        grid = (1,)
        fspec = pl.BlockSpec((win, full.shape[1]), lambda i, cq: (2 * cq[1] + cq[0], 0))
        wspec = pl.BlockSpec((None, win, C), lambda i, cq: (cq[1], 0, 0))
        lspec = pl.BlockSpec((3, win, C), lambda i, cq: (0, 0, 0))
        sspec = pl.BlockSpec((None, R, C), lambda i, cq: (layer, 0, 0))

    def body(cq_ref, full_ref, own_ref, land_ref, w_ref, m_ref, v_ref, *rest):
        g_ref, d_ref, nm_ref, nv_ref = rest[-4:]
        if transposed:
            rd = lambda r, *lead: r[lead] if lead else r[...]
        elif axis == 1:
            rd = lambda r, *lead: r[(*lead, slice(None), slice(0, valid))]
        else:
            rd = lambda r, *lead: r[(*lead, slice(0, valid), slice(None))]
        g = rd(full_ref).astype(F32) + rd(own_ref).astype(F32)
        for k in range(3):
            g = g + rd(land_ref, k).astype(F32)
        if transposed:
            g = g.T[0:valid, :]
        g_ref[...] = g
        d, nm, nv = _adamw(w_ref[...], g, m_ref[...], v_ref[...])
        d_ref[...] = d
        nm_ref[...] = nm
        nv_ref[...] = nv

    extra = (list(prev) if prev is not None else []) + ([dep] if dep is not None else [])
    return pl.pallas_call(
        body, name="reduce_adamw",
        grid_spec=pltpu.PrefetchScalarGridSpec(
            num_scalar_prefetch=1, grid=grid,
            in_specs=[fspec, wspec, lspec, sspec, sspec, sspec] + [ANY] * len(extra), out_specs=[sspec] * 4),
        out_shape=[_sds(w.shape)] * 4,
        input_output_aliases={7 + k: k for k in range(4 if prev is not None else 0)},
        compiler_params=_cp(("arbitrary",)),
    )(cq_arr, full, land1, land, w, m, v, *extra)


def place_slot(packed, j_arr):
    R = packed.shape[0]

    def body(j_ref, src, dst):
        dst[...] = src[...]

    return pl.pallas_call(
        body, name="place_slot",
        grid_spec=pltpu.PrefetchScalarGridSpec(
            num_scalar_prefetch=1, grid=(1,),
            in_specs=[pl.BlockSpec((R, 128), lambda i, j: (0, 0))],
            out_specs=[pl.BlockSpec((None, R, 128), lambda i, j: (j[0], 0, 0))]),
        out_shape=[_sds((N_DEV, R, 128))], compiler_params=_cp(("arbitrary",)),
    )(j_arr, packed)[0]


def sum_slots(gathered):
    def body(g_ref, o_ref):
        g = g_ref[0]
        for dev in range(1, N_DEV):
            g = g + g_ref[dev]
        o_ref[...] = g

    return pl.pallas_call(body, name="sum_slots", out_shape=_sds(gathered.shape[1:]), compiler_params=_cp())(gathered)


def small_adamw(gs, wmv):
    k = len(gs)

    def body(*refs):
        for a in range(k):
            g, w, m, v = refs[4 * a:4 * a + 4]
            d, nm, nv = _adamw(w[...], g[...], m[...], v[...])
            refs[4 * k + 3 * a][...] = d
            refs[4 * k + 3 * a + 1][...] = nm
            refs[4 * k + 3 * a + 2][...] = nv

    args = [t for g, tup in zip(gs, wmv) for t in (g,) + tuple(tup)]
    out_shape = [_sds(g.shape) for g in gs for _ in range(3)]
    return pl.pallas_call(body, name="small_adamw", out_shape=out_shape, compiler_params=_cp())(*args)


WEIGHT_NAMES = ("even_w_in", "even_w_out", "pool_w", "pool_scale", "odd_w_in", "odd_w_out", "conv_dw", "conv_ln_g",
                "conv_ln_b", "sg_ln_g", "sg_ln_b", "sg_w", "sg_b", "ln_mix_g", "ln_mix_b", "ffn_w_gate", "ffn_w_up",
                "ffn_w_down", "ln_ffn_g", "ln_ffn_b", "ple_w_proj", "ple_w_gate", "ple_b_gate")
PACK_ARGS = ("even_w_in", "even_w_out", "odd_w_in", "odd_w_out", "ffn_w_gate", "ffn_w_up", "ffn_w_down",
             "ple_w_gate", "ple_w_proj")
REPLICATED = ("pool_w", "pool_scale", "sg_w", "sg_b", "ln_mix_g", "ln_mix_b", "ln_ffn_g", "ln_ffn_b", "ple_b_gate")
SHARDED_SMALL = ("conv_dw", "conv_ln_g", "conv_ln_b", "sg_ln_g", "sg_ln_b")
NATURAL = {"pool_w": (4, 128, 128), "pool_scale": (1, 512), "sg_w": (4, 128, 128), "sg_b": (4, 128),
           "ln_mix_g": (2, 1024), "ln_mix_b": (2, 1024), "ln_ffn_g": (2, 1024), "ln_ffn_b": (2, 1024),
           "ple_b_gate": (2, 1024)}


def kernel(x, p, even_w_in, even_w_out, pool_w, pool_scale, odd_w_in, odd_w_out, conv_dw, conv_ln_g, conv_ln_b, sg_ln_g, sg_ln_b, sg_w, sg_b, ln_mix_g, ln_mix_b, ffn_w_gate, ffn_w_up, ffn_w_down, ln_ffn_g, ln_ffn_b, ple_w_proj, ple_w_gate, ple_b_gate, loss_target, m_even_w_in, m_even_w_out, m_pool_w, m_pool_scale, m_odd_w_in, m_odd_w_out, m_conv_dw, m_conv_ln_g, m_conv_ln_b, m_sg_ln_g, m_sg_ln_b, m_sg_w, m_sg_b, m_ln_mix_g, m_ln_mix_b, m_ffn_w_gate, m_ffn_w_up, m_ffn_w_down, m_ln_ffn_g, m_ln_ffn_b, m_ple_w_proj, m_ple_w_gate, m_ple_b_gate, v_even_w_in, v_even_w_out, v_pool_w, v_pool_scale, v_odd_w_in, v_odd_w_out, v_conv_dw, v_conv_ln_g, v_conv_ln_b, v_sg_ln_g, v_sg_ln_b, v_sg_w, v_sg_b, v_ln_mix_g, v_ln_mix_b, v_ffn_w_gate, v_ffn_w_up, v_ffn_w_down, v_ln_ffn_g, v_ln_ffn_b, v_ple_w_proj, v_ple_w_gate, v_ple_b_gate):
    A = dict(locals())
    for arg in TRANSPOSED_ARGS:
        for pre in ("", "m_", "v_"):
            A[pre + arg] = jnp.swapaxes(A[pre + arg], 1, 2)
    mx, my, mc = _mesh_pos()
    j = 4 * mx + 2 * my + mc
    j_arr = j.astype(jnp.int32).reshape(1)
    cq_arr = jnp.stack([mc, 2 * mx + my]).astype(jnp.int32)
    res = {}

    def adam(nm, full, land1, land2, dep):
        arg = BIG[nm][4]
        res[arg] = reduce_adamw(full, land1, land2, A[arg], A["m_" + arg], A["v_" + arg], BIG[nm], cq_arr,
                                res.get(arg), dep)
        return res[arg][0]

    class Comm:
        def __init__(self):
            names = [nm for g in AG_GROUPS for nm in g]
            small_blk = jnp.concatenate([conv_dw[0], conv_ln_g, conv_ln_b, sg_ln_g, sg_ln_b, jnp.zeros((5, 64), F32)], axis=0)
            mine = pack_weights([A[k] for k in PACK_ARGS], PACK_ARGS, small_blk[None], names, j_arr)
            self.gat = Gatherer(AG_GROUPS, mine, {nm: _spec(nm) for nm in names}, "ag")
            self.gat.start()
            self.red = Reducer(cq_arr, adam)
            self.W = {k: A[k].reshape(NATURAL[k]) for k in REPLICATED}

        def weights(self, stage, after):
            if stage in AG_NEED:
                got = self.gat.finish(AG_NEED[stage], after)
                if "small" in got:
                    sm = got.pop("small").transpose(1, 0, 2).reshape(40, 512)
                    got.update(conv_dw=sm[0:31], conv_ln_g=sm[31:32], conv_ln_b=sm[32:33], sg_ln_g=sm[33:34],
                               sg_ln_b=sm[34:35])
                self.W.update(got)
            return self.W

        def all_weights(self):
            return self.W

        def poke(self, tag, after):
            if tag in AG_PASS:
                return self.gat.forward(AG_PASS[tag], after)
            if tag[0] == "bwd":
                return self.red.step(after)
            return None

        def grads(self, grads):
            self.n_grads = getattr(self, "n_grads", 0) + 1
            if self.n_grads == 2:
                self.held = grads
                return None
            if self.n_grads == 3:
                grads = {**self.held, **grads}
            tok = self.red.step(next(iter(grads.values())))
            return self.red.add(grads, after=tok)

    comm = Comm()
    sq, dx, small = run_layers(x[0], p[:, 0], loss_target[0], comm)
    red = comm.red
    tok = red.step(dx)

    names = REPLICATED + SHARDED_SMALL
    flat = jnp.concatenate([small[k].reshape(-1) for k in names] + [jnp.sum(sq).reshape(1)])
    rows = -(-flat.shape[0] // 1024) * 8
    packed = jnp.pad(flat, (0, rows * 128 - flat.shape[0])).reshape(rows, 128)
    sg = Gatherer((["g"],), [place_slot(packed, j_arr)], {"g": ((N_DEV, rows, 128), 0, 1, 1)}, "sg")
    red.last = sg.start(after=tok)
    older = sum(grp["stage"] == 2 for grp in red.groups) - 1
    for k in range(older):
        red.finish_oldest()
        if k == 0:
            sg.forward(0, after=red.last)
    gsum_flat = sum_slots(sg.finish(0, after=red.last)["g"]).reshape(-1)
    loss = 0.5 * gsum_flat[flat.shape[0] - 1] / x.shape[-1]
    gs, off = [], 0
    for k in names:
        n = math.prod(small[k].shape)
        g = gsum_flat[off:off + n].reshape(small[k].shape)
        off += n
        if k in SHARDED_SMALL:
            g = lax.dynamic_slice_in_dim(g, j * 64, 64, axis=1)
        gs.append(g.reshape(A[k].shape))
    outs = small_adamw(gs, [(A[k], A["m_" + k], A["v_" + k]) for k in names])
    for a, k in enumerate(names):
        res[k] = (gs[a],) + tuple(outs[3 * a:3 * a + 3])
    red.last = outs[0]
    while red.finish_oldest():
        pass

    for arg in TRANSPOSED_ARGS:
        res[arg] = [jnp.swapaxes(t, 1, 2) for t in res[arg]]
    out = [loss, dx[None]]
    for part in range(4):
        out += [res[k][part] for k in WEIGHT_NAMES]
    return tuple(out)
```

```python
import functools
import math

import jax
import jax.numpy as jnp
from jax import lax
from jax.experimental import pallas as pl
from jax.experimental.pallas import tpu as pltpu

F32, BF16 = jnp.float32, jnp.bfloat16
ALPHA = 4.0 ** 0.25
LN_EPS = 1e-5
QK_SCALE = 0.125
POOL_WINDOWS = (2, 4, 8, 16)
CONV_TAPS = 31
N_DEV = 8
FF_SHARD, FF_PAD = 352, 384
ADAM_LR, ADAM_B1, ADAM_B2, ADAM_EPS, ADAM_WD, ADAM_STEP = 0.001, 0.9, 0.999, 1e-08, 0.01, 10
VMEM_LIMIT = 56 * 1024 * 1024
MESH_T = pl.DeviceIdType.MESH


def _cp(sem=None):
    return pltpu.CompilerParams(dimension_semantics=sem, vmem_limit_bytes=VMEM_LIMIT)


def _dot(a, b):
    return jnp.dot(a, b, preferred_element_type=F32)


def _dot_nt(a, b):
    return lax.dot_general(a, b, (((1,), (1,)), ((), ())), preferred_element_type=F32)


def _dot_tn(a, b):
    return lax.dot_general(a, b, (((0,), (0,)), ((), ())), preferred_element_type=F32)


def _sigmoid(x):
    return 1.0 / (1.0 + jnp.exp(-x))


def _softplus(z):
    return jnp.maximum(z, 0.0) + jnp.log(1.0 + jnp.exp(-jnp.abs(z)))


_GELU_C = math.sqrt(2.0 / math.pi)


def _gelu(x):
    return 0.5 * x * (1.0 + jnp.tanh(_GELU_C * (x + 0.044715 * x * x * x)))


def _gelu_grad(x):
    t = jnp.tanh(_GELU_C * (x + 0.044715 * x * x * x))
    return 0.5 * (1.0 + t) + 0.5 * x * (1.0 - t * t) * _GELU_C * (1.0 + 3.0 * 0.044715 * x * x)


def _ln_fwd(r, g, b):
    mu = jnp.mean(r, axis=-1, keepdims=True)
    xc = r - mu
    var = jnp.mean(xc * xc, axis=-1, keepdims=True)
    rstd = lax.rsqrt(var + LN_EPS)
    xh = xc * rstd
    return xh * g + b, xh, rstd


def _ln_bwd(dy, xh, rstd, g):
    dxh = dy * g
    m1 = jnp.mean(dxh, axis=-1, keepdims=True)
    m2 = jnp.mean(dxh * xh, axis=-1, keepdims=True)
    return rstd * (dxh - m1 - xh * m2)


def _split2(x):
    hi = x.astype(BF16)
    lo = (x - hi.astype(F32)).astype(BF16)
    return hi, lo


def _colsum(x):
    return jnp.sum(x, axis=0, keepdims=True)


def _tok_call(name, body, tiled, full, out_tiled, out_acc=(), tm=256, scratch=(), dep=None):
    def arr(t):
        return t[0] if isinstance(t, tuple) else t
    full = [t[0] if isinstance(t, tuple) and t[1] is None else t for t in full]
    S = arr(tiled[0]).shape[0]
    tm = min(tm, S)
    n_in = len(tiled) + len(full)
    deps = [] if dep is None else [dep]
    if deps:
        inner = body
        body = lambda *refs: inner(*refs[:n_in], *refs[n_in + 1:])

    def tspec(t):
        if isinstance(t, tuple):
            _, w, cb = t
            return pl.BlockSpec((tm, w), lambda i, cb=cb: (i, cb))
        return pl.BlockSpec((tm, t.shape[1]), lambda i: (i, 0))

    def fspec(t):
        if isinstance(t, tuple):
            a, l = t
            nd = a.ndim - 1
            return pl.BlockSpec((None,) + a.shape[1:], lambda i, l=l, nd=nd: (l,) + (0,) * nd)
        nd = t.ndim
        return pl.BlockSpec(t.shape, lambda i, nd=nd: (0,) * nd)

    def ospec(o):
        return pl.BlockSpec((tm, o.shape[1]), lambda i: (i, 0))

    def aspec(o):
        nd = len(o.shape)
        return pl.BlockSpec(o.shape, lambda i, nd=nd: (0,) * nd)

    outs = pl.pallas_call(
        body, name=name, grid=(S // tm,),
        in_specs=[tspec(t) for t in tiled] + [fspec(t) for t in full] + [ANY] * len(deps),
        out_specs=[ospec(o) for o in out_tiled] + [aspec(o) for o in out_acc],
        out_shape=list(out_tiled) + list(out_acc),
        scratch_shapes=list(scratch),
        compiler_params=_cp(("arbitrary",)),
    )(*[arr(t) for t in tiled], *[arr(t) for t in full], *deps)
    return outs


def _sds(shape, dtype=F32):
    return jax.ShapeDtypeStruct(tuple(shape), dtype)


def _acc(ref, val):
    @pl.when(pl.program_id(0) == 0)
    def _():
        ref[...] = val

    @pl.when(pl.program_id(0) != 0)
    def _():
        ref[...] += val


def mm_in(x, w, nb16=0):
    S, N = x.shape[0], w.shape[1]

    def body(x_ref, w_ref, h_ref, xb_ref, *hb_ref):
        xb = x_ref[...].astype(BF16)
        xb_ref[...] = xb
        h = _dot(xb, w_ref[...])
        h_ref[...] = h
        if nb16:
            hb_ref[0][...] = h[:, 0:nb16].astype(BF16)

    outs = [_sds((S, N)), _sds((S, x.shape[1]), BF16)] + ([_sds((S, nb16), BF16)] if nb16 else [])
    return _tok_call("mm_in", body, [x], [w], outs, tm=512)


def _stack_heads(x, hm0, dtype=BF16):
    return jnp.concatenate([jnp.where(hm0, x, 0), jnp.where(hm0, 0, x)], axis=0).astype(dtype)


def _unstack_k(x, T):
    return jnp.concatenate([x[0:T], x[T:2 * T]], axis=1)


def _cumsum_mm(x, u):
    n = x.shape[0]
    hi, lo = _split2(x)
    r = _dot(jnp.concatenate([hi, lo], axis=0), u)
    return r[0:n] + r[n:2 * n]


def attn_fwd(qkv, T=256):
    S = qkv.shape[0]
    T = min(T, S)
    nq = S // T

    def body(q_ref, k_ref, v_ref, o_ref, t_ref, acc_ref, c_ref, qh_ref):
        i = pl.program_id(0)
        hm0 = lax.broadcasted_iota(jnp.int32, (1, 128), 1) < 64
        r2 = lax.broadcasted_iota(jnp.int32, (2 * T, T), 0)
        c2 = lax.broadcasted_iota(jnp.int32, (2 * T, T), 1)
        causal = c2 < jnp.where(r2 >= T, r2 - T, r2)
        ur = lax.broadcasted_iota(jnp.int32, (T, T), 0)
        uc = lax.broadcasted_iota(jnp.int32, (T, T), 1)
        u_incl = (ur >= uc).astype(BF16)
        acc_ref[...] = jnp.zeros_like(acc_ref)
        c_ref[...] = jnp.zeros_like(c_ref)
        for pp in range(4):
            qh_ref[pp] = _stack_heads(q_ref[:, pp * 128:(pp + 1) * 128] * QK_SCALE, hm0)

        def block(kb, diag):
            ks = pl.multiple_of(kb * T, T)
            cols = [slice(pp * 128, (pp + 1) * 128) for pp in range(4)]
            zs = [_dot_nt(qh_ref[pp], k_ref[pl.ds(ks, T), cols[pp]]) for pp in range(4)]
            incls = []
            for pp in range(4):
                sp = _softplus(zs[pp])
                if diag:
                    sp = jnp.where(causal, sp, 0.0)
                incls.append(_cumsum_mm(sp, u_incl))
            for pp in range(4):
                c = c_ref[pp]
                w = jnp.exp(zs[pp] - incls[pp] - c)
                if diag:
                    w = jnp.where(causal, w, 0.0)
                acc_ref[:, cols[pp]] += _dot(_unstack_k(w.astype(BF16), T),
                                             _stack_heads(v_ref[pl.ds(ks, T), cols[pp]], hm0))
                c_ref[pp] = c + jnp.broadcast_to(incls[pp][:, 0:1], (2 * T, T))

        block(i, True)

        def step(jj, carry):
            block(i - 1 - jj, False)
            return carry

        lax.fori_loop(0, i, step, 0)
        o_ref[...] = acc_ref[...].astype(BF16)
        for pp in range(4):
            for hd in range(2):
                t_ref[2 * pp + hd] = c_ref[pp, hd * T:(hd + 1) * T, 0:128]

    return pl.pallas_call(
        body, name="attn_fwd", grid=(nq,),
        in_specs=[pl.BlockSpec((T, 512), lambda i: (i, 0)),
                  pl.BlockSpec((S, 512), lambda i: (0, 1)),
                  pl.BlockSpec((S, 512), lambda i: (0, 2))],
        out_specs=[pl.BlockSpec((T, 512), lambda i: (i, 0)),
                   pl.BlockSpec((8, T, 128), lambda i: (0, i, 0))],
        out_shape=[_sds((S, 512), BF16), _sds((8, S, 128))],
        scratch_shapes=[pltpu.VMEM((T, 512), F32), pltpu.VMEM((4, 2 * T, T), F32), pltpu.VMEM((4, 2 * T, 128), BF16)],
        compiler_params=_cp(("arbitrary",)),
    )(qkv, qkv, qkv)


def pool_fwd(h, pool_w, pool_scale, CH=256):
    S = h.shape[0]
    CH = min(CH, S)

    def body(u_ref, w_ref, sc_ref, b_ref, pooled_ref, pad_ref):
        pad_ref[0:16, :] = jnp.zeros((16, 512), F32)
        pad_ref[16:16 + S, :] = u_ref[...]
        for g, win in enumerate(POOL_WINDOWS):
            cs = slice(g * 128, (g + 1) * 128)
            wq = w_ref[g].astype(BF16)
            for ch in range(S // CH):
                base = ch * CH
                acc = pad_ref[16 + base:16 + base + CH, cs]
                for sft in range(1, win):
                    acc = acc + pad_ref[16 + base - sft:16 + base - sft + CH, cs]
                t = base + lax.broadcasted_iota(jnp.int32, (CH, 1), 0)
                cnt = jnp.minimum(t + 1, win).astype(F32)
                pooled = (acc / cnt - pad_ref[16 + base:16 + base + CH, cs]).astype(BF16)
                pooled_ref[base:base + CH, cs] = pooled
                b_ref[base:base + CH, cs] = (_dot(pooled, wq) * sc_ref[:, cs]).astype(BF16)

    return pl.pallas_call(
        body, name="pool_fwd", grid=(1,),
        in_specs=[pl.BlockSpec((S, 512), lambda i: (0, 3)),
                  pl.BlockSpec((4, 128, 128), lambda i: (0, 0, 0)),
                  pl.BlockSpec((1, 512), lambda i: (0, 0))],
        out_specs=[pl.BlockSpec((S, 512), lambda i: (0, 0)), pl.BlockSpec((S, 512), lambda i: (0, 0))],
        out_shape=[_sds((S, 512), BF16), _sds((S, 512), BF16)],
        scratch_shapes=[pltpu.VMEM((S + 16, 512), F32)],
        compiler_params=_cp(("arbitrary",)),
    )(h, pool_w, pool_scale)


def conv_fwd(h, dw, CH=128):
    S = h.shape[0]

    def body(a_ref, g_ref, dw_ref, y_ref, hc_ref, pad_ref):
        hc = a_ref[...] * _sigmoid(g_ref[...])
        hc_ref[...] = hc
        pad_ref[0:32, :] = jnp.zeros((32, 128), F32)
        pad_ref[32:32 + S, :] = hc
        for ch in range(S // CH):
            base = ch * CH + 2
            acc = dw_ref[0:1, :] * pad_ref[base:base + CH, :]
            for k in range(1, CONV_TAPS):
                acc = acc + dw_ref[k:k + 1, :] * pad_ref[base + k:base + k + CH, :]
            y_ref[ch * CH:(ch + 1) * CH, :] = acc

    return pl.pallas_call(
        body, name="conv_fwd", grid=(4,),
        in_specs=[pl.BlockSpec((S, 128), lambda c: (0, c)),
                  pl.BlockSpec((S, 128), lambda c: (0, 4 + c)),
                  pl.BlockSpec((CONV_TAPS, 128), lambda c: (0, c))],
        out_specs=[pl.BlockSpec((S, 128), lambda c: (0, c)), pl.BlockSpec((S, 128), lambda c: (0, c))],
        out_shape=[_sds((S, 512)), _sds((S, 512))],
        scratch_shapes=[pltpu.VMEM((S + 32, 128), F32)],
        compiler_params=_cp(("arbitrary",)),
    )(h, h, dw)


def _masked_sg_w(w_ref, g):
    row = lax.broadcasted_iota(jnp.int32, (128, 128), 0)
    col = lax.broadcasted_iota(jnp.int32, (128, 128), 1)
    return jnp.where(row >= col, w_ref[g], 0.0).astype(BF16)


def odd_post(y, h, cl_g, cl_b, sl_g, sl_b, sg_w, sgb_bc, tm=256):
    S = y.shape[0]
    tm = min(tm, S)

    def body(y_ref, zc_ref, clg, clb, slg, slb, w_ref, sb_ref,
             c_ref, d_ref, xhc_ref, rsc_ref, xhv_ref, rsv_ref, sv_ref):
        lnc, xhc, rsc = _ln_fwd(y_ref[...], clg[...], clb[...])
        c_ref[...] = (lnc * _sigmoid(lnc)).astype(BF16)
        xhc_ref[...] = xhc
        rsc_ref[...] = rsc
        z = _gelu(zc_ref[...])
        vn, xhv, rsv = _ln_fwd(z[:, 512:], slg[...], slb[...])
        xhv_ref[...] = xhv
        rsv_ref[...] = rsv
        vnb = vn.astype(BF16)
        for g in range(4):
            wm = _masked_sg_w(w_ref, g)
            for ch in range(tm // 128):
                rs, cs = slice(ch * 128, (ch + 1) * 128), slice(g * 128, (g + 1) * 128)
                sv_ref[rs, cs] = _dot(wm, vnb[rs, cs]) + sb_ref[g]
        d_ref[...] = (z[:, :512] * sv_ref[...]).astype(BF16)

    return _tok_call(
        "odd_post", body, [y, (h, 1024, 1)], [cl_g, cl_b, sl_g, sl_b, sg_w, sgb_bc],
        [_sds((S, 512), BF16), _sds((S, 512), BF16), _sds((S, 512)), _sds((S, 1)),
         _sds((S, 512)), _sds((S, 1)), _sds((S, 512))], tm=tm)


def mm_out_ln(l1, l2, x, w, g, b, dep=None):
    S, D = x.shape

    def body(l1_ref, l2_ref, x_ref, w_ref, g_ref, b_ref, y_ref, xh_ref, rs_ref):
        mix = _dot(l1_ref[...], w_ref[0:512, :]) + _dot(l2_ref[...], w_ref[512:1024, :])
        y, xh, rs = _ln_fwd(ALPHA * x_ref[...] + mix, g_ref[...], b_ref[...])
        y_ref[...] = y
        xh_ref[...] = xh
        rs_ref[...] = rs

    return _tok_call("mm_out_ln", body, [l1, l2, x], [w, g, b],
                     [_sds((S, D)), _sds((S, D)), _sds((S, 1))], dep=dep)


def ffn_up(x1, wg, wu, layer, dep=None):
    S, D = x1.shape
    F = wg.shape[-1]

    def body(x_ref, wg_ref, wu_ref, gate_ref, up_ref, hb_ref, xb_ref):
        xb = x_ref[...].astype(BF16)
        xb_ref[...] = xb
        gate = _dot(xb, wg_ref[...])
        up = _dot(xb, wu_ref[...])
        gate_ref[...] = gate.astype(BF16)
        up_ref[...] = up.astype(BF16)
        hb_ref[...] = (gate * _sigmoid(gate) * up).astype(BF16)

    return _tok_call("ffn_up", body, [x1], [(wg, layer), (wu, layer)],
                     [_sds((S, F), BF16), _sds((S, F), BF16), _sds((S, F), BF16), _sds((S, D), BF16)], dep=dep)


def ffn_down_ln(hb, x1, wd, layer, g, b):
    S, D = x1.shape

    def body(h_ref, x_ref, w_ref, g_ref, b_ref, y_ref, xh_ref, rs_ref):
        f = _dot(h_ref[...], w_ref[...])
        y, xh, rs = _ln_fwd(ALPHA * x_ref[...] + f, g_ref[...], b_ref[...])
        y_ref[...] = y
        xh_ref[...] = xh
        rs_ref[...] = rs

    return _tok_call("ffn_down_ln", body, [hb, x1], [(wd, layer), g, b],
                     [_sds((S, D)), _sds((S, D)), _sds((S, 1))])


def ple_fwd(x2, p, wpg, wpp, layer, bg, target=None, dep=None):
    S, D = x2.shape
    last = target is not None

    def body(*refs):
        if last:
            x_ref, p_ref, t_ref, wg_ref, wp_ref, b_ref, x3_ref, sg_ref, pp_ref, xb_ref, pb_ref, dy_ref, ls_ref = refs
        else:
            x_ref, p_ref, wg_ref, wp_ref, b_ref, x3_ref, sg_ref, pp_ref, xb_ref, pb_ref = refs
        x = x_ref[...]
        xb = x.astype(BF16)
        pb = p_ref[...].astype(BF16)
        xb_ref[...] = xb
        pb_ref[...] = pb
        sg = _sigmoid(_dot(xb, wg_ref[...]) + b_ref[...])
        pp = _dot(pb, wp_ref[...])
        sg_ref[...] = sg.astype(BF16)
        pp_ref[...] = pp.astype(BF16)
        x3 = x + sg * pp
        x3_ref[...] = x3
        if last:
            err = x3 - t_ref[...]
            dy_ref[...] = err * (1.0 / D)
            _acc(ls_ref, _colsum(err * err))

    outs = [_sds((S, D)), _sds((S, D), BF16), _sds((S, D), BF16), _sds((S, D), BF16), _sds((S, p.shape[1]), BF16)]
    tiled = [x2, p] + ([target] if last else [])
    if last:
        outs.append(_sds((S, D)))
    return _tok_call("ple_fwd", body, tiled, [(wpg, layer), (wpp, layer), bg], outs,
                     [_sds((1, D))] if last else [], dep=dep)


def ple_ln_bwd(dx3, sg, pp, x2b, pb, wpg, xh, rs, g, dep=None):
    S, D = dx3.shape

    def body(d_ref, sg_ref, pp_ref, x2b_ref, pb_ref, xh_ref, rs_ref, w_ref, g_ref,
             dr_ref, drb_ref, dbg_ref, dlg_ref, dlb_ref, dwg_ref, dwp_ref, accg_ref, accp_ref):
        fin_g = _sum_steps(accg_ref, dwg_ref)
        fin_p = _sum_steps(accp_ref, dwp_ref)
        d, sg = d_ref[...], sg_ref[...].astype(F32)
        dgp = d * pp_ref[...].astype(F32) * sg * (1.0 - sg)
        dgpb = dgp.astype(BF16)
        accg_ref[...] += _dot_tn(x2b_ref[...], dgpb)
        accp_ref[...] += _dot_tn(pb_ref[...], (d * sg).astype(BF16))
        _acc(dbg_ref, _colsum(dgp))
        dx2 = d + _dot_nt(dgpb, w_ref[...])
        xh = xh_ref[...]
        dr = _ln_bwd(dx2, xh, rs_ref[...], g_ref[...])
        dr_ref[...] = dr
        drb_ref[...] = dr.astype(BF16)
        _acc(dlg_ref, _colsum(dx2 * xh))
        _acc(dlb_ref, _colsum(dx2))
        fin_g()
        fin_p()

    P = pb.shape[1]
    return _tok_call("ple_ln_bwd", body, [dx3, sg, pp, x2b, pb, xh, rs], [wpg, g],
                     [_sds((S, D)), _sds((S, D), BF16)],
                     [_sds((1, D)), _sds((1, D)), _sds((1, D)), _sds((D, D), BF16), _sds((P, D), BF16)],
                     scratch=[pltpu.VMEM((D, D), F32), pltpu.VMEM((P, D), F32)], dep=dep, tm=ACC_TM)


def ffn_bwd(dr_b, x1b, gate, up, hb, wg, wu, wd, TH=256):
    S, D = dr_b.shape
    F = gate.shape[1]

    def body(dr_hbm, x_hbm, gate_ref, up_ref, hb_ref, wg_ref, wu_ref, wd_ref,
             dx_ref, dwg_ref, dwu_ref, dwd_ref, dr_v, x_v, sem, dg_s, du_s):
        @pl.when(pl.program_id(0) == 0)
        def _():
            c1 = pltpu.make_async_copy(dr_hbm, dr_v, sem.at[0])
            c2 = pltpu.make_async_copy(x_hbm, x_v, sem.at[1])
            c1.start()
            c2.start()
            c1.wait()
            c2.wait()
            dx_ref[...] = jnp.zeros_like(dx_ref)

        for ch in range(S // CH):
            rows = slice(ch * CH, (ch + 1) * CH)
            dh = _dot_nt(dr_v[rows, :], wd_ref[...])
            g, u = gate_ref[rows, :].astype(F32), up_ref[rows, :].astype(F32)
            s = _sigmoid(g)
            dgb = (dh * u * s * (1.0 + g * (1.0 - s))).astype(BF16)
            dub = (dh * g * s).astype(BF16)
            dg_s[rows, :] = dgb
            du_s[rows, :] = dub
            dx_ref[rows, :] += _dot_nt(dgb, wg_ref[...]) + _dot_nt(dub, wu_ref[...])
        x = x_v[...]
        dwg_ref[...] = _dot_tn(x, dg_s[...]).astype(BF16)
        dwu_ref[...] = _dot_tn(x, du_s[...]).astype(BF16)
        dwd_ref[...] = _dot_tn(hb_ref[...], dr_v[...]).astype(BF16)

    CH = min(512, S)
    col = lambda rows: pl.BlockSpec((rows, TH), lambda j: (0, j))
    row = pl.BlockSpec((TH, D), lambda j: (j, 0))
    return pl.pallas_call(
        body, name="ffn_bwd", grid=(F // TH,),
        in_specs=[ANY, ANY, col(S), col(S), col(S), col(D), col(D), row],
        out_specs=[pl.BlockSpec((S, D), lambda j: (0, 0)), col(D), col(D), row],
        out_shape=[_sds((S, D)), _sds((D, F), BF16), _sds((D, F), BF16), _sds((F, D), BF16)],
        scratch_shapes=[pltpu.VMEM((S, D), BF16), pltpu.VMEM((S, D), BF16), pltpu.SemaphoreType.DMA((2,)),
                        pltpu.VMEM((S, TH), BF16), pltpu.VMEM((S, TH), BF16)],
        compiler_params=pltpu.CompilerParams(dimension_semantics=("arbitrary",), vmem_limit_bytes=60 * 1024 * 1024),
    )(dr_b, x1b, gate, up, hb, wg, wu, wd)


ACC_TM = 512


def _sum_steps(acc_ref, out_ref):
    @pl.when(pl.program_id(0) == 0)
    def _():
        acc_ref[...] = jnp.zeros_like(acc_ref)

    def finish():
        @pl.when(pl.program_id(0) == pl.num_programs(0) - 1)
        def _():
            out_ref[...] = acc_ref[...].astype(BF16)
    return finish


def mix_bwd(dxp, dr2, xh, rs, l1, l2, g, w, dep=None):
    S, D = dxp.shape
    K1 = l1.shape[1]

    def body(dxp_ref, dr2_ref, xh_ref, rs_ref, l1_ref, l2_ref, g_ref, w_ref,
             dr_ref, dl_ref, dlg_ref, dlb_ref, dw_ref, acc_ref):
        finish = _sum_steps(acc_ref, dw_ref)
        d, xh = ALPHA * dr2_ref[...] + dxp_ref[...], xh_ref[...]
        dr = _ln_bwd(d, xh, rs_ref[...], g_ref[...])
        drb = dr.astype(BF16)
        dr_ref[...] = dr
        dl_ref[...] = _dot_nt(drb, w_ref[...])
        _acc(dlg_ref, _colsum(d * xh))
        _acc(dlb_ref, _colsum(d))
        acc_ref[0:K1, :] += _dot_tn(l1_ref[...], drb)
        acc_ref[K1:, :] += _dot_tn(l2_ref[...], drb)
        finish()

    return _tok_call("mix_bwd", body, [dxp, dr2, xh, rs, l1, l2], [g, w],
                     [_sds((S, D)), _sds((S, D))], [_sds((1, D)), _sds((1, D)), _sds(w.shape, BF16)],
                     scratch=[pltpu.VMEM(w.shape, F32)], dep=dep, tm=ACC_TM)


def dx_in(dr, pieces, w, xb):
    S, D = dr.shape
    offs = [o for _, o in pieces]
    widths = [a.shape[1] for a, _ in pieces]
    npc = len(pieces)

    def body(*refs):
        dr_ref, prefs, xb_ref, w_ref = refs[0], refs[1:1 + npc], refs[1 + npc], refs[2 + npc]
        dx_ref, dw_ref, acc_ref = refs[3 + npc:]
        finish = _sum_steps(acc_ref, dw_ref)
        acc = ALPHA * dr_ref[...]
        xb_t = xb_ref[...]
        for pr, o, n in zip(prefs, offs, widths):
            piece = pr[...]
            acc = acc + _dot_nt(piece, w_ref[:, o:o + n])
            acc_ref[:, o:o + n] += _dot_tn(xb_t, piece)
        dx_ref[...] = acc
        finish()

    return _tok_call("dx_in", body, [dr] + [a for a, _ in pieces] + [xb], [w], [_sds((S, D))],
                     [_sds(w.shape, BF16)], scratch=[pltpu.VMEM(w.shape, F32)], tm=ACC_TM)


def odd_post_bwd(dl, h, xhc, rsc, xhv, rsv, sv, cl_g, cl_b, sl_g, sl_b, sg_w, tm=256, dep=None):
    S = dl.shape[0]
    tm = min(tm, S)

    def body(dl_ref, zc_ref, xhc_ref, rsc_ref, xhv_ref, rsv_ref, sv_ref, clg, clb, slg, slb, w_ref,
             dy_ref, dzc_ref, dclg_ref, dclb_ref, dslg_ref, dslb_ref, dwm_ref, dsb_ref, dvn_ref):
        first = pl.program_id(0) == 0
        last = pl.program_id(0) == pl.num_programs(0) - 1
        dc, dd = dl_ref[:, 0:512], dl_ref[:, 512:1024]
        xhc = xhc_ref[...]
        lnc = xhc * clg[...] + clb[...]
        s = _sigmoid(lnc)
        dlnc = dc * s * (1.0 + lnc * (1.0 - s))
        dy_ref[...] = _ln_bwd(dlnc, xhc, rsc_ref[...], clg[...])
        _acc(dclg_ref, _colsum(dlnc * xhc))
        _acc(dclb_ref, _colsum(dlnc))
        zc = zc_ref[...]
        z = _gelu(zc)
        dsv = dd * z[:, :512]
        dsvb = dsv.astype(BF16)
        xhv = xhv_ref[...]
        vnb = (xhv * slg[...] + slb[...]).astype(BF16)

        @pl.when(first)
        def _():
            dwm_ref[...] = jnp.zeros_like(dwm_ref)
            dsb_ref[...] = jnp.zeros_like(dsb_ref)

        for g in range(4):
            wm = _masked_sg_w(w_ref, g)
            for ch in range(tm // 128):
                rs_, cs = slice(ch * 128, (ch + 1) * 128), slice(g * 128, (g + 1) * 128)
                dwm_ref[g] += _dot_nt(dsvb[rs_, cs], vnb[rs_, cs])
                dvn_ref[rs_, cs] = _dot_tn(wm, dsvb[rs_, cs])
                dsb_ref[g] += dsv[rs_, cs]
        dvn = dvn_ref[...]
        dvv = _ln_bwd(dvn, xhv, rsv_ref[...], slg[...])
        _acc(dslg_ref, _colsum(dvn * xhv))
        _acc(dslb_ref, _colsum(dvn))
        gg = _gelu_grad(zc)
        dzc_ref[:, 0:512] = (dd * sv_ref[...] * gg[:, :512]).astype(BF16)
        dzc_ref[:, 512:1024] = (dvv * gg[:, 512:]).astype(BF16)

        @pl.when(last)
        def _():
            row = lax.broadcasted_iota(jnp.int32, (128, 128), 0)
            col = lax.broadcasted_iota(jnp.int32, (128, 128), 1)
            for g in range(4):
                dwm_ref[g] = jnp.where(row >= col, dwm_ref[g], 0.0)
                dsb_ref[g] = jnp.broadcast_to(jnp.sum(dsb_ref[g], axis=1, keepdims=True), (128, 128))

    return _tok_call(
        "odd_post_bwd", body, [dl, (h, 1024, 1), xhc, rsc, xhv, rsv, sv], [cl_g, cl_b, sl_g, sl_b, sg_w],
        [_sds((S, 512)), _sds((S, 1024), BF16)],
        [_sds((1, 512)), _sds((1, 512)), _sds((1, 512)), _sds((1, 512)), _sds((4, 128, 128)), _sds((4, 128, 128))],
        tm=tm, scratch=[pltpu.VMEM((tm, 512), F32)], dep=dep)


def conv_bwd(dy, hc, h, dw, CH=128):
    S = dy.shape[0]

    def body(dy_ref, hc_ref, a_ref, g_ref, dw_ref, da_ref, dg_ref, ddw_ref, padh_ref, padd_ref, dhc_ref):
        padh_ref[0:32, :] = jnp.zeros((32, 128), F32)
        padh_ref[32:32 + S, :] = hc_ref[...]
        padd_ref[0:S, :] = dy_ref[...]
        padd_ref[S:S + 32, :] = jnp.zeros((32, 128), F32)
        taps = [jnp.zeros((1, 128), F32) for _ in range(CONV_TAPS)]
        for ch in range(S // CH):
            b0 = ch * CH
            dyc = padd_ref[b0:b0 + CH, :]
            acc = dw_ref[0:1, :] * padd_ref[b0 + 30:b0 + 30 + CH, :]
            taps[0] = taps[0] + _colsum(dyc * padh_ref[b0 + 2:b0 + 2 + CH, :])
            for k in range(1, CONV_TAPS):
                acc = acc + dw_ref[k:k + 1, :] * padd_ref[b0 + 30 - k:b0 + 30 - k + CH, :]
                taps[k] = taps[k] + _colsum(dyc * padh_ref[b0 + 2 + k:b0 + 2 + k + CH, :])
            dhc_ref[b0:b0 + CH, :] = acc
        for k in range(CONV_TAPS):
            ddw_ref[k:k + 1, :] = taps[k]
        dhc = dhc_ref[...]
        s = _sigmoid(g_ref[...])
        da_ref[...] = (dhc * s).astype(BF16)
        dg_ref[...] = (dhc * a_ref[...] * s * (1.0 - s)).astype(BF16)

    return pl.pallas_call(
        body, name="conv_bwd", grid=(4,),
        in_specs=[pl.BlockSpec((S, 128), lambda c: (0, c)),
                  pl.BlockSpec((S, 128), lambda c: (0, c)),
                  pl.BlockSpec((S, 128), lambda c: (0, c)),
                  pl.BlockSpec((S, 128), lambda c: (0, 4 + c)),
                  pl.BlockSpec((CONV_TAPS, 128), lambda c: (0, c))],
        out_specs=[pl.BlockSpec((S, 128), lambda c: (0, c)), pl.BlockSpec((S, 128), lambda c: (0, c)),
                   pl.BlockSpec((CONV_TAPS, 128), lambda c: (0, c))],
        out_shape=[_sds((S, 512), BF16), _sds((S, 512), BF16), _sds((CONV_TAPS, 512))],
        scratch_shapes=[pltpu.VMEM((S + 32, 128), F32), pltpu.VMEM((S + 32, 128), F32), pltpu.VMEM((S, 128), F32)],
        compiler_params=_cp(("arbitrary",)),
    )(dy, hc, h, h, dw)


def attn_bwd(qkv, dl, tb, T=256, dep=None):
    S = qkv.shape[0]
    T = min(T, S)
    nq = S // T

    def body(q_ref, k_ref, v_ref, do_ref, t_ref, dq_ref, dk_ref, dv_ref,
             dka_ref, dva_ref, dqa_ref, pc_ref, gc_ref, qh_ref, doh_ref):
        i = pl.program_id(0)
        hm0 = lax.broadcasted_iota(jnp.int32, (1, 128), 1) < 64
        r2 = lax.broadcasted_iota(jnp.int32, (2 * T, T), 0)
        c2 = lax.broadcasted_iota(jnp.int32, (2 * T, T), 1)
        causal = c2 < jnp.where(r2 >= T, r2 - T, r2)
        ur = lax.broadcasted_iota(jnp.int32, (T, T), 0)
        uc = lax.broadcasted_iota(jnp.int32, (T, T), 1)
        u_le = (ur <= uc).astype(BF16)
        u_lt = (ur < uc).astype(BF16)

        @pl.when(i == 0)
        def _():
            dka_ref[...] = jnp.zeros_like(dka_ref)
            dva_ref[...] = jnp.zeros_like(dva_ref)

        dqa_ref[...] = jnp.zeros_like(dqa_ref)
        gc_ref[...] = jnp.zeros_like(gc_ref)
        for pp in range(4):
            cs = slice(pp * 128, (pp + 1) * 128)
            qh_ref[pp] = _stack_heads(q_ref[:, cs] * QK_SCALE, hm0)
            doh_ref[pp] = _stack_heads(do_ref[:, cs], hm0)
            for hd in range(2):
                for half in range(T // 128):
                    pc_ref[pp, hd * T:(hd + 1) * T, half * 128:(half + 1) * 128] = t_ref[2 * pp + hd]

        def block(kb, diag):
            ks = pl.multiple_of(kb * T, T)
            cols = [slice(pp * 128, (pp + 1) * 128) for pp in range(4)]
            zs = [_dot_nt(qh_ref[pp], k_ref[pl.ds(ks, T), cols[pp]]) for pp in range(4)]
            dws = [_dot_nt(doh_ref[pp], v_ref[pl.ds(ks, T), cols[pp]]) for pp in range(4)]
            a_s, pres = [], []
            for pp in range(4):
                sp = _softplus(zs[pp])
                a_s.append(zs[pp] - sp)
                if diag:
                    sp = jnp.where(causal, sp, 0.0)
                pres.append(_cumsum_mm(sp, u_le))
            ws, gmats, gsums = [], [], []
            for pp in range(4):
                rem = pc_ref[pp]
                w = jnp.exp(a_s[pp] - rem + pres[pp])
                if diag:
                    w = jnp.where(causal, w, 0.0)
                gmat = dws[pp] * w
                ws.append(w.astype(BF16))
                gmats.append(gmat)
                gsums.append(_cumsum_mm(gmat, u_lt))
                pc_ref[pp] = rem - jnp.broadcast_to(pres[pp][:, T - 1:T], (2 * T, T))
            for pp in range(4):
                cs = cols[pp]
                sig = jnp.exp(a_s[pp])
                gex = gc_ref[pp] + gsums[pp]
                dz = gmats[pp] * (1.0 - sig) - sig * gex
                if diag:
                    dz = jnp.where(causal, dz, 0.0)
                dzb = dz.astype(BF16)
                dqa_ref[:, cs] += _dot(_unstack_k(dzb, T), _stack_heads(k_ref[pl.ds(ks, T), cs], hm0))
                dka_ref[pl.ds(ks, T), cs] += _dot_tn(dzb, qh_ref[pp])
                dva_ref[pl.ds(ks, T), cs] += _dot_tn(ws[pp], doh_ref[pp])
                gc_ref[pp] = jnp.broadcast_to(gex[:, T - 1:T] + gmats[pp][:, T - 1:T], (2 * T, T))

        def step(kb, carry):
            block(kb, False)
            return carry

        lax.fori_loop(0, i, step, 0)
        block(i, True)
        dq_ref[...] = (dqa_ref[...] * QK_SCALE).astype(BF16)

        @pl.when(i == nq - 1)
        def _():
            dk_ref[...] = dka_ref[...].astype(BF16)
            dv_ref[...] = dva_ref[...].astype(BF16)

    deps = [] if dep is None else [dep]
    call_body = body if dep is None else (lambda *refs: body(*refs[:5], *refs[6:]))
    return pl.pallas_call(
        call_body, name="attn_bwd", grid=(nq,),
        in_specs=[pl.BlockSpec((T, 512), lambda i: (i, 0)),
                  pl.BlockSpec((S, 512), lambda i: (0, 1)),
                  pl.BlockSpec((S, 512), lambda i: (0, 2)),
                  pl.BlockSpec((T, 512), lambda i: (i, 0)),
                  pl.BlockSpec((8, T, 128), lambda i: (0, i, 0))] + [ANY] * len(deps),
        out_specs=[pl.BlockSpec((T, 512), lambda i: (i, 0)),
                   pl.BlockSpec((S, 512), lambda i: (0, 0)),
                   pl.BlockSpec((S, 512), lambda i: (0, 0))],
        out_shape=[_sds((S, 512), BF16), _sds((S, 512), BF16), _sds((S, 512), BF16)],
        scratch_shapes=[pltpu.VMEM((S, 512), F32), pltpu.VMEM((S, 512), F32), pltpu.VMEM((T, 512), F32),
                        pltpu.VMEM((4, 2 * T, T), F32), pltpu.VMEM((4, 2 * T, T), F32),
                        pltpu.VMEM((4, 2 * T, 128), BF16), pltpu.VMEM((4, 2 * T, 128), BF16)],
        compiler_params=_cp(("arbitrary",)),
    )(qkv, qkv, qkv, dl, tb, *deps)


def pool_bwd(dl, pooled_b, pool_w, pool_scale, CH=256):
    S = dl.shape[0]
    CH = min(CH, S)

    def body(db_ref, pooled_ref, w_ref, sc_ref, du_ref, dw_ref, dsc_ref, pad_ref, dp_ref):
        pad_ref[S:S + 16, :] = jnp.zeros((16, 128), F32)
        for g, win in enumerate(POOL_WINDOWS):
            cs = slice(g * 128, (g + 1) * 128)
            wq = w_ref[g].astype(BF16)
            dwg = jnp.zeros((128, 128), F32)
            dsc = jnp.zeros((1, 128), F32)
            for ch in range(S // CH):
                rs_ = slice(ch * CH, (ch + 1) * CH)
                db = db_ref[rs_, cs]
                pb = pooled_ref[rs_, cs]
                dsc = dsc + _colsum(db * _dot(pb, wq))
                dmsb = (db * sc_ref[:, cs]).astype(BF16)
                dwg = dwg + _dot_tn(pb, dmsb)
                dpool = _dot_nt(dmsb, wq)
                t = ch * CH + lax.broadcasted_iota(jnp.int32, (CH, 1), 0)
                cnt = jnp.minimum(t + 1, win).astype(F32)
                dp_ref[rs_, :] = dpool
                pad_ref[rs_, :] = dpool / cnt
            dw_ref[g] = dwg
            dsc_ref[:, cs] = dsc
            for ch in range(S // CH):
                base = ch * CH
                acc = pad_ref[base:base + CH, :]
                for sft in range(1, win):
                    acc = acc + pad_ref[base + sft:base + sft + CH, :]
                du_ref[base:base + CH, cs] = (acc - dp_ref[base:base + CH, :]).astype(BF16)

    return pl.pallas_call(
        body, name="pool_bwd", grid=(1,),
        in_specs=[pl.BlockSpec((S, 512), lambda i: (0, 1)),
                  pl.BlockSpec((S, 512), lambda i: (0, 0)),
                  pl.BlockSpec((4, 128, 128), lambda i: (0, 0, 0)),
                  pl.BlockSpec((1, 512), lambda i: (0, 0))],
        out_specs=[pl.BlockSpec((S, 512), lambda i: (0, 0)),
                   pl.BlockSpec((4, 128, 128), lambda i: (0, 0, 0)),
                   pl.BlockSpec((1, 512), lambda i: (0, 0))],
        out_shape=[_sds((S, 512), BF16), _sds((4, 128, 128)), _sds((1, 512))],
        scratch_shapes=[pltpu.VMEM((S + 16, 128), F32), pltpu.VMEM((S, 128), F32)],
        compiler_params=_cp(("arbitrary",)),
    )(dl, pooled_b, pool_w, pool_scale)


def _row(a, i):
    return a[i:i + 1]


MIXER_NAMES = (("even_w_in", "even_w_out"), ("odd_w_in", "odd_w_out"))


def fwd_layer(i, xin, p_i, target, comm):
    s = {}
    W = comm.weights(("mix", i), xin)
    w_in = W[MIXER_NAMES[i][0]]
    if i == 0:
        s["h"], s["xb"], s["qkv"] = mm_in(xin, w_in, nb16=1536)
        comm.poke(("in", i), s["h"])
        s["l1"], s["tb"] = attn_fwd(s["qkv"])
        s["l2"], s["pooled"] = pool_fwd(s["h"], W["pool_w"], W["pool_scale"])
    else:
        s["h"], s["xb"] = mm_in(xin, w_in)
        comm.poke(("in", i), s["h"])
        s["y"], s["hc"] = conv_fwd(s["h"], W["conv_dw"])
        sgb_bc = jnp.broadcast_to(W["sg_b"][:, :, None], (4, 128, 128))
        (s["l1"], s["l2"], s["xhc"], s["rsc"], s["xhv"], s["rsv"], s["sv"]) = odd_post(
            s["y"], s["h"], W["conv_ln_g"], W["conv_ln_b"], W["sg_ln_g"], W["sg_ln_b"], W["sg_w"], sgb_bc)
    tok = comm.poke(("mixed", i), s["l1"])
    W = comm.weights(("out", i), s["l1"])
    x1, s["xh1"], s["rs1"] = mm_out_ln(s["l1"], s["l2"], xin, W[MIXER_NAMES[i][1]], _row(W["ln_mix_g"], i),
                                       _row(W["ln_mix_b"], i), dep=tok)
    W = comm.weights(("ffn", i), x1)
    tok = comm.poke(("up", i), x1)
    s["gate"], s["up"], s["hb"], s["x1b"] = ffn_up(x1, W["ffn_w_gate%d" % i], W["ffn_w_up%d" % i], None, dep=tok)
    W = comm.weights(("down", i), s["hb"])
    x2, s["xh2"], s["rs2"] = ffn_down_ln(s["hb"], x1, W["ffn_w_down%d" % i], None,
                                         _row(W["ln_ffn_g"], i), _row(W["ln_ffn_b"], i))
    tok = comm.poke(("ffn", i), x2)
    outs = ple_fwd(x2, p_i, W["ple_w_gate%d" % i], W["ple_w_proj%d" % i], None, _row(W["ple_b_gate"], i), target,
                   dep=tok)
    s["sg"], s["pp"], s["x2b"], s["pb"] = outs[1:5]
    return outs[0], s, outs[5:]


def bwd_layer(i, dx, s, W, comm, tok=None):
    small = {}
    dr2, dr2_b, small["ple_b_gate"], small["ln_ffn_g"], small["ln_ffn_b"], dwpg, dwpp = ple_ln_bwd(
        dx, s["sg"], s["pp"], s["x2b"], s["pb"], W["ple_w_gate%d" % i], s["xh2"], s["rs2"],
        _row(W["ln_ffn_g"], i), dep=tok)
    dxp, dwg, dwu, dwd = ffn_bwd(dr2_b, s["x1b"], s["gate"], s["up"], s["hb"], W["ffn_w_gate%d" % i],
                                 W["ffn_w_up%d" % i], W["ffn_w_down%d" % i])
    tok = comm.grads({"ple_w_gate%d" % i: dwpg, "ple_w_proj%d" % i: dwpp, "ffn_w_down%d" % i: dwd,
                      "ffn_w_gate%d" % i: dwg, "ffn_w_up%d" % i: dwu})
    iname, oname = MIXER_NAMES[i]
    dr1, dl, small["ln_mix_g"], small["ln_mix_b"], dwout = mix_bwd(
        dxp, dr2, s["xh1"], s["rs1"], s["l1"], s["l2"], _row(W["ln_mix_g"], i), W[oname], dep=tok)
    tok = comm.poke(("bwd", i), dl)
    if i == 1:
        (dy, dzc_b, small["conv_ln_g"], small["conv_ln_b"], small["sg_ln_g"], small["sg_ln_b"],
         small["sg_w"], dsb) = odd_post_bwd(dl, s["h"], s["xhc"], s["rsc"], s["xhv"], s["rsv"], s["sv"],
                                            W["conv_ln_g"], W["conv_ln_b"], W["sg_ln_g"], W["sg_ln_b"], W["sg_w"],
                                            dep=tok)
        small["sg_b"] = dsb[:, :, 0]
        da_b, dg_b, small["conv_dw"] = conv_bwd(dy, s["hc"], s["h"], W["conv_dw"])
        pieces = [(da_b, 0), (dg_b, 512), (dzc_b, 1024)]
    else:
        dq_b, dk_b, dv_b = attn_bwd(s["qkv"], dl, s["tb"], dep=tok)
        du_b, small["pool_w"], small["pool_scale"] = pool_bwd(dl, s["pooled"], W["pool_w"], W["pool_scale"])
        pieces = [(dq_b, 0), (dk_b, 512), (dv_b, 1024), (du_b, 1536)]
    dxin, dwin = dx_in(dr1, pieces, W[iname], s["xb"])
    tok = comm.grads({oname: dwout, iname: dwin})
    return dxin, small, tok


def run_layers(x, p, target, comm):
    saved, xin = [], x
    for i in range(2):
        xin, s, extra = fwd_layer(i, xin, p[i], target if i == 1 else None, comm)
        saved.append(s)
    dx, sq = extra
    W = comm.all_weights()
    per_layer = [None, None]
    tok = None
    for i in (1, 0):
        dx, per_layer[i], tok = bwd_layer(i, dx, saved[i], W, comm, tok)
    small = {}
    for k in ("ln_mix_g", "ln_mix_b", "ln_ffn_g", "ln_ffn_b", "ple_b_gate"):
        small[k] = jnp.concatenate([per_layer[0][k], per_layer[1][k]], axis=0)
    for i in range(2):
        small.update({k: v for k, v in per_layer[i].items() if k not in small})
    return sq, dx, small


def _big_table():
    t = {}
    for nm in ("even", "odd"):
        t[nm + "_w_in"] = ((1024, 2048), 1, 256, 256, nm + "_w_in", 0)
        t[nm + "_w_out"] = ((1024, 1024), 0, 128, 128, nm + "_w_out", 0)
    for l in range(2):
        t["ffn_w_gate%d" % l] = ((1024, 8 * FF_PAD), 1, FF_PAD, FF_SHARD, "ffn_w_gate", l)
        t["ffn_w_up%d" % l] = ((1024, 8 * FF_PAD), 1, FF_PAD, FF_SHARD, "ffn_w_up", l)
        t["ffn_w_down%d" % l] = ((8 * FF_PAD, 1024), 0, FF_PAD, FF_SHARD, "ffn_w_down", l)
        t["ple_w_gate%d" % l] = ((1024, 1024), 0, 128, 128, "ple_w_gate", l)
        t["ple_w_proj%d" % l] = ((256, 1024), 1, 128, 128, "ple_w_proj", l)
    return t


BIG = _big_table()
TRANSPOSED_ARGS = ("ffn_w_gate", "ffn_w_up")
SMALL_SPEC = ((N_DEV, 40, 64), 0, 1, 1)
_UP_GROUP = lambda l: ["ffn_w_gate%d" % l, "ffn_w_up%d" % l]
_DOWN_GROUP = lambda l: ["ffn_w_down%d" % l, "ple_w_gate%d" % l, "ple_w_proj%d" % l]
AG_GROUPS = (["even_w_in"], ["even_w_out"], _UP_GROUP(0), _DOWN_GROUP(0), ["odd_w_in", "odd_w_out", "small"],
             _UP_GROUP(1), _DOWN_GROUP(1))
AG_NEED = {("mix", 0): 0, ("out", 0): 1, ("ffn", 0): 2, ("down", 0): 3, ("mix", 1): 4, ("ffn", 1): 5, ("down", 1): 6}
AG_PASS = {("in", 0): 1, ("mixed", 0): 2, ("up", 0): 3, ("ffn", 0): 4, ("mixed", 1): 5, ("up", 1): 6}
ANY = pl.BlockSpec(memory_space=pl.ANY)
SEM = pl.BlockSpec(memory_space=pltpu.SEMAPHORE)


def _spec(name):
    return SMALL_SPEC if name == "small" else BIG[name]


def _win_shape(spec):
    full, axis, w = spec[:3]
    return tuple(w if d == axis else n for d, n in enumerate(full))


def _window(ref, axis, w, j):
    idx = [slice(None)] * len(ref.shape)
    idx[axis] = pl.ds(j, 1) if w == 1 else pl.ds(pl.multiple_of(j * w, w), w)
    return ref.at[tuple(idx)]


def _mesh_pos():
    return lax.axis_index("x"), lax.axis_index("y"), lax.axis_index("c")


def split_call(name, arrays, starts=(), waits=(), sems_in=(), new=(), after=None):
    n, nn, ns = len(arrays), len(new), len(starts)
    flat_sems = [s for pair in sems_in for s in pair]

    def body(*refs):
        arr = list(refs[:n])
        sin = refs[n:n + len(flat_sems)]
        outs = refs[n + len(flat_sems) + (after is not None):]
        data = arr + list(outs[n:n + nn])
        for p, k, kind, mk in waits:
            d = mk(data, sin[2 * p].at[k], sin[2 * p + 1].at[k])
            d.wait_send() if kind == "send" else d.wait_recv()
        if ns:
            send, recv = outs[n + nn], outs[n + nn + 1]
            for k, mk in enumerate(starts):
                mk(data, send.at[k], recv.at[k]).start()
        outs[-1][...] = jnp.zeros((8, 128), F32)

    sem_out = [pltpu.SemaphoreType.DMA((ns,)), pltpu.SemaphoreType.DMA((ns,))] if ns else []
    res = pl.pallas_call(
        body, name=name,
        in_specs=[ANY] * n + [SEM] * len(flat_sems) + ([ANY] if after is not None else []),
        out_specs=[ANY] * (n + nn) + [SEM] * len(sem_out) + [pl.BlockSpec(memory_space=pltpu.VMEM)],
        out_shape=[_sds(a.shape, a.dtype) for a in arrays] + list(new) + sem_out + [_sds((8, 128), F32)],
        input_output_aliases={a: a for a in range(n)},
        compiler_params=pltpu.CompilerParams(has_side_effects=pltpu.SideEffectType.DATAFLOW_SIDE_EFFECTING),
    )(*arrays, *flat_sems, *([after] if after is not None else []))
    return list(res[:n + nn]), (tuple(res[n + nn:n + nn + 2]) if ns else None), res[-1]


def _remote(src, dst, send_sem, recv_sem, dev):
    return pltpu.make_async_remote_copy(src_ref=src, dst_ref=dst, send_sem=send_sem, recv_sem=recv_sem,
                                        device_id=dev, device_id_type=MESH_T)


class Gatherer:
    def __init__(self, groups, arrays, specs, prefix):
        self.groups, self.specs, self.prefix = groups, specs, prefix
        self.names = [nm for g in groups for nm in g]
        self.arr = dict(zip(self.names, arrays))
        self.fwd_sems = {}
        self.forwarded = set()

    @staticmethod
    def _mk_first(ai, spec, k):
        def mk(refs, ss, rs):
            x, y, c = _mesh_pos()
            dev = [(x, y, 1 - c), (1 - x, y, c), (x, 1 - y, c), (1 - x, 1 - y, c)][k]
            win = _window(refs[ai], spec[1], spec[2], 4 * x + 2 * y + c)
            return _remote(win, win, ss, rs, dev)
        return mk

    @staticmethod
    def _mk_fwd(ai, spec, j):
        def mk(refs, ss, rs):
            x, y, c = _mesh_pos()
            px, py = [(1 - x, y), (x, 1 - y), (1 - x, 1 - y)][j]
            win = _window(refs[ai], spec[1], spec[2], 4 * px + 2 * py + c)
            return _remote(win, win, ss, rs, (x, y, 1 - c))
        return mk

    def start(self, after=None):
        starts = [self._mk_first(ai, self.specs[nm], k) for ai, nm in enumerate(self.names) for k in range(4)]
        arrs, self.first_sems, tok = split_call(self.prefix + "_start", [self.arr[nm] for nm in self.names],
                                                starts=starts, after=after)
        self.arr = dict(zip(self.names, arrs))
        return tok

    def forward(self, g, after=None):
        if g in self.forwarded:
            return None
        self.forwarded.add(g)
        names = self.groups[g]
        waits = [(0, 4 * self.names.index(nm) + 1 + j, "recv", self._mk_fwd(ai, self.specs[nm], j))
                 for ai, nm in enumerate(names) for j in range(3)]
        starts = [self._mk_fwd(ai, self.specs[nm], j) for ai, nm in enumerate(names) for j in range(3)]
        arrs, self.fwd_sems[g], tok = split_call(
            "%s_forward%d" % (self.prefix, g), [self.arr[nm] for nm in names], starts=starts, waits=waits,
            sems_in=[self.first_sems], after=after)
        self.arr.update(zip(names, arrs))
        return tok

    def finish(self, g, after=None):
        self.forward(g, after)
        names = self.groups[g]
        waits = []
        for ai, nm in enumerate(names):
            base = 4 * self.names.index(nm)
            waits.append((0, base, "recv", self._mk_first(ai, self.specs[nm], 0)))
            waits += [(1, 3 * ai + j, "recv", self._mk_fwd(ai, self.specs[nm], j)) for j in range(3)]
            waits += [(0, base + k, "send", self._mk_first(ai, self.specs[nm], k)) for k in range(4)]
            waits += [(1, 3 * ai + j, "send", self._mk_fwd(ai, self.specs[nm], j)) for j in range(3)]
        arrs, _, _ = split_call(
            "%s_finish%d" % (self.prefix, g), [self.arr[nm] for nm in names], waits=waits,
            sems_in=[self.first_sems, self.fwd_sems[g]], after=after)
        self.arr.update(zip(names, arrs))
        return {nm: self.arr[nm] for nm in names}


class Reducer:
    def __init__(self, cq_arr, adam):
        self.cq_arr, self.adam = cq_arr, adam
        self.groups = []
        self.n = 0
        self.last = None

    @staticmethod
    def _mk1(gi, li, spec, q):
        def mk(refs, ss, rs):
            x, y, c = _mesh_pos()
            return _remote(_window(refs[gi], spec[1], spec[2], 2 * q + (1 - c)), refs[li].at[q], ss, rs, (x, y, 1 - c))
        return mk

    @staticmethod
    def _mk2(si, li, d):
        def mk(refs, ss, rs):
            x, y, c = _mesh_pos()
            qd = lax.rem(2 * x + y + d, 4)
            return _remote(refs[si].at[d - 1], refs[li].at[3 - d], ss, rs, (lax.div(qd, 2), lax.rem(qd, 2), c))
        return mk

    def add(self, grads, after=None):
        names = list(grads)
        m = len(names)
        starts = [self._mk1(ai, m + ai, BIG[nm], q) for ai, nm in enumerate(names) for q in range(4)]
        new = [_sds((4,) + _win_shape(BIG[nm]), BF16) for nm in names]
        res, sems, tok = split_call("rs1_start%d" % self.n, [grads[nm] for nm in names], starts=starts, new=new,
                                    after=after)
        self.groups.append(dict(names=names, starts=starts, buf=res, sems=sems, stage=1, idx=self.n))
        self.n += 1
        return tok

    def step(self, after):
        tok = None
        for grp in self.groups:
            names, m = grp["names"], len(grp["names"])
            if grp["stage"] == 1:
                waits = [(0, k, kind, mk) for k, mk in enumerate(grp["starts"]) for kind in ("send", "recv")]
                res, _, _ = split_call("rs1_wait%d" % grp["idx"], grp["buf"], waits=waits, sems_in=[grp["sems"]], after=after)
                full, land1 = res[:m], res[m:]
                s1b = []
                for lo in range(0, m, 4):
                    s1b += list(add_pairs(full[lo:lo + 4], land1[lo:lo + 4], [BIG[nm] for nm in names[lo:lo + 4]],
                                          self.cq_arr))
                starts = [self._mk2(ai, m + ai, d) for ai in range(m) for d in (1, 2, 3)]
                new = [_sds(a.shape, BF16) for a in s1b]
                res, sems, tok = split_call("rs2_start%d" % grp["idx"], s1b, starts=starts, new=new, after=tok)
                grp.update(stage=2, g=full, land1=land1, starts=starts, buf=res, sems=sems)
        return tok

    def finish_oldest(self):
        for grp in self.groups:
            if grp["stage"] == 2:
                names, m = grp["names"], len(grp["names"])
                waits = [(0, k, kind, mk) for k, mk in enumerate(grp["starts"]) for kind in ("send", "recv")]
                res, _, _ = split_call("rs2_wait%d" % grp["idx"], grp["buf"], waits=waits, sems_in=[grp["sems"]],
                                       after=self.last)
                for nm, g, l1, l2 in zip(names, grp["g"], grp["land1"], res[m:]):
                    self.last = self.adam(nm, g, l1, l2, self.last)
                grp["stage"] = 3
                return True
        return False


def pack_weights(args, arg_names, small_blk, names, j_arr):
    n_in = len(args)

    def body(j_ref, *refs):
        for o, nm in enumerate(names):
            dst = refs[n_in + 1 + o]
            if nm == "small":
                dst[...] = refs[n_in][...]
                continue
            _, axis, w, valid, arg, layer = BIG[nm]
            if arg in TRANSPOSED_ARGS:
                s = refs[arg_names.index(arg)][layer]
                s = jnp.concatenate([s, jnp.zeros((w - valid, s.shape[1]), F32)], axis=0)
                dst[...] = s.T.astype(BF16)
                continue
            src = refs[arg_names.index(arg)][layer].astype(BF16)
            if valid == w:
                dst[...] = src
            else:
                dst[...] = jnp.zeros(dst.shape, BF16)
                if axis == 1:
                    dst[:, 0:valid] = src
                else:
                    dst[0:valid, :] = src

    def ispec(a):
        return pl.BlockSpec(a.shape, lambda i, j_ref: (0, 0, 0))

    def ospec(spec):
        axis, nd = spec[1], len(spec[0])
        return pl.BlockSpec(_win_shape(spec),
                            lambda i, j_ref, axis=axis, nd=nd: tuple(j_ref[0] if d == axis else 0 for d in range(nd)))

    specs = [_spec(nm) for nm in names]
    return pl.pallas_call(
        body, name="pack_weights",
        grid_spec=pltpu.PrefetchScalarGridSpec(
            num_scalar_prefetch=1, grid=(1,),
            in_specs=[ispec(a) for a in list(args) + [small_blk]], out_specs=[ospec(s) for s in specs]),
        out_shape=[_sds(s[0], F32 if nm == "small" else BF16) for nm, s in zip(names, specs)],
        compiler_params=_cp(("arbitrary",)),
    )(j_arr, *args, small_blk)


def add_pairs(fulls, lands, specs, cq_arr):
    def chip(d, cq):
        return lax.rem(cq[1] + d + 1, 4)

    in_specs, args = [], []
    for full, land, spec in zip(fulls, lands, specs):
        axis, w = spec[1], spec[2]
        R, C = full.shape
        for d in range(3):
            if axis == 1:
                in_specs.append(pl.BlockSpec((R, w), lambda i, cq, d=d: (0, 2 * chip(d, cq) + cq[0])))
                in_specs.append(pl.BlockSpec((None, R, w), lambda i, cq, d=d: (chip(d, cq), 0, 0)))
            else:
                in_specs.append(pl.BlockSpec((w, C), lambda i, cq, d=d: (2 * chip(d, cq) + cq[0], 0)))
                in_specs.append(pl.BlockSpec((None, w, C), lambda i, cq, d=d: (chip(d, cq), 0, 0)))
            args += [full, land]
    out_shape = [_sds((3,) + land.shape[1:], BF16) for land in lands]
    n = len(fulls)

    def body(cq_ref, *refs):
        for a in range(n):
            for d in range(3):
                own, got = refs[6 * a + 2 * d], refs[6 * a + 2 * d + 1]
                refs[6 * n + a][d] = (own[...].astype(F32) + got[...].astype(F32)).astype(BF16)

    return pl.pallas_call(
        body, name="add_pairs",
        grid_spec=pltpu.PrefetchScalarGridSpec(
            num_scalar_prefetch=1, grid=(1,), in_specs=in_specs,
            out_specs=[pl.BlockSpec(o.shape, lambda i, cq: (0, 0, 0)) for o in out_shape]),
        out_shape=out_shape,
        compiler_params=_cp(("arbitrary",)),
    )(cq_arr, *args)


def _adamw(w, g, m, v):
    m = ADAM_B1 * m + (1.0 - ADAM_B1) * g
    v = ADAM_B2 * v + (1.0 - ADAM_B2) * (g * g)
    m_hat = m / (1.0 - ADAM_B1 ** ADAM_STEP)
    v_hat = v / (1.0 - ADAM_B2 ** ADAM_STEP)
    delta = -ADAM_LR * (m_hat / (jnp.sqrt(v_hat) + ADAM_EPS) + ADAM_WD * w)
    return delta, m, v


def reduce_adamw(full, land1, land, w, m, v, spec, cq_arr, prev=None, dep=None):
    axis, win, valid, layer = spec[1], spec[2], spec[3], spec[5]
    L, R, C = w.shape
    transposed = spec[4] in TRANSPOSED_ARGS
    TL = 256
    if transposed:
        grid = (C // TL,)
        fspec = pl.BlockSpec((TL, win), lambda i, cq: (i, 2 * cq[1] + cq[0]))
        wspec = pl.BlockSpec((None, TL, win), lambda i, cq: (cq[1], i, 0))
        lspec = pl.BlockSpec((3, TL, win), lambda i, cq: (0, i, 0))
        sspec = pl.BlockSpec((None, R, TL), lambda i, cq: (layer, 0, i))
    elif axis == 1:
        tr = min(TL, R)
        grid = (R // tr,)
        fspec = pl.BlockSpec((tr, win), lambda i, cq: (i, 2 * cq[1] + cq[0]))
        wspec = pl.BlockSpec((None, tr, win), lambda i, cq: (cq[1], i, 0))
        lspec = pl.BlockSpec((3, tr, win), lambda i, cq: (0, i, 0))
        sspec = pl.BlockSpec((None, tr, C), lambda i, cq: (layer, i, 0))
    else:
        grid = (C // TL,)
        fspec = pl.BlockSpec((win, TL), lambda i, cq: (2 * cq[1] + cq[0], i))
        wspec = pl.BlockSpec((None, win, TL), lambda i, cq: (cq[1], 0, i))
        lspec = pl.BlockSpec((3, win, TL), lambda i, cq: (0, 0, i))
        sspec = pl.BlockSpec((None, R, TL), lambda i, cq: (layer, 0, i))

    def body(cq_ref, full_ref, own_ref, land_ref, w_ref, m_ref, v_ref, *rest):
        g_ref, d_ref, nm_ref, nv_ref = rest[-4:]
        if transposed:
            rd = lambda r, *lead: r[lead] if lead else r[...]
        elif axis == 1:
            rd = lambda r, *lead: r[(*lead, slice(None), slice(0, valid))]
        else:
            rd = lambda r, *lead: r[(*lead, slice(0, valid), slice(None))]
        g = rd(full_ref).astype(F32) + rd(own_ref).astype(F32)
        for k in range(3):
            g = g + rd(land_ref, k).astype(F32)
        if transposed:
            g = g.T[0:valid, :]
        g_ref[...] = g
        d, nm, nv = _adamw(w_ref[...], g, m_ref[...], v_ref[...])
        d_ref[...] = d
        nm_ref[...] = nm
        nv_ref[...] = nv

    extra = (list(prev) if prev is not None else []) + ([dep] if dep is not None else [])
    return pl.pallas_call(
        body, name="reduce_adamw",
        grid_spec=pltpu.PrefetchScalarGridSpec(
            num_scalar_prefetch=1, grid=grid,
            in_specs=[fspec, wspec, lspec, sspec, sspec, sspec] + [ANY] * len(extra), out_specs=[sspec] * 4),
        out_shape=[_sds(w.shape)] * 4,
        input_output_aliases={7 + k: k for k in range(4 if prev is not None else 0)},
        compiler_params=_cp(("arbitrary",)),
    )(cq_arr, full, land1, land, w, m, v, *extra)


def place_slot(packed, j_arr):
    R = packed.shape[0]

    def body(j_ref, src, dst):
        dst[...] = src[...]

    return pl.pallas_call(
        body, name="place_slot",
        grid_spec=pltpu.PrefetchScalarGridSpec(
            num_scalar_prefetch=1, grid=(1,),
            in_specs=[pl.BlockSpec((R, 128), lambda i, j: (0, 0))],
            out_specs=[pl.BlockSpec((None, R, 128), lambda i, j: (j[0], 0, 0))]),
        out_shape=[_sds((N_DEV, R, 128))], compiler_params=_cp(("arbitrary",)),
    )(j_arr, packed)[0]


def sum_slots(gathered):
    def body(g_ref, o_ref):
        g = g_ref[0]
        for dev in range(1, N_DEV):
            g = g + g_ref[dev]
        o_ref[...] = g

    return pl.pallas_call(body, name="sum_slots", out_shape=_sds(gathered.shape[1:]), compiler_params=_cp())(gathered)


def small_adamw(gs, wmv):
    k = len(gs)

    def body(*refs):
        for a in range(k):
            g, w, m, v = refs[4 * a:4 * a + 4]
            d, nm, nv = _adamw(w[...], g[...], m[...], v[...])
            refs[4 * k + 3 * a][...] = d
            refs[4 * k + 3 * a + 1][...] = nm
            refs[4 * k + 3 * a + 2][...] = nv

    args = [t for g, tup in zip(gs, wmv) for t in (g,) + tuple(tup)]
    out_shape = [_sds(g.shape) for g in gs for _ in range(3)]
    return pl.pallas_call(body, name="small_adamw", out_shape=out_shape, compiler_params=_cp())(*args)


WEIGHT_NAMES = ("even_w_in", "even_w_out", "pool_w", "pool_scale", "odd_w_in", "odd_w_out", "conv_dw", "conv_ln_g",
                "conv_ln_b", "sg_ln_g", "sg_ln_b", "sg_w", "sg_b", "ln_mix_g", "ln_mix_b", "ffn_w_gate", "ffn_w_up",
                "ffn_w_down", "ln_ffn_g", "ln_ffn_b", "ple_w_proj", "ple_w_gate", "ple_b_gate")
PACK_ARGS = ("even_w_in", "even_w_out", "odd_w_in", "odd_w_out", "ffn_w_gate", "ffn_w_up", "ffn_w_down",
             "ple_w_gate", "ple_w_proj")
REPLICATED = ("pool_w", "pool_scale", "sg_w", "sg_b", "ln_mix_g", "ln_mix_b", "ln_ffn_g", "ln_ffn_b", "ple_b_gate")
SHARDED_SMALL = ("conv_dw", "conv_ln_g", "conv_ln_b", "sg_ln_g", "sg_ln_b")
NATURAL = {"pool_w": (4, 128, 128), "pool_scale": (1, 512), "sg_w": (4, 128, 128), "sg_b": (4, 128),
           "ln_mix_g": (2, 1024), "ln_mix_b": (2, 1024), "ln_ffn_g": (2, 1024), "ln_ffn_b": (2, 1024),
           "ple_b_gate": (2, 1024)}


def kernel(x, p, even_w_in, even_w_out, pool_w, pool_scale, odd_w_in, odd_w_out, conv_dw, conv_ln_g, conv_ln_b, sg_ln_g, sg_ln_b, sg_w, sg_b, ln_mix_g, ln_mix_b, ffn_w_gate, ffn_w_up, ffn_w_down, ln_ffn_g, ln_ffn_b, ple_w_proj, ple_w_gate, ple_b_gate, loss_target, m_even_w_in, m_even_w_out, m_pool_w, m_pool_scale, m_odd_w_in, m_odd_w_out, m_conv_dw, m_conv_ln_g, m_conv_ln_b, m_sg_ln_g, m_sg_ln_b, m_sg_w, m_sg_b, m_ln_mix_g, m_ln_mix_b, m_ffn_w_gate, m_ffn_w_up, m_ffn_w_down, m_ln_ffn_g, m_ln_ffn_b, m_ple_w_proj, m_ple_w_gate, m_ple_b_gate, v_even_w_in, v_even_w_out, v_pool_w, v_pool_scale, v_odd_w_in, v_odd_w_out, v_conv_dw, v_conv_ln_g, v_conv_ln_b, v_sg_ln_g, v_sg_ln_b, v_sg_w, v_sg_b, v_ln_mix_g, v_ln_mix_b, v_ffn_w_gate, v_ffn_w_up, v_ffn_w_down, v_ln_ffn_g, v_ln_ffn_b, v_ple_w_proj, v_ple_w_gate, v_ple_b_gate):
    A = dict(locals())
    for arg in TRANSPOSED_ARGS:
        for pre in ("", "m_", "v_"):
            A[pre + arg] = jnp.swapaxes(A[pre + arg], 1, 2)
    mx, my, mc = _mesh_pos()
    j = 4 * mx + 2 * my + mc
    j_arr = j.astype(jnp.int32).reshape(1)
    cq_arr = jnp.stack([mc, 2 * mx + my]).astype(jnp.int32)
    res = {}

    def adam(nm, full, land1, land2, dep):
        arg = BIG[nm][4]
        res[arg] = reduce_adamw(full, land1, land2, A[arg], A["m_" + arg], A["v_" + arg], BIG[nm], cq_arr,
                                res.get(arg), dep)
        return res[arg][0]

    class Comm:
        def __init__(self):
            names = [nm for g in AG_GROUPS for nm in g]
            small_blk = jnp.concatenate([conv_dw[0], conv_ln_g, conv_ln_b, sg_ln_g, sg_ln_b, jnp.zeros((5, 64), F32)], axis=0)
            mine = pack_weights([A[k] for k in PACK_ARGS], PACK_ARGS, small_blk[None], names, j_arr)
            self.gat = Gatherer(AG_GROUPS, mine, {nm: _spec(nm) for nm in names}, "ag")
            self.gat.start()
            self.red = Reducer(cq_arr, adam)
            self.W = {k: A[k].reshape(NATURAL[k]) for k in REPLICATED}

        def weights(self, stage, after):
            if stage in AG_NEED:
                got = self.gat.finish(AG_NEED[stage], after)
                if "small" in got:
                    sm = got.pop("small").transpose(1, 0, 2).reshape(40, 512)
                    got.update(conv_dw=sm[0:31], conv_ln_g=sm[31:32], conv_ln_b=sm[32:33], sg_ln_g=sm[33:34],
                               sg_ln_b=sm[34:35])
                self.W.update(got)
            return self.W

        def all_weights(self):
            return self.W

        def poke(self, tag, after):
            if tag in AG_PASS:
                return self.gat.forward(AG_PASS[tag], after)
            if tag[0] == "bwd":
                return self.red.step(after)
            return None

        def grads(self, grads):
            self.n_grads = getattr(self, "n_grads", 0) + 1
            if self.n_grads == 2:
                self.held = grads
                return None
            if self.n_grads == 3:
                grads = {**self.held, **grads}
            tok = self.red.step(next(iter(grads.values())))
            return self.red.add(grads, after=tok)

    comm = Comm()
    sq, dx, small = run_layers(x[0], p[:, 0], loss_target[0], comm)
    red = comm.red
    tok = red.step(dx)

    names = REPLICATED + SHARDED_SMALL
    flat = jnp.concatenate([small[k].reshape(-1) for k in names] + [jnp.sum(sq).reshape(1)])
    rows = -(-flat.shape[0] // 1024) * 8
    packed = jnp.pad(flat, (0, rows * 128 - flat.shape[0])).reshape(rows, 128)
    sg = Gatherer((["g"],), [place_slot(packed, j_arr)], {"g": ((N_DEV, rows, 128), 0, 1, 1)}, "sg")
    red.last = sg.start(after=tok)
    older = sum(grp["stage"] == 2 for grp in red.groups) - 1
    for k in range(older):
        red.finish_oldest()
        if k == 0:
            sg.forward(0, after=red.last)
    gsum_flat = sum_slots(sg.finish(0, after=red.last)["g"]).reshape(-1)
    loss = 0.5 * gsum_flat[flat.shape[0] - 1] / x.shape[-1]
    gs, off = [], 0
    for k in names:
        n = math.prod(small[k].shape)
        g = gsum_flat[off:off + n].reshape(small[k].shape)
        off += n
        if k in SHARDED_SMALL:
            g = lax.dynamic_slice_in_dim(g, j * 64, 64, axis=1)
        gs.append(g.reshape(A[k].shape))
    outs = small_adamw(gs, [(A[k], A["m_" + k], A["v_" + k]) for k in names])
    for a, k in enumerate(names):
        res[k] = (gs[a],) + tuple(outs[3 * a:3 * a + 3])
    red.last = outs[0]
    while red.finish_oldest():
        pass

    for arg in TRANSPOSED_ARGS:
        res[arg] = [jnp.swapaxes(t, 1, 2) for t in res[arg]]
    out = [loss, dx[None]]
    for part in range(4):
        out += [res[k][part] for k in WEIGHT_NAMES]
    return tuple(out)
```

```python
import functools
import math

import jax
import jax.numpy as jnp
from jax import lax
from jax.experimental import pallas as pl
from jax.experimental.pallas import tpu as pltpu

F32, BF16 = jnp.float32, jnp.bfloat16
ALPHA = 4.0 ** 0.25
LN_EPS = 1e-5
QK_SCALE = 0.125
POOL_WINDOWS = (2, 4, 8, 16)
CONV_TAPS = 31
N_DEV = 8
FF_SHARD, FF_PAD = 352, 384
ADAM_LR, ADAM_B1, ADAM_B2, ADAM_EPS, ADAM_WD, ADAM_STEP = 0.001, 0.9, 0.999, 1e-08, 0.01, 10
VMEM_LIMIT = 56 * 1024 * 1024
MESH_T = pl.DeviceIdType.MESH


def _cp(sem=None):
    return pltpu.CompilerParams(dimension_semantics=sem, vmem_limit_bytes=VMEM_LIMIT)


def _dot(a, b):
    return jnp.dot(a, b, preferred_element_type=F32)


def _dot_nt(a, b):
    return lax.dot_general(a, b, (((1,), (1,)), ((), ())), preferred_element_type=F32)


def _dot_tn(a, b):
    return lax.dot_general(a, b, (((0,), (0,)), ((), ())), preferred_element_type=F32)


def _sigmoid(x):
    return 1.0 / (1.0 + jnp.exp(-x))


def _softplus(z):
    return jnp.maximum(z, 0.0) + jnp.log(1.0 + jnp.exp(-jnp.abs(z)))


_GELU_C = math.sqrt(2.0 / math.pi)


def _gelu(x):
    return 0.5 * x * (1.0 + jnp.tanh(_GELU_C * (x + 0.044715 * x * x * x)))


def _gelu_grad(x):
    t = jnp.tanh(_GELU_C * (x + 0.044715 * x * x * x))
    return 0.5 * (1.0 + t) + 0.5 * x * (1.0 - t * t) * _GELU_C * (1.0 + 3.0 * 0.044715 * x * x)


def _ln_fwd(r, g, b):
    mu = jnp.mean(r, axis=-1, keepdims=True)
    xc = r - mu
    var = jnp.mean(xc * xc, axis=-1, keepdims=True)
    rstd = lax.rsqrt(var + LN_EPS)
    xh = xc * rstd
    return xh * g + b, xh, rstd


def _ln_bwd(dy, xh, rstd, g):
    dxh = dy * g
    m1 = jnp.mean(dxh, axis=-1, keepdims=True)
    m2 = jnp.mean(dxh * xh, axis=-1, keepdims=True)
    return rstd * (dxh - m1 - xh * m2)


def _split2(x):
    hi = x.astype(BF16)
    lo = (x - hi.astype(F32)).astype(BF16)
    return hi, lo


def _colsum(x):
    return jnp.sum(x, axis=0, keepdims=True)


def _tok_call(name, body, tiled, full, out_tiled, out_acc=(), tm=256, scratch=(), dep=None):
    def arr(t):
        return t[0] if isinstance(t, tuple) else t
    full = [t[0] if isinstance(t, tuple) and t[1] is None else t for t in full]
    S = arr(tiled[0]).shape[0]
    tm = min(tm, S)
    n_in = len(tiled) + len(full)
    deps = [] if dep is None else [dep]
    if deps:
        inner = body
        body = lambda *refs: inner(*refs[:n_in], *refs[n_in + 1:])

    def tspec(t):
        if isinstance(t, tuple):
            _, w, cb = t
            return pl.BlockSpec((tm, w), lambda i, cb=cb: (i, cb))
        return pl.BlockSpec((tm, t.shape[1]), lambda i: (i, 0))

    def fspec(t):
        if isinstance(t, tuple):
            a, l = t
            nd = a.ndim - 1
            return pl.BlockSpec((None,) + a.shape[1:], lambda i, l=l, nd=nd: (l,) + (0,) * nd)
        nd = t.ndim
        return pl.BlockSpec(t.shape, lambda i, nd=nd: (0,) * nd)

    def ospec(o):
        return pl.BlockSpec((tm, o.shape[1]), lambda i: (i, 0))

    def aspec(o):
        nd = len(o.shape)
        return pl.BlockSpec(o.shape, lambda i, nd=nd: (0,) * nd)

    outs = pl.pallas_call(
        body, name=name, grid=(S // tm,),
        in_specs=[tspec(t) for t in tiled] + [fspec(t) for t in full] + [ANY] * len(deps),
        out_specs=[ospec(o) for o in out_tiled] + [aspec(o) for o in out_acc],
        out_shape=list(out_tiled) + list(out_acc),
        scratch_shapes=list(scratch),
        compiler_params=_cp(("arbitrary",)),
    )(*[arr(t) for t in tiled], *[arr(t) for t in full], *deps)
    return outs


def _sds(shape, dtype=F32):
    return jax.ShapeDtypeStruct(tuple(shape), dtype)


def _acc(ref, val):
    @pl.when(pl.program_id(0) == 0)
    def _():
        ref[...] = val

    @pl.when(pl.program_id(0) != 0)
    def _():
        ref[...] += val


def mm_in(x, w, nb16=0):
    S, N = x.shape[0], w.shape[1]

    def body(x_ref, w_ref, h_ref, xb_ref, *hb_ref):
        xb = x_ref[...].astype(BF16)
        xb_ref[...] = xb
        h = _dot(xb, w_ref[...])
        h_ref[...] = h
        if nb16:
            hb_ref[0][...] = h[:, 0:nb16].astype(BF16)

    outs = [_sds((S, N)), _sds((S, x.shape[1]), BF16)] + ([_sds((S, nb16), BF16)] if nb16 else [])
    return _tok_call("mm_in", body, [x], [w], outs, tm=512)


def _stack_heads(x, hm0, dtype=BF16):
    return jnp.concatenate([jnp.where(hm0, x, 0), jnp.where(hm0, 0, x)], axis=0).astype(dtype)


def _unstack_k(x, T):
    return jnp.concatenate([x[0:T], x[T:2 * T]], axis=1)


def _cumsum_mm(x, u):
    n = x.shape[0]
    hi, lo = _split2(x)
    r = _dot(jnp.concatenate([hi, lo], axis=0), u)
    return r[0:n] + r[n:2 * n]


def attn_fwd(qkv, T=256):
    S = qkv.shape[0]
    T = min(T, S)
    nq = S // T

    def body(q_ref, k_ref, v_ref, o_ref, t_ref, acc_ref, c_ref, qh_ref):
        i = pl.program_id(0)
        hm0 = lax.broadcasted_iota(jnp.int32, (1, 128), 1) < 64
        r2 = lax.broadcasted_iota(jnp.int32, (2 * T, T), 0)
        c2 = lax.broadcasted_iota(jnp.int32, (2 * T, T), 1)
        causal = c2 < jnp.where(r2 >= T, r2 - T, r2)
        ur = lax.broadcasted_iota(jnp.int32, (T, T), 0)
        uc = lax.broadcasted_iota(jnp.int32, (T, T), 1)
        u_incl = (ur >= uc).astype(BF16)
        acc_ref[...] = jnp.zeros_like(acc_ref)
        c_ref[...] = jnp.zeros_like(c_ref)
        for pp in range(4):
            qh_ref[pp] = _stack_heads(q_ref[:, pp * 128:(pp + 1) * 128] * QK_SCALE, hm0)

        def block(kb, diag):
            ks = pl.multiple_of(kb * T, T)
            cols = [slice(pp * 128, (pp + 1) * 128) for pp in range(4)]
            zs = [_dot_nt(qh_ref[pp], k_ref[pl.ds(ks, T), cols[pp]]) for pp in range(4)]
            incls = []
            for pp in range(4):
                sp = _softplus(zs[pp])
                if diag:
                    sp = jnp.where(causal, sp, 0.0)
                incls.append(_cumsum_mm(sp, u_incl))
            for pp in range(4):
                c = c_ref[pp]
                w = jnp.exp(zs[pp] - incls[pp] - c)
                if diag:
                    w = jnp.where(causal, w, 0.0)
                acc_ref[:, cols[pp]] += _dot(_unstack_k(w.astype(BF16), T),
                                             _stack_heads(v_ref[pl.ds(ks, T), cols[pp]], hm0))
                c_ref[pp] = c + jnp.broadcast_to(incls[pp][:, 0:1], (2 * T, T))

        block(i, True)

        def step(jj, carry):
            block(i - 1 - jj, False)
            return carry

        lax.fori_loop(0, i, step, 0)
        o_ref[...] = acc_ref[...].astype(BF16)
        for pp in range(4):
            for hd in range(2):
                t_ref[2 * pp + hd] = c_ref[pp, hd * T:(hd + 1) * T, 0:128]

    return pl.pallas_call(
        body, name="attn_fwd", grid=(nq,),
        in_specs=[pl.BlockSpec((T, 512), lambda i: (i, 0)),
                  pl.BlockSpec((S, 512), lambda i: (0, 1)),
                  pl.BlockSpec((S, 512), lambda i: (0, 2))],
        out_specs=[pl.BlockSpec((T, 512), lambda i: (i, 0)),
                   pl.BlockSpec((8, T, 128), lambda i: (0, i, 0))],
        out_shape=[_sds((S, 512), BF16), _sds((8, S, 128))],
        scratch_shapes=[pltpu.VMEM((T, 512), F32), pltpu.VMEM((4, 2 * T, T), F32), pltpu.VMEM((4, 2 * T, 128), BF16)],
        compiler_params=_cp(("arbitrary",)),
    )(qkv, qkv, qkv)


def pool_fwd(h, pool_w, pool_scale, CH=256):
    S = h.shape[0]
    CH = min(CH, S)

    def body(u_ref, w_ref, sc_ref, b_ref, pooled_ref, pad_ref):
        pad_ref[0:16, :] = jnp.zeros((16, 512), F32)
        pad_ref[16:16 + S, :] = u_ref[...]
        for g, win in enumerate(POOL_WINDOWS):
            cs = slice(g * 128, (g + 1) * 128)
            wq = w_ref[g].astype(BF16)
            for ch in range(S // CH):
                base = ch * CH
                acc = pad_ref[16 + base:16 + base + CH, cs]
                for sft in range(1, win):
                    acc = acc + pad_ref[16 + base - sft:16 + base - sft + CH, cs]
                t = base + lax.broadcasted_iota(jnp.int32, (CH, 1), 0)
                cnt = jnp.minimum(t + 1, win).astype(F32)
                pooled = (acc / cnt - pad_ref[16 + base:16 + base + CH, cs]).astype(BF16)
                pooled_ref[base:base + CH, cs] = pooled
                b_ref[base:base + CH, cs] = (_dot(pooled, wq) * sc_ref[:, cs]).astype(BF16)

    return pl.pallas_call(
        body, name="pool_fwd", grid=(1,),
        in_specs=[pl.BlockSpec((S, 512), lambda i: (0, 3)),
                  pl.BlockSpec((4, 128, 128), lambda i: (0, 0, 0)),
                  pl.BlockSpec((1, 512), lambda i: (0, 0))],
        out_specs=[pl.BlockSpec((S, 512), lambda i: (0, 0)), pl.BlockSpec((S, 512), lambda i: (0, 0))],
        out_shape=[_sds((S, 512), BF16), _sds((S, 512), BF16)],
        scratch_shapes=[pltpu.VMEM((S + 16, 512), F32)],
        compiler_params=_cp(("arbitrary",)),
    )(h, pool_w, pool_scale)


def conv_fwd(h, dw, CH=128):
    S = h.shape[0]

    def body(a_ref, g_ref, dw_ref, y_ref, hc_ref, pad_ref):
        hc = a_ref[...] * _sigmoid(g_ref[...])
        hc_ref[...] = hc
        pad_ref[0:32, :] = jnp.zeros((32, 128), F32)
        pad_ref[32:32 + S, :] = hc
        for ch in range(S // CH):
            base = ch * CH + 2
            acc = dw_ref[0:1, :] * pad_ref[base:base + CH, :]
            for k in range(1, CONV_TAPS):
                acc = acc + dw_ref[k:k + 1, :] * pad_ref[base + k:base + k + CH, :]
            y_ref[ch * CH:(ch + 1) * CH, :] = acc

    return pl.pallas_call(
        body, name="conv_fwd", grid=(4,),
        in_specs=[pl.BlockSpec((S, 128), lambda c: (0, c)),
                  pl.BlockSpec((S, 128), lambda c: (0, 4 + c)),
                  pl.BlockSpec((CONV_TAPS, 128), lambda c: (0, c))],
        out_specs=[pl.BlockSpec((S, 128), lambda c: (0, c)), pl.BlockSpec((S, 128), lambda c: (0, c))],
        out_shape=[_sds((S, 512)), _sds((S, 512))],
        scratch_shapes=[pltpu.VMEM((S + 32, 128), F32)],
        compiler_params=_cp(("arbitrary",)),
    )(h, h, dw)


def _masked_sg_w(w_ref, g):
    row = lax.broadcasted_iota(jnp.int32, (128, 128), 0)
    col = lax.broadcasted_iota(jnp.int32, (128, 128), 1)
    return jnp.where(row >= col, w_ref[g], 0.0).astype(BF16)


def odd_post(y, h, cl_g, cl_b, sl_g, sl_b, sg_w, sgb_bc, tm=256):
    S = y.shape[0]
    tm = min(tm, S)

    def body(y_ref, zc_ref, clg, clb, slg, slb, w_ref, sb_ref,
             c_ref, d_ref, xhc_ref, rsc_ref, xhv_ref, rsv_ref, sv_ref):
        lnc, xhc, rsc = _ln_fwd(y_ref[...], clg[...], clb[...])
        c_ref[...] = (lnc * _sigmoid(lnc)).astype(BF16)
        xhc_ref[...] = xhc
        rsc_ref[...] = rsc
        z = _gelu(zc_ref[...])
        vn, xhv, rsv = _ln_fwd(z[:, 512:], slg[...], slb[...])
        xhv_ref[...] = xhv
        rsv_ref[...] = rsv
        vnb = vn.astype(BF16)
        for g in range(4):
            wm = _masked_sg_w(w_ref, g)
            for ch in range(tm // 128):
                rs, cs = slice(ch * 128, (ch + 1) * 128), slice(g * 128, (g + 1) * 128)
                sv_ref[rs, cs] = _dot(wm, vnb[rs, cs]) + sb_ref[g]
        d_ref[...] = (z[:, :512] * sv_ref[...]).astype(BF16)

    return _tok_call(
        "odd_post", body, [y, (h, 1024, 1)], [cl_g, cl_b, sl_g, sl_b, sg_w, sgb_bc],
        [_sds((S, 512), BF16), _sds((S, 512), BF16), _sds((S, 512)), _sds((S, 1)),
         _sds((S, 512)), _sds((S, 1)), _sds((S, 512))], tm=tm)


def mm_out_ln(l1, l2, x, w, g, b, dep=None):
    S, D = x.shape

    def body(l1_ref, l2_ref, x_ref, w_ref, g_ref, b_ref, y_ref, xh_ref, rs_ref):
        mix = _dot(l1_ref[...], w_ref[0:512, :]) + _dot(l2_ref[...], w_ref[512:1024, :])
        y, xh, rs = _ln_fwd(ALPHA * x_ref[...] + mix, g_ref[...], b_ref[...])
        y_ref[...] = y
        xh_ref[...] = xh
        rs_ref[...] = rs

    return _tok_call("mm_out_ln", body, [l1, l2, x], [w, g, b],
                     [_sds((S, D)), _sds((S, D)), _sds((S, 1))], dep=dep)


def ffn_up(x1, wg, wu, layer, dep=None):
    S, D = x1.shape
    F = wg.shape[-1]

    def body(x_ref, wg_ref, wu_ref, gate_ref, up_ref, hb_ref, xb_ref):
        xb = x_ref[...].astype(BF16)
        xb_ref[...] = xb
        gate = _dot(xb, wg_ref[...])
        up = _dot(xb, wu_ref[...])
        gate_ref[...] = gate.astype(BF16)
        up_ref[...] = up.astype(BF16)
        hb_ref[...] = (gate * _sigmoid(gate) * up).astype(BF16)

    return _tok_call("ffn_up", body, [x1], [(wg, layer), (wu, layer)],
                     [_sds((S, F), BF16), _sds((S, F), BF16), _sds((S, F), BF16), _sds((S, D), BF16)], dep=dep)


def ffn_down_ln(hb, x1, wd, layer, g, b):
    S, D = x1.shape

    def body(h_ref, x_ref, w_ref, g_ref, b_ref, y_ref, xh_ref, rs_ref):
        f = _dot(h_ref[...], w_ref[...])
        y, xh, rs = _ln_fwd(ALPHA * x_ref[...] + f, g_ref[...], b_ref[...])
        y_ref[...] = y
        xh_ref[...] = xh
        rs_ref[...] = rs

    return _tok_call("ffn_down_ln", body, [hb, x1], [(wd, layer), g, b],
                     [_sds((S, D)), _sds((S, D)), _sds((S, 1))])


def ple_fwd(x2, p, wpg, wpp, layer, bg, target=None, dep=None):
    S, D = x2.shape
    last = target is not None

    def body(*refs):
        if last:
            x_ref, p_ref, t_ref, wg_ref, wp_ref, b_ref, x3_ref, sg_ref, pp_ref, xb_ref, pb_ref, dy_ref, ls_ref = refs
        else:
            x_ref, p_ref, wg_ref, wp_ref, b_ref, x3_ref, sg_ref, pp_ref, xb_ref, pb_ref = refs
        x = x_ref[...]
        xb = x.astype(BF16)
        pb = p_ref[...].astype(BF16)
        xb_ref[...] = xb
        pb_ref[...] = pb
        sg = _sigmoid(_dot(xb, wg_ref[...]) + b_ref[...])
        pp = _dot(pb, wp_ref[...])
        sg_ref[...] = sg.astype(BF16)
        pp_ref[...] = pp.astype(BF16)
        x3 = x + sg * pp
        x3_ref[...] = x3
        if last:
            err = x3 - t_ref[...]
            dy_ref[...] = err * (1.0 / D)
            _acc(ls_ref, _colsum(err * err))

    outs = [_sds((S, D)), _sds((S, D), BF16), _sds((S, D), BF16), _sds((S, D), BF16), _sds((S, p.shape[1]), BF16)]
    tiled = [x2, p] + ([target] if last else [])
    if last:
        outs.append(_sds((S, D)))
    return _tok_call("ple_fwd", body, tiled, [(wpg, layer), (wpp, layer), bg], outs,
                     [_sds((1, D))] if last else [], dep=dep)


def ple_ln_bwd(dx3, sg, pp, x2b, pb, wpg, xh, rs, g, dep=None):
    S, D = dx3.shape

    def body(d_ref, sg_ref, pp_ref, x2b_ref, pb_ref, xh_ref, rs_ref, w_ref, g_ref,
             dr_ref, drb_ref, dbg_ref, dlg_ref, dlb_ref, dwg_ref, dwp_ref, accg_ref, accp_ref):
        fin_g = _sum_steps(accg_ref, dwg_ref)
        fin_p = _sum_steps(accp_ref, dwp_ref)
        d, sg = d_ref[...], sg_ref[...].astype(F32)
        dgp = d * pp_ref[...].astype(F32) * sg * (1.0 - sg)
        dgpb = dgp.astype(BF16)
        accg_ref[...] += _dot_tn(x2b_ref[...], dgpb)
        accp_ref[...] += _dot_tn(pb_ref[...], (d * sg).astype(BF16))
        _acc(dbg_ref, _colsum(dgp))
        dx2 = d + _dot_nt(dgpb, w_ref[...])
        xh = xh_ref[...]
        dr = _ln_bwd(dx2, xh, rs_ref[...], g_ref[...])
        dr_ref[...] = dr
        drb_ref[...] = dr.astype(BF16)
        _acc(dlg_ref, _colsum(dx2 * xh))
        _acc(dlb_ref, _colsum(dx2))
        fin_g()
        fin_p()

    P = pb.shape[1]
    return _tok_call("ple_ln_bwd", body, [dx3, sg, pp, x2b, pb, xh, rs], [wpg, g],
                     [_sds((S, D)), _sds((S, D), BF16)],
                     [_sds((1, D)), _sds((1, D)), _sds((1, D)), _sds((D, D), BF16), _sds((P, D), BF16)],
                     scratch=[pltpu.VMEM((D, D), F32), pltpu.VMEM((P, D), F32)], dep=dep, tm=ACC_TM)


def ffn_bwd(dr_b, x1b, gate, up, hb, wg, wu, wd, TH=256):
    S, D = dr_b.shape
    F = gate.shape[1]

    def body(dr_hbm, x_hbm, gate_ref, up_ref, hb_ref, wg_ref, wu_ref, wd_ref,
             dx_ref, dwg_ref, dwu_ref, dwd_ref, dr_v, x_v, sem, dg_s, du_s):
        @pl.when(pl.program_id(0) == 0)
        def _():
            c1 = pltpu.make_async_copy(dr_hbm, dr_v, sem.at[0])
            c2 = pltpu.make_async_copy(x_hbm, x_v, sem.at[1])
            c1.start()
            c2.start()
            c1.wait()
            c2.wait()
            dx_ref[...] = jnp.zeros_like(dx_ref)

        for ch in range(S // CH):
            rows = slice(ch * CH, (ch + 1) * CH)
            dh = _dot_nt(dr_v[rows, :], wd_ref[...])
            g, u = gate_ref[rows, :].astype(F32), up_ref[rows, :].astype(F32)
            s = _sigmoid(g)
            dgb = (dh * u * s * (1.0 + g * (1.0 - s))).astype(BF16)
            dub = (dh * g * s).astype(BF16)
            dg_s[rows, :] = dgb
            du_s[rows, :] = dub
            dx_ref[rows, :] += _dot_nt(dgb, wg_ref[...]) + _dot_nt(dub, wu_ref[...])
        x = x_v[...]
        dwg_ref[...] = _dot_tn(x, dg_s[...]).astype(BF16)
        dwu_ref[...] = _dot_tn(x, du_s[...]).astype(BF16)
        dwd_ref[...] = _dot_tn(hb_ref[...], dr_v[...]).astype(BF16)

    CH = min(512, S)
    col = lambda rows: pl.BlockSpec((rows, TH), lambda j: (0, j))
    row = pl.BlockSpec((TH, D), lambda j: (j, 0))
    return pl.pallas_call(
        body, name="ffn_bwd", grid=(F // TH,),
        in_specs=[ANY, ANY, col(S), col(S), col(S), col(D), col(D), row],
        out_specs=[pl.BlockSpec((S, D), lambda j: (0, 0)), col(D), col(D), row],
        out_shape=[_sds((S, D)), _sds((D, F), BF16), _sds((D, F), BF16), _sds((F, D), BF16)],
        scratch_shapes=[pltpu.VMEM((S, D), BF16), pltpu.VMEM((S, D), BF16), pltpu.SemaphoreType.DMA((2,)),
                        pltpu.VMEM((S, TH), BF16), pltpu.VMEM((S, TH), BF16)],
        compiler_params=pltpu.CompilerParams(dimension_semantics=("arbitrary",), vmem_limit_bytes=60 * 1024 * 1024),
    )(dr_b, x1b, gate, up, hb, wg, wu, wd)


ACC_TM = 512


def _sum_steps(acc_ref, out_ref):
    @pl.when(pl.program_id(0) == 0)
    def _():
        acc_ref[...] = jnp.zeros_like(acc_ref)

    def finish():
        @pl.when(pl.program_id(0) == pl.num_programs(0) - 1)
        def _():
            out_ref[...] = acc_ref[...].astype(BF16)
    return finish


def mix_bwd(dxp, dr2, xh, rs, l1, l2, g, w, dep=None):
    S, D = dxp.shape
    K1 = l1.shape[1]

    def body(dxp_ref, dr2_ref, xh_ref, rs_ref, l1_ref, l2_ref, g_ref, w_ref,
             dr_ref, dl_ref, dlg_ref, dlb_ref, dw_ref, acc_ref):
        finish = _sum_steps(acc_ref, dw_ref)
        d, xh = ALPHA * dr2_ref[...] + dxp_ref[...], xh_ref[...]
        dr = _ln_bwd(d, xh, rs_ref[...], g_ref[...])
        drb = dr.astype(BF16)
        dr_ref[...] = dr
        dl_ref[...] = _dot_nt(drb, w_ref[...])
        _acc(dlg_ref, _colsum(d * xh))
        _acc(dlb_ref, _colsum(d))
        acc_ref[0:K1, :] += _dot_tn(l1_ref[...], drb)
        acc_ref[K1:, :] += _dot_tn(l2_ref[...], drb)
        finish()

    return _tok_call("mix_bwd", body, [dxp, dr2, xh, rs, l1, l2], [g, w],
                     [_sds((S, D)), _sds((S, D))], [_sds((1, D)), _sds((1, D)), _sds(w.shape, BF16)],
                     scratch=[pltpu.VMEM(w.shape, F32)], dep=dep, tm=ACC_TM)


def dx_in(dr, pieces, w, xb):
    S, D = dr.shape
    offs = [o for _, o in pieces]
    widths = [a.shape[1] for a, _ in pieces]
    npc = len(pieces)

    def body(*refs):
        dr_ref, prefs, xb_ref, w_ref = refs[0], refs[1:1 + npc], refs[1 + npc], refs[2 + npc]
        dx_ref, dw_ref, acc_ref = refs[3 + npc:]
        finish = _sum_steps(acc_ref, dw_ref)
        acc = ALPHA * dr_ref[...]
        xb_t = xb_ref[...]
        for pr, o, n in zip(prefs, offs, widths):
            piece = pr[...]
            acc = acc + _dot_nt(piece, w_ref[:, o:o + n])
            acc_ref[:, o:o + n] += _dot_tn(xb_t, piece)
        dx_ref[...] = acc
        finish()

    return _tok_call("dx_in", body, [dr] + [a for a, _ in pieces] + [xb], [w], [_sds((S, D))],
                     [_sds(w.shape, BF16)], scratch=[pltpu.VMEM(w.shape, F32)], tm=ACC_TM)


def odd_post_bwd(dl, h, xhc, rsc, xhv, rsv, sv, cl_g, cl_b, sl_g, sl_b, sg_w, tm=256, dep=None):
    S = dl.shape[0]
    tm = min(tm, S)

    def body(dl_ref, zc_ref, xhc_ref, rsc_ref, xhv_ref, rsv_ref, sv_ref, clg, clb, slg, slb, w_ref,
             dy_ref, dzc_ref, dclg_ref, dclb_ref, dslg_ref, dslb_ref, dwm_ref, dsb_ref, dvn_ref):
        first = pl.program_id(0) == 0
        last = pl.program_id(0) == pl.num_programs(0) - 1
        dc, dd = dl_ref[:, 0:512], dl_ref[:, 512:1024]
        xhc = xhc_ref[...]
        lnc = xhc * clg[...] + clb[...]
        s = _sigmoid(lnc)
        dlnc = dc * s * (1.0 + lnc * (1.0 - s))
        dy_ref[...] = _ln_bwd(dlnc, xhc, rsc_ref[...], clg[...])
        _acc(dclg_ref, _colsum(dlnc * xhc))
        _acc(dclb_ref, _colsum(dlnc))
        zc = zc_ref[...]
        z = _gelu(zc)
        dsv = dd * z[:, :512]
        dsvb = dsv.astype(BF16)
        xhv = xhv_ref[...]
        vnb = (xhv * slg[...] + slb[...]).astype(BF16)

        @pl.when(first)
        def _():
            dwm_ref[...] = jnp.zeros_like(dwm_ref)
            dsb_ref[...] = jnp.zeros_like(dsb_ref)

        for g in range(4):
            wm = _masked_sg_w(w_ref, g)
            for ch in range(tm // 128):
                rs_, cs = slice(ch * 128, (ch + 1) * 128), slice(g * 128, (g + 1) * 128)
                dwm_ref[g] += _dot_nt(dsvb[rs_, cs], vnb[rs_, cs])
                dvn_ref[rs_, cs] = _dot_tn(wm, dsvb[rs_, cs])
                dsb_ref[g] += dsv[rs_, cs]
        dvn = dvn_ref[...]
        dvv = _ln_bwd(dvn, xhv, rsv_ref[...], slg[...])
        _acc(dslg_ref, _colsum(dvn * xhv))
        _acc(dslb_ref, _colsum(dvn))
        gg = _gelu_grad(zc)
        dzc_ref[:, 0:512] = (dd * sv_ref[...] * gg[:, :512]).astype(BF16)
        dzc_ref[:, 512:1024] = (dvv * gg[:, 512:]).astype(BF16)

        @pl.when(last)
        def _():
            row = lax.broadcasted_iota(jnp.int32, (128, 128), 0)
            col = lax.broadcasted_iota(jnp.int32, (128, 128), 1)
            for g in range(4):
                dwm_ref[g] = jnp.where(row >= col, dwm_ref[g], 0.0)
                dsb_ref[g] = jnp.broadcast_to(jnp.sum(dsb_ref[g], axis=1, keepdims=True), (128, 128))

    return _tok_call(
        "odd_post_bwd", body, [dl, (h, 1024, 1), xhc, rsc, xhv, rsv, sv], [cl_g, cl_b, sl_g, sl_b, sg_w],
        [_sds((S, 512)), _sds((S, 1024), BF16)],
        [_sds((1, 512)), _sds((1, 512)), _sds((1, 512)), _sds((1, 512)), _sds((4, 128, 128)), _sds((4, 128, 128))],
        tm=tm, scratch=[pltpu.VMEM((tm, 512), F32)], dep=dep)


def conv_bwd(dy, hc, h, dw, CH=128):
    S = dy.shape[0]

    def body(dy_ref, hc_ref, a_ref, g_ref, dw_ref, da_ref, dg_ref, ddw_ref, padh_ref, padd_ref, dhc_ref):
        padh_ref[0:32, :] = jnp.zeros((32, 128), F32)
        padh_ref[32:32 + S, :] = hc_ref[...]
        padd_ref[0:S, :] = dy_ref[...]
        padd_ref[S:S + 32, :] = jnp.zeros((32, 128), F32)
        taps = [jnp.zeros((1, 128), F32) for _ in range(CONV_TAPS)]
        for ch in range(S // CH):
            b0 = ch * CH
            dyc = padd_ref[b0:b0 + CH, :]
            acc = dw_ref[0:1, :] * padd_ref[b0 + 30:b0 + 30 + CH, :]
            taps[0] = taps[0] + _colsum(dyc * padh_ref[b0 + 2:b0 + 2 + CH, :])
            for k in range(1, CONV_TAPS):
                acc = acc + dw_ref[k:k + 1, :] * padd_ref[b0 + 30 - k:b0 + 30 - k + CH, :]
                taps[k] = taps[k] + _colsum(dyc * padh_ref[b0 + 2 + k:b0 + 2 + k + CH, :])
            dhc_ref[b0:b0 + CH, :] = acc
        for k in range(CONV_TAPS):
            ddw_ref[k:k + 1, :] = taps[k]
        dhc = dhc_ref[...]
        s = _sigmoid(g_ref[...])
        da_ref[...] = (dhc * s).astype(BF16)
        dg_ref[...] = (dhc * a_ref[...] * s * (1.0 - s)).astype(BF16)

    return pl.pallas_call(
        body, name="conv_bwd", grid=(4,),
        in_specs=[pl.BlockSpec((S, 128), lambda c: (0, c)),
                  pl.BlockSpec((S, 128), lambda c: (0, c)),
                  pl.BlockSpec((S, 128), lambda c: (0, c)),
                  pl.BlockSpec((S, 128), lambda c: (0, 4 + c)),
                  pl.BlockSpec((CONV_TAPS, 128), lambda c: (0, c))],
        out_specs=[pl.BlockSpec((S, 128), lambda c: (0, c)), pl.BlockSpec((S, 128), lambda c: (0, c)),
                   pl.BlockSpec((CONV_TAPS, 128), lambda c: (0, c))],
        out_shape=[_sds((S, 512), BF16), _sds((S, 512), BF16), _sds((CONV_TAPS, 512))],
        scratch_shapes=[pltpu.VMEM((S + 32, 128), F32), pltpu.VMEM((S + 32, 128), F32), pltpu.VMEM((S, 128), F32)],
        compiler_params=_cp(("arbitrary",)),
    )(dy, hc, h, h, dw)


def attn_bwd(qkv, dl, tb, T=256, dep=None):
    S = qkv.shape[0]
    T = min(T, S)
    nq = S // T

    def body(q_ref, k_ref, v_ref, do_ref, t_ref, dq_ref, dk_ref, dv_ref,
             dka_ref, dva_ref, dqa_ref, pc_ref, gc_ref, qh_ref, doh_ref):
        i = pl.program_id(0)
        hm0 = lax.broadcasted_iota(jnp.int32, (1, 128), 1) < 64
        r2 = lax.broadcasted_iota(jnp.int32, (2 * T, T), 0)
        c2 = lax.broadcasted_iota(jnp.int32, (2 * T, T), 1)
        causal = c2 < jnp.where(r2 >= T, r2 - T, r2)
        ur = lax.broadcasted_iota(jnp.int32, (T, T), 0)
        uc = lax.broadcasted_iota(jnp.int32, (T, T), 1)
        u_le = (ur <= uc).astype(BF16)
        u_lt = (ur < uc).astype(BF16)

        @pl.when(i == 0)
        def _():
            dka_ref[...] = jnp.zeros_like(dka_ref)
            dva_ref[...] = jnp.zeros_like(dva_ref)

        dqa_ref[...] = jnp.zeros_like(dqa_ref)
        gc_ref[...] = jnp.zeros_like(gc_ref)
        for pp in range(4):
            cs = slice(pp * 128, (pp + 1) * 128)
            qh_ref[pp] = _stack_heads(q_ref[:, cs] * QK_SCALE, hm0)
            doh_ref[pp] = _stack_heads(do_ref[:, cs], hm0)
            for hd in range(2):
                for half in range(T // 128):
                    pc_ref[pp, hd * T:(hd + 1) * T, half * 128:(half + 1) * 128] = t_ref[2 * pp + hd]

        def block(kb, diag):
            ks = pl.multiple_of(kb * T, T)
            cols = [slice(pp * 128, (pp + 1) * 128) for pp in range(4)]
            zs = [_dot_nt(qh_ref[pp], k_ref[pl.ds(ks, T), cols[pp]]) for pp in range(4)]
            dws = [_dot_nt(doh_ref[pp], v_ref[pl.ds(ks, T), cols[pp]]) for pp in range(4)]
            a_s, pres = [], []
            for pp in range(4):
                sp = _softplus(zs[pp])
                a_s.append(zs[pp] - sp)
                if diag:
                    sp = jnp.where(causal, sp, 0.0)
                pres.append(_cumsum_mm(sp, u_le))
            ws, gmats, gsums = [], [], []
            for pp in range(4):
                rem = pc_ref[pp]
                w = jnp.exp(a_s[pp] - rem + pres[pp])
                if diag:
                    w = jnp.where(causal, w, 0.0)
                gmat = dws[pp] * w
                ws.append(w.astype(BF16))
                gmats.append(gmat)
                gsums.append(_cumsum_mm(gmat, u_lt))
                pc_ref[pp] = rem - jnp.broadcast_to(pres[pp][:, T - 1:T], (2 * T, T))
            for pp in range(4):
                cs = cols[pp]
                sig = jnp.exp(a_s[pp])
                gex = gc_ref[pp] + gsums[pp]
                dz = gmats[pp] * (1.0 - sig) - sig * gex
                if diag:
                    dz = jnp.where(causal, dz, 0.0)
                dzb = dz.astype(BF16)
                dqa_ref[:, cs] += _dot(_unstack_k(dzb, T), _stack_heads(k_ref[pl.ds(ks, T), cs], hm0))
                dka_ref[pl.ds(ks, T), cs] += _dot_tn(dzb, qh_ref[pp])
                dva_ref[pl.ds(ks, T), cs] += _dot_tn(ws[pp], doh_ref[pp])
                gc_ref[pp] = jnp.broadcast_to(gex[:, T - 1:T] + gmats[pp][:, T - 1:T], (2 * T, T))

        def step(kb, carry):
            block(kb, False)
            return carry

        lax.fori_loop(0, i, step, 0)
        block(i, True)
        dq_ref[...] = (dqa_ref[...] * QK_SCALE).astype(BF16)

        @pl.when(i == nq - 1)
        def _():
            dk_ref[...] = dka_ref[...].astype(BF16)
            dv_ref[...] = dva_ref[...].astype(BF16)

    deps = [] if dep is None else [dep]
    call_body = body if dep is None else (lambda *refs: body(*refs[:5], *refs[6:]))
    return pl.pallas_call(
        call_body, name="attn_bwd", grid=(nq,),
        in_specs=[pl.BlockSpec((T, 512), lambda i: (i, 0)),
                  pl.BlockSpec((S, 512), lambda i: (0, 1)),
                  pl.BlockSpec((S, 512), lambda i: (0, 2)),
                  pl.BlockSpec((T, 512), lambda i: (i, 0)),
                  pl.BlockSpec((8, T, 128), lambda i: (0, i, 0))] + [ANY] * len(deps),
        out_specs=[pl.BlockSpec((T, 512), lambda i: (i, 0)),
                   pl.BlockSpec((S, 512), lambda i: (0, 0)),
                   pl.BlockSpec((S, 512), lambda i: (0, 0))],
        out_shape=[_sds((S, 512), BF16), _sds((S, 512), BF16), _sds((S, 512), BF16)],
        scratch_shapes=[pltpu.VMEM((S, 512), F32), pltpu.VMEM((S, 512), F32), pltpu.VMEM((T, 512), F32),
                        pltpu.VMEM((4, 2 * T, T), F32), pltpu.VMEM((4, 2 * T, T), F32),
                        pltpu.VMEM((4, 2 * T, 128), BF16), pltpu.VMEM((4, 2 * T, 128), BF16)],
        compiler_params=_cp(("arbitrary",)),
    )(qkv, qkv, qkv, dl, tb, *deps)


def pool_bwd(dl, pooled_b, pool_w, pool_scale, CH=256):
    S = dl.shape[0]
    CH = min(CH, S)

    def body(db_ref, pooled_ref, w_ref, sc_ref, du_ref, dw_ref, dsc_ref, pad_ref, dp_ref):
        pad_ref[S:S + 16, :] = jnp.zeros((16, 128), F32)
        for g, win in enumerate(POOL_WINDOWS):
            cs = slice(g * 128, (g + 1) * 128)
            wq = w_ref[g].astype(BF16)
            dwg = jnp.zeros((128, 128), F32)
            dsc = jnp.zeros((1, 128), F32)
            for ch in range(S // CH):
                rs_ = slice(ch * CH, (ch + 1) * CH)
                db = db_ref[rs_, cs]
                pb = pooled_ref[rs_, cs]
                dsc = dsc + _colsum(db * _dot(pb, wq))
                dmsb = (db * sc_ref[:, cs]).astype(BF16)
                dwg = dwg + _dot_tn(pb, dmsb)
                dpool = _dot_nt(dmsb, wq)
                t = ch * CH + lax.broadcasted_iota(jnp.int32, (CH, 1), 0)
                cnt = jnp.minimum(t + 1, win).astype(F32)
                dp_ref[rs_, :] = dpool
                pad_ref[rs_, :] = dpool / cnt
            dw_ref[g] = dwg
            dsc_ref[:, cs] = dsc
            for ch in range(S // CH):
                base = ch * CH
                acc = pad_ref[base:base + CH, :]
                for sft in range(1, win):
                    acc = acc + pad_ref[base + sft:base + sft + CH, :]
                du_ref[base:base + CH, cs] = (acc - dp_ref[base:base + CH, :]).astype(BF16)

    return pl.pallas_call(
        body, name="pool_bwd", grid=(1,),
        in_specs=[pl.BlockSpec((S, 512), lambda i: (0, 1)),
                  pl.BlockSpec((S, 512), lambda i: (0, 0)),
                  pl.BlockSpec((4, 128, 128), lambda i: (0, 0, 0)),
                  pl.BlockSpec((1, 512), lambda i: (0, 0))],
        out_specs=[pl.BlockSpec((S, 512), lambda i: (0, 0)),
                   pl.BlockSpec((4, 128, 128), lambda i: (0, 0, 0)),
                   pl.BlockSpec((1, 512), lambda i: (0, 0))],
        out_shape=[_sds((S, 512), BF16), _sds((4, 128, 128)), _sds((1, 512))],
        scratch_shapes=[pltpu.VMEM((S + 16, 128), F32), pltpu.VMEM((S, 128), F32)],
        compiler_params=_cp(("arbitrary",)),
    )(dl, pooled_b, pool_w, pool_scale)


def _row(a, i):
    return a[i:i + 1]


MIXER_NAMES = (("even_w_in", "even_w_out"), ("odd_w_in", "odd_w_out"))


def fwd_layer(i, xin, p_i, target, comm):
    s = {}
    W = comm.weights(("mix", i), xin)
    w_in = W[MIXER_NAMES[i][0]]
    if i == 0:
        s["h"], s["xb"], s["qkv"] = mm_in(xin, w_in, nb16=1536)
        comm.poke(("in", i), s["h"])
        s["l1"], s["tb"] = attn_fwd(s["qkv"])
        s["l2"], s["pooled"] = pool_fwd(s["h"], W["pool_w"], W["pool_scale"])
    else:
        s["h"], s["xb"] = mm_in(xin, w_in)
        comm.poke(("in", i), s["h"])
        s["y"], s["hc"] = conv_fwd(s["h"], W["conv_dw"])
        sgb_bc = jnp.broadcast_to(W["sg_b"][:, :, None], (4, 128, 128))
        (s["l1"], s["l2"], s["xhc"], s["rsc"], s["xhv"], s["rsv"], s["sv"]) = odd_post(
            s["y"], s["h"], W["conv_ln_g"], W["conv_ln_b"], W["sg_ln_g"], W["sg_ln_b"], W["sg_w"], sgb_bc)
    tok = comm.poke(("mixed", i), s["l1"])
    W = comm.weights(("out", i), s["l1"])
    x1, s["xh1"], s["rs1"] = mm_out_ln(s["l1"], s["l2"], xin, W[MIXER_NAMES[i][1]], _row(W["ln_mix_g"], i),
                                       _row(W["ln_mix_b"], i), dep=tok)
    W = comm.weights(("ffn", i), x1)
    tok = comm.poke(("up", i), x1)
    s["gate"], s["up"], s["hb"], s["x1b"] = ffn_up(x1, W["ffn_w_gate%d" % i], W["ffn_w_up%d" % i], None, dep=tok)
    W = comm.weights(("down", i), s["hb"])
    x2, s["xh2"], s["rs2"] = ffn_down_ln(s["hb"], x1, W["ffn_w_down%d" % i], None,
                                         _row(W["ln_ffn_g"], i), _row(W["ln_ffn_b"], i))
    tok = comm.poke(("ffn", i), x2)
    outs = ple_fwd(x2, p_i, W["ple_w_gate%d" % i], W["ple_w_proj%d" % i], None, _row(W["ple_b_gate"], i), target,
                   dep=tok)
    s["sg"], s["pp"], s["x2b"], s["pb"] = outs[1:5]
    return outs[0], s, outs[5:]


def bwd_layer(i, dx, s, W, comm, tok=None):
    small = {}
    dr2, dr2_b, small["ple_b_gate"], small["ln_ffn_g"], small["ln_ffn_b"], dwpg, dwpp = ple_ln_bwd(
        dx, s["sg"], s["pp"], s["x2b"], s["pb"], W["ple_w_gate%d" % i], s["xh2"], s["rs2"],
        _row(W["ln_ffn_g"], i), dep=tok)
    dxp, dwg, dwu, dwd = ffn_bwd(dr2_b, s["x1b"], s["gate"], s["up"], s["hb"], W["ffn_w_gate%d" % i],
                                 W["ffn_w_up%d" % i], W["ffn_w_down%d" % i])
    tok = comm.grads({"ple_w_gate%d" % i: dwpg, "ple_w_proj%d" % i: dwpp, "ffn_w_down%d" % i: dwd,
                      "ffn_w_gate%d" % i: dwg, "ffn_w_up%d" % i: dwu})
    iname, oname = MIXER_NAMES[i]
    dr1, dl, small["ln_mix_g"], small["ln_mix_b"], dwout = mix_bwd(
        dxp, dr2, s["xh1"], s["rs1"], s["l1"], s["l2"], _row(W["ln_mix_g"], i), W[oname], dep=tok)
    tok = comm.poke(("bwd", i), dl)
    if i == 1:
        (dy, dzc_b, small["conv_ln_g"], small["conv_ln_b"], small["sg_ln_g"], small["sg_ln_b"],
         small["sg_w"], dsb) = odd_post_bwd(dl, s["h"], s["xhc"], s["rsc"], s["xhv"], s["rsv"], s["sv"],
                                            W["conv_ln_g"], W["conv_ln_b"], W["sg_ln_g"], W["sg_ln_b"], W["sg_w"],
                                            dep=tok)
        small["sg_b"] = dsb[:, :, 0]
        da_b, dg_b, small["conv_dw"] = conv_bwd(dy, s["hc"], s["h"], W["conv_dw"])
        pieces = [(da_b, 0), (dg_b, 512), (dzc_b, 1024)]
    else:
        dq_b, dk_b, dv_b = attn_bwd(s["qkv"], dl, s["tb"], dep=tok)
        du_b, small["pool_w"], small["pool_scale"] = pool_bwd(dl, s["pooled"], W["pool_w"], W["pool_scale"])
        pieces = [(dq_b, 0), (dk_b, 512), (dv_b, 1024), (du_b, 1536)]
    dxin, dwin = dx_in(dr1, pieces, W[iname], s["xb"])
    tok = comm.grads({oname: dwout, iname: dwin})
    return dxin, small, tok


def run_layers(x, p, target, comm):
    saved, xin = [], x
    for i in range(2):
        xin, s, extra = fwd_layer(i, xin, p[i], target if i == 1 else None, comm)
        saved.append(s)
    dx, sq = extra
    W = comm.all_weights()
    per_layer = [None, None]
    tok = None
    for i in (1, 0):
        dx, per_layer[i], tok = bwd_layer(i, dx, saved[i], W, comm, tok)
    small = {}
    for k in ("ln_mix_g", "ln_mix_b", "ln_ffn_g", "ln_ffn_b", "ple_b_gate"):
        small[k] = jnp.concatenate([per_layer[0][k], per_layer[1][k]], axis=0)
    for i in range(2):
        small.update({k: v for k, v in per_layer[i].items() if k not in small})
    return sq, dx, small


def _big_table():
    t = {}
    for nm in ("even", "odd"):
        t[nm + "_w_in"] = ((1024, 2048), 1, 256, 256, nm + "_w_in", 0)
        t[nm + "_w_out"] = ((1024, 1024), 0, 128, 128, nm + "_w_out", 0)
    for l in range(2):
        t["ffn_w_gate%d" % l] = ((1024, 8 * FF_PAD), 1, FF_PAD, FF_SHARD, "ffn_w_gate", l)
        t["ffn_w_up%d" % l] = ((1024, 8 * FF_PAD), 1, FF_PAD, FF_SHARD, "ffn_w_up", l)
        t["ffn_w_down%d" % l] = ((8 * FF_PAD, 1024), 0, FF_PAD, FF_SHARD, "ffn_w_down", l)
        t["ple_w_gate%d" % l] = ((1024, 1024), 0, 128, 128, "ple_w_gate", l)
        t["ple_w_proj%d" % l] = ((256, 1024), 1, 128, 128, "ple_w_proj", l)
    return t


BIG = _big_table()
TRANSPOSED_ARGS = ("ffn_w_gate", "ffn_w_up")
SMALL_SPEC = ((N_DEV, 40, 64), 0, 1, 1)
_UP_GROUP = lambda l: ["ffn_w_gate%d" % l, "ffn_w_up%d" % l]
_DOWN_GROUP = lambda l: ["ffn_w_down%d" % l, "ple_w_gate%d" % l, "ple_w_proj%d" % l]
AG_GROUPS = (["even_w_in"], ["even_w_out"], _UP_GROUP(0), _DOWN_GROUP(0), ["odd_w_in", "odd_w_out", "small"],
             _UP_GROUP(1), _DOWN_GROUP(1))
AG_NEED = {("mix", 0): 0, ("out", 0): 1, ("ffn", 0): 2, ("down", 0): 3, ("mix", 1): 4, ("ffn", 1): 5, ("down", 1): 6}
AG_PASS = {("in", 0): 1, ("mixed", 0): 2, ("up", 0): 3, ("ffn", 0): 4, ("mixed", 1): 5, ("up", 1): 6}
ANY = pl.BlockSpec(memory_space=pl.ANY)
SEM = pl.BlockSpec(memory_space=pltpu.SEMAPHORE)


def _spec(name):
    return SMALL_SPEC if name == "small" else BIG[name]


def _win_shape(spec):
    full, axis, w = spec[:3]
    return tuple(w if d == axis else n for d, n in enumerate(full))


def _window(ref, axis, w, j):
    idx = [slice(None)] * len(ref.shape)
    idx[axis] = pl.ds(j, 1) if w == 1 else pl.ds(pl.multiple_of(j * w, w), w)
    return ref.at[tuple(idx)]


def _mesh_pos():
    return lax.axis_index("x"), lax.axis_index("y"), lax.axis_index("c")


def split_call(name, arrays, starts=(), waits=(), sems_in=(), new=(), after=None):
    n, nn, ns = len(arrays), len(new), len(starts)
    flat_sems = [s for pair in sems_in for s in pair]

    def body(*refs):
        arr = list(refs[:n])
        sin = refs[n:n + len(flat_sems)]
        outs = refs[n + len(flat_sems) + (after is not None):]
        data = arr + list(outs[n:n + nn])
        for p, k, kind, mk in waits:
            d = mk(data, sin[2 * p].at[k], sin[2 * p + 1].at[k])
            d.wait_send() if kind == "send" else d.wait_recv()
        if ns:
            send, recv = outs[n + nn], outs[n + nn + 1]
            for k, mk in enumerate(starts):
                mk(data, send.at[k], recv.at[k]).start()
        outs[-1][...] = jnp.zeros((8, 128), F32)

    sem_out = [pltpu.SemaphoreType.DMA((ns,)), pltpu.SemaphoreType.DMA((ns,))] if ns else []
    res = pl.pallas_call(
        body, name=name,
        in_specs=[ANY] * n + [SEM] * len(flat_sems) + ([ANY] if after is not None else []),
        out_specs=[ANY] * (n + nn) + [SEM] * len(sem_out) + [pl.BlockSpec(memory_space=pltpu.VMEM)],
        out_shape=[_sds(a.shape, a.dtype) for a in arrays] + list(new) + sem_out + [_sds((8, 128), F32)],
        input_output_aliases={a: a for a in range(n)},
        compiler_params=pltpu.CompilerParams(has_side_effects=pltpu.SideEffectType.DATAFLOW_SIDE_EFFECTING),
    )(*arrays, *flat_sems, *([after] if after is not None else []))
    return list(res[:n + nn]), (tuple(res[n + nn:n + nn + 2]) if ns else None), res[-1]


def _remote(src, dst, send_sem, recv_sem, dev):
    return pltpu.make_async_remote_copy(src_ref=src, dst_ref=dst, send_sem=send_sem, recv_sem=recv_sem,
                                        device_id=dev, device_id_type=MESH_T)


class Gatherer:
    def __init__(self, groups, arrays, specs, prefix):
        self.groups, self.specs, self.prefix = groups, specs, prefix
        self.names = [nm for g in groups for nm in g]
        self.arr = dict(zip(self.names, arrays))
        self.fwd_sems = {}
        self.forwarded = set()

    @staticmethod
    def _mk_first(ai, spec, k):
        def mk(refs, ss, rs):
            x, y, c = _mesh_pos()
            dev = [(x, y, 1 - c), (1 - x, y, c), (x, 1 - y, c), (1 - x, 1 - y, c)][k]
            win = _window(refs[ai], spec[1], spec[2], 4 * x + 2 * y + c)
            return _remote(win, win, ss, rs, dev)
        return mk

    @staticmethod
    def _mk_fwd(ai, spec, j):
        def mk(refs, ss, rs):
            x, y, c = _mesh_pos()
            px, py = [(1 - x, y), (x, 1 - y), (1 - x, 1 - y)][j]
            win = _window(refs[ai], spec[1], spec[2], 4 * px + 2 * py + c)
            return _remote(win, win, ss, rs, (x, y, 1 - c))
        return mk

    def start(self, after=None):
        starts = [self._mk_first(ai, self.specs[nm], k) for ai, nm in enumerate(self.names) for k in range(4)]
        arrs, self.first_sems, tok = split_call(self.prefix + "_start", [self.arr[nm] for nm in self.names],
                                                starts=starts, after=after)
        self.arr = dict(zip(self.names, arrs))
        return tok

    def forward(self, g, after=None):
        if g in self.forwarded:
            return None
        self.forwarded.add(g)
        names = self.groups[g]
        waits = [(0, 4 * self.names.index(nm) + 1 + j, "recv", self._mk_fwd(ai, self.specs[nm], j))
                 for ai, nm in enumerate(names) for j in range(3)]
        starts = [self._mk_fwd(ai, self.specs[nm], j) for ai, nm in enumerate(names) for j in range(3)]
        arrs, self.fwd_sems[g], tok = split_call(
            "%s_forward%d" % (self.prefix, g), [self.arr[nm] for nm in names], starts=starts, waits=waits,
            sems_in=[self.first_sems], after=after)
        self.arr.update(zip(names, arrs))
        return tok

    def finish(self, g, after=None):
        self.forward(g, after)
        names = self.groups[g]
        waits = []
        for ai, nm in enumerate(names):
            base = 4 * self.names.index(nm)
            waits.append((0, base, "recv", self._mk_first(ai, self.specs[nm], 0)))
            waits += [(1, 3 * ai + j, "recv", self._mk_fwd(ai, self.specs[nm], j)) for j in range(3)]
            waits += [(0, base + k, "send", self._mk_first(ai, self.specs[nm], k)) for k in range(4)]
            waits += [(1, 3 * ai + j, "send", self._mk_fwd(ai, self.specs[nm], j)) for j in range(3)]
        arrs, _, _ = split_call(
            "%s_finish%d" % (self.prefix, g), [self.arr[nm] for nm in names], waits=waits,
            sems_in=[self.first_sems, self.fwd_sems[g]], after=after)
        self.arr.update(zip(names, arrs))
        return {nm: self.arr[nm] for nm in names}


class Reducer:
    def __init__(self, cq_arr, adam):
        self.cq_arr, self.adam = cq_arr, adam
        self.groups = []
        self.n = 0
        self.last = None

    @staticmethod
    def _mk1(gi, li, spec, q):
        def mk(refs, ss, rs):
            x, y, c = _mesh_pos()
            return _remote(_window(refs[gi], spec[1], spec[2], 2 * q + (1 - c)), refs[li].at[q], ss, rs, (x, y, 1 - c))
        return mk

    @staticmethod
    def _mk2(si, li, d):
        def mk(refs, ss, rs):
            x, y, c = _mesh_pos()
            qd = lax.rem(2 * x + y + d, 4)
            return _remote(refs[si].at[d - 1], refs[li].at[3 - d], ss, rs, (lax.div(qd, 2), lax.rem(qd, 2), c))
        return mk

    def add(self, grads, after=None):
        names = list(grads)
        m = len(names)
        starts = [self._mk1(ai, m + ai, BIG[nm], q) for ai, nm in enumerate(names) for q in range(4)]
        new = [_sds((4,) + _win_shape(BIG[nm]), BF16) for nm in names]
        res, sems, tok = split_call("rs1_start%d" % self.n, [grads[nm] for nm in names], starts=starts, new=new,
                                    after=after)
        self.groups.append(dict(names=names, starts=starts, buf=res, sems=sems, stage=1, idx=self.n))
        self.n += 1
        return tok

    def step(self, after):
        tok = None
        for grp in self.groups:
            names, m = grp["names"], len(grp["names"])
            if grp["stage"] == 1:
                waits = [(0, k, kind, mk) for k, mk in enumerate(grp["starts"]) for kind in ("send", "recv")]
                res, _, _ = split_call("rs1_wait%d" % grp["idx"], grp["buf"], waits=waits, sems_in=[grp["sems"]], after=after)
                full, land1 = res[:m], res[m:]
                s1b = []
                for lo in range(0, m, 4):
                    s1b += list(add_pairs(full[lo:lo + 4], land1[lo:lo + 4], [BIG[nm] for nm in names[lo:lo + 4]],
                                          self.cq_arr))
                starts = [self._mk2(ai, m + ai, d) for ai in range(m) for d in (1, 2, 3)]
                new = [_sds(a.shape, BF16) for a in s1b]
                res, sems, tok = split_call("rs2_start%d" % grp["idx"], s1b, starts=starts, new=new, after=tok)
                grp.update(stage=2, g=full, land1=land1, starts=starts, buf=res, sems=sems)
        return tok

    def finish_oldest(self):
        for grp in self.groups:
            if grp["stage"] == 2:
                names, m = grp["names"], len(grp["names"])
                waits = [(0, k, kind, mk) for k, mk in enumerate(grp["starts"]) for kind in ("send", "recv")]
                res, _, _ = split_call("rs2_wait%d" % grp["idx"], grp["buf"], waits=waits, sems_in=[grp["sems"]],
                                       after=self.last)
                for nm, g, l1, l2 in zip(names, grp["g"], grp["land1"], res[m:]):
                    self.last = self.adam(nm, g, l1, l2, self.last)
                grp["stage"] = 3
                return True
        return False


def pack_weights(args, arg_names, small_blk, names, j_arr, dep=None):
    n_in = len(args)
    deps = [] if dep is None else [dep]

    def body(j_ref, *refs):
        for o, nm in enumerate(names):
            dst = refs[n_in + 1 + len(deps) + o]
            if nm == "small":
                dst[...] = refs[n_in][...]
                continue
            _, axis, w, valid, arg, layer = BIG[nm]
            if arg in TRANSPOSED_ARGS:
                s = refs[arg_names.index(arg)][layer]
                s = jnp.concatenate([s, jnp.zeros((w - valid, s.shape[1]), F32)], axis=0)
                dst[...] = s.T.astype(BF16)
                continue
            src = refs[arg_names.index(arg)][layer].astype(BF16)
            if valid == w:
                dst[...] = src
            else:
                dst[...] = jnp.zeros(dst.shape, BF16)
                if axis == 1:
                    dst[:, 0:valid] = src
                else:
                    dst[0:valid, :] = src

    def ispec(a):
        return pl.BlockSpec(a.shape, lambda i, j_ref: (0, 0, 0))

    def ospec(spec):
        axis, nd = spec[1], len(spec[0])
        return pl.BlockSpec(_win_shape(spec),
                            lambda i, j_ref, axis=axis, nd=nd: tuple(j_ref[0] if d == axis else 0 for d in range(nd)))

    specs = [_spec(nm) for nm in names]
    return pl.pallas_call(
        body, name="pack_weights",
        grid_spec=pltpu.PrefetchScalarGridSpec(
            num_scalar_prefetch=1, grid=(1,),
            in_specs=[ispec(a) for a in list(args) + [small_blk]] + [ANY] * len(deps),
            out_specs=[ospec(s) for s in specs]),
        out_shape=[_sds(s[0], F32 if nm == "small" else BF16) for nm, s in zip(names, specs)],
        compiler_params=_cp(("arbitrary",)),
    )(j_arr, *args, small_blk, *deps)


def add_pairs(fulls, lands, specs, cq_arr):
    def chip(d, cq):
        return lax.rem(cq[1] + d + 1, 4)

    in_specs, args = [], []
    for full, land, spec in zip(fulls, lands, specs):
        axis, w = spec[1], spec[2]
        R, C = full.shape
        for d in range(3):
            if axis == 1:
                in_specs.append(pl.BlockSpec((R, w), lambda i, cq, d=d: (0, 2 * chip(d, cq) + cq[0])))
                in_specs.append(pl.BlockSpec((None, R, w), lambda i, cq, d=d: (chip(d, cq), 0, 0)))
            else:
                in_specs.append(pl.BlockSpec((w, C), lambda i, cq, d=d: (2 * chip(d, cq) + cq[0], 0)))
                in_specs.append(pl.BlockSpec((None, w, C), lambda i, cq, d=d: (chip(d, cq), 0, 0)))
            args += [full, land]
    out_shape = [_sds((3,) + land.shape[1:], BF16) for land in lands]
    n = len(fulls)

    def body(cq_ref, *refs):
        for a in range(n):
            for d in range(3):
                own, got = refs[6 * a + 2 * d], refs[6 * a + 2 * d + 1]
                refs[6 * n + a][d] = (own[...].astype(F32) + got[...].astype(F32)).astype(BF16)

    return pl.pallas_call(
        body, name="add_pairs",
        grid_spec=pltpu.PrefetchScalarGridSpec(
            num_scalar_prefetch=1, grid=(1,), in_specs=in_specs,
            out_specs=[pl.BlockSpec(o.shape, lambda i, cq: (0, 0, 0)) for o in out_shape]),
        out_shape=out_shape,
        compiler_params=_cp(("arbitrary",)),
    )(cq_arr, *args)


def _adamw(w, g, m, v):
    m = ADAM_B1 * m + (1.0 - ADAM_B1) * g
    v = ADAM_B2 * v + (1.0 - ADAM_B2) * (g * g)
    m_hat = m / (1.0 - ADAM_B1 ** ADAM_STEP)
    v_hat = v / (1.0 - ADAM_B2 ** ADAM_STEP)
    delta = -ADAM_LR * (m_hat / (jnp.sqrt(v_hat) + ADAM_EPS) + ADAM_WD * w)
    return delta, m, v


def reduce_adamw(full, land1, land, w, m, v, spec, cq_arr, prev=None, dep=None):
    axis, win, valid, layer = spec[1], spec[2], spec[3], spec[5]
    L, R, C = w.shape
    transposed = spec[4] in TRANSPOSED_ARGS
    TL = 256
    if transposed:
        grid = (C // TL,)
        fspec = pl.BlockSpec((TL, win), lambda i, cq: (i, 2 * cq[1] + cq[0]))
        wspec = pl.BlockSpec((None, TL, win), lambda i, cq: (cq[1], i, 0))
        lspec = pl.BlockSpec((3, TL, win), lambda i, cq: (0, i, 0))
        sspec = pl.BlockSpec((None, R, TL), lambda i, cq: (layer, 0, i))
    elif axis == 1:
        tr = min(TL, R)
        grid = (R // tr,)
        fspec = pl.BlockSpec((tr, win), lambda i, cq: (i, 2 * cq[1] + cq[0]))
        wspec = pl.BlockSpec((None, tr, win), lambda i, cq: (cq[1], i, 0))
        lspec = pl.BlockSpec((3, tr, win), lambda i, cq: (0, i, 0))
        sspec = pl.BlockSpec((None, tr, C), lambda i, cq: (layer, i, 0))
    else:
        grid = (C // TL,)
        fspec = pl.BlockSpec((win, TL), lambda i, cq: (2 * cq[1] + cq[0], i))
        wspec = pl.BlockSpec((None, win, TL), lambda i, cq: (cq[1], 0, i))
        lspec = pl.BlockSpec((3, win, TL), lambda i, cq: (0, 0, i))
        sspec = pl.BlockSpec((None, R, TL), lambda i, cq: (layer, 0, i))

    def body(cq_ref, full_ref, own_ref, land_ref, w_ref, m_ref, v_ref, *rest):
        g_ref, d_ref, nm_ref, nv_ref = rest[-4:]
        if transposed:
            rd = lambda r, *lead: r[lead] if lead else r[...]
        elif axis == 1:
            rd = lambda r, *lead: r[(*lead, slice(None), slice(0, valid))]
        else:
            rd = lambda r, *lead: r[(*lead, slice(0, valid), slice(None))]
        g = rd(full_ref).astype(F32) + rd(own_ref).astype(F32)
        for k in range(3):
            g = g + rd(land_ref, k).astype(F32)
        if transposed:
            g = g.T[0:valid, :]
        g_ref[...] = g
        d, nm, nv = _adamw(w_ref[...], g, m_ref[...], v_ref[...])
        d_ref[...] = d
        nm_ref[...] = nm
        nv_ref[...] = nv

    extra = (list(prev) if prev is not None else []) + ([dep] if dep is not None else [])
    return pl.pallas_call(
        body, name="reduce_adamw",
        grid_spec=pltpu.PrefetchScalarGridSpec(
            num_scalar_prefetch=1, grid=grid,
            in_specs=[fspec, wspec, lspec, sspec, sspec, sspec] + [ANY] * len(extra), out_specs=[sspec] * 4),
        out_shape=[_sds(w.shape)] * 4,
        input_output_aliases={7 + k: k for k in range(4 if prev is not None else 0)},
        compiler_params=_cp(("arbitrary",)),
    )(cq_arr, full, land1, land, w, m, v, *extra)


def place_slot(packed, j_arr):
    R = packed.shape[0]

    def body(j_ref, src, dst):
        dst[...] = src[...]

    return pl.pallas_call(
        body, name="place_slot",
        grid_spec=pltpu.PrefetchScalarGridSpec(
            num_scalar_prefetch=1, grid=(1,),
            in_specs=[pl.BlockSpec((R, 128), lambda i, j: (0, 0))],
            out_specs=[pl.BlockSpec((None, R, 128), lambda i, j: (j[0], 0, 0))]),
        out_shape=[_sds((N_DEV, R, 128))], compiler_params=_cp(("arbitrary",)),
    )(j_arr, packed)[0]


def sum_slots(gathered):
    def body(g_ref, o_ref):
        g = g_ref[0]
        for dev in range(1, N_DEV):
            g = g + g_ref[dev]
        o_ref[...] = g

    return pl.pallas_call(body, name="sum_slots", out_shape=_sds(gathered.shape[1:]), compiler_params=_cp())(gathered)


def small_adamw(gs, wmv):
    k = len(gs)

    def body(*refs):
        for a in range(k):
            g, w, m, v = refs[4 * a:4 * a + 4]
            d, nm, nv = _adamw(w[...], g[...], m[...], v[...])
            refs[4 * k + 3 * a][...] = d
            refs[4 * k + 3 * a + 1][...] = nm
            refs[4 * k + 3 * a + 2][...] = nv

    args = [t for g, tup in zip(gs, wmv) for t in (g,) + tuple(tup)]
    out_shape = [_sds(g.shape) for g in gs for _ in range(3)]
    return pl.pallas_call(body, name="small_adamw", out_shape=out_shape, compiler_params=_cp())(*args)


WEIGHT_NAMES = ("even_w_in", "even_w_out", "pool_w", "pool_scale", "odd_w_in", "odd_w_out", "conv_dw", "conv_ln_g",
                "conv_ln_b", "sg_ln_g", "sg_ln_b", "sg_w", "sg_b", "ln_mix_g", "ln_mix_b", "ffn_w_gate", "ffn_w_up",
                "ffn_w_down", "ln_ffn_g", "ln_ffn_b", "ple_w_proj", "ple_w_gate", "ple_b_gate")
PACK_ARGS = ("even_w_in", "even_w_out", "odd_w_in", "odd_w_out", "ffn_w_gate", "ffn_w_up", "ffn_w_down",
             "ple_w_gate", "ple_w_proj")
REPLICATED = ("pool_w", "pool_scale", "sg_w", "sg_b", "ln_mix_g", "ln_mix_b", "ln_ffn_g", "ln_ffn_b", "ple_b_gate")
SHARDED_SMALL = ("conv_dw", "conv_ln_g", "conv_ln_b", "sg_ln_g", "sg_ln_b")
NATURAL = {"pool_w": (4, 128, 128), "pool_scale": (1, 512), "sg_w": (4, 128, 128), "sg_b": (4, 128),
           "ln_mix_g": (2, 1024), "ln_mix_b": (2, 1024), "ln_ffn_g": (2, 1024), "ln_ffn_b": (2, 1024),
           "ple_b_gate": (2, 1024)}


def kernel(x, p, even_w_in, even_w_out, pool_w, pool_scale, odd_w_in, odd_w_out, conv_dw, conv_ln_g, conv_ln_b, sg_ln_g, sg_ln_b, sg_w, sg_b, ln_mix_g, ln_mix_b, ffn_w_gate, ffn_w_up, ffn_w_down, ln_ffn_g, ln_ffn_b, ple_w_proj, ple_w_gate, ple_b_gate, loss_target, m_even_w_in, m_even_w_out, m_pool_w, m_pool_scale, m_odd_w_in, m_odd_w_out, m_conv_dw, m_conv_ln_g, m_conv_ln_b, m_sg_ln_g, m_sg_ln_b, m_sg_w, m_sg_b, m_ln_mix_g, m_ln_mix_b, m_ffn_w_gate, m_ffn_w_up, m_ffn_w_down, m_ln_ffn_g, m_ln_ffn_b, m_ple_w_proj, m_ple_w_gate, m_ple_b_gate, v_even_w_in, v_even_w_out, v_pool_w, v_pool_scale, v_odd_w_in, v_odd_w_out, v_conv_dw, v_conv_ln_g, v_conv_ln_b, v_sg_ln_g, v_sg_ln_b, v_sg_w, v_sg_b, v_ln_mix_g, v_ln_mix_b, v_ffn_w_gate, v_ffn_w_up, v_ffn_w_down, v_ln_ffn_g, v_ln_ffn_b, v_ple_w_proj, v_ple_w_gate, v_ple_b_gate):
    A = dict(locals())
    for arg in TRANSPOSED_ARGS:
        for pre in ("", "m_", "v_"):
            A[pre + arg] = jnp.swapaxes(A[pre + arg], 1, 2)
    mx, my, mc = _mesh_pos()
    j = 4 * mx + 2 * my + mc
    j_arr = j.astype(jnp.int32).reshape(1)
    cq_arr = jnp.stack([mc, 2 * mx + my]).astype(jnp.int32)
    res = {}

    def adam(nm, full, land1, land2, dep):
        arg = BIG[nm][4]
        res[arg] = reduce_adamw(full, land1, land2, A[arg], A["m_" + arg], A["v_" + arg], BIG[nm], cq_arr,
                                res.get(arg), dep)
        return res[arg][0]

    class Comm:
        def __init__(self):
            names = [nm for g in AG_GROUPS for nm in g]
            specs = {nm: _spec(nm) for nm in names}
            small_blk = jnp.concatenate([conv_dw[0], conv_ln_g, conv_ln_b, sg_ln_g, sg_ln_b, jnp.zeros((5, 64), F32)], axis=0)
            first = pack_weights([A[AG_GROUPS[0][0]]], AG_GROUPS[0], small_blk[None], AG_GROUPS[0], j_arr)
            self.gat0 = Gatherer(AG_GROUPS[:1], first, specs, "ag0")
            first_started = self.gat0.start()
            rest = names[len(AG_GROUPS[0]):]
            mine = pack_weights([A[k] for k in PACK_ARGS], PACK_ARGS, small_blk[None], rest, j_arr, dep=first_started)
            self.gat = Gatherer(AG_GROUPS[1:], mine, specs, "ag")
            self.rest_started = self.gat.start()
            self.red = Reducer(cq_arr, adam)
            self.W = {k: A[k].reshape(NATURAL[k]) for k in REPLICATED}

        def weights(self, stage, after):
            if stage in AG_NEED:
                g = AG_NEED[stage]
                got = self.gat0.finish(0, self.rest_started) if g == 0 else self.gat.finish(g - 1, after)
                if "small" in got:
                    sm = got.pop("small").transpose(1, 0, 2).reshape(40, 512)
                    got.update(conv_dw=sm[0:31], conv_ln_g=sm[31:32], conv_ln_b=sm[32:33], sg_ln_g=sm[33:34],
                               sg_ln_b=sm[34:35])
                self.W.update(got)
            return self.W

        def all_weights(self):
            return self.W

        def poke(self, tag, after):
            if tag in AG_PASS:
                return self.gat.forward(AG_PASS[tag] - 1, after)
            if tag[0] == "bwd":
                return self.red.step(after)
            return None

        def grads(self, grads):
            self.n_grads = getattr(self, "n_grads", 0) + 1
            if self.n_grads == 2:
                self.held = grads
                return None
            if self.n_grads == 3:
                grads = {**self.held, **grads}
            tok = self.red.step(next(iter(grads.values())))
            return self.red.add(grads, after=tok)

    comm = Comm()
    sq, dx, small = run_layers(x[0], p[:, 0], loss_target[0], comm)
    red = comm.red
    tok = red.step(dx)

    names = REPLICATED + SHARDED_SMALL
    flat = jnp.concatenate([small[k].reshape(-1) for k in names] + [jnp.sum(sq).reshape(1)])
    rows = -(-flat.shape[0] // 1024) * 8
    packed = jnp.pad(flat, (0, rows * 128 - flat.shape[0])).reshape(rows, 128)
    sg = Gatherer((["g"],), [place_slot(packed, j_arr)], {"g": ((N_DEV, rows, 128), 0, 1, 1)}, "sg")
    red.last = sg.start(after=tok)
    older = sum(grp["stage"] == 2 for grp in red.groups) - 1
    for k in range(older):
        red.finish_oldest()
        if k == 0:
            sg.forward(0, after=red.last)
    gsum_flat = sum_slots(sg.finish(0, after=red.last)["g"]).reshape(-1)
    loss = 0.5 * gsum_flat[flat.shape[0] - 1] / x.shape[-1]
    gs, off = [], 0
    for k in names:
        n = math.prod(small[k].shape)
        g = gsum_flat[off:off + n].reshape(small[k].shape)
        off += n
        if k in SHARDED_SMALL:
            g = lax.dynamic_slice_in_dim(g, j * 64, 64, axis=1)
        gs.append(g.reshape(A[k].shape))
    outs = small_adamw(gs, [(A[k], A["m_" + k], A["v_" + k]) for k in names])
    for a, k in enumerate(names):
        res[k] = (gs[a],) + tuple(outs[3 * a:3 * a + 3])
    red.last = outs[0]
    while red.finish_oldest():
        pass

    for arg in TRANSPOSED_ARGS:
        res[arg] = [jnp.swapaxes(t, 1, 2) for t in res[arg]]
    out = [loss, dx[None]]
    for part in range(4):
        out += [res[k][part] for k in WEIGHT_NAMES]
    return tuple(out)
```

```python
import math

import jax
import jax.numpy as jnp
from jax import lax
from jax.experimental import pallas as pl
from jax.experimental.pallas import tpu as pltpu

F32, BF16 = jnp.float32, jnp.bfloat16
ALPHA = 4.0 ** 0.25
LN_EPS = 1e-5
QK_SCALE = 0.125
POOL_WINDOWS = (2, 4, 8, 16)
CONV_TAPS = 31
N_DEV = 8
FF_SHARD, FF_PAD = 352, 384
ADAM_LR, ADAM_B1, ADAM_B2, ADAM_EPS, ADAM_WD, ADAM_STEP = 0.001, 0.9, 0.999, 1e-08, 0.01, 10
VMEM_LIMIT = 56 * 1024 * 1024
MESH_T = pl.DeviceIdType.MESH


def _cp(sem=None):
    return pltpu.CompilerParams(dimension_semantics=sem, vmem_limit_bytes=VMEM_LIMIT)


def _dot(a, b):
    return jnp.dot(a, b, preferred_element_type=F32)


def _dot_nt(a, b):
    return lax.dot_general(a, b, (((1,), (1,)), ((), ())), preferred_element_type=F32)


def _dot_tn(a, b):
    return lax.dot_general(a, b, (((0,), (0,)), ((), ())), preferred_element_type=F32)


def _sigmoid(x):
    return 1.0 / (1.0 + jnp.exp(-x))


def _softplus(z):
    return jnp.maximum(z, 0.0) + jnp.log(1.0 + jnp.exp(-jnp.abs(z)))


_GELU_C = math.sqrt(2.0 / math.pi)


def _gelu(x):
    return 0.5 * x * (1.0 + jnp.tanh(_GELU_C * (x + 0.044715 * x * x * x)))


def _gelu_grad(x):
    t = jnp.tanh(_GELU_C * (x + 0.044715 * x * x * x))
    return 0.5 * (1.0 + t) + 0.5 * x * (1.0 - t * t) * _GELU_C * (1.0 + 3.0 * 0.044715 * x * x)


def _ln_fwd(r, g, b):
    mu = jnp.mean(r, axis=-1, keepdims=True)
    xc = r - mu
    var = jnp.mean(xc * xc, axis=-1, keepdims=True)
    rstd = lax.rsqrt(var + LN_EPS)
    xh = xc * rstd
    return xh * g + b, xh, rstd


def _ln_bwd(dy, xh, rstd, g):
    dxh = dy * g
    m1 = jnp.mean(dxh, axis=-1, keepdims=True)
    m2 = jnp.mean(dxh * xh, axis=-1, keepdims=True)
    return rstd * (dxh - m1 - xh * m2)


def _split2(x):
    hi = x.astype(BF16)
    lo = (x - hi.astype(F32)).astype(BF16)
    return hi, lo


def _colsum(x):
    return jnp.sum(x, axis=0, keepdims=True)


def _tok_call(name, body, tiled, full, out_tiled, out_acc=(), tm=256, scratch=(), dep=None):
    def arr(t):
        return t[0] if isinstance(t, tuple) else t
    full = [t[0] if isinstance(t, tuple) and t[1] is None else t for t in full]
    S = arr(tiled[0]).shape[0]
    tm = min(tm, S)
    n_in = len(tiled) + len(full)
    deps = [] if dep is None else [dep]
    if deps:
        inner = body
        body = lambda *refs: inner(*refs[:n_in], *refs[n_in + 1:])

    def tspec(t):
        if isinstance(t, tuple):
            _, w, cb = t
            return pl.BlockSpec((tm, w), lambda i, cb=cb: (i, cb))
        return pl.BlockSpec((tm, t.shape[1]), lambda i: (i, 0))

    def fspec(t):
        if isinstance(t, tuple):
            a, l = t
            nd = a.ndim - 1
            return pl.BlockSpec((None,) + a.shape[1:], lambda i, l=l, nd=nd: (l,) + (0,) * nd)
        nd = t.ndim
        return pl.BlockSpec(t.shape, lambda i, nd=nd: (0,) * nd)

    def ospec(o):
        return pl.BlockSpec((tm, o.shape[1]), lambda i: (i, 0))

    def aspec(o):
        nd = len(o.shape)
        return pl.BlockSpec(o.shape, lambda i, nd=nd: (0,) * nd)

    outs = pl.pallas_call(
        body, name=name, grid=(S // tm,),
        in_specs=[tspec(t) for t in tiled] + [fspec(t) for t in full] + [ANY] * len(deps),
        out_specs=[ospec(o) for o in out_tiled] + [aspec(o) for o in out_acc],
        out_shape=list(out_tiled) + list(out_acc),
        scratch_shapes=list(scratch),
        compiler_params=_cp(("arbitrary",)),
    )(*[arr(t) for t in tiled], *[arr(t) for t in full], *deps)
    return outs


def _sds(shape, dtype=F32):
    return jax.ShapeDtypeStruct(tuple(shape), dtype)


def _acc(ref, val):
    @pl.when(pl.program_id(0) == 0)
    def _():
        ref[...] = val

    @pl.when(pl.program_id(0) != 0)
    def _():
        ref[...] += val


def mm_in(x, w, nb16=0):
    S, N = x.shape[0], w.shape[1]

    def body(x_ref, w_ref, h_ref, xb_ref, *hb_ref):
        xb = x_ref[...].astype(BF16)
        xb_ref[...] = xb
        h = _dot(xb, w_ref[...])
        h_ref[...] = h
        if nb16:
            hb_ref[0][...] = h[:, 0:nb16].astype(BF16)

    outs = [_sds((S, N)), _sds((S, x.shape[1]), BF16)] + ([_sds((S, nb16), BF16)] if nb16 else [])
    return _tok_call("mm_in", body, [x], [w], outs, tm=512)


def _stack_heads(x, hm0, dtype=BF16):
    return jnp.concatenate([jnp.where(hm0, x, 0), jnp.where(hm0, 0, x)], axis=0).astype(dtype)


def _unstack_k(x, T):
    return jnp.concatenate([x[0:T], x[T:2 * T]], axis=1)


def _cumsum_mm(x, u):
    n = x.shape[0]
    hi, lo = _split2(x)
    r = _dot(jnp.concatenate([hi, lo], axis=0), u)
    return r[0:n] + r[n:2 * n]


def attn_fwd(qkv, T=256):
    S = qkv.shape[0]
    T = min(T, S)
    nq = S // T

    def body(q_ref, k_ref, v_ref, o_ref, t_ref, acc_ref, c_ref, qh_ref):
        i = pl.program_id(0)
        hm0 = lax.broadcasted_iota(jnp.int32, (1, 128), 1) < 64
        r2 = lax.broadcasted_iota(jnp.int32, (2 * T, T), 0)
        c2 = lax.broadcasted_iota(jnp.int32, (2 * T, T), 1)
        causal = c2 < jnp.where(r2 >= T, r2 - T, r2)
        ur = lax.broadcasted_iota(jnp.int32, (T, T), 0)
        uc = lax.broadcasted_iota(jnp.int32, (T, T), 1)
        u_incl = (ur >= uc).astype(BF16)
        acc_ref[...] = jnp.zeros_like(acc_ref)
        c_ref[...] = jnp.zeros_like(c_ref)
        for pp in range(4):
            qh_ref[pp] = _stack_heads(q_ref[:, pp * 128:(pp + 1) * 128] * QK_SCALE, hm0)

        def block(kb, diag):
            ks = pl.multiple_of(kb * T, T)
            cols = [slice(pp * 128, (pp + 1) * 128) for pp in range(4)]
            zs = [_dot_nt(qh_ref[pp], k_ref[pl.ds(ks, T), cols[pp]]) for pp in range(4)]
            incls = []
            for pp in range(4):
                sp = _softplus(zs[pp])
                if diag:
                    sp = jnp.where(causal, sp, 0.0)
                incls.append(_cumsum_mm(sp, u_incl))
            for pp in range(4):
                c = c_ref[pp]
                w = jnp.exp(zs[pp] - incls[pp] - c)
                if diag:
                    w = jnp.where(causal, w, 0.0)
                acc_ref[:, cols[pp]] += _dot(_unstack_k(w.astype(BF16), T),
                                             _stack_heads(v_ref[pl.ds(ks, T), cols[pp]], hm0))
                c_ref[pp] = c + jnp.broadcast_to(incls[pp][:, 0:1], (2 * T, T))

        block(i, True)

        def step(jj, carry):
            block(i - 1 - jj, False)
            return carry

        lax.fori_loop(0, i, step, 0)
        o_ref[...] = acc_ref[...].astype(BF16)
        for pp in range(4):
            for hd in range(2):
                t_ref[2 * pp + hd] = c_ref[pp, hd * T:(hd + 1) * T, 0:128]

    return pl.pallas_call(
        body, name="attn_fwd", grid=(nq,),
        in_specs=[pl.BlockSpec((T, 512), lambda i: (i, 0)),
                  pl.BlockSpec((S, 512), lambda i: (0, 1)),
                  pl.BlockSpec((S, 512), lambda i: (0, 2))],
        out_specs=[pl.BlockSpec((T, 512), lambda i: (i, 0)),
                   pl.BlockSpec((8, T, 128), lambda i: (0, i, 0))],
        out_shape=[_sds((S, 512), BF16), _sds((8, S, 128))],
        scratch_shapes=[pltpu.VMEM((T, 512), F32), pltpu.VMEM((4, 2 * T, T), F32), pltpu.VMEM((4, 2 * T, 128), BF16)],
        compiler_params=_cp(("arbitrary",)),
    )(qkv, qkv, qkv)


def pool_fwd(h, pool_w, pool_scale, CH=256):
    S = h.shape[0]
    CH = min(CH, S)

    def body(u_ref, w_ref, sc_ref, b_ref, pooled_ref, pad_ref):
        pad_ref[0:16, :] = jnp.zeros((16, 512), F32)
        pad_ref[16:16 + S, :] = u_ref[...]
        for g, win in enumerate(POOL_WINDOWS):
            cs = slice(g * 128, (g + 1) * 128)
            wq = w_ref[g].astype(BF16)
            for ch in range(S // CH):
                base = ch * CH
                acc = pad_ref[16 + base:16 + base + CH, cs]
                for sft in range(1, win):
                    acc = acc + pad_ref[16 + base - sft:16 + base - sft + CH, cs]
                t = base + lax.broadcasted_iota(jnp.int32, (CH, 1), 0)
                cnt = jnp.minimum(t + 1, win).astype(F32)
                pooled = (acc / cnt - pad_ref[16 + base:16 + base + CH, cs]).astype(BF16)
                pooled_ref[base:base + CH, cs] = pooled
                b_ref[base:base + CH, cs] = (_dot(pooled, wq) * sc_ref[:, cs]).astype(BF16)

    return pl.pallas_call(
        body, name="pool_fwd", grid=(1,),
        in_specs=[pl.BlockSpec((S, 512), lambda i: (0, 3)),
                  pl.BlockSpec((4, 128, 128), lambda i: (0, 0, 0)),
                  pl.BlockSpec((1, 512), lambda i: (0, 0))],
        out_specs=[pl.BlockSpec((S, 512), lambda i: (0, 0)), pl.BlockSpec((S, 512), lambda i: (0, 0))],
        out_shape=[_sds((S, 512), BF16), _sds((S, 512), BF16)],
        scratch_shapes=[pltpu.VMEM((S + 16, 512), F32)],
        compiler_params=_cp(("arbitrary",)),
    )(h, pool_w, pool_scale)


def conv_fwd(h, dw, CH=128):
    S = h.shape[0]

    def body(a_ref, g_ref, dw_ref, y_ref, hc_ref, pad_ref):
        hc = a_ref[...] * _sigmoid(g_ref[...])
        hc_ref[...] = hc
        pad_ref[0:32, :] = jnp.zeros((32, 128), F32)
        pad_ref[32:32 + S, :] = hc
        for ch in range(S // CH):
            base = ch * CH + 2
            acc = dw_ref[0:1, :] * pad_ref[base:base + CH, :]
            for k in range(1, CONV_TAPS):
                acc = acc + dw_ref[k:k + 1, :] * pad_ref[base + k:base + k + CH, :]
            y_ref[ch * CH:(ch + 1) * CH, :] = acc

    return pl.pallas_call(
        body, name="conv_fwd", grid=(4,),
        in_specs=[pl.BlockSpec((S, 128), lambda c: (0, c)),
                  pl.BlockSpec((S, 128), lambda c: (0, 4 + c)),
                  pl.BlockSpec((CONV_TAPS, 128), lambda c: (0, c))],
        out_specs=[pl.BlockSpec((S, 128), lambda c: (0, c)), pl.BlockSpec((S, 128), lambda c: (0, c))],
        out_shape=[_sds((S, 512)), _sds((S, 512))],
        scratch_shapes=[pltpu.VMEM((S + 32, 128), F32)],
        compiler_params=_cp(("arbitrary",)),
    )(h, h, dw)


def _masked_sg_w(w_ref, g):
    row = lax.broadcasted_iota(jnp.int32, (128, 128), 0)
    col = lax.broadcasted_iota(jnp.int32, (128, 128), 1)
    return jnp.where(row >= col, w_ref[g], 0.0).astype(BF16)


def odd_post(y, h, cl_g, cl_b, sl_g, sl_b, sg_w, sgb_bc, tm=256):
    S = y.shape[0]
    tm = min(tm, S)

    def body(y_ref, zc_ref, clg, clb, slg, slb, w_ref, sb_ref,
             c_ref, d_ref, xhc_ref, rsc_ref, xhv_ref, rsv_ref, sv_ref):
        lnc, xhc, rsc = _ln_fwd(y_ref[...], clg[...], clb[...])
        c_ref[...] = (lnc * _sigmoid(lnc)).astype(BF16)
        xhc_ref[...] = xhc
        rsc_ref[...] = rsc
        z = _gelu(zc_ref[...])
        vn, xhv, rsv = _ln_fwd(z[:, 512:], slg[...], slb[...])
        xhv_ref[...] = xhv
        rsv_ref[...] = rsv
        vnb = vn.astype(BF16)
        for g in range(4):
            wm = _masked_sg_w(w_ref, g)
            for ch in range(tm // 128):
                rs, cs = slice(ch * 128, (ch + 1) * 128), slice(g * 128, (g + 1) * 128)
                sv_ref[rs, cs] = _dot(wm, vnb[rs, cs]) + sb_ref[g]
        d_ref[...] = (z[:, :512] * sv_ref[...]).astype(BF16)

    return _tok_call(
        "odd_post", body, [y, (h, 1024, 1)], [cl_g, cl_b, sl_g, sl_b, sg_w, sgb_bc],
        [_sds((S, 512), BF16), _sds((S, 512), BF16), _sds((S, 512)), _sds((S, 1)),
         _sds((S, 512)), _sds((S, 1)), _sds((S, 512))], tm=tm)


def mm_out_ln(l1, l2, x, w, g, b, dep=None):
    S, D = x.shape

    def body(l1_ref, l2_ref, x_ref, w_ref, g_ref, b_ref, y_ref, xh_ref, rs_ref):
        mix = _dot(l1_ref[...], w_ref[0:512, :]) + _dot(l2_ref[...], w_ref[512:1024, :])
        y, xh, rs = _ln_fwd(ALPHA * x_ref[...] + mix, g_ref[...], b_ref[...])
        y_ref[...] = y
        xh_ref[...] = xh
        rs_ref[...] = rs

    return _tok_call("mm_out_ln", body, [l1, l2, x], [w, g, b],
                     [_sds((S, D)), _sds((S, D)), _sds((S, 1))], dep=dep, tm=512)


def ffn_up(x1, wg, wu, layer, dep=None):
    S, D = x1.shape
    F = wg.shape[-1]

    def body(x_ref, wg_ref, wu_ref, gate_ref, up_ref, hb_ref, xb_ref):
        xb = x_ref[...].astype(BF16)
        xb_ref[...] = xb
        gate = _dot(xb, wg_ref[...])
        up = _dot(xb, wu_ref[...])
        gate_ref[...] = gate.astype(BF16)
        up_ref[...] = up.astype(BF16)
        hb_ref[...] = (gate * _sigmoid(gate) * up).astype(BF16)

    return _tok_call("ffn_up", body, [x1], [(wg, layer), (wu, layer)],
                     [_sds((S, F), BF16), _sds((S, F), BF16), _sds((S, F), BF16), _sds((S, D), BF16)], dep=dep)


def ffn_down_ln(hb, x1, wd, layer, g, b):
    S, D = x1.shape

    def body(h_ref, x_ref, w_ref, g_ref, b_ref, y_ref, xh_ref, rs_ref):
        f = _dot(h_ref[...], w_ref[...])
        y, xh, rs = _ln_fwd(ALPHA * x_ref[...] + f, g_ref[...], b_ref[...])
        y_ref[...] = y
        xh_ref[...] = xh
        rs_ref[...] = rs

    return _tok_call("ffn_down_ln", body, [hb, x1], [(wd, layer), g, b],
                     [_sds((S, D)), _sds((S, D)), _sds((S, 1))], tm=512)


def ple_fwd(x2, p, wpg, wpp, layer, bg, target=None, dep=None):
    S, D = x2.shape
    last = target is not None

    def body(*refs):
        if last:
            x_ref, p_ref, t_ref, wg_ref, wp_ref, b_ref, x3_ref, sg_ref, pp_ref, xb_ref, pb_ref, dy_ref, ls_ref = refs
        else:
            x_ref, p_ref, wg_ref, wp_ref, b_ref, x3_ref, sg_ref, pp_ref, xb_ref, pb_ref = refs
        x = x_ref[...]
        xb = x.astype(BF16)
        pb = p_ref[...].astype(BF16)
        xb_ref[...] = xb
        pb_ref[...] = pb
        sg = _sigmoid(_dot(xb, wg_ref[...]) + b_ref[...])
        pp = _dot(pb, wp_ref[...])
        sg_ref[...] = sg.astype(BF16)
        pp_ref[...] = pp.astype(BF16)
        x3 = x + sg * pp
        x3_ref[...] = x3
        if last:
            err = x3 - t_ref[...]
            dy_ref[...] = err * (1.0 / D)
            _acc(ls_ref, _colsum(err * err))

    outs = [_sds((S, D)), _sds((S, D), BF16), _sds((S, D), BF16), _sds((S, D), BF16), _sds((S, p.shape[1]), BF16)]
    tiled = [x2, p] + ([target] if last else [])
    if last:
        outs.append(_sds((S, D)))
    return _tok_call("ple_fwd", body, tiled, [(wpg, layer), (wpp, layer), bg], outs,
                     [_sds((1, D))] if last else [], dep=dep, tm=512)


def ple_ln_bwd(dx3, sg, pp, x2b, pb, wpg, xh, rs, g, dep=None):
    S, D = dx3.shape

    def body(d_ref, sg_ref, pp_ref, x2b_ref, pb_ref, xh_ref, rs_ref, w_ref, g_ref,
             dr_ref, drb_ref, dbg_ref, dlg_ref, dlb_ref, dwg_ref, dwp_ref, accg_ref, accp_ref):
        fin_g = _sum_steps(accg_ref, dwg_ref)
        fin_p = _sum_steps(accp_ref, dwp_ref)
        d, sg = d_ref[...], sg_ref[...].astype(F32)
        dgp = d * pp_ref[...].astype(F32) * sg * (1.0 - sg)
        dgpb = dgp.astype(BF16)
        accg_ref[...] += _dot_tn(x2b_ref[...], dgpb)
        accp_ref[...] += _dot_tn(pb_ref[...], (d * sg).astype(BF16))
        _acc(dbg_ref, _colsum(dgp))
        dx2 = d + _dot_nt(dgpb, w_ref[...])
        xh = xh_ref[...]
        dr = _ln_bwd(dx2, xh, rs_ref[...], g_ref[...])
        dr_ref[...] = dr
        drb_ref[...] = dr.astype(BF16)
        _acc(dlg_ref, _colsum(dx2 * xh))
        _acc(dlb_ref, _colsum(dx2))
        fin_g()
        fin_p()

    P = pb.shape[1]
    return _tok_call("ple_ln_bwd", body, [dx3, sg, pp, x2b, pb, xh, rs], [wpg, g],
                     [_sds((S, D)), _sds((S, D), BF16)],
                     [_sds((1, D)), _sds((1, D)), _sds((1, D)), _sds((D, D), BF16), _sds((P, D), BF16)],
                     scratch=[pltpu.VMEM((D, D), F32), pltpu.VMEM((P, D), F32)], dep=dep, tm=ACC_TM)


def ffn_bwd(dr_b, x1b, gate, up, hb, wg, wu, wd, TH=256):
    S, D = dr_b.shape
    F = gate.shape[1]

    def body(dr_hbm, x_hbm, gate_ref, up_ref, hb_ref, wg_ref, wu_ref, wd_ref,
             dx_ref, dwg_ref, dwu_ref, dwd_ref, dr_v, x_v, sem, dg_s, du_s):
        @pl.when(pl.program_id(0) == 0)
        def _():
            c1 = pltpu.make_async_copy(dr_hbm, dr_v, sem.at[0])
            c2 = pltpu.make_async_copy(x_hbm, x_v, sem.at[1])
            c1.start()
            c2.start()
            c1.wait()
            c2.wait()
            dx_ref[...] = jnp.zeros_like(dx_ref)

        for ch in range(S // CH):
            rows = slice(ch * CH, (ch + 1) * CH)
            dh = _dot_nt(dr_v[rows, :], wd_ref[...])
            g, u = gate_ref[rows, :].astype(F32), up_ref[rows, :].astype(F32)
            s = _sigmoid(g)
            dgb = (dh * u * s * (1.0 + g * (1.0 - s))).astype(BF16)
            dub = (dh * g * s).astype(BF16)
            dg_s[rows, :] = dgb
            du_s[rows, :] = dub
            dx_ref[rows, :] += _dot_nt(dgb, wg_ref[...]) + _dot_nt(dub, wu_ref[...])
        x = x_v[...]
        dwg_ref[...] = _dot_tn(x, dg_s[...]).astype(BF16)
        dwu_ref[...] = _dot_tn(x, du_s[...]).astype(BF16)
        dwd_ref[...] = _dot_tn(hb_ref[...], dr_v[...]).astype(BF16)

    CH = min(512, S)
    col = lambda rows: pl.BlockSpec((rows, TH), lambda j: (0, j))
    row = pl.BlockSpec((TH, D), lambda j: (j, 0))
    return pl.pallas_call(
        body, name="ffn_bwd", grid=(F // TH,),
        in_specs=[ANY, ANY, col(S), col(S), col(S), col(D), col(D), row],
        out_specs=[pl.BlockSpec((S, D), lambda j: (0, 0)), col(D), col(D), row],
        out_shape=[_sds((S, D)), _sds((D, F), BF16), _sds((D, F), BF16), _sds((F, D), BF16)],
        scratch_shapes=[pltpu.VMEM((S, D), BF16), pltpu.VMEM((S, D), BF16), pltpu.SemaphoreType.DMA((2,)),
                        pltpu.VMEM((S, TH), BF16), pltpu.VMEM((S, TH), BF16)],
        compiler_params=pltpu.CompilerParams(dimension_semantics=("arbitrary",), vmem_limit_bytes=60 * 1024 * 1024),
    )(dr_b, x1b, gate, up, hb, wg, wu, wd)


ACC_TM = 512


def _sum_steps(acc_ref, out_ref):
    @pl.when(pl.program_id(0) == 0)
    def _():
        acc_ref[...] = jnp.zeros_like(acc_ref)

    def finish():
        @pl.when(pl.program_id(0) == pl.num_programs(0) - 1)
        def _():
            out_ref[...] = acc_ref[...].astype(BF16)
    return finish


def mix_bwd(dxp, dr2, xh, rs, l1, l2, g, w, dep=None):
    S, D = dxp.shape
    K1 = l1.shape[1]

    def body(dxp_ref, dr2_ref, xh_ref, rs_ref, l1_ref, l2_ref, g_ref, w_ref,
             dr_ref, dl_ref, dlg_ref, dlb_ref, dw_ref, acc_ref):
        finish = _sum_steps(acc_ref, dw_ref)
        d, xh = ALPHA * dr2_ref[...] + dxp_ref[...], xh_ref[...]
        dr = _ln_bwd(d, xh, rs_ref[...], g_ref[...])
        drb = dr.astype(BF16)
        dr_ref[...] = dr
        dl_ref[...] = _dot_nt(drb, w_ref[...])
        _acc(dlg_ref, _colsum(d * xh))
        _acc(dlb_ref, _colsum(d))
        acc_ref[0:K1, :] += _dot_tn(l1_ref[...], drb)
        acc_ref[K1:, :] += _dot_tn(l2_ref[...], drb)
        finish()

    return _tok_call("mix_bwd", body, [dxp, dr2, xh, rs, l1, l2], [g, w],
                     [_sds((S, D)), _sds((S, D))], [_sds((1, D)), _sds((1, D)), _sds(w.shape, BF16)],
                     scratch=[pltpu.VMEM(w.shape, F32)], dep=dep, tm=ACC_TM)


def dx_in(dr, pieces, w, xb):
    S, D = dr.shape
    offs = [o for _, o in pieces]
    widths = [a.shape[1] for a, _ in pieces]
    npc = len(pieces)

    def body(*refs):
        dr_ref, prefs, xb_ref, w_ref = refs[0], refs[1:1 + npc], refs[1 + npc], refs[2 + npc]
        dx_ref, dw_ref, acc_ref = refs[3 + npc:]
        finish = _sum_steps(acc_ref, dw_ref)
        acc = ALPHA * dr_ref[...]
        xb_t = xb_ref[...]
        for pr, o, n in zip(prefs, offs, widths):
            piece = pr[...]
            acc = acc + _dot_nt(piece, w_ref[:, o:o + n])
            acc_ref[:, o:o + n] += _dot_tn(xb_t, piece)
        dx_ref[...] = acc
        finish()

    return _tok_call("dx_in", body, [dr] + [a for a, _ in pieces] + [xb], [w], [_sds((S, D))],
                     [_sds(w.shape, BF16)], scratch=[pltpu.VMEM(w.shape, F32)], tm=ACC_TM)


def odd_post_bwd(dl, h, xhc, rsc, xhv, rsv, sv, cl_g, cl_b, sl_g, sl_b, sg_w, tm=256, dep=None):
    S = dl.shape[0]
    tm = min(tm, S)

    def body(dl_ref, zc_ref, xhc_ref, rsc_ref, xhv_ref, rsv_ref, sv_ref, clg, clb, slg, slb, w_ref,
             dy_ref, dzc_ref, dclg_ref, dclb_ref, dslg_ref, dslb_ref, dwm_ref, dsb_ref, dvn_ref):
        first = pl.program_id(0) == 0
        last = pl.program_id(0) == pl.num_programs(0) - 1
        dc, dd = dl_ref[:, 0:512], dl_ref[:, 512:1024]
        xhc = xhc_ref[...]
        lnc = xhc * clg[...] + clb[...]
        s = _sigmoid(lnc)
        dlnc = dc * s * (1.0 + lnc * (1.0 - s))
        dy_ref[...] = _ln_bwd(dlnc, xhc, rsc_ref[...], clg[...])
        _acc(dclg_ref, _colsum(dlnc * xhc))
        _acc(dclb_ref, _colsum(dlnc))
        zc = zc_ref[...]
        z = _gelu(zc)
        dsv = dd * z[:, :512]
        dsvb = dsv.astype(BF16)
        xhv = xhv_ref[...]
        vnb = (xhv * slg[...] + slb[...]).astype(BF16)

        @pl.when(first)
        def _():
            dwm_ref[...] = jnp.zeros_like(dwm_ref)
            dsb_ref[...] = jnp.zeros_like(dsb_ref)

        for g in range(4):
            wm = _masked_sg_w(w_ref, g)
            for ch in range(tm // 128):
                rs_, cs = slice(ch * 128, (ch + 1) * 128), slice(g * 128, (g + 1) * 128)
                dwm_ref[g] += _dot_nt(dsvb[rs_, cs], vnb[rs_, cs])
                dvn_ref[rs_, cs] = _dot_tn(wm, dsvb[rs_, cs])
                dsb_ref[g] += dsv[rs_, cs]
        dvn = dvn_ref[...]
        dvv = _ln_bwd(dvn, xhv, rsv_ref[...], slg[...])
        _acc(dslg_ref, _colsum(dvn * xhv))
        _acc(dslb_ref, _colsum(dvn))
        gg = _gelu_grad(zc)
        dzc_ref[:, 0:512] = (dd * sv_ref[...] * gg[:, :512]).astype(BF16)
        dzc_ref[:, 512:1024] = (dvv * gg[:, 512:]).astype(BF16)

        @pl.when(last)
        def _():
            row = lax.broadcasted_iota(jnp.int32, (128, 128), 0)
            col = lax.broadcasted_iota(jnp.int32, (128, 128), 1)
            for g in range(4):
                dwm_ref[g] = jnp.where(row >= col, dwm_ref[g], 0.0)
                dsb_ref[g] = jnp.broadcast_to(jnp.sum(dsb_ref[g], axis=1, keepdims=True), (128, 128))

    return _tok_call(
        "odd_post_bwd", body, [dl, (h, 1024, 1), xhc, rsc, xhv, rsv, sv], [cl_g, cl_b, sl_g, sl_b, sg_w],
        [_sds((S, 512)), _sds((S, 1024), BF16)],
        [_sds((1, 512)), _sds((1, 512)), _sds((1, 512)), _sds((1, 512)), _sds((4, 128, 128)), _sds((4, 128, 128))],
        tm=tm, scratch=[pltpu.VMEM((tm, 512), F32)], dep=dep)


def conv_bwd(dy, hc, h, dw, CH=128):
    S = dy.shape[0]

    def body(dy_ref, hc_ref, a_ref, g_ref, dw_ref, da_ref, dg_ref, ddw_ref, padh_ref, padd_ref, dhc_ref):
        padh_ref[0:32, :] = jnp.zeros((32, 128), F32)
        padh_ref[32:32 + S, :] = hc_ref[...]
        padd_ref[0:S, :] = dy_ref[...]
        padd_ref[S:S + 32, :] = jnp.zeros((32, 128), F32)
        taps = [jnp.zeros((1, 128), F32) for _ in range(CONV_TAPS)]
        for ch in range(S // CH):
            b0 = ch * CH
            dyc = padd_ref[b0:b0 + CH, :]
            acc = dw_ref[0:1, :] * padd_ref[b0 + 30:b0 + 30 + CH, :]
            taps[0] = taps[0] + _colsum(dyc * padh_ref[b0 + 2:b0 + 2 + CH, :])
            for k in range(1, CONV_TAPS):
                acc = acc + dw_ref[k:k + 1, :] * padd_ref[b0 + 30 - k:b0 + 30 - k + CH, :]
                taps[k] = taps[k] + _colsum(dyc * padh_ref[b0 + 2 + k:b0 + 2 + k + CH, :])
            dhc_ref[b0:b0 + CH, :] = acc
        for k in range(CONV_TAPS):
            ddw_ref[k:k + 1, :] = taps[k]
        dhc = dhc_ref[...]
        s = _sigmoid(g_ref[...])
        da_ref[...] = (dhc * s).astype(BF16)
        dg_ref[...] = (dhc * a_ref[...] * s * (1.0 - s)).astype(BF16)

    return pl.pallas_call(
        body, name="conv_bwd", grid=(4,),
        in_specs=[pl.BlockSpec((S, 128), lambda c: (0, c)),
                  pl.BlockSpec((S, 128), lambda c: (0, c)),
                  pl.BlockSpec((S, 128), lambda c: (0, c)),
                  pl.BlockSpec((S, 128), lambda c: (0, 4 + c)),
                  pl.BlockSpec((CONV_TAPS, 128), lambda c: (0, c))],
        out_specs=[pl.BlockSpec((S, 128), lambda c: (0, c)), pl.BlockSpec((S, 128), lambda c: (0, c)),
                   pl.BlockSpec((CONV_TAPS, 128), lambda c: (0, c))],
        out_shape=[_sds((S, 512), BF16), _sds((S, 512), BF16), _sds((CONV_TAPS, 512))],
        scratch_shapes=[pltpu.VMEM((S + 32, 128), F32), pltpu.VMEM((S + 32, 128), F32), pltpu.VMEM((S, 128), F32)],
        compiler_params=_cp(("arbitrary",)),
    )(dy, hc, h, h, dw)


def attn_bwd(qkv, dl, tb, T=256, dep=None):
    S = qkv.shape[0]
    T = min(T, S)
    nq = S // T

    def body(q_ref, k_ref, v_ref, do_ref, t_ref, dq_ref, dk_ref, dv_ref,
             dka_ref, dva_ref, dqa_ref, pc_ref, gc_ref, qh_ref, doh_ref):
        i = pl.program_id(0)
        hm0 = lax.broadcasted_iota(jnp.int32, (1, 128), 1) < 64
        r2 = lax.broadcasted_iota(jnp.int32, (2 * T, T), 0)
        c2 = lax.broadcasted_iota(jnp.int32, (2 * T, T), 1)
        causal = c2 < jnp.where(r2 >= T, r2 - T, r2)
        ur = lax.broadcasted_iota(jnp.int32, (T, T), 0)
        uc = lax.broadcasted_iota(jnp.int32, (T, T), 1)
        u_le = (ur <= uc).astype(BF16)
        u_lt = (ur < uc).astype(BF16)

        @pl.when(i == 0)
        def _():
            dka_ref[...] = jnp.zeros_like(dka_ref)
            dva_ref[...] = jnp.zeros_like(dva_ref)

        dqa_ref[...] = jnp.zeros_like(dqa_ref)
        gc_ref[...] = jnp.zeros_like(gc_ref)
        for pp in range(4):
            cs = slice(pp * 128, (pp + 1) * 128)
            qh_ref[pp] = _stack_heads(q_ref[:, cs] * QK_SCALE, hm0)
            doh_ref[pp] = _stack_heads(do_ref[:, cs], hm0)
            for hd in range(2):
                for half in range(T // 128):
                    pc_ref[pp, hd * T:(hd + 1) * T, half * 128:(half + 1) * 128] = t_ref[2 * pp + hd]

        def block(kb, diag):
            ks = pl.multiple_of(kb * T, T)
            cols = [slice(pp * 128, (pp + 1) * 128) for pp in range(4)]
            zs = [_dot_nt(qh_ref[pp], k_ref[pl.ds(ks, T), cols[pp]]) for pp in range(4)]
            dws = [_dot_nt(doh_ref[pp], v_ref[pl.ds(ks, T), cols[pp]]) for pp in range(4)]
            a_s, pres = [], []
            for pp in range(4):
                sp = _softplus(zs[pp])
                a_s.append(zs[pp] - sp)
                if diag:
                    sp = jnp.where(causal, sp, 0.0)
                pres.append(_cumsum_mm(sp, u_le))
            ws, gmats, gsums = [], [], []
            for pp in range(4):
                rem = pc_ref[pp]
                w = jnp.exp(a_s[pp] - rem + pres[pp])
                if diag:
                    w = jnp.where(causal, w, 0.0)
                gmat = dws[pp] * w
                ws.append(w.astype(BF16))
                gmats.append(gmat)
                gsums.append(_cumsum_mm(gmat, u_lt))
                pc_ref[pp] = rem - jnp.broadcast_to(pres[pp][:, T - 1:T], (2 * T, T))
            for pp in range(4):
                cs = cols[pp]
                sig = jnp.exp(a_s[pp])
                gex = gc_ref[pp] + gsums[pp]
                dz = gmats[pp] * (1.0 - sig) - sig * gex
                if diag:
                    dz = jnp.where(causal, dz, 0.0)
                dzb = dz.astype(BF16)
                dqa_ref[:, cs] += _dot(_unstack_k(dzb, T), _stack_heads(k_ref[pl.ds(ks, T), cs], hm0))
                dka_ref[pl.ds(ks, T), cs] += _dot_tn(dzb, qh_ref[pp])
                dva_ref[pl.ds(ks, T), cs] += _dot_tn(ws[pp], doh_ref[pp])
                gc_ref[pp] = jnp.broadcast_to(gex[:, T - 1:T] + gmats[pp][:, T - 1:T], (2 * T, T))

        def step(kb, carry):
            block(kb, False)
            return carry

        lax.fori_loop(0, i, step, 0)
        block(i, True)
        dq_ref[...] = (dqa_ref[...] * QK_SCALE).astype(BF16)

        @pl.when(i == nq - 1)
        def _():
            dk_ref[...] = dka_ref[...].astype(BF16)
            dv_ref[...] = dva_ref[...].astype(BF16)

    deps = [] if dep is None else [dep]
    call_body = body if dep is None else (lambda *refs: body(*refs[:5], *refs[6:]))
    return pl.pallas_call(
        call_body, name="attn_bwd", grid=(nq,),
        in_specs=[pl.BlockSpec((T, 512), lambda i: (i, 0)),
                  pl.BlockSpec((S, 512), lambda i: (0, 1)),
                  pl.BlockSpec((S, 512), lambda i: (0, 2)),
                  pl.BlockSpec((T, 512), lambda i: (i, 0)),
                  pl.BlockSpec((8, T, 128), lambda i: (0, i, 0))] + [ANY] * len(deps),
        out_specs=[pl.BlockSpec((T, 512), lambda i: (i, 0)),
                   pl.BlockSpec((S, 512), lambda i: (0, 0)),
                   pl.BlockSpec((S, 512), lambda i: (0, 0))],
        out_shape=[_sds((S, 512), BF16), _sds((S, 512), BF16), _sds((S, 512), BF16)],
        scratch_shapes=[pltpu.VMEM((S, 512), F32), pltpu.VMEM((S, 512), F32), pltpu.VMEM((T, 512), F32),
                        pltpu.VMEM((4, 2 * T, T), F32), pltpu.VMEM((4, 2 * T, T), F32),
                        pltpu.VMEM((4, 2 * T, 128), BF16), pltpu.VMEM((4, 2 * T, 128), BF16)],
        compiler_params=_cp(("arbitrary",)),
    )(qkv, qkv, qkv, dl, tb, *deps)


def pool_bwd(dl, pooled_b, pool_w, pool_scale, CH=256):
    S = dl.shape[0]
    CH = min(CH, S)

    def body(db_ref, pooled_ref, w_ref, sc_ref, du_ref, dw_ref, dsc_ref, pad_ref, dp_ref):
        pad_ref[S:S + 16, :] = jnp.zeros((16, 128), F32)
        for g, win in enumerate(POOL_WINDOWS):
            cs = slice(g * 128, (g + 1) * 128)
            wq = w_ref[g].astype(BF16)
            dwg = jnp.zeros((128, 128), F32)
            dsc = jnp.zeros((1, 128), F32)
            for ch in range(S // CH):
                rs_ = slice(ch * CH, (ch + 1) * CH)
                db = db_ref[rs_, cs]
                pb = pooled_ref[rs_, cs]
                dsc = dsc + _colsum(db * _dot(pb, wq))
                dmsb = (db * sc_ref[:, cs]).astype(BF16)
                dwg = dwg + _dot_tn(pb, dmsb)
                dpool = _dot_nt(dmsb, wq)
                t = ch * CH + lax.broadcasted_iota(jnp.int32, (CH, 1), 0)
                cnt = jnp.minimum(t + 1, win).astype(F32)
                dp_ref[rs_, :] = dpool
                pad_ref[rs_, :] = dpool / cnt
            dw_ref[g] = dwg
            dsc_ref[:, cs] = dsc
            for ch in range(S // CH):
                base = ch * CH
                acc = pad_ref[base:base + CH, :]
                for sft in range(1, win):
                    acc = acc + pad_ref[base + sft:base + sft + CH, :]
                du_ref[base:base + CH, cs] = (acc - dp_ref[base:base + CH, :]).astype(BF16)

    return pl.pallas_call(
        body, name="pool_bwd", grid=(1,),
        in_specs=[pl.BlockSpec((S, 512), lambda i: (0, 1)),
                  pl.BlockSpec((S, 512), lambda i: (0, 0)),
                  pl.BlockSpec((4, 128, 128), lambda i: (0, 0, 0)),
                  pl.BlockSpec((1, 512), lambda i: (0, 0))],
        out_specs=[pl.BlockSpec((S, 512), lambda i: (0, 0)),
                   pl.BlockSpec((4, 128, 128), lambda i: (0, 0, 0)),
                   pl.BlockSpec((1, 512), lambda i: (0, 0))],
        out_shape=[_sds((S, 512), BF16), _sds((4, 128, 128)), _sds((1, 512))],
        scratch_shapes=[pltpu.VMEM((S + 16, 128), F32), pltpu.VMEM((S, 128), F32)],
        compiler_params=_cp(("arbitrary",)),
    )(dl, pooled_b, pool_w, pool_scale)


def _row(a, i):
    return a[i:i + 1]


MIXER_NAMES = (("even_w_in", "even_w_out"), ("odd_w_in", "odd_w_out"))


def fwd_layer(i, xin, p_i, target, comm):
    s = {}
    W = comm.weights(("mix", i), xin)
    w_in = W[MIXER_NAMES[i][0]]
    if i == 0:
        s["h"], s["xb"], s["qkv"] = mm_in(xin, w_in, nb16=1536)
        comm.poke(("in", i), s["h"])
        s["l1"], s["tb"] = attn_fwd(s["qkv"])
        s["l2"], s["pooled"] = pool_fwd(s["h"], W["pool_w"], W["pool_scale"])
    else:
        s["h"], s["xb"] = mm_in(xin, w_in)
        comm.poke(("in", i), s["h"])
        s["y"], s["hc"] = conv_fwd(s["h"], W["conv_dw"])
        sgb_bc = jnp.broadcast_to(W["sg_b"][:, :, None], (4, 128, 128))
        (s["l1"], s["l2"], s["xhc"], s["rsc"], s["xhv"], s["rsv"], s["sv"]) = odd_post(
            s["y"], s["h"], W["conv_ln_g"], W["conv_ln_b"], W["sg_ln_g"], W["sg_ln_b"], W["sg_w"], sgb_bc)
    tok = comm.poke(("mixed", i), s["l1"])
    W = comm.weights(("out", i), s["l1"])
    x1, s["xh1"], s["rs1"] = mm_out_ln(s["l1"], s["l2"], xin, W[MIXER_NAMES[i][1]], _row(W["ln_mix_g"], i),
                                       _row(W["ln_mix_b"], i), dep=tok)
    W = comm.weights(("ffn", i), x1)
    tok = comm.poke(("up", i), x1)
    s["gate"], s["up"], s["hb"], s["x1b"] = ffn_up(x1, W["ffn_w_gate%d" % i], W["ffn_w_up%d" % i], None, dep=tok)
    W = comm.weights(("down", i), s["hb"])
    x2, s["xh2"], s["rs2"] = ffn_down_ln(s["hb"], x1, W["ffn_w_down%d" % i], None,
                                         _row(W["ln_ffn_g"], i), _row(W["ln_ffn_b"], i))
    tok = comm.poke(("ffn", i), x2)
    outs = ple_fwd(x2, p_i, W["ple_w_gate%d" % i], W["ple_w_proj%d" % i], None, _row(W["ple_b_gate"], i), target,
                   dep=tok)
    s["sg"], s["pp"], s["x2b"], s["pb"] = outs[1:5]
    return outs[0], s, outs[5:]


def bwd_layer(i, dx, s, W, comm, tok=None):
    small = {}
    dr2, dr2_b, small["ple_b_gate"], small["ln_ffn_g"], small["ln_ffn_b"], dwpg, dwpp = ple_ln_bwd(
        dx, s["sg"], s["pp"], s["x2b"], s["pb"], W["ple_w_gate%d" % i], s["xh2"], s["rs2"],
        _row(W["ln_ffn_g"], i), dep=tok)
    dxp, dwg, dwu, dwd = ffn_bwd(dr2_b, s["x1b"], s["gate"], s["up"], s["hb"], W["ffn_w_gate%d" % i],
                                 W["ffn_w_up%d" % i], W["ffn_w_down%d" % i])
    tok = comm.grads({"ple_w_gate%d" % i: dwpg, "ple_w_proj%d" % i: dwpp, "ffn_w_down%d" % i: dwd,
                      "ffn_w_gate%d" % i: dwg, "ffn_w_up%d" % i: dwu})
    iname, oname = MIXER_NAMES[i]
    dr1, dl, small["ln_mix_g"], small["ln_mix_b"], dwout = mix_bwd(
        dxp, dr2, s["xh1"], s["rs1"], s["l1"], s["l2"], _row(W["ln_mix_g"], i), W[oname], dep=tok)
    tok = comm.poke(("bwd", i), dl)
    if i == 1:
        (dy, dzc_b, small["conv_ln_g"], small["conv_ln_b"], small["sg_ln_g"], small["sg_ln_b"],
         small["sg_w"], dsb) = odd_post_bwd(dl, s["h"], s["xhc"], s["rsc"], s["xhv"], s["rsv"], s["sv"],
                                            W["conv_ln_g"], W["conv_ln_b"], W["sg_ln_g"], W["sg_ln_b"], W["sg_w"],
                                            dep=tok)
        small["sg_b"] = dsb[:, :, 0]
        da_b, dg_b, small["conv_dw"] = conv_bwd(dy, s["hc"], s["h"], W["conv_dw"])
        pieces = [(da_b, 0), (dg_b, 512), (dzc_b, 1024)]
    else:
        dq_b, dk_b, dv_b = attn_bwd(s["qkv"], dl, s["tb"], dep=tok)
        du_b, small["pool_w"], small["pool_scale"] = pool_bwd(dl, s["pooled"], W["pool_w"], W["pool_scale"])
        pieces = [(dq_b, 0), (dk_b, 512), (dv_b, 1024), (du_b, 1536)]
    dxin, dwin = dx_in(dr1, pieces, W[iname], s["xb"])
    tok = comm.grads({oname: dwout, iname: dwin})
    return dxin, small, tok


def run_layers(x, p, target, comm):
    saved, xin = [], x
    for i in range(2):
        xin, s, extra = fwd_layer(i, xin, p[i], target if i == 1 else None, comm)
        saved.append(s)
    dx, sq = extra
    W = comm.all_weights()
    per_layer = [None, None]
    tok = None
    for i in (1, 0):
        dx, per_layer[i], tok = bwd_layer(i, dx, saved[i], W, comm, tok)
    small = {}
    for k in ("ln_mix_g", "ln_mix_b", "ln_ffn_g", "ln_ffn_b", "ple_b_gate"):
        small[k] = jnp.concatenate([per_layer[0][k], per_layer[1][k]], axis=0)
    for i in range(2):
        small.update({k: v for k, v in per_layer[i].items() if k not in small})
    return sq, dx, small


def _big_table():
    t = {}
    for nm in ("even", "odd"):
        t[nm + "_w_in"] = ((1024, 2048), 1, 256, 256, nm + "_w_in", 0)
        t[nm + "_w_out"] = ((1024, 1024), 0, 128, 128, nm + "_w_out", 0)
    for l in range(2):
        t["ffn_w_gate%d" % l] = ((1024, 8 * FF_PAD), 1, FF_PAD, FF_SHARD, "ffn_w_gate", l)
        t["ffn_w_up%d" % l] = ((1024, 8 * FF_PAD), 1, FF_PAD, FF_SHARD, "ffn_w_up", l)
        t["ffn_w_down%d" % l] = ((8 * FF_PAD, 1024), 0, FF_PAD, FF_SHARD, "ffn_w_down", l)
        t["ple_w_gate%d" % l] = ((1024, 1024), 0, 128, 128, "ple_w_gate", l)
        t["ple_w_proj%d" % l] = ((256, 1024), 1, 128, 128, "ple_w_proj", l)
    return t


BIG = _big_table()
TRANSPOSED_ARGS = ("ffn_w_gate", "ffn_w_up")
SMALL_SPEC = ((N_DEV, 40, 64), 0, 1, 1)
_UP_GROUP = lambda l: ["ffn_w_gate%d" % l, "ffn_w_up%d" % l]
_DOWN_GROUP = lambda l: ["ffn_w_down%d" % l, "ple_w_gate%d" % l, "ple_w_proj%d" % l]
AG_GROUPS = (["even_w_in"], ["even_w_out"], _UP_GROUP(0), _DOWN_GROUP(0), ["odd_w_in", "odd_w_out", "small"],
             _UP_GROUP(1), _DOWN_GROUP(1))
AG_NEED = {("mix", 0): 0, ("out", 0): 1, ("ffn", 0): 2, ("down", 0): 3, ("mix", 1): 4, ("ffn", 1): 5, ("down", 1): 6}
AG_PASS = {("in", 0): 1, ("mixed", 0): 2, ("up", 0): 3, ("ffn", 0): 4, ("mixed", 1): 5, ("up", 1): 6}
ANY = pl.BlockSpec(memory_space=pl.ANY)
SEM = pl.BlockSpec(memory_space=pltpu.SEMAPHORE)


def _spec(name):
    return SMALL_SPEC if name == "small" else BIG[name]


def _win_shape(spec):
    full, axis, w = spec[:3]
    return tuple(w if d == axis else n for d, n in enumerate(full))


def _window(ref, axis, w, j):
    idx = [slice(None)] * len(ref.shape)
    idx[axis] = pl.ds(j, 1) if w == 1 else pl.ds(pl.multiple_of(j * w, w), w)
    return ref.at[tuple(idx)]


def _mesh_pos():
    return lax.axis_index("x"), lax.axis_index("y"), lax.axis_index("c")


def split_call(name, arrays, starts=(), waits=(), sems_in=(), new=(), after=None):
    n, nn, ns = len(arrays), len(new), len(starts)
    flat_sems = [s for pair in sems_in for s in pair]

    def body(*refs):
        arr = list(refs[:n])
        sin = refs[n:n + len(flat_sems)]
        outs = refs[n + len(flat_sems) + (after is not None):]
        data = arr + list(outs[n:n + nn])
        for p, k, kind, mk in waits:
            d = mk(data, sin[2 * p].at[k], sin[2 * p + 1].at[k])
            d.wait_send() if kind == "send" else d.wait_recv()
        if ns:
            send, recv = outs[n + nn], outs[n + nn + 1]
            for k, mk in enumerate(starts):
                mk(data, send.at[k], recv.at[k]).start()
        outs[-1][...] = jnp.zeros((8, 128), F32)

    sem_out = [pltpu.SemaphoreType.DMA((ns,)), pltpu.SemaphoreType.DMA((ns,))] if ns else []
    res = pl.pallas_call(
        body, name=name,
        in_specs=[ANY] * n + [SEM] * len(flat_sems) + ([ANY] if after is not None else []),
        out_specs=[ANY] * (n + nn) + [SEM] * len(sem_out) + [pl.BlockSpec(memory_space=pltpu.VMEM)],
        out_shape=[_sds(a.shape, a.dtype) for a in arrays] + list(new) + sem_out + [_sds((8, 128), F32)],
        input_output_aliases={a: a for a in range(n)},
        compiler_params=pltpu.CompilerParams(has_side_effects=pltpu.SideEffectType.DATAFLOW_SIDE_EFFECTING),
    )(*arrays, *flat_sems, *([after] if after is not None else []))
    return list(res[:n + nn]), (tuple(res[n + nn:n + nn + 2]) if ns else None), res[-1]


def _remote(src, dst, send_sem, recv_sem, dev):
    return pltpu.make_async_remote_copy(src_ref=src, dst_ref=dst, send_sem=send_sem, recv_sem=recv_sem,
                                        device_id=dev, device_id_type=MESH_T)


class Gatherer:
    def __init__(self, groups, arrays, specs, prefix):
        self.groups, self.specs, self.prefix = groups, specs, prefix
        self.names = [nm for g in groups for nm in g]
        self.arr = dict(zip(self.names, arrays))
        self.fwd_sems = {}
        self.forwarded = set()

    @staticmethod
    def _mk_first(ai, spec, k):
        def mk(refs, ss, rs):
            x, y, c = _mesh_pos()
            dev = [(x, y, 1 - c), (1 - x, y, c), (x, 1 - y, c), (1 - x, 1 - y, c)][k]
            win = _window(refs[ai], spec[1], spec[2], 4 * x + 2 * y + c)
            return _remote(win, win, ss, rs, dev)
        return mk

    @staticmethod
    def _mk_fwd(ai, spec, j):
        def mk(refs, ss, rs):
            x, y, c = _mesh_pos()
            px, py = [(1 - x, y), (x, 1 - y), (1 - x, 1 - y)][j]
            win = _window(refs[ai], spec[1], spec[2], 4 * px + 2 * py + c)
            return _remote(win, win, ss, rs, (x, y, 1 - c))
        return mk

    def start(self, after=None):
        starts = [self._mk_first(ai, self.specs[nm], k) for ai, nm in enumerate(self.names) for k in range(4)]
        arrs, self.first_sems, tok = split_call(self.prefix + "_start", [self.arr[nm] for nm in self.names],
                                                starts=starts, after=after)
        self.arr = dict(zip(self.names, arrs))
        return tok

    def forward(self, g, after=None):
        if g in self.forwarded:
            return None
        self.forwarded.add(g)
        names = self.groups[g]
        waits = [(0, 4 * self.names.index(nm) + 1 + j, "recv", self._mk_fwd(ai, self.specs[nm], j))
                 for ai, nm in enumerate(names) for j in range(3)]
        starts = [self._mk_fwd(ai, self.specs[nm], j) for ai, nm in enumerate(names) for j in range(3)]
        arrs, self.fwd_sems[g], tok = split_call(
            "%s_forward%d" % (self.prefix, g), [self.arr[nm] for nm in names], starts=starts, waits=waits,
            sems_in=[self.first_sems], after=after)
        self.arr.update(zip(names, arrs))
        return tok

    def finish(self, g, after=None):
        self.forward(g, after)
        names = self.groups[g]
        waits = []
        for ai, nm in enumerate(names):
            base = 4 * self.names.index(nm)
            waits.append((0, base, "recv", self._mk_first(ai, self.specs[nm], 0)))
            waits += [(1, 3 * ai + j, "recv", self._mk_fwd(ai, self.specs[nm], j)) for j in range(3)]
            waits += [(0, base + k, "send", self._mk_first(ai, self.specs[nm], k)) for k in range(4)]
            waits += [(1, 3 * ai + j, "send", self._mk_fwd(ai, self.specs[nm], j)) for j in range(3)]
        arrs, _, _ = split_call(
            "%s_finish%d" % (self.prefix, g), [self.arr[nm] for nm in names], waits=waits,
            sems_in=[self.first_sems, self.fwd_sems[g]], after=after)
        self.arr.update(zip(names, arrs))
        return {nm: self.arr[nm] for nm in names}


class Reducer:
    def __init__(self, cq_arr, adam):
        self.cq_arr, self.adam = cq_arr, adam
        self.groups = []
        self.n = 0
        self.last = None

    @staticmethod
    def _mk1(gi, li, spec, q):
        def mk(refs, ss, rs):
            x, y, c = _mesh_pos()
            return _remote(_window(refs[gi], spec[1], spec[2], 2 * q + (1 - c)), refs[li].at[q], ss, rs, (x, y, 1 - c))
        return mk

    @staticmethod
    def _mk2(si, li, d):
        def mk(refs, ss, rs):
            x, y, c = _mesh_pos()
            qd = lax.rem(2 * x + y + d, 4)
            return _remote(refs[si].at[d - 1], refs[li].at[3 - d], ss, rs, (lax.div(qd, 2), lax.rem(qd, 2), c))
        return mk

    def add(self, grads, after=None):
        names = list(grads)
        m = len(names)
        starts = [self._mk1(ai, m + ai, BIG[nm], q) for ai, nm in enumerate(names) for q in range(4)]
        new = [_sds((4,) + _win_shape(BIG[nm]), BF16) for nm in names]
        res, sems, tok = split_call("rs1_start%d" % self.n, [grads[nm] for nm in names], starts=starts, new=new,
                                    after=after)
        self.groups.append(dict(names=names, starts=starts, buf=res, sems=sems, stage=1, idx=self.n))
        self.n += 1
        return tok

    def step(self, after):
        tok = None
        for grp in self.groups:
            names, m = grp["names"], len(grp["names"])
            if grp["stage"] == 1:
                waits = [(0, k, kind, mk) for k, mk in enumerate(grp["starts"]) for kind in ("send", "recv")]
                res, _, _ = split_call("rs1_wait%d" % grp["idx"], grp["buf"], waits=waits, sems_in=[grp["sems"]], after=after)
                full, land1 = res[:m], res[m:]
                s1b = []
                for lo in range(0, m, 4):
                    s1b += list(add_pairs(full[lo:lo + 4], land1[lo:lo + 4], [BIG[nm] for nm in names[lo:lo + 4]],
                                          self.cq_arr))
                starts = [self._mk2(ai, m + ai, d) for ai in range(m) for d in (1, 2, 3)]
                new = [_sds(a.shape, BF16) for a in s1b]
                res, sems, tok = split_call("rs2_start%d" % grp["idx"], s1b, starts=starts, new=new, after=tok)
                grp.update(stage=2, g=full, land1=land1, starts=starts, buf=res, sems=sems)
        return tok

    def finish_oldest(self):
        for grp in self.groups:
            if grp["stage"] == 2:
                names, m = grp["names"], len(grp["names"])
                waits = [(0, k, kind, mk) for k, mk in enumerate(grp["starts"]) for kind in ("send", "recv")]
                res, _, _ = split_call("rs2_wait%d" % grp["idx"], grp["buf"], waits=waits, sems_in=[grp["sems"]],
                                       after=self.last)
                for nm, g, l1, l2 in zip(names, grp["g"], grp["land1"], res[m:]):
                    self.last = self.adam(nm, g, l1, l2, self.last)
                grp["stage"] = 3
                return True
        return False


def pack_weights(args, arg_names, small_blk, names, j_arr, dep=None):
    n_in = len(args)
    deps = [] if dep is None else [dep]

    def body(j_ref, *refs):
        for o, nm in enumerate(names):
            dst = refs[n_in + 1 + len(deps) + o]
            if nm == "small":
                dst[...] = refs[n_in][...]
                continue
            _, axis, w, valid, arg, layer = BIG[nm]
            if arg in TRANSPOSED_ARGS:
                s = refs[arg_names.index(arg)][layer]
                s = jnp.concatenate([s, jnp.zeros((w - valid, s.shape[1]), F32)], axis=0)
                dst[...] = s.T.astype(BF16)
                continue
            src = refs[arg_names.index(arg)][layer].astype(BF16)
            if valid == w:
                dst[...] = src
            else:
                dst[...] = jnp.zeros(dst.shape, BF16)
                if axis == 1:
                    dst[:, 0:valid] = src
                else:
                    dst[0:valid, :] = src

    def ispec(a):
        return pl.BlockSpec(a.shape, lambda i, j_ref: (0, 0, 0))

    def ospec(spec):
        axis, nd = spec[1], len(spec[0])
        return pl.BlockSpec(_win_shape(spec),
                            lambda i, j_ref, axis=axis, nd=nd: tuple(j_ref[0] if d == axis else 0 for d in range(nd)))

    specs = [_spec(nm) for nm in names]
    return pl.pallas_call(
        body, name="pack_weights",
        grid_spec=pltpu.PrefetchScalarGridSpec(
            num_scalar_prefetch=1, grid=(1,),
            in_specs=[ispec(a) for a in list(args) + [small_blk]] + [ANY] * len(deps),
            out_specs=[ospec(s) for s in specs]),
        out_shape=[_sds(s[0], F32 if nm == "small" else BF16) for nm, s in zip(names, specs)],
        compiler_params=_cp(("arbitrary",)),
    )(j_arr, *args, small_blk, *deps)


def add_pairs(fulls, lands, specs, cq_arr):
    def chip(d, cq):
        return lax.rem(cq[1] + d + 1, 4)

    in_specs, args = [], []
    for full, land, spec in zip(fulls, lands, specs):
        axis, w = spec[1], spec[2]
        R, C = full.shape
        for d in range(3):
            if axis == 1:
                in_specs.append(pl.BlockSpec((R, w), lambda i, cq, d=d: (0, 2 * chip(d, cq) + cq[0])))
                in_specs.append(pl.BlockSpec((None, R, w), lambda i, cq, d=d: (chip(d, cq), 0, 0)))
            else:
                in_specs.append(pl.BlockSpec((w, C), lambda i, cq, d=d: (2 * chip(d, cq) + cq[0], 0)))
                in_specs.append(pl.BlockSpec((None, w, C), lambda i, cq, d=d: (chip(d, cq), 0, 0)))
            args += [full, land]
    out_shape = [_sds((3,) + land.shape[1:], BF16) for land in lands]
    n = len(fulls)

    def body(cq_ref, *refs):
        for a in range(n):
            for d in range(3):
                own, got = refs[6 * a + 2 * d], refs[6 * a + 2 * d + 1]
                refs[6 * n + a][d] = (own[...].astype(F32) + got[...].astype(F32)).astype(BF16)

    return pl.pallas_call(
        body, name="add_pairs",
        grid_spec=pltpu.PrefetchScalarGridSpec(
            num_scalar_prefetch=1, grid=(1,), in_specs=in_specs,
            out_specs=[pl.BlockSpec(o.shape, lambda i, cq: (0, 0, 0)) for o in out_shape]),
        out_shape=out_shape,
        compiler_params=_cp(("arbitrary",)),
    )(cq_arr, *args)


def _adamw(w, g, m, v):
    m = ADAM_B1 * m + (1.0 - ADAM_B1) * g
    v = ADAM_B2 * v + (1.0 - ADAM_B2) * (g * g)
    m_hat = m / (1.0 - ADAM_B1 ** ADAM_STEP)
    v_hat = v / (1.0 - ADAM_B2 ** ADAM_STEP)
    delta = -ADAM_LR * (m_hat / (jnp.sqrt(v_hat) + ADAM_EPS) + ADAM_WD * w)
    return delta, m, v


def reduce_adamw(full, land1, land, w, m, v, spec, cq_arr, prev=None, dep=None):
    axis, win, valid, layer = spec[1], spec[2], spec[3], spec[5]
    L, R, C = w.shape
    transposed = spec[4] in TRANSPOSED_ARGS
    TL = 256
    if transposed:
        grid = (C // TL,)
        fspec = pl.BlockSpec((TL, win), lambda i, cq: (i, 2 * cq[1] + cq[0]))
        wspec = pl.BlockSpec((None, TL, win), lambda i, cq: (cq[1], i, 0))
        lspec = pl.BlockSpec((3, TL, win), lambda i, cq: (0, i, 0))
        sspec = pl.BlockSpec((None, R, TL), lambda i, cq: (layer, 0, i))
    elif axis == 1:
        tr = min(TL, R)
        grid = (R // tr,)
        fspec = pl.BlockSpec((tr, win), lambda i, cq: (i, 2 * cq[1] + cq[0]))
        wspec = pl.BlockSpec((None, tr, win), lambda i, cq: (cq[1], i, 0))
        lspec = pl.BlockSpec((3, tr, win), lambda i, cq: (0, i, 0))
        sspec = pl.BlockSpec((None, tr, C), lambda i, cq: (layer, i, 0))
    else:
        grid = (C // TL,)
        fspec = pl.BlockSpec((win, TL), lambda i, cq: (2 * cq[1] + cq[0], i))
        wspec = pl.BlockSpec((None, win, TL), lambda i, cq: (cq[1], 0, i))
        lspec = pl.BlockSpec((3, win, TL), lambda i, cq: (0, 0, i))
        sspec = pl.BlockSpec((None, R, TL), lambda i, cq: (layer, 0, i))

    def body(cq_ref, full_ref, own_ref, land_ref, w_ref, m_ref, v_ref, *rest):
        g_ref, d_ref, nm_ref, nv_ref = rest[-4:]
        if transposed:
            rd = lambda r, *lead: r[lead] if lead else r[...]
        elif axis == 1:
            rd = lambda r, *lead: r[(*lead, slice(None), slice(0, valid))]
        else:
            rd = lambda r, *lead: r[(*lead, slice(0, valid), slice(None))]
        g = rd(full_ref).astype(F32) + rd(own_ref).astype(F32)
        for k in range(3):
            g = g + rd(land_ref, k).astype(F32)
        if transposed:
            g = g.T[0:valid, :]
        g_ref[...] = g
        d, nm, nv = _adamw(w_ref[...], g, m_ref[...], v_ref[...])
        d_ref[...] = d
        nm_ref[...] = nm
        nv_ref[...] = nv

    extra = (list(prev) if prev is not None else []) + ([dep] if dep is not None else [])
    return pl.pallas_call(
        body, name="reduce_adamw",
        grid_spec=pltpu.PrefetchScalarGridSpec(
            num_scalar_prefetch=1, grid=grid,
            in_specs=[fspec, wspec, lspec, sspec, sspec, sspec] + [ANY] * len(extra), out_specs=[sspec] * 4),
        out_shape=[_sds(w.shape)] * 4,
        input_output_aliases={7 + k: k for k in range(4 if prev is not None else 0)},
        compiler_params=_cp(("arbitrary",)),
    )(cq_arr, full, land1, land, w, m, v, *extra)


def place_slot(packed, j_arr):
    R = packed.shape[0]

    def body(j_ref, src, dst):
        dst[...] = src[...]

    return pl.pallas_call(
        body, name="place_slot",
        grid_spec=pltpu.PrefetchScalarGridSpec(
            num_scalar_prefetch=1, grid=(1,),
            in_specs=[pl.BlockSpec((R, 128), lambda i, j: (0, 0))],
            out_specs=[pl.BlockSpec((None, R, 128), lambda i, j: (j[0], 0, 0))]),
        out_shape=[_sds((N_DEV, R, 128))], compiler_params=_cp(("arbitrary",)),
    )(j_arr, packed)[0]


def sum_slots(gathered):
    def body(g_ref, o_ref):
        g = g_ref[0]
        for dev in range(1, N_DEV):
            g = g + g_ref[dev]
        o_ref[...] = g

    return pl.pallas_call(body, name="sum_slots", out_shape=_sds(gathered.shape[1:]), compiler_params=_cp())(gathered)


def small_adamw(gs, wmv):
    k = len(gs)

    def body(*refs):
        for a in range(k):
            g, w, m, v = refs[4 * a:4 * a + 4]
            d, nm, nv = _adamw(w[...], g[...], m[...], v[...])
            refs[4 * k + 3 * a][...] = d
            refs[4 * k + 3 * a + 1][...] = nm
            refs[4 * k + 3 * a + 2][...] = nv

    args = [t for g, tup in zip(gs, wmv) for t in (g,) + tuple(tup)]
    out_shape = [_sds(g.shape) for g in gs for _ in range(3)]
    return pl.pallas_call(body, name="small_adamw", out_shape=out_shape, compiler_params=_cp())(*args)


WEIGHT_NAMES = ("even_w_in", "even_w_out", "pool_w", "pool_scale", "odd_w_in", "odd_w_out", "conv_dw", "conv_ln_g",
                "conv_ln_b", "sg_ln_g", "sg_ln_b", "sg_w", "sg_b", "ln_mix_g", "ln_mix_b", "ffn_w_gate", "ffn_w_up",
                "ffn_w_down", "ln_ffn_g", "ln_ffn_b", "ple_w_proj", "ple_w_gate", "ple_b_gate")
PACK_ARGS = ("even_w_in", "even_w_out", "odd_w_in", "odd_w_out", "ffn_w_gate", "ffn_w_up", "ffn_w_down",
             "ple_w_gate", "ple_w_proj")
REPLICATED = ("pool_w", "pool_scale", "sg_w", "sg_b", "ln_mix_g", "ln_mix_b", "ln_ffn_g", "ln_ffn_b", "ple_b_gate")
SHARDED_SMALL = ("conv_dw", "conv_ln_g", "conv_ln_b", "sg_ln_g", "sg_ln_b")
NATURAL = {"pool_w": (4, 128, 128), "pool_scale": (1, 512), "sg_w": (4, 128, 128), "sg_b": (4, 128),
           "ln_mix_g": (2, 1024), "ln_mix_b": (2, 1024), "ln_ffn_g": (2, 1024), "ln_ffn_b": (2, 1024),
           "ple_b_gate": (2, 1024)}


def kernel(x, p, even_w_in, even_w_out, pool_w, pool_scale, odd_w_in, odd_w_out, conv_dw, conv_ln_g, conv_ln_b, sg_ln_g, sg_ln_b, sg_w, sg_b, ln_mix_g, ln_mix_b, ffn_w_gate, ffn_w_up, ffn_w_down, ln_ffn_g, ln_ffn_b, ple_w_proj, ple_w_gate, ple_b_gate, loss_target, m_even_w_in, m_even_w_out, m_pool_w, m_pool_scale, m_odd_w_in, m_odd_w_out, m_conv_dw, m_conv_ln_g, m_conv_ln_b, m_sg_ln_g, m_sg_ln_b, m_sg_w, m_sg_b, m_ln_mix_g, m_ln_mix_b, m_ffn_w_gate, m_ffn_w_up, m_ffn_w_down, m_ln_ffn_g, m_ln_ffn_b, m_ple_w_proj, m_ple_w_gate, m_ple_b_gate, v_even_w_in, v_even_w_out, v_pool_w, v_pool_scale, v_odd_w_in, v_odd_w_out, v_conv_dw, v_conv_ln_g, v_conv_ln_b, v_sg_ln_g, v_sg_ln_b, v_sg_w, v_sg_b, v_ln_mix_g, v_ln_mix_b, v_ffn_w_gate, v_ffn_w_up, v_ffn_w_down, v_ln_ffn_g, v_ln_ffn_b, v_ple_w_proj, v_ple_w_gate, v_ple_b_gate):
    A = dict(locals())
    for arg in TRANSPOSED_ARGS:
        for pre in ("", "m_", "v_"):
            A[pre + arg] = jnp.swapaxes(A[pre + arg], 1, 2)
    mx, my, mc = _mesh_pos()
    j = 4 * mx + 2 * my + mc
    j_arr = j.astype(jnp.int32).reshape(1)
    cq_arr = jnp.stack([mc, 2 * mx + my]).astype(jnp.int32)
    res = {}

    def adam(nm, full, land1, land2, dep):
        arg = BIG[nm][4]
        res[arg] = reduce_adamw(full, land1, land2, A[arg], A["m_" + arg], A["v_" + arg], BIG[nm], cq_arr,
                                res.get(arg), dep)
        return res[arg][0]

    class Comm:
        def __init__(self):
            names = [nm for g in AG_GROUPS for nm in g]
            specs = {nm: _spec(nm) for nm in names}
            small_blk = jnp.concatenate([conv_dw[0], conv_ln_g, conv_ln_b, sg_ln_g, sg_ln_b, jnp.zeros((5, 64), F32)], axis=0)
            first = pack_weights([A[AG_GROUPS[0][0]]], AG_GROUPS[0], small_blk[None], AG_GROUPS[0], j_arr)
            self.gat0 = Gatherer(AG_GROUPS[:1], first, specs, "ag0")
            first_started = self.gat0.start()
            rest = names[len(AG_GROUPS[0]):]
            mine = pack_weights([A[k] for k in PACK_ARGS], PACK_ARGS, small_blk[None], rest, j_arr, dep=first_started)
            self.gat = Gatherer(AG_GROUPS[1:], mine, specs, "ag")
            self.rest_started = self.gat.start()
            self.red = Reducer(cq_arr, adam)
            self.W = {k: A[k].reshape(NATURAL[k]) for k in REPLICATED}

        def weights(self, stage, after):
            if stage in AG_NEED:
                g = AG_NEED[stage]
                got = self.gat0.finish(0, self.rest_started) if g == 0 else self.gat.finish(g - 1, after)
                if "small" in got:
                    sm = got.pop("small").transpose(1, 0, 2).reshape(40, 512)
                    got.update(conv_dw=sm[0:31], conv_ln_g=sm[31:32], conv_ln_b=sm[32:33], sg_ln_g=sm[33:34],
                               sg_ln_b=sm[34:35])
                self.W.update(got)
            return self.W

        def all_weights(self):
            return self.W

        def poke(self, tag, after):
            if tag in AG_PASS:
                return self.gat.forward(AG_PASS[tag] - 1, after)
            if tag[0] == "bwd":
                return self.red.step(after)
            return None

        def grads(self, grads):
            self.n_grads = getattr(self, "n_grads", 0) + 1
            if self.n_grads == 2:
                self.held = grads
                return None
            if self.n_grads == 3:
                grads = {**self.held, **grads}
            tok = self.red.step(next(iter(grads.values())))
            return self.red.add(grads, after=tok)

    comm = Comm()
    sq, dx, small = run_layers(x[0], p[:, 0], loss_target[0], comm)
    red = comm.red
    tok = red.step(dx)

    names = REPLICATED + SHARDED_SMALL
    flat = jnp.concatenate([small[k].reshape(-1) for k in names] + [jnp.sum(sq).reshape(1)])
    rows = -(-flat.shape[0] // 1024) * 8
    packed = jnp.pad(flat, (0, rows * 128 - flat.shape[0])).reshape(rows, 128)
    sg = Gatherer((["g"],), [place_slot(packed, j_arr)], {"g": ((N_DEV, rows, 128), 0, 1, 1)}, "sg")
    red.last = sg.start(after=tok)
    older = sum(grp["stage"] == 2 for grp in red.groups) - 1
    for k in range(older):
        red.finish_oldest()
        if k == 0:
            sg.forward(0, after=red.last)
    gsum_flat = sum_slots(sg.finish(0, after=red.last)["g"]).reshape(-1)
    loss = 0.5 * gsum_flat[flat.shape[0] - 1] / x.shape[-1]
    gs, off = [], 0
    for k in names:
        n = math.prod(small[k].shape)
        g = gsum_flat[off:off + n].reshape(small[k].shape)
        off += n
        if k in SHARDED_SMALL:
            g = lax.dynamic_slice_in_dim(g, j * 64, 64, axis=1)
        gs.append(g.reshape(A[k].shape))
    outs = small_adamw(gs, [(A[k], A["m_" + k], A["v_" + k]) for k in names])
    for a, k in enumerate(names):
        res[k] = (gs[a],) + tuple(outs[3 * a:3 * a + 3])
    red.last = outs[0]
    while red.finish_oldest():
        pass

    for arg in TRANSPOSED_ARGS:
        res[arg] = [jnp.swapaxes(t, 1, 2) for t in res[arg]]
    out = [loss, dx[None]]
    for part in range(4):
        out += [res[k][part] for k in WEIGHT_NAMES]
    return tuple(out)
```

```python
import math

import jax
import jax.numpy as jnp
from jax import lax
from jax.experimental import pallas as pl
from jax.experimental.pallas import tpu as pltpu

F32, BF16 = jnp.float32, jnp.bfloat16
ALPHA = 4.0 ** 0.25
LN_EPS = 1e-5
QK_SCALE = 0.125
POOL_WINDOWS = (2, 4, 8, 16)
CONV_TAPS = 31
N_DEV = 8
FF_SHARD, FF_PAD = 352, 384
ADAM_LR, ADAM_B1, ADAM_B2, ADAM_EPS, ADAM_WD, ADAM_STEP = 0.001, 0.9, 0.999, 1e-08, 0.01, 10
VMEM_LIMIT = 56 * 1024 * 1024
MESH_T = pl.DeviceIdType.MESH


def _cp(sem=None):
    return pltpu.CompilerParams(dimension_semantics=sem, vmem_limit_bytes=VMEM_LIMIT)


def _dot(a, b):
    return jnp.dot(a, b, preferred_element_type=F32)


def _dot_nt(a, b):
    return lax.dot_general(a, b, (((1,), (1,)), ((), ())), preferred_element_type=F32)


def _dot_tn(a, b):
    return lax.dot_general(a, b, (((0,), (0,)), ((), ())), preferred_element_type=F32)


def _sigmoid(x):
    return 1.0 / (1.0 + jnp.exp(-x))


def _softplus(z):
    return jnp.maximum(z, 0.0) + jnp.log(1.0 + jnp.exp(-jnp.abs(z)))


_GELU_C = math.sqrt(2.0 / math.pi)


def _gelu(x):
    return 0.5 * x * (1.0 + jnp.tanh(_GELU_C * (x + 0.044715 * x * x * x)))


def _gelu_grad(x):
    t = jnp.tanh(_GELU_C * (x + 0.044715 * x * x * x))
    return 0.5 * (1.0 + t) + 0.5 * x * (1.0 - t * t) * _GELU_C * (1.0 + 3.0 * 0.044715 * x * x)


def _ln_fwd(r, g, b):
    mu = jnp.mean(r, axis=-1, keepdims=True)
    xc = r - mu
    var = jnp.mean(xc * xc, axis=-1, keepdims=True)
    rstd = lax.rsqrt(var + LN_EPS)
    xh = xc * rstd
    return xh * g + b, xh, rstd


def _ln_bwd(dy, xh, rstd, g):
    dxh = dy * g
    m1 = jnp.mean(dxh, axis=-1, keepdims=True)
    m2 = jnp.mean(dxh * xh, axis=-1, keepdims=True)
    return rstd * (dxh - m1 - xh * m2)


def _split2(x):
    hi = x.astype(BF16)
    lo = (x - hi.astype(F32)).astype(BF16)
    return hi, lo


def _colsum(x):
    return jnp.sum(x, axis=0, keepdims=True)


def _tok_call(name, body, tiled, full, out_tiled, out_acc=(), tm=256, scratch=(), dep=None):
    def arr(t):
        return t[0] if isinstance(t, tuple) else t
    full = [t[0] if isinstance(t, tuple) and t[1] is None else t for t in full]
    S = arr(tiled[0]).shape[0]
    tm = min(tm, S)
    n_in = len(tiled) + len(full)
    deps = [] if dep is None else [dep]
    if deps:
        inner = body
        body = lambda *refs: inner(*refs[:n_in], *refs[n_in + 1:])

    def tspec(t):
        if isinstance(t, tuple):
            _, w, cb = t
            return pl.BlockSpec((tm, w), lambda i, cb=cb: (i, cb))
        return pl.BlockSpec((tm, t.shape[1]), lambda i: (i, 0))

    def fspec(t):
        if isinstance(t, tuple):
            a, l = t
            nd = a.ndim - 1
            return pl.BlockSpec((None,) + a.shape[1:], lambda i, l=l, nd=nd: (l,) + (0,) * nd)
        nd = t.ndim
        return pl.BlockSpec(t.shape, lambda i, nd=nd: (0,) * nd)

    def ospec(o):
        return pl.BlockSpec((tm, o.shape[1]), lambda i: (i, 0))

    def aspec(o):
        nd = len(o.shape)
        return pl.BlockSpec(o.shape, lambda i, nd=nd: (0,) * nd)

    outs = pl.pallas_call(
        body, name=name, grid=(S // tm,),
        in_specs=[tspec(t) for t in tiled] + [fspec(t) for t in full] + [ANY] * len(deps),
        out_specs=[ospec(o) for o in out_tiled] + [aspec(o) for o in out_acc],
        out_shape=list(out_tiled) + list(out_acc),
        scratch_shapes=list(scratch),
        compiler_params=_cp(("arbitrary",)),
    )(*[arr(t) for t in tiled], *[arr(t) for t in full], *deps)
    return outs


def _sds(shape, dtype=F32):
    return jax.ShapeDtypeStruct(tuple(shape), dtype)


def _acc(ref, val):
    @pl.when(pl.program_id(0) == 0)
    def _():
        ref[...] = val

    @pl.when(pl.program_id(0) != 0)
    def _():
        ref[...] += val


def mm_in(x, w, nb16=0):
    S, N = x.shape[0], w.shape[1]

    def body(x_ref, w_ref, h_ref, xb_ref, *hb_ref):
        xb = x_ref[...].astype(BF16)
        xb_ref[...] = xb
        h = _dot(xb, w_ref[...])
        h_ref[...] = h
        if nb16:
            hb_ref[0][...] = h[:, 0:nb16].astype(BF16)

    outs = [_sds((S, N)), _sds((S, x.shape[1]), BF16)] + ([_sds((S, nb16), BF16)] if nb16 else [])
    return _tok_call("mm_in", body, [x], [w], outs, tm=512)


def _stack_heads(x, hm0, dtype=BF16):
    return jnp.concatenate([jnp.where(hm0, x, 0), jnp.where(hm0, 0, x)], axis=0).astype(dtype)


def _unstack_k(x, T):
    return jnp.concatenate([x[0:T], x[T:2 * T]], axis=1)


def _cumsum_mm(x, u):
    n = x.shape[0]
    hi, lo = _split2(x)
    r = _dot(jnp.concatenate([hi, lo], axis=0), u)
    return r[0:n] + r[n:2 * n]


def attn_fwd(qkv, T=256):
    S = qkv.shape[0]
    T = min(T, S)
    nq = S // T

    def body(q_ref, k_ref, v_ref, o_ref, t_ref, acc_ref, c_ref, qh_ref):
        i = pl.program_id(0)
        hm0 = lax.broadcasted_iota(jnp.int32, (1, 128), 1) < 64
        r2 = lax.broadcasted_iota(jnp.int32, (2 * T, T), 0)
        c2 = lax.broadcasted_iota(jnp.int32, (2 * T, T), 1)
        causal = c2 < jnp.where(r2 >= T, r2 - T, r2)
        ur = lax.broadcasted_iota(jnp.int32, (T, T), 0)
        uc = lax.broadcasted_iota(jnp.int32, (T, T), 1)
        u_incl = (ur >= uc).astype(BF16)
        acc_ref[...] = jnp.zeros_like(acc_ref)
        c_ref[...] = jnp.zeros_like(c_ref)
        for pp in range(4):
            qh_ref[pp] = _stack_heads(q_ref[:, pp * 128:(pp + 1) * 128] * QK_SCALE, hm0)

        def block(kb, diag):
            ks = pl.multiple_of(kb * T, T)
            cols = [slice(pp * 128, (pp + 1) * 128) for pp in range(4)]
            zs = [_dot_nt(qh_ref[pp], k_ref[pl.ds(ks, T), cols[pp]]) for pp in range(4)]
            incls = []
            for pp in range(4):
                sp = _softplus(zs[pp])
                if diag:
                    sp = jnp.where(causal, sp, 0.0)
                incls.append(_cumsum_mm(sp, u_incl))
            for pp in range(4):
                c = c_ref[pp]
                w = jnp.exp(zs[pp] - incls[pp] - c)
                if diag:
                    w = jnp.where(causal, w, 0.0)
                acc_ref[:, cols[pp]] += _dot(_unstack_k(w.astype(BF16), T),
                                             _stack_heads(v_ref[pl.ds(ks, T), cols[pp]], hm0))
                c_ref[pp] = c + jnp.broadcast_to(incls[pp][:, 0:1], (2 * T, T))

        block(i, True)

        def step(jj, carry):
            block(i - 1 - jj, False)
            return carry

        lax.fori_loop(0, i, step, 0)
        o_ref[...] = acc_ref[...].astype(BF16)
        for pp in range(4):
            for hd in range(2):
                t_ref[2 * pp + hd] = c_ref[pp, hd * T:(hd + 1) * T, 0:128]

    return pl.pallas_call(
        body, name="attn_fwd", grid=(nq,),
        in_specs=[pl.BlockSpec((T, 512), lambda i: (i, 0)),
                  pl.BlockSpec((S, 512), lambda i: (0, 1)),
                  pl.BlockSpec((S, 512), lambda i: (0, 2))],
        out_specs=[pl.BlockSpec((T, 512), lambda i: (i, 0)),
                   pl.BlockSpec((8, T, 128), lambda i: (0, i, 0))],
        out_shape=[_sds((S, 512), BF16), _sds((8, S, 128))],
        scratch_shapes=[pltpu.VMEM((T, 512), F32), pltpu.VMEM((4, 2 * T, T), F32), pltpu.VMEM((4, 2 * T, 128), BF16)],
        compiler_params=_cp(("arbitrary",)),
    )(qkv, qkv, qkv)


def pool_fwd(h, pool_w, pool_scale, CH=256):
    S = h.shape[0]
    CH = min(CH, S)

    def body(u_ref, w_ref, sc_ref, b_ref, pooled_ref, pad_ref):
        pad_ref[0:16, :] = jnp.zeros((16, 512), F32)
        pad_ref[16:16 + S, :] = u_ref[...]
        for g, win in enumerate(POOL_WINDOWS):
            cs = slice(g * 128, (g + 1) * 128)
            wq = w_ref[g].astype(BF16)
            for ch in range(S // CH):
                base = ch * CH
                acc = pad_ref[16 + base:16 + base + CH, cs]
                for sft in range(1, win):
                    acc = acc + pad_ref[16 + base - sft:16 + base - sft + CH, cs]
                t = base + lax.broadcasted_iota(jnp.int32, (CH, 1), 0)
                cnt = jnp.minimum(t + 1, win).astype(F32)
                pooled = (acc / cnt - pad_ref[16 + base:16 + base + CH, cs]).astype(BF16)
                pooled_ref[base:base + CH, cs] = pooled
                b_ref[base:base + CH, cs] = (_dot(pooled, wq) * sc_ref[:, cs]).astype(BF16)

    return pl.pallas_call(
        body, name="pool_fwd", grid=(1,),
        in_specs=[pl.BlockSpec((S, 512), lambda i: (0, 3)),
                  pl.BlockSpec((4, 128, 128), lambda i: (0, 0, 0)),
                  pl.BlockSpec((1, 512), lambda i: (0, 0))],
        out_specs=[pl.BlockSpec((S, 512), lambda i: (0, 0)), pl.BlockSpec((S, 512), lambda i: (0, 0))],
        out_shape=[_sds((S, 512), BF16), _sds((S, 512), BF16)],
        scratch_shapes=[pltpu.VMEM((S + 16, 512), F32)],
        compiler_params=_cp(("arbitrary",)),
    )(h, pool_w, pool_scale)


def conv_fwd(h, dw, CH=128):
    S = h.shape[0]

    def body(a_ref, g_ref, dw_ref, y_ref, hc_ref, pad_ref):
        hc = a_ref[...] * _sigmoid(g_ref[...])
        hc_ref[...] = hc
        pad_ref[0:32, :] = jnp.zeros((32, 128), F32)
        pad_ref[32:32 + S, :] = hc
        for ch in range(S // CH):
            base = ch * CH + 2
            acc = dw_ref[0:1, :] * pad_ref[base:base + CH, :]
            for k in range(1, CONV_TAPS):
                acc = acc + dw_ref[k:k + 1, :] * pad_ref[base + k:base + k + CH, :]
            y_ref[ch * CH:(ch + 1) * CH, :] = acc

    return pl.pallas_call(
        body, name="conv_fwd", grid=(4,),
        in_specs=[pl.BlockSpec((S, 128), lambda c: (0, c)),
                  pl.BlockSpec((S, 128), lambda c: (0, 4 + c)),
                  pl.BlockSpec((CONV_TAPS, 128), lambda c: (0, c))],
        out_specs=[pl.BlockSpec((S, 128), lambda c: (0, c)), pl.BlockSpec((S, 128), lambda c: (0, c))],
        out_shape=[_sds((S, 512)), _sds((S, 512))],
        scratch_shapes=[pltpu.VMEM((S + 32, 128), F32)],
        compiler_params=_cp(("arbitrary",)),
    )(h, h, dw)


def _masked_sg_w(w_ref, g):
    row = lax.broadcasted_iota(jnp.int32, (128, 128), 0)
    col = lax.broadcasted_iota(jnp.int32, (128, 128), 1)
    return jnp.where(row >= col, w_ref[g], 0.0).astype(BF16)


def odd_post(y, h, cl_g, cl_b, sl_g, sl_b, sg_w, sgb_bc, tm=256):
    S = y.shape[0]
    tm = min(tm, S)

    def body(y_ref, zc_ref, clg, clb, slg, slb, w_ref, sb_ref,
             c_ref, d_ref, xhc_ref, rsc_ref, xhv_ref, rsv_ref, sv_ref):
        lnc, xhc, rsc = _ln_fwd(y_ref[...], clg[...], clb[...])
        c_ref[...] = (lnc * _sigmoid(lnc)).astype(BF16)
        xhc_ref[...] = xhc
        rsc_ref[...] = rsc
        z = _gelu(zc_ref[...])
        vn, xhv, rsv = _ln_fwd(z[:, 512:], slg[...], slb[...])
        xhv_ref[...] = xhv
        rsv_ref[...] = rsv
        vnb = vn.astype(BF16)
        for g in range(4):
            wm = _masked_sg_w(w_ref, g)
            for ch in range(tm // 128):
                rs, cs = slice(ch * 128, (ch + 1) * 128), slice(g * 128, (g + 1) * 128)
                sv_ref[rs, cs] = _dot(wm, vnb[rs, cs]) + sb_ref[g]
        d_ref[...] = (z[:, :512] * sv_ref[...]).astype(BF16)

    return _tok_call(
        "odd_post", body, [y, (h, 1024, 1)], [cl_g, cl_b, sl_g, sl_b, sg_w, sgb_bc],
        [_sds((S, 512), BF16), _sds((S, 512), BF16), _sds((S, 512)), _sds((S, 1)),
         _sds((S, 512)), _sds((S, 1)), _sds((S, 512))], tm=tm)


def mm_out_ln(l1, l2, x, w, g, b, dep=None):
    S, D = x.shape

    def body(l1_ref, l2_ref, x_ref, w_ref, g_ref, b_ref, y_ref, xh_ref, rs_ref):
        mix = _dot(l1_ref[...], w_ref[0:512, :]) + _dot(l2_ref[...], w_ref[512:1024, :])
        y, xh, rs = _ln_fwd(ALPHA * x_ref[...] + mix, g_ref[...], b_ref[...])
        y_ref[...] = y
        xh_ref[...] = xh
        rs_ref[...] = rs

    return _tok_call("mm_out_ln", body, [l1, l2, x], [w, g, b],
                     [_sds((S, D)), _sds((S, D)), _sds((S, 1))], dep=dep, tm=512)


def ffn_up(x1, wg, wu, layer, dep=None):
    S, D = x1.shape
    F = wg.shape[-1]

    def body(x_ref, wg_ref, wu_ref, gate_ref, up_ref, hb_ref, xb_ref):
        xb = x_ref[...].astype(BF16)
        xb_ref[...] = xb
        gate = _dot(xb, wg_ref[...])
        up = _dot(xb, wu_ref[...])
        gate_ref[...] = gate.astype(BF16)
        up_ref[...] = up.astype(BF16)
        hb_ref[...] = (gate * _sigmoid(gate) * up).astype(BF16)

    return _tok_call("ffn_up", body, [x1], [(wg, layer), (wu, layer)],
                     [_sds((S, F), BF16), _sds((S, F), BF16), _sds((S, F), BF16), _sds((S, D), BF16)], dep=dep)


def ffn_down_ln(hb, x1, wd, layer, g, b):
    S, D = x1.shape

    def body(h_ref, x_ref, w_ref, g_ref, b_ref, y_ref, xh_ref, rs_ref):
        f = _dot(h_ref[...], w_ref[...])
        y, xh, rs = _ln_fwd(ALPHA * x_ref[...] + f, g_ref[...], b_ref[...])
        y_ref[...] = y
        xh_ref[...] = xh
        rs_ref[...] = rs

    return _tok_call("ffn_down_ln", body, [hb, x1], [(wd, layer), g, b],
                     [_sds((S, D)), _sds((S, D)), _sds((S, 1))], tm=512)


def ple_fwd(x2, p, wpg, wpp, layer, bg, target=None, dep=None):
    S, D = x2.shape
    last = target is not None

    def body(*refs):
        if last:
            x_ref, p_ref, t_ref, wg_ref, wp_ref, b_ref, x3_ref, sg_ref, pp_ref, xb_ref, pb_ref, dy_ref, ls_ref = refs
        else:
            x_ref, p_ref, wg_ref, wp_ref, b_ref, x3_ref, sg_ref, pp_ref, xb_ref, pb_ref = refs
        x = x_ref[...]
        xb = x.astype(BF16)
        pb = p_ref[...].astype(BF16)
        xb_ref[...] = xb
        pb_ref[...] = pb
        sg = _sigmoid(_dot(xb, wg_ref[...]) + b_ref[...])
        pp = _dot(pb, wp_ref[...])
        sg_ref[...] = sg.astype(BF16)
        pp_ref[...] = pp.astype(BF16)
        x3 = x + sg * pp
        x3_ref[...] = x3
        if last:
            err = x3 - t_ref[...]
            dy_ref[...] = err * (1.0 / D)
            _acc(ls_ref, _colsum(err * err))

    outs = [_sds((S, D)), _sds((S, D), BF16), _sds((S, D), BF16), _sds((S, D), BF16), _sds((S, p.shape[1]), BF16)]
    tiled = [x2, p] + ([target] if last else [])
    if last:
        outs.append(_sds((S, D)))
    return _tok_call("ple_fwd", body, tiled, [(wpg, layer), (wpp, layer), bg], outs,
                     [_sds((1, D))] if last else [], dep=dep, tm=512)


def ple_ln_bwd(dx3, sg, pp, x2b, pb, wpg, xh, rs, g, dep=None):
    S, D = dx3.shape

    def body(d_ref, sg_ref, pp_ref, x2b_ref, pb_ref, xh_ref, rs_ref, w_ref, g_ref,
             dr_ref, drb_ref, dbg_ref, dlg_ref, dlb_ref, dwg_ref, dwp_ref, accg_ref, accp_ref):
        fin_g = _sum_steps(accg_ref, dwg_ref)
        fin_p = _sum_steps(accp_ref, dwp_ref)
        d, sg = d_ref[...], sg_ref[...].astype(F32)
        dgp = d * pp_ref[...].astype(F32) * sg * (1.0 - sg)
        dgpb = dgp.astype(BF16)
        accg_ref[...] += _dot_tn(x2b_ref[...], dgpb)
        accp_ref[...] += _dot_tn(pb_ref[...], (d * sg).astype(BF16))
        _acc(dbg_ref, _colsum(dgp))
        dx2 = d + _dot_nt(dgpb, w_ref[...])
        xh = xh_ref[...]
        dr = _ln_bwd(dx2, xh, rs_ref[...], g_ref[...])
        dr_ref[...] = dr
        drb_ref[...] = dr.astype(BF16)
        _acc(dlg_ref, _colsum(dx2 * xh))
        _acc(dlb_ref, _colsum(dx2))
        fin_g()
        fin_p()

    P = pb.shape[1]
    return _tok_call("ple_ln_bwd", body, [dx3, sg, pp, x2b, pb, xh, rs], [wpg, g],
                     [_sds((S, D)), _sds((S, D), BF16)],
                     [_sds((1, D)), _sds((1, D)), _sds((1, D)), _sds((D, D), BF16), _sds((P, D), BF16)],
                     scratch=[pltpu.VMEM((D, D), F32), pltpu.VMEM((P, D), F32)], dep=dep, tm=ACC_TM)


def ffn_bwd(dr_b, x1b, gate, up, hb, wg, wu, wd, TH=512):
    S, D = dr_b.shape
    F = gate.shape[1]

    def body(dr_hbm, x_hbm, gate_ref, up_ref, hb_ref, wg_ref, wu_ref, wd_ref,
             dx_ref, dwg_ref, dwu_ref, dwd_ref, dr_v, x_v, sem, dg_s, du_s):
        @pl.when(pl.program_id(0) == 0)
        def _():
            c1 = pltpu.make_async_copy(dr_hbm, dr_v, sem.at[0])
            c2 = pltpu.make_async_copy(x_hbm, x_v, sem.at[1])
            c1.start()
            c2.start()
            c1.wait()
            c2.wait()
            dx_ref[...] = jnp.zeros_like(dx_ref)

        for ch in range(S // CH):
            rows = slice(ch * CH, (ch + 1) * CH)
            dh = _dot_nt(dr_v[rows, :], wd_ref[...])
            g, u = gate_ref[rows, :].astype(F32), up_ref[rows, :].astype(F32)
            s = _sigmoid(g)
            dgb = (dh * u * s * (1.0 + g * (1.0 - s))).astype(BF16)
            dub = (dh * g * s).astype(BF16)
            dg_s[rows, :] = dgb
            du_s[rows, :] = dub
            dx_ref[rows, :] += _dot_nt(dgb, wg_ref[...]) + _dot_nt(dub, wu_ref[...])
        x = x_v[...]
        dwg_ref[...] = _dot_tn(x, dg_s[...]).astype(BF16)
        dwu_ref[...] = _dot_tn(x, du_s[...]).astype(BF16)
        dwd_ref[...] = _dot_tn(hb_ref[...], dr_v[...]).astype(BF16)

    CH = min(256, S)
    col = lambda rows: pl.BlockSpec((rows, TH), lambda j: (0, j))
    row = pl.BlockSpec((TH, D), lambda j: (j, 0))
    return pl.pallas_call(
        body, name="ffn_bwd", grid=(F // TH,),
        in_specs=[ANY, ANY, col(S), col(S), col(S), col(D), col(D), row],
        out_specs=[pl.BlockSpec((S, D), lambda j: (0, 0)), col(D), col(D), row],
        out_shape=[_sds((S, D)), _sds((D, F), BF16), _sds((D, F), BF16), _sds((F, D), BF16)],
        scratch_shapes=[pltpu.VMEM((S, D), BF16), pltpu.VMEM((S, D), BF16), pltpu.SemaphoreType.DMA((2,)),
                        pltpu.VMEM((S, TH), BF16), pltpu.VMEM((S, TH), BF16)],
        compiler_params=pltpu.CompilerParams(dimension_semantics=("arbitrary",), vmem_limit_bytes=60 * 1024 * 1024),
    )(dr_b, x1b, gate, up, hb, wg, wu, wd)


ACC_TM = 512


def _sum_steps(acc_ref, out_ref):
    @pl.when(pl.program_id(0) == 0)
    def _():
        acc_ref[...] = jnp.zeros_like(acc_ref)

    def finish():
        @pl.when(pl.program_id(0) == pl.num_programs(0) - 1)
        def _():
            out_ref[...] = acc_ref[...].astype(BF16)
    return finish


def mix_bwd(dxp, dr2, xh, rs, l1, l2, g, w, dep=None):
    S, D = dxp.shape
    K1 = l1.shape[1]

    def body(dxp_ref, dr2_ref, xh_ref, rs_ref, l1_ref, l2_ref, g_ref, w_ref,
             dr_ref, dl_ref, dlg_ref, dlb_ref, dw_ref, acc_ref):
        finish = _sum_steps(acc_ref, dw_ref)
        d, xh = ALPHA * dr2_ref[...] + dxp_ref[...], xh_ref[...]
        dr = _ln_bwd(d, xh, rs_ref[...], g_ref[...])
        drb = dr.astype(BF16)
        dr_ref[...] = dr
        dl_ref[...] = _dot_nt(drb, w_ref[...])
        _acc(dlg_ref, _colsum(d * xh))
        _acc(dlb_ref, _colsum(d))
        acc_ref[0:K1, :] += _dot_tn(l1_ref[...], drb)
        acc_ref[K1:, :] += _dot_tn(l2_ref[...], drb)
        finish()

    return _tok_call("mix_bwd", body, [dxp, dr2, xh, rs, l1, l2], [g, w],
                     [_sds((S, D)), _sds((S, D))], [_sds((1, D)), _sds((1, D)), _sds(w.shape, BF16)],
                     scratch=[pltpu.VMEM(w.shape, F32)], dep=dep, tm=ACC_TM)


def dx_in(dr, pieces, w, xb):
    S, D = dr.shape
    offs = [o for _, o in pieces]
    widths = [a.shape[1] for a, _ in pieces]
    npc = len(pieces)

    def body(*refs):
        dr_ref, prefs, xb_ref, w_ref = refs[0], refs[1:1 + npc], refs[1 + npc], refs[2 + npc]
        dx_ref, dw_ref, acc_ref = refs[3 + npc:]
        finish = _sum_steps(acc_ref, dw_ref)
        acc = ALPHA * dr_ref[...]
        xb_t = xb_ref[...]
        for pr, o, n in zip(prefs, offs, widths):
            piece = pr[...]
            acc = acc + _dot_nt(piece, w_ref[:, o:o + n])
            acc_ref[:, o:o + n] += _dot_tn(xb_t, piece)
        dx_ref[...] = acc
        finish()

    return _tok_call("dx_in", body, [dr] + [a for a, _ in pieces] + [xb], [w], [_sds((S, D))],
                     [_sds(w.shape, BF16)], scratch=[pltpu.VMEM(w.shape, F32)], tm=ACC_TM)


def odd_post_bwd(dl, h, xhc, rsc, xhv, rsv, sv, cl_g, cl_b, sl_g, sl_b, sg_w, tm=256, dep=None):
    S = dl.shape[0]
    tm = min(tm, S)

    def body(dl_ref, zc_ref, xhc_ref, rsc_ref, xhv_ref, rsv_ref, sv_ref, clg, clb, slg, slb, w_ref,
             dy_ref, dzc_ref, dclg_ref, dclb_ref, dslg_ref, dslb_ref, dwm_ref, dsb_ref, dvn_ref):
        first = pl.program_id(0) == 0
        last = pl.program_id(0) == pl.num_programs(0) - 1
        dc, dd = dl_ref[:, 0:512], dl_ref[:, 512:1024]
        xhc = xhc_ref[...]
        lnc = xhc * clg[...] + clb[...]
        s = _sigmoid(lnc)
        dlnc = dc * s * (1.0 + lnc * (1.0 - s))
        dy_ref[...] = _ln_bwd(dlnc, xhc, rsc_ref[...], clg[...])
        _acc(dclg_ref, _colsum(dlnc * xhc))
        _acc(dclb_ref, _colsum(dlnc))
        zc = zc_ref[...]
        z = _gelu(zc)
        dsv = dd * z[:, :512]
        dsvb = dsv.astype(BF16)
        xhv = xhv_ref[...]
        vnb = (xhv * slg[...] + slb[...]).astype(BF16)

        @pl.when(first)
        def _():
            dwm_ref[...] = jnp.zeros_like(dwm_ref)
            dsb_ref[...] = jnp.zeros_like(dsb_ref)

        for g in range(4):
            wm = _masked_sg_w(w_ref, g)
            for ch in range(tm // 128):
                rs_, cs = slice(ch * 128, (ch + 1) * 128), slice(g * 128, (g + 1) * 128)
                dwm_ref[g] += _dot_nt(dsvb[rs_, cs], vnb[rs_, cs])
                dvn_ref[rs_, cs] = _dot_tn(wm, dsvb[rs_, cs])
                dsb_ref[g] += dsv[rs_, cs]
        dvn = dvn_ref[...]
        dvv = _ln_bwd(dvn, xhv, rsv_ref[...], slg[...])
        _acc(dslg_ref, _colsum(dvn * xhv))
        _acc(dslb_ref, _colsum(dvn))
        gg = _gelu_grad(zc)
        dzc_ref[:, 0:512] = (dd * sv_ref[...] * gg[:, :512]).astype(BF16)
        dzc_ref[:, 512:1024] = (dvv * gg[:, 512:]).astype(BF16)

        @pl.when(last)
        def _():
            row = lax.broadcasted_iota(jnp.int32, (128, 128), 0)
            col = lax.broadcasted_iota(jnp.int32, (128, 128), 1)
            for g in range(4):
                dwm_ref[g] = jnp.where(row >= col, dwm_ref[g], 0.0)
                dsb_ref[g] = jnp.broadcast_to(jnp.sum(dsb_ref[g], axis=1, keepdims=True), (128, 128))

    return _tok_call(
        "odd_post_bwd", body, [dl, (h, 1024, 1), xhc, rsc, xhv, rsv, sv], [cl_g, cl_b, sl_g, sl_b, sg_w],
        [_sds((S, 512)), _sds((S, 1024), BF16)],
        [_sds((1, 512)), _sds((1, 512)), _sds((1, 512)), _sds((1, 512)), _sds((4, 128, 128)), _sds((4, 128, 128))],
        tm=tm, scratch=[pltpu.VMEM((tm, 512), F32)], dep=dep)


def conv_bwd(dy, hc, h, dw, CH=128):
    S = dy.shape[0]

    def body(dy_ref, hc_ref, a_ref, g_ref, dw_ref, da_ref, dg_ref, ddw_ref, padh_ref, padd_ref, dhc_ref):
        padh_ref[0:32, :] = jnp.zeros((32, 128), F32)
        padh_ref[32:32 + S, :] = hc_ref[...]
        padd_ref[0:S, :] = dy_ref[...]
        padd_ref[S:S + 32, :] = jnp.zeros((32, 128), F32)
        taps = [jnp.zeros((1, 128), F32) for _ in range(CONV_TAPS)]
        for ch in range(S // CH):
            b0 = ch * CH
            dyc = padd_ref[b0:b0 + CH, :]
            acc = dw_ref[0:1, :] * padd_ref[b0 + 30:b0 + 30 + CH, :]
            taps[0] = taps[0] + _colsum(dyc * padh_ref[b0 + 2:b0 + 2 + CH, :])
            for k in range(1, CONV_TAPS):
                acc = acc + dw_ref[k:k + 1, :] * padd_ref[b0 + 30 - k:b0 + 30 - k + CH, :]
                taps[k] = taps[k] + _colsum(dyc * padh_ref[b0 + 2 + k:b0 + 2 + k + CH, :])
            dhc_ref[b0:b0 + CH, :] = acc
        for k in range(CONV_TAPS):
            ddw_ref[k:k + 1, :] = taps[k]
        dhc = dhc_ref[...]
        s = _sigmoid(g_ref[...])
        da_ref[...] = (dhc * s).astype(BF16)
        dg_ref[...] = (dhc * a_ref[...] * s * (1.0 - s)).astype(BF16)

    return pl.pallas_call(
        body, name="conv_bwd", grid=(4,),
        in_specs=[pl.BlockSpec((S, 128), lambda c: (0, c)),
                  pl.BlockSpec((S, 128), lambda c: (0, c)),
                  pl.BlockSpec((S, 128), lambda c: (0, c)),
                  pl.BlockSpec((S, 128), lambda c: (0, 4 + c)),
                  pl.BlockSpec((CONV_TAPS, 128), lambda c: (0, c))],
        out_specs=[pl.BlockSpec((S, 128), lambda c: (0, c)), pl.BlockSpec((S, 128), lambda c: (0, c)),
                   pl.BlockSpec((CONV_TAPS, 128), lambda c: (0, c))],
        out_shape=[_sds((S, 512), BF16), _sds((S, 512), BF16), _sds((CONV_TAPS, 512))],
        scratch_shapes=[pltpu.VMEM((S + 32, 128), F32), pltpu.VMEM((S + 32, 128), F32), pltpu.VMEM((S, 128), F32)],
        compiler_params=_cp(("arbitrary",)),
    )(dy, hc, h, h, dw)


def attn_bwd(qkv, dl, tb, T=256, dep=None):
    S = qkv.shape[0]
    T = min(T, S)
    nq = S // T

    def body(q_ref, k_ref, v_ref, do_ref, t_ref, dq_ref, dk_ref, dv_ref,
             dka_ref, dva_ref, dqa_ref, pc_ref, gc_ref, qh_ref, doh_ref):
        i = pl.program_id(0)
        hm0 = lax.broadcasted_iota(jnp.int32, (1, 128), 1) < 64
        r2 = lax.broadcasted_iota(jnp.int32, (2 * T, T), 0)
        c2 = lax.broadcasted_iota(jnp.int32, (2 * T, T), 1)
        causal = c2 < jnp.where(r2 >= T, r2 - T, r2)
        ur = lax.broadcasted_iota(jnp.int32, (T, T), 0)
        uc = lax.broadcasted_iota(jnp.int32, (T, T), 1)
        u_le = (ur <= uc).astype(BF16)
        u_lt = (ur < uc).astype(BF16)

        @pl.when(i == 0)
        def _():
            dka_ref[...] = jnp.zeros_like(dka_ref)
            dva_ref[...] = jnp.zeros_like(dva_ref)

        dqa_ref[...] = jnp.zeros_like(dqa_ref)
        gc_ref[...] = jnp.zeros_like(gc_ref)
        for pp in range(4):
            cs = slice(pp * 128, (pp + 1) * 128)
            qh_ref[pp] = _stack_heads(q_ref[:, cs] * QK_SCALE, hm0)
            doh_ref[pp] = _stack_heads(do_ref[:, cs], hm0)
            for hd in range(2):
                for half in range(T // 128):
                    pc_ref[pp, hd * T:(hd + 1) * T, half * 128:(half + 1) * 128] = t_ref[2 * pp + hd]

        def block(kb, diag):
            ks = pl.multiple_of(kb * T, T)
            cols = [slice(pp * 128, (pp + 1) * 128) for pp in range(4)]
            zs = [_dot_nt(qh_ref[pp], k_ref[pl.ds(ks, T), cols[pp]]) for pp in range(4)]
            dws = [_dot_nt(doh_ref[pp], v_ref[pl.ds(ks, T), cols[pp]]) for pp in range(4)]
            a_s, pres = [], []
            for pp in range(4):
                sp = _softplus(zs[pp])
                a_s.append(zs[pp] - sp)
                if diag:
                    sp = jnp.where(causal, sp, 0.0)
                pres.append(_cumsum_mm(sp, u_le))
            ws, gmats, gsums = [], [], []
            for pp in range(4):
                rem = pc_ref[pp]
                w = jnp.exp(a_s[pp] - rem + pres[pp])
                if diag:
                    w = jnp.where(causal, w, 0.0)
                gmat = dws[pp] * w
                ws.append(w.astype(BF16))
                gmats.append(gmat)
                gsums.append(_cumsum_mm(gmat, u_lt))
                pc_ref[pp] = rem - jnp.broadcast_to(pres[pp][:, T - 1:T], (2 * T, T))
            for pp in range(4):
                cs = cols[pp]
                sig = jnp.exp(a_s[pp])
                gex = gc_ref[pp] + gsums[pp]
                dz = gmats[pp] * (1.0 - sig) - sig * gex
                if diag:
                    dz = jnp.where(causal, dz, 0.0)
                dzb = dz.astype(BF16)
                dqa_ref[:, cs] += _dot(_unstack_k(dzb, T), _stack_heads(k_ref[pl.ds(ks, T), cs], hm0))
                dka_ref[pl.ds(ks, T), cs] += _dot_tn(dzb, qh_ref[pp])
                dva_ref[pl.ds(ks, T), cs] += _dot_tn(ws[pp], doh_ref[pp])
                gc_ref[pp] = jnp.broadcast_to(gex[:, T - 1:T] + gmats[pp][:, T - 1:T], (2 * T, T))

        def step(kb, carry):
            block(kb, False)
            return carry

        lax.fori_loop(0, i, step, 0)
        block(i, True)
        dq_ref[...] = (dqa_ref[...] * QK_SCALE).astype(BF16)

        @pl.when(i == nq - 1)
        def _():
            dk_ref[...] = dka_ref[...].astype(BF16)
            dv_ref[...] = dva_ref[...].astype(BF16)

    deps = [] if dep is None else [dep]
    call_body = body if dep is None else (lambda *refs: body(*refs[:5], *refs[6:]))
    return pl.pallas_call(
        call_body, name="attn_bwd", grid=(nq,),
        in_specs=[pl.BlockSpec((T, 512), lambda i: (i, 0)),
                  pl.BlockSpec((S, 512), lambda i: (0, 1)),
                  pl.BlockSpec((S, 512), lambda i: (0, 2)),
                  pl.BlockSpec((T, 512), lambda i: (i, 0)),
                  pl.BlockSpec((8, T, 128), lambda i: (0, i, 0))] + [ANY] * len(deps),
        out_specs=[pl.BlockSpec((T, 512), lambda i: (i, 0)),
                   pl.BlockSpec((S, 512), lambda i: (0, 0)),
                   pl.BlockSpec((S, 512), lambda i: (0, 0))],
        out_shape=[_sds((S, 512), BF16), _sds((S, 512), BF16), _sds((S, 512), BF16)],
        scratch_shapes=[pltpu.VMEM((S, 512), F32), pltpu.VMEM((S, 512), F32), pltpu.VMEM((T, 512), F32),
                        pltpu.VMEM((4, 2 * T, T), F32), pltpu.VMEM((4, 2 * T, T), F32),
                        pltpu.VMEM((4, 2 * T, 128), BF16), pltpu.VMEM((4, 2 * T, 128), BF16)],
        compiler_params=_cp(("arbitrary",)),
    )(qkv, qkv, qkv, dl, tb, *deps)


def pool_bwd(dl, pooled_b, pool_w, pool_scale, CH=256):
    S = dl.shape[0]
    CH = min(CH, S)

    def body(db_ref, pooled_ref, w_ref, sc_ref, du_ref, dw_ref, dsc_ref, pad_ref, dp_ref):
        pad_ref[S:S + 16, :] = jnp.zeros((16, 128), F32)
        for g, win in enumerate(POOL_WINDOWS):
            cs = slice(g * 128, (g + 1) * 128)
            wq = w_ref[g].astype(BF16)
            dwg = jnp.zeros((128, 128), F32)
            dsc = jnp.zeros((1, 128), F32)
            for ch in range(S // CH):
                rs_ = slice(ch * CH, (ch + 1) * CH)
                db = db_ref[rs_, cs]
                pb = pooled_ref[rs_, cs]
                dsc = dsc + _colsum(db * _dot(pb, wq))
                dmsb = (db * sc_ref[:, cs]).astype(BF16)
                dwg = dwg + _dot_tn(pb, dmsb)
                dpool = _dot_nt(dmsb, wq)
                t = ch * CH + lax.broadcasted_iota(jnp.int32, (CH, 1), 0)
                cnt = jnp.minimum(t + 1, win).astype(F32)
                dp_ref[rs_, :] = dpool
                pad_ref[rs_, :] = dpool / cnt
            dw_ref[g] = dwg
            dsc_ref[:, cs] = dsc
            for ch in range(S // CH):
                base = ch * CH
                acc = pad_ref[base:base + CH, :]
                for sft in range(1, win):
                    acc = acc + pad_ref[base + sft:base + sft + CH, :]
                du_ref[base:base + CH, cs] = (acc - dp_ref[base:base + CH, :]).astype(BF16)

    return pl.pallas_call(
        body, name="pool_bwd", grid=(1,),
        in_specs=[pl.BlockSpec((S, 512), lambda i: (0, 1)),
                  pl.BlockSpec((S, 512), lambda i: (0, 0)),
                  pl.BlockSpec((4, 128, 128), lambda i: (0, 0, 0)),
                  pl.BlockSpec((1, 512), lambda i: (0, 0))],
        out_specs=[pl.BlockSpec((S, 512), lambda i: (0, 0)),
                   pl.BlockSpec((4, 128, 128), lambda i: (0, 0, 0)),
                   pl.BlockSpec((1, 512), lambda i: (0, 0))],
        out_shape=[_sds((S, 512), BF16), _sds((4, 128, 128)), _sds((1, 512))],
        scratch_shapes=[pltpu.VMEM((S + 16, 128), F32), pltpu.VMEM((S, 128), F32)],
        compiler_params=_cp(("arbitrary",)),
    )(dl, pooled_b, pool_w, pool_scale)


def _row(a, i):
    return a[i:i + 1]


MIXER_NAMES = (("even_w_in", "even_w_out"), ("odd_w_in", "odd_w_out"))


def fwd_layer(i, xin, p_i, target, comm):
    s = {}
    W = comm.weights(("mix", i), xin)
    w_in = W[MIXER_NAMES[i][0]]
    if i == 0:
        s["h"], s["xb"], s["qkv"] = mm_in(xin, w_in, nb16=1536)
        comm.poke(("in", i), s["h"])
        s["l1"], s["tb"] = attn_fwd(s["qkv"])
        s["l2"], s["pooled"] = pool_fwd(s["h"], W["pool_w"], W["pool_scale"])
    else:
        s["h"], s["xb"] = mm_in(xin, w_in)
        comm.poke(("in", i), s["h"])
        s["y"], s["hc"] = conv_fwd(s["h"], W["conv_dw"])
        sgb_bc = jnp.broadcast_to(W["sg_b"][:, :, None], (4, 128, 128))
        (s["l1"], s["l2"], s["xhc"], s["rsc"], s["xhv"], s["rsv"], s["sv"]) = odd_post(
            s["y"], s["h"], W["conv_ln_g"], W["conv_ln_b"], W["sg_ln_g"], W["sg_ln_b"], W["sg_w"], sgb_bc)
    tok = comm.poke(("mixed", i), s["l1"])
    W = comm.weights(("out", i), s["l1"])
    x1, s["xh1"], s["rs1"] = mm_out_ln(s["l1"], s["l2"], xin, W[MIXER_NAMES[i][1]], _row(W["ln_mix_g"], i),
                                       _row(W["ln_mix_b"], i), dep=tok)
    W = comm.weights(("ffn", i), x1)
    tok = comm.poke(("up", i), x1)
    s["gate"], s["up"], s["hb"], s["x1b"] = ffn_up(x1, W["ffn_w_gate%d" % i], W["ffn_w_up%d" % i], None, dep=tok)
    W = comm.weights(("down", i), s["hb"])
    x2, s["xh2"], s["rs2"] = ffn_down_ln(s["hb"], x1, W["ffn_w_down%d" % i], None,
                                         _row(W["ln_ffn_g"], i), _row(W["ln_ffn_b"], i))
    tok = comm.poke(("ffn", i), x2)
    outs = ple_fwd(x2, p_i, W["ple_w_gate%d" % i], W["ple_w_proj%d" % i], None, _row(W["ple_b_gate"], i), target,
                   dep=tok)
    s["sg"], s["pp"], s["x2b"], s["pb"] = outs[1:5]
    return outs[0], s, outs[5:]


def bwd_layer(i, dx, s, W, comm, tok=None):
    small = {}
    dr2, dr2_b, small["ple_b_gate"], small["ln_ffn_g"], small["ln_ffn_b"], dwpg, dwpp = ple_ln_bwd(
        dx, s["sg"], s["pp"], s["x2b"], s["pb"], W["ple_w_gate%d" % i], s["xh2"], s["rs2"],
        _row(W["ln_ffn_g"], i), dep=tok)
    dxp, dwg, dwu, dwd = ffn_bwd(dr2_b, s["x1b"], s["gate"], s["up"], s["hb"], W["ffn_w_gate%d" % i],
                                 W["ffn_w_up%d" % i], W["ffn_w_down%d" % i])
    tok = comm.grads({"ple_w_gate%d" % i: dwpg, "ple_w_proj%d" % i: dwpp, "ffn_w_down%d" % i: dwd,
                      "ffn_w_gate%d" % i: dwg, "ffn_w_up%d" % i: dwu})
    iname, oname = MIXER_NAMES[i]
    dr1, dl, small["ln_mix_g"], small["ln_mix_b"], dwout = mix_bwd(
        dxp, dr2, s["xh1"], s["rs1"], s["l1"], s["l2"], _row(W["ln_mix_g"], i), W[oname], dep=tok)
    tok = comm.poke(("bwd", i), dl)
    if i == 1:
        (dy, dzc_b, small["conv_ln_g"], small["conv_ln_b"], small["sg_ln_g"], small["sg_ln_b"],
         small["sg_w"], dsb) = odd_post_bwd(dl, s["h"], s["xhc"], s["rsc"], s["xhv"], s["rsv"], s["sv"],
                                            W["conv_ln_g"], W["conv_ln_b"], W["sg_ln_g"], W["sg_ln_b"], W["sg_w"],
                                            dep=tok)
        small["sg_b"] = dsb[:, :, 0]
        da_b, dg_b, small["conv_dw"] = conv_bwd(dy, s["hc"], s["h"], W["conv_dw"])
        pieces = [(da_b, 0), (dg_b, 512), (dzc_b, 1024)]
    else:
        dq_b, dk_b, dv_b = attn_bwd(s["qkv"], dl, s["tb"], dep=tok)
        du_b, small["pool_w"], small["pool_scale"] = pool_bwd(dl, s["pooled"], W["pool_w"], W["pool_scale"])
        pieces = [(dq_b, 0), (dk_b, 512), (dv_b, 1024), (du_b, 1536)]
    dxin, dwin = dx_in(dr1, pieces, W[iname], s["xb"])
    tok = comm.grads({oname: dwout, iname: dwin})
    return dxin, small, tok


def run_layers(x, p, target, comm):
    saved, xin = [], x
    for i in range(2):
        xin, s, extra = fwd_layer(i, xin, p[i], target if i == 1 else None, comm)
        saved.append(s)
    dx, sq = extra
    W = comm.all_weights()
    per_layer = [None, None]
    tok = None
    for i in (1, 0):
        dx, per_layer[i], tok = bwd_layer(i, dx, saved[i], W, comm, tok)
    small = {}
    for k in ("ln_mix_g", "ln_mix_b", "ln_ffn_g", "ln_ffn_b", "ple_b_gate"):
        small[k] = jnp.concatenate([per_layer[0][k], per_layer[1][k]], axis=0)
    for i in range(2):
        small.update({k: v for k, v in per_layer[i].items() if k not in small})
    return sq, dx, small


def _big_table():
    t = {}
    for nm in ("even", "odd"):
        t[nm + "_w_in"] = ((1024, 2048), 1, 256, 256, nm + "_w_in", 0)
        t[nm + "_w_out"] = ((1024, 1024), 0, 128, 128, nm + "_w_out", 0)
    for l in range(2):
        t["ffn_w_gate%d" % l] = ((1024, 8 * FF_PAD), 1, FF_PAD, FF_SHARD, "ffn_w_gate", l)
        t["ffn_w_up%d" % l] = ((1024, 8 * FF_PAD), 1, FF_PAD, FF_SHARD, "ffn_w_up", l)
        t["ffn_w_down%d" % l] = ((8 * FF_PAD, 1024), 0, FF_PAD, FF_SHARD, "ffn_w_down", l)
        t["ple_w_gate%d" % l] = ((1024, 1024), 0, 128, 128, "ple_w_gate", l)
        t["ple_w_proj%d" % l] = ((256, 1024), 1, 128, 128, "ple_w_proj", l)
    return t


BIG = _big_table()
TRANSPOSED_ARGS = ("ffn_w_gate", "ffn_w_up")
SMALL_SPEC = ((N_DEV, 40, 64), 0, 1, 1)
_UP_GROUP = lambda l: ["ffn_w_gate%d" % l, "ffn_w_up%d" % l]
_DOWN_GROUP = lambda l: ["ffn_w_down%d" % l, "ple_w_gate%d" % l, "ple_w_proj%d" % l]
AG_GROUPS = (["even_w_in"], ["even_w_out"], _UP_GROUP(0), _DOWN_GROUP(0), ["odd_w_in", "odd_w_out", "small"],
             _UP_GROUP(1), _DOWN_GROUP(1))
AG_NEED = {("mix", 0): 0, ("out", 0): 1, ("ffn", 0): 2, ("down", 0): 3, ("mix", 1): 4, ("ffn", 1): 5, ("down", 1): 6}
AG_PASS = {("in", 0): 1, ("mixed", 0): 2, ("up", 0): 3, ("ffn", 0): 4, ("mixed", 1): 5, ("up", 1): 6}
ANY = pl.BlockSpec(memory_space=pl.ANY)
SEM = pl.BlockSpec(memory_space=pltpu.SEMAPHORE)


def _spec(name):
    return SMALL_SPEC if name == "small" else BIG[name]


def _win_shape(spec):
    full, axis, w = spec[:3]
    return tuple(w if d == axis else n for d, n in enumerate(full))


def _window(ref, axis, w, j):
    idx = [slice(None)] * len(ref.shape)
    idx[axis] = pl.ds(j, 1) if w == 1 else pl.ds(pl.multiple_of(j * w, w), w)
    return ref.at[tuple(idx)]


def _mesh_pos():
    return lax.axis_index("x"), lax.axis_index("y"), lax.axis_index("c")


def split_call(name, arrays, starts=(), waits=(), sems_in=(), new=(), after=None):
    n, nn, ns = len(arrays), len(new), len(starts)
    flat_sems = [s for pair in sems_in for s in pair]

    def body(*refs):
        arr = list(refs[:n])
        sin = refs[n:n + len(flat_sems)]
        outs = refs[n + len(flat_sems) + (after is not None):]
        data = arr + list(outs[n:n + nn])
        for p, k, kind, mk in waits:
            d = mk(data, sin[2 * p].at[k], sin[2 * p + 1].at[k])
            d.wait_send() if kind == "send" else d.wait_recv()
        if ns:
            send, recv = outs[n + nn], outs[n + nn + 1]
            for k, mk in enumerate(starts):
                mk(data, send.at[k], recv.at[k]).start()
        outs[-1][...] = jnp.zeros((8, 128), F32)

    sem_out = [pltpu.SemaphoreType.DMA((ns,)), pltpu.SemaphoreType.DMA((ns,))] if ns else []
    res = pl.pallas_call(
        body, name=name,
        in_specs=[ANY] * n + [SEM] * len(flat_sems) + ([ANY] if after is not None else []),
        out_specs=[ANY] * (n + nn) + [SEM] * len(sem_out) + [pl.BlockSpec(memory_space=pltpu.VMEM)],
        out_shape=[_sds(a.shape, a.dtype) for a in arrays] + list(new) + sem_out + [_sds((8, 128), F32)],
        input_output_aliases={a: a for a in range(n)},
        compiler_params=pltpu.CompilerParams(has_side_effects=pltpu.SideEffectType.DATAFLOW_SIDE_EFFECTING),
    )(*arrays, *flat_sems, *([after] if after is not None else []))
    return list(res[:n + nn]), (tuple(res[n + nn:n + nn + 2]) if ns else None), res[-1]


def _remote(src, dst, send_sem, recv_sem, dev):
    return pltpu.make_async_remote_copy(src_ref=src, dst_ref=dst, send_sem=send_sem, recv_sem=recv_sem,
                                        device_id=dev, device_id_type=MESH_T)


class Gatherer:
    def __init__(self, groups, arrays, specs, prefix):
        self.groups, self.specs, self.prefix = groups, specs, prefix
        self.names = [nm for g in groups for nm in g]
        self.arr = dict(zip(self.names, arrays))
        self.fwd_sems = {}
        self.forwarded = set()

    @staticmethod
    def _mk_first(ai, spec, k):
        def mk(refs, ss, rs):
            x, y, c = _mesh_pos()
            dev = [(x, y, 1 - c), (1 - x, y, c), (x, 1 - y, c), (1 - x, 1 - y, c)][k]
            win = _window(refs[ai], spec[1], spec[2], 4 * x + 2 * y + c)
            return _remote(win, win, ss, rs, dev)
        return mk

    @staticmethod
    def _mk_fwd(ai, spec, j):
        def mk(refs, ss, rs):
            x, y, c = _mesh_pos()
            px, py = [(1 - x, y), (x, 1 - y), (1 - x, 1 - y)][j]
            win = _window(refs[ai], spec[1], spec[2], 4 * px + 2 * py + c)
            return _remote(win, win, ss, rs, (x, y, 1 - c))
        return mk

    def start(self, after=None):
        starts = [self._mk_first(ai, self.specs[nm], k) for ai, nm in enumerate(self.names) for k in range(4)]
        arrs, self.first_sems, tok = split_call(self.prefix + "_start", [self.arr[nm] for nm in self.names],
                                                starts=starts, after=after)
        self.arr = dict(zip(self.names, arrs))
        return tok

    def forward(self, g, after=None):
        if g in self.forwarded:
            return None
        self.forwarded.add(g)
        names = self.groups[g]
        waits = [(0, 4 * self.names.index(nm) + 1 + j, "recv", self._mk_fwd(ai, self.specs[nm], j))
                 for ai, nm in enumerate(names) for j in range(3)]
        starts = [self._mk_fwd(ai, self.specs[nm], j) for ai, nm in enumerate(names) for j in range(3)]
        arrs, self.fwd_sems[g], tok = split_call(
            "%s_forward%d" % (self.prefix, g), [self.arr[nm] for nm in names], starts=starts, waits=waits,
            sems_in=[self.first_sems], after=after)
        self.arr.update(zip(names, arrs))
        return tok

    def finish(self, g, after=None):
        self.forward(g, after)
        names = self.groups[g]
        waits = []
        for ai, nm in enumerate(names):
            base = 4 * self.names.index(nm)
            waits.append((0, base, "recv", self._mk_first(ai, self.specs[nm], 0)))
            waits += [(1, 3 * ai + j, "recv", self._mk_fwd(ai, self.specs[nm], j)) for j in range(3)]
            waits += [(0, base + k, "send", self._mk_first(ai, self.specs[nm], k)) for k in range(4)]
            waits += [(1, 3 * ai + j, "send", self._mk_fwd(ai, self.specs[nm], j)) for j in range(3)]
        arrs, _, _ = split_call(
            "%s_finish%d" % (self.prefix, g), [self.arr[nm] for nm in names], waits=waits,
            sems_in=[self.first_sems, self.fwd_sems[g]], after=after)
        self.arr.update(zip(names, arrs))
        return {nm: self.arr[nm] for nm in names}


class Reducer:
    def __init__(self, cq_arr, adam):
        self.cq_arr, self.adam = cq_arr, adam
        self.groups = []
        self.n = 0
        self.last = None

    @staticmethod
    def _mk1(gi, li, spec, q):
        def mk(refs, ss, rs):
            x, y, c = _mesh_pos()
            return _remote(_window(refs[gi], spec[1], spec[2], 2 * q + (1 - c)), refs[li].at[q], ss, rs, (x, y, 1 - c))
        return mk

    @staticmethod
    def _mk2(si, li, d):
        def mk(refs, ss, rs):
            x, y, c = _mesh_pos()
            qd = lax.rem(2 * x + y + d, 4)
            return _remote(refs[si].at[d - 1], refs[li].at[3 - d], ss, rs, (lax.div(qd, 2), lax.rem(qd, 2), c))
        return mk

    def add(self, grads, after=None):
        names = list(grads)
        m = len(names)
        starts = [self._mk1(ai, m + ai, BIG[nm], q) for ai, nm in enumerate(names) for q in range(4)]
        new = [_sds((4,) + _win_shape(BIG[nm]), BF16) for nm in names]
        res, sems, tok = split_call("rs1_start%d" % self.n, [grads[nm] for nm in names], starts=starts, new=new,
                                    after=after)
        self.groups.append(dict(names=names, starts=starts, buf=res, sems=sems, stage=1, idx=self.n))
        self.n += 1
        return tok

    def step(self, after):
        tok = None
        for grp in self.groups:
            names, m = grp["names"], len(grp["names"])
            if grp["stage"] == 1:
                waits = [(0, k, kind, mk) for k, mk in enumerate(grp["starts"]) for kind in ("send", "recv")]
                res, _, _ = split_call("rs1_wait%d" % grp["idx"], grp["buf"], waits=waits, sems_in=[grp["sems"]], after=after)
                full, land1 = res[:m], res[m:]
                s1b = []
                for lo in range(0, m, 4):
                    s1b += list(add_pairs(full[lo:lo + 4], land1[lo:lo + 4], [BIG[nm] for nm in names[lo:lo + 4]],
                                          self.cq_arr))
                starts = [self._mk2(ai, m + ai, d) for ai in range(m) for d in (1, 2, 3)]
                new = [_sds(a.shape, BF16) for a in s1b]
                res, sems, tok = split_call("rs2_start%d" % grp["idx"], s1b, starts=starts, new=new, after=tok)
                grp.update(stage=2, g=full, land1=land1, starts=starts, buf=res, sems=sems)
        return tok

    def finish_oldest(self):
        for grp in self.groups:
            if grp["stage"] == 2:
                names, m = grp["names"], len(grp["names"])
                waits = [(0, k, kind, mk) for k, mk in enumerate(grp["starts"]) for kind in ("send", "recv")]
                res, _, _ = split_call("rs2_wait%d" % grp["idx"], grp["buf"], waits=waits, sems_in=[grp["sems"]],
                                       after=self.last)
                for nm, g, l1, l2 in zip(names, grp["g"], grp["land1"], res[m:]):
                    self.last = self.adam(nm, g, l1, l2, self.last)
                grp["stage"] = 3
                return True
        return False


def pack_weights(args, arg_names, small_blk, names, j_arr, dep=None):
    n_in = len(args)
    deps = [] if dep is None else [dep]

    def body(j_ref, *refs):
        for o, nm in enumerate(names):
            dst = refs[n_in + 1 + len(deps) + o]
            if nm == "small":
                dst[...] = refs[n_in][...]
                continue
            _, axis, w, valid, arg, layer = BIG[nm]
            if arg in TRANSPOSED_ARGS:
                s = refs[arg_names.index(arg)][layer]
                s = jnp.concatenate([s, jnp.zeros((w - valid, s.shape[1]), F32)], axis=0)
                dst[...] = s.T.astype(BF16)
                continue
            src = refs[arg_names.index(arg)][layer].astype(BF16)
            if valid == w:
                dst[...] = src
            else:
                dst[...] = jnp.zeros(dst.shape, BF16)
                if axis == 1:
                    dst[:, 0:valid] = src
                else:
                    dst[0:valid, :] = src

    def ispec(a):
        return pl.BlockSpec(a.shape, lambda i, j_ref: (0, 0, 0))

    def ospec(spec):
        axis, nd = spec[1], len(spec[0])
        return pl.BlockSpec(_win_shape(spec),
                            lambda i, j_ref, axis=axis, nd=nd: tuple(j_ref[0] if d == axis else 0 for d in range(nd)))

    specs = [_spec(nm) for nm in names]
    return pl.pallas_call(
        body, name="pack_weights",
        grid_spec=pltpu.PrefetchScalarGridSpec(
            num_scalar_prefetch=1, grid=(1,),
            in_specs=[ispec(a) for a in list(args) + [small_blk]] + [ANY] * len(deps),
            out_specs=[ospec(s) for s in specs]),
        out_shape=[_sds(s[0], F32 if nm == "small" else BF16) for nm, s in zip(names, specs)],
        compiler_params=_cp(("arbitrary",)),
    )(j_arr, *args, small_blk, *deps)


def add_pairs(fulls, lands, specs, cq_arr):
    def chip(d, cq):
        return lax.rem(cq[1] + d + 1, 4)

    in_specs, args = [], []
    for full, land, spec in zip(fulls, lands, specs):
        axis, w = spec[1], spec[2]
        R, C = full.shape
        for d in range(3):
            if axis == 1:
                in_specs.append(pl.BlockSpec((R, w), lambda i, cq, d=d: (0, 2 * chip(d, cq) + cq[0])))
                in_specs.append(pl.BlockSpec((None, R, w), lambda i, cq, d=d: (chip(d, cq), 0, 0)))
            else:
                in_specs.append(pl.BlockSpec((w, C), lambda i, cq, d=d: (2 * chip(d, cq) + cq[0], 0)))
                in_specs.append(pl.BlockSpec((None, w, C), lambda i, cq, d=d: (chip(d, cq), 0, 0)))
            args += [full, land]
    out_shape = [_sds((3,) + land.shape[1:], BF16) for land in lands]
    n = len(fulls)

    def body(cq_ref, *refs):
        for a in range(n):
            for d in range(3):
                own, got = refs[6 * a + 2 * d], refs[6 * a + 2 * d + 1]
                refs[6 * n + a][d] = (own[...].astype(F32) + got[...].astype(F32)).astype(BF16)

    return pl.pallas_call(
        body, name="add_pairs",
        grid_spec=pltpu.PrefetchScalarGridSpec(
            num_scalar_prefetch=1, grid=(1,), in_specs=in_specs,
            out_specs=[pl.BlockSpec(o.shape, lambda i, cq: (0, 0, 0)) for o in out_shape]),
        out_shape=out_shape,
        compiler_params=_cp(("arbitrary",)),
    )(cq_arr, *args)


def _adamw(w, g, m, v):
    m = ADAM_B1 * m + (1.0 - ADAM_B1) * g
    v = ADAM_B2 * v + (1.0 - ADAM_B2) * (g * g)
    m_hat = m / (1.0 - ADAM_B1 ** ADAM_STEP)
    v_hat = v / (1.0 - ADAM_B2 ** ADAM_STEP)
    delta = -ADAM_LR * (m_hat / (jnp.sqrt(v_hat) + ADAM_EPS) + ADAM_WD * w)
    return delta, m, v


def reduce_adamw(full, land1, land, w, m, v, spec, cq_arr, prev=None, dep=None):
    axis, win, valid, layer = spec[1], spec[2], spec[3], spec[5]
    L, R, C = w.shape
    transposed = spec[4] in TRANSPOSED_ARGS
    TL = 256
    if transposed:
        grid = (C // TL,)
        fspec = pl.BlockSpec((TL, win), lambda i, cq: (i, 2 * cq[1] + cq[0]))
        wspec = pl.BlockSpec((None, TL, win), lambda i, cq: (cq[1], i, 0))
        lspec = pl.BlockSpec((3, TL, win), lambda i, cq: (0, i, 0))
        sspec = pl.BlockSpec((None, R, TL), lambda i, cq: (layer, 0, i))
    elif axis == 1:
        tr = min(TL, R)
        grid = (R // tr,)
        fspec = pl.BlockSpec((tr, win), lambda i, cq: (i, 2 * cq[1] + cq[0]))
        wspec = pl.BlockSpec((None, tr, win), lambda i, cq: (cq[1], i, 0))
        lspec = pl.BlockSpec((3, tr, win), lambda i, cq: (0, i, 0))
        sspec = pl.BlockSpec((None, tr, C), lambda i, cq: (layer, i, 0))
    else:
        grid = (C // TL,)
        fspec = pl.BlockSpec((win, TL), lambda i, cq: (2 * cq[1] + cq[0], i))
        wspec = pl.BlockSpec((None, win, TL), lambda i, cq: (cq[1], 0, i))
        lspec = pl.BlockSpec((3, win, TL), lambda i, cq: (0, 0, i))
        sspec = pl.BlockSpec((None, R, TL), lambda i, cq: (layer, 0, i))

    def body(cq_ref, full_ref, own_ref, land_ref, w_ref, m_ref, v_ref, *rest):
        g_ref, d_ref, nm_ref, nv_ref = rest[-4:]
        if transposed:
            rd = lambda r, *lead: r[lead] if lead else r[...]
        elif axis == 1:
            rd = lambda r, *lead: r[(*lead, slice(None), slice(0, valid))]
        else:
            rd = lambda r, *lead: r[(*lead, slice(0, valid), slice(None))]
        g = rd(full_ref).astype(F32) + rd(own_ref).astype(F32)
        for k in range(3):
            g = g + rd(land_ref, k).astype(F32)
        if transposed:
            g = g.T[0:valid, :]
        g_ref[...] = g
        d, nm, nv = _adamw(w_ref[...], g, m_ref[...], v_ref[...])
        d_ref[...] = d
        nm_ref[...] = nm
        nv_ref[...] = nv

    extra = (list(prev) if prev is not None else []) + ([dep] if dep is not None else [])
    return pl.pallas_call(
        body, name="reduce_adamw",
        grid_spec=pltpu.PrefetchScalarGridSpec(
            num_scalar_prefetch=1, grid=grid,
            in_specs=[fspec, wspec, lspec, sspec, sspec, sspec] + [ANY] * len(extra), out_specs=[sspec] * 4),
        out_shape=[_sds(w.shape)] * 4,
        input_output_aliases={7 + k: k for k in range(4 if prev is not None else 0)},
        compiler_params=_cp(("arbitrary",)),
    )(cq_arr, full, land1, land, w, m, v, *extra)


def place_slot(packed, j_arr):
    R = packed.shape[0]

    def body(j_ref, src, dst):
        dst[...] = src[...]

    return pl.pallas_call(
        body, name="place_slot",
        grid_spec=pltpu.PrefetchScalarGridSpec(
            num_scalar_prefetch=1, grid=(1,),
            in_specs=[pl.BlockSpec((R, 128), lambda i, j: (0, 0))],
            out_specs=[pl.BlockSpec((None, R, 128), lambda i, j: (j[0], 0, 0))]),
        out_shape=[_sds((N_DEV, R, 128))], compiler_params=_cp(("arbitrary",)),
    )(j_arr, packed)[0]


def sum_slots(gathered):
    def body(g_ref, o_ref):
        g = g_ref[0]
        for dev in range(1, N_DEV):
            g = g + g_ref[dev]
        o_ref[...] = g

    return pl.pallas_call(body, name="sum_slots", out_shape=_sds(gathered.shape[1:]), compiler_params=_cp())(gathered)


def small_adamw(gs, wmv):
    k = len(gs)

    def body(*refs):
        for a in range(k):
            g, w, m, v = refs[4 * a:4 * a + 4]
            d, nm, nv = _adamw(w[...], g[...], m[...], v[...])
            refs[4 * k + 3 * a][...] = d
            refs[4 * k + 3 * a + 1][...] = nm
            refs[4 * k + 3 * a + 2][...] = nv

    args = [t for g, tup in zip(gs, wmv) for t in (g,) + tuple(tup)]
    out_shape = [_sds(g.shape) for g in gs for _ in range(3)]
    return pl.pallas_call(body, name="small_adamw", out_shape=out_shape, compiler_params=_cp())(*args)


WEIGHT_NAMES = ("even_w_in", "even_w_out", "pool_w", "pool_scale", "odd_w_in", "odd_w_out", "conv_dw", "conv_ln_g",
                "conv_ln_b", "sg_ln_g", "sg_ln_b", "sg_w", "sg_b", "ln_mix_g", "ln_mix_b", "ffn_w_gate", "ffn_w_up",
                "ffn_w_down", "ln_ffn_g", "ln_ffn_b", "ple_w_proj", "ple_w_gate", "ple_b_gate")
PACK_ARGS = ("even_w_in", "even_w_out", "odd_w_in", "odd_w_out", "ffn_w_gate", "ffn_w_up", "ffn_w_down",
             "ple_w_gate", "ple_w_proj")
REPLICATED = ("pool_w", "pool_scale", "sg_w", "sg_b", "ln_mix_g", "ln_mix_b", "ln_ffn_g", "ln_ffn_b", "ple_b_gate")
SHARDED_SMALL = ("conv_dw", "conv_ln_g", "conv_ln_b", "sg_ln_g", "sg_ln_b")
NATURAL = {"pool_w": (4, 128, 128), "pool_scale": (1, 512), "sg_w": (4, 128, 128), "sg_b": (4, 128),
           "ln_mix_g": (2, 1024), "ln_mix_b": (2, 1024), "ln_ffn_g": (2, 1024), "ln_ffn_b": (2, 1024),
           "ple_b_gate": (2, 1024)}


def kernel(x, p, even_w_in, even_w_out, pool_w, pool_scale, odd_w_in, odd_w_out, conv_dw, conv_ln_g, conv_ln_b, sg_ln_g, sg_ln_b, sg_w, sg_b, ln_mix_g, ln_mix_b, ffn_w_gate, ffn_w_up, ffn_w_down, ln_ffn_g, ln_ffn_b, ple_w_proj, ple_w_gate, ple_b_gate, loss_target, m_even_w_in, m_even_w_out, m_pool_w, m_pool_scale, m_odd_w_in, m_odd_w_out, m_conv_dw, m_conv_ln_g, m_conv_ln_b, m_sg_ln_g, m_sg_ln_b, m_sg_w, m_sg_b, m_ln_mix_g, m_ln_mix_b, m_ffn_w_gate, m_ffn_w_up, m_ffn_w_down, m_ln_ffn_g, m_ln_ffn_b, m_ple_w_proj, m_ple_w_gate, m_ple_b_gate, v_even_w_in, v_even_w_out, v_pool_w, v_pool_scale, v_odd_w_in, v_odd_w_out, v_conv_dw, v_conv_ln_g, v_conv_ln_b, v_sg_ln_g, v_sg_ln_b, v_sg_w, v_sg_b, v_ln_mix_g, v_ln_mix_b, v_ffn_w_gate, v_ffn_w_up, v_ffn_w_down, v_ln_ffn_g, v_ln_ffn_b, v_ple_w_proj, v_ple_w_gate, v_ple_b_gate):
    A = dict(locals())
    for arg in TRANSPOSED_ARGS:
        for pre in ("", "m_", "v_"):
            A[pre + arg] = jnp.swapaxes(A[pre + arg], 1, 2)
    mx, my, mc = _mesh_pos()
    j = 4 * mx + 2 * my + mc
    j_arr = j.astype(jnp.int32).reshape(1)
    cq_arr = jnp.stack([mc, 2 * mx + my]).astype(jnp.int32)
    res = {}

    def adam(nm, full, land1, land2, dep):
        arg = BIG[nm][4]
        res[arg] = reduce_adamw(full, land1, land2, A[arg], A["m_" + arg], A["v_" + arg], BIG[nm], cq_arr,
                                res.get(arg), dep)
        return res[arg][0]

    class Comm:
        def __init__(self):
            names = [nm for g in AG_GROUPS for nm in g]
            specs = {nm: _spec(nm) for nm in names}
            small_blk = jnp.concatenate([conv_dw[0], conv_ln_g, conv_ln_b, sg_ln_g, sg_ln_b, jnp.zeros((5, 64), F32)], axis=0)
            first = pack_weights([A[AG_GROUPS[0][0]]], AG_GROUPS[0], small_blk[None], AG_GROUPS[0], j_arr)
            self.gat0 = Gatherer(AG_GROUPS[:1], first, specs, "ag0")
            first_started = self.gat0.start()
            rest = names[len(AG_GROUPS[0]):]
            mine = pack_weights([A[k] for k in PACK_ARGS], PACK_ARGS, small_blk[None], rest, j_arr, dep=first_started)
            self.gat = Gatherer(AG_GROUPS[1:], mine, specs, "ag")
            self.rest_started = self.gat.start()
            self.red = Reducer(cq_arr, adam)
            self.W = {k: A[k].reshape(NATURAL[k]) for k in REPLICATED}

        def weights(self, stage, after):
            if stage in AG_NEED:
                g = AG_NEED[stage]
                got = self.gat0.finish(0, self.rest_started) if g == 0 else self.gat.finish(g - 1, after)
                if "small" in got:
                    sm = got.pop("small").transpose(1, 0, 2).reshape(40, 512)
                    got.update(conv_dw=sm[0:31], conv_ln_g=sm[31:32], conv_ln_b=sm[32:33], sg_ln_g=sm[33:34],
                               sg_ln_b=sm[34:35])
                self.W.update(got)
            return self.W

        def all_weights(self):
            return self.W

        def poke(self, tag, after):
            if tag in AG_PASS:
                return self.gat.forward(AG_PASS[tag] - 1, after)
            if tag[0] == "bwd":
                return self.red.step(after)
            return None

        def grads(self, grads):
            self.n_grads = getattr(self, "n_grads", 0) + 1
            if self.n_grads == 2:
                self.held = grads
                return None
            if self.n_grads == 3:
                grads = {**self.held, **grads}
            tok = self.red.step(next(iter(grads.values())))
            return self.red.add(grads, after=tok)

    comm = Comm()
    sq, dx, small = run_layers(x[0], p[:, 0], loss_target[0], comm)
    red = comm.red
    tok = red.step(dx)

    names = REPLICATED + SHARDED_SMALL
    flat = jnp.concatenate([small[k].reshape(-1) for k in names] + [jnp.sum(sq).reshape(1)])
    rows = -(-flat.shape[0] // 1024) * 8
    packed = jnp.pad(flat, (0, rows * 128 - flat.shape[0])).reshape(rows, 128)
    sg = Gatherer((["g"],), [place_slot(packed, j_arr)], {"g": ((N_DEV, rows, 128), 0, 1, 1)}, "sg")
    red.last = sg.start(after=tok)
    older = sum(grp["stage"] == 2 for grp in red.groups) - 1
    for k in range(older):
        red.finish_oldest()
        if k == 0:
            sg.forward(0, after=red.last)
    gsum_flat = sum_slots(sg.finish(0, after=red.last)["g"]).reshape(-1)
    loss = 0.5 * gsum_flat[flat.shape[0] - 1] / x.shape[-1]
    gs, off = [], 0
    for k in names:
        n = math.prod(small[k].shape)
        g = gsum_flat[off:off + n].reshape(small[k].shape)
        off += n
        if k in SHARDED_SMALL:
            g = lax.dynamic_slice_in_dim(g, j * 64, 64, axis=1)
        gs.append(g.reshape(A[k].shape))
    outs = small_adamw(gs, [(A[k], A["m_" + k], A["v_" + k]) for k in names])
    for a, k in enumerate(names):
        res[k] = (gs[a],) + tuple(outs[3 * a:3 * a + 3])
    red.last = outs[0]
    while red.finish_oldest():
        pass

    for arg in TRANSPOSED_ARGS:
        res[arg] = [jnp.swapaxes(t, 1, 2) for t in res[arg]]
    out = [loss, dx[None]]
    for part in range(4):
        out += [res[k][part] for k in WEIGHT_NAMES]
    return tuple(out)
```

```python
import math

import jax
import jax.numpy as jnp
from jax import lax
from jax.experimental import pallas as pl
from jax.experimental.pallas import tpu as pltpu

F32, BF16 = jnp.float32, jnp.bfloat16
ALPHA = 4.0 ** 0.25
LN_EPS = 1e-5
QK_SCALE = 0.125
POOL_WINDOWS = (2, 4, 8, 16)
CONV_TAPS = 31
N_DEV = 8
FF_SHARD, FF_PAD = 352, 384
ADAM_LR, ADAM_B1, ADAM_B2, ADAM_EPS, ADAM_WD, ADAM_STEP = 0.001, 0.9, 0.999, 1e-08, 0.01, 10
VMEM_LIMIT = 56 * 1024 * 1024
MESH_T = pl.DeviceIdType.MESH


def _cp(sem=None):
    return pltpu.CompilerParams(dimension_semantics=sem, vmem_limit_bytes=VMEM_LIMIT)


def _dot(a, b):
    return jnp.dot(a, b, preferred_element_type=F32)


def _dot_nt(a, b):
    return lax.dot_general(a, b, (((1,), (1,)), ((), ())), preferred_element_type=F32)


def _dot_tn(a, b):
    return lax.dot_general(a, b, (((0,), (0,)), ((), ())), preferred_element_type=F32)


def _sigmoid(x):
    return 1.0 / (1.0 + jnp.exp(-x))


def _softplus(z):
    return jnp.maximum(z, 0.0) + jnp.log(1.0 + jnp.exp(-jnp.abs(z)))


_GELU_C = math.sqrt(2.0 / math.pi)


def _gelu(x):
    return 0.5 * x * (1.0 + jnp.tanh(_GELU_C * (x + 0.044715 * x * x * x)))


def _gelu_grad(x):
    t = jnp.tanh(_GELU_C * (x + 0.044715 * x * x * x))
    return 0.5 * (1.0 + t) + 0.5 * x * (1.0 - t * t) * _GELU_C * (1.0 + 3.0 * 0.044715 * x * x)


def _ln_fwd(r, g, b):
    mu = jnp.mean(r, axis=-1, keepdims=True)
    xc = r - mu
    var = jnp.mean(xc * xc, axis=-1, keepdims=True)
    rstd = lax.rsqrt(var + LN_EPS)
    xh = xc * rstd
    return xh * g + b, xh, rstd


def _ln_bwd(dy, xh, rstd, g):
    dxh = dy * g
    m1 = jnp.mean(dxh, axis=-1, keepdims=True)
    m2 = jnp.mean(dxh * xh, axis=-1, keepdims=True)
    return rstd * (dxh - m1 - xh * m2)


def _split2(x):
    hi = x.astype(BF16)
    lo = (x - hi.astype(F32)).astype(BF16)
    return hi, lo


def _colsum(x):
    return jnp.sum(x, axis=0, keepdims=True)


def _tok_call(name, body, tiled, full, out_tiled, out_acc=(), tm=256, scratch=(), dep=None):
    def arr(t):
        return t[0] if isinstance(t, tuple) else t
    full = [t[0] if isinstance(t, tuple) and t[1] is None else t for t in full]
    S = arr(tiled[0]).shape[0]
    tm = min(tm, S)
    n_in = len(tiled) + len(full)
    deps = [] if dep is None else [dep]
    if deps:
        inner = body
        body = lambda *refs: inner(*refs[:n_in], *refs[n_in + 1:])

    def tspec(t):
        if isinstance(t, tuple):
            _, w, cb = t
            return pl.BlockSpec((tm, w), lambda i, cb=cb: (i, cb))
        return pl.BlockSpec((tm, t.shape[1]), lambda i: (i, 0))

    def fspec(t):
        if isinstance(t, tuple):
            a, l = t
            nd = a.ndim - 1
            return pl.BlockSpec((None,) + a.shape[1:], lambda i, l=l, nd=nd: (l,) + (0,) * nd)
        nd = t.ndim
        return pl.BlockSpec(t.shape, lambda i, nd=nd: (0,) * nd)

    def ospec(o):
        return pl.BlockSpec((tm, o.shape[1]), lambda i: (i, 0))

    def aspec(o):
        nd = len(o.shape)
        return pl.BlockSpec(o.shape, lambda i, nd=nd: (0,) * nd)

    outs = pl.pallas_call(
        body, name=name, grid=(S // tm,),
        in_specs=[tspec(t) for t in tiled] + [fspec(t) for t in full] + [ANY] * len(deps),
        out_specs=[ospec(o) for o in out_tiled] + [aspec(o) for o in out_acc],
        out_shape=list(out_tiled) + list(out_acc),
        scratch_shapes=list(scratch),
        compiler_params=_cp(("arbitrary",)),
    )(*[arr(t) for t in tiled], *[arr(t) for t in full], *deps)
    return outs


def _sds(shape, dtype=F32):
    return jax.ShapeDtypeStruct(tuple(shape), dtype)


def _acc(ref, val):
    @pl.when(pl.program_id(0) == 0)
    def _():
        ref[...] = val

    @pl.when(pl.program_id(0) != 0)
    def _():
        ref[...] += val


def mm_in(x, w, nb16=0):
    S, N = x.shape[0], w.shape[1]

    def body(x_ref, w_ref, h_ref, xb_ref, *hb_ref):
        xb = x_ref[...].astype(BF16)
        xb_ref[...] = xb
        h = _dot(xb, w_ref[...])
        h_ref[...] = h
        if nb16:
            hb_ref[0][...] = h[:, 0:nb16].astype(BF16)

    outs = [_sds((S, N)), _sds((S, x.shape[1]), BF16)] + ([_sds((S, nb16), BF16)] if nb16 else [])
    return _tok_call("mm_in", body, [x], [w], outs, tm=512)


def _stack_heads(x, hm0, dtype=BF16):
    return jnp.concatenate([jnp.where(hm0, x, 0), jnp.where(hm0, 0, x)], axis=0).astype(dtype)


def _unstack_k(x, T):
    return jnp.concatenate([x[0:T], x[T:2 * T]], axis=1)


def _cumsum_mm(x, u):
    n = x.shape[0]
    hi, lo = _split2(x)
    r = _dot(jnp.concatenate([hi, lo], axis=0), u)
    return r[0:n] + r[n:2 * n]


def attn_fwd(qkv, T=256):
    S = qkv.shape[0]
    T = min(T, S)
    nq = S // T

    def body(q_ref, k_ref, v_ref, o_ref, t_ref, acc_ref, c_ref, qh_ref):
        i = pl.program_id(0)
        hm0 = lax.broadcasted_iota(jnp.int32, (1, 128), 1) < 64
        r2 = lax.broadcasted_iota(jnp.int32, (2 * T, T), 0)
        c2 = lax.broadcasted_iota(jnp.int32, (2 * T, T), 1)
        causal = c2 < jnp.where(r2 >= T, r2 - T, r2)
        ur = lax.broadcasted_iota(jnp.int32, (T, T), 0)
        uc = lax.broadcasted_iota(jnp.int32, (T, T), 1)
        u_incl = (ur >= uc).astype(BF16)
        acc_ref[...] = jnp.zeros_like(acc_ref)
        c_ref[...] = jnp.zeros_like(c_ref)
        for pp in range(4):
            qh_ref[pp] = _stack_heads(q_ref[:, pp * 128:(pp + 1) * 128] * QK_SCALE, hm0)

        def block(kb, diag):
            ks = pl.multiple_of(kb * T, T)
            cols = [slice(pp * 128, (pp + 1) * 128) for pp in range(4)]
            zs = [_dot_nt(qh_ref[pp], k_ref[pl.ds(ks, T), cols[pp]]) for pp in range(4)]
            incls = []
            for pp in range(4):
                sp = _softplus(zs[pp])
                if diag:
                    sp = jnp.where(causal, sp, 0.0)
                incls.append(_cumsum_mm(sp, u_incl))
            for pp in range(4):
                c = c_ref[pp]
                w = jnp.exp(zs[pp] - incls[pp] - c)
                if diag:
                    w = jnp.where(causal, w, 0.0)
                acc_ref[:, cols[pp]] += _dot(_unstack_k(w.astype(BF16), T),
                                             _stack_heads(v_ref[pl.ds(ks, T), cols[pp]], hm0))
                c_ref[pp] = c + jnp.broadcast_to(incls[pp][:, 0:1], (2 * T, T))

        block(i, True)

        def step(jj, carry):
            block(i - 1 - jj, False)
            return carry

        lax.fori_loop(0, i, step, 0)
        o_ref[...] = acc_ref[...].astype(BF16)
        for pp in range(4):
            for hd in range(2):
                t_ref[2 * pp + hd] = c_ref[pp, hd * T:(hd + 1) * T, 0:128]

    return pl.pallas_call(
        body, name="attn_fwd", grid=(nq,),
        in_specs=[pl.BlockSpec((T, 512), lambda i: (i, 0)),
                  pl.BlockSpec((S, 512), lambda i: (0, 1)),
                  pl.BlockSpec((S, 512), lambda i: (0, 2))],
        out_specs=[pl.BlockSpec((T, 512), lambda i: (i, 0)),
                   pl.BlockSpec((8, T, 128), lambda i: (0, i, 0))],
        out_shape=[_sds((S, 512), BF16), _sds((8, S, 128))],
        scratch_shapes=[pltpu.VMEM((T, 512), F32), pltpu.VMEM((4, 2 * T, T), F32), pltpu.VMEM((4, 2 * T, 128), BF16)],
        compiler_params=_cp(("arbitrary",)),
    )(qkv, qkv, qkv)


def pool_fwd(h, pool_w, pool_scale, CH=256):
    S = h.shape[0]
    CH = min(CH, S)

    def body(u_ref, w_ref, sc_ref, b_ref, pooled_ref, pad_ref):
        pad_ref[0:16, :] = jnp.zeros((16, 512), F32)
        pad_ref[16:16 + S, :] = u_ref[...]
        for g, win in enumerate(POOL_WINDOWS):
            cs = slice(g * 128, (g + 1) * 128)
            wq = w_ref[g].astype(BF16)
            for ch in range(S // CH):
                base = ch * CH
                acc = pad_ref[16 + base:16 + base + CH, cs]
                for sft in range(1, win):
                    acc = acc + pad_ref[16 + base - sft:16 + base - sft + CH, cs]
                t = base + lax.broadcasted_iota(jnp.int32, (CH, 1), 0)
                cnt = jnp.minimum(t + 1, win).astype(F32)
                pooled = (acc / cnt - pad_ref[16 + base:16 + base + CH, cs]).astype(BF16)
                pooled_ref[base:base + CH, cs] = pooled
                b_ref[base:base + CH, cs] = (_dot(pooled, wq) * sc_ref[:, cs]).astype(BF16)

    return pl.pallas_call(
        body, name="pool_fwd", grid=(1,),
        in_specs=[pl.BlockSpec((S, 512), lambda i: (0, 3)),
                  pl.BlockSpec((4, 128, 128), lambda i: (0, 0, 0)),
                  pl.BlockSpec((1, 512), lambda i: (0, 0))],
        out_specs=[pl.BlockSpec((S, 512), lambda i: (0, 0)), pl.BlockSpec((S, 512), lambda i: (0, 0))],
        out_shape=[_sds((S, 512), BF16), _sds((S, 512), BF16)],
        scratch_shapes=[pltpu.VMEM((S + 16, 512), F32)],
        compiler_params=_cp(("arbitrary",)),
    )(h, pool_w, pool_scale)


def conv_fwd(h, dw, CH=128):
    S = h.shape[0]

    def body(a_ref, g_ref, dw_ref, y_ref, hc_ref, pad_ref):
        hc = a_ref[...] * _sigmoid(g_ref[...])
        hc_ref[...] = hc
        pad_ref[0:32, :] = jnp.zeros((32, 128), F32)
        pad_ref[32:32 + S, :] = hc
        for ch in range(S // CH):
            base = ch * CH + 2
            acc = dw_ref[0:1, :] * pad_ref[base:base + CH, :]
            for k in range(1, CONV_TAPS):
                acc = acc + dw_ref[k:k + 1, :] * pad_ref[base + k:base + k + CH, :]
            y_ref[ch * CH:(ch + 1) * CH, :] = acc

    return pl.pallas_call(
        body, name="conv_fwd", grid=(4,),
        in_specs=[pl.BlockSpec((S, 128), lambda c: (0, c)),
                  pl.BlockSpec((S, 128), lambda c: (0, 4 + c)),
                  pl.BlockSpec((CONV_TAPS, 128), lambda c: (0, c))],
        out_specs=[pl.BlockSpec((S, 128), lambda c: (0, c)), pl.BlockSpec((S, 128), lambda c: (0, c))],
        out_shape=[_sds((S, 512)), _sds((S, 512))],
        scratch_shapes=[pltpu.VMEM((S + 32, 128), F32)],
        compiler_params=_cp(("arbitrary",)),
    )(h, h, dw)


def _masked_sg_w(w_ref, g):
    row = lax.broadcasted_iota(jnp.int32, (128, 128), 0)
    col = lax.broadcasted_iota(jnp.int32, (128, 128), 1)
    return jnp.where(row >= col, w_ref[g], 0.0).astype(BF16)


def odd_post(y, h, cl_g, cl_b, sl_g, sl_b, sg_w, sgb_bc, tm=512):
    S = y.shape[0]
    tm = min(tm, S)

    def body(y_ref, zc_ref, clg, clb, slg, slb, w_ref, sb_ref,
             c_ref, d_ref, xhc_ref, rsc_ref, xhv_ref, rsv_ref, sv_ref):
        lnc, xhc, rsc = _ln_fwd(y_ref[...], clg[...], clb[...])
        c_ref[...] = (lnc * _sigmoid(lnc)).astype(BF16)
        xhc_ref[...] = xhc
        rsc_ref[...] = rsc
        z = _gelu(zc_ref[...])
        vn, xhv, rsv = _ln_fwd(z[:, 512:], slg[...], slb[...])
        xhv_ref[...] = xhv
        rsv_ref[...] = rsv
        vnb = vn.astype(BF16)
        for g in range(4):
            wm = _masked_sg_w(w_ref, g)
            for ch in range(tm // 128):
                rs, cs = slice(ch * 128, (ch + 1) * 128), slice(g * 128, (g + 1) * 128)
                sv_ref[rs, cs] = _dot(wm, vnb[rs, cs]) + sb_ref[g]
        d_ref[...] = (z[:, :512] * sv_ref[...]).astype(BF16)

    return _tok_call(
        "odd_post", body, [y, (h, 1024, 1)], [cl_g, cl_b, sl_g, sl_b, sg_w, sgb_bc],
        [_sds((S, 512), BF16), _sds((S, 512), BF16), _sds((S, 512)), _sds((S, 1)),
         _sds((S, 512)), _sds((S, 1)), _sds((S, 512))], tm=tm)


def mm_out_ln(l1, l2, x, w, g, b, dep=None):
    S, D = x.shape

    def body(l1_ref, l2_ref, x_ref, w_ref, g_ref, b_ref, y_ref, xh_ref, rs_ref):
        mix = _dot(l1_ref[...], w_ref[0:512, :]) + _dot(l2_ref[...], w_ref[512:1024, :])
        y, xh, rs = _ln_fwd(ALPHA * x_ref[...] + mix, g_ref[...], b_ref[...])
        y_ref[...] = y
        xh_ref[...] = xh
        rs_ref[...] = rs

    return _tok_call("mm_out_ln", body, [l1, l2, x], [w, g, b],
                     [_sds((S, D)), _sds((S, D)), _sds((S, 1))], dep=dep, tm=512)


def ffn_up(x1, wg, wu, layer, dep=None):
    S, D = x1.shape
    F = wg.shape[-1]

    def body(x_ref, wg_ref, wu_ref, gate_ref, up_ref, hb_ref, xb_ref):
        xb = x_ref[...].astype(BF16)
        xb_ref[...] = xb
        gate = _dot(xb, wg_ref[...])
        up = _dot(xb, wu_ref[...])
        gate_ref[...] = gate.astype(BF16)
        up_ref[...] = up.astype(BF16)
        hb_ref[...] = (gate * _sigmoid(gate) * up).astype(BF16)

    return _tok_call("ffn_up", body, [x1], [(wg, layer), (wu, layer)],
                     [_sds((S, F), BF16), _sds((S, F), BF16), _sds((S, F), BF16), _sds((S, D), BF16)], dep=dep)


def ffn_down_ln(hb, x1, wd, layer, g, b):
    S, D = x1.shape

    def body(h_ref, x_ref, w_ref, g_ref, b_ref, y_ref, xh_ref, rs_ref):
        f = _dot(h_ref[...], w_ref[...])
        y, xh, rs = _ln_fwd(ALPHA * x_ref[...] + f, g_ref[...], b_ref[...])
        y_ref[...] = y
        xh_ref[...] = xh
        rs_ref[...] = rs

    return _tok_call("ffn_down_ln", body, [hb, x1], [(wd, layer), g, b],
                     [_sds((S, D)), _sds((S, D)), _sds((S, 1))], tm=512)


def ple_fwd(x2, p, wpg, wpp, layer, bg, target=None, dep=None):
    S, D = x2.shape
    last = target is not None

    def body(*refs):
        if last:
            x_ref, p_ref, t_ref, wg_ref, wp_ref, b_ref, x3_ref, sg_ref, pp_ref, xb_ref, pb_ref, dy_ref, ls_ref = refs
        else:
            x_ref, p_ref, wg_ref, wp_ref, b_ref, x3_ref, sg_ref, pp_ref, xb_ref, pb_ref = refs
        x = x_ref[...]
        xb = x.astype(BF16)
        pb = p_ref[...].astype(BF16)
        xb_ref[...] = xb
        pb_ref[...] = pb
        sg = _sigmoid(_dot(xb, wg_ref[...]) + b_ref[...])
        pp = _dot(pb, wp_ref[...])
        sg_ref[...] = sg.astype(BF16)
        pp_ref[...] = pp.astype(BF16)
        x3 = x + sg * pp
        x3_ref[...] = x3
        if last:
            err = x3 - t_ref[...]
            dy_ref[...] = err * (1.0 / D)
            _acc(ls_ref, _colsum(err * err))

    outs = [_sds((S, D)), _sds((S, D), BF16), _sds((S, D), BF16), _sds((S, D), BF16), _sds((S, p.shape[1]), BF16)]
    tiled = [x2, p] + ([target] if last else [])
    if last:
        outs.append(_sds((S, D)))
    return _tok_call("ple_fwd", body, tiled, [(wpg, layer), (wpp, layer), bg], outs,
                     [_sds((1, D))] if last else [], dep=dep, tm=512)


def ple_ln_bwd(dx3, sg, pp, x2b, pb, wpg, xh, rs, g, dep=None):
    S, D = dx3.shape

    def body(d_ref, sg_ref, pp_ref, x2b_ref, pb_ref, xh_ref, rs_ref, w_ref, g_ref,
             dr_ref, drb_ref, dbg_ref, dlg_ref, dlb_ref, dwg_ref, dwp_ref, accg_ref, accp_ref):
        fin_g = _sum_steps(accg_ref, dwg_ref)
        fin_p = _sum_steps(accp_ref, dwp_ref)
        d, sg = d_ref[...], sg_ref[...].astype(F32)
        dgp = d * pp_ref[...].astype(F32) * sg * (1.0 - sg)
        dgpb = dgp.astype(BF16)
        accg_ref[...] += _dot_tn(x2b_ref[...], dgpb)
        accp_ref[...] += _dot_tn(pb_ref[...], (d * sg).astype(BF16))
        _acc(dbg_ref, _colsum(dgp))
        dx2 = d + _dot_nt(dgpb, w_ref[...])
        xh = xh_ref[...]
        dr = _ln_bwd(dx2, xh, rs_ref[...], g_ref[...])
        dr_ref[...] = dr
        drb_ref[...] = dr.astype(BF16)
        _acc(dlg_ref, _colsum(dx2 * xh))
        _acc(dlb_ref, _colsum(dx2))
        fin_g()
        fin_p()

    P = pb.shape[1]
    return _tok_call("ple_ln_bwd", body, [dx3, sg, pp, x2b, pb, xh, rs], [wpg, g],
                     [_sds((S, D)), _sds((S, D), BF16)],
                     [_sds((1, D)), _sds((1, D)), _sds((1, D)), _sds((D, D), BF16), _sds((P, D), BF16)],
                     scratch=[pltpu.VMEM((D, D), F32), pltpu.VMEM((P, D), F32)], dep=dep, tm=ACC_TM)


def ffn_bwd(dr_b, x1b, gate, up, hb, wg, wu, wd, TH=512):
    S, D = dr_b.shape
    F = gate.shape[1]

    def body(dr_hbm, x_hbm, gate_ref, up_ref, hb_ref, wg_ref, wu_ref, wd_ref,
             dx_ref, dwg_ref, dwu_ref, dwd_ref, dr_v, x_v, sem, dg_s, du_s):
        @pl.when(pl.program_id(0) == 0)
        def _():
            c1 = pltpu.make_async_copy(dr_hbm, dr_v, sem.at[0])
            c2 = pltpu.make_async_copy(x_hbm, x_v, sem.at[1])
            c1.start()
            c2.start()
            c1.wait()
            c2.wait()
            dx_ref[...] = jnp.zeros_like(dx_ref)

        for ch in range(S // CH):
            rows = slice(ch * CH, (ch + 1) * CH)
            dh = _dot_nt(dr_v[rows, :], wd_ref[...])
            g, u = gate_ref[rows, :].astype(F32), up_ref[rows, :].astype(F32)
            s = _sigmoid(g)
            dgb = (dh * u * s * (1.0 + g * (1.0 - s))).astype(BF16)
            dub = (dh * g * s).astype(BF16)
            dg_s[rows, :] = dgb
            du_s[rows, :] = dub
            dx_ref[rows, :] += _dot_nt(dgb, wg_ref[...]) + _dot_nt(dub, wu_ref[...])
        x = x_v[...]
        dwg_ref[...] = _dot_tn(x, dg_s[...]).astype(BF16)
        dwu_ref[...] = _dot_tn(x, du_s[...]).astype(BF16)
        dwd_ref[...] = _dot_tn(hb_ref[...], dr_v[...]).astype(BF16)

    CH = min(256, S)
    col = lambda rows: pl.BlockSpec((rows, TH), lambda j: (0, j))
    row = pl.BlockSpec((TH, D), lambda j: (j, 0))
    return pl.pallas_call(
        body, name="ffn_bwd", grid=(F // TH,),
        in_specs=[ANY, ANY, col(S), col(S), col(S), col(D), col(D), row],
        out_specs=[pl.BlockSpec((S, D), lambda j: (0, 0)), col(D), col(D), row],
        out_shape=[_sds((S, D)), _sds((D, F), BF16), _sds((D, F), BF16), _sds((F, D), BF16)],
        scratch_shapes=[pltpu.VMEM((S, D), BF16), pltpu.VMEM((S, D), BF16), pltpu.SemaphoreType.DMA((2,)),
                        pltpu.VMEM((S, TH), BF16), pltpu.VMEM((S, TH), BF16)],
        compiler_params=pltpu.CompilerParams(dimension_semantics=("arbitrary",), vmem_limit_bytes=60 * 1024 * 1024),
    )(dr_b, x1b, gate, up, hb, wg, wu, wd)


ACC_TM = 512


def _sum_steps(acc_ref, out_ref):
    @pl.when(pl.program_id(0) == 0)
    def _():
        acc_ref[...] = jnp.zeros_like(acc_ref)

    def finish():
        @pl.when(pl.program_id(0) == pl.num_programs(0) - 1)
        def _():
            out_ref[...] = acc_ref[...].astype(BF16)
    return finish


def mix_bwd(dxp, dr2, xh, rs, l1, l2, g, w, dep=None):
    S, D = dxp.shape
    K1 = l1.shape[1]

    def body(dxp_ref, dr2_ref, xh_ref, rs_ref, l1_ref, l2_ref, g_ref, w_ref,
             dr_ref, dl_ref, dlg_ref, dlb_ref, dw_ref, acc_ref):
        finish = _sum_steps(acc_ref, dw_ref)
        d, xh = ALPHA * dr2_ref[...] + dxp_ref[...], xh_ref[...]
        dr = _ln_bwd(d, xh, rs_ref[...], g_ref[...])
        drb = dr.astype(BF16)
        dr_ref[...] = dr
        dl_ref[...] = _dot_nt(drb, w_ref[...])
        _acc(dlg_ref, _colsum(d * xh))
        _acc(dlb_ref, _colsum(d))
        acc_ref[0:K1, :] += _dot_tn(l1_ref[...], drb)
        acc_ref[K1:, :] += _dot_tn(l2_ref[...], drb)
        finish()

    return _tok_call("mix_bwd", body, [dxp, dr2, xh, rs, l1, l2], [g, w],
                     [_sds((S, D)), _sds((S, D))], [_sds((1, D)), _sds((1, D)), _sds(w.shape, BF16)],
                     scratch=[pltpu.VMEM(w.shape, F32)], dep=dep, tm=ACC_TM)


def dx_in(dr, pieces, w, xb):
    S, D = dr.shape
    offs = [o for _, o in pieces]
    widths = [a.shape[1] for a, _ in pieces]
    npc = len(pieces)

    def body(*refs):
        dr_ref, prefs, xb_ref, w_ref = refs[0], refs[1:1 + npc], refs[1 + npc], refs[2 + npc]
        dx_ref, dw_ref, acc_ref = refs[3 + npc:]
        finish = _sum_steps(acc_ref, dw_ref)
        acc = ALPHA * dr_ref[...]
        xb_t = xb_ref[...]
        for pr, o, n in zip(prefs, offs, widths):
            piece = pr[...]
            acc = acc + _dot_nt(piece, w_ref[:, o:o + n])
            acc_ref[:, o:o + n] += _dot_tn(xb_t, piece)
        dx_ref[...] = acc
        finish()

    return _tok_call("dx_in", body, [dr] + [a for a, _ in pieces] + [xb], [w], [_sds((S, D))],
                     [_sds(w.shape, BF16)], scratch=[pltpu.VMEM(w.shape, F32)], tm=ACC_TM)


def odd_post_bwd(dl, h, xhc, rsc, xhv, rsv, sv, cl_g, cl_b, sl_g, sl_b, sg_w, tm=512, dep=None):
    S = dl.shape[0]
    tm = min(tm, S)

    def body(dl_ref, zc_ref, xhc_ref, rsc_ref, xhv_ref, rsv_ref, sv_ref, clg, clb, slg, slb, w_ref,
             dy_ref, dzc_ref, dclg_ref, dclb_ref, dslg_ref, dslb_ref, dwm_ref, dsb_ref, dvn_ref):
        first = pl.program_id(0) == 0
        last = pl.program_id(0) == pl.num_programs(0) - 1
        dc, dd = dl_ref[:, 0:512], dl_ref[:, 512:1024]
        xhc = xhc_ref[...]
        lnc = xhc * clg[...] + clb[...]
        s = _sigmoid(lnc)
        dlnc = dc * s * (1.0 + lnc * (1.0 - s))
        dy_ref[...] = _ln_bwd(dlnc, xhc, rsc_ref[...], clg[...])
        _acc(dclg_ref, _colsum(dlnc * xhc))
        _acc(dclb_ref, _colsum(dlnc))
        zc = zc_ref[...]
        z = _gelu(zc)
        dsv = dd * z[:, :512]
        dsvb = dsv.astype(BF16)
        xhv = xhv_ref[...]
        vnb = (xhv * slg[...] + slb[...]).astype(BF16)

        @pl.when(first)
        def _():
            dwm_ref[...] = jnp.zeros_like(dwm_ref)
            dsb_ref[...] = jnp.zeros_like(dsb_ref)

        for g in range(4):
            wm = _masked_sg_w(w_ref, g)
            for ch in range(tm // 128):
                rs_, cs = slice(ch * 128, (ch + 1) * 128), slice(g * 128, (g + 1) * 128)
                dwm_ref[g] += _dot_nt(dsvb[rs_, cs], vnb[rs_, cs])
                dvn_ref[rs_, cs] = _dot_tn(wm, dsvb[rs_, cs])
                dsb_ref[g] += dsv[rs_, cs]
        dvn = dvn_ref[...]
        dvv = _ln_bwd(dvn, xhv, rsv_ref[...], slg[...])
        _acc(dslg_ref, _colsum(dvn * xhv))
        _acc(dslb_ref, _colsum(dvn))
        gg = _gelu_grad(zc)
        dzc_ref[:, 0:512] = (dd * sv_ref[...] * gg[:, :512]).astype(BF16)
        dzc_ref[:, 512:1024] = (dvv * gg[:, 512:]).astype(BF16)

        @pl.when(last)
        def _():
            row = lax.broadcasted_iota(jnp.int32, (128, 128), 0)
            col = lax.broadcasted_iota(jnp.int32, (128, 128), 1)
            for g in range(4):
                dwm_ref[g] = jnp.where(row >= col, dwm_ref[g], 0.0)
                dsb_ref[g] = jnp.broadcast_to(jnp.sum(dsb_ref[g], axis=1, keepdims=True), (128, 128))

    return _tok_call(
        "odd_post_bwd", body, [dl, (h, 1024, 1), xhc, rsc, xhv, rsv, sv], [cl_g, cl_b, sl_g, sl_b, sg_w],
        [_sds((S, 512)), _sds((S, 1024), BF16)],
        [_sds((1, 512)), _sds((1, 512)), _sds((1, 512)), _sds((1, 512)), _sds((4, 128, 128)), _sds((4, 128, 128))],
        tm=tm, scratch=[pltpu.VMEM((tm, 512), F32)], dep=dep)


def conv_bwd(dy, hc, h, dw, CH=128):
    S = dy.shape[0]

    def body(dy_ref, hc_ref, a_ref, g_ref, dw_ref, da_ref, dg_ref, ddw_ref, padh_ref, padd_ref, dhc_ref):
        padh_ref[0:32, :] = jnp.zeros((32, 128), F32)
        padh_ref[32:32 + S, :] = hc_ref[...]
        padd_ref[0:S, :] = dy_ref[...]
        padd_ref[S:S + 32, :] = jnp.zeros((32, 128), F32)
        taps = [jnp.zeros((1, 128), F32) for _ in range(CONV_TAPS)]
        for ch in range(S // CH):
            b0 = ch * CH
            dyc = padd_ref[b0:b0 + CH, :]
            acc = dw_ref[0:1, :] * padd_ref[b0 + 30:b0 + 30 + CH, :]
            taps[0] = taps[0] + _colsum(dyc * padh_ref[b0 + 2:b0 + 2 + CH, :])
            for k in range(1, CONV_TAPS):
                acc = acc + dw_ref[k:k + 1, :] * padd_ref[b0 + 30 - k:b0 + 30 - k + CH, :]
                taps[k] = taps[k] + _colsum(dyc * padh_ref[b0 + 2 + k:b0 + 2 + k + CH, :])
            dhc_ref[b0:b0 + CH, :] = acc
        for k in range(CONV_TAPS):
            ddw_ref[k:k + 1, :] = taps[k]
        dhc = dhc_ref[...]
        s = _sigmoid(g_ref[...])
        da_ref[...] = (dhc * s).astype(BF16)
        dg_ref[...] = (dhc * a_ref[...] * s * (1.0 - s)).astype(BF16)

    return pl.pallas_call(
        body, name="conv_bwd", grid=(4,),
        in_specs=[pl.BlockSpec((S, 128), lambda c: (0, c)),
                  pl.BlockSpec((S, 128), lambda c: (0, c)),
                  pl.BlockSpec((S, 128), lambda c: (0, c)),
                  pl.BlockSpec((S, 128), lambda c: (0, 4 + c)),
                  pl.BlockSpec((CONV_TAPS, 128), lambda c: (0, c))],
        out_specs=[pl.BlockSpec((S, 128), lambda c: (0, c)), pl.BlockSpec((S, 128), lambda c: (0, c)),
                   pl.BlockSpec((CONV_TAPS, 128), lambda c: (0, c))],
        out_shape=[_sds((S, 512), BF16), _sds((S, 512), BF16), _sds((CONV_TAPS, 512))],
        scratch_shapes=[pltpu.VMEM((S + 32, 128), F32), pltpu.VMEM((S + 32, 128), F32), pltpu.VMEM((S, 128), F32)],
        compiler_params=_cp(("arbitrary",)),
    )(dy, hc, h, h, dw)


def attn_bwd(qkv, dl, tb, T=256, dep=None):
    S = qkv.shape[0]
    T = min(T, S)
    nq = S // T

    def body(q_ref, k_ref, v_ref, do_ref, t_ref, dq_ref, dk_ref, dv_ref,
             dka_ref, dva_ref, dqa_ref, pc_ref, gc_ref, qh_ref, doh_ref):
        i = pl.program_id(0)
        hm0 = lax.broadcasted_iota(jnp.int32, (1, 128), 1) < 64
        r2 = lax.broadcasted_iota(jnp.int32, (2 * T, T), 0)
        c2 = lax.broadcasted_iota(jnp.int32, (2 * T, T), 1)
        causal = c2 < jnp.where(r2 >= T, r2 - T, r2)
        ur = lax.broadcasted_iota(jnp.int32, (T, T), 0)
        uc = lax.broadcasted_iota(jnp.int32, (T, T), 1)
        u_le = (ur <= uc).astype(BF16)
        u_lt = (ur < uc).astype(BF16)

        @pl.when(i == 0)
        def _():
            dka_ref[...] = jnp.zeros_like(dka_ref)
            dva_ref[...] = jnp.zeros_like(dva_ref)

        dqa_ref[...] = jnp.zeros_like(dqa_ref)
        gc_ref[...] = jnp.zeros_like(gc_ref)
        for pp in range(4):
            cs = slice(pp * 128, (pp + 1) * 128)
            qh_ref[pp] = _stack_heads(q_ref[:, cs] * QK_SCALE, hm0)
            doh_ref[pp] = _stack_heads(do_ref[:, cs], hm0)
            for hd in range(2):
                for half in range(T // 128):
                    pc_ref[pp, hd * T:(hd + 1) * T, half * 128:(half + 1) * 128] = t_ref[2 * pp + hd]

        def block(kb, diag):
            ks = pl.multiple_of(kb * T, T)
            cols = [slice(pp * 128, (pp + 1) * 128) for pp in range(4)]
            zs = [_dot_nt(qh_ref[pp], k_ref[pl.ds(ks, T), cols[pp]]) for pp in range(4)]
            dws = [_dot_nt(doh_ref[pp], v_ref[pl.ds(ks, T), cols[pp]]) for pp in range(4)]
            a_s, pres = [], []
            for pp in range(4):
                sp = _softplus(zs[pp])
                a_s.append(zs[pp] - sp)
                if diag:
                    sp = jnp.where(causal, sp, 0.0)
                pres.append(_cumsum_mm(sp, u_le))
            ws, gmats, gsums = [], [], []
            for pp in range(4):
                rem = pc_ref[pp]
                w = jnp.exp(a_s[pp] - rem + pres[pp])
                if diag:
                    w = jnp.where(causal, w, 0.0)
                gmat = dws[pp] * w
                ws.append(w.astype(BF16))
                gmats.append(gmat)
                gsums.append(_cumsum_mm(gmat, u_lt))
                pc_ref[pp] = rem - jnp.broadcast_to(pres[pp][:, T - 1:T], (2 * T, T))
            for pp in range(4):
                cs = cols[pp]
                sig = jnp.exp(a_s[pp])
                gex = gc_ref[pp] + gsums[pp]
                dz = gmats[pp] - sig * (gmats[pp] + gex)
                if diag:
                    dz = jnp.where(causal, dz, 0.0)
                dzb = dz.astype(BF16)
                dqa_ref[:, cs] += _dot(_unstack_k(dzb, T), _stack_heads(k_ref[pl.ds(ks, T), cs], hm0))
                dka_ref[pl.ds(ks, T), cs] += _dot_tn(dzb, qh_ref[pp])
                dva_ref[pl.ds(ks, T), cs] += _dot_tn(ws[pp], doh_ref[pp])
                gc_ref[pp] = jnp.broadcast_to(gex[:, T - 1:T] + gmats[pp][:, T - 1:T], (2 * T, T))

        def step(kb, carry):
            block(kb, False)
            return carry

        lax.fori_loop(0, i, step, 0)
        block(i, True)
        dq_ref[...] = (dqa_ref[...] * QK_SCALE).astype(BF16)

        @pl.when(i == nq - 1)
        def _():
            dk_ref[...] = dka_ref[...].astype(BF16)
            dv_ref[...] = dva_ref[...].astype(BF16)

    deps = [] if dep is None else [dep]
    call_body = body if dep is None else (lambda *refs: body(*refs[:5], *refs[6:]))
    return pl.pallas_call(
        call_body, name="attn_bwd", grid=(nq,),
        in_specs=[pl.BlockSpec((T, 512), lambda i: (i, 0)),
                  pl.BlockSpec((S, 512), lambda i: (0, 1)),
                  pl.BlockSpec((S, 512), lambda i: (0, 2)),
                  pl.BlockSpec((T, 512), lambda i: (i, 0)),
                  pl.BlockSpec((8, T, 128), lambda i: (0, i, 0))] + [ANY] * len(deps),
        out_specs=[pl.BlockSpec((T, 512), lambda i: (i, 0)),
                   pl.BlockSpec((S, 512), lambda i: (0, 0)),
                   pl.BlockSpec((S, 512), lambda i: (0, 0))],
        out_shape=[_sds((S, 512), BF16), _sds((S, 512), BF16), _sds((S, 512), BF16)],
        scratch_shapes=[pltpu.VMEM((S, 512), F32), pltpu.VMEM((S, 512), F32), pltpu.VMEM((T, 512), F32),
                        pltpu.VMEM((4, 2 * T, T), F32), pltpu.VMEM((4, 2 * T, T), F32),
                        pltpu.VMEM((4, 2 * T, 128), BF16), pltpu.VMEM((4, 2 * T, 128), BF16)],
        compiler_params=_cp(("arbitrary",)),
    )(qkv, qkv, qkv, dl, tb, *deps)


def pool_bwd(dl, pooled_b, pool_w, pool_scale, CH=256):
    S = dl.shape[0]
    CH = min(CH, S)

    def body(db_ref, pooled_ref, w_ref, sc_ref, du_ref, dw_ref, dsc_ref, pad_ref, dp_ref):
        pad_ref[S:S + 16, :] = jnp.zeros((16, 128), F32)
        for g, win in enumerate(POOL_WINDOWS):
            cs = slice(g * 128, (g + 1) * 128)
            wq = w_ref[g].astype(BF16)
            dwg = jnp.zeros((128, 128), F32)
            dsc = jnp.zeros((1, 128), F32)
            for ch in range(S // CH):
                rs_ = slice(ch * CH, (ch + 1) * CH)
                db = db_ref[rs_, cs]
                pb = pooled_ref[rs_, cs]
                dsc = dsc + _colsum(db * _dot(pb, wq))
                dmsb = (db * sc_ref[:, cs]).astype(BF16)
                dwg = dwg + _dot_tn(pb, dmsb)
                dpool = _dot_nt(dmsb, wq)
                t = ch * CH + lax.broadcasted_iota(jnp.int32, (CH, 1), 0)
                cnt = jnp.minimum(t + 1, win).astype(F32)
                dp_ref[rs_, :] = dpool
                pad_ref[rs_, :] = dpool / cnt
            dw_ref[g] = dwg
            dsc_ref[:, cs] = dsc
            for ch in range(S // CH):
                base = ch * CH
                acc = pad_ref[base:base + CH, :]
                for sft in range(1, win):
                    acc = acc + pad_ref[base + sft:base + sft + CH, :]
                du_ref[base:base + CH, cs] = (acc - dp_ref[base:base + CH, :]).astype(BF16)

    return pl.pallas_call(
        body, name="pool_bwd", grid=(1,),
        in_specs=[pl.BlockSpec((S, 512), lambda i: (0, 1)),
                  pl.BlockSpec((S, 512), lambda i: (0, 0)),
                  pl.BlockSpec((4, 128, 128), lambda i: (0, 0, 0)),
                  pl.BlockSpec((1, 512), lambda i: (0, 0))],
        out_specs=[pl.BlockSpec((S, 512), lambda i: (0, 0)),
                   pl.BlockSpec((4, 128, 128), lambda i: (0, 0, 0)),
                   pl.BlockSpec((1, 512), lambda i: (0, 0))],
        out_shape=[_sds((S, 512), BF16), _sds((4, 128, 128)), _sds((1, 512))],
        scratch_shapes=[pltpu.VMEM((S + 16, 128), F32), pltpu.VMEM((S, 128), F32)],
        compiler_params=_cp(("arbitrary",)),
    )(dl, pooled_b, pool_w, pool_scale)


def _row(a, i):
    return a[i:i + 1]


MIXER_NAMES = (("even_w_in", "even_w_out"), ("odd_w_in", "odd_w_out"))


def fwd_layer(i, xin, p_i, target, comm):
    s = {}
    W = comm.weights(("mix", i), xin)
    w_in = W[MIXER_NAMES[i][0]]
    if i == 0:
        s["h"], s["xb"], s["qkv"] = mm_in(xin, w_in, nb16=1536)
        comm.poke(("in", i), s["h"])
        s["l1"], s["tb"] = attn_fwd(s["qkv"])
        s["l2"], s["pooled"] = pool_fwd(s["h"], W["pool_w"], W["pool_scale"])
    else:
        s["h"], s["xb"] = mm_in(xin, w_in)
        comm.poke(("in", i), s["h"])
        s["y"], s["hc"] = conv_fwd(s["h"], W["conv_dw"])
        sgb_bc = jnp.broadcast_to(W["sg_b"][:, :, None], (4, 128, 128))
        (s["l1"], s["l2"], s["xhc"], s["rsc"], s["xhv"], s["rsv"], s["sv"]) = odd_post(
            s["y"], s["h"], W["conv_ln_g"], W["conv_ln_b"], W["sg_ln_g"], W["sg_ln_b"], W["sg_w"], sgb_bc)
    tok = comm.poke(("mixed", i), s["l1"])
    W = comm.weights(("out", i), s["l1"])
    x1, s["xh1"], s["rs1"] = mm_out_ln(s["l1"], s["l2"], xin, W[MIXER_NAMES[i][1]], _row(W["ln_mix_g"], i),
                                       _row(W["ln_mix_b"], i), dep=tok)
    W = comm.weights(("ffn", i), x1)
    tok = comm.poke(("up", i), x1)
    s["gate"], s["up"], s["hb"], s["x1b"] = ffn_up(x1, W["ffn_w_gate%d" % i], W["ffn_w_up%d" % i], None, dep=tok)
    W = comm.weights(("down", i), s["hb"])
    x2, s["xh2"], s["rs2"] = ffn_down_ln(s["hb"], x1, W["ffn_w_down%d" % i], None,
                                         _row(W["ln_ffn_g"], i), _row(W["ln_ffn_b"], i))
    tok = comm.poke(("ffn", i), x2)
    outs = ple_fwd(x2, p_i, W["ple_w_gate%d" % i], W["ple_w_proj%d" % i], None, _row(W["ple_b_gate"], i), target,
                   dep=tok)
    s["sg"], s["pp"], s["x2b"], s["pb"] = outs[1:5]
    return outs[0], s, outs[5:]


def bwd_layer(i, dx, s, W, comm, tok=None):
    small = {}
    dr2, dr2_b, small["ple_b_gate"], small["ln_ffn_g"], small["ln_ffn_b"], dwpg, dwpp = ple_ln_bwd(
        dx, s["sg"], s["pp"], s["x2b"], s["pb"], W["ple_w_gate%d" % i], s["xh2"], s["rs2"],
        _row(W["ln_ffn_g"], i), dep=tok)
    dxp, dwg, dwu, dwd = ffn_bwd(dr2_b, s["x1b"], s["gate"], s["up"], s["hb"], W["ffn_w_gate%d" % i],
                                 W["ffn_w_up%d" % i], W["ffn_w_down%d" % i])
    tok = comm.grads({"ple_w_gate%d" % i: dwpg, "ple_w_proj%d" % i: dwpp, "ffn_w_down%d" % i: dwd,
                      "ffn_w_gate%d" % i: dwg, "ffn_w_up%d" % i: dwu})
    iname, oname = MIXER_NAMES[i]
    dr1, dl, small["ln_mix_g"], small["ln_mix_b"], dwout = mix_bwd(
        dxp, dr2, s["xh1"], s["rs1"], s["l1"], s["l2"], _row(W["ln_mix_g"], i), W[oname], dep=tok)
    tok = comm.poke(("bwd", i), dl)
    if i == 1:
        (dy, dzc_b, small["conv_ln_g"], small["conv_ln_b"], small["sg_ln_g"], small["sg_ln_b"],
         small["sg_w"], dsb) = odd_post_bwd(dl, s["h"], s["xhc"], s["rsc"], s["xhv"], s["rsv"], s["sv"],
                                            W["conv_ln_g"], W["conv_ln_b"], W["sg_ln_g"], W["sg_ln_b"], W["sg_w"],
                                            dep=tok)
        small["sg_b"] = dsb[:, :, 0]
        da_b, dg_b, small["conv_dw"] = conv_bwd(dy, s["hc"], s["h"], W["conv_dw"])
        pieces = [(da_b, 0), (dg_b, 512), (dzc_b, 1024)]
    else:
        dq_b, dk_b, dv_b = attn_bwd(s["qkv"], dl, s["tb"], dep=tok)
        du_b, small["pool_w"], small["pool_scale"] = pool_bwd(dl, s["pooled"], W["pool_w"], W["pool_scale"])
        pieces = [(dq_b, 0), (dk_b, 512), (dv_b, 1024), (du_b, 1536)]
    dxin, dwin = dx_in(dr1, pieces, W[iname], s["xb"])
    tok = comm.grads({oname: dwout, iname: dwin})
    return dxin, small, tok


def run_layers(x, p, target, comm):
    saved, xin = [], x
    for i in range(2):
        xin, s, extra = fwd_layer(i, xin, p[i], target if i == 1 else None, comm)
        saved.append(s)
    dx, sq = extra
    W = comm.all_weights()
    per_layer = [None, None]
    tok = None
    for i in (1, 0):
        dx, per_layer[i], tok = bwd_layer(i, dx, saved[i], W, comm, tok)
    small = {}
    for k in ("ln_mix_g", "ln_mix_b", "ln_ffn_g", "ln_ffn_b", "ple_b_gate"):
        small[k] = jnp.concatenate([per_layer[0][k], per_layer[1][k]], axis=0)
    for i in range(2):
        small.update({k: v for k, v in per_layer[i].items() if k not in small})
    return sq, dx, small


def _big_table():
    t = {}
    for nm in ("even", "odd"):
        t[nm + "_w_in"] = ((1024, 2048), 1, 256, 256, nm + "_w_in", 0)
        t[nm + "_w_out"] = ((1024, 1024), 0, 128, 128, nm + "_w_out", 0)
    for l in range(2):
        t["ffn_w_gate%d" % l] = ((1024, 8 * FF_PAD), 1, FF_PAD, FF_SHARD, "ffn_w_gate", l)
        t["ffn_w_up%d" % l] = ((1024, 8 * FF_PAD), 1, FF_PAD, FF_SHARD, "ffn_w_up", l)
        t["ffn_w_down%d" % l] = ((8 * FF_PAD, 1024), 0, FF_PAD, FF_SHARD, "ffn_w_down", l)
        t["ple_w_gate%d" % l] = ((1024, 1024), 0, 128, 128, "ple_w_gate", l)
        t["ple_w_proj%d" % l] = ((256, 1024), 1, 128, 128, "ple_w_proj", l)
    return t


BIG = _big_table()
TRANSPOSED_ARGS = ("ffn_w_gate", "ffn_w_up")
SMALL_SPEC = ((N_DEV, 40, 64), 0, 1, 1)
_UP_GROUP = lambda l: ["ffn_w_gate%d" % l, "ffn_w_up%d" % l]
_DOWN_GROUP = lambda l: ["ffn_w_down%d" % l, "ple_w_gate%d" % l, "ple_w_proj%d" % l]
AG_GROUPS = (["even_w_in"], ["even_w_out"], _UP_GROUP(0), _DOWN_GROUP(0), ["odd_w_in", "odd_w_out", "small"],
             _UP_GROUP(1), _DOWN_GROUP(1))
AG_NEED = {("mix", 0): 0, ("out", 0): 1, ("ffn", 0): 2, ("down", 0): 3, ("mix", 1): 4, ("ffn", 1): 5, ("down", 1): 6}
AG_PASS = {("in", 0): 1, ("mixed", 0): 2, ("up", 0): 3, ("ffn", 0): 4, ("mixed", 1): 5, ("up", 1): 6}
ANY = pl.BlockSpec(memory_space=pl.ANY)
SEM = pl.BlockSpec(memory_space=pltpu.SEMAPHORE)


def _spec(name):
    return SMALL_SPEC if name == "small" else BIG[name]


def _win_shape(spec):
    full, axis, w = spec[:3]
    return tuple(w if d == axis else n for d, n in enumerate(full))


def _window(ref, axis, w, j):
    idx = [slice(None)] * len(ref.shape)
    idx[axis] = pl.ds(j, 1) if w == 1 else pl.ds(pl.multiple_of(j * w, w), w)
    return ref.at[tuple(idx)]


def _mesh_pos():
    return lax.axis_index("x"), lax.axis_index("y"), lax.axis_index("c")


def split_call(name, arrays, starts=(), waits=(), sems_in=(), new=(), after=None):
    n, nn, ns = len(arrays), len(new), len(starts)
    flat_sems = [s for pair in sems_in for s in pair]

    def body(*refs):
        arr = list(refs[:n])
        sin = refs[n:n + len(flat_sems)]
        outs = refs[n + len(flat_sems) + (after is not None):]
        data = arr + list(outs[n:n + nn])
        for p, k, kind, mk in waits:
            d = mk(data, sin[2 * p].at[k], sin[2 * p + 1].at[k])
            d.wait_send() if kind == "send" else d.wait_recv()
        if ns:
            send, recv = outs[n + nn], outs[n + nn + 1]
            for k, mk in enumerate(starts):
                mk(data, send.at[k], recv.at[k]).start()
        outs[-1][...] = jnp.zeros((8, 128), F32)

    sem_out = [pltpu.SemaphoreType.DMA((ns,)), pltpu.SemaphoreType.DMA((ns,))] if ns else []
    res = pl.pallas_call(
        body, name=name,
        in_specs=[ANY] * n + [SEM] * len(flat_sems) + ([ANY] if after is not None else []),
        out_specs=[ANY] * (n + nn) + [SEM] * len(sem_out) + [pl.BlockSpec(memory_space=pltpu.VMEM)],
        out_shape=[_sds(a.shape, a.dtype) for a in arrays] + list(new) + sem_out + [_sds((8, 128), F32)],
        input_output_aliases={a: a for a in range(n)},
        compiler_params=pltpu.CompilerParams(has_side_effects=pltpu.SideEffectType.DATAFLOW_SIDE_EFFECTING),
    )(*arrays, *flat_sems, *([after] if after is not None else []))
    return list(res[:n + nn]), (tuple(res[n + nn:n + nn + 2]) if ns else None), res[-1]


def _remote(src, dst, send_sem, recv_sem, dev):
    return pltpu.make_async_remote_copy(src_ref=src, dst_ref=dst, send_sem=send_sem, recv_sem=recv_sem,
                                        device_id=dev, device_id_type=MESH_T)


class Gatherer:
    def __init__(self, groups, arrays, specs, prefix):
        self.groups, self.specs, self.prefix = groups, specs, prefix
        self.names = [nm for g in groups for nm in g]
        self.arr = dict(zip(self.names, arrays))
        self.fwd_sems = {}
        self.forwarded = set()

    @staticmethod
    def _mk_first(ai, spec, k):
        def mk(refs, ss, rs):
            x, y, c = _mesh_pos()
            dev = [(x, y, 1 - c), (1 - x, y, c), (x, 1 - y, c), (1 - x, 1 - y, c)][k]
            win = _window(refs[ai], spec[1], spec[2], 4 * x + 2 * y + c)
            return _remote(win, win, ss, rs, dev)
        return mk

    @staticmethod
    def _mk_fwd(ai, spec, j):
        def mk(refs, ss, rs):
            x, y, c = _mesh_pos()
            px, py = [(1 - x, y), (x, 1 - y), (1 - x, 1 - y)][j]
            win = _window(refs[ai], spec[1], spec[2], 4 * px + 2 * py + c)
            return _remote(win, win, ss, rs, (x, y, 1 - c))
        return mk

    def start(self, after=None):
        starts = [self._mk_first(ai, self.specs[nm], k) for ai, nm in enumerate(self.names) for k in range(4)]
        arrs, self.first_sems, tok = split_call(self.prefix + "_start", [self.arr[nm] for nm in self.names],
                                                starts=starts, after=after)
        self.arr = dict(zip(self.names, arrs))
        return tok

    def forward(self, g, after=None):
        if g in self.forwarded:
            return None
        self.forwarded.add(g)
        names = self.groups[g]
        waits = [(0, 4 * self.names.index(nm) + 1 + j, "recv", self._mk_fwd(ai, self.specs[nm], j))
                 for ai, nm in enumerate(names) for j in range(3)]
        starts = [self._mk_fwd(ai, self.specs[nm], j) for ai, nm in enumerate(names) for j in range(3)]
        arrs, self.fwd_sems[g], tok = split_call(
            "%s_forward%d" % (self.prefix, g), [self.arr[nm] for nm in names], starts=starts, waits=waits,
            sems_in=[self.first_sems], after=after)
        self.arr.update(zip(names, arrs))
        return tok

    def finish(self, g, after=None):
        self.forward(g, after)
        names = self.groups[g]
        waits = []
        for ai, nm in enumerate(names):
            base = 4 * self.names.index(nm)
            waits.append((0, base, "recv", self._mk_first(ai, self.specs[nm], 0)))
            waits += [(1, 3 * ai + j, "recv", self._mk_fwd(ai, self.specs[nm], j)) for j in range(3)]
            waits += [(0, base + k, "send", self._mk_first(ai, self.specs[nm], k)) for k in range(4)]
            waits += [(1, 3 * ai + j, "send", self._mk_fwd(ai, self.specs[nm], j)) for j in range(3)]
        arrs, _, _ = split_call(
            "%s_finish%d" % (self.prefix, g), [self.arr[nm] for nm in names], waits=waits,
            sems_in=[self.first_sems, self.fwd_sems[g]], after=after)
        self.arr.update(zip(names, arrs))
        return {nm: self.arr[nm] for nm in names}


class Reducer:
    def __init__(self, cq_arr, adam):
        self.cq_arr, self.adam = cq_arr, adam
        self.groups = []
        self.n = 0
        self.last = None

    @staticmethod
    def _mk1(gi, li, spec, q):
        def mk(refs, ss, rs):
            x, y, c = _mesh_pos()
            return _remote(_window(refs[gi], spec[1], spec[2], 2 * q + (1 - c)), refs[li].at[q], ss, rs, (x, y, 1 - c))
        return mk

    @staticmethod
    def _mk2(si, li, d):
        def mk(refs, ss, rs):
            x, y, c = _mesh_pos()
            qd = lax.rem(2 * x + y + d, 4)
            return _remote(refs[si].at[d - 1], refs[li].at[3 - d], ss, rs, (lax.div(qd, 2), lax.rem(qd, 2), c))
        return mk

    def add(self, grads, after=None):
        names = list(grads)
        m = len(names)
        starts = [self._mk1(ai, m + ai, BIG[nm], q) for ai, nm in enumerate(names) for q in range(4)]
        new = [_sds((4,) + _win_shape(BIG[nm]), BF16) for nm in names]
        res, sems, tok = split_call("rs1_start%d" % self.n, [grads[nm] for nm in names], starts=starts, new=new,
                                    after=after)
        self.groups.append(dict(names=names, starts=starts, buf=res, sems=sems, stage=1, idx=self.n))
        self.n += 1
        return tok

    def step(self, after):
        tok = None
        for grp in self.groups:
            names, m = grp["names"], len(grp["names"])
            if grp["stage"] == 1:
                waits = [(0, k, kind, mk) for k, mk in enumerate(grp["starts"]) for kind in ("send", "recv")]
                res, _, _ = split_call("rs1_wait%d" % grp["idx"], grp["buf"], waits=waits, sems_in=[grp["sems"]], after=after)
                full, land1 = res[:m], res[m:]
                s1b = []
                for lo in range(0, m, 4):
                    s1b += list(add_pairs(full[lo:lo + 4], land1[lo:lo + 4], [BIG[nm] for nm in names[lo:lo + 4]],
                                          self.cq_arr))
                starts = [self._mk2(ai, m + ai, d) for ai in range(m) for d in (1, 2, 3)]
                new = [_sds(a.shape, BF16) for a in s1b]
                res, sems, tok = split_call("rs2_start%d" % grp["idx"], s1b, starts=starts, new=new, after=tok)
                grp.update(stage=2, g=full, land1=land1, starts=starts, buf=res, sems=sems)
        return tok

    def finish_oldest(self):
        for grp in self.groups:
            if grp["stage"] == 2:
                names, m = grp["names"], len(grp["names"])
                waits = [(0, k, kind, mk) for k, mk in enumerate(grp["starts"]) for kind in ("send", "recv")]
                res, _, _ = split_call("rs2_wait%d" % grp["idx"], grp["buf"], waits=waits, sems_in=[grp["sems"]],
                                       after=self.last)
                for nm, g, l1, l2 in zip(names, grp["g"], grp["land1"], res[m:]):
                    self.last = self.adam(nm, g, l1, l2, self.last)
                grp["stage"] = 3
                return True
        return False


def pack_weights(args, arg_names, small_blk, names, j_arr, dep=None):
    n_in = len(args)
    deps = [] if dep is None else [dep]

    def body(j_ref, *refs):
        for o, nm in enumerate(names):
            dst = refs[n_in + 1 + len(deps) + o]
            if nm == "small":
                dst[...] = refs[n_in][...]
                continue
            _, axis, w, valid, arg, layer = BIG[nm]
            if arg in TRANSPOSED_ARGS:
                s = refs[arg_names.index(arg)][layer]
                s = jnp.concatenate([s, jnp.zeros((w - valid, s.shape[1]), F32)], axis=0)
                dst[...] = s.T.astype(BF16)
                continue
            src = refs[arg_names.index(arg)][layer].astype(BF16)
            if valid == w:
                dst[...] = src
            else:
                dst[...] = jnp.zeros(dst.shape, BF16)
                if axis == 1:
                    dst[:, 0:valid] = src
                else:
                    dst[0:valid, :] = src

    def ispec(a):
        return pl.BlockSpec(a.shape, lambda i, j_ref: (0, 0, 0))

    def ospec(spec):
        axis, nd = spec[1], len(spec[0])
        return pl.BlockSpec(_win_shape(spec),
                            lambda i, j_ref, axis=axis, nd=nd: tuple(j_ref[0] if d == axis else 0 for d in range(nd)))

    specs = [_spec(nm) for nm in names]
    return pl.pallas_call(
        body, name="pack_weights",
        grid_spec=pltpu.PrefetchScalarGridSpec(
            num_scalar_prefetch=1, grid=(1,),
            in_specs=[ispec(a) for a in list(args) + [small_blk]] + [ANY] * len(deps),
            out_specs=[ospec(s) for s in specs]),
        out_shape=[_sds(s[0], F32 if nm == "small" else BF16) for nm, s in zip(names, specs)],
        compiler_params=_cp(("arbitrary",)),
    )(j_arr, *args, small_blk, *deps)


def add_pairs(fulls, lands, specs, cq_arr):
    def chip(d, cq):
        return lax.rem(cq[1] + d + 1, 4)

    in_specs, args = [], []
    for full, land, spec in zip(fulls, lands, specs):
        axis, w = spec[1], spec[2]
        R, C = full.shape
        for d in range(3):
            if axis == 1:
                in_specs.append(pl.BlockSpec((R, w), lambda i, cq, d=d: (0, 2 * chip(d, cq) + cq[0])))
                in_specs.append(pl.BlockSpec((None, R, w), lambda i, cq, d=d: (chip(d, cq), 0, 0)))
            else:
                in_specs.append(pl.BlockSpec((w, C), lambda i, cq, d=d: (2 * chip(d, cq) + cq[0], 0)))
                in_specs.append(pl.BlockSpec((None, w, C), lambda i, cq, d=d: (chip(d, cq), 0, 0)))
            args += [full, land]
    out_shape = [_sds((3,) + land.shape[1:], BF16) for land in lands]
    n = len(fulls)

    def body(cq_ref, *refs):
        for a in range(n):
            for d in range(3):
                own, got = refs[6 * a + 2 * d], refs[6 * a + 2 * d + 1]
                refs[6 * n + a][d] = (own[...].astype(F32) + got[...].astype(F32)).astype(BF16)

    return pl.pallas_call(
        body, name="add_pairs",
        grid_spec=pltpu.PrefetchScalarGridSpec(
            num_scalar_prefetch=1, grid=(1,), in_specs=in_specs,
            out_specs=[pl.BlockSpec(o.shape, lambda i, cq: (0, 0, 0)) for o in out_shape]),
        out_shape=out_shape,
        compiler_params=_cp(("arbitrary",)),
    )(cq_arr, *args)


def _adamw(w, g, m, v):
    m = ADAM_B1 * m + (1.0 - ADAM_B1) * g
    v = ADAM_B2 * v + (1.0 - ADAM_B2) * (g * g)
    m_hat = m / (1.0 - ADAM_B1 ** ADAM_STEP)
    v_hat = v / (1.0 - ADAM_B2 ** ADAM_STEP)
    delta = -ADAM_LR * (m_hat / (jnp.sqrt(v_hat) + ADAM_EPS) + ADAM_WD * w)
    return delta, m, v


def reduce_adamw(full, land1, land, w, m, v, spec, cq_arr, prev=None, dep=None):
    axis, win, valid, layer = spec[1], spec[2], spec[3], spec[5]
    L, R, C = w.shape
    transposed = spec[4] in TRANSPOSED_ARGS
    TL = 256
    if transposed:
        grid = (C // TL,)
        fspec = pl.BlockSpec((TL, win), lambda i, cq: (i, 2 * cq[1] + cq[0]))
        wspec = pl.BlockSpec((None, TL, win), lambda i, cq: (cq[1], i, 0))
        lspec = pl.BlockSpec((3, TL, win), lambda i, cq: (0, i, 0))
        sspec = pl.BlockSpec((None, R, TL), lambda i, cq: (layer, 0, i))
    elif axis == 1:
        tr = min(TL, R)
        grid = (R // tr,)
        fspec = pl.BlockSpec((tr, win), lambda i, cq: (i, 2 * cq[1] + cq[0]))
        wspec = pl.BlockSpec((None, tr, win), lambda i, cq: (cq[1], i, 0))
        lspec = pl.BlockSpec((3, tr, win), lambda i, cq: (0, i, 0))
        sspec = pl.BlockSpec((None, tr, C), lambda i, cq: (layer, i, 0))
    else:
        grid = (C // TL,)
        fspec = pl.BlockSpec((win, TL), lambda i, cq: (2 * cq[1] + cq[0], i))
        wspec = pl.BlockSpec((None, win, TL), lambda i, cq: (cq[1], 0, i))
        lspec = pl.BlockSpec((3, win, TL), lambda i, cq: (0, 0, i))
        sspec = pl.BlockSpec((None, R, TL), lambda i, cq: (layer, 0, i))

    def body(cq_ref, full_ref, own_ref, land_ref, w_ref, m_ref, v_ref, *rest):
        g_ref, d_ref, nm_ref, nv_ref = rest[-4:]
        if transposed:
            rd = lambda r, *lead: r[lead] if lead else r[...]
        elif axis == 1:
            rd = lambda r, *lead: r[(*lead, slice(None), slice(0, valid))]
        else:
            rd = lambda r, *lead: r[(*lead, slice(0, valid), slice(None))]
        g = rd(full_ref).astype(F32) + rd(own_ref).astype(F32)
        for k in range(3):
            g = g + rd(land_ref, k).astype(F32)
        if transposed:
            g = g.T[0:valid, :]
        g_ref[...] = g
        d, nm, nv = _adamw(w_ref[...], g, m_ref[...], v_ref[...])
        d_ref[...] = d
        nm_ref[...] = nm
        nv_ref[...] = nv

    extra = (list(prev) if prev is not None else []) + ([dep] if dep is not None else [])
    return pl.pallas_call(
        body, name="reduce_adamw",
        grid_spec=pltpu.PrefetchScalarGridSpec(
            num_scalar_prefetch=1, grid=grid,
            in_specs=[fspec, wspec, lspec, sspec, sspec, sspec] + [ANY] * len(extra), out_specs=[sspec] * 4),
        out_shape=[_sds(w.shape)] * 4,
        input_output_aliases={7 + k: k for k in range(4 if prev is not None else 0)},
        compiler_params=_cp(("arbitrary",)),
    )(cq_arr, full, land1, land, w, m, v, *extra)


def place_slot(packed, j_arr):
    R = packed.shape[0]

    def body(j_ref, src, dst):
        dst[...] = src[...]

    return pl.pallas_call(
        body, name="place_slot",
        grid_spec=pltpu.PrefetchScalarGridSpec(
            num_scalar_prefetch=1, grid=(1,),
            in_specs=[pl.BlockSpec((R, 128), lambda i, j: (0, 0))],
            out_specs=[pl.BlockSpec((None, R, 128), lambda i, j: (j[0], 0, 0))]),
        out_shape=[_sds((N_DEV, R, 128))], compiler_params=_cp(("arbitrary",)),
    )(j_arr, packed)[0]


def sum_slots(gathered):
    def body(g_ref, o_ref):
        g = g_ref[0]
        for dev in range(1, N_DEV):
            g = g + g_ref[dev]
        o_ref[...] = g

    return pl.pallas_call(body, name="sum_slots", out_shape=_sds(gathered.shape[1:]), compiler_params=_cp())(gathered)


def small_adamw(gs, wmv):
    k = len(gs)

    def body(*refs):
        for a in range(k):
            g, w, m, v = refs[4 * a:4 * a + 4]
            d, nm, nv = _adamw(w[...], g[...], m[...], v[...])
            refs[4 * k + 3 * a][...] = d
            refs[4 * k + 3 * a + 1][...] = nm
            refs[4 * k + 3 * a + 2][...] = nv

    args = [t for g, tup in zip(gs, wmv) for t in (g,) + tuple(tup)]
    out_shape = [_sds(g.shape) for g in gs for _ in range(3)]
    return pl.pallas_call(body, name="small_adamw", out_shape=out_shape, compiler_params=_cp())(*args)


WEIGHT_NAMES = ("even_w_in", "even_w_out", "pool_w", "pool_scale", "odd_w_in", "odd_w_out", "conv_dw", "conv_ln_g",
                "conv_ln_b", "sg_ln_g", "sg_ln_b", "sg_w", "sg_b", "ln_mix_g", "ln_mix_b", "ffn_w_gate", "ffn_w_up",
                "ffn_w_down", "ln_ffn_g", "ln_ffn_b", "ple_w_proj", "ple_w_gate", "ple_b_gate")
PACK_ARGS = ("even_w_in", "even_w_out", "odd_w_in", "odd_w_out", "ffn_w_gate", "ffn_w_up", "ffn_w_down",
             "ple_w_gate", "ple_w_proj")
REPLICATED = ("pool_w", "pool_scale", "sg_w", "sg_b", "ln_mix_g", "ln_mix_b", "ln_ffn_g", "ln_ffn_b", "ple_b_gate")
SHARDED_SMALL = ("conv_dw", "conv_ln_g", "conv_ln_b", "sg_ln_g", "sg_ln_b")
NATURAL = {"pool_w": (4, 128, 128), "pool_scale": (1, 512), "sg_w": (4, 128, 128), "sg_b": (4, 128),
           "ln_mix_g": (2, 1024), "ln_mix_b": (2, 1024), "ln_ffn_g": (2, 1024), "ln_ffn_b": (2, 1024),
           "ple_b_gate": (2, 1024)}


def kernel(x, p, even_w_in, even_w_out, pool_w, pool_scale, odd_w_in, odd_w_out, conv_dw, conv_ln_g, conv_ln_b, sg_ln_g, sg_ln_b, sg_w, sg_b, ln_mix_g, ln_mix_b, ffn_w_gate, ffn_w_up, ffn_w_down, ln_ffn_g, ln_ffn_b, ple_w_proj, ple_w_gate, ple_b_gate, loss_target, m_even_w_in, m_even_w_out, m_pool_w, m_pool_scale, m_odd_w_in, m_odd_w_out, m_conv_dw, m_conv_ln_g, m_conv_ln_b, m_sg_ln_g, m_sg_ln_b, m_sg_w, m_sg_b, m_ln_mix_g, m_ln_mix_b, m_ffn_w_gate, m_ffn_w_up, m_ffn_w_down, m_ln_ffn_g, m_ln_ffn_b, m_ple_w_proj, m_ple_w_gate, m_ple_b_gate, v_even_w_in, v_even_w_out, v_pool_w, v_pool_scale, v_odd_w_in, v_odd_w_out, v_conv_dw, v_conv_ln_g, v_conv_ln_b, v_sg_ln_g, v_sg_ln_b, v_sg_w, v_sg_b, v_ln_mix_g, v_ln_mix_b, v_ffn_w_gate, v_ffn_w_up, v_ffn_w_down, v_ln_ffn_g, v_ln_ffn_b, v_ple_w_proj, v_ple_w_gate, v_ple_b_gate):
    A = dict(locals())
    for arg in TRANSPOSED_ARGS:
        for pre in ("", "m_", "v_"):
            A[pre + arg] = jnp.swapaxes(A[pre + arg], 1, 2)
    mx, my, mc = _mesh_pos()
    j = 4 * mx + 2 * my + mc
    j_arr = j.astype(jnp.int32).reshape(1)
    cq_arr = jnp.stack([mc, 2 * mx + my]).astype(jnp.int32)
    res = {}

    def adam(nm, full, land1, land2, dep):
        arg = BIG[nm][4]
        res[arg] = reduce_adamw(full, land1, land2, A[arg], A["m_" + arg], A["v_" + arg], BIG[nm], cq_arr,
                                res.get(arg), dep)
        return res[arg][0]

    class Comm:
        def __init__(self):
            names = [nm for g in AG_GROUPS for nm in g]
            specs = {nm: _spec(nm) for nm in names}
            small_blk = jnp.concatenate([conv_dw[0], conv_ln_g, conv_ln_b, sg_ln_g, sg_ln_b, jnp.zeros((5, 64), F32)], axis=0)
            first = pack_weights([A[AG_GROUPS[0][0]]], AG_GROUPS[0], small_blk[None], AG_GROUPS[0], j_arr)
            self.gat0 = Gatherer(AG_GROUPS[:1], first, specs, "ag0")
            first_started = self.gat0.start()
            rest = names[len(AG_GROUPS[0]):]
            mine = pack_weights([A[k] for k in PACK_ARGS], PACK_ARGS, small_blk[None], rest, j_arr, dep=first_started)
            self.gat = Gatherer(AG_GROUPS[1:], mine, specs, "ag")
            self.rest_started = self.gat.start()
            self.red = Reducer(cq_arr, adam)
            self.W = {k: A[k].reshape(NATURAL[k]) for k in REPLICATED}

        def weights(self, stage, after):
            if stage in AG_NEED:
                g = AG_NEED[stage]
                got = self.gat0.finish(0, self.rest_started) if g == 0 else self.gat.finish(g - 1, after)
                if "small" in got:
                    sm = got.pop("small").transpose(1, 0, 2).reshape(40, 512)
                    got.update(conv_dw=sm[0:31], conv_ln_g=sm[31:32], conv_ln_b=sm[32:33], sg_ln_g=sm[33:34],
                               sg_ln_b=sm[34:35])
                self.W.update(got)
            return self.W

        def all_weights(self):
            return self.W

        def poke(self, tag, after):
            if tag in AG_PASS:
                return self.gat.forward(AG_PASS[tag] - 1, after)
            if tag[0] == "bwd":
                return self.red.step(after)
            return None

        def grads(self, grads):
            self.n_grads = getattr(self, "n_grads", 0) + 1
            if self.n_grads == 2:
                self.held = grads
                return None
            if self.n_grads == 3:
                grads = {**self.held, **grads}
            tok = self.red.step(next(iter(grads.values())))
            return self.red.add(grads, after=tok)

    comm = Comm()
    sq, dx, small = run_layers(x[0], p[:, 0], loss_target[0], comm)
    red = comm.red
    tok = red.step(dx)

    names = REPLICATED + SHARDED_SMALL
    flat = jnp.concatenate([small[k].reshape(-1) for k in names] + [jnp.sum(sq).reshape(1)])
    rows = -(-flat.shape[0] // 1024) * 8
    packed = jnp.pad(flat, (0, rows * 128 - flat.shape[0])).reshape(rows, 128)
    sg = Gatherer((["g"],), [place_slot(packed, j_arr)], {"g": ((N_DEV, rows, 128), 0, 1, 1)}, "sg")
    red.last = sg.start(after=tok)
    older = sum(grp["stage"] == 2 for grp in red.groups) - 1
    for k in range(older):
        red.finish_oldest()
        if k == 0:
            sg.forward(0, after=red.last)
    gsum_flat = sum_slots(sg.finish(0, after=red.last)["g"]).reshape(-1)
    loss = 0.5 * gsum_flat[flat.shape[0] - 1] / x.shape[-1]
    gs, off = [], 0
    for k in names:
        n = math.prod(small[k].shape)
        g = gsum_flat[off:off + n].reshape(small[k].shape)
        off += n
        if k in SHARDED_SMALL:
            g = lax.dynamic_slice_in_dim(g, j * 64, 64, axis=1)
        gs.append(g.reshape(A[k].shape))
    outs = small_adamw(gs, [(A[k], A["m_" + k], A["v_" + k]) for k in names])
    for a, k in enumerate(names):
        res[k] = (gs[a],) + tuple(outs[3 * a:3 * a + 3])
    red.last = outs[0]
    while red.finish_oldest():
        pass

    for arg in TRANSPOSED_ARGS:
        res[arg] = [jnp.swapaxes(t, 1, 2) for t in res[arg]]
    out = [loss, dx[None]]
    for part in range(4):
        out += [res[k][part] for k in WEIGHT_NAMES]
    return tuple(out)
```

```python
import math

import jax
import jax.numpy as jnp
from jax import lax
from jax.experimental import pallas as pl
from jax.experimental.pallas import tpu as pltpu

F32, BF16 = jnp.float32, jnp.bfloat16
ALPHA = 4.0 ** 0.25
LN_EPS = 1e-5
QK_SCALE = 0.125
POOL_WINDOWS = (2, 4, 8, 16)
CONV_TAPS = 31
N_DEV = 8
FF_SHARD, FF_PAD = 352, 384
ADAM_LR, ADAM_B1, ADAM_B2, ADAM_EPS, ADAM_WD, ADAM_STEP = 0.001, 0.9, 0.999, 1e-08, 0.01, 10
VMEM_LIMIT = 56 * 1024 * 1024
MESH_T = pl.DeviceIdType.MESH


def _cp(sem=None):
    return pltpu.CompilerParams(dimension_semantics=sem, vmem_limit_bytes=VMEM_LIMIT)


def _dot(a, b):
    return jnp.dot(a, b, preferred_element_type=F32)


def _dot_nt(a, b):
    return lax.dot_general(a, b, (((1,), (1,)), ((), ())), preferred_element_type=F32)


def _dot_tn(a, b):
    return lax.dot_general(a, b, (((0,), (0,)), ((), ())), preferred_element_type=F32)


def _sigmoid(x):
    return 1.0 / (1.0 + jnp.exp(-x))


def _softplus(z):
    return jnp.maximum(z, 0.0) + jnp.log(1.0 + jnp.exp(-jnp.abs(z)))


_GELU_C = math.sqrt(2.0 / math.pi)


def _gelu(x):
    return 0.5 * x * (1.0 + jnp.tanh(_GELU_C * (x + 0.044715 * x * x * x)))


def _gelu_grad(x):
    t = jnp.tanh(_GELU_C * (x + 0.044715 * x * x * x))
    return 0.5 * (1.0 + t) + 0.5 * x * (1.0 - t * t) * _GELU_C * (1.0 + 3.0 * 0.044715 * x * x)


def _ln_fwd(r, g, b):
    mu = jnp.mean(r, axis=-1, keepdims=True)
    xc = r - mu
    var = jnp.mean(xc * xc, axis=-1, keepdims=True)
    rstd = lax.rsqrt(var + LN_EPS)
    xh = xc * rstd
    return xh * g + b, xh, rstd


def _ln_bwd(dy, xh, rstd, g):
    dxh = dy * g
    m1 = jnp.mean(dxh, axis=-1, keepdims=True)
    m2 = jnp.mean(dxh * xh, axis=-1, keepdims=True)
    return rstd * (dxh - m1 - xh * m2)


def _split2(x):
    hi = x.astype(BF16)
    lo = (x - hi.astype(F32)).astype(BF16)
    return hi, lo


def _colsum(x):
    return jnp.sum(x, axis=0, keepdims=True)


def _tok_call(name, body, tiled, full, out_tiled, out_acc=(), tm=256, scratch=(), dep=None):
    def arr(t):
        return t[0] if isinstance(t, tuple) else t
    full = [t[0] if isinstance(t, tuple) and t[1] is None else t for t in full]
    S = arr(tiled[0]).shape[0]
    tm = min(tm, S)
    n_in = len(tiled) + len(full)
    deps = [] if dep is None else [dep]
    if deps:
        inner = body
        body = lambda *refs: inner(*refs[:n_in], *refs[n_in + 1:])

    def tspec(t):
        if isinstance(t, tuple):
            _, w, cb = t
            return pl.BlockSpec((tm, w), lambda i, cb=cb: (i, cb))
        return pl.BlockSpec((tm, t.shape[1]), lambda i: (i, 0))

    def fspec(t):
        if isinstance(t, tuple):
            a, l = t
            nd = a.ndim - 1
            return pl.BlockSpec((None,) + a.shape[1:], lambda i, l=l, nd=nd: (l,) + (0,) * nd)
        nd = t.ndim
        return pl.BlockSpec(t.shape, lambda i, nd=nd: (0,) * nd)

    def ospec(o):
        return pl.BlockSpec((tm, o.shape[1]), lambda i: (i, 0))

    def aspec(o):
        nd = len(o.shape)
        return pl.BlockSpec(o.shape, lambda i, nd=nd: (0,) * nd)

    outs = pl.pallas_call(
        body, name=name, grid=(S // tm,),
        in_specs=[tspec(t) for t in tiled] + [fspec(t) for t in full] + [ANY] * len(deps),
        out_specs=[ospec(o) for o in out_tiled] + [aspec(o) for o in out_acc],
        out_shape=list(out_tiled) + list(out_acc),
        scratch_shapes=list(scratch),
        compiler_params=_cp(("arbitrary",)),
    )(*[arr(t) for t in tiled], *[arr(t) for t in full], *deps)
    return outs


def _sds(shape, dtype=F32):
    return jax.ShapeDtypeStruct(tuple(shape), dtype)


def _acc(ref, val):
    @pl.when(pl.program_id(0) == 0)
    def _():
        ref[...] = val

    @pl.when(pl.program_id(0) != 0)
    def _():
        ref[...] += val


def mm_in(x, w, nb16=0):
    S, N = x.shape[0], w.shape[1]

    def body(x_ref, w_ref, h_ref, xb_ref, *hb_ref):
        xb = x_ref[...].astype(BF16)
        xb_ref[...] = xb
        h = _dot(xb, w_ref[...])
        h_ref[...] = h
        if nb16:
            hb_ref[0][...] = h[:, 0:nb16].astype(BF16)

    outs = [_sds((S, N)), _sds((S, x.shape[1]), BF16)] + ([_sds((S, nb16), BF16)] if nb16 else [])
    return _tok_call("mm_in", body, [x], [w], outs, tm=512)


def _stack_heads(x, hm0, dtype=BF16):
    return jnp.concatenate([jnp.where(hm0, x, 0), jnp.where(hm0, 0, x)], axis=0).astype(dtype)


def _unstack_k(x, T):
    return jnp.concatenate([x[0:T], x[T:2 * T]], axis=1)


def _cumsum_mm(x, u):
    n = x.shape[0]
    hi, lo = _split2(x)
    r = _dot(jnp.concatenate([hi, lo], axis=0), u)
    return r[0:n] + r[n:2 * n]


def attn_fwd(qkv, T=256):
    S = qkv.shape[0]
    T = min(T, S)
    nq = S // T

    def body(q_ref, k_ref, v_ref, o_ref, t_ref, acc_ref, c_ref, qh_ref):
        i = pl.program_id(0)
        hm0 = lax.broadcasted_iota(jnp.int32, (1, 128), 1) < 64
        r2 = lax.broadcasted_iota(jnp.int32, (2 * T, T), 0)
        c2 = lax.broadcasted_iota(jnp.int32, (2 * T, T), 1)
        causal = c2 < jnp.where(r2 >= T, r2 - T, r2)
        ur = lax.broadcasted_iota(jnp.int32, (T, T), 0)
        uc = lax.broadcasted_iota(jnp.int32, (T, T), 1)
        u_incl = (ur >= uc).astype(BF16)
        acc_ref[...] = jnp.zeros_like(acc_ref)
        c_ref[...] = jnp.zeros_like(c_ref)
        for pp in range(4):
            qh_ref[pp] = _stack_heads(q_ref[:, pp * 128:(pp + 1) * 128] * QK_SCALE, hm0)

        def block(kb, diag):
            ks = pl.multiple_of(kb * T, T)
            cols = [slice(pp * 128, (pp + 1) * 128) for pp in range(4)]
            zs = [_dot_nt(qh_ref[pp], k_ref[pl.ds(ks, T), cols[pp]]) for pp in range(4)]
            incls = []
            for pp in range(4):
                sp = _softplus(zs[pp])
                if diag:
                    sp = jnp.where(causal, sp, 0.0)
                incls.append(_cumsum_mm(sp, u_incl))
            for pp in range(4):
                c = c_ref[pp]
                w = jnp.exp(zs[pp] - incls[pp] - c)
                if diag:
                    w = jnp.where(causal, w, 0.0)
                acc_ref[:, cols[pp]] += _dot(_unstack_k(w.astype(BF16), T),
                                             _stack_heads(v_ref[pl.ds(ks, T), cols[pp]], hm0))
                c_ref[pp] = c + jnp.broadcast_to(incls[pp][:, 0:1], (2 * T, T))

        block(i, True)

        def step(jj, carry):
            block(i - 1 - jj, False)
            return carry

        lax.fori_loop(0, i, step, 0)
        o_ref[...] = acc_ref[...].astype(BF16)
        for pp in range(4):
            for hd in range(2):
                t_ref[2 * pp + hd] = c_ref[pp, hd * T:(hd + 1) * T, 0:128]

    return pl.pallas_call(
        body, name="attn_fwd", grid=(nq,),
        in_specs=[pl.BlockSpec((T, 512), lambda i: (i, 0)),
                  pl.BlockSpec((S, 512), lambda i: (0, 1)),
                  pl.BlockSpec((S, 512), lambda i: (0, 2))],
        out_specs=[pl.BlockSpec((T, 512), lambda i: (i, 0)),
                   pl.BlockSpec((8, T, 128), lambda i: (0, i, 0))],
        out_shape=[_sds((S, 512), BF16), _sds((8, S, 128))],
        scratch_shapes=[pltpu.VMEM((T, 512), F32), pltpu.VMEM((4, 2 * T, T), F32), pltpu.VMEM((4, 2 * T, 128), BF16)],
        compiler_params=_cp(("arbitrary",)),
    )(qkv, qkv, qkv)


def pool_fwd(h, pool_w, pool_scale, CH=256):
    S = h.shape[0]
    CH = min(CH, S)

    def body(u_ref, w_ref, sc_ref, b_ref, pooled_ref, pad_ref):
        pad_ref[0:16, :] = jnp.zeros((16, 512), F32)
        pad_ref[16:16 + S, :] = u_ref[...]
        for g, win in enumerate(POOL_WINDOWS):
            cs = slice(g * 128, (g + 1) * 128)
            wq = w_ref[g].astype(BF16)
            for ch in range(S // CH):
                base = ch * CH
                acc = pad_ref[16 + base:16 + base + CH, cs]
                for sft in range(1, win):
                    acc = acc + pad_ref[16 + base - sft:16 + base - sft + CH, cs]
                t = base + lax.broadcasted_iota(jnp.int32, (CH, 1), 0)
                cnt = jnp.minimum(t + 1, win).astype(F32)
                pooled = (acc / cnt - pad_ref[16 + base:16 + base + CH, cs]).astype(BF16)
                pooled_ref[base:base + CH, cs] = pooled
                b_ref[base:base + CH, cs] = (_dot(pooled, wq) * sc_ref[:, cs]).astype(BF16)

    return pl.pallas_call(
        body, name="pool_fwd", grid=(1,),
        in_specs=[pl.BlockSpec((S, 512), lambda i: (0, 3)),
                  pl.BlockSpec((4, 128, 128), lambda i: (0, 0, 0)),
                  pl.BlockSpec((1, 512), lambda i: (0, 0))],
        out_specs=[pl.BlockSpec((S, 512), lambda i: (0, 0)), pl.BlockSpec((S, 512), lambda i: (0, 0))],
        out_shape=[_sds((S, 512), BF16), _sds((S, 512), BF16)],
        scratch_shapes=[pltpu.VMEM((S + 16, 512), F32)],
        compiler_params=_cp(("arbitrary",)),
    )(h, pool_w, pool_scale)


def conv_fwd(h, dw, CH=128):
    S = h.shape[0]

    def body(a_ref, g_ref, dw_ref, y_ref, hc_ref, pad_ref):
        hc = a_ref[...] * _sigmoid(g_ref[...])
        hc_ref[...] = hc
        pad_ref[0:32, :] = jnp.zeros((32, 128), F32)
        pad_ref[32:32 + S, :] = hc
        for ch in range(S // CH):
            base = ch * CH + 2
            acc = dw_ref[0:1, :] * pad_ref[base:base + CH, :]
            for k in range(1, CONV_TAPS):
                acc = acc + dw_ref[k:k + 1, :] * pad_ref[base + k:base + k + CH, :]
            y_ref[ch * CH:(ch + 1) * CH, :] = acc

    return pl.pallas_call(
        body, name="conv_fwd", grid=(4,),
        in_specs=[pl.BlockSpec((S, 128), lambda c: (0, c)),
                  pl.BlockSpec((S, 128), lambda c: (0, 4 + c)),
                  pl.BlockSpec((CONV_TAPS, 128), lambda c: (0, c))],
        out_specs=[pl.BlockSpec((S, 128), lambda c: (0, c)), pl.BlockSpec((S, 128), lambda c: (0, c))],
        out_shape=[_sds((S, 512)), _sds((S, 512))],
        scratch_shapes=[pltpu.VMEM((S + 32, 128), F32)],
        compiler_params=_cp(("arbitrary",)),
    )(h, h, dw)


def _masked_sg_w(w_ref, g):
    row = lax.broadcasted_iota(jnp.int32, (128, 128), 0)
    col = lax.broadcasted_iota(jnp.int32, (128, 128), 1)
    return jnp.where(row >= col, w_ref[g], 0.0).astype(BF16)


def odd_post(y, h, cl_g, cl_b, sl_g, sl_b, sg_w, sgb_bc, tm=512):
    S = y.shape[0]
    tm = min(tm, S)

    def body(y_ref, zc_ref, clg, clb, slg, slb, w_ref, sb_ref,
             c_ref, d_ref, xhc_ref, rsc_ref, xhv_ref, rsv_ref, sv_ref):
        lnc, xhc, rsc = _ln_fwd(y_ref[...], clg[...], clb[...])
        c_ref[...] = (lnc * _sigmoid(lnc)).astype(BF16)
        xhc_ref[...] = xhc
        rsc_ref[...] = rsc
        z = _gelu(zc_ref[...])
        vn, xhv, rsv = _ln_fwd(z[:, 512:], slg[...], slb[...])
        xhv_ref[...] = xhv
        rsv_ref[...] = rsv
        vnb = vn.astype(BF16)
        for g in range(4):
            wm = _masked_sg_w(w_ref, g)
            for ch in range(tm // 128):
                rs, cs = slice(ch * 128, (ch + 1) * 128), slice(g * 128, (g + 1) * 128)
                sv_ref[rs, cs] = _dot(wm, vnb[rs, cs]) + sb_ref[g]
        d_ref[...] = (z[:, :512] * sv_ref[...]).astype(BF16)

    return _tok_call(
        "odd_post", body, [y, (h, 1024, 1)], [cl_g, cl_b, sl_g, sl_b, sg_w, sgb_bc],
        [_sds((S, 512), BF16), _sds((S, 512), BF16), _sds((S, 512)), _sds((S, 1)),
         _sds((S, 512)), _sds((S, 1)), _sds((S, 512))], tm=tm)


def mm_out_ln(l1, l2, x, w, g, b, dep=None):
    S, D = x.shape

    def body(l1_ref, l2_ref, x_ref, w_ref, g_ref, b_ref, y_ref, xh_ref, rs_ref):
        mix = _dot(l1_ref[...], w_ref[0:512, :]) + _dot(l2_ref[...], w_ref[512:1024, :])
        y, xh, rs = _ln_fwd(ALPHA * x_ref[...] + mix, g_ref[...], b_ref[...])
        y_ref[...] = y
        xh_ref[...] = xh
        rs_ref[...] = rs

    return _tok_call("mm_out_ln", body, [l1, l2, x], [w, g, b],
                     [_sds((S, D)), _sds((S, D)), _sds((S, 1))], dep=dep, tm=512)


def ffn_up(x1, wg, wu, layer, dep=None):
    S, D = x1.shape
    F = wg.shape[-1]

    def body(x_ref, wg_ref, wu_ref, gate_ref, up_ref, hb_ref, xb_ref):
        xb = x_ref[...].astype(BF16)
        xb_ref[...] = xb
        gate = _dot(xb, wg_ref[...])
        up = _dot(xb, wu_ref[...])
        gate_ref[...] = gate.astype(BF16)
        up_ref[...] = up.astype(BF16)
        hb_ref[...] = (gate * _sigmoid(gate) * up).astype(BF16)

    return _tok_call("ffn_up", body, [x1], [(wg, layer), (wu, layer)],
                     [_sds((S, F), BF16), _sds((S, F), BF16), _sds((S, F), BF16), _sds((S, D), BF16)], dep=dep)


def ffn_down_ln(hb, x1, wd, layer, g, b):
    S, D = x1.shape

    def body(h_ref, x_ref, w_ref, g_ref, b_ref, y_ref, xh_ref, rs_ref):
        f = _dot(h_ref[...], w_ref[...])
        y, xh, rs = _ln_fwd(ALPHA * x_ref[...] + f, g_ref[...], b_ref[...])
        y_ref[...] = y
        xh_ref[...] = xh
        rs_ref[...] = rs

    return _tok_call("ffn_down_ln", body, [hb, x1], [(wd, layer), g, b],
                     [_sds((S, D)), _sds((S, D)), _sds((S, 1))], tm=512)


def ple_fwd(x2, p, wpg, wpp, layer, bg, target=None, dep=None):
    S, D = x2.shape
    last = target is not None

    def body(*refs):
        if last:
            x_ref, p_ref, t_ref, wg_ref, wp_ref, b_ref, x3_ref, sg_ref, pp_ref, xb_ref, pb_ref, dy_ref, ls_ref = refs
        else:
            x_ref, p_ref, wg_ref, wp_ref, b_ref, x3_ref, sg_ref, pp_ref, xb_ref, pb_ref = refs
        x = x_ref[...]
        xb = x.astype(BF16)
        pb = p_ref[...].astype(BF16)
        xb_ref[...] = xb
        pb_ref[...] = pb
        sg = _sigmoid(_dot(xb, wg_ref[...]) + b_ref[...])
        pp = _dot(pb, wp_ref[...])
        sg_ref[...] = sg.astype(BF16)
        pp_ref[...] = pp.astype(BF16)
        x3 = x + sg * pp
        x3_ref[...] = x3
        if last:
            err = x3 - t_ref[...]
            dy_ref[...] = err * (1.0 / D)
            _acc(ls_ref, _colsum(err * err))

    outs = [_sds((S, D)), _sds((S, D), BF16), _sds((S, D), BF16), _sds((S, D), BF16), _sds((S, p.shape[1]), BF16)]
    tiled = [x2, p] + ([target] if last else [])
    if last:
        outs.append(_sds((S, D)))
    return _tok_call("ple_fwd", body, tiled, [(wpg, layer), (wpp, layer), bg], outs,
                     [_sds((1, D))] if last else [], dep=dep, tm=512)


def ple_ln_bwd(dx3, sg, pp, x2b, pb, wpg, xh, rs, g, dep=None):
    S, D = dx3.shape

    def body(d_ref, sg_ref, pp_ref, x2b_ref, pb_ref, xh_ref, rs_ref, w_ref, g_ref,
             dr_ref, drb_ref, dbg_ref, dlg_ref, dlb_ref, dwg_ref, dwp_ref, accg_ref, accp_ref):
        fin_g = _sum_steps(accg_ref, dwg_ref)
        fin_p = _sum_steps(accp_ref, dwp_ref)
        d, sg = d_ref[...], sg_ref[...].astype(F32)
        dgp = d * pp_ref[...].astype(F32) * sg * (1.0 - sg)
        dgpb = dgp.astype(BF16)
        accg_ref[...] += _dot_tn(x2b_ref[...], dgpb)
        accp_ref[...] += _dot_tn(pb_ref[...], (d * sg).astype(BF16))
        _acc(dbg_ref, _colsum(dgp))
        dx2 = d + _dot_nt(dgpb, w_ref[...])
        xh = xh_ref[...]
        dr = _ln_bwd(dx2, xh, rs_ref[...], g_ref[...])
        dr_ref[...] = dr
        drb_ref[...] = dr.astype(BF16)
        _acc(dlg_ref, _colsum(dx2 * xh))
        _acc(dlb_ref, _colsum(dx2))
        fin_g()
        fin_p()

    P = pb.shape[1]
    return _tok_call("ple_ln_bwd", body, [dx3, sg, pp, x2b, pb, xh, rs], [wpg, g],
                     [_sds((S, D)), _sds((S, D), BF16)],
                     [_sds((1, D)), _sds((1, D)), _sds((1, D)), _sds((D, D), BF16), _sds((P, D), BF16)],
                     scratch=[pltpu.VMEM((D, D), F32), pltpu.VMEM((P, D), F32)], dep=dep, tm=ACC_TM)


def ffn_bwd(dr_b, x1b, gate, up, hb, wg, wu, wd, TH=512):
    S, D = dr_b.shape
    F = gate.shape[1]

    def body(dr_hbm, x_hbm, gate_ref, up_ref, hb_ref, wg_ref, wu_ref, wd_ref,
             dx_ref, dwg_ref, dwu_ref, dwd_ref, dr_v, x_v, sem, dg_s, du_s):
        @pl.when(pl.program_id(0) == 0)
        def _():
            c1 = pltpu.make_async_copy(dr_hbm, dr_v, sem.at[0])
            c2 = pltpu.make_async_copy(x_hbm, x_v, sem.at[1])
            c1.start()
            c2.start()
            c1.wait()
            c2.wait()
            dx_ref[...] = jnp.zeros_like(dx_ref)

        for ch in range(S // CH):
            rows = slice(ch * CH, (ch + 1) * CH)
            dh = _dot_nt(dr_v[rows, :], wd_ref[...])
            g, u = gate_ref[rows, :].astype(F32), up_ref[rows, :].astype(F32)
            s = _sigmoid(g)
            dgb = (dh * u * s * (1.0 + g * (1.0 - s))).astype(BF16)
            dub = (dh * g * s).astype(BF16)
            dg_s[rows, :] = dgb
            du_s[rows, :] = dub
            dx_ref[rows, :] += _dot_nt(dgb, wg_ref[...]) + _dot_nt(dub, wu_ref[...])
        x = x_v[...]
        dwg_ref[...] = _dot_tn(x, dg_s[...]).astype(BF16)
        dwu_ref[...] = _dot_tn(x, du_s[...]).astype(BF16)
        dwd_ref[...] = _dot_tn(hb_ref[...], dr_v[...]).astype(BF16)

    CH = min(256, S)
    col = lambda rows: pl.BlockSpec((rows, TH), lambda j: (0, j))
    row = pl.BlockSpec((TH, D), lambda j: (j, 0))
    return pl.pallas_call(
        body, name="ffn_bwd", grid=(F // TH,),
        in_specs=[ANY, ANY, col(S), col(S), col(S), col(D), col(D), row],
        out_specs=[pl.BlockSpec((S, D), lambda j: (0, 0)), col(D), col(D), row],
        out_shape=[_sds((S, D)), _sds((D, F), BF16), _sds((D, F), BF16), _sds((F, D), BF16)],
        scratch_shapes=[pltpu.VMEM((S, D), BF16), pltpu.VMEM((S, D), BF16), pltpu.SemaphoreType.DMA((2,)),
                        pltpu.VMEM((S, TH), BF16), pltpu.VMEM((S, TH), BF16)],
        compiler_params=pltpu.CompilerParams(dimension_semantics=("arbitrary",), vmem_limit_bytes=60 * 1024 * 1024),
    )(dr_b, x1b, gate, up, hb, wg, wu, wd)


ACC_TM = 512


def _sum_steps(acc_ref, out_ref):
    @pl.when(pl.program_id(0) == 0)
    def _():
        acc_ref[...] = jnp.zeros_like(acc_ref)

    def finish():
        @pl.when(pl.program_id(0) == pl.num_programs(0) - 1)
        def _():
            out_ref[...] = acc_ref[...].astype(BF16)
    return finish


def mix_bwd(dxp, dr2, xh, rs, l1, l2, g, w, dep=None):
    S, D = dxp.shape
    K1 = l1.shape[1]

    def body(dxp_ref, dr2_ref, xh_ref, rs_ref, l1_ref, l2_ref, g_ref, w_ref,
             dr_ref, dl_ref, dlg_ref, dlb_ref, dw_ref, acc_ref):
        finish = _sum_steps(acc_ref, dw_ref)
        d, xh = ALPHA * dr2_ref[...] + dxp_ref[...], xh_ref[...]
        dr = _ln_bwd(d, xh, rs_ref[...], g_ref[...])
        drb = dr.astype(BF16)
        dr_ref[...] = dr
        dl_ref[...] = _dot_nt(drb, w_ref[...])
        _acc(dlg_ref, _colsum(d * xh))
        _acc(dlb_ref, _colsum(d))
        acc_ref[0:K1, :] += _dot_tn(l1_ref[...], drb)
        acc_ref[K1:, :] += _dot_tn(l2_ref[...], drb)
        finish()

    return _tok_call("mix_bwd", body, [dxp, dr2, xh, rs, l1, l2], [g, w],
                     [_sds((S, D)), _sds((S, D))], [_sds((1, D)), _sds((1, D)), _sds(w.shape, BF16)],
                     scratch=[pltpu.VMEM(w.shape, F32)], dep=dep, tm=ACC_TM)


def dx_in(dr, pieces, w, xb):
    S, D = dr.shape
    offs = [o for _, o in pieces]
    widths = [a.shape[1] for a, _ in pieces]
    npc = len(pieces)

    def body(*refs):
        dr_ref, prefs, xb_ref, w_ref = refs[0], refs[1:1 + npc], refs[1 + npc], refs[2 + npc]
        dx_ref, dw_ref, acc_ref = refs[3 + npc:]
        finish = _sum_steps(acc_ref, dw_ref)
        acc = ALPHA * dr_ref[...]
        xb_t = xb_ref[...]
        for pr, o, n in zip(prefs, offs, widths):
            piece = pr[...]
            acc = acc + _dot_nt(piece, w_ref[:, o:o + n])
            acc_ref[:, o:o + n] += _dot_tn(xb_t, piece)
        dx_ref[...] = acc
        finish()

    return _tok_call("dx_in", body, [dr] + [a for a, _ in pieces] + [xb], [w], [_sds((S, D))],
                     [_sds(w.shape, BF16)], scratch=[pltpu.VMEM(w.shape, F32)], tm=ACC_TM)


def odd_post_bwd(dl, h, xhc, rsc, xhv, rsv, sv, cl_g, cl_b, sl_g, sl_b, sg_w, tm=512, dep=None):
    S = dl.shape[0]
    tm = min(tm, S)

    def body(dl_ref, zc_ref, xhc_ref, rsc_ref, xhv_ref, rsv_ref, sv_ref, clg, clb, slg, slb, w_ref,
             dy_ref, dzc_ref, dclg_ref, dclb_ref, dslg_ref, dslb_ref, dwm_ref, dsb_ref, dvn_ref):
        first = pl.program_id(0) == 0
        last = pl.program_id(0) == pl.num_programs(0) - 1
        dc, dd = dl_ref[:, 0:512], dl_ref[:, 512:1024]
        xhc = xhc_ref[...]
        lnc = xhc * clg[...] + clb[...]
        s = _sigmoid(lnc)
        dlnc = dc * s * (1.0 + lnc * (1.0 - s))
        dy_ref[...] = _ln_bwd(dlnc, xhc, rsc_ref[...], clg[...])
        _acc(dclg_ref, _colsum(dlnc * xhc))
        _acc(dclb_ref, _colsum(dlnc))
        zc = zc_ref[...]
        z = _gelu(zc)
        dsv = dd * z[:, :512]
        dsvb = dsv.astype(BF16)
        xhv = xhv_ref[...]
        vnb = (xhv * slg[...] + slb[...]).astype(BF16)

        @pl.when(first)
        def _():
            dwm_ref[...] = jnp.zeros_like(dwm_ref)
            dsb_ref[...] = jnp.zeros_like(dsb_ref)

        for g in range(4):
            wm = _masked_sg_w(w_ref, g)
            for ch in range(tm // 128):
                rs_, cs = slice(ch * 128, (ch + 1) * 128), slice(g * 128, (g + 1) * 128)
                dwm_ref[g] += _dot_nt(dsvb[rs_, cs], vnb[rs_, cs])
                dvn_ref[rs_, cs] = _dot_tn(wm, dsvb[rs_, cs])
                dsb_ref[g] += dsv[rs_, cs]
        dvn = dvn_ref[...]
        dvv = _ln_bwd(dvn, xhv, rsv_ref[...], slg[...])
        _acc(dslg_ref, _colsum(dvn * xhv))
        _acc(dslb_ref, _colsum(dvn))
        gg = _gelu_grad(zc)
        dzc_ref[:, 0:512] = (dd * sv_ref[...] * gg[:, :512]).astype(BF16)
        dzc_ref[:, 512:1024] = (dvv * gg[:, 512:]).astype(BF16)

        @pl.when(last)
        def _():
            row = lax.broadcasted_iota(jnp.int32, (128, 128), 0)
            col = lax.broadcasted_iota(jnp.int32, (128, 128), 1)
            for g in range(4):
                dwm_ref[g] = jnp.where(row >= col, dwm_ref[g], 0.0)
                dsb_ref[g] = jnp.broadcast_to(jnp.sum(dsb_ref[g], axis=1, keepdims=True), (128, 128))

    return _tok_call(
        "odd_post_bwd", body, [dl, (h, 1024, 1), xhc, rsc, xhv, rsv, sv], [cl_g, cl_b, sl_g, sl_b, sg_w],
        [_sds((S, 512)), _sds((S, 1024), BF16)],
        [_sds((1, 512)), _sds((1, 512)), _sds((1, 512)), _sds((1, 512)), _sds((4, 128, 128)), _sds((4, 128, 128))],
        tm=tm, scratch=[pltpu.VMEM((tm, 512), F32)], dep=dep)


def conv_bwd(dy, hc, h, dw, CH=128):
    S = dy.shape[0]

    def body(dy_ref, hc_ref, a_ref, g_ref, dw_ref, da_ref, dg_ref, ddw_ref, padh_ref, padd_ref, dhc_ref):
        padh_ref[0:32, :] = jnp.zeros((32, 128), F32)
        padh_ref[32:32 + S, :] = hc_ref[...]
        padd_ref[0:S, :] = dy_ref[...]
        padd_ref[S:S + 32, :] = jnp.zeros((32, 128), F32)
        taps = [jnp.zeros((1, 128), F32) for _ in range(CONV_TAPS)]
        for ch in range(S // CH):
            b0 = ch * CH
            dyc = padd_ref[b0:b0 + CH, :]
            acc = dw_ref[0:1, :] * padd_ref[b0 + 30:b0 + 30 + CH, :]
            taps[0] = taps[0] + _colsum(dyc * padh_ref[b0 + 2:b0 + 2 + CH, :])
            for k in range(1, CONV_TAPS):
                acc = acc + dw_ref[k:k + 1, :] * padd_ref[b0 + 30 - k:b0 + 30 - k + CH, :]
                taps[k] = taps[k] + _colsum(dyc * padh_ref[b0 + 2 + k:b0 + 2 + k + CH, :])
            dhc_ref[b0:b0 + CH, :] = acc
        for k in range(CONV_TAPS):
            ddw_ref[k:k + 1, :] = taps[k]
        dhc = dhc_ref[...]
        s = _sigmoid(g_ref[...])
        da_ref[...] = (dhc * s).astype(BF16)
        dg_ref[...] = (dhc * a_ref[...] * s * (1.0 - s)).astype(BF16)

    return pl.pallas_call(
        body, name="conv_bwd", grid=(4,),
        in_specs=[pl.BlockSpec((S, 128), lambda c: (0, c)),
                  pl.BlockSpec((S, 128), lambda c: (0, c)),
                  pl.BlockSpec((S, 128), lambda c: (0, c)),
                  pl.BlockSpec((S, 128), lambda c: (0, 4 + c)),
                  pl.BlockSpec((CONV_TAPS, 128), lambda c: (0, c))],
        out_specs=[pl.BlockSpec((S, 128), lambda c: (0, c)), pl.BlockSpec((S, 128), lambda c: (0, c)),
                   pl.BlockSpec((CONV_TAPS, 128), lambda c: (0, c))],
        out_shape=[_sds((S, 512), BF16), _sds((S, 512), BF16), _sds((CONV_TAPS, 512))],
        scratch_shapes=[pltpu.VMEM((S + 32, 128), F32), pltpu.VMEM((S + 32, 128), F32), pltpu.VMEM((S, 128), F32)],
        compiler_params=_cp(("arbitrary",)),
    )(dy, hc, h, h, dw)


def attn_bwd(qkv, dl, tb, T=256, dep=None):
    S = qkv.shape[0]
    T = min(T, S)
    nq = S // T

    def body(q_ref, k_ref, v_ref, do_ref, t_ref, dq_ref, dk_ref, dv_ref,
             dka_ref, dva_ref, dqa_ref, pc_ref, gc_ref, qh_ref, doh_ref):
        i = pl.program_id(0)
        hm0 = lax.broadcasted_iota(jnp.int32, (1, 128), 1) < 64
        r2 = lax.broadcasted_iota(jnp.int32, (2 * T, T), 0)
        c2 = lax.broadcasted_iota(jnp.int32, (2 * T, T), 1)
        causal = c2 < jnp.where(r2 >= T, r2 - T, r2)
        ur = lax.broadcasted_iota(jnp.int32, (T, T), 0)
        uc = lax.broadcasted_iota(jnp.int32, (T, T), 1)
        u_le = (ur <= uc).astype(BF16)
        u_lt = (ur < uc).astype(BF16)

        @pl.when(i == 0)
        def _():
            dka_ref[...] = jnp.zeros_like(dka_ref)
            dva_ref[...] = jnp.zeros_like(dva_ref)

        dqa_ref[...] = jnp.zeros_like(dqa_ref)
        gc_ref[...] = jnp.zeros_like(gc_ref)
        for pp in range(4):
            cs = slice(pp * 128, (pp + 1) * 128)
            qh_ref[pp] = _stack_heads(q_ref[:, cs] * QK_SCALE, hm0)
            doh_ref[pp] = _stack_heads(do_ref[:, cs], hm0)
            for hd in range(2):
                for half in range(T // 128):
                    pc_ref[pp, hd * T:(hd + 1) * T, half * 128:(half + 1) * 128] = t_ref[2 * pp + hd]

        def block(kb, diag):
            ks = pl.multiple_of(kb * T, T)
            cols = [slice(pp * 128, (pp + 1) * 128) for pp in range(4)]
            zs = [_dot_nt(qh_ref[pp], k_ref[pl.ds(ks, T), cols[pp]]) for pp in range(4)]
            dws = [_dot_nt(doh_ref[pp], v_ref[pl.ds(ks, T), cols[pp]]) for pp in range(4)]
            a_s, pres = [], []
            for pp in range(4):
                sp = _softplus(zs[pp])
                a_s.append(zs[pp] - sp)
                if diag:
                    sp = jnp.where(causal, sp, 0.0)
                pres.append(_cumsum_mm(sp, u_le))
            ws, gmats, gsums = [], [], []
            for pp in range(4):
                rem = pc_ref[pp]
                w = jnp.exp(a_s[pp] - rem + pres[pp])
                if diag:
                    w = jnp.where(causal, w, 0.0)
                gmat = dws[pp] * w
                ws.append(w.astype(BF16))
                gmats.append(gmat)
                gsums.append(_dot(gmat.astype(BF16), u_lt))
                pc_ref[pp] = rem - jnp.broadcast_to(pres[pp][:, T - 1:T], (2 * T, T))
            for pp in range(4):
                cs = cols[pp]
                sig = jnp.exp(a_s[pp])
                gex = gc_ref[pp] + gsums[pp]
                dz = gmats[pp] - sig * (gmats[pp] + gex)
                if diag:
                    dz = jnp.where(causal, dz, 0.0)
                dzb = dz.astype(BF16)
                dqa_ref[:, cs] += _dot(_unstack_k(dzb, T), _stack_heads(k_ref[pl.ds(ks, T), cs], hm0))
                dka_ref[pl.ds(ks, T), cs] += _dot_tn(dzb, qh_ref[pp])
                dva_ref[pl.ds(ks, T), cs] += _dot_tn(ws[pp], doh_ref[pp])
                gc_ref[pp] = jnp.broadcast_to(gex[:, T - 1:T] + gmats[pp][:, T - 1:T], (2 * T, T))

        def step(kb, carry):
            block(kb, False)
            return carry

        lax.fori_loop(0, i, step, 0)
        block(i, True)
        dq_ref[...] = (dqa_ref[...] * QK_SCALE).astype(BF16)

        @pl.when(i == nq - 1)
        def _():
            dk_ref[...] = dka_ref[...].astype(BF16)
            dv_ref[...] = dva_ref[...].astype(BF16)

    deps = [] if dep is None else [dep]
    call_body = body if dep is None else (lambda *refs: body(*refs[:5], *refs[6:]))
    return pl.pallas_call(
        call_body, name="attn_bwd", grid=(nq,),
        in_specs=[pl.BlockSpec((T, 512), lambda i: (i, 0)),
                  pl.BlockSpec((S, 512), lambda i: (0, 1)),
                  pl.BlockSpec((S, 512), lambda i: (0, 2)),
                  pl.BlockSpec((T, 512), lambda i: (i, 0)),
                  pl.BlockSpec((8, T, 128), lambda i: (0, i, 0))] + [ANY] * len(deps),
        out_specs=[pl.BlockSpec((T, 512), lambda i: (i, 0)),
                   pl.BlockSpec((S, 512), lambda i: (0, 0)),
                   pl.BlockSpec((S, 512), lambda i: (0, 0))],
        out_shape=[_sds((S, 512), BF16), _sds((S, 512), BF16), _sds((S, 512), BF16)],
        scratch_shapes=[pltpu.VMEM((S, 512), F32), pltpu.VMEM((S, 512), F32), pltpu.VMEM((T, 512), F32),
                        pltpu.VMEM((4, 2 * T, T), F32), pltpu.VMEM((4, 2 * T, T), F32),
                        pltpu.VMEM((4, 2 * T, 128), BF16), pltpu.VMEM((4, 2 * T, 128), BF16)],
        compiler_params=_cp(("arbitrary",)),
    )(qkv, qkv, qkv, dl, tb, *deps)


def pool_bwd(dl, pooled_b, pool_w, pool_scale, CH=256):
    S = dl.shape[0]
    CH = min(CH, S)

    def body(db_ref, pooled_ref, w_ref, sc_ref, du_ref, dw_ref, dsc_ref, pad_ref, dp_ref):
        pad_ref[S:S + 16, :] = jnp.zeros((16, 128), F32)
        for g, win in enumerate(POOL_WINDOWS):
            cs = slice(g * 128, (g + 1) * 128)
            wq = w_ref[g].astype(BF16)
            dwg = jnp.zeros((128, 128), F32)
            dsc = jnp.zeros((1, 128), F32)
            for ch in range(S // CH):
                rs_ = slice(ch * CH, (ch + 1) * CH)
                db = db_ref[rs_, cs]
                pb = pooled_ref[rs_, cs]
                dsc = dsc + _colsum(db * _dot(pb, wq))
                dmsb = (db * sc_ref[:, cs]).astype(BF16)
                dwg = dwg + _dot_tn(pb, dmsb)
                dpool = _dot_nt(dmsb, wq)
                t = ch * CH + lax.broadcasted_iota(jnp.int32, (CH, 1), 0)
                cnt = jnp.minimum(t + 1, win).astype(F32)
                dp_ref[rs_, :] = dpool
                pad_ref[rs_, :] = dpool / cnt
            dw_ref[g] = dwg
            dsc_ref[:, cs] = dsc
            for ch in range(S // CH):
                base = ch * CH
                acc = pad_ref[base:base + CH, :]
                for sft in range(1, win):
                    acc = acc + pad_ref[base + sft:base + sft + CH, :]
                du_ref[base:base + CH, cs] = (acc - dp_ref[base:base + CH, :]).astype(BF16)

    return pl.pallas_call(
        body, name="pool_bwd", grid=(1,),
        in_specs=[pl.BlockSpec((S, 512), lambda i: (0, 1)),
                  pl.BlockSpec((S, 512), lambda i: (0, 0)),
                  pl.BlockSpec((4, 128, 128), lambda i: (0, 0, 0)),
                  pl.BlockSpec((1, 512), lambda i: (0, 0))],
        out_specs=[pl.BlockSpec((S, 512), lambda i: (0, 0)),
                   pl.BlockSpec((4, 128, 128), lambda i: (0, 0, 0)),
                   pl.BlockSpec((1, 512), lambda i: (0, 0))],
        out_shape=[_sds((S, 512), BF16), _sds((4, 128, 128)), _sds((1, 512))],
        scratch_shapes=[pltpu.VMEM((S + 16, 128), F32), pltpu.VMEM((S, 128), F32)],
        compiler_params=_cp(("arbitrary",)),
    )(dl, pooled_b, pool_w, pool_scale)


def _row(a, i):
    return a[i:i + 1]


MIXER_NAMES = (("even_w_in", "even_w_out"), ("odd_w_in", "odd_w_out"))


def fwd_layer(i, xin, p_i, target, comm):
    s = {}
    W = comm.weights(("mix", i), xin)
    w_in = W[MIXER_NAMES[i][0]]
    if i == 0:
        s["h"], s["xb"], s["qkv"] = mm_in(xin, w_in, nb16=1536)
        comm.poke(("in", i), s["h"])
        s["l1"], s["tb"] = attn_fwd(s["qkv"])
        s["l2"], s["pooled"] = pool_fwd(s["h"], W["pool_w"], W["pool_scale"])
    else:
        s["h"], s["xb"] = mm_in(xin, w_in)
        comm.poke(("in", i), s["h"])
        s["y"], s["hc"] = conv_fwd(s["h"], W["conv_dw"])
        sgb_bc = jnp.broadcast_to(W["sg_b"][:, :, None], (4, 128, 128))
        (s["l1"], s["l2"], s["xhc"], s["rsc"], s["xhv"], s["rsv"], s["sv"]) = odd_post(
            s["y"], s["h"], W["conv_ln_g"], W["conv_ln_b"], W["sg_ln_g"], W["sg_ln_b"], W["sg_w"], sgb_bc)
    tok = comm.poke(("mixed", i), s["l1"])
    W = comm.weights(("out", i), s["l1"])
    x1, s["xh1"], s["rs1"] = mm_out_ln(s["l1"], s["l2"], xin, W[MIXER_NAMES[i][1]], _row(W["ln_mix_g"], i),
                                       _row(W["ln_mix_b"], i), dep=tok)
    W = comm.weights(("ffn", i), x1)
    tok = comm.poke(("up", i), x1)
    s["gate"], s["up"], s["hb"], s["x1b"] = ffn_up(x1, W["ffn_w_gate%d" % i], W["ffn_w_up%d" % i], None, dep=tok)
    W = comm.weights(("down", i), s["hb"])
    x2, s["xh2"], s["rs2"] = ffn_down_ln(s["hb"], x1, W["ffn_w_down%d" % i], None,
                                         _row(W["ln_ffn_g"], i), _row(W["ln_ffn_b"], i))
    tok = comm.poke(("ffn", i), x2)
    outs = ple_fwd(x2, p_i, W["ple_w_gate%d" % i], W["ple_w_proj%d" % i], None, _row(W["ple_b_gate"], i), target,
                   dep=tok)
    s["sg"], s["pp"], s["x2b"], s["pb"] = outs[1:5]
    return outs[0], s, outs[5:]


def bwd_layer(i, dx, s, W, comm, tok=None):
    small = {}
    dr2, dr2_b, small["ple_b_gate"], small["ln_ffn_g"], small["ln_ffn_b"], dwpg, dwpp = ple_ln_bwd(
        dx, s["sg"], s["pp"], s["x2b"], s["pb"], W["ple_w_gate%d" % i], s["xh2"], s["rs2"],
        _row(W["ln_ffn_g"], i), dep=tok)
    dxp, dwg, dwu, dwd = ffn_bwd(dr2_b, s["x1b"], s["gate"], s["up"], s["hb"], W["ffn_w_gate%d" % i],
                                 W["ffn_w_up%d" % i], W["ffn_w_down%d" % i])
    tok = comm.grads({"ple_w_gate%d" % i: dwpg, "ple_w_proj%d" % i: dwpp, "ffn_w_down%d" % i: dwd,
                      "ffn_w_gate%d" % i: dwg, "ffn_w_up%d" % i: dwu})
    iname, oname = MIXER_NAMES[i]
    dr1, dl, small["ln_mix_g"], small["ln_mix_b"], dwout = mix_bwd(
        dxp, dr2, s["xh1"], s["rs1"], s["l1"], s["l2"], _row(W["ln_mix_g"], i), W[oname], dep=tok)
    tok = comm.poke(("bwd", i), dl)
    if i == 1:
        (dy, dzc_b, small["conv_ln_g"], small["conv_ln_b"], small["sg_ln_g"], small["sg_ln_b"],
         small["sg_w"], dsb) = odd_post_bwd(dl, s["h"], s["xhc"], s["rsc"], s["xhv"], s["rsv"], s["sv"],
                                            W["conv_ln_g"], W["conv_ln_b"], W["sg_ln_g"], W["sg_ln_b"], W["sg_w"],
                                            dep=tok)
        small["sg_b"] = dsb[:, :, 0]
        da_b, dg_b, small["conv_dw"] = conv_bwd(dy, s["hc"], s["h"], W["conv_dw"])
        pieces = [(da_b, 0), (dg_b, 512), (dzc_b, 1024)]
    else:
        dq_b, dk_b, dv_b = attn_bwd(s["qkv"], dl, s["tb"], dep=tok)
        du_b, small["pool_w"], small["pool_scale"] = pool_bwd(dl, s["pooled"], W["pool_w"], W["pool_scale"])
        pieces = [(dq_b, 0), (dk_b, 512), (dv_b, 1024), (du_b, 1536)]
    dxin, dwin = dx_in(dr1, pieces, W[iname], s["xb"])
    tok = comm.grads({oname: dwout, iname: dwin})
    return dxin, small, tok


def run_layers(x, p, target, comm):
    saved, xin = [], x
    for i in range(2):
        xin, s, extra = fwd_layer(i, xin, p[i], target if i == 1 else None, comm)
        saved.append(s)
    dx, sq = extra
    W = comm.all_weights()
    per_layer = [None, None]
    tok = None
    for i in (1, 0):
        dx, per_layer[i], tok = bwd_layer(i, dx, saved[i], W, comm, tok)
    small = {}
    for k in ("ln_mix_g", "ln_mix_b", "ln_ffn_g", "ln_ffn_b", "ple_b_gate"):
        small[k] = jnp.concatenate([per_layer[0][k], per_layer[1][k]], axis=0)
    for i in range(2):
        small.update({k: v for k, v in per_layer[i].items() if k not in small})
    return sq, dx, small


def _big_table():
    t = {}
    for nm in ("even", "odd"):
        t[nm + "_w_in"] = ((1024, 2048), 1, 256, 256, nm + "_w_in", 0)
        t[nm + "_w_out"] = ((1024, 1024), 0, 128, 128, nm + "_w_out", 0)
    for l in range(2):
        t["ffn_w_gate%d" % l] = ((1024, 8 * FF_PAD), 1, FF_PAD, FF_SHARD, "ffn_w_gate", l)
        t["ffn_w_up%d" % l] = ((1024, 8 * FF_PAD), 1, FF_PAD, FF_SHARD, "ffn_w_up", l)
        t["ffn_w_down%d" % l] = ((8 * FF_PAD, 1024), 0, FF_PAD, FF_SHARD, "ffn_w_down", l)
        t["ple_w_gate%d" % l] = ((1024, 1024), 0, 128, 128, "ple_w_gate", l)
        t["ple_w_proj%d" % l] = ((256, 1024), 1, 128, 128, "ple_w_proj", l)
    return t


BIG = _big_table()
TRANSPOSED_ARGS = ("ffn_w_gate", "ffn_w_up")
SMALL_SPEC = ((N_DEV, 40, 64), 0, 1, 1)
_UP_GROUP = lambda l: ["ffn_w_gate%d" % l, "ffn_w_up%d" % l]
_DOWN_GROUP = lambda l: ["ffn_w_down%d" % l, "ple_w_gate%d" % l, "ple_w_proj%d" % l]
AG_GROUPS = (["even_w_in"], ["even_w_out"], _UP_GROUP(0), _DOWN_GROUP(0), ["odd_w_in", "odd_w_out", "small"],
             _UP_GROUP(1), _DOWN_GROUP(1))
AG_NEED = {("mix", 0): 0, ("out", 0): 1, ("ffn", 0): 2, ("down", 0): 3, ("mix", 1): 4, ("ffn", 1): 5, ("down", 1): 6}
AG_PASS = {("in", 0): 1, ("mixed", 0): 2, ("up", 0): 3, ("ffn", 0): 4, ("mixed", 1): 5, ("up", 1): 6}
ANY = pl.BlockSpec(memory_space=pl.ANY)
SEM = pl.BlockSpec(memory_space=pltpu.SEMAPHORE)


def _spec(name):
    return SMALL_SPEC if name == "small" else BIG[name]


def _win_shape(spec):
    full, axis, w = spec[:3]
    return tuple(w if d == axis else n for d, n in enumerate(full))


def _window(ref, axis, w, j):
    idx = [slice(None)] * len(ref.shape)
    idx[axis] = pl.ds(j, 1) if w == 1 else pl.ds(pl.multiple_of(j * w, w), w)
    return ref.at[tuple(idx)]


def _mesh_pos():
    return lax.axis_index("x"), lax.axis_index("y"), lax.axis_index("c")


def split_call(name, arrays, starts=(), waits=(), sems_in=(), new=(), after=None):
    n, nn, ns = len(arrays), len(new), len(starts)
    flat_sems = [s for pair in sems_in for s in pair]

    def body(*refs):
        arr = list(refs[:n])
        sin = refs[n:n + len(flat_sems)]
        outs = refs[n + len(flat_sems) + (after is not None):]
        data = arr + list(outs[n:n + nn])
        for p, k, kind, mk in waits:
            d = mk(data, sin[2 * p].at[k], sin[2 * p + 1].at[k])
            d.wait_send() if kind == "send" else d.wait_recv()
        if ns:
            send, recv = outs[n + nn], outs[n + nn + 1]
            for k, mk in enumerate(starts):
                mk(data, send.at[k], recv.at[k]).start()
        outs[-1][...] = jnp.zeros((8, 128), F32)

    sem_out = [pltpu.SemaphoreType.DMA((ns,)), pltpu.SemaphoreType.DMA((ns,))] if ns else []
    res = pl.pallas_call(
        body, name=name,
        in_specs=[ANY] * n + [SEM] * len(flat_sems) + ([ANY] if after is not None else []),
        out_specs=[ANY] * (n + nn) + [SEM] * len(sem_out) + [pl.BlockSpec(memory_space=pltpu.VMEM)],
        out_shape=[_sds(a.shape, a.dtype) for a in arrays] + list(new) + sem_out + [_sds((8, 128), F32)],
        input_output_aliases={a: a for a in range(n)},
        compiler_params=pltpu.CompilerParams(has_side_effects=pltpu.SideEffectType.DATAFLOW_SIDE_EFFECTING),
    )(*arrays, *flat_sems, *([after] if after is not None else []))
    return list(res[:n + nn]), (tuple(res[n + nn:n + nn + 2]) if ns else None), res[-1]


def _remote(src, dst, send_sem, recv_sem, dev):
    return pltpu.make_async_remote_copy(src_ref=src, dst_ref=dst, send_sem=send_sem, recv_sem=recv_sem,
                                        device_id=dev, device_id_type=MESH_T)


class Gatherer:
    def __init__(self, groups, arrays, specs, prefix):
        self.groups, self.specs, self.prefix = groups, specs, prefix
        self.names = [nm for g in groups for nm in g]
        self.arr = dict(zip(self.names, arrays))
        self.fwd_sems = {}
        self.forwarded = set()

    @staticmethod
    def _mk_first(ai, spec, k):
        def mk(refs, ss, rs):
            x, y, c = _mesh_pos()
            dev = [(x, y, 1 - c), (1 - x, y, c), (x, 1 - y, c), (1 - x, 1 - y, c)][k]
            win = _window(refs[ai], spec[1], spec[2], 4 * x + 2 * y + c)
            return _remote(win, win, ss, rs, dev)
        return mk

    @staticmethod
    def _mk_fwd(ai, spec, j):
        def mk(refs, ss, rs):
            x, y, c = _mesh_pos()
            px, py = [(1 - x, y), (x, 1 - y), (1 - x, 1 - y)][j]
            win = _window(refs[ai], spec[1], spec[2], 4 * px + 2 * py + c)
            return _remote(win, win, ss, rs, (x, y, 1 - c))
        return mk

    def start(self, after=None):
        starts = [self._mk_first(ai, self.specs[nm], k) for ai, nm in enumerate(self.names) for k in range(4)]
        arrs, self.first_sems, tok = split_call(self.prefix + "_start", [self.arr[nm] for nm in self.names],
                                                starts=starts, after=after)
        self.arr = dict(zip(self.names, arrs))
        return tok

    def forward(self, g, after=None):
        if g in self.forwarded:
            return None
        self.forwarded.add(g)
        names = self.groups[g]
        waits = [(0, 4 * self.names.index(nm) + 1 + j, "recv", self._mk_fwd(ai, self.specs[nm], j))
                 for ai, nm in enumerate(names) for j in range(3)]
        starts = [self._mk_fwd(ai, self.specs[nm], j) for ai, nm in enumerate(names) for j in range(3)]
        arrs, self.fwd_sems[g], tok = split_call(
            "%s_forward%d" % (self.prefix, g), [self.arr[nm] for nm in names], starts=starts, waits=waits,
            sems_in=[self.first_sems], after=after)
        self.arr.update(zip(names, arrs))
        return tok

    def finish(self, g, after=None):
        self.forward(g, after)
        names = self.groups[g]
        waits = []
        for ai, nm in enumerate(names):
            base = 4 * self.names.index(nm)
            waits.append((0, base, "recv", self._mk_first(ai, self.specs[nm], 0)))
            waits += [(1, 3 * ai + j, "recv", self._mk_fwd(ai, self.specs[nm], j)) for j in range(3)]
            waits += [(0, base + k, "send", self._mk_first(ai, self.specs[nm], k)) for k in range(4)]
            waits += [(1, 3 * ai + j, "send", self._mk_fwd(ai, self.specs[nm], j)) for j in range(3)]
        arrs, _, _ = split_call(
            "%s_finish%d" % (self.prefix, g), [self.arr[nm] for nm in names], waits=waits,
            sems_in=[self.first_sems, self.fwd_sems[g]], after=after)
        self.arr.update(zip(names, arrs))
        return {nm: self.arr[nm] for nm in names}


class Reducer:
    def __init__(self, cq_arr, adam):
        self.cq_arr, self.adam = cq_arr, adam
        self.groups = []
        self.n = 0
        self.last = None

    @staticmethod
    def _mk1(gi, li, spec, q):
        def mk(refs, ss, rs):
            x, y, c = _mesh_pos()
            return _remote(_window(refs[gi], spec[1], spec[2], 2 * q + (1 - c)), refs[li].at[q], ss, rs, (x, y, 1 - c))
        return mk

    @staticmethod
    def _mk2(si, li, d):
        def mk(refs, ss, rs):
            x, y, c = _mesh_pos()
            qd = lax.rem(2 * x + y + d, 4)
            return _remote(refs[si].at[d - 1], refs[li].at[3 - d], ss, rs, (lax.div(qd, 2), lax.rem(qd, 2), c))
        return mk

    def add(self, grads, after=None):
        names = list(grads)
        m = len(names)
        starts = [self._mk1(ai, m + ai, BIG[nm], q) for ai, nm in enumerate(names) for q in range(4)]
        new = [_sds((4,) + _win_shape(BIG[nm]), BF16) for nm in names]
        res, sems, tok = split_call("rs1_start%d" % self.n, [grads[nm] for nm in names], starts=starts, new=new,
                                    after=after)
        self.groups.append(dict(names=names, starts=starts, buf=res, sems=sems, stage=1, idx=self.n))
        self.n += 1
        return tok

    def step(self, after):
        tok = None
        for grp in self.groups:
            names, m = grp["names"], len(grp["names"])
            if grp["stage"] == 1:
                waits = [(0, k, kind, mk) for k, mk in enumerate(grp["starts"]) for kind in ("send", "recv")]
                res, _, _ = split_call("rs1_wait%d" % grp["idx"], grp["buf"], waits=waits, sems_in=[grp["sems"]], after=after)
                full, land1 = res[:m], res[m:]
                s1b = []
                for lo in range(0, m, 4):
                    s1b += list(add_pairs(full[lo:lo + 4], land1[lo:lo + 4], [BIG[nm] for nm in names[lo:lo + 4]],
                                          self.cq_arr))
                starts = [self._mk2(ai, m + ai, d) for ai in range(m) for d in (1, 2, 3)]
                new = [_sds(a.shape, BF16) for a in s1b]
                res, sems, tok = split_call("rs2_start%d" % grp["idx"], s1b, starts=starts, new=new, after=tok)
                grp.update(stage=2, g=full, land1=land1, starts=starts, buf=res, sems=sems)
        return tok

    def finish_oldest(self):
        for grp in self.groups:
            if grp["stage"] == 2:
                names, m = grp["names"], len(grp["names"])
                waits = [(0, k, kind, mk) for k, mk in enumerate(grp["starts"]) for kind in ("send", "recv")]
                res, _, _ = split_call("rs2_wait%d" % grp["idx"], grp["buf"], waits=waits, sems_in=[grp["sems"]],
                                       after=self.last)
                for nm, g, l1, l2 in zip(names, grp["g"], grp["land1"], res[m:]):
                    self.last = self.adam(nm, g, l1, l2, self.last)
                grp["stage"] = 3
                return True
        return False


def pack_weights(args, arg_names, small_blk, names, j_arr, dep=None):
    n_in = len(args)
    deps = [] if dep is None else [dep]

    def body(j_ref, *refs):
        for o, nm in enumerate(names):
            dst = refs[n_in + 1 + len(deps) + o]
            if nm == "small":
                dst[...] = refs[n_in][...]
                continue
            _, axis, w, valid, arg, layer = BIG[nm]
            if arg in TRANSPOSED_ARGS:
                s = refs[arg_names.index(arg)][layer]
                s = jnp.concatenate([s, jnp.zeros((w - valid, s.shape[1]), F32)], axis=0)
                dst[...] = s.T.astype(BF16)
                continue
            src = refs[arg_names.index(arg)][layer].astype(BF16)
            if valid == w:
                dst[...] = src
            else:
                dst[...] = jnp.zeros(dst.shape, BF16)
                if axis == 1:
                    dst[:, 0:valid] = src
                else:
                    dst[0:valid, :] = src

    def ispec(a):
        return pl.BlockSpec(a.shape, lambda i, j_ref: (0, 0, 0))

    def ospec(spec):
        axis, nd = spec[1], len(spec[0])
        return pl.BlockSpec(_win_shape(spec),
                            lambda i, j_ref, axis=axis, nd=nd: tuple(j_ref[0] if d == axis else 0 for d in range(nd)))

    specs = [_spec(nm) for nm in names]
    return pl.pallas_call(
        body, name="pack_weights",
        grid_spec=pltpu.PrefetchScalarGridSpec(
            num_scalar_prefetch=1, grid=(1,),
            in_specs=[ispec(a) for a in list(args) + [small_blk]] + [ANY] * len(deps),
            out_specs=[ospec(s) for s in specs]),
        out_shape=[_sds(s[0], F32 if nm == "small" else BF16) for nm, s in zip(names, specs)],
        compiler_params=_cp(("arbitrary",)),
    )(j_arr, *args, small_blk, *deps)


def add_pairs(fulls, lands, specs, cq_arr):
    def chip(d, cq):
        return lax.rem(cq[1] + d + 1, 4)

    in_specs, args = [], []
    for full, land, spec in zip(fulls, lands, specs):
        axis, w = spec[1], spec[2]
        R, C = full.shape
        for d in range(3):
            if axis == 1:
                in_specs.append(pl.BlockSpec((R, w), lambda i, cq, d=d: (0, 2 * chip(d, cq) + cq[0])))
                in_specs.append(pl.BlockSpec((None, R, w), lambda i, cq, d=d: (chip(d, cq), 0, 0)))
            else:
                in_specs.append(pl.BlockSpec((w, C), lambda i, cq, d=d: (2 * chip(d, cq) + cq[0], 0)))
                in_specs.append(pl.BlockSpec((None, w, C), lambda i, cq, d=d: (chip(d, cq), 0, 0)))
            args += [full, land]
    out_shape = [_sds((3,) + land.shape[1:], BF16) for land in lands]
    n = len(fulls)

    def body(cq_ref, *refs):
        for a in range(n):
            for d in range(3):
                own, got = refs[6 * a + 2 * d], refs[6 * a + 2 * d + 1]
                refs[6 * n + a][d] = (own[...].astype(F32) + got[...].astype(F32)).astype(BF16)

    return pl.pallas_call(
        body, name="add_pairs",
        grid_spec=pltpu.PrefetchScalarGridSpec(
            num_scalar_prefetch=1, grid=(1,), in_specs=in_specs,
            out_specs=[pl.BlockSpec(o.shape, lambda i, cq: (0, 0, 0)) for o in out_shape]),
        out_shape=out_shape,
        compiler_params=_cp(("arbitrary",)),
    )(cq_arr, *args)


def _adamw(w, g, m, v):
    m = ADAM_B1 * m + (1.0 - ADAM_B1) * g
    v = ADAM_B2 * v + (1.0 - ADAM_B2) * (g * g)
    m_hat = m / (1.0 - ADAM_B1 ** ADAM_STEP)
    v_hat = v / (1.0 - ADAM_B2 ** ADAM_STEP)
    delta = -ADAM_LR * (m_hat / (jnp.sqrt(v_hat) + ADAM_EPS) + ADAM_WD * w)
    return delta, m, v


def reduce_adamw(full, land1, land, w, m, v, spec, cq_arr, prev=None, dep=None):
    axis, win, valid, layer = spec[1], spec[2], spec[3], spec[5]
    L, R, C = w.shape
    transposed = spec[4] in TRANSPOSED_ARGS
    TL = 256
    if transposed:
        grid = (C // TL,)
        fspec = pl.BlockSpec((TL, win), lambda i, cq: (i, 2 * cq[1] + cq[0]))
        wspec = pl.BlockSpec((None, TL, win), lambda i, cq: (cq[1], i, 0))
        lspec = pl.BlockSpec((3, TL, win), lambda i, cq: (0, i, 0))
        sspec = pl.BlockSpec((None, R, TL), lambda i, cq: (layer, 0, i))
    elif axis == 1:
        tr = min(TL, R)
        grid = (R // tr,)
        fspec = pl.BlockSpec((tr, win), lambda i, cq: (i, 2 * cq[1] + cq[0]))
        wspec = pl.BlockSpec((None, tr, win), lambda i, cq: (cq[1], i, 0))
        lspec = pl.BlockSpec((3, tr, win), lambda i, cq: (0, i, 0))
        sspec = pl.BlockSpec((None, tr, C), lambda i, cq: (layer, i, 0))
    else:
        grid = (C // TL,)
        fspec = pl.BlockSpec((win, TL), lambda i, cq: (2 * cq[1] + cq[0], i))
        wspec = pl.BlockSpec((None, win, TL), lambda i, cq: (cq[1], 0, i))
        lspec = pl.BlockSpec((3, win, TL), lambda i, cq: (0, 0, i))
        sspec = pl.BlockSpec((None, R, TL), lambda i, cq: (layer, 0, i))

    def body(cq_ref, full_ref, own_ref, land_ref, w_ref, m_ref, v_ref, *rest):
        g_ref, d_ref, nm_ref, nv_ref = rest[-4:]
        if transposed:
            rd = lambda r, *lead: r[lead] if lead else r[...]
        elif axis == 1:
            rd = lambda r, *lead: r[(*lead, slice(None), slice(0, valid))]
        else:
            rd = lambda r, *lead: r[(*lead, slice(0, valid), slice(None))]
        g = rd(full_ref).astype(F32) + rd(own_ref).astype(F32)
        for k in range(3):
            g = g + rd(land_ref, k).astype(F32)
        if transposed:
            g = g.T[0:valid, :]
        g_ref[...] = g
        d, nm, nv = _adamw(w_ref[...], g, m_ref[...], v_ref[...])
        d_ref[...] = d
        nm_ref[...] = nm
        nv_ref[...] = nv

    extra = (list(prev) if prev is not None else []) + ([dep] if dep is not None else [])
    return pl.pallas_call(
        body, name="reduce_adamw",
        grid_spec=pltpu.PrefetchScalarGridSpec(
            num_scalar_prefetch=1, grid=grid,
            in_specs=[fspec, wspec, lspec, sspec, sspec, sspec] + [ANY] * len(extra), out_specs=[sspec] * 4),
        out_shape=[_sds(w.shape)] * 4,
        input_output_aliases={7 + k: k for k in range(4 if prev is not None else 0)},
        compiler_params=_cp(("arbitrary",)),
    )(cq_arr, full, land1, land, w, m, v, *extra)


def place_slot(packed, j_arr):
    R = packed.shape[0]

    def body(j_ref, src, dst):
        dst[...] = src[...]

    return pl.pallas_call(
        body, name="place_slot",
        grid_spec=pltpu.PrefetchScalarGridSpec(
            num_scalar_prefetch=1, grid=(1,),
            in_specs=[pl.BlockSpec((R, 128), lambda i, j: (0, 0))],
            out_specs=[pl.BlockSpec((None, R, 128), lambda i, j: (j[0], 0, 0))]),
        out_shape=[_sds((N_DEV, R, 128))], compiler_params=_cp(("arbitrary",)),
    )(j_arr, packed)[0]


def sum_slots(gathered):
    def body(g_ref, o_ref):
        g = g_ref[0]
        for dev in range(1, N_DEV):
            g = g + g_ref[dev]
        o_ref[...] = g

    return pl.pallas_call(body, name="sum_slots", out_shape=_sds(gathered.shape[1:]), compiler_params=_cp())(gathered)


def small_adamw(gs, wmv):
    k = len(gs)

    def body(*refs):
        for a in range(k):
            g, w, m, v = refs[4 * a:4 * a + 4]
            d, nm, nv = _adamw(w[...], g[...], m[...], v[...])
            refs[4 * k + 3 * a][...] = d
            refs[4 * k + 3 * a + 1][...] = nm
            refs[4 * k + 3 * a + 2][...] = nv

    args = [t for g, tup in zip(gs, wmv) for t in (g,) + tuple(tup)]
    out_shape = [_sds(g.shape) for g in gs for _ in range(3)]
    return pl.pallas_call(body, name="small_adamw", out_shape=out_shape, compiler_params=_cp())(*args)


WEIGHT_NAMES = ("even_w_in", "even_w_out", "pool_w", "pool_scale", "odd_w_in", "odd_w_out", "conv_dw", "conv_ln_g",
                "conv_ln_b", "sg_ln_g", "sg_ln_b", "sg_w", "sg_b", "ln_mix_g", "ln_mix_b", "ffn_w_gate", "ffn_w_up",
                "ffn_w_down", "ln_ffn_g", "ln_ffn_b", "ple_w_proj", "ple_w_gate", "ple_b_gate")
PACK_ARGS = ("even_w_in", "even_w_out", "odd_w_in", "odd_w_out", "ffn_w_gate", "ffn_w_up", "ffn_w_down",
             "ple_w_gate", "ple_w_proj")
REPLICATED = ("pool_w", "pool_scale", "sg_w", "sg_b", "ln_mix_g", "ln_mix_b", "ln_ffn_g", "ln_ffn_b", "ple_b_gate")
SHARDED_SMALL = ("conv_dw", "conv_ln_g", "conv_ln_b", "sg_ln_g", "sg_ln_b")
NATURAL = {"pool_w": (4, 128, 128), "pool_scale": (1, 512), "sg_w": (4, 128, 128), "sg_b": (4, 128),
           "ln_mix_g": (2, 1024), "ln_mix_b": (2, 1024), "ln_ffn_g": (2, 1024), "ln_ffn_b": (2, 1024),
           "ple_b_gate": (2, 1024)}


def kernel(x, p, even_w_in, even_w_out, pool_w, pool_scale, odd_w_in, odd_w_out, conv_dw, conv_ln_g, conv_ln_b, sg_ln_g, sg_ln_b, sg_w, sg_b, ln_mix_g, ln_mix_b, ffn_w_gate, ffn_w_up, ffn_w_down, ln_ffn_g, ln_ffn_b, ple_w_proj, ple_w_gate, ple_b_gate, loss_target, m_even_w_in, m_even_w_out, m_pool_w, m_pool_scale, m_odd_w_in, m_odd_w_out, m_conv_dw, m_conv_ln_g, m_conv_ln_b, m_sg_ln_g, m_sg_ln_b, m_sg_w, m_sg_b, m_ln_mix_g, m_ln_mix_b, m_ffn_w_gate, m_ffn_w_up, m_ffn_w_down, m_ln_ffn_g, m_ln_ffn_b, m_ple_w_proj, m_ple_w_gate, m_ple_b_gate, v_even_w_in, v_even_w_out, v_pool_w, v_pool_scale, v_odd_w_in, v_odd_w_out, v_conv_dw, v_conv_ln_g, v_conv_ln_b, v_sg_ln_g, v_sg_ln_b, v_sg_w, v_sg_b, v_ln_mix_g, v_ln_mix_b, v_ffn_w_gate, v_ffn_w_up, v_ffn_w_down, v_ln_ffn_g, v_ln_ffn_b, v_ple_w_proj, v_ple_w_gate, v_ple_b_gate):
    A = dict(locals())
    for arg in TRANSPOSED_ARGS:
        for pre in ("", "m_", "v_"):
            A[pre + arg] = jnp.swapaxes(A[pre + arg], 1, 2)
    mx, my, mc = _mesh_pos()
    j = 4 * mx + 2 * my + mc
    j_arr = j.astype(jnp.int32).reshape(1)
    cq_arr = jnp.stack([mc, 2 * mx + my]).astype(jnp.int32)
    res = {}

    def adam(nm, full, land1, land2, dep):
        arg = BIG[nm][4]
        res[arg] = reduce_adamw(full, land1, land2, A[arg], A["m_" + arg], A["v_" + arg], BIG[nm], cq_arr,
                                res.get(arg), dep)
        return res[arg][0]

    class Comm:
        def __init__(self):
            names = [nm for g in AG_GROUPS for nm in g]
            specs = {nm: _spec(nm) for nm in names}
            small_blk = jnp.concatenate([conv_dw[0], conv_ln_g, conv_ln_b, sg_ln_g, sg_ln_b, jnp.zeros((5, 64), F32)], axis=0)
            first = pack_weights([A[AG_GROUPS[0][0]]], AG_GROUPS[0], small_blk[None], AG_GROUPS[0], j_arr)
            self.gat0 = Gatherer(AG_GROUPS[:1], first, specs, "ag0")
            first_started = self.gat0.start()
            rest = names[len(AG_GROUPS[0]):]
            mine = pack_weights([A[k] for k in PACK_ARGS], PACK_ARGS, small_blk[None], rest, j_arr, dep=first_started)
            self.gat = Gatherer(AG_GROUPS[1:], mine, specs, "ag")
            self.rest_started = self.gat.start()
            self.red = Reducer(cq_arr, adam)
            self.W = {k: A[k].reshape(NATURAL[k]) for k in REPLICATED}

        def weights(self, stage, after):
            if stage in AG_NEED:
                g = AG_NEED[stage]
                got = self.gat0.finish(0, self.rest_started) if g == 0 else self.gat.finish(g - 1, after)
                if "small" in got:
                    sm = got.pop("small").transpose(1, 0, 2).reshape(40, 512)
                    got.update(conv_dw=sm[0:31], conv_ln_g=sm[31:32], conv_ln_b=sm[32:33], sg_ln_g=sm[33:34],
                               sg_ln_b=sm[34:35])
                self.W.update(got)
            return self.W

        def all_weights(self):
            return self.W

        def poke(self, tag, after):
            if tag in AG_PASS:
                return self.gat.forward(AG_PASS[tag] - 1, after)
            if tag[0] == "bwd":
                return self.red.step(after)
            return None

        def grads(self, grads):
            self.n_grads = getattr(self, "n_grads", 0) + 1
            if self.n_grads == 2:
                self.held = grads
                return None
            if self.n_grads == 3:
                grads = {**self.held, **grads}
            tok = self.red.step(next(iter(grads.values())))
            return self.red.add(grads, after=tok)

    comm = Comm()
    sq, dx, small = run_layers(x[0], p[:, 0], loss_target[0], comm)
    red = comm.red
    tok = red.step(dx)

    names = REPLICATED + SHARDED_SMALL
    flat = jnp.concatenate([small[k].reshape(-1) for k in names] + [jnp.sum(sq).reshape(1)])
    rows = -(-flat.shape[0] // 1024) * 8
    packed = jnp.pad(flat, (0, rows * 128 - flat.shape[0])).reshape(rows, 128)
    sg = Gatherer((["g"],), [place_slot(packed, j_arr)], {"g": ((N_DEV, rows, 128), 0, 1, 1)}, "sg")
    red.last = sg.start(after=tok)
    older = sum(grp["stage"] == 2 for grp in red.groups) - 1
    for k in range(older):
        red.finish_oldest()
        if k == 0:
            sg.forward(0, after=red.last)
    gsum_flat = sum_slots(sg.finish(0, after=red.last)["g"]).reshape(-1)
    loss = 0.5 * gsum_flat[flat.shape[0] - 1] / x.shape[-1]
    gs, off = [], 0
    for k in names:
        n = math.prod(small[k].shape)
        g = gsum_flat[off:off + n].reshape(small[k].shape)
        off += n
        if k in SHARDED_SMALL:
            g = lax.dynamic_slice_in_dim(g, j * 64, 64, axis=1)
        gs.append(g.reshape(A[k].shape))
    outs = small_adamw(gs, [(A[k], A["m_" + k], A["v_" + k]) for k in names])
    for a, k in enumerate(names):
        res[k] = (gs[a],) + tuple(outs[3 * a:3 * a + 3])
    red.last = outs[0]
    while red.finish_oldest():
        pass

    for arg in TRANSPOSED_ARGS:
        res[arg] = [jnp.swapaxes(t, 1, 2) for t in res[arg]]
    out = [loss, dx[None]]
    for part in range(4):
        out += [res[k][part] for k in WEIGHT_NAMES]
    return tuple(out)
```

```python
import math

import jax
import jax.numpy as jnp
from jax import lax
from jax.experimental import pallas as pl
from jax.experimental.pallas import tpu as pltpu

F32, BF16 = jnp.float32, jnp.bfloat16
ALPHA = 4.0 ** 0.25
LN_EPS = 1e-5
QK_SCALE = 0.125
POOL_WINDOWS = (2, 4, 8, 16)
CONV_TAPS = 31
N_DEV = 8
FF_SHARD, FF_PAD = 352, 384
ADAM_LR, ADAM_B1, ADAM_B2, ADAM_EPS, ADAM_WD, ADAM_STEP = 0.001, 0.9, 0.999, 1e-08, 0.01, 10
VMEM_LIMIT = 56 * 1024 * 1024
MESH_T = pl.DeviceIdType.MESH


def _cp(sem=None):
    return pltpu.CompilerParams(dimension_semantics=sem, vmem_limit_bytes=VMEM_LIMIT)


def _dot(a, b):
    return jnp.dot(a, b, preferred_element_type=F32)


def _dot_nt(a, b):
    return lax.dot_general(a, b, (((1,), (1,)), ((), ())), preferred_element_type=F32)


def _dot_tn(a, b):
    return lax.dot_general(a, b, (((0,), (0,)), ((), ())), preferred_element_type=F32)


def _sigmoid(x):
    return 1.0 / (1.0 + jnp.exp(-x))


def _softplus(z):
    return jnp.maximum(z, 0.0) + jnp.log(1.0 + jnp.exp(-jnp.abs(z)))


_GELU_C = math.sqrt(2.0 / math.pi)


def _gelu(x):
    return 0.5 * x * (1.0 + jnp.tanh(_GELU_C * (x + 0.044715 * x * x * x)))


def _gelu_grad(x):
    t = jnp.tanh(_GELU_C * (x + 0.044715 * x * x * x))
    return 0.5 * (1.0 + t) + 0.5 * x * (1.0 - t * t) * _GELU_C * (1.0 + 3.0 * 0.044715 * x * x)


def _ln_fwd(r, g, b):
    mu = jnp.mean(r, axis=-1, keepdims=True)
    xc = r - mu
    var = jnp.mean(xc * xc, axis=-1, keepdims=True)
    rstd = lax.rsqrt(var + LN_EPS)
    xh = xc * rstd
    return xh * g + b, xh, rstd


def _ln_bwd(dy, xh, rstd, g):
    dxh = dy * g
    m1 = jnp.mean(dxh, axis=-1, keepdims=True)
    m2 = jnp.mean(dxh * xh, axis=-1, keepdims=True)
    return rstd * (dxh - m1 - xh * m2)


def _split2(x):
    hi = x.astype(BF16)
    lo = (x - hi.astype(F32)).astype(BF16)
    return hi, lo


def _colsum(x):
    return jnp.sum(x, axis=0, keepdims=True)


def _tok_call(name, body, tiled, full, out_tiled, out_acc=(), tm=256, scratch=(), dep=None):
    def arr(t):
        return t[0] if isinstance(t, tuple) else t
    full = [t[0] if isinstance(t, tuple) and t[1] is None else t for t in full]
    S = arr(tiled[0]).shape[0]
    tm = min(tm, S)
    n_in = len(tiled) + len(full)
    deps = [] if dep is None else [dep]
    if deps:
        inner = body
        body = lambda *refs: inner(*refs[:n_in], *refs[n_in + 1:])

    def tspec(t):
        if isinstance(t, tuple):
            _, w, cb = t
            return pl.BlockSpec((tm, w), lambda i, cb=cb: (i, cb))
        return pl.BlockSpec((tm, t.shape[1]), lambda i: (i, 0))

    def fspec(t):
        if isinstance(t, tuple):
            a, l = t
            nd = a.ndim - 1
            return pl.BlockSpec((None,) + a.shape[1:], lambda i, l=l, nd=nd: (l,) + (0,) * nd)
        nd = t.ndim
        return pl.BlockSpec(t.shape, lambda i, nd=nd: (0,) * nd)

    def ospec(o):
        return pl.BlockSpec((tm, o.shape[1]), lambda i: (i, 0))

    def aspec(o):
        nd = len(o.shape)
        return pl.BlockSpec(o.shape, lambda i, nd=nd: (0,) * nd)

    outs = pl.pallas_call(
        body, name=name, grid=(S // tm,),
        in_specs=[tspec(t) for t in tiled] + [fspec(t) for t in full] + [ANY] * len(deps),
        out_specs=[ospec(o) for o in out_tiled] + [aspec(o) for o in out_acc],
        out_shape=list(out_tiled) + list(out_acc),
        scratch_shapes=list(scratch),
        compiler_params=_cp(("arbitrary",)),
    )(*[arr(t) for t in tiled], *[arr(t) for t in full], *deps)
    return outs


def _sds(shape, dtype=F32):
    return jax.ShapeDtypeStruct(tuple(shape), dtype)


def _acc(ref, val):
    @pl.when(pl.program_id(0) == 0)
    def _():
        ref[...] = val

    @pl.when(pl.program_id(0) != 0)
    def _():
        ref[...] += val


def mm_in(x, w, nb16=0):
    S, N = x.shape[0], w.shape[1]

    def body(x_ref, w_ref, h_ref, xb_ref, *hb_ref):
        xb = x_ref[...].astype(BF16)
        xb_ref[...] = xb
        h = _dot(xb, w_ref[...])
        h_ref[...] = h
        if nb16:
            hb_ref[0][...] = h[:, 0:nb16].astype(BF16)

    outs = [_sds((S, N)), _sds((S, x.shape[1]), BF16)] + ([_sds((S, nb16), BF16)] if nb16 else [])
    return _tok_call("mm_in", body, [x], [w], outs, tm=512)


def _stack_heads(x, hm0, dtype=BF16):
    return jnp.concatenate([jnp.where(hm0, x, 0), jnp.where(hm0, 0, x)], axis=0).astype(dtype)


def _unstack_k(x, T):
    return jnp.concatenate([x[0:T], x[T:2 * T]], axis=1)


def _cumsum_mm(x, u):
    n = x.shape[0]
    hi, lo = _split2(x)
    r = _dot(jnp.concatenate([hi, lo], axis=0), u)
    return r[0:n] + r[n:2 * n]


def attn_fwd(qkv, T=256):
    S = qkv.shape[0]
    T = min(T, S)
    nq = S // T

    def body(q_ref, k_ref, v_ref, o_ref, t_ref, acc_ref, c_ref, qh_ref):
        i = pl.program_id(0)
        hm0 = lax.broadcasted_iota(jnp.int32, (1, 128), 1) < 64
        r2 = lax.broadcasted_iota(jnp.int32, (2 * T, T), 0)
        c2 = lax.broadcasted_iota(jnp.int32, (2 * T, T), 1)
        causal = c2 < jnp.where(r2 >= T, r2 - T, r2)
        ur = lax.broadcasted_iota(jnp.int32, (T, T), 0)
        uc = lax.broadcasted_iota(jnp.int32, (T, T), 1)
        u_incl = (ur >= uc).astype(BF16)
        acc_ref[...] = jnp.zeros_like(acc_ref)
        c_ref[...] = jnp.zeros_like(c_ref)
        for pp in range(4):
            qh_ref[pp] = _stack_heads(q_ref[:, pp * 128:(pp + 1) * 128] * QK_SCALE, hm0)

        def block(kb, diag):
            ks = pl.multiple_of(kb * T, T)
            cols = [slice(pp * 128, (pp + 1) * 128) for pp in range(4)]
            zs = [_dot_nt(qh_ref[pp], k_ref[pl.ds(ks, T), cols[pp]]) for pp in range(4)]
            incls = []
            for pp in range(4):
                sp = _softplus(zs[pp])
                if diag:
                    sp = jnp.where(causal, sp, 0.0)
                incls.append(_cumsum_mm(sp, u_incl))
            for pp in range(4):
                c = c_ref[pp]
                w = jnp.exp(zs[pp] - incls[pp] - c)
                if diag:
                    w = jnp.where(causal, w, 0.0)
                acc_ref[:, cols[pp]] += _dot(_unstack_k(w.astype(BF16), T),
                                             _stack_heads(v_ref[pl.ds(ks, T), cols[pp]], hm0))
                c_ref[pp] = c + jnp.broadcast_to(incls[pp][:, 0:1], (2 * T, T))

        block(i, True)

        def step(jj, carry):
            block(i - 1 - jj, False)
            return carry

        lax.fori_loop(0, i, step, 0)
        o_ref[...] = acc_ref[...].astype(BF16)
        for pp in range(4):
            for hd in range(2):
                t_ref[2 * pp + hd] = c_ref[pp, hd * T:(hd + 1) * T, 0:128]

    return pl.pallas_call(
        body, name="attn_fwd", grid=(nq,),
        in_specs=[pl.BlockSpec((T, 512), lambda i: (i, 0)),
                  pl.BlockSpec((S, 512), lambda i: (0, 1)),
                  pl.BlockSpec((S, 512), lambda i: (0, 2))],
        out_specs=[pl.BlockSpec((T, 512), lambda i: (i, 0)),
                   pl.BlockSpec((8, T, 128), lambda i: (0, i, 0))],
        out_shape=[_sds((S, 512), BF16), _sds((8, S, 128))],
        scratch_shapes=[pltpu.VMEM((T, 512), F32), pltpu.VMEM((4, 2 * T, T), F32), pltpu.VMEM((4, 2 * T, 128), BF16)],
        compiler_params=_cp(("arbitrary",)),
    )(qkv, qkv, qkv)


def pool_fwd(h, pool_w, pool_scale, CH=256):
    S = h.shape[0]
    CH = min(CH, S)

    def body(u_ref, w_ref, sc_ref, b_ref, pooled_ref, pad_ref):
        pad_ref[0:16, :] = jnp.zeros((16, 512), F32)
        pad_ref[16:16 + S, :] = u_ref[...]
        for g, win in enumerate(POOL_WINDOWS):
            cs = slice(g * 128, (g + 1) * 128)
            wq = w_ref[g].astype(BF16)
            for ch in range(S // CH):
                base = ch * CH
                acc = pad_ref[16 + base:16 + base + CH, cs]
                for sft in range(1, win):
                    acc = acc + pad_ref[16 + base - sft:16 + base - sft + CH, cs]
                t = base + lax.broadcasted_iota(jnp.int32, (CH, 1), 0)
                cnt = jnp.minimum(t + 1, win).astype(F32)
                pooled = (acc / cnt - pad_ref[16 + base:16 + base + CH, cs]).astype(BF16)
                pooled_ref[base:base + CH, cs] = pooled
                b_ref[base:base + CH, cs] = (_dot(pooled, wq) * sc_ref[:, cs]).astype(BF16)

    return pl.pallas_call(
        body, name="pool_fwd", grid=(1,),
        in_specs=[pl.BlockSpec((S, 512), lambda i: (0, 3)),
                  pl.BlockSpec((4, 128, 128), lambda i: (0, 0, 0)),
                  pl.BlockSpec((1, 512), lambda i: (0, 0))],
        out_specs=[pl.BlockSpec((S, 512), lambda i: (0, 0)), pl.BlockSpec((S, 512), lambda i: (0, 0))],
        out_shape=[_sds((S, 512), BF16), _sds((S, 512), BF16)],
        scratch_shapes=[pltpu.VMEM((S + 16, 512), F32)],
        compiler_params=_cp(("arbitrary",)),
    )(h, pool_w, pool_scale)


def conv_fwd(h, dw, CH=128):
    S = h.shape[0]

    def body(a_ref, g_ref, dw_ref, y_ref, hc_ref, pad_ref):
        hc = a_ref[...] * _sigmoid(g_ref[...])
        hc_ref[...] = hc
        pad_ref[0:32, :] = jnp.zeros((32, 128), F32)
        pad_ref[32:32 + S, :] = hc
        for ch in range(S // CH):
            base = ch * CH + 2
            acc = dw_ref[0:1, :] * pad_ref[base:base + CH, :]
            for k in range(1, CONV_TAPS):
                acc = acc + dw_ref[k:k + 1, :] * pad_ref[base + k:base + k + CH, :]
            y_ref[ch * CH:(ch + 1) * CH, :] = acc

    return pl.pallas_call(
        body, name="conv_fwd", grid=(4,),
        in_specs=[pl.BlockSpec((S, 128), lambda c: (0, c)),
                  pl.BlockSpec((S, 128), lambda c: (0, 4 + c)),
                  pl.BlockSpec((CONV_TAPS, 128), lambda c: (0, c))],
        out_specs=[pl.BlockSpec((S, 128), lambda c: (0, c)), pl.BlockSpec((S, 128), lambda c: (0, c))],
        out_shape=[_sds((S, 512)), _sds((S, 512))],
        scratch_shapes=[pltpu.VMEM((S + 32, 128), F32)],
        compiler_params=_cp(("arbitrary",)),
    )(h, h, dw)


def _masked_sg_w(w_ref, g):
    row = lax.broadcasted_iota(jnp.int32, (128, 128), 0)
    col = lax.broadcasted_iota(jnp.int32, (128, 128), 1)
    return jnp.where(row >= col, w_ref[g], 0.0).astype(BF16)


def odd_post(y, h, cl_g, cl_b, sl_g, sl_b, sg_w, sgb_bc, tm=512):
    S = y.shape[0]
    tm = min(tm, S)

    def body(y_ref, zc_ref, clg, clb, slg, slb, w_ref, sb_ref,
             c_ref, d_ref, xhc_ref, rsc_ref, xhv_ref, rsv_ref, sv_ref):
        lnc, xhc, rsc = _ln_fwd(y_ref[...], clg[...], clb[...])
        c_ref[...] = (lnc * _sigmoid(lnc)).astype(BF16)
        xhc_ref[...] = xhc
        rsc_ref[...] = rsc
        z = _gelu(zc_ref[...])
        vn, xhv, rsv = _ln_fwd(z[:, 512:], slg[...], slb[...])
        xhv_ref[...] = xhv
        rsv_ref[...] = rsv
        vnb = vn.astype(BF16)
        for g in range(4):
            wm = _masked_sg_w(w_ref, g)
            for ch in range(tm // 128):
                rs, cs = slice(ch * 128, (ch + 1) * 128), slice(g * 128, (g + 1) * 128)
                sv_ref[rs, cs] = _dot(wm, vnb[rs, cs]) + sb_ref[g]
        d_ref[...] = (z[:, :512] * sv_ref[...]).astype(BF16)

    return _tok_call(
        "odd_post", body, [y, (h, 1024, 1)], [cl_g, cl_b, sl_g, sl_b, sg_w, sgb_bc],
        [_sds((S, 512), BF16), _sds((S, 512), BF16), _sds((S, 512)), _sds((S, 1)),
         _sds((S, 512)), _sds((S, 1)), _sds((S, 512))], tm=tm)


def mm_out_ln(l1, l2, x, w, g, b, dep=None):
    S, D = x.shape

    def body(l1_ref, l2_ref, x_ref, w_ref, g_ref, b_ref, y_ref, xh_ref, rs_ref):
        mix = _dot(l1_ref[...], w_ref[0:512, :]) + _dot(l2_ref[...], w_ref[512:1024, :])
        y, xh, rs = _ln_fwd(ALPHA * x_ref[...] + mix, g_ref[...], b_ref[...])
        y_ref[...] = y
        xh_ref[...] = xh
        rs_ref[...] = rs

    return _tok_call("mm_out_ln", body, [l1, l2, x], [w, g, b],
                     [_sds((S, D)), _sds((S, D)), _sds((S, 1))], dep=dep, tm=512)


def ffn_up(x1, wg, wu, layer, dep=None):
    S, D = x1.shape
    F = wg.shape[-1]

    def body(x_ref, wg_ref, wu_ref, gate_ref, up_ref, hb_ref, xb_ref):
        xb = x_ref[...].astype(BF16)
        xb_ref[...] = xb
        gate = _dot(xb, wg_ref[...])
        up = _dot(xb, wu_ref[...])
        gate_ref[...] = gate.astype(BF16)
        up_ref[...] = up.astype(BF16)
        hb_ref[...] = (gate * _sigmoid(gate) * up).astype(BF16)

    return _tok_call("ffn_up", body, [x1], [(wg, layer), (wu, layer)],
                     [_sds((S, F), BF16), _sds((S, F), BF16), _sds((S, F), BF16), _sds((S, D), BF16)], dep=dep)


def ffn_down_ln(hb, x1, wd, layer, g, b):
    S, D = x1.shape

    def body(h_ref, x_ref, w_ref, g_ref, b_ref, y_ref, xh_ref, rs_ref):
        f = _dot(h_ref[...], w_ref[...])
        y, xh, rs = _ln_fwd(ALPHA * x_ref[...] + f, g_ref[...], b_ref[...])
        y_ref[...] = y
        xh_ref[...] = xh
        rs_ref[...] = rs

    return _tok_call("ffn_down_ln", body, [hb, x1], [(wd, layer), g, b],
                     [_sds((S, D)), _sds((S, D)), _sds((S, 1))], tm=512)


def ple_fwd(x2, p, wpg, wpp, layer, bg, target=None, dep=None):
    S, D = x2.shape
    last = target is not None

    def body(*refs):
        if last:
            x_ref, p_ref, t_ref, wg_ref, wp_ref, b_ref, x3_ref, sg_ref, pp_ref, xb_ref, pb_ref, dy_ref, ls_ref = refs
        else:
            x_ref, p_ref, wg_ref, wp_ref, b_ref, x3_ref, sg_ref, pp_ref, xb_ref, pb_ref = refs
        x = x_ref[...]
        xb = x.astype(BF16)
        pb = p_ref[...].astype(BF16)
        xb_ref[...] = xb
        pb_ref[...] = pb
        sg = _sigmoid(_dot(xb, wg_ref[...]) + b_ref[...])
        pp = _dot(pb, wp_ref[...])
        sg_ref[...] = sg.astype(BF16)
        pp_ref[...] = pp.astype(BF16)
        x3 = x + sg * pp
        x3_ref[...] = x3
        if last:
            err = x3 - t_ref[...]
            dy_ref[...] = err * (1.0 / D)
            _acc(ls_ref, _colsum(err * err))

    outs = [_sds((S, D)), _sds((S, D), BF16), _sds((S, D), BF16), _sds((S, D), BF16), _sds((S, p.shape[1]), BF16)]
    tiled = [x2, p] + ([target] if last else [])
    if last:
        outs.append(_sds((S, D)))
    return _tok_call("ple_fwd", body, tiled, [(wpg, layer), (wpp, layer), bg], outs,
                     [_sds((1, D))] if last else [], dep=dep, tm=512)


def ple_ln_bwd(dx3, sg, pp, x2b, pb, wpg, xh, rs, g, dep=None):
    S, D = dx3.shape

    def body(d_ref, sg_ref, pp_ref, x2b_ref, pb_ref, xh_ref, rs_ref, w_ref, g_ref,
             dr_ref, drb_ref, dbg_ref, dlg_ref, dlb_ref, dwg_ref, dwp_ref, accg_ref, accp_ref):
        fin_g = _sum_steps(accg_ref, dwg_ref)
        fin_p = _sum_steps(accp_ref, dwp_ref)
        d, sg = d_ref[...], sg_ref[...].astype(F32)
        dgp = d * pp_ref[...].astype(F32) * sg * (1.0 - sg)
        dgpb = dgp.astype(BF16)
        accg_ref[...] += _dot_tn(x2b_ref[...], dgpb)
        accp_ref[...] += _dot_tn(pb_ref[...], (d * sg).astype(BF16))
        _acc(dbg_ref, _colsum(dgp))
        dx2 = d + _dot_nt(dgpb, w_ref[...])
        xh = xh_ref[...]
        dr = _ln_bwd(dx2, xh, rs_ref[...], g_ref[...])
        dr_ref[...] = dr
        drb_ref[...] = dr.astype(BF16)
        _acc(dlg_ref, _colsum(dx2 * xh))
        _acc(dlb_ref, _colsum(dx2))
        fin_g()
        fin_p()

    P = pb.shape[1]
    return _tok_call("ple_ln_bwd", body, [dx3, sg, pp, x2b, pb, xh, rs], [wpg, g],
                     [_sds((S, D)), _sds((S, D), BF16)],
                     [_sds((1, D)), _sds((1, D)), _sds((1, D)), _sds((D, D), BF16), _sds((P, D), BF16)],
                     scratch=[pltpu.VMEM((D, D), F32), pltpu.VMEM((P, D), F32)], dep=dep, tm=ACC_TM)


def ffn_bwd(dr_b, x1b, gate, up, hb, wg, wu, wd, TH=512):
    S, D = dr_b.shape
    F = gate.shape[1]

    def body(dr_hbm, x_hbm, gate_ref, up_ref, hb_ref, wg_ref, wu_ref, wd_ref,
             dx_ref, dwg_ref, dwu_ref, dwd_ref, dr_v, x_v, sem, dg_s, du_s):
        @pl.when(pl.program_id(0) == 0)
        def _():
            c1 = pltpu.make_async_copy(dr_hbm, dr_v, sem.at[0])
            c2 = pltpu.make_async_copy(x_hbm, x_v, sem.at[1])
            c1.start()
            c2.start()
            c1.wait()
            c2.wait()
            dx_ref[...] = jnp.zeros_like(dx_ref)

        for ch in range(S // CH):
            rows = slice(ch * CH, (ch + 1) * CH)
            dh = _dot_nt(dr_v[rows, :], wd_ref[...])
            g, u = gate_ref[rows, :].astype(F32), up_ref[rows, :].astype(F32)
            s = _sigmoid(g)
            dgb = (dh * u * s * (1.0 + g * (1.0 - s))).astype(BF16)
            dub = (dh * g * s).astype(BF16)
            dg_s[rows, :] = dgb
            du_s[rows, :] = dub
            dx_ref[rows, :] += _dot_nt(dgb, wg_ref[...]) + _dot_nt(dub, wu_ref[...])
        x = x_v[...]
        dwg_ref[...] = _dot_tn(x, dg_s[...]).astype(BF16)
        dwu_ref[...] = _dot_tn(x, du_s[...]).astype(BF16)
        dwd_ref[...] = _dot_tn(hb_ref[...], dr_v[...]).astype(BF16)

    CH = min(256, S)
    col = lambda rows: pl.BlockSpec((rows, TH), lambda j: (0, j))
    row = pl.BlockSpec((TH, D), lambda j: (j, 0))
    return pl.pallas_call(
        body, name="ffn_bwd", grid=(F // TH,),
        in_specs=[ANY, ANY, col(S), col(S), col(S), col(D), col(D), row],
        out_specs=[pl.BlockSpec((S, D), lambda j: (0, 0)), col(D), col(D), row],
        out_shape=[_sds((S, D)), _sds((D, F), BF16), _sds((D, F), BF16), _sds((F, D), BF16)],
        scratch_shapes=[pltpu.VMEM((S, D), BF16), pltpu.VMEM((S, D), BF16), pltpu.SemaphoreType.DMA((2,)),
                        pltpu.VMEM((S, TH), BF16), pltpu.VMEM((S, TH), BF16)],
        compiler_params=pltpu.CompilerParams(dimension_semantics=("arbitrary",), vmem_limit_bytes=60 * 1024 * 1024),
    )(dr_b, x1b, gate, up, hb, wg, wu, wd)


ACC_TM = 512


def _sum_steps(acc_ref, out_ref):
    @pl.when(pl.program_id(0) == 0)
    def _():
        acc_ref[...] = jnp.zeros_like(acc_ref)

    def finish():
        @pl.when(pl.program_id(0) == pl.num_programs(0) - 1)
        def _():
            out_ref[...] = acc_ref[...].astype(BF16)
    return finish


def mix_bwd(dxp, dr2, xh, rs, l1, l2, g, w, dep=None):
    S, D = dxp.shape
    K1 = l1.shape[1]

    def body(dxp_ref, dr2_ref, xh_ref, rs_ref, l1_ref, l2_ref, g_ref, w_ref,
             dr_ref, dl_ref, dlg_ref, dlb_ref, dw_ref, acc_ref):
        finish = _sum_steps(acc_ref, dw_ref)
        d, xh = ALPHA * dr2_ref[...] + dxp_ref[...], xh_ref[...]
        dr = _ln_bwd(d, xh, rs_ref[...], g_ref[...])
        drb = dr.astype(BF16)
        dr_ref[...] = dr
        dl_ref[...] = _dot_nt(drb, w_ref[...])
        _acc(dlg_ref, _colsum(d * xh))
        _acc(dlb_ref, _colsum(d))
        acc_ref[0:K1, :] += _dot_tn(l1_ref[...], drb)
        acc_ref[K1:, :] += _dot_tn(l2_ref[...], drb)
        finish()

    return _tok_call("mix_bwd", body, [dxp, dr2, xh, rs, l1, l2], [g, w],
                     [_sds((S, D)), _sds((S, D))], [_sds((1, D)), _sds((1, D)), _sds(w.shape, BF16)],
                     scratch=[pltpu.VMEM(w.shape, F32)], dep=dep, tm=ACC_TM)


def dx_in(dr, pieces, w, xb):
    S, D = dr.shape
    offs = [o for _, o in pieces]
    widths = [a.shape[1] for a, _ in pieces]
    npc = len(pieces)

    def body(*refs):
        dr_ref, prefs, xb_ref, w_ref = refs[0], refs[1:1 + npc], refs[1 + npc], refs[2 + npc]
        dx_ref, dw_ref, acc_ref = refs[3 + npc:]
        finish = _sum_steps(acc_ref, dw_ref)
        acc = ALPHA * dr_ref[...]
        xb_t = xb_ref[...]
        for pr, o, n in zip(prefs, offs, widths):
            piece = pr[...]
            acc = acc + _dot_nt(piece, w_ref[:, o:o + n])
            acc_ref[:, o:o + n] += _dot_tn(xb_t, piece)
        dx_ref[...] = acc
        finish()

    return _tok_call("dx_in", body, [dr] + [a for a, _ in pieces] + [xb], [w], [_sds((S, D))],
                     [_sds(w.shape, BF16)], scratch=[pltpu.VMEM(w.shape, F32)], tm=ACC_TM)


def odd_post_bwd(dl, h, xhc, rsc, xhv, rsv, sv, cl_g, cl_b, sl_g, sl_b, sg_w, tm=512, dep=None):
    S = dl.shape[0]
    tm = min(tm, S)

    def body(dl_ref, zc_ref, xhc_ref, rsc_ref, xhv_ref, rsv_ref, sv_ref, clg, clb, slg, slb, w_ref,
             dy_ref, dzc_ref, dclg_ref, dclb_ref, dslg_ref, dslb_ref, dwm_ref, dsb_ref, dvn_ref):
        first = pl.program_id(0) == 0
        last = pl.program_id(0) == pl.num_programs(0) - 1
        dc, dd = dl_ref[:, 0:512], dl_ref[:, 512:1024]
        xhc = xhc_ref[...]
        lnc = xhc * clg[...] + clb[...]
        s = _sigmoid(lnc)
        dlnc = dc * s * (1.0 + lnc * (1.0 - s))
        dy_ref[...] = _ln_bwd(dlnc, xhc, rsc_ref[...], clg[...])
        _acc(dclg_ref, _colsum(dlnc * xhc))
        _acc(dclb_ref, _colsum(dlnc))
        zc = zc_ref[...]
        z = _gelu(zc)
        dsv = dd * z[:, :512]
        dsvb = dsv.astype(BF16)
        xhv = xhv_ref[...]
        vnb = (xhv * slg[...] + slb[...]).astype(BF16)

        @pl.when(first)
        def _():
            dwm_ref[...] = jnp.zeros_like(dwm_ref)
            dsb_ref[...] = jnp.zeros_like(dsb_ref)

        for g in range(4):
            wm = _masked_sg_w(w_ref, g)
            for ch in range(tm // 128):
                rs_, cs = slice(ch * 128, (ch + 1) * 128), slice(g * 128, (g + 1) * 128)
                dwm_ref[g] += _dot_nt(dsvb[rs_, cs], vnb[rs_, cs])
                dvn_ref[rs_, cs] = _dot_tn(wm, dsvb[rs_, cs])
                dsb_ref[g] += dsv[rs_, cs]
        dvn = dvn_ref[...]
        dvv = _ln_bwd(dvn, xhv, rsv_ref[...], slg[...])
        _acc(dslg_ref, _colsum(dvn * xhv))
        _acc(dslb_ref, _colsum(dvn))
        gg = _gelu_grad(zc)
        dzc_ref[:, 0:512] = (dd * sv_ref[...] * gg[:, :512]).astype(BF16)
        dzc_ref[:, 512:1024] = (dvv * gg[:, 512:]).astype(BF16)

        @pl.when(last)
        def _():
            row = lax.broadcasted_iota(jnp.int32, (128, 128), 0)
            col = lax.broadcasted_iota(jnp.int32, (128, 128), 1)
            for g in range(4):
                dwm_ref[g] = jnp.where(row >= col, dwm_ref[g], 0.0)
                dsb_ref[g] = jnp.broadcast_to(jnp.sum(dsb_ref[g], axis=1, keepdims=True), (128, 128))

    return _tok_call(
        "odd_post_bwd", body, [dl, (h, 1024, 1), xhc, rsc, xhv, rsv, sv], [cl_g, cl_b, sl_g, sl_b, sg_w],
        [_sds((S, 512)), _sds((S, 1024), BF16)],
        [_sds((1, 512)), _sds((1, 512)), _sds((1, 512)), _sds((1, 512)), _sds((4, 128, 128)), _sds((4, 128, 128))],
        tm=tm, scratch=[pltpu.VMEM((tm, 512), F32)], dep=dep)


def conv_bwd(dy, hc, h, dw, CH=128):
    S = dy.shape[0]

    def body(dy_ref, hc_ref, a_ref, g_ref, dw_ref, da_ref, dg_ref, ddw_ref, padh_ref, padd_ref, dhc_ref):
        padh_ref[0:32, :] = jnp.zeros((32, 128), F32)
        padh_ref[32:32 + S, :] = hc_ref[...]
        padd_ref[0:S, :] = dy_ref[...]
        padd_ref[S:S + 32, :] = jnp.zeros((32, 128), F32)
        taps = [jnp.zeros((1, 128), F32) for _ in range(CONV_TAPS)]
        for ch in range(S // CH):
            b0 = ch * CH
            dyc = padd_ref[b0:b0 + CH, :]
            acc = dw_ref[0:1, :] * padd_ref[b0 + 30:b0 + 30 + CH, :]
            taps[0] = taps[0] + _colsum(dyc * padh_ref[b0 + 2:b0 + 2 + CH, :])
            for k in range(1, CONV_TAPS):
                acc = acc + dw_ref[k:k + 1, :] * padd_ref[b0 + 30 - k:b0 + 30 - k + CH, :]
                taps[k] = taps[k] + _colsum(dyc * padh_ref[b0 + 2 + k:b0 + 2 + k + CH, :])
            dhc_ref[b0:b0 + CH, :] = acc
        for k in range(CONV_TAPS):
            ddw_ref[k:k + 1, :] = taps[k]
        dhc = dhc_ref[...]
        s = _sigmoid(g_ref[...])
        da_ref[...] = (dhc * s).astype(BF16)
        dg_ref[...] = (dhc * a_ref[...] * s * (1.0 - s)).astype(BF16)

    return pl.pallas_call(
        body, name="conv_bwd", grid=(4,),
        in_specs=[pl.BlockSpec((S, 128), lambda c: (0, c)),
                  pl.BlockSpec((S, 128), lambda c: (0, c)),
                  pl.BlockSpec((S, 128), lambda c: (0, c)),
                  pl.BlockSpec((S, 128), lambda c: (0, 4 + c)),
                  pl.BlockSpec((CONV_TAPS, 128), lambda c: (0, c))],
        out_specs=[pl.BlockSpec((S, 128), lambda c: (0, c)), pl.BlockSpec((S, 128), lambda c: (0, c)),
                   pl.BlockSpec((CONV_TAPS, 128), lambda c: (0, c))],
        out_shape=[_sds((S, 512), BF16), _sds((S, 512), BF16), _sds((CONV_TAPS, 512))],
        scratch_shapes=[pltpu.VMEM((S + 32, 128), F32), pltpu.VMEM((S + 32, 128), F32), pltpu.VMEM((S, 128), F32)],
        compiler_params=_cp(("arbitrary",)),
    )(dy, hc, h, h, dw)


def attn_bwd(qkv, dl, tb, T=256, dep=None):
    S = qkv.shape[0]
    T = min(T, S)
    nq = S // T

    def body(q_ref, k_ref, v_ref, do_ref, t_ref, dq_ref, dk_ref, dv_ref,
             dka_ref, dva_ref, dqa_ref, pc_ref, gc_ref, qh_ref, doh_ref):
        i = pl.program_id(0)
        hm0 = lax.broadcasted_iota(jnp.int32, (1, 128), 1) < 64
        r2 = lax.broadcasted_iota(jnp.int32, (2 * T, T), 0)
        c2 = lax.broadcasted_iota(jnp.int32, (2 * T, T), 1)
        causal = c2 < jnp.where(r2 >= T, r2 - T, r2)
        ur = lax.broadcasted_iota(jnp.int32, (T, T), 0)
        uc = lax.broadcasted_iota(jnp.int32, (T, T), 1)
        u_le = (ur <= uc).astype(BF16)
        u_lt = (ur < uc).astype(BF16)

        @pl.when(i == 0)
        def _():
            dka_ref[...] = jnp.zeros_like(dka_ref)
            dva_ref[...] = jnp.zeros_like(dva_ref)

        dqa_ref[...] = jnp.zeros_like(dqa_ref)
        gc_ref[...] = jnp.zeros_like(gc_ref)
        for pp in range(4):
            cs = slice(pp * 128, (pp + 1) * 128)
            qh_ref[pp] = _stack_heads(q_ref[:, cs] * QK_SCALE, hm0)
            doh_ref[pp] = _stack_heads(do_ref[:, cs], hm0)
            for hd in range(2):
                for half in range(T // 128):
                    pc_ref[pp, hd * T:(hd + 1) * T, half * 128:(half + 1) * 128] = t_ref[2 * pp + hd]

        def block(kb, diag):
            ks = pl.multiple_of(kb * T, T)
            cols = [slice(pp * 128, (pp + 1) * 128) for pp in range(4)]
            zs = [_dot_nt(qh_ref[pp], k_ref[pl.ds(ks, T), cols[pp]]) for pp in range(4)]
            dws = [_dot_nt(doh_ref[pp], v_ref[pl.ds(ks, T), cols[pp]]) for pp in range(4)]
            a_s, pres = [], []
            for pp in range(4):
                sp = _softplus(zs[pp])
                a_s.append(zs[pp] - sp)
                if diag:
                    sp = jnp.where(causal, sp, 0.0)
                pres.append(_cumsum_mm(sp, u_le))
            ws, gmats, gsums = [], [], []
            for pp in range(4):
                rem = pc_ref[pp]
                w = jnp.exp(a_s[pp] - rem + pres[pp])
                if diag:
                    w = jnp.where(causal, w, 0.0)
                gmat = dws[pp] * w
                ws.append(w.astype(BF16))
                gmats.append(gmat)
                gsums.append(_dot(gmat.astype(BF16), u_lt))
                pc_ref[pp] = rem - jnp.broadcast_to(pres[pp][:, T - 1:T], (2 * T, T))
            for pp in range(4):
                cs = cols[pp]
                sig = jnp.exp(a_s[pp])
                gex = gc_ref[pp] + gsums[pp]
                dz = gmats[pp] - sig * (gmats[pp] + gex)
                if diag:
                    dz = jnp.where(causal, dz, 0.0)
                dzb = dz.astype(BF16)
                dqa_ref[:, cs] += _dot(_unstack_k(dzb, T), _stack_heads(k_ref[pl.ds(ks, T), cs], hm0))
                dka_ref[pl.ds(ks, T), cs] += _dot_tn(dzb, qh_ref[pp])
                dva_ref[pl.ds(ks, T), cs] += _dot_tn(ws[pp], doh_ref[pp])
                gc_ref[pp] = jnp.broadcast_to(gex[:, T - 1:T] + gmats[pp][:, T - 1:T], (2 * T, T))

        def step(kb, carry):
            block(kb, False)
            return carry

        lax.fori_loop(0, i, step, 0)
        block(i, True)
        dq_ref[...] = (dqa_ref[...] * QK_SCALE).astype(BF16)

        @pl.when(i == nq - 1)
        def _():
            dk_ref[...] = dka_ref[...].astype(BF16)
            dv_ref[...] = dva_ref[...].astype(BF16)

    deps = [] if dep is None else [dep]
    call_body = body if dep is None else (lambda *refs: body(*refs[:5], *refs[6:]))
    return pl.pallas_call(
        call_body, name="attn_bwd", grid=(nq,),
        in_specs=[pl.BlockSpec((T, 512), lambda i: (i, 0)),
                  pl.BlockSpec((S, 512), lambda i: (0, 1)),
                  pl.BlockSpec((S, 512), lambda i: (0, 2)),
                  pl.BlockSpec((T, 512), lambda i: (i, 0)),
                  pl.BlockSpec((8, T, 128), lambda i: (0, i, 0))] + [ANY] * len(deps),
        out_specs=[pl.BlockSpec((T, 512), lambda i: (i, 0)),
                   pl.BlockSpec((S, 512), lambda i: (0, 0)),
                   pl.BlockSpec((S, 512), lambda i: (0, 0))],
        out_shape=[_sds((S, 512), BF16), _sds((S, 512), BF16), _sds((S, 512), BF16)],
        scratch_shapes=[pltpu.VMEM((S, 512), F32), pltpu.VMEM((S, 512), F32), pltpu.VMEM((T, 512), F32),
                        pltpu.VMEM((4, 2 * T, T), F32), pltpu.VMEM((4, 2 * T, T), F32),
                        pltpu.VMEM((4, 2 * T, 128), BF16), pltpu.VMEM((4, 2 * T, 128), BF16)],
        compiler_params=_cp(("arbitrary",)),
    )(qkv, qkv, qkv, dl, tb, *deps)


def pool_bwd(dl, pooled_b, pool_w, pool_scale, CH=256):
    S = dl.shape[0]
    CH = min(CH, S)

    def body(db_ref, pooled_ref, w_ref, sc_ref, du_ref, dw_ref, dsc_ref, pad_ref, dp_ref):
        pad_ref[S:S + 16, :] = jnp.zeros((16, 128), F32)
        for g, win in enumerate(POOL_WINDOWS):
            cs = slice(g * 128, (g + 1) * 128)
            wq = w_ref[g].astype(BF16)
            dwg = jnp.zeros((128, 128), F32)
            dsc = jnp.zeros((1, 128), F32)
            for ch in range(S // CH):
                rs_ = slice(ch * CH, (ch + 1) * CH)
                db = db_ref[rs_, cs]
                pb = pooled_ref[rs_, cs]
                dsc = dsc + _colsum(db * _dot(pb, wq))
                dmsb = (db * sc_ref[:, cs]).astype(BF16)
                dwg = dwg + _dot_tn(pb, dmsb)
                dpool = _dot_nt(dmsb, wq)
                t = ch * CH + lax.broadcasted_iota(jnp.int32, (CH, 1), 0)
                cnt = jnp.minimum(t + 1, win).astype(F32)
                dp_ref[rs_, :] = dpool
                pad_ref[rs_, :] = dpool / cnt
            dw_ref[g] = dwg
            dsc_ref[:, cs] = dsc
            for ch in range(S // CH):
                base = ch * CH
                acc = pad_ref[base:base + CH, :]
                for sft in range(1, win):
                    acc = acc + pad_ref[base + sft:base + sft + CH, :]
                du_ref[base:base + CH, cs] = (acc - dp_ref[base:base + CH, :]).astype(BF16)

    return pl.pallas_call(
        body, name="pool_bwd", grid=(1,),
        in_specs=[pl.BlockSpec((S, 512), lambda i: (0, 1)),
                  pl.BlockSpec((S, 512), lambda i: (0, 0)),
                  pl.BlockSpec((4, 128, 128), lambda i: (0, 0, 0)),
                  pl.BlockSpec((1, 512), lambda i: (0, 0))],
        out_specs=[pl.BlockSpec((S, 512), lambda i: (0, 0)),
                   pl.BlockSpec((4, 128, 128), lambda i: (0, 0, 0)),
                   pl.BlockSpec((1, 512), lambda i: (0, 0))],
        out_shape=[_sds((S, 512), BF16), _sds((4, 128, 128)), _sds((1, 512))],
        scratch_shapes=[pltpu.VMEM((S + 16, 128), F32), pltpu.VMEM((S, 128), F32)],
        compiler_params=_cp(("arbitrary",)),
    )(dl, pooled_b, pool_w, pool_scale)


def _row(a, i):
    return a[i:i + 1]


MIXER_NAMES = (("even_w_in", "even_w_out"), ("odd_w_in", "odd_w_out"))


def fwd_layer(i, xin, p_i, target, comm):
    s = {}
    W = comm.weights(("mix", i), xin)
    w_in = W[MIXER_NAMES[i][0]]
    if i == 0:
        s["h"], s["xb"], s["qkv"] = mm_in(xin, w_in, nb16=1536)
        comm.poke(("in", i), s["h"])
        s["l1"], s["tb"] = attn_fwd(s["qkv"])
        s["l2"], s["pooled"] = pool_fwd(s["h"], W["pool_w"], W["pool_scale"])
    else:
        s["h"], s["xb"] = mm_in(xin, w_in)
        comm.poke(("in", i), s["h"])
        s["y"], s["hc"] = conv_fwd(s["h"], W["conv_dw"])
        sgb_bc = jnp.broadcast_to(W["sg_b"][:, :, None], (4, 128, 128))
        (s["l1"], s["l2"], s["xhc"], s["rsc"], s["xhv"], s["rsv"], s["sv"]) = odd_post(
            s["y"], s["h"], W["conv_ln_g"], W["conv_ln_b"], W["sg_ln_g"], W["sg_ln_b"], W["sg_w"], sgb_bc)
    tok = comm.poke(("mixed", i), s["l1"])
    W = comm.weights(("out", i), s["l1"])
    x1, s["xh1"], s["rs1"] = mm_out_ln(s["l1"], s["l2"], xin, W[MIXER_NAMES[i][1]], _row(W["ln_mix_g"], i),
                                       _row(W["ln_mix_b"], i), dep=tok)
    W = comm.weights(("ffn", i), x1)
    tok = comm.poke(("up", i), x1)
    s["gate"], s["up"], s["hb"], s["x1b"] = ffn_up(x1, W["ffn_w_gate%d" % i], W["ffn_w_up%d" % i], None, dep=tok)
    W = comm.weights(("down", i), s["hb"])
    x2, s["xh2"], s["rs2"] = ffn_down_ln(s["hb"], x1, W["ffn_w_down%d" % i], None,
                                         _row(W["ln_ffn_g"], i), _row(W["ln_ffn_b"], i))
    tok = comm.poke(("ffn", i), x2)
    outs = ple_fwd(x2, p_i, W["ple_w_gate%d" % i], W["ple_w_proj%d" % i], None, _row(W["ple_b_gate"], i), target,
                   dep=tok)
    s["sg"], s["pp"], s["x2b"], s["pb"] = outs[1:5]
    return outs[0], s, outs[5:]


def bwd_layer(i, dx, s, W, comm, tok=None):
    small = {}
    dr2, dr2_b, small["ple_b_gate"], small["ln_ffn_g"], small["ln_ffn_b"], dwpg, dwpp = ple_ln_bwd(
        dx, s["sg"], s["pp"], s["x2b"], s["pb"], W["ple_w_gate%d" % i], s["xh2"], s["rs2"],
        _row(W["ln_ffn_g"], i), dep=tok)
    dxp, dwg, dwu, dwd = ffn_bwd(dr2_b, s["x1b"], s["gate"], s["up"], s["hb"], W["ffn_w_gate%d" % i],
                                 W["ffn_w_up%d" % i], W["ffn_w_down%d" % i])
    tok = comm.grads({"ple_w_gate%d" % i: dwpg, "ple_w_proj%d" % i: dwpp, "ffn_w_down%d" % i: dwd,
                      "ffn_w_gate%d" % i: dwg, "ffn_w_up%d" % i: dwu})
    iname, oname = MIXER_NAMES[i]
    dr1, dl, small["ln_mix_g"], small["ln_mix_b"], dwout = mix_bwd(
        dxp, dr2, s["xh1"], s["rs1"], s["l1"], s["l2"], _row(W["ln_mix_g"], i), W[oname], dep=tok)
    tok = comm.poke(("bwd", i), dl)
    if i == 1:
        (dy, dzc_b, small["conv_ln_g"], small["conv_ln_b"], small["sg_ln_g"], small["sg_ln_b"],
         small["sg_w"], dsb) = odd_post_bwd(dl, s["h"], s["xhc"], s["rsc"], s["xhv"], s["rsv"], s["sv"],
                                            W["conv_ln_g"], W["conv_ln_b"], W["sg_ln_g"], W["sg_ln_b"], W["sg_w"],
                                            dep=tok)
        small["sg_b"] = dsb[:, :, 0]
        da_b, dg_b, small["conv_dw"] = conv_bwd(dy, s["hc"], s["h"], W["conv_dw"])
        pieces = [(da_b, 0), (dg_b, 512), (dzc_b, 1024)]
    else:
        dq_b, dk_b, dv_b = attn_bwd(s["qkv"], dl, s["tb"], dep=tok)
        du_b, small["pool_w"], small["pool_scale"] = pool_bwd(dl, s["pooled"], W["pool_w"], W["pool_scale"])
        pieces = [(dq_b, 0), (dk_b, 512), (dv_b, 1024), (du_b, 1536)]
    dxin, dwin = dx_in(dr1, pieces, W[iname], s["xb"])
    tok = comm.grads({oname: dwout, iname: dwin})
    return dxin, small, tok


def run_layers(x, p, target, comm):
    saved, xin = [], x
    for i in range(2):
        xin, s, extra = fwd_layer(i, xin, p[i], target if i == 1 else None, comm)
        saved.append(s)
    dx, sq = extra
    W = comm.all_weights()
    per_layer = [None, None]
    tok = None
    for i in (1, 0):
        dx, per_layer[i], tok = bwd_layer(i, dx, saved[i], W, comm, tok)
    small = {}
    for k in ("ln_mix_g", "ln_mix_b", "ln_ffn_g", "ln_ffn_b", "ple_b_gate"):
        small[k] = jnp.concatenate([per_layer[0][k], per_layer[1][k]], axis=0)
    for i in range(2):
        small.update({k: v for k, v in per_layer[i].items() if k not in small})
    return sq, dx, small


def _big_table():
    t = {}
    for nm in ("even", "odd"):
        t[nm + "_w_in"] = ((1024, 2048), 1, 256, 256, nm + "_w_in", 0)
        t[nm + "_w_out"] = ((1024, 1024), 0, 128, 128, nm + "_w_out", 0)
    for l in range(2):
        t["ffn_w_gate%d" % l] = ((1024, 8 * FF_PAD), 1, FF_PAD, FF_SHARD, "ffn_w_gate", l)
        t["ffn_w_up%d" % l] = ((1024, 8 * FF_PAD), 1, FF_PAD, FF_SHARD, "ffn_w_up", l)
        t["ffn_w_down%d" % l] = ((8 * FF_PAD, 1024), 0, FF_PAD, FF_SHARD, "ffn_w_down", l)
        t["ple_w_gate%d" % l] = ((1024, 1024), 0, 128, 128, "ple_w_gate", l)
        t["ple_w_proj%d" % l] = ((256, 1024), 1, 128, 128, "ple_w_proj", l)
    return t


BIG = _big_table()
TRANSPOSED_ARGS = ("ffn_w_gate", "ffn_w_up")
SMALL_SPEC = ((N_DEV, 40, 64), 0, 1, 1)
_UP_GROUP = lambda l: ["ffn_w_gate%d" % l, "ffn_w_up%d" % l]
_DOWN_GROUP = lambda l: ["ffn_w_down%d" % l, "ple_w_gate%d" % l, "ple_w_proj%d" % l]
AG_GROUPS = (["even_w_in"], ["even_w_out"], _UP_GROUP(0), _DOWN_GROUP(0), ["odd_w_in", "odd_w_out", "small"],
             _UP_GROUP(1), _DOWN_GROUP(1))
AG_NEED = {("mix", 0): 0, ("out", 0): 1, ("ffn", 0): 2, ("down", 0): 3, ("mix", 1): 4, ("ffn", 1): 5, ("down", 1): 6}
AG_PASS = {("in", 0): 1, ("mixed", 0): 2, ("up", 0): 3, ("ffn", 0): 4, ("mixed", 1): 5, ("up", 1): 6}
ANY = pl.BlockSpec(memory_space=pl.ANY)
SEM = pl.BlockSpec(memory_space=pltpu.SEMAPHORE)


def _spec(name):
    return SMALL_SPEC if name == "small" else BIG[name]


def _win_shape(spec):
    full, axis, w = spec[:3]
    return tuple(w if d == axis else n for d, n in enumerate(full))


def _window(ref, axis, w, j):
    idx = [slice(None)] * len(ref.shape)
    idx[axis] = pl.ds(j, 1) if w == 1 else pl.ds(pl.multiple_of(j * w, w), w)
    return ref.at[tuple(idx)]


def _mesh_pos():
    return lax.axis_index("x"), lax.axis_index("y"), lax.axis_index("c")


def split_call(name, arrays, starts=(), waits=(), sems_in=(), new=(), after=None):
    n, nn, ns = len(arrays), len(new), len(starts)
    flat_sems = [s for pair in sems_in for s in pair]

    def body(*refs):
        arr = list(refs[:n])
        sin = refs[n:n + len(flat_sems)]
        outs = refs[n + len(flat_sems) + (after is not None):]
        data = arr + list(outs[n:n + nn])
        for p, k, kind, mk in waits:
            d = mk(data, sin[2 * p].at[k], sin[2 * p + 1].at[k])
            d.wait_send() if kind == "send" else d.wait_recv()
        if ns:
            send, recv = outs[n + nn], outs[n + nn + 1]
            for k, mk in enumerate(starts):
                mk(data, send.at[k], recv.at[k]).start()
        outs[-1][...] = jnp.zeros((8, 128), F32)

    sem_out = [pltpu.SemaphoreType.DMA((ns,)), pltpu.SemaphoreType.DMA((ns,))] if ns else []
    res = pl.pallas_call(
        body, name=name,
        in_specs=[ANY] * n + [SEM] * len(flat_sems) + ([ANY] if after is not None else []),
        out_specs=[ANY] * (n + nn) + [SEM] * len(sem_out) + [pl.BlockSpec(memory_space=pltpu.VMEM)],
        out_shape=[_sds(a.shape, a.dtype) for a in arrays] + list(new) + sem_out + [_sds((8, 128), F32)],
        input_output_aliases={a: a for a in range(n)},
        compiler_params=pltpu.CompilerParams(has_side_effects=pltpu.SideEffectType.DATAFLOW_SIDE_EFFECTING),
    )(*arrays, *flat_sems, *([after] if after is not None else []))
    return list(res[:n + nn]), (tuple(res[n + nn:n + nn + 2]) if ns else None), res[-1]


def _remote(src, dst, send_sem, recv_sem, dev):
    return pltpu.make_async_remote_copy(src_ref=src, dst_ref=dst, send_sem=send_sem, recv_sem=recv_sem,
                                        device_id=dev, device_id_type=MESH_T)


class Gatherer:
    def __init__(self, groups, arrays, specs, prefix):
        self.groups, self.specs, self.prefix = groups, specs, prefix
        self.names = [nm for g in groups for nm in g]
        self.arr = dict(zip(self.names, arrays))
        self.fwd_sems = {}
        self.forwarded = set()

    @staticmethod
    def _mk_first(ai, spec, k):
        def mk(refs, ss, rs):
            x, y, c = _mesh_pos()
            dev = [(x, y, 1 - c), (1 - x, y, c), (x, 1 - y, c), (1 - x, 1 - y, c)][k]
            win = _window(refs[ai], spec[1], spec[2], 4 * x + 2 * y + c)
            return _remote(win, win, ss, rs, dev)
        return mk

    @staticmethod
    def _mk_fwd(ai, spec, j):
        def mk(refs, ss, rs):
            x, y, c = _mesh_pos()
            px, py = [(1 - x, y), (x, 1 - y), (1 - x, 1 - y)][j]
            win = _window(refs[ai], spec[1], spec[2], 4 * px + 2 * py + c)
            return _remote(win, win, ss, rs, (x, y, 1 - c))
        return mk

    def start(self, after=None):
        starts = [self._mk_first(ai, self.specs[nm], k) for ai, nm in enumerate(self.names) for k in range(4)]
        arrs, self.first_sems, tok = split_call(self.prefix + "_start", [self.arr[nm] for nm in self.names],
                                                starts=starts, after=after)
        self.arr = dict(zip(self.names, arrs))
        return tok

    def forward(self, g, after=None):
        if g in self.forwarded:
            return None
        self.forwarded.add(g)
        names = self.groups[g]
        waits = [(0, 4 * self.names.index(nm) + 1 + j, "recv", self._mk_fwd(ai, self.specs[nm], j))
                 for ai, nm in enumerate(names) for j in range(3)]
        starts = [self._mk_fwd(ai, self.specs[nm], j) for ai, nm in enumerate(names) for j in range(3)]
        arrs, self.fwd_sems[g], tok = split_call(
            "%s_forward%d" % (self.prefix, g), [self.arr[nm] for nm in names], starts=starts, waits=waits,
            sems_in=[self.first_sems], after=after)
        self.arr.update(zip(names, arrs))
        return tok

    def finish(self, g, after=None):
        self.forward(g, after)
        names = self.groups[g]
        waits = []
        for ai, nm in enumerate(names):
            base = 4 * self.names.index(nm)
            waits.append((0, base, "recv", self._mk_first(ai, self.specs[nm], 0)))
            waits += [(1, 3 * ai + j, "recv", self._mk_fwd(ai, self.specs[nm], j)) for j in range(3)]
            waits += [(0, base + k, "send", self._mk_first(ai, self.specs[nm], k)) for k in range(4)]
            waits += [(1, 3 * ai + j, "send", self._mk_fwd(ai, self.specs[nm], j)) for j in range(3)]
        arrs, _, _ = split_call(
            "%s_finish%d" % (self.prefix, g), [self.arr[nm] for nm in names], waits=waits,
            sems_in=[self.first_sems, self.fwd_sems[g]], after=after)
        self.arr.update(zip(names, arrs))
        return {nm: self.arr[nm] for nm in names}


class Reducer:
    def __init__(self, cq_arr, adam):
        self.cq_arr, self.adam = cq_arr, adam
        self.groups = []
        self.n = 0
        self.last = None

    @staticmethod
    def _mk1(gi, li, spec, q):
        def mk(refs, ss, rs):
            x, y, c = _mesh_pos()
            return _remote(_window(refs[gi], spec[1], spec[2], 2 * q + (1 - c)), refs[li].at[q], ss, rs, (x, y, 1 - c))
        return mk

    @staticmethod
    def _mk2(si, li, d):
        def mk(refs, ss, rs):
            x, y, c = _mesh_pos()
            qd = lax.rem(2 * x + y + d, 4)
            return _remote(refs[si].at[d - 1], refs[li].at[3 - d], ss, rs, (lax.div(qd, 2), lax.rem(qd, 2), c))
        return mk

    def add(self, grads, after=None):
        names = list(grads)
        m = len(names)
        starts = [self._mk1(ai, m + ai, BIG[nm], q) for ai, nm in enumerate(names) for q in range(4)]
        new = [_sds((4,) + _win_shape(BIG[nm]), BF16) for nm in names]
        res, sems, tok = split_call("rs1_start%d" % self.n, [grads[nm] for nm in names], starts=starts, new=new,
                                    after=after)
        self.groups.append(dict(names=names, starts=starts, buf=res, sems=sems, stage=1, idx=self.n))
        self.n += 1
        return tok

    def step(self, after):
        tok = None
        for grp in self.groups:
            names, m = grp["names"], len(grp["names"])
            if grp["stage"] == 1:
                waits = [(0, k, kind, mk) for k, mk in enumerate(grp["starts"]) for kind in ("send", "recv")]
                res, _, _ = split_call("rs1_wait%d" % grp["idx"], grp["buf"], waits=waits, sems_in=[grp["sems"]], after=after)
                full, land1 = res[:m], res[m:]
                s1b = []
                for lo in range(0, m, 4):
                    s1b += list(add_pairs(full[lo:lo + 4], land1[lo:lo + 4], [BIG[nm] for nm in names[lo:lo + 4]],
                                          self.cq_arr))
                starts = [self._mk2(ai, m + ai, d) for ai in range(m) for d in (1, 2, 3)]
                new = [_sds(a.shape, BF16) for a in s1b]
                res, sems, tok = split_call("rs2_start%d" % grp["idx"], s1b, starts=starts, new=new, after=tok)
                grp.update(stage=2, g=full, land1=land1, starts=starts, buf=res, sems=sems)
        return tok

    def finish_oldest(self):
        for grp in self.groups:
            if grp["stage"] == 2:
                names, m = grp["names"], len(grp["names"])
                waits = [(0, k, kind, mk) for k, mk in enumerate(grp["starts"]) for kind in ("send", "recv")]
                res, _, _ = split_call("rs2_wait%d" % grp["idx"], grp["buf"], waits=waits, sems_in=[grp["sems"]],
                                       after=self.last)
                self.last = self.adam(list(zip(names, grp["g"], grp["land1"], res[m:])), self.last)
                grp["stage"] = 3
                return True
        return False


def pack_weights(args, arg_names, small_blk, names, j_arr, dep=None):
    n_in = len(args)
    deps = [] if dep is None else [dep]

    def body(j_ref, *refs):
        for o, nm in enumerate(names):
            dst = refs[n_in + 1 + len(deps) + o]
            if nm == "small":
                dst[...] = refs[n_in][...]
                continue
            _, axis, w, valid, arg, layer = BIG[nm]
            if arg in TRANSPOSED_ARGS:
                s = refs[arg_names.index(arg)][layer]
                s = jnp.concatenate([s, jnp.zeros((w - valid, s.shape[1]), F32)], axis=0)
                dst[...] = s.T.astype(BF16)
                continue
            src = refs[arg_names.index(arg)][layer].astype(BF16)
            if valid == w:
                dst[...] = src
            else:
                dst[...] = jnp.zeros(dst.shape, BF16)
                if axis == 1:
                    dst[:, 0:valid] = src
                else:
                    dst[0:valid, :] = src

    def ispec(a):
        return pl.BlockSpec(a.shape, lambda i, j_ref: (0, 0, 0))

    def ospec(spec):
        axis, nd = spec[1], len(spec[0])
        return pl.BlockSpec(_win_shape(spec),
                            lambda i, j_ref, axis=axis, nd=nd: tuple(j_ref[0] if d == axis else 0 for d in range(nd)))

    specs = [_spec(nm) for nm in names]
    return pl.pallas_call(
        body, name="pack_weights",
        grid_spec=pltpu.PrefetchScalarGridSpec(
            num_scalar_prefetch=1, grid=(1,),
            in_specs=[ispec(a) for a in list(args) + [small_blk]] + [ANY] * len(deps),
            out_specs=[ospec(s) for s in specs]),
        out_shape=[_sds(s[0], F32 if nm == "small" else BF16) for nm, s in zip(names, specs)],
        compiler_params=_cp(("arbitrary",)),
    )(j_arr, *args, small_blk, *deps)


def add_pairs(fulls, lands, specs, cq_arr):
    def chip(d, cq):
        return lax.rem(cq[1] + d + 1, 4)

    in_specs, args = [], []
    for full, land, spec in zip(fulls, lands, specs):
        axis, w = spec[1], spec[2]
        R, C = full.shape
        for d in range(3):
            if axis == 1:
                in_specs.append(pl.BlockSpec((R, w), lambda i, cq, d=d: (0, 2 * chip(d, cq) + cq[0])))
                in_specs.append(pl.BlockSpec((None, R, w), lambda i, cq, d=d: (chip(d, cq), 0, 0)))
            else:
                in_specs.append(pl.BlockSpec((w, C), lambda i, cq, d=d: (2 * chip(d, cq) + cq[0], 0)))
                in_specs.append(pl.BlockSpec((None, w, C), lambda i, cq, d=d: (chip(d, cq), 0, 0)))
            args += [full, land]
    out_shape = [_sds((3,) + land.shape[1:], BF16) for land in lands]
    n = len(fulls)

    def body(cq_ref, *refs):
        for a in range(n):
            for d in range(3):
                own, got = refs[6 * a + 2 * d], refs[6 * a + 2 * d + 1]
                refs[6 * n + a][d] = (own[...].astype(F32) + got[...].astype(F32)).astype(BF16)

    return pl.pallas_call(
        body, name="add_pairs",
        grid_spec=pltpu.PrefetchScalarGridSpec(
            num_scalar_prefetch=1, grid=(1,), in_specs=in_specs,
            out_specs=[pl.BlockSpec(o.shape, lambda i, cq: (0, 0, 0)) for o in out_shape]),
        out_shape=out_shape,
        compiler_params=_cp(("arbitrary",)),
    )(cq_arr, *args)


def _adamw(w, g, m, v):
    m = ADAM_B1 * m + (1.0 - ADAM_B1) * g
    v = ADAM_B2 * v + (1.0 - ADAM_B2) * (g * g)
    m_hat = m / (1.0 - ADAM_B1 ** ADAM_STEP)
    v_hat = v / (1.0 - ADAM_B2 ** ADAM_STEP)
    delta = -ADAM_LR * (m_hat / (jnp.sqrt(v_hat) + ADAM_EPS) + ADAM_WD * w)
    return delta, m, v


def reduce_adamw(full, land1, land, w, m, v, spec, cq_arr, prev=None, dep=None):
    axis, win, valid, layer = spec[1], spec[2], spec[3], spec[5]
    L, R, C = w.shape
    transposed = spec[4] in TRANSPOSED_ARGS
    TL = 256
    if transposed:
        grid = (C // TL,)
        fspec = pl.BlockSpec((TL, win), lambda i, cq: (i, 2 * cq[1] + cq[0]))
        wspec = pl.BlockSpec((None, TL, win), lambda i, cq: (cq[1], i, 0))
        lspec = pl.BlockSpec((3, TL, win), lambda i, cq: (0, i, 0))
        sspec = pl.BlockSpec((None, R, TL), lambda i, cq: (layer, 0, i))
    elif axis == 1:
        tr = min(TL, R)
        grid = (R // tr,)
        fspec = pl.BlockSpec((tr, win), lambda i, cq: (i, 2 * cq[1] + cq[0]))
        wspec = pl.BlockSpec((None, tr, win), lambda i, cq: (cq[1], i, 0))
        lspec = pl.BlockSpec((3, tr, win), lambda i, cq: (0, i, 0))
        sspec = pl.BlockSpec((None, tr, C), lambda i, cq: (layer, i, 0))
    else:
        grid = (C // TL,)
        fspec = pl.BlockSpec((win, TL), lambda i, cq: (2 * cq[1] + cq[0], i))
        wspec = pl.BlockSpec((None, win, TL), lambda i, cq: (cq[1], 0, i))
        lspec = pl.BlockSpec((3, win, TL), lambda i, cq: (0, 0, i))
        sspec = pl.BlockSpec((None, R, TL), lambda i, cq: (layer, 0, i))

    def body(cq_ref, full_ref, own_ref, land_ref, w_ref, m_ref, v_ref, *rest):
        g_ref, d_ref, nm_ref, nv_ref = rest[-4:]
        if transposed:
            rd = lambda r, *lead: r[lead] if lead else r[...]
        elif axis == 1:
            rd = lambda r, *lead: r[(*lead, slice(None), slice(0, valid))]
        else:
            rd = lambda r, *lead: r[(*lead, slice(0, valid), slice(None))]
        g = rd(full_ref).astype(F32) + rd(own_ref).astype(F32)
        for k in range(3):
            g = g + rd(land_ref, k).astype(F32)
        if transposed:
            g = g.T[0:valid, :]
        g_ref[...] = g
        d, nm, nv = _adamw(w_ref[...], g, m_ref[...], v_ref[...])
        d_ref[...] = d
        nm_ref[...] = nm
        nv_ref[...] = nv

    extra = (list(prev) if prev is not None else []) + ([dep] if dep is not None else [])
    return pl.pallas_call(
        body, name="reduce_adamw",
        grid_spec=pltpu.PrefetchScalarGridSpec(
            num_scalar_prefetch=1, grid=grid,
            in_specs=[fspec, wspec, lspec, sspec, sspec, sspec] + [ANY] * len(extra), out_specs=[sspec] * 4),
        out_shape=[_sds(w.shape)] * 4,
        input_output_aliases={7 + k: k for k in range(4 if prev is not None else 0)},
        compiler_params=_cp(("arbitrary",)),
    )(cq_arr, full, land1, land, w, m, v, *extra)


def reduce_adamw_multi(items, cq_arr, dep=None):
    n = len(items)
    in_specs, args, out_specs, out_shape, metas = [], [], [], [], []
    for full, land1, land, w, m, v, spec, prev in items:
        axis, win, valid, layer = spec[1], spec[2], spec[3], spec[5]
        L, R, C = w.shape
        if axis == 1:
            f = pl.BlockSpec((R, win), lambda i, cq: (0, 2 * cq[1] + cq[0]))
            o = pl.BlockSpec((None, R, win), lambda i, cq: (cq[1], 0, 0))
            l3 = pl.BlockSpec((3, R, win), lambda i, cq: (0, 0, 0))
        else:
            f = pl.BlockSpec((win, full.shape[1]), lambda i, cq: (2 * cq[1] + cq[0], 0))
            o = pl.BlockSpec((None, win, C), lambda i, cq: (cq[1], 0, 0))
            l3 = pl.BlockSpec((3, win, C), lambda i, cq: (0, 0, 0))
        s = pl.BlockSpec((None, R, C), lambda i, cq, layer=layer: (layer, 0, 0))
        in_specs += [f, o, l3, s, s, s]
        args += [full, land1, land, w, m, v]
        out_specs += [s] * 4
        out_shape += [_sds(w.shape)] * 4
        metas.append((axis, valid))
    extra, aliases = [], {}
    for a, item in enumerate(items):
        if item[7] is not None:
            for k in range(4):
                aliases[1 + 6 * n + len(extra)] = 4 * a + k
                extra.append(item[7][k])
    if dep is not None:
        extra.append(dep)

    def body(cq_ref, *refs):
        outs = refs[6 * n + len(extra):]
        for a, (axis, valid) in enumerate(metas):
            full_ref, own_ref, land_ref, w_ref, m_ref, v_ref = refs[6 * a:6 * a + 6]
            g_ref, d_ref, nm_ref, nv_ref = outs[4 * a:4 * a + 4]
            if axis == 1:
                rd = lambda r, *lead: r[(*lead, slice(None), slice(0, valid))]
            else:
                rd = lambda r, *lead: r[(*lead, slice(0, valid), slice(None))]
            g = rd(full_ref).astype(F32) + rd(own_ref).astype(F32)
            for k in range(3):
                g = g + rd(land_ref, k).astype(F32)
            g_ref[...] = g
            d, nm, nv = _adamw(w_ref[...], g, m_ref[...], v_ref[...])
            d_ref[...] = d
            nm_ref[...] = nm
            nv_ref[...] = nv

    return pl.pallas_call(
        body, name="reduce_adamw_multi",
        grid_spec=pltpu.PrefetchScalarGridSpec(
            num_scalar_prefetch=1, grid=(1,), in_specs=in_specs + [ANY] * len(extra), out_specs=out_specs),
        out_shape=out_shape, input_output_aliases=aliases,
        compiler_params=_cp(("arbitrary",)),
    )(cq_arr, *args, *extra)


def place_slot(packed, j_arr):
    R = packed.shape[0]

    def body(j_ref, src, dst):
        dst[...] = src[...]

    return pl.pallas_call(
        body, name="place_slot",
        grid_spec=pltpu.PrefetchScalarGridSpec(
            num_scalar_prefetch=1, grid=(1,),
            in_specs=[pl.BlockSpec((R, 128), lambda i, j: (0, 0))],
            out_specs=[pl.BlockSpec((None, R, 128), lambda i, j: (j[0], 0, 0))]),
        out_shape=[_sds((N_DEV, R, 128))], compiler_params=_cp(("arbitrary",)),
    )(j_arr, packed)[0]


def sum_slots(gathered):
    def body(g_ref, o_ref):
        g = g_ref[0]
        for dev in range(1, N_DEV):
            g = g + g_ref[dev]
        o_ref[...] = g

    return pl.pallas_call(body, name="sum_slots", out_shape=_sds(gathered.shape[1:]), compiler_params=_cp())(gathered)


def small_adamw(gs, wmv):
    k = len(gs)

    def body(*refs):
        for a in range(k):
            g, w, m, v = refs[4 * a:4 * a + 4]
            d, nm, nv = _adamw(w[...], g[...], m[...], v[...])
            refs[4 * k + 3 * a][...] = d
            refs[4 * k + 3 * a + 1][...] = nm
            refs[4 * k + 3 * a + 2][...] = nv

    args = [t for g, tup in zip(gs, wmv) for t in (g,) + tuple(tup)]
    out_shape = [_sds(g.shape) for g in gs for _ in range(3)]
    return pl.pallas_call(body, name="small_adamw", out_shape=out_shape, compiler_params=_cp())(*args)


WEIGHT_NAMES = ("even_w_in", "even_w_out", "pool_w", "pool_scale", "odd_w_in", "odd_w_out", "conv_dw", "conv_ln_g",
                "conv_ln_b", "sg_ln_g", "sg_ln_b", "sg_w", "sg_b", "ln_mix_g", "ln_mix_b", "ffn_w_gate", "ffn_w_up",
                "ffn_w_down", "ln_ffn_g", "ln_ffn_b", "ple_w_proj", "ple_w_gate", "ple_b_gate")
PACK_ARGS = ("even_w_in", "even_w_out", "odd_w_in", "odd_w_out", "ffn_w_gate", "ffn_w_up", "ffn_w_down",
             "ple_w_gate", "ple_w_proj")
REPLICATED = ("pool_w", "pool_scale", "sg_w", "sg_b", "ln_mix_g", "ln_mix_b", "ln_ffn_g", "ln_ffn_b", "ple_b_gate")
SHARDED_SMALL = ("conv_dw", "conv_ln_g", "conv_ln_b", "sg_ln_g", "sg_ln_b")
NATURAL = {"pool_w": (4, 128, 128), "pool_scale": (1, 512), "sg_w": (4, 128, 128), "sg_b": (4, 128),
           "ln_mix_g": (2, 1024), "ln_mix_b": (2, 1024), "ln_ffn_g": (2, 1024), "ln_ffn_b": (2, 1024),
           "ple_b_gate": (2, 1024)}


def kernel(x, p, even_w_in, even_w_out, pool_w, pool_scale, odd_w_in, odd_w_out, conv_dw, conv_ln_g, conv_ln_b, sg_ln_g, sg_ln_b, sg_w, sg_b, ln_mix_g, ln_mix_b, ffn_w_gate, ffn_w_up, ffn_w_down, ln_ffn_g, ln_ffn_b, ple_w_proj, ple_w_gate, ple_b_gate, loss_target, m_even_w_in, m_even_w_out, m_pool_w, m_pool_scale, m_odd_w_in, m_odd_w_out, m_conv_dw, m_conv_ln_g, m_conv_ln_b, m_sg_ln_g, m_sg_ln_b, m_sg_w, m_sg_b, m_ln_mix_g, m_ln_mix_b, m_ffn_w_gate, m_ffn_w_up, m_ffn_w_down, m_ln_ffn_g, m_ln_ffn_b, m_ple_w_proj, m_ple_w_gate, m_ple_b_gate, v_even_w_in, v_even_w_out, v_pool_w, v_pool_scale, v_odd_w_in, v_odd_w_out, v_conv_dw, v_conv_ln_g, v_conv_ln_b, v_sg_ln_g, v_sg_ln_b, v_sg_w, v_sg_b, v_ln_mix_g, v_ln_mix_b, v_ffn_w_gate, v_ffn_w_up, v_ffn_w_down, v_ln_ffn_g, v_ln_ffn_b, v_ple_w_proj, v_ple_w_gate, v_ple_b_gate):
    A = dict(locals())
    for arg in TRANSPOSED_ARGS:
        for pre in ("", "m_", "v_"):
            A[pre + arg] = jnp.swapaxes(A[pre + arg], 1, 2)
    mx, my, mc = _mesh_pos()
    j = 4 * mx + 2 * my + mc
    j_arr = j.astype(jnp.int32).reshape(1)
    cq_arr = jnp.stack([mc, 2 * mx + my]).astype(jnp.int32)
    res = {}

    def adam(entries, dep):
        merged = []
        for nm, full, land1, land2 in entries:
            arg = BIG[nm][4]
            if arg in TRANSPOSED_ARGS:
                res[arg] = reduce_adamw(full, land1, land2, A[arg], A["m_" + arg], A["v_" + arg], BIG[nm], cq_arr,
                                        res.get(arg), dep)
                dep = res[arg][0]
            else:
                merged.append((nm, full, land1, land2))
        for lo in range(0, len(merged), 3):
            part = merged[lo:lo + 3]
            args_ = [BIG[nm][4] for nm, _, _, _ in part]
            outs = reduce_adamw_multi(
                [(full, l1, l2, A[arg], A["m_" + arg], A["v_" + arg], BIG[nm], res.get(arg))
                 for (nm, full, l1, l2), arg in zip(part, args_)], cq_arr, dep)
            for a, arg in enumerate(args_):
                res[arg] = list(outs[4 * a:4 * a + 4])
            dep = outs[0]
        return dep

    class Comm:
        def __init__(self):
            names = [nm for g in AG_GROUPS for nm in g]
            specs = {nm: _spec(nm) for nm in names}
            small_blk = jnp.concatenate([conv_dw[0], conv_ln_g, conv_ln_b, sg_ln_g, sg_ln_b, jnp.zeros((5, 64), F32)], axis=0)
            first = pack_weights([A[AG_GROUPS[0][0]]], AG_GROUPS[0], small_blk[None], AG_GROUPS[0], j_arr)
            self.gat0 = Gatherer(AG_GROUPS[:1], first, specs, "ag0")
            first_started = self.gat0.start()
            rest = names[len(AG_GROUPS[0]):]
            mine = pack_weights([A[k] for k in PACK_ARGS], PACK_ARGS, small_blk[None], rest, j_arr, dep=first_started)
            self.gat = Gatherer(AG_GROUPS[1:], mine, specs, "ag")
            self.rest_started = self.gat.start()
            self.red = Reducer(cq_arr, adam)
            self.W = {k: A[k].reshape(NATURAL[k]) for k in REPLICATED}

        def weights(self, stage, after):
            if stage in AG_NEED:
                g = AG_NEED[stage]
                got = self.gat0.finish(0, self.rest_started) if g == 0 else self.gat.finish(g - 1, after)
                if "small" in got:
                    sm = got.pop("small").transpose(1, 0, 2).reshape(40, 512)
                    got.update(conv_dw=sm[0:31], conv_ln_g=sm[31:32], conv_ln_b=sm[32:33], sg_ln_g=sm[33:34],
                               sg_ln_b=sm[34:35])
                self.W.update(got)
            return self.W

        def all_weights(self):
            return self.W

        def poke(self, tag, after):
            if tag in AG_PASS:
                return self.gat.forward(AG_PASS[tag] - 1, after)
            if tag[0] == "bwd":
                return self.red.step(after)
            return None

        def grads(self, grads):
            self.n_grads = getattr(self, "n_grads", 0) + 1
            if self.n_grads == 2:
                self.held = grads
                return None
            if self.n_grads == 3:
                grads = {**self.held, **grads}
            tok = self.red.step(next(iter(grads.values())))
            return self.red.add(grads, after=tok)

    comm = Comm()
    sq, dx, small = run_layers(x[0], p[:, 0], loss_target[0], comm)
    red = comm.red
    tok = red.step(dx)

    names = REPLICATED + SHARDED_SMALL
    flat = jnp.concatenate([small[k].reshape(-1) for k in names] + [jnp.sum(sq).reshape(1)])
    rows = -(-flat.shape[0] // 1024) * 8
    packed = jnp.pad(flat, (0, rows * 128 - flat.shape[0])).reshape(rows, 128)
    sg = Gatherer((["g"],), [place_slot(packed, j_arr)], {"g": ((N_DEV, rows, 128), 0, 1, 1)}, "sg")
    red.last = sg.start(after=tok)
    older = sum(grp["stage"] == 2 for grp in red.groups) - 1
    for k in range(older):
        red.finish_oldest()
        if k == 0:
            sg.forward(0, after=red.last)
    gsum_flat = sum_slots(sg.finish(0, after=red.last)["g"]).reshape(-1)
    loss = 0.5 * gsum_flat[flat.shape[0] - 1] / x.shape[-1]
    gs, off = [], 0
    for k in names:
        n = math.prod(small[k].shape)
        g = gsum_flat[off:off + n].reshape(small[k].shape)
        off += n
        if k in SHARDED_SMALL:
            g = lax.dynamic_slice_in_dim(g, j * 64, 64, axis=1)
        gs.append(g.reshape(A[k].shape))
    outs = small_adamw(gs, [(A[k], A["m_" + k], A["v_" + k]) for k in names])
    for a, k in enumerate(names):
        res[k] = (gs[a],) + tuple(outs[3 * a:3 * a + 3])
    red.last = outs[0]
    while red.finish_oldest():
        pass

    for arg in TRANSPOSED_ARGS:
        res[arg] = [jnp.swapaxes(t, 1, 2) for t in res[arg]]
    out = [loss, dx[None]]
    for part in range(4):
        out += [res[k][part] for k in WEIGHT_NAMES]
    return tuple(out)
```
